```python
import jax, jax.numpy as jnp
from jax import lax
import numpy as np

D_MODEL = 1024
BATCH = 8
SEQ = 4096
DEPTH = 4

CHUNK = 64
N_MIXERS = 2
N_ATTN_LAYERS = (DEPTH + 1) // 2
N_RNN_LAYERS = DEPTH // 2

ATTN_HEADS = 16
ATTN_HEAD_DIM = D_MODEL // ATTN_HEADS
ATTN_WIDTH = ATTN_HEADS * ATTN_HEAD_DIM
Q_BLOCK = 128

RNN_WIDTH = D_MODEL
RNN_BLOCK_WIDTH = 256
RNN_BLOCKS = RNN_WIDTH // RNN_BLOCK_WIDTH
CONV_WIDTH = 4
LRU_C = 8.0

DEEPNORM_ALPHA = (2.0 * DEPTH) ** 0.25
DEEPNORM_BETA = (8.0 * DEPTH) ** -0.25
LN_EPS = 1e-5

kernel_name = "fox_rglru_deepnorm_hybrid"


def layer_norm(x, g, b):
    xf = x.astype(jnp.float32)
    mu = jnp.mean(xf, axis=-1, keepdims=True)
    var = jnp.mean(jnp.square(xf - mu), axis=-1, keepdims=True)
    y = (xf - mu) * lax.rsqrt(var + LN_EPS) * g.astype(jnp.float32) + b.astype(jnp.float32)
    return y.astype(x.dtype)


def forgetting_attention(x, w_in, b_f, w_out):
    B, S, _ = x.shape
    H, Dh = ATTN_HEADS, ATTN_HEAD_DIM
    proj = jnp.einsum("bsd,de->bse", x, w_in)
    q, k, v, gate = jnp.split(proj[..., :4 * ATTN_WIDTH], 4, axis=-1)
    f_logit = proj[..., 4 * ATTN_WIDTH:].astype(jnp.float32) + b_f.astype(jnp.float32)
    cum = jnp.cumsum(jax.nn.log_sigmoid(f_logit), axis=1).transpose(0, 2, 1)

    def heads(t):
        return t.reshape(B, S, H, Dh).transpose(0, 2, 1, 3)

    q = heads(q) * (Dh ** -0.5)
    k = heads(k)
    v = heads(v)
    outs = []
    for blk in range(S // Q_BLOCK):
        q0 = blk * Q_BLOCK
        q1 = q0 + Q_BLOCK
        s = jnp.einsum("bhqd,bhkd->bhqk", q[:, :, q0:q1], k[:, :, :q1]).astype(jnp.float32)
        s = s + cum[:, :, q0:q1, None] - cum[:, :, None, :q1]
        causal = (q0 + jnp.arange(Q_BLOCK))[:, None] >= jnp.arange(q1)[None, :]
        p = jax.nn.softmax(jnp.where(causal, s, -jnp.inf), axis=-1)
        outs.append(jnp.einsum("bhqk,bhkd->bhqd", p.astype(v.dtype), v[:, :, :q1]))
    o = jnp.concatenate(outs, axis=2).transpose(0, 2, 1, 3).reshape(B, S, ATTN_WIDTH)
    return jnp.einsum("bse,ed->bsd", o * jax.nn.silu(gate), w_out)


def rglru_block(x, w_in, conv_w, conv_b, w_a, b_a, w_i, b_i, lam, w_out):
    B, S, _ = x.shape
    proj = jnp.einsum("bsd,de->bse", x, w_in)
    u, gate = jnp.split(proj, 2, axis=-1)
    u_pad = jnp.pad(u, ((0, 0), (CONV_WIDTH - 1, 0), (0, 0)))
    u = conv_b + sum(u_pad[:, tap:tap + S] * conv_w[tap] for tap in range(CONV_WIDTH))
    ub = u.reshape(B, S, RNN_BLOCKS, RNN_BLOCK_WIDTH)
    r = jax.nn.sigmoid(jnp.einsum("bsnc,ncd->bsnd", ub, w_a).reshape(B, S, RNN_WIDTH) + b_a)
    i = jax.nn.sigmoid(jnp.einsum("bsnc,ncd->bsnd", ub, w_i).reshape(B, S, RNN_WIDTH) + b_i)
    log_a = -LRU_C * r.astype(jnp.float32) * jax.nn.softplus(-lam.astype(jnp.float32))
    a = jnp.exp(log_a)
    bterm = jnp.sqrt(-jnp.expm1(2.0 * log_a)) * (i * u).astype(jnp.float32)

    def combine(left, right):
        a1, b1 = left
        a2, b2 = right
        return a1 * a2, a2 * b1 + b2

    _, h = lax.associative_scan(combine, (a, bterm), axis=1)
    y = h.astype(x.dtype) * jax.nn.silu(gate)
    return jnp.einsum("bse,ed->bsd", y, w_out)


def _fwd_setup_inputs(seed: int = 0) -> dict:
    key = jax.random.key(seed)
    ks = jax.random.split(key, 16)
    nA, nR = N_ATTN_LAYERS, N_RNN_LAYERS
    f32 = jnp.float32
    x = jax.random.normal(ks[0], (BATCH, SEQ, D_MODEL), f32)
    ln_g = 1.0 + 0.02 * jax.random.normal(ks[1], (DEPTH, D_MODEL), f32)
    ln_b = 0.02 * jax.random.normal(ks[2], (DEPTH, D_MODEL), f32)
    attn_w_in = jax.random.normal(ks[3], (nA, D_MODEL, 4 * ATTN_WIDTH + ATTN_HEADS), f32) * D_MODEL ** -0.5
    attn_b_f = jax.random.uniform(ks[4], (nA, ATTN_HEADS), f32, 1.0, 4.0)
    attn_w_out = jax.random.normal(ks[5], (nA, ATTN_WIDTH, D_MODEL), f32) * (ATTN_WIDTH ** -0.5 * DEEPNORM_BETA)
    rnn_w_in = jax.random.normal(ks[6], (nR, D_MODEL, 2 * RNN_WIDTH), f32) * D_MODEL ** -0.5
    rnn_conv_w = jax.random.normal(ks[7], (nR, CONV_WIDTH, RNN_WIDTH), f32) * CONV_WIDTH ** -0.5
    rnn_conv_b = 0.02 * jax.random.normal(ks[8], (nR, RNN_WIDTH), f32)
    rnn_w_a = jax.random.normal(ks[9], (nR, RNN_BLOCKS, RNN_BLOCK_WIDTH, RNN_BLOCK_WIDTH), f32) * RNN_BLOCK_WIDTH ** -0.5
    rnn_b_a = 0.02 * jax.random.normal(ks[10], (nR, RNN_WIDTH), f32)
    rnn_w_i = jax.random.normal(ks[11], (nR, RNN_BLOCKS, RNN_BLOCK_WIDTH, RNN_BLOCK_WIDTH), f32) * RNN_BLOCK_WIDTH ** -0.5
    rnn_b_i = 0.02 * jax.random.normal(ks[12], (nR, RNN_WIDTH), f32)
    a0 = jax.random.uniform(ks[13], (nR, RNN_WIDTH), f32, 0.9, 0.999)
    rnn_lambda = jnp.log(a0) - jnp.log1p(-a0)
    rnn_w_out = jax.random.normal(ks[14], (nR, RNN_WIDTH, D_MODEL), f32) * (RNN_WIDTH ** -0.5 * DEEPNORM_BETA)
    return {
        "x": x, "ln_g": ln_g, "ln_b": ln_b,
        "attn_w_in": attn_w_in, "attn_b_f": attn_b_f, "attn_w_out": attn_w_out,
        "rnn_w_in": rnn_w_in, "rnn_conv_w": rnn_conv_w, "rnn_conv_b": rnn_conv_b,
        "rnn_w_a": rnn_w_a, "rnn_b_a": rnn_b_a, "rnn_w_i": rnn_w_i, "rnn_b_i": rnn_b_i,
        "rnn_lambda": rnn_lambda, "rnn_w_out": rnn_w_out,
    }


def _fwd_reference(x, ln_g, ln_b, attn_w_in, attn_b_f, attn_w_out, rnn_w_in, rnn_conv_w,
              rnn_conv_b, rnn_w_a, rnn_b_a, rnn_w_i, rnn_b_i, rnn_lambda, rnn_w_out):
    for layer in range(DEPTH):
        idx = layer // N_MIXERS
        if layer % N_MIXERS == 0:
            h = forgetting_attention(x, attn_w_in[idx], attn_b_f[idx], attn_w_out[idx])
        else:
            h = rglru_block(x, rnn_w_in[idx], rnn_conv_w[idx], rnn_conv_b[idx],
                            rnn_w_a[idx], rnn_b_a[idx], rnn_w_i[idx], rnn_b_i[idx],
                            rnn_lambda[idx], rnn_w_out[idx])
        x = layer_norm(DEEPNORM_ALPHA * x + h, ln_g[layer], ln_b[layer])
    return x


import jax as _jax
import jax.numpy as _jnp

TWIN_FORMAT = 'train_step'
FWD_PARAMS = ['x', 'ln_g', 'ln_b', 'attn_w_in', 'attn_b_f', 'attn_w_out', 'rnn_w_in', 'rnn_conv_w', 'rnn_conv_b', 'rnn_w_a', 'rnn_b_a', 'rnn_w_i', 'rnn_b_i', 'rnn_lambda', 'rnn_w_out']
TWIN_WEIGHTS = ['ln_g', 'ln_b', 'attn_w_in', 'attn_b_f', 'attn_w_out', 'rnn_w_in', 'rnn_conv_w', 'rnn_conv_b', 'rnn_w_a', 'rnn_b_a', 'rnn_w_i', 'rnn_b_i', 'rnn_lambda', 'rnn_w_out']
TWIN_DIFF_INPUT = 'x'
TWIN_INPUTS = ['x', 'ln_g', 'ln_b', 'attn_w_in', 'attn_b_f', 'attn_w_out', 'rnn_w_in', 'rnn_conv_w', 'rnn_conv_b', 'rnn_w_a', 'rnn_b_a', 'rnn_w_i', 'rnn_b_i', 'rnn_lambda', 'rnn_w_out', 'loss_target', 'm_ln_g', 'm_ln_b', 'm_attn_w_in', 'm_attn_b_f', 'm_attn_w_out', 'm_rnn_w_in', 'm_rnn_conv_w', 'm_rnn_conv_b', 'm_rnn_w_a', 'm_rnn_b_a', 'm_rnn_w_i', 'm_rnn_b_i', 'm_rnn_lambda', 'm_rnn_w_out', 'v_ln_g', 'v_ln_b', 'v_attn_w_in', 'v_attn_b_f', 'v_attn_w_out', 'v_rnn_w_in', 'v_rnn_conv_w', 'v_rnn_conv_b', 'v_rnn_w_a', 'v_rnn_b_a', 'v_rnn_w_i', 'v_rnn_b_i', 'v_rnn_lambda', 'v_rnn_w_out']
TWIN_OUTPUTS = ['loss', 'grad_x', 'grad_ln_g', 'grad_ln_b', 'grad_attn_w_in', 'grad_attn_b_f', 'grad_attn_w_out', 'grad_rnn_w_in', 'grad_rnn_conv_w', 'grad_rnn_conv_b', 'grad_rnn_w_a', 'grad_rnn_b_a', 'grad_rnn_w_i', 'grad_rnn_b_i', 'grad_rnn_lambda', 'grad_rnn_w_out', 'delta_ln_g', 'delta_ln_b', 'delta_attn_w_in', 'delta_attn_b_f', 'delta_attn_w_out', 'delta_rnn_w_in', 'delta_rnn_conv_w', 'delta_rnn_conv_b', 'delta_rnn_w_a', 'delta_rnn_b_a', 'delta_rnn_w_i', 'delta_rnn_b_i', 'delta_rnn_lambda', 'delta_rnn_w_out', 'new_m_ln_g', 'new_m_ln_b', 'new_m_attn_w_in', 'new_m_attn_b_f', 'new_m_attn_w_out', 'new_m_rnn_w_in', 'new_m_rnn_conv_w', 'new_m_rnn_conv_b', 'new_m_rnn_w_a', 'new_m_rnn_b_a', 'new_m_rnn_w_i', 'new_m_rnn_b_i', 'new_m_rnn_lambda', 'new_m_rnn_w_out', 'new_v_ln_g', 'new_v_ln_b', 'new_v_attn_w_in', 'new_v_attn_b_f', 'new_v_attn_w_out', 'new_v_rnn_w_in', 'new_v_rnn_conv_w', 'new_v_rnn_conv_b', 'new_v_rnn_w_a', 'new_v_rnn_b_a', 'new_v_rnn_w_i', 'new_v_rnn_b_i', 'new_v_rnn_lambda', 'new_v_rnn_w_out']
TWIN_LEAF_KINDS = {'loss': 'loss', 'grad_x': 'grad_x', 'grad_ln_g': 'grad_w', 'grad_ln_b': 'grad_w', 'grad_attn_w_in': 'grad_w', 'grad_attn_b_f': 'grad_w', 'grad_attn_w_out': 'grad_w', 'grad_rnn_w_in': 'grad_w', 'grad_rnn_conv_w': 'grad_w', 'grad_rnn_conv_b': 'grad_w', 'grad_rnn_w_a': 'grad_w', 'grad_rnn_b_a': 'grad_w', 'grad_rnn_w_i': 'grad_w', 'grad_rnn_b_i': 'grad_w', 'grad_rnn_lambda': 'grad_w', 'grad_rnn_w_out': 'grad_w', 'delta_ln_g': 'delta_w', 'delta_ln_b': 'delta_w', 'delta_attn_w_in': 'delta_w', 'delta_attn_b_f': 'delta_w', 'delta_attn_w_out': 'delta_w', 'delta_rnn_w_in': 'delta_w', 'delta_rnn_conv_w': 'delta_w', 'delta_rnn_conv_b': 'delta_w', 'delta_rnn_w_a': 'delta_w', 'delta_rnn_b_a': 'delta_w', 'delta_rnn_w_i': 'delta_w', 'delta_rnn_b_i': 'delta_w', 'delta_rnn_lambda': 'delta_w', 'delta_rnn_w_out': 'delta_w', 'new_m_ln_g': 'new_m', 'new_m_ln_b': 'new_m', 'new_m_attn_w_in': 'new_m', 'new_m_attn_b_f': 'new_m', 'new_m_attn_w_out': 'new_m', 'new_m_rnn_w_in': 'new_m', 'new_m_rnn_conv_w': 'new_m', 'new_m_rnn_conv_b': 'new_m', 'new_m_rnn_w_a': 'new_m', 'new_m_rnn_b_a': 'new_m', 'new_m_rnn_w_i': 'new_m', 'new_m_rnn_b_i': 'new_m', 'new_m_rnn_lambda': 'new_m', 'new_m_rnn_w_out': 'new_m', 'new_v_ln_g': 'new_v', 'new_v_ln_b': 'new_v', 'new_v_attn_w_in': 'new_v', 'new_v_attn_b_f': 'new_v', 'new_v_attn_w_out': 'new_v', 'new_v_rnn_w_in': 'new_v', 'new_v_rnn_conv_w': 'new_v', 'new_v_rnn_conv_b': 'new_v', 'new_v_rnn_w_a': 'new_v', 'new_v_rnn_b_a': 'new_v', 'new_v_rnn_w_i': 'new_v', 'new_v_rnn_b_i': 'new_v', 'new_v_rnn_lambda': 'new_v', 'new_v_rnn_w_out': 'new_v'}


def _forward(args):
    return _fwd_reference(*[args[k] for k in FWD_PARAMS])


def _output_shape():
    out = _jax.eval_shape(lambda: _forward(_fwd_setup_inputs(0)))
    return out.shape, out.dtype

N_MICROBATCH = 1
ADAM_LR = 0.001
ADAM_B1 = 0.9
ADAM_B2 = 0.999
ADAM_EPS = 1e-08
ADAM_WD = 0.01
ADAM_STEP = 10
PER_EXAMPLE_BATCH_AXIS = {'x': 0, 'loss_target': 0}
SHARED_INPUTS = []
_WEIGHT_DTYPES = {'ln_g': _jnp.float32, 'ln_b': _jnp.float32, 'attn_w_in': _jnp.float32, 'attn_b_f': _jnp.float32, 'attn_w_out': _jnp.float32, 'rnn_w_in': _jnp.float32, 'rnn_conv_w': _jnp.float32, 'rnn_conv_b': _jnp.float32, 'rnn_w_a': _jnp.float32, 'rnn_b_a': _jnp.float32, 'rnn_w_i': _jnp.float32, 'rnn_b_i': _jnp.float32, 'rnn_lambda': _jnp.float32, 'rnn_w_out': _jnp.float32}
MOMENT_SCALE = {'ln_g': 1.603828e+01, 'ln_b': 6.445010e-01, 'attn_w_in': 9.328257e-03, 'attn_b_f': 5.851789e-02, 'attn_w_out': 2.474930e-02, 'rnn_w_in': 1.568997e-02, 'rnn_conv_w': 1.603137e-02, 'rnn_conv_b': 8.903256e-02, 'rnn_w_a': 3.647904e-03, 'rnn_b_a': 3.639656e-03, 'rnn_w_i': 6.228375e-03, 'rnn_b_i': 5.652416e-03, 'rnn_lambda': 7.557964e-03, 'rnn_w_out': 3.551603e-02}


def _to_microbatches(a, axis):
    t = _jnp.moveaxis(a, axis, 0)
    t = t.reshape((N_MICROBATCH, t.shape[0] // N_MICROBATCH) + t.shape[1:])
    return _jnp.moveaxis(t, 1, axis + 1)


def setup_inputs(seed: int = 0) -> dict:
    inp = _fwd_setup_inputs(seed)
    key = _jax.random.fold_in(_jax.random.key(seed), 7919)
    shape, _ = _output_shape()
    out = dict(inp)
    out["loss_target"] = _jax.random.normal(_jax.random.fold_in(key, 0), shape, _jnp.float32)
    for i, name in enumerate(TWIN_WEIGHTS):
        w = inp[name].astype(_jnp.float32)
        if MOMENT_SCALE is None:
            s = _jnp.sqrt(_jnp.mean(_jnp.square(w)) + 1e-30)
        else:
            s = MOMENT_SCALE[name]
        km, kv = _jax.random.split(_jax.random.fold_in(key, i + 1))
        out[name] = w
        out["m_" + name] = s * _jax.random.normal(km, w.shape, _jnp.float32)
        out["v_" + name] = (s * s) * _jax.random.uniform(kv, w.shape, _jnp.float32, 0.5, 1.5)
    if N_MICROBATCH > 1:
        for name, axis in PER_EXAMPLE_BATCH_AXIS.items():
            out[name] = _to_microbatches(out[name], axis)
    return {'x': out['x'], 'ln_g': out['ln_g'], 'ln_b': out['ln_b'], 'attn_w_in': out['attn_w_in'], 'attn_b_f': out['attn_b_f'], 'attn_w_out': out['attn_w_out'], 'rnn_w_in': out['rnn_w_in'], 'rnn_conv_w': out['rnn_conv_w'], 'rnn_conv_b': out['rnn_conv_b'], 'rnn_w_a': out['rnn_w_a'], 'rnn_b_a': out['rnn_b_a'], 'rnn_w_i': out['rnn_w_i'], 'rnn_b_i': out['rnn_b_i'], 'rnn_lambda': out['rnn_lambda'], 'rnn_w_out': out['rnn_w_out'], 'loss_target': out['loss_target'], 'm_ln_g': out['m_ln_g'], 'm_ln_b': out['m_ln_b'], 'm_attn_w_in': out['m_attn_w_in'], 'm_attn_b_f': out['m_attn_b_f'], 'm_attn_w_out': out['m_attn_w_out'], 'm_rnn_w_in': out['m_rnn_w_in'], 'm_rnn_conv_w': out['m_rnn_conv_w'], 'm_rnn_conv_b': out['m_rnn_conv_b'], 'm_rnn_w_a': out['m_rnn_w_a'], 'm_rnn_b_a': out['m_rnn_b_a'], 'm_rnn_w_i': out['m_rnn_w_i'], 'm_rnn_b_i': out['m_rnn_b_i'], 'm_rnn_lambda': out['m_rnn_lambda'], 'm_rnn_w_out': out['m_rnn_w_out'], 'v_ln_g': out['v_ln_g'], 'v_ln_b': out['v_ln_b'], 'v_attn_w_in': out['v_attn_w_in'], 'v_attn_b_f': out['v_attn_b_f'], 'v_attn_w_out': out['v_attn_w_out'], 'v_rnn_w_in': out['v_rnn_w_in'], 'v_rnn_conv_w': out['v_rnn_conv_w'], 'v_rnn_conv_b': out['v_rnn_conv_b'], 'v_rnn_w_a': out['v_rnn_w_a'], 'v_rnn_b_a': out['v_rnn_b_a'], 'v_rnn_w_i': out['v_rnn_w_i'], 'v_rnn_b_i': out['v_rnn_b_i'], 'v_rnn_lambda': out['v_rnn_lambda'], 'v_rnn_w_out': out['v_rnn_w_out']}


def _loss(weights, diff, rest, loss_target):
    with _jax.named_scope("forward"):
        args = {**rest, TWIN_DIFF_INPUT: diff, **{k: w.astype(_WEIGHT_DTYPES[k]) for k, w in weights.items()}}
        y = _forward(args)
    with _jax.named_scope("loss_head"):
        err = _jnp.square(y.astype(_jnp.float32) - loss_target)
        return 0.5 * _jnp.sum(_jnp.mean(err, axis=-1)) if err.ndim else 0.5 * err


def _adamw(w, g, m, v):
    m = ADAM_B1 * m + (1.0 - ADAM_B1) * g
    v = ADAM_B2 * v + (1.0 - ADAM_B2) * _jnp.square(g)
    m_hat = m / (1.0 - ADAM_B1 ** ADAM_STEP)
    v_hat = v / (1.0 - ADAM_B2 ** ADAM_STEP)
    delta = -ADAM_LR * (m_hat / (_jnp.sqrt(v_hat) + ADAM_EPS) + ADAM_WD * w)
    return delta, m, v


def reference(x, ln_g, ln_b, attn_w_in, attn_b_f, attn_w_out, rnn_w_in, rnn_conv_w, rnn_conv_b, rnn_w_a, rnn_b_a, rnn_w_i, rnn_b_i, rnn_lambda, rnn_w_out, loss_target, m_ln_g, m_ln_b, m_attn_w_in, m_attn_b_f, m_attn_w_out, m_rnn_w_in, m_rnn_conv_w, m_rnn_conv_b, m_rnn_w_a, m_rnn_b_a, m_rnn_w_i, m_rnn_b_i, m_rnn_lambda, m_rnn_w_out, v_ln_g, v_ln_b, v_attn_w_in, v_attn_b_f, v_attn_w_out, v_rnn_w_in, v_rnn_conv_w, v_rnn_conv_b, v_rnn_w_a, v_rnn_b_a, v_rnn_w_i, v_rnn_b_i, v_rnn_lambda, v_rnn_w_out):
    given = dict(x=x, ln_g=ln_g, ln_b=ln_b, attn_w_in=attn_w_in, attn_b_f=attn_b_f, attn_w_out=attn_w_out, rnn_w_in=rnn_w_in, rnn_conv_w=rnn_conv_w, rnn_conv_b=rnn_conv_b, rnn_w_a=rnn_w_a, rnn_b_a=rnn_b_a, rnn_w_i=rnn_w_i, rnn_b_i=rnn_b_i, rnn_lambda=rnn_lambda, rnn_w_out=rnn_w_out, loss_target=loss_target, m_ln_g=m_ln_g, m_ln_b=m_ln_b, m_attn_w_in=m_attn_w_in, m_attn_b_f=m_attn_b_f, m_attn_w_out=m_attn_w_out, m_rnn_w_in=m_rnn_w_in, m_rnn_conv_w=m_rnn_conv_w, m_rnn_conv_b=m_rnn_conv_b, m_rnn_w_a=m_rnn_w_a, m_rnn_b_a=m_rnn_b_a, m_rnn_w_i=m_rnn_w_i, m_rnn_b_i=m_rnn_b_i, m_rnn_lambda=m_rnn_lambda, m_rnn_w_out=m_rnn_w_out, v_ln_g=v_ln_g, v_ln_b=v_ln_b, v_attn_w_in=v_attn_w_in, v_attn_b_f=v_attn_b_f, v_attn_w_out=v_attn_w_out, v_rnn_w_in=v_rnn_w_in, v_rnn_conv_w=v_rnn_conv_w, v_rnn_conv_b=v_rnn_conv_b, v_rnn_w_a=v_rnn_w_a, v_rnn_b_a=v_rnn_b_a, v_rnn_w_i=v_rnn_w_i, v_rnn_b_i=v_rnn_b_i, v_rnn_lambda=v_rnn_lambda, v_rnn_w_out=v_rnn_w_out)
    weights = {n: given[n] for n in TWIN_WEIGHTS}
    shared = {n: given[n] for n in SHARED_INPUTS}
    per_example = {n: given[n] for n in ['x']}
    grad_fn = _jax.value_and_grad(_loss, argnums=(0, 1))

    def one_microbatch(ex, loss_target):
        ex = dict(ex)
        diff = ex.pop(TWIN_DIFF_INPUT)
        return grad_fn(weights, diff, {**shared, **ex}, loss_target)

    if N_MICROBATCH == 1:
        loss, (grad_w, grad_x) = one_microbatch(per_example, given["loss_target"])
    else:
        def body(carry, xs):
            loss_sum, grad_sum = carry
            l_k, (gw_k, gx_k) = one_microbatch(xs[0], xs[1])
            with _jax.named_scope("update"):
                return (loss_sum + l_k, _jax.tree.map(_jnp.add, grad_sum, gw_k)), gx_k

        init = (_jnp.zeros((), _jnp.float32), _jax.tree.map(_jnp.zeros_like, weights))
        (loss, grad_w), grad_x = _jax.lax.scan(body, init, (per_example, given["loss_target"]))
    with _jax.named_scope("update"):
        delta_w, new_m, new_v = {}, {}, {}
        for n in TWIN_WEIGHTS:
            delta_w[n], new_m[n], new_v[n] = _adamw(weights[n], grad_w[n], given["m_" + n], given["v_" + n])
    return (loss, grad_x, *[grad_w[n] for n in TWIN_WEIGHTS], *[delta_w[n] for n in TWIN_WEIGHTS],
            *[new_m[n] for n in TWIN_WEIGHTS], *[new_v[n] for n in TWIN_WEIGHTS])
```

```python
import functools

import jax
import jax.numpy as jnp
from jax import lax
from jax.experimental import pallas as pl
from jax.experimental.pallas import tpu as pltpu

F32 = jnp.float32
BF16 = jnp.bfloat16
MESH = pl.DeviceIdType.MESH

DEPTH = 4
HEAD_DIM = 64
HEAD_PAIR = 2 * HEAD_DIM
RNN_BLOCK = 256
CONV_WIDTH = 4
LRU_C = 8.0
ALPHA = (2.0 * DEPTH) ** 0.25
LN_EPS = 1e-5
ADAM_LR, ADAM_B1, ADAM_B2, ADAM_EPS, ADAM_WD, ADAM_STEP = 0.001, 0.9, 0.999, 1e-08, 0.01, 10
N_CHIPS = 4
LANES = 1024
ROW_ALIGN = 256
NEG = -1e30

NT_DIMS = (((1,), (1,)), ((), ()))
TN_DIMS = (((0,), (0,)), ((), ()))


def _sigmoid(z):
    return 1.0 / (1.0 + jnp.exp(-z))


def _softplus(z):
    return jnp.maximum(z, 0.0) + jnp.log(1.0 + jnp.exp(-jnp.abs(z)))


def _neg_expm1(y):
    poly = y * (1.0 + y * (0.5 + y * (1.0 / 6.0 + y * (1.0 / 24.0))))
    return -jnp.where(y > -0.05, poly, jnp.exp(y) - 1.0)


def _shift_down(cur, prev8, k):
    n = cur.shape[0]
    row = lax.broadcasted_iota(jnp.int32, cur.shape, 0)
    fix = jnp.tile(pltpu.roll(prev8, k, 0), (n // 8, 1))
    return jnp.where(row < k, fix, pltpu.roll(cur, k, 0))


def _shift_up(cur, next8, k):
    n = cur.shape[0]
    row = lax.broadcasted_iota(jnp.int32, cur.shape, 0)
    fix = jnp.tile(pltpu.roll(next8, 8 - k, 0), (n // 8, 1))
    return jnp.where(row >= n - k, fix, pltpu.roll(cur, n - k, 0))


def _proj(x, w, outs, name):
    s_len, d = x.shape
    tm = min(512, s_len)

    def body(x_ref, w_ref, *o_refs):
        xb = x_ref[...].astype(BF16)
        for (c0, wd, dt), o_ref in zip(outs, o_refs):
            step = min(wd, 512)
            for cc in range(0, wd, step):
                o_ref[:, cc:cc + step] = jnp.dot(
                    xb, w_ref[:, c0 + cc:c0 + cc + step], preferred_element_type=F32).astype(dt)

    return pl.pallas_call(
        body, name=name, grid=(s_len // tm,),
        in_specs=[pl.BlockSpec((tm, d), lambda i: (i, 0)), pl.BlockSpec(w.shape, lambda i: (0, 0))],
        out_specs=[pl.BlockSpec((tm, wd), lambda i: (i, 0)) for _, wd, _ in outs],
        out_shape=[jax.ShapeDtypeStruct((s_len, wd), dt) for _, wd, dt in outs],
        compiler_params=pltpu.CompilerParams(dimension_semantics=("parallel",)),
    )(x, w)


def _mm_nt(dz, pieces, w, name):
    s_len, d = dz.shape
    tm = min(256, s_len)
    n = len(pieces)
    cols = [(c0, p.shape[1]) for p, c0 in pieces]

    def body(dz_ref, *rest):
        w_ref, o_ref = rest[n], rest[n + 1]
        acc = ALPHA * dz_ref[...]
        for (c0, wd), p_ref in zip(cols, rest[:n]):
            acc = acc + lax.dot_general(p_ref[...], w_ref[:, c0:c0 + wd], NT_DIMS, preferred_element_type=F32)
        o_ref[...] = acc

    return pl.pallas_call(
        body, name=name, grid=(s_len // tm,),
        in_specs=[pl.BlockSpec((tm, d), lambda i: (i, 0))]
        + [pl.BlockSpec((tm, wd), lambda i: (i, 0)) for _, wd in cols]
        + [pl.BlockSpec(w.shape, lambda i: (0, 0))],
        out_specs=pl.BlockSpec((tm, d), lambda i: (i, 0)),
        out_shape=jax.ShapeDtypeStruct((s_len, d), F32),
        compiler_params=pltpu.CompilerParams(dimension_semantics=("parallel",)),
    )(dz, *[p for p, _ in pieces], w)


def _mm_tn(a, b, name):
    s_len, m = a.shape
    n = b.shape[1]
    tn = min(n, 512)
    ts = min(s_len, 512)

    def body(a_ref, b_ref, o_ref):
        @pl.when(pl.program_id(1) == 0)
        def _():
            o_ref[...] = jnp.zeros_like(o_ref)

        o_ref[...] += lax.dot_general(a_ref[...].astype(BF16), b_ref[...].astype(BF16), TN_DIMS,
                                      preferred_element_type=F32)

    return pl.pallas_call(
        body, name=name, grid=(n // tn, s_len // ts),
        in_specs=[pl.BlockSpec((ts, m), lambda j, s: (s, 0)), pl.BlockSpec((ts, tn), lambda j, s: (s, j))],
        out_specs=pl.BlockSpec((m, tn), lambda j, s: (0, j)),
        out_shape=jax.ShapeDtypeStruct((m, n), F32),
        compiler_params=pltpu.CompilerParams(dimension_semantics=("parallel", "arbitrary")),
    )(a, b)


def _fcum(fl, bias):
    s_len = fl.shape[0]

    def body(fl_ref, b_ref, cum_ref, sg_ref):
        z = fl_ref[...] + b_ref[...]
        e = jnp.exp(-jnp.abs(z))
        logf = jnp.minimum(z, 0.0) - jnp.log(1.0 + e)
        sneg = jnp.where(z >= 0, e, 1.0) / (1.0 + e)
        run = logf.T[0:16, :]
        sg_ref[...] = sneg.T[0:16, :]
        lane = lax.broadcasted_iota(jnp.int32, (16, s_len), 1)
        sh = 1
        while sh < s_len:
            run = run + jnp.where(lane >= sh, pltpu.roll(run, sh, 1), 0.0)
            sh *= 2
        cum_ref[...] = run

    return pl.pallas_call(
        body, name="fcum",
        out_shape=[jax.ShapeDtypeStruct((16, s_len), F32), jax.ShapeDtypeStruct((16, s_len), F32)],
    )(fl, bias)


def _fbwd(dcum, sneg):
    s_len = dcum.shape[1]

    def body(dc_ref, sg_ref, dl_ref, db_ref):
        run = dc_ref[...]
        lane = lax.broadcasted_iota(jnp.int32, (16, s_len), 1)
        sh = 1
        while sh < s_len:
            run = run + jnp.where(lane < s_len - sh, pltpu.roll(run, s_len - sh, 1), 0.0)
            sh *= 2
        dlog = run * sg_ref[...]
        db_ref[...] = jnp.broadcast_to(jnp.sum(dlog, axis=1, keepdims=True), (16, 128))
        full = jnp.concatenate([dlog, jnp.zeros((112, s_len), F32)], axis=0)
        dl_ref[...] = full.T.astype(BF16)

    return pl.pallas_call(
        body, name="fbwd",
        out_shape=[jax.ShapeDtypeStruct((s_len, 128), BF16), jax.ShapeDtypeStruct((16, 128), F32)],
    )(dcum, sneg)


def _att_tile(s_len):
    return 256 if s_len >= 512 else s_len // 2


def _flash_fwd(q, k, v, cumr, tile):
    s_len, width = q.shape
    nt = s_len // tile

    def body(q_ref, k_ref, v_ref, cum_ref, o_ref, lse_ref):
        pid = pl.program_id(0)
        is0 = lax.broadcasted_iota(jnp.int32, (tile, HEAD_PAIR), 1) < HEAD_DIM
        causal = (lax.broadcasted_iota(jnp.int32, (tile, tile), 0)
                  >= lax.broadcasted_iota(jnp.int32, (tile, tile), 1))

        def q_loop(qi, carry):
            q0 = pl.multiple_of(qi * tile, tile)
            qt = q_ref[pl.ds(q0, tile), :]
            zq = jnp.zeros_like(qt)
            res = []
            for h in (0, 1):
                qm = jnp.where(is0, qt, zq) if h == 0 else jnp.where(is0, zq, qt)
                hrow = (2 * pid + h) * nt

                def step(kj, m, l, acc, masked, qm=qm, hrow=hrow):
                    k0 = pl.multiple_of(kj * tile, tile)
                    s = lax.dot_general(qm, k_ref[pl.ds(k0, tile), :], NT_DIMS, preferred_element_type=F32)
                    s = s - cum_ref[pl.ds(hrow + kj, 1), :]
                    if masked:
                        s = jnp.where(causal, s, NEG)
                    m_new = jnp.maximum(m, jnp.max(s, axis=1, keepdims=True))
                    p = jnp.exp(s - m_new)
                    scale = jnp.exp(m - m_new)
                    l = scale * l + jnp.sum(p, axis=1, keepdims=True)
                    acc = scale * acc + jnp.dot(p.astype(BF16), v_ref[pl.ds(k0, tile), :],
                                                preferred_element_type=F32)
                    return m_new, l, acc

                init = (jnp.full((tile, 1), NEG, F32), jnp.zeros((tile, 1), F32), jnp.zeros((tile, HEAD_PAIR), F32))
                state = step(qi, *init, True)
                m, l, acc = lax.fori_loop(0, qi, lambda kj, c, step=step: step(kj, *c, False), state)
                res.append((acc / l, m + jnp.log(l)))
            o_ref[pl.ds(q0, tile), :] = jnp.where(is0, res[0][0], res[1][0])
            lse_ref[pl.ds(q0, tile), :] = jnp.where(is0, res[0][1], res[1][1])
            return carry

        lax.fori_loop(0, nt, q_loop, 0)

    blk = pl.BlockSpec((s_len, HEAD_PAIR), lambda p: (0, p))
    return pl.pallas_call(
        body, name="flash_fwd", grid=(width // HEAD_PAIR,),
        in_specs=[blk, blk, blk, pl.BlockSpec(cumr.shape, lambda p: (0, 0))],
        out_specs=[blk, blk],
        out_shape=[jax.ShapeDtypeStruct((s_len, width), F32), jax.ShapeDtypeStruct((s_len, width), F32)],
        compiler_params=pltpu.CompilerParams(dimension_semantics=("parallel",)),
    )(q, k, v, cumr)


def _flash_bwd(q, k, v, o, do, lse, cumr, tile):
    s_len, width = q.shape
    nt = s_len // tile

    def body(q_ref, k_ref, v_ref, o_ref, do_ref, lse_ref, cum_ref, dq_ref, dk_ref, dv_ref, dcum_ref,
             dq_acc, delta, do_bf, row_sum):
        pid = pl.program_id(0)
        is0 = lax.broadcasted_iota(jnp.int32, (tile, HEAD_PAIR), 1) < HEAD_DIM
        causal = (lax.broadcasted_iota(jnp.int32, (tile, tile), 0)
                  >= lax.broadcasted_iota(jnp.int32, (tile, tile), 1))

        def pre(i, carry):
            r0 = pl.multiple_of(i * tile, tile)
            d_o = do_ref[pl.ds(r0, tile), :]
            prod = d_o * o_ref[pl.ds(r0, tile), :]
            d0 = jnp.sum(jnp.where(is0, prod, 0.0), axis=1, keepdims=True)
            d1 = jnp.sum(jnp.where(is0, 0.0, prod), axis=1, keepdims=True)
            delta[pl.ds(r0, tile), :] = jnp.where(is0, d0, d1)
            do_bf[pl.ds(r0, tile), :] = d_o.astype(BF16)
            dq_acc[pl.ds(r0, tile), :] = jnp.zeros((tile, HEAD_PAIR), F32)
            row_sum[pl.ds(r0, tile), :] = jnp.zeros((tile, HEAD_PAIR), F32)
            return carry

        lax.fori_loop(0, nt, pre, 0)

        def kv_loop(kj, carry):
            k0 = pl.multiple_of(kj * tile, tile)
            kt = k_ref[pl.ds(k0, tile), :]
            vt = v_ref[pl.ds(k0, tile), :]
            zb = jnp.zeros_like(kt)
            kms = (jnp.where(is0, kt, zb), jnp.where(is0, zb, kt))
            vms = (jnp.where(is0, vt, zb), jnp.where(is0, zb, vt))
            cums = [cum_ref[pl.ds((2 * pid + h) * nt + kj, 1), :] for h in (0, 1)]

            def step(qi, state, masked):
                dk_a, dv_a, dc0, dc1 = state
                q0 = pl.multiple_of(qi * tile, tile)
                qt = q_ref[pl.ds(q0, tile), :]
                dot = do_bf[pl.ds(q0, tile), :]
                lse_t = lse_ref[pl.ds(q0, tile), :]
                dl_t = delta[pl.ds(q0, tile), :]
                dq_t = jnp.zeros((tile, HEAD_PAIR), F32)
                dcs = [dc0, dc1]
                rsum = []
                for h in (0, 1):
                    c0 = h * HEAD_DIM
                    s = lax.dot_general(qt, kms[h], NT_DIMS, preferred_element_type=F32) - cums[h]
                    if masked:
                        s = jnp.where(causal, s, NEG)
                    p = jnp.exp(s - lse_t[:, c0:c0 + 1])
                    dp = lax.dot_general(dot, vms[h], NT_DIMS, preferred_element_type=F32)
                    ds = p * (dp - dl_t[:, c0:c0 + 1])
                    dcs[h] = dcs[h] + jnp.sum(ds, axis=0, keepdims=True)
                    rsum.append(jnp.sum(ds, axis=1, keepdims=True))
                    pb = p.astype(BF16)
                    dsb = ds.astype(BF16)
                    do_h = jnp.where(is0, dot, zb) if h == 0 else jnp.where(is0, zb, dot)
                    q_h = jnp.where(is0, qt, zb) if h == 0 else jnp.where(is0, zb, qt)
                    dv_a = dv_a + lax.dot_general(pb, do_h, TN_DIMS, preferred_element_type=F32)
                    dk_a = dk_a + lax.dot_general(dsb, q_h, TN_DIMS, preferred_element_type=F32)
                    dq_t = dq_t + jnp.dot(dsb, kms[h], preferred_element_type=F32)
                dq_acc[pl.ds(q0, tile), :] += dq_t
                row_sum[pl.ds(q0, tile), :] += jnp.where(is0, rsum[0], rsum[1])
                return dk_a, dv_a, dcs[0], dcs[1]

            zero = jnp.zeros((tile, HEAD_PAIR), F32)
            zrow = jnp.zeros((1, tile), F32)
            state = step(kj, (zero, zero, zrow, zrow), True)
            dk_a, dv_a, dc0, dc1 = lax.fori_loop(kj + 1, nt, lambda qi, c: step(qi, c, False), state)
            dk_ref[pl.ds(k0, tile), :] = dk_a.astype(BF16)
            dv_ref[pl.ds(k0, tile), :] = dv_a.astype(BF16)
            dcum_ref[pl.ds((2 * pid) * nt + kj, 1), :] = -dc0
            dcum_ref[pl.ds((2 * pid + 1) * nt + kj, 1), :] = -dc1
            return carry

        lax.fori_loop(0, nt, kv_loop, 0)

        def fin(i, carry):
            r0 = pl.multiple_of(i * tile, tile)
            dq_ref[pl.ds(r0, tile), :] = dq_acc[pl.ds(r0, tile), :].astype(BF16)
            return carry

        lax.fori_loop(0, nt, fin, 0)

        by_time = row_sum[...].T
        for h in (0, 1):
            for j in range(nt):
                dcum_ref[pl.ds((2 * pid + h) * nt + j, 1), :] += by_time[h * HEAD_DIM:h * HEAD_DIM + 1,
                                                                          j * tile:(j + 1) * tile]

    blk = pl.BlockSpec((s_len, HEAD_PAIR), lambda p: (0, p))
    full = pl.BlockSpec(cumr.shape, lambda p: (0, 0))
    return pl.pallas_call(
        body, name="flash_bwd", grid=(width // HEAD_PAIR,),
        in_specs=[blk, blk, blk, blk, blk, blk, full],
        out_specs=[blk, blk, blk, full],
        out_shape=[jax.ShapeDtypeStruct((s_len, width), BF16)] * 3 + [jax.ShapeDtypeStruct(cumr.shape, F32)],
        scratch_shapes=[pltpu.VMEM((s_len, HEAD_PAIR), F32), pltpu.VMEM((s_len, HEAD_PAIR), F32),
                        pltpu.VMEM((s_len, HEAD_PAIR), BF16), pltpu.VMEM((s_len, HEAD_PAIR), F32)],
        compiler_params=pltpu.CompilerParams(dimension_semantics=("arbitrary",)),
    )(q, k, v, o, do, lse, cumr)


def _out_ln(a, gate, x, w, g, b, name):
    s_len, d = x.shape
    tm = min(256, s_len)

    def body(a_ref, gate_ref, x_ref, w_ref, g_ref, b_ref, xn_ref, xh_ref, rs_ref, yg_ref):
        gt = gate_ref[...]
        yg = (a_ref[...] * (gt * _sigmoid(gt))).astype(BF16)
        yg_ref[...] = yg
        z = ALPHA * x_ref[...] + jnp.dot(yg, w_ref[...], preferred_element_type=F32)
        zc = z - jnp.mean(z, axis=1, keepdims=True)
        rstd = lax.rsqrt(jnp.mean(zc * zc, axis=1, keepdims=True) + LN_EPS)
        xh = zc * rstd
        xh_ref[...] = xh
        xn_ref[...] = xh * g_ref[...] + b_ref[...]
        rs_ref[...] = jnp.broadcast_to(rstd, (tm, 128))

    row = pl.BlockSpec((tm, d), lambda i: (i, 0))
    vec = pl.BlockSpec((1, d), lambda i: (0, 0))
    return pl.pallas_call(
        body, name=name, grid=(s_len // tm,),
        in_specs=[row, row, row, pl.BlockSpec(w.shape, lambda i: (0, 0)), vec, vec],
        out_specs=[row, row, pl.BlockSpec((tm, 128), lambda i: (i, 0)), row],
        out_shape=[jax.ShapeDtypeStruct((s_len, d), F32), jax.ShapeDtypeStruct((s_len, d), F32),
                   jax.ShapeDtypeStruct((s_len, 128), F32), jax.ShapeDtypeStruct((s_len, d), BF16)],
        compiler_params=pltpu.CompilerParams(dimension_semantics=("parallel",)),
    )(a, gate, x, w, g, b)


def _ln_bwd(dout, xh, rs, g, w, gate, a, name):
    s_len, d = dout.shape
    tm = min(256, s_len)

    def body(do_ref, xh_ref, rs_ref, g_ref, w_ref, gate_ref, a_ref, dz_ref, da_ref, dgate_ref, dg_ref, db_ref):
        @pl.when(pl.program_id(0) == 0)
        def _():
            dg_ref[...] = jnp.zeros_like(dg_ref)
            db_ref[...] = jnp.zeros_like(db_ref)

        dout_t = do_ref[...]
        xh_t = xh_ref[...]
        dg_ref[...] += jnp.sum(dout_t * xh_t, axis=0, keepdims=True)
        db_ref[...] += jnp.sum(dout_t, axis=0, keepdims=True)
        dxh = dout_t * g_ref[...]
        m1 = jnp.mean(dxh, axis=1, keepdims=True)
        m2 = jnp.mean(dxh * xh_t, axis=1, keepdims=True)
        dz = rs_ref[:, 0:1] * (dxh - m1 - xh_t * m2)
        dz_ref[...] = dz
        dyg = lax.dot_general(dz.astype(BF16), w_ref[...], NT_DIMS, preferred_element_type=F32)
        gt = gate_ref[...]
        sg = _sigmoid(gt)
        da_ref[...] = dyg * (gt * sg)
        dgate_ref[...] = (dyg * a_ref[...] * (sg * (1.0 + gt * (1.0 - sg)))).astype(BF16)

    row = pl.BlockSpec((tm, d), lambda i: (i, 0))
    vec = pl.BlockSpec((1, d), lambda i: (0, 0))
    return pl.pallas_call(
        body, name=name, grid=(s_len // tm,),
        in_specs=[row, row, pl.BlockSpec((tm, 128), lambda i: (i, 0)), vec, pl.BlockSpec(w.shape, lambda i: (0, 0)),
                  row, row],
        out_specs=[row, row, row, vec, vec],
        out_shape=[jax.ShapeDtypeStruct((s_len, d), F32), jax.ShapeDtypeStruct((s_len, d), F32),
                   jax.ShapeDtypeStruct((s_len, d), BF16), jax.ShapeDtypeStruct((1, d), F32),
                   jax.ShapeDtypeStruct((1, d), F32)],
        compiler_params=pltpu.CompilerParams(dimension_semantics=("arbitrary",)),
    )(dout, xh, rs, g, w, gate, a)


def _loss_grad(y, target):
    s_len, d = y.shape
    tm = min(512, s_len)

    def body(y_ref, t_ref, dy_ref, acc_ref):
        @pl.when(pl.program_id(0) == 0)
        def _():
            acc_ref[...] = jnp.zeros_like(acc_ref)

        diff = y_ref[...] - t_ref[...]
        dy_ref[...] = diff * (1.0 / d)
        acc_ref[...] += jnp.sum(diff * diff, axis=0, keepdims=True)

    row = pl.BlockSpec((tm, d), lambda i: (i, 0))
    return pl.pallas_call(
        body, name="loss_grad", grid=(s_len // tm,),
        in_specs=[row, row], out_specs=[row, pl.BlockSpec((1, d), lambda i: (0, 0))],
        out_shape=[jax.ShapeDtypeStruct((s_len, d), F32), jax.ShapeDtypeStruct((1, d), F32)],
        compiler_params=pltpu.CompilerParams(dimension_semantics=("arbitrary",)),
    )(y, target)


def _rnn_fwd(u, conv_w, conv_b, wa, ba, wi, bi, lam):
    s_len, width = u.shape
    tm = min(256, s_len)
    nb = width // RNN_BLOCK

    def body(u_ref, up_ref, cw_ref, cb_ref, wa_ref, ba_ref, wi_ref, bi_ref, lam_ref,
             h_ref, uc_ref, r_ref, i_ref, a_ref, b_scr, h_carry):
        step_id = pl.program_id(0)

        @pl.when(step_id == 0)
        def _():
            h_carry[...] = jnp.zeros_like(h_carry)

        for n in range(nb):
            blk = slice(n * RNN_BLOCK, (n + 1) * RNN_BLOCK)
            ut = u_ref[:, blk]
            prev = jnp.where(step_id > 0, up_ref[:, blk], 0.0)
            uc = cb_ref[:, blk] + cw_ref[3:4, blk] * ut
            for k in (1, 2, 3):
                uc = uc + cw_ref[3 - k:4 - k, blk] * _shift_down(ut, prev, k)
            ucb = uc.astype(BF16)
            r = _sigmoid(jnp.dot(ucb, wa_ref[n], preferred_element_type=F32) + ba_ref[:, blk])
            ig = _sigmoid(jnp.dot(ucb, wi_ref[n], preferred_element_type=F32) + bi_ref[:, blk])
            log_a = -LRU_C * r * _softplus(-lam_ref[:, blk])
            uc_ref[:, blk] = uc
            r_ref[:, blk] = r
            i_ref[:, blk] = ig
            a_ref[:, blk] = jnp.exp(log_a)
            b_scr[:, blk] = jnp.sqrt(_neg_expm1(2.0 * log_a)) * (ig * uc)

        def scan(t, h):
            h = a_ref[pl.ds(t, 1), :] * h + b_scr[pl.ds(t, 1), :]
            h_ref[pl.ds(t, 1), :] = h
            return h

        h_carry[...] = lax.fori_loop(0, tm, scan, h_carry[...], unroll=8)

    row = pl.BlockSpec((tm, width), lambda i: (i, 0))
    prev8 = pl.BlockSpec((8, width), lambda i: (jnp.maximum(i * (tm // 8) - 1, 0), 0))
    vec = pl.BlockSpec((1, width), lambda i: (0, 0))
    mat = pl.BlockSpec(wa.shape, lambda i: (0, 0, 0))
    return pl.pallas_call(
        body, name="rnn_fwd", grid=(s_len // tm,),
        in_specs=[row, prev8, pl.BlockSpec(conv_w.shape, lambda i: (0, 0)), vec, mat, vec, mat, vec, vec],
        out_specs=[row] * 5,
        out_shape=[jax.ShapeDtypeStruct((s_len, width), F32)] * 5,
        scratch_shapes=[pltpu.VMEM((tm, width), F32), pltpu.VMEM((1, width), F32)],
        compiler_params=pltpu.CompilerParams(dimension_semantics=("arbitrary",)),
    )(u, u, conv_w, conv_b, wa, ba, wi, bi, lam)


def _rnn_bwd(dh, a, h, r, ig, uc, lam, wa, wi):
    s_len, width = dh.shape
    tm = min(256, s_len)
    nt = s_len // tm
    nb = width // RNN_BLOCK

    def body(dh_ref, a_ref, h_ref, hp_ref, r_ref, i_ref, uc_ref, lam_ref, wa_ref, wi_ref,
             duc_ref, dwa_ref, dwi_ref, dba_ref, dbi_ref, dlam_ref, g_scr, c_scr, dsp_scr):
        step_id = pl.program_id(0)
        tile_id = nt - 1 - step_id

        @pl.when(step_id == 0)
        def _():
            c_scr[...] = jnp.zeros_like(c_scr)
            dsp_scr[...] = jnp.zeros_like(dsp_scr)
            dwa_ref[...] = jnp.zeros_like(dwa_ref)
            dwi_ref[...] = jnp.zeros_like(dwi_ref)
            dba_ref[...] = jnp.zeros_like(dba_ref)
            dbi_ref[...] = jnp.zeros_like(dbi_ref)

        def scan(j, c):
            t = tm - 1 - j
            g = dh_ref[pl.ds(t, 1), :] + c
            g_scr[pl.ds(t, 1), :] = g
            return a_ref[pl.ds(t, 1), :] * g

        c_scr[...] = lax.fori_loop(0, tm, scan, c_scr[...], unroll=8)

        for n in range(nb):
            blk = slice(n * RNN_BLOCK, (n + 1) * RNN_BLOCK)
            g_t = g_scr[:, blk]
            hp8 = jnp.where(tile_id > 0, hp_ref[:, blk], 0.0)
            h_prev = _shift_down(h_ref[:, blk], hp8, 1)
            av, rv, iv, ucv = a_ref[:, blk], r_ref[:, blk], i_ref[:, blk], uc_ref[:, blk]
            sp = _softplus(-lam_ref[:, blk])
            mag = jnp.sqrt(_neg_expm1(-2.0 * LRU_C * rv * sp))
            d_iu = g_t * mag
            dla = g_t * h_prev * av - g_t * (iv * ucv) * (av * av) / mag
            dsp_scr[:, blk] += jnp.sum(dla * (-LRU_C * rv), axis=0, keepdims=True)
            dra = dla * (-LRU_C * sp) * rv * (1.0 - rv)
            dia = d_iu * ucv * iv * (1.0 - iv)
            drab, diab, ucb = dra.astype(BF16), dia.astype(BF16), ucv.astype(BF16)
            duc_ref[:, blk] = (d_iu * iv
                               + lax.dot_general(drab, wa_ref[n], NT_DIMS, preferred_element_type=F32)
                               + lax.dot_general(diab, wi_ref[n], NT_DIMS, preferred_element_type=F32))
            dwa_ref[n] += lax.dot_general(ucb, drab, TN_DIMS, preferred_element_type=F32)
            dwi_ref[n] += lax.dot_general(ucb, diab, TN_DIMS, preferred_element_type=F32)
            dba_ref[:, blk] += jnp.sum(dra, axis=0, keepdims=True)
            dbi_ref[:, blk] += jnp.sum(dia, axis=0, keepdims=True)

        @pl.when(step_id == nt - 1)
        def _():
            dlam_ref[...] = -dsp_scr[...] * _sigmoid(-lam_ref[...])

    row = pl.BlockSpec((tm, width), lambda i: (nt - 1 - i, 0))
    prev8 = pl.BlockSpec((8, width), lambda i: (jnp.maximum((nt - 1 - i) * (tm // 8) - 1, 0), 0))
    vec = pl.BlockSpec((1, width), lambda i: (0, 0))
    mat = pl.BlockSpec(wa.shape, lambda i: (0, 0, 0))
    return pl.pallas_call(
        body, name="rnn_bwd", grid=(nt,),
        in_specs=[row, row, row, prev8, row, row, row, vec, mat, mat],
        out_specs=[row, mat, mat, vec, vec, vec],
        out_shape=[jax.ShapeDtypeStruct((s_len, width), F32), jax.ShapeDtypeStruct(wa.shape, F32),
                   jax.ShapeDtypeStruct(wa.shape, F32)] + [jax.ShapeDtypeStruct((1, width), F32)] * 3,
        scratch_shapes=[pltpu.VMEM((tm, width), F32), pltpu.VMEM((1, width), F32), pltpu.VMEM((1, width), F32)],
        compiler_params=pltpu.CompilerParams(dimension_semantics=("arbitrary",)),
    )(dh, a, h, h, r, ig, uc, lam, wa, wi)


def _conv_bwd(duc, u, conv_w):
    s_len, width = duc.shape
    tm = min(256, s_len)
    nt = s_len // tm

    def body(d_ref, dn_ref, u_ref, up_ref, cw_ref, du_ref, dcw_ref, dcb_ref):
        step_id = pl.program_id(0)

        @pl.when(step_id == 0)
        def _():
            dcw_ref[...] = jnp.zeros_like(dcw_ref)
            dcb_ref[...] = jnp.zeros_like(dcb_ref)

        for n in range(width // RNN_BLOCK):
            blk = slice(n * RNN_BLOCK, (n + 1) * RNN_BLOCK)
            dt = d_ref[:, blk]
            ut = u_ref[:, blk]
            nxt = jnp.where(step_id < nt - 1, dn_ref[:, blk], 0.0)
            prev = jnp.where(step_id > 0, up_ref[:, blk], 0.0)
            du = cw_ref[3:4, blk] * dt
            dcw_ref[3:4, blk] += jnp.sum(dt * ut, axis=0, keepdims=True)
            for k in (1, 2, 3):
                du = du + cw_ref[3 - k:4 - k, blk] * _shift_up(dt, nxt, k)
                dcw_ref[3 - k:4 - k, blk] += jnp.sum(dt * _shift_down(ut, prev, k), axis=0, keepdims=True)
            du_ref[:, blk] = du.astype(BF16)
            dcb_ref[:, blk] += jnp.sum(dt, axis=0, keepdims=True)

    row = pl.BlockSpec((tm, width), lambda i: (i, 0))
    prev8 = pl.BlockSpec((8, width), lambda i: (jnp.maximum(i * (tm // 8) - 1, 0), 0))
    next8 = pl.BlockSpec((8, width), lambda i: (jnp.minimum((i + 1) * (tm // 8), s_len // 8 - 1), 0))
    return pl.pallas_call(
        body, name="conv_bwd", grid=(nt,),
        in_specs=[row, next8, row, prev8, pl.BlockSpec(conv_w.shape, lambda i: (0, 0))],
        out_specs=[row, pl.BlockSpec(conv_w.shape, lambda i: (0, 0)), pl.BlockSpec((1, width), lambda i: (0, 0))],
        out_shape=[jax.ShapeDtypeStruct((s_len, width), BF16), jax.ShapeDtypeStruct(conv_w.shape, F32),
                   jax.ShapeDtypeStruct((1, width), F32)],
        compiler_params=pltpu.CompilerParams(dimension_semantics=("arbitrary",)),
    )(duc, duc, u, u, conv_w)


def _add2(a, b):
    shape = a.shape
    tb = 128
    blk = pl.BlockSpec((shape[0], tb, LANES), lambda i: (0, i, 0))

    def body(a_ref, b_ref, o_ref):
        o_ref[...] = a_ref[...] + b_ref[...]

    return pl.pallas_call(
        body, name="pair_sum", grid=(shape[1] // tb,), in_specs=[blk, blk], out_specs=blk,
        out_shape=jax.ShapeDtypeStruct(shape, F32),
        compiler_params=pltpu.CompilerParams(dimension_semantics=("parallel",)),
    )(a, b)


def _sum_chips(parts):
    rows = parts.shape[1]
    tb = 128

    def body(p_ref, o_ref):
        o_ref[...] = ((p_ref[0] + p_ref[1]) + p_ref[2]) + p_ref[3]

    return pl.pallas_call(
        body, name="chip_sum", grid=(rows // tb,),
        in_specs=[pl.BlockSpec((N_CHIPS, tb, LANES), lambda i: (0, i, 0))],
        out_specs=pl.BlockSpec((tb, LANES), lambda i: (i, 0)),
        out_shape=jax.ShapeDtypeStruct((rows, LANES), F32),
        compiler_params=pltpu.CompilerParams(dimension_semantics=("parallel",)),
    )(parts)


def _adamw(w, g, m, v):
    rows = w.shape[0]
    tb = 128
    blk = pl.BlockSpec((tb, LANES), lambda i: (i, 0))

    def body(w_ref, g_ref, m_ref, v_ref, d_ref, nm_ref, nv_ref):
        gt = g_ref[...]
        nm = ADAM_B1 * m_ref[...] + (1.0 - ADAM_B1) * gt
        nv = ADAM_B2 * v_ref[...] + (1.0 - ADAM_B2) * (gt * gt)
        m_hat = nm / (1.0 - ADAM_B1 ** ADAM_STEP)
        v_hat = nv / (1.0 - ADAM_B2 ** ADAM_STEP)
        d_ref[...] = -ADAM_LR * (m_hat / (jnp.sqrt(v_hat) + ADAM_EPS) + ADAM_WD * w_ref[...])
        nm_ref[...] = nm
        nv_ref[...] = nv

    return pl.pallas_call(
        body, name="adamw", grid=(rows // tb,), in_specs=[blk] * 4, out_specs=[blk] * 3,
        out_shape=[jax.ShapeDtypeStruct(w.shape, F32)] * 3,
        compiler_params=pltpu.CompilerParams(dimension_semantics=("parallel",)),
    )(w, g, m, v)


def _place():
    x, y, c = lax.axis_index("x"), lax.axis_index("y"), lax.axis_index("c")
    return x, y, c, [(1 - x, y), (x, 1 - y), (1 - x, 1 - y)]


ANY = pl.BlockSpec(memory_space=pl.ANY)


def _gather_chips(shard):
    rows = shard.shape[0]
    half = rows // 2

    def body(p_ref, out_ref, send_sems, recv_sems, local_sem):
        x, y, c, chips = _place()
        me = 2 * x + y

        def rows_of(chip, hc):
            return out_ref.at[chip, pl.ds(hc * half, half), :]

        def copy(k, src, dst, to):
            return pltpu.make_async_remote_copy(src_ref=src, dst_ref=dst, send_sem=send_sems.at[k],
                                                recv_sem=recv_sems.at[k], device_id=to, device_id_type=MESH)

        mine = pltpu.make_async_copy(p_ref, out_ref.at[me], local_sem)
        mine.start()
        first = [copy(j, p_ref.at[pl.ds(c * half, half), :], rows_of(me, c), (cx, cy, c))
                 for j, (cx, cy) in enumerate(chips)]
        for cp in first:
            cp.start()
        passed = []
        for j, (cx, cy) in enumerate(chips):
            landed = rows_of(2 * cx + cy, c)
            copy(j, landed, landed, (x, y, c)).wait_recv()
            fwd = copy(3 + j, landed, landed, (x, y, 1 - c))
            fwd.start()
            passed.append(fwd)
        for j, (cx, cy) in enumerate(chips):
            other = rows_of(2 * cx + cy, 1 - c)
            copy(3 + j, other, other, (x, y, c)).wait_recv()
        for cp in first + passed:
            cp.wait_send()
        mine.wait()

    return pl.pallas_call(
        body, name="gather_chips", in_specs=[ANY], out_specs=ANY,
        out_shape=jax.ShapeDtypeStruct((N_CHIPS,) + shard.shape, shard.dtype),
        scratch_shapes=[pltpu.SemaphoreType.DMA((6,)), pltpu.SemaphoreType.DMA((6,)), pltpu.SemaphoreType.DMA],
    )(shard)


def _swap_halves(grads):
    n, rows, lanes = grads.shape
    half = rows // 2

    def body(g_ref, own_ref, theirs_ref, send_sem, recv_sem, local_sem):
        x, y, c, _ = _place()
        keep = pltpu.make_async_copy(g_ref.at[:, pl.ds(c * half, half), :], own_ref, local_sem)
        keep.start()
        give = pltpu.make_async_remote_copy(
            src_ref=g_ref.at[:, pl.ds((1 - c) * half, half), :], dst_ref=theirs_ref, send_sem=send_sem,
            recv_sem=recv_sem, device_id=(x, y, 1 - c), device_id_type=MESH)
        give.start()
        give.wait()
        keep.wait()

    out = jax.ShapeDtypeStruct((n, half, lanes), grads.dtype)
    return pl.pallas_call(
        body, name="swap_halves", in_specs=[ANY], out_specs=[ANY, ANY], out_shape=[out, out],
        scratch_shapes=[pltpu.SemaphoreType.DMA, pltpu.SemaphoreType.DMA, pltpu.SemaphoreType.DMA],
    )(grads)


def _scatter_chips(parts):
    def body(t_ref, out_ref, send_sems, recv_sems, local_sem):
        x, y, c, chips = _place()
        me = 2 * x + y
        mine = pltpu.make_async_copy(t_ref.at[me], out_ref.at[me], local_sem)
        mine.start()
        sends = []
        for j, (cx, cy) in enumerate(chips):
            cp = pltpu.make_async_remote_copy(
                src_ref=t_ref.at[2 * cx + cy], dst_ref=out_ref.at[me], send_sem=send_sems.at[j],
                recv_sem=recv_sems.at[j], device_id=(cx, cy, c), device_id_type=MESH)
            cp.start()
            sends.append(cp)
        for j, (cx, cy) in enumerate(chips):
            slot = out_ref.at[2 * cx + cy]
            pltpu.make_async_remote_copy(src_ref=slot, dst_ref=slot, send_sem=send_sems.at[j],
                                         recv_sem=recv_sems.at[j], device_id=(x, y, c), device_id_type=MESH).wait_recv()
        for cp in sends:
            cp.wait_send()
        mine.wait()

    return pl.pallas_call(
        body, name="scatter_chips", in_specs=[ANY], out_specs=ANY,
        out_shape=jax.ShapeDtypeStruct(parts.shape, parts.dtype),
        scratch_shapes=[pltpu.SemaphoreType.DMA((3,)), pltpu.SemaphoreType.DMA((3,)), pltpu.SemaphoreType.DMA],
    )(parts)


def _join_halves(mine):
    half, lanes = mine.shape

    def body(h_ref, out_ref, send_sem, recv_sem, local_sem):
        x, y, c, _ = _place()
        keep = pltpu.make_async_copy(h_ref, out_ref.at[pl.ds(c * half, half), :], local_sem)
        keep.start()
        give = pltpu.make_async_remote_copy(
            src_ref=h_ref, dst_ref=out_ref.at[pl.ds(c * half, half), :], send_sem=send_sem, recv_sem=recv_sem,
            device_id=(x, y, 1 - c), device_id_type=MESH)
        give.start()
        give.wait_send()
        theirs = out_ref.at[pl.ds((1 - c) * half, half), :]
        pltpu.make_async_remote_copy(src_ref=theirs, dst_ref=theirs, send_sem=send_sem, recv_sem=recv_sem,
                                     device_id=(x, y, c), device_id_type=MESH).wait_recv()
        keep.wait()

    return pl.pallas_call(
        body, name="join_halves", in_specs=[ANY], out_specs=ANY,
        out_shape=jax.ShapeDtypeStruct((2 * half, lanes), mine.dtype),
        scratch_shapes=[pltpu.SemaphoreType.DMA, pltpu.SemaphoreType.DMA, pltpu.SemaphoreType.DMA],
    )(mine)


def _pack(arrays, dtype):
    flat = []
    for arr in arrays:
        v = arr.reshape(-1)
        flat.append(jnp.pad(v, (0, (-v.shape[0]) % LANES)))
    total = sum(f.shape[0] for f in flat) // LANES
    pad_rows = (-total) % ROW_ALIGN
    if pad_rows:
        flat.append(jnp.zeros((pad_rows * LANES,), dtype))
    return jnp.concatenate(flat).reshape(-1, LANES)


def _unpack(buf, shapes):
    lead = buf.shape[:-2]
    flat = buf.reshape(lead + (-1,))
    out, off = [], 0
    for shape in shapes:
        size = 1
        for dim in shape:
            size *= dim
        out.append(flat[..., off:off + size].reshape(lead + tuple(shape)))
        off += size + (-size) % LANES
    return out


def _from_chips(parts, axis):
    return jnp.concatenate([parts[j] for j in range(N_CHIPS)], axis=axis)


def _shard_of(full, axis, j):
    size = full.shape[axis] // N_CHIPS
    return lax.slice_in_dim(full, j * size, (j + 1) * size, axis=axis)


def kernel(x, ln_g, ln_b, attn_w_in, attn_b_f, attn_w_out, rnn_w_in, rnn_conv_w, rnn_conv_b, rnn_w_a, rnn_b_a, rnn_w_i, rnn_b_i, rnn_lambda, rnn_w_out, loss_target, m_ln_g, m_ln_b, m_attn_w_in, m_attn_b_f, m_attn_w_out, m_rnn_w_in, m_rnn_conv_w, m_rnn_conv_b, m_rnn_w_a, m_rnn_b_a, m_rnn_w_i, m_rnn_b_i, m_rnn_lambda, m_rnn_w_out, v_ln_g, v_ln_b, v_attn_w_in, v_attn_b_f, v_attn_w_out, v_rnn_w_in, v_rnn_conv_w, v_rnn_conv_b, v_rnn_w_a, v_rnn_b_a, v_rnn_w_i, v_rnn_b_i, v_rnn_lambda, v_rnn_w_out):
    s_len, d = x.shape[1], x.shape[2]
    xs = x.reshape(s_len, d)
    target = loss_target.reshape(s_len, d)
    tile = _att_tile(s_len)
    nt = s_len // tile
    heads = attn_b_f.shape[1]
    qkvg = attn_w_in.shape[2] * N_CHIPS - heads

    big = [attn_w_in, attn_w_out, rnn_w_in, rnn_w_a, rnn_w_i, rnn_w_out]
    small = [rnn_conv_w, rnn_conv_b, rnn_b_a, rnn_b_i, rnn_lambda]
    packed = _pack([w.astype(BF16) for w in big] + [lax.bitcast_convert_type(w, BF16) for w in small], BF16)
    gathered = _gather_chips(packed)
    parts = _unpack(gathered, [w.shape for w in big] + [w.shape + (2,) for w in small])
    w_in_a, w_out_a, w_in_r, w_a, w_i, w_out_r = [
        _from_chips(p, ax) for p, ax in zip(parts[:6], (2, 1, 2, 2, 2, 1))]
    conv_w, conv_b, b_a, b_i, lam = [
        _from_chips(lax.bitcast_convert_type(p, F32), ax) for p, ax in zip(parts[6:], (2, 1, 1, 1, 1))]
    w_cat = jnp.concatenate(
        [w_in_a[:, :, :d] * (HEAD_DIM ** -0.5), w_in_a[:, :, d:qkvg],
         jnp.pad(w_in_a[:, :, qkvg:], ((0, 0), (0, 0), (0, 128 - heads)))], axis=2).astype(BF16)
    b_f = jnp.pad(attn_b_f, ((0, 0), (0, 128 - heads)))

    saved = []
    cur = xs
    for layer in range(DEPTH):
        idx = layer // 2
        g_l, b_l = ln_g[layer:layer + 1], ln_b[layer:layer + 1]
        if layer % 2 == 0:
            q, k, v, gate, fl = _proj(cur, w_cat[idx], [(0, d, BF16), (d, d, BF16), (2 * d, d, BF16),
                                                         (3 * d, d, F32), (4 * d, 128, F32)], "attn_proj")
            cum, sneg = _fcum(fl, b_f[idx:idx + 1])
            cumr = cum.reshape(heads * nt, tile)
            o, lse = _flash_fwd(q, k, v, cumr, tile)
            nxt, xh, rs, yg = _out_ln(o, gate, cur, w_out_a[idx], g_l, b_l, "out_ln")
            saved.append(dict(x=cur, q=q, k=k, v=v, gate=gate, cumr=cumr, sneg=sneg, a=o, lse=lse, xh=xh, rs=rs, yg=yg))
        else:
            u, gate = _proj(cur, w_in_r[idx], [(0, d, F32), (d, d, F32)], "rnn_proj")
            vecs = [t[idx:idx + 1] for t in (conv_b, b_a, b_i, lam)]
            hseq, uc, r, ig, a = _rnn_fwd(u, conv_w[idx], vecs[0], w_a[idx], vecs[1], w_i[idx], vecs[2], vecs[3])
            nxt, xh, rs, yg = _out_ln(hseq, gate, cur, w_out_r[idx], g_l, b_l, "out_ln")
            saved.append(dict(x=cur, u=u, gate=gate, uc=uc, r=r, ig=ig, av=a, a=hseq, xh=xh, rs=rs, yg=yg, lam=vecs[3]))
        cur = nxt

    dout, sq = _loss_grad(cur, target)
    loss = lax.psum(0.5 * jnp.sum(sq) / d, ("x", "y", "c"))

    g_ln_g, g_ln_b = [None] * DEPTH, [None] * DEPTH
    g_attn = [None] * 2
    g_rnn = [None] * 2
    for layer in reversed(range(DEPTH)):
        idx = layer // 2
        sv = saved[layer]
        w_out = w_out_a[idx] if layer % 2 == 0 else w_out_r[idx]
        dz, da, dgate, dg, db = _ln_bwd(dout, sv["xh"], sv["rs"], ln_g[layer:layer + 1], w_out, sv["gate"], sv["a"],
                                        "ln_bwd")
        g_ln_g[layer], g_ln_b[layer] = dg, db
        d_w_out = _mm_tn(sv["yg"], dz, "dw_out")
        if layer % 2 == 0:
            dq, dk, dv, dcumr = _flash_bwd(sv["q"], sv["k"], sv["v"], sv["a"], da, sv["lse"], sv["cumr"], tile)
            dlogit, dbf = _fbwd(dcumr.reshape(heads, s_len), sv["sneg"])
            pieces = [dq, dk, dv, dgate, dlogit]
            dout = _mm_nt(dz, [(p, j * d) for j, p in enumerate(pieces)], w_cat[idx], "attn_dx")
            dws = [_mm_tn(sv["x"], p, "dw_in") for p in pieces[:4]] + [_mm_tn(sv["x"], dlogit, "dw_f")]
            d_w_in = jnp.concatenate([dws[0] * (HEAD_DIM ** -0.5), dws[1], dws[2], dws[3], dws[4][:, :heads]], axis=1)
            g_attn[idx] = dict(w_in=d_w_in, b_f=dbf[:, 0], w_out=d_w_out)
        else:
            duc, d_wa, d_wi, d_ba, d_bi, d_lam = _rnn_bwd(da, sv["av"], sv["a"], sv["r"], sv["ig"], sv["uc"], sv["lam"],
                                                          w_a[idx], w_i[idx])
            du, d_cw, d_cb = _conv_bwd(duc, sv["u"], conv_w[idx])
            dout = _mm_nt(dz, [(du, 0), (dgate, d)], w_in_r[idx], "rnn_dx")
            d_w_in = jnp.concatenate([_mm_tn(sv["x"], du, "dw_in"), _mm_tn(sv["x"], dgate, "dw_in")], axis=1)
            g_rnn[idx] = dict(w_in=d_w_in, conv_w=d_cw, conv_b=d_cb[0], w_a=d_wa, b_a=d_ba[0], w_i=d_wi, b_i=d_bi[0],
                              lam=d_lam[0], w_out=d_w_out)
    grad_x = dout.reshape(x.shape)

    def stack(items, key):
        return jnp.stack([it[key] for it in items])

    full = [
        (stack(g_attn, "w_in"), 2), (stack(g_attn, "w_out"), 1), (stack(g_rnn, "w_in"), 2),
        (stack(g_rnn, "conv_w"), 2), (stack(g_rnn, "conv_b"), 1), (stack(g_rnn, "w_a"), 2),
        (stack(g_rnn, "b_a"), 1), (stack(g_rnn, "w_i"), 2), (stack(g_rnn, "b_i"), 1), (stack(g_rnn, "lam"), 1),
        (stack(g_rnn, "w_out"), 1),
    ]
    whole = [jnp.concatenate(g_ln_g), jnp.concatenate(g_ln_b), stack(g_attn, "b_f")]
    per_chip = jnp.stack([_pack([_shard_of(g, ax, j) for g, ax in full] + whole, F32) for j in range(N_CHIPS)])

    own, theirs = _swap_halves(per_chip)
    from_chips = _scatter_chips(_add2(own, theirs))
    grads = _join_halves(_sum_chips(from_chips))

    sharded = [attn_w_in, attn_w_out, rnn_w_in, rnn_conv_w, rnn_conv_b, rnn_w_a, rnn_b_a, rnn_w_i, rnn_b_i,
               rnn_lambda, rnn_w_out]
    m_sh = [m_attn_w_in, m_attn_w_out, m_rnn_w_in, m_rnn_conv_w, m_rnn_conv_b, m_rnn_w_a, m_rnn_b_a, m_rnn_w_i,
            m_rnn_b_i, m_rnn_lambda, m_rnn_w_out]
    v_sh = [v_attn_w_in, v_attn_w_out, v_rnn_w_in, v_rnn_conv_w, v_rnn_conv_b, v_rnn_w_a, v_rnn_b_a, v_rnn_w_i,
            v_rnn_b_i, v_rnn_lambda, v_rnn_w_out]
    order = sharded + [ln_g, ln_b, attn_b_f]
    shapes = [w.shape for w in order]
    delta, new_m, new_v = _adamw(_pack(order, F32), grads, _pack(m_sh + [m_ln_g, m_ln_b, m_attn_b_f], F32),
                                 _pack(v_sh + [v_ln_g, v_ln_b, v_attn_b_f], F32))

    def in_weight_order(buf):
        t = _unpack(buf, shapes)
        return [t[11], t[12], t[0], t[13]] + t[1:11]

    return (loss, grad_x, *in_weight_order(grads), *in_weight_order(delta), *in_weight_order(new_m),
            *in_weight_order(new_v))
```

```python
import functools

import jax
import jax.numpy as jnp
from jax import lax
from jax.experimental import pallas as pl
from jax.experimental.pallas import tpu as pltpu

F32 = jnp.float32
BF16 = jnp.bfloat16
MESH = pl.DeviceIdType.MESH

DEPTH = 4
HEAD_DIM = 64
HEAD_PAIR = 2 * HEAD_DIM
RNN_BLOCK = 256
CONV_WIDTH = 4
LRU_C = 8.0
ALPHA = (2.0 * DEPTH) ** 0.25
LN_EPS = 1e-5
ADAM_LR, ADAM_B1, ADAM_B2, ADAM_EPS, ADAM_WD, ADAM_STEP = 0.001, 0.9, 0.999, 1e-08, 0.01, 10
N_CHIPS = 4
LANES = 1024
ROW_ALIGN = 256
NEG = -1e30

NT_DIMS = (((1,), (1,)), ((), ()))
TN_DIMS = (((0,), (0,)), ((), ()))


def _sigmoid(z):
    return 1.0 / (1.0 + jnp.exp(-z))


def _softplus(z):
    return jnp.maximum(z, 0.0) + jnp.log(1.0 + jnp.exp(-jnp.abs(z)))


def _neg_expm1(y):
    poly = y * (1.0 + y * (0.5 + y * (1.0 / 6.0 + y * (1.0 / 24.0))))
    return -jnp.where(y > -0.05, poly, jnp.exp(y) - 1.0)


def _shift_down(cur, prev8, k):
    n = cur.shape[0]
    row = lax.broadcasted_iota(jnp.int32, cur.shape, 0)
    fix = jnp.tile(pltpu.roll(prev8, k, 0), (n // 8, 1))
    return jnp.where(row < k, fix, pltpu.roll(cur, k, 0))


def _shift_up(cur, next8, k):
    n = cur.shape[0]
    row = lax.broadcasted_iota(jnp.int32, cur.shape, 0)
    fix = jnp.tile(pltpu.roll(next8, 8 - k, 0), (n // 8, 1))
    return jnp.where(row >= n - k, fix, pltpu.roll(cur, n - k, 0))


def _proj(x, w, outs, name):
    s_len, d = x.shape
    tm = min(512, s_len)

    def body(x_ref, w_ref, *o_refs):
        xb = x_ref[...].astype(BF16)
        for (c0, wd, dt), o_ref in zip(outs, o_refs):
            step = min(wd, 512)
            for cc in range(0, wd, step):
                o_ref[:, cc:cc + step] = jnp.dot(
                    xb, w_ref[:, c0 + cc:c0 + cc + step], preferred_element_type=F32).astype(dt)

    return pl.pallas_call(
        body, name=name, grid=(s_len // tm,),
        in_specs=[pl.BlockSpec((tm, d), lambda i: (i, 0)), pl.BlockSpec(w.shape, lambda i: (0, 0))],
        out_specs=[pl.BlockSpec((tm, wd), lambda i: (i, 0)) for _, wd, _ in outs],
        out_shape=[jax.ShapeDtypeStruct((s_len, wd), dt) for _, wd, dt in outs],
        compiler_params=pltpu.CompilerParams(dimension_semantics=("parallel",)),
    )(x, w)


def _mm_nt(dz, pieces, w, name):
    s_len, d = dz.shape
    tm = min(256, s_len)
    n = len(pieces)
    cols = [(c0, p.shape[1]) for p, c0 in pieces]

    def body(dz_ref, *rest):
        w_ref, o_ref = rest[n], rest[n + 1]
        acc = ALPHA * dz_ref[...]
        for (c0, wd), p_ref in zip(cols, rest[:n]):
            acc = acc + lax.dot_general(p_ref[...], w_ref[:, c0:c0 + wd], NT_DIMS, preferred_element_type=F32)
        o_ref[...] = acc

    return pl.pallas_call(
        body, name=name, grid=(s_len // tm,),
        in_specs=[pl.BlockSpec((tm, d), lambda i: (i, 0))]
        + [pl.BlockSpec((tm, wd), lambda i: (i, 0)) for _, wd in cols]
        + [pl.BlockSpec(w.shape, lambda i: (0, 0))],
        out_specs=pl.BlockSpec((tm, d), lambda i: (i, 0)),
        out_shape=jax.ShapeDtypeStruct((s_len, d), F32),
        compiler_params=pltpu.CompilerParams(dimension_semantics=("parallel",)),
    )(dz, *[p for p, _ in pieces], w)


def _mm_tn(a, b, name):
    s_len, m = a.shape
    n = b.shape[1]
    tn = min(n, 512)
    ts = min(s_len, 512)

    def body(a_ref, b_ref, o_ref):
        @pl.when(pl.program_id(1) == 0)
        def _():
            o_ref[...] = jnp.zeros_like(o_ref)

        o_ref[...] += lax.dot_general(a_ref[...].astype(BF16), b_ref[...].astype(BF16), TN_DIMS,
                                      preferred_element_type=F32)

    return pl.pallas_call(
        body, name=name, grid=(n // tn, s_len // ts),
        in_specs=[pl.BlockSpec((ts, m), lambda j, s: (s, 0)), pl.BlockSpec((ts, tn), lambda j, s: (s, j))],
        out_specs=pl.BlockSpec((m, tn), lambda j, s: (0, j)),
        out_shape=jax.ShapeDtypeStruct((m, n), F32),
        compiler_params=pltpu.CompilerParams(dimension_semantics=("parallel", "arbitrary")),
    )(a, b)


def _fcum(fl, bias):
    s_len = fl.shape[0]

    def body(fl_ref, b_ref, cum_ref, sg_ref):
        z = fl_ref[...] + b_ref[...]
        e = jnp.exp(-jnp.abs(z))
        logf = jnp.minimum(z, 0.0) - jnp.log(1.0 + e)
        sneg = jnp.where(z >= 0, e, 1.0) / (1.0 + e)
        run = logf.T[0:16, :]
        sg_ref[...] = sneg.T[0:16, :]
        lane = lax.broadcasted_iota(jnp.int32, (16, s_len), 1)
        sh = 1
        while sh < s_len:
            run = run + jnp.where(lane >= sh, pltpu.roll(run, sh, 1), 0.0)
            sh *= 2
        cum_ref[...] = run

    return pl.pallas_call(
        body, name="fcum",
        out_shape=[jax.ShapeDtypeStruct((16, s_len), F32), jax.ShapeDtypeStruct((16, s_len), F32)],
    )(fl, bias)


def _fbwd(dcum, sneg):
    s_len = dcum.shape[1]

    def body(dc_ref, sg_ref, dl_ref, db_ref):
        run = dc_ref[...]
        lane = lax.broadcasted_iota(jnp.int32, (16, s_len), 1)
        sh = 1
        while sh < s_len:
            run = run + jnp.where(lane < s_len - sh, pltpu.roll(run, s_len - sh, 1), 0.0)
            sh *= 2
        dlog = run * sg_ref[...]
        db_ref[...] = jnp.broadcast_to(jnp.sum(dlog, axis=1, keepdims=True), (16, 128))
        full = jnp.concatenate([dlog, jnp.zeros((112, s_len), F32)], axis=0)
        dl_ref[...] = full.T.astype(BF16)

    return pl.pallas_call(
        body, name="fbwd",
        out_shape=[jax.ShapeDtypeStruct((s_len, 128), BF16), jax.ShapeDtypeStruct((16, 128), F32)],
    )(dcum, sneg)


def _att_tile(s_len):
    return 256 if s_len >= 512 else s_len // 2


def _flash_fwd(q, k, v, cumr, tile):
    s_len, width = q.shape
    nt = s_len // tile

    def body(q_ref, k_ref, v_ref, cum_ref, o_ref, lse_ref):
        pid = pl.program_id(0)
        is0 = lax.broadcasted_iota(jnp.int32, (tile, HEAD_PAIR), 1) < HEAD_DIM
        causal = (lax.broadcasted_iota(jnp.int32, (tile, tile), 0)
                  >= lax.broadcasted_iota(jnp.int32, (tile, tile), 1))

        def q_loop(qi, carry):
            q0 = pl.multiple_of(qi * tile, tile)
            qt = q_ref[pl.ds(q0, tile), :]
            zq = jnp.zeros_like(qt)
            res = []
            for h in (0, 1):
                qm = jnp.where(is0, qt, zq) if h == 0 else jnp.where(is0, zq, qt)
                hrow = (2 * pid + h) * nt

                def step(kj, m, l, acc, masked, qm=qm, hrow=hrow):
                    k0 = pl.multiple_of(kj * tile, tile)
                    s = lax.dot_general(qm, k_ref[pl.ds(k0, tile), :], NT_DIMS, preferred_element_type=F32)
                    s = s - cum_ref[pl.ds(hrow + kj, 1), :]
                    if masked:
                        s = jnp.where(causal, s, NEG)
                    m_new = jnp.maximum(m, jnp.max(s, axis=1, keepdims=True))
                    p = jnp.exp(s - m_new)
                    scale = jnp.exp(m - m_new)
                    l = scale * l + jnp.sum(p, axis=1, keepdims=True)
                    acc = scale * acc + jnp.dot(p.astype(BF16), v_ref[pl.ds(k0, tile), :],
                                                preferred_element_type=F32)
                    return m_new, l, acc

                init = (jnp.full((tile, 1), NEG, F32), jnp.zeros((tile, 1), F32), jnp.zeros((tile, HEAD_PAIR), F32))
                state = step(qi, *init, True)
                m, l, acc = lax.fori_loop(0, qi, lambda kj, c, step=step: step(kj, *c, False), state)
                res.append((acc / l, m + jnp.log(l)))
            o_ref[pl.ds(q0, tile), :] = jnp.where(is0, res[0][0], res[1][0])
            lse_ref[pl.ds(q0, tile), :] = jnp.where(is0, res[0][1], res[1][1])
            return carry

        lax.fori_loop(0, nt, q_loop, 0)

    blk = pl.BlockSpec((s_len, HEAD_PAIR), lambda p: (0, p))
    return pl.pallas_call(
        body, name="flash_fwd", grid=(width // HEAD_PAIR,),
        in_specs=[blk, blk, blk, pl.BlockSpec(cumr.shape, lambda p: (0, 0))],
        out_specs=[blk, blk],
        out_shape=[jax.ShapeDtypeStruct((s_len, width), F32), jax.ShapeDtypeStruct((s_len, width), F32)],
        compiler_params=pltpu.CompilerParams(dimension_semantics=("parallel",)),
    )(q, k, v, cumr)


def _flash_bwd(q, k, v, o, do, lse, cumr, tile):
    s_len, width = q.shape
    nt = s_len // tile

    def body(q_ref, k_ref, v_ref, o_ref, do_ref, lse_ref, cum_ref, dq_ref, dk_ref, dv_ref, dcum_ref,
             dq_acc, delta, do_bf, row_sum):
        pid = pl.program_id(0)
        is0 = lax.broadcasted_iota(jnp.int32, (tile, HEAD_PAIR), 1) < HEAD_DIM
        causal = (lax.broadcasted_iota(jnp.int32, (tile, tile), 0)
                  >= lax.broadcasted_iota(jnp.int32, (tile, tile), 1))

        def pre(i, carry):
            r0 = pl.multiple_of(i * tile, tile)
            d_o = do_ref[pl.ds(r0, tile), :]
            prod = d_o * o_ref[pl.ds(r0, tile), :]
            d0 = jnp.sum(jnp.where(is0, prod, 0.0), axis=1, keepdims=True)
            d1 = jnp.sum(jnp.where(is0, 0.0, prod), axis=1, keepdims=True)
            delta[pl.ds(r0, tile), :] = jnp.where(is0, d0, d1)
            do_bf[pl.ds(r0, tile), :] = d_o.astype(BF16)
            dq_acc[pl.ds(r0, tile), :] = jnp.zeros((tile, HEAD_PAIR), F32)
            row_sum[pl.ds(r0, tile), :] = jnp.zeros((tile, HEAD_PAIR), F32)
            return carry

        lax.fori_loop(0, nt, pre, 0)

        def kv_loop(kj, carry):
            k0 = pl.multiple_of(kj * tile, tile)
            kt = k_ref[pl.ds(k0, tile), :]
            vt = v_ref[pl.ds(k0, tile), :]
            zb = jnp.zeros_like(kt)
            kms = (jnp.where(is0, kt, zb), jnp.where(is0, zb, kt))
            vms = (jnp.where(is0, vt, zb), jnp.where(is0, zb, vt))
            cums = [cum_ref[pl.ds((2 * pid + h) * nt + kj, 1), :] for h in (0, 1)]

            def step(qi, state, masked):
                dk_a, dv_a, dc0, dc1 = state
                q0 = pl.multiple_of(qi * tile, tile)
                qt = q_ref[pl.ds(q0, tile), :]
                dot = do_bf[pl.ds(q0, tile), :]
                lse_t = lse_ref[pl.ds(q0, tile), :]
                dl_t = delta[pl.ds(q0, tile), :]
                dq_t = jnp.zeros((tile, HEAD_PAIR), F32)
                dcs = [dc0, dc1]
                rsum = []
                for h in (0, 1):
                    c0 = h * HEAD_DIM
                    s = lax.dot_general(qt, kms[h], NT_DIMS, preferred_element_type=F32) - cums[h]
                    if masked:
                        s = jnp.where(causal, s, NEG)
                    p = jnp.exp(s - lse_t[:, c0:c0 + 1])
                    dp = lax.dot_general(dot, vms[h], NT_DIMS, preferred_element_type=F32)
                    ds = p * (dp - dl_t[:, c0:c0 + 1])
                    dcs[h] = dcs[h] + jnp.sum(ds, axis=0, keepdims=True)
                    rsum.append(jnp.sum(ds, axis=1, keepdims=True))
                    pb = p.astype(BF16)
                    dsb = ds.astype(BF16)
                    do_h = jnp.where(is0, dot, zb) if h == 0 else jnp.where(is0, zb, dot)
                    q_h = jnp.where(is0, qt, zb) if h == 0 else jnp.where(is0, zb, qt)
                    dv_a = dv_a + lax.dot_general(pb, do_h, TN_DIMS, preferred_element_type=F32)
                    dk_a = dk_a + lax.dot_general(dsb, q_h, TN_DIMS, preferred_element_type=F32)
                    dq_t = dq_t + jnp.dot(dsb, kms[h], preferred_element_type=F32)
                dq_acc[pl.ds(q0, tile), :] += dq_t
                row_sum[pl.ds(q0, tile), :] += jnp.where(is0, rsum[0], rsum[1])
                return dk_a, dv_a, dcs[0], dcs[1]

            zero = jnp.zeros((tile, HEAD_PAIR), F32)
            zrow = jnp.zeros((1, tile), F32)
            state = step(kj, (zero, zero, zrow, zrow), True)
            dk_a, dv_a, dc0, dc1 = lax.fori_loop(kj + 1, nt, lambda qi, c: step(qi, c, False), state)
            dk_ref[pl.ds(k0, tile), :] = dk_a.astype(BF16)
            dv_ref[pl.ds(k0, tile), :] = dv_a.astype(BF16)
            dcum_ref[pl.ds((2 * pid) * nt + kj, 1), :] = -dc0
            dcum_ref[pl.ds((2 * pid + 1) * nt + kj, 1), :] = -dc1
            return carry

        lax.fori_loop(0, nt, kv_loop, 0)

        def fin(i, carry):
            r0 = pl.multiple_of(i * tile, tile)
            dq_ref[pl.ds(r0, tile), :] = dq_acc[pl.ds(r0, tile), :].astype(BF16)
            return carry

        lax.fori_loop(0, nt, fin, 0)

        by_time = row_sum[...].T
        for h in (0, 1):
            for j in range(nt):
                dcum_ref[pl.ds((2 * pid + h) * nt + j, 1), :] += by_time[h * HEAD_DIM:h * HEAD_DIM + 1,
                                                                          j * tile:(j + 1) * tile]

    blk = pl.BlockSpec((s_len, HEAD_PAIR), lambda p: (0, p))
    full = pl.BlockSpec(cumr.shape, lambda p: (0, 0))
    return pl.pallas_call(
        body, name="flash_bwd", grid=(width // HEAD_PAIR,),
        in_specs=[blk, blk, blk, blk, blk, blk, full],
        out_specs=[blk, blk, blk, full],
        out_shape=[jax.ShapeDtypeStruct((s_len, width), BF16)] * 3 + [jax.ShapeDtypeStruct(cumr.shape, F32)],
        scratch_shapes=[pltpu.VMEM((s_len, HEAD_PAIR), F32), pltpu.VMEM((s_len, HEAD_PAIR), F32),
                        pltpu.VMEM((s_len, HEAD_PAIR), BF16), pltpu.VMEM((s_len, HEAD_PAIR), F32)],
        compiler_params=pltpu.CompilerParams(dimension_semantics=("arbitrary",)),
    )(q, k, v, o, do, lse, cumr)


def _out_ln(a, gate, x, w, g, b, name):
    s_len, d = x.shape
    tm = min(256, s_len)

    def body(a_ref, gate_ref, x_ref, w_ref, g_ref, b_ref, xn_ref, xh_ref, rs_ref, yg_ref):
        gt = gate_ref[...]
        yg = (a_ref[...] * (gt * _sigmoid(gt))).astype(BF16)
        yg_ref[...] = yg
        z = ALPHA * x_ref[...] + jnp.dot(yg, w_ref[...], preferred_element_type=F32)
        zc = z - jnp.mean(z, axis=1, keepdims=True)
        rstd = lax.rsqrt(jnp.mean(zc * zc, axis=1, keepdims=True) + LN_EPS)
        xh = zc * rstd
        xh_ref[...] = xh
        xn_ref[...] = xh * g_ref[...] + b_ref[...]
        rs_ref[...] = jnp.broadcast_to(rstd, (tm, 128))

    row = pl.BlockSpec((tm, d), lambda i: (i, 0))
    vec = pl.BlockSpec((1, d), lambda i: (0, 0))
    return pl.pallas_call(
        body, name=name, grid=(s_len // tm,),
        in_specs=[row, row, row, pl.BlockSpec(w.shape, lambda i: (0, 0)), vec, vec],
        out_specs=[row, row, pl.BlockSpec((tm, 128), lambda i: (i, 0)), row],
        out_shape=[jax.ShapeDtypeStruct((s_len, d), F32), jax.ShapeDtypeStruct((s_len, d), F32),
                   jax.ShapeDtypeStruct((s_len, 128), F32), jax.ShapeDtypeStruct((s_len, d), BF16)],
        compiler_params=pltpu.CompilerParams(dimension_semantics=("parallel",)),
    )(a, gate, x, w, g, b)


def _ln_bwd(dout, xh, rs, g, w, gate, a, name):
    s_len, d = dout.shape
    tm = min(256, s_len)

    def body(do_ref, xh_ref, rs_ref, g_ref, w_ref, gate_ref, a_ref, dz_ref, da_ref, dgate_ref, dg_ref, db_ref):
        @pl.when(pl.program_id(0) == 0)
        def _():
            dg_ref[...] = jnp.zeros_like(dg_ref)
            db_ref[...] = jnp.zeros_like(db_ref)

        dout_t = do_ref[...]
        xh_t = xh_ref[...]
        dg_ref[...] += jnp.sum(dout_t * xh_t, axis=0, keepdims=True)
        db_ref[...] += jnp.sum(dout_t, axis=0, keepdims=True)
        dxh = dout_t * g_ref[...]
        m1 = jnp.mean(dxh, axis=1, keepdims=True)
        m2 = jnp.mean(dxh * xh_t, axis=1, keepdims=True)
        dz = rs_ref[:, 0:1] * (dxh - m1 - xh_t * m2)
        dz_ref[...] = dz
        dyg = lax.dot_general(dz.astype(BF16), w_ref[...], NT_DIMS, preferred_element_type=F32)
        gt = gate_ref[...]
        sg = _sigmoid(gt)
        da_ref[...] = dyg * (gt * sg)
        dgate_ref[...] = (dyg * a_ref[...] * (sg * (1.0 + gt * (1.0 - sg)))).astype(BF16)

    row = pl.BlockSpec((tm, d), lambda i: (i, 0))
    vec = pl.BlockSpec((1, d), lambda i: (0, 0))
    return pl.pallas_call(
        body, name=name, grid=(s_len // tm,),
        in_specs=[row, row, pl.BlockSpec((tm, 128), lambda i: (i, 0)), vec, pl.BlockSpec(w.shape, lambda i: (0, 0)),
                  row, row],
        out_specs=[row, row, row, vec, vec],
        out_shape=[jax.ShapeDtypeStruct((s_len, d), F32), jax.ShapeDtypeStruct((s_len, d), F32),
                   jax.ShapeDtypeStruct((s_len, d), BF16), jax.ShapeDtypeStruct((1, d), F32),
                   jax.ShapeDtypeStruct((1, d), F32)],
        compiler_params=pltpu.CompilerParams(dimension_semantics=("arbitrary",)),
    )(dout, xh, rs, g, w, gate, a)


def _loss_grad(y, target):
    s_len, d = y.shape
    tm = min(512, s_len)

    def body(y_ref, t_ref, dy_ref, acc_ref):
        @pl.when(pl.program_id(0) == 0)
        def _():
            acc_ref[...] = jnp.zeros_like(acc_ref)

        diff = y_ref[...] - t_ref[...]
        dy_ref[...] = diff * (1.0 / d)
        acc_ref[...] += jnp.sum(diff * diff, axis=0, keepdims=True)

    row = pl.BlockSpec((tm, d), lambda i: (i, 0))
    return pl.pallas_call(
        body, name="loss_grad", grid=(s_len // tm,),
        in_specs=[row, row], out_specs=[row, pl.BlockSpec((1, d), lambda i: (0, 0))],
        out_shape=[jax.ShapeDtypeStruct((s_len, d), F32), jax.ShapeDtypeStruct((1, d), F32)],
        compiler_params=pltpu.CompilerParams(dimension_semantics=("arbitrary",)),
    )(y, target)


def _rnn_fwd(u, conv_w, conv_b, wa, ba, wi, bi, lam):
    s_len, width = u.shape
    tm = min(256, s_len)
    nb = width // RNN_BLOCK

    def body(u_ref, up_ref, cw_ref, cb_ref, wa_ref, ba_ref, wi_ref, bi_ref, lam_ref,
             h_ref, uc_ref, r_ref, i_ref, a_ref, b_scr, h_carry):
        step_id = pl.program_id(0)

        @pl.when(step_id == 0)
        def _():
            h_carry[...] = jnp.zeros_like(h_carry)

        for n in range(nb):
            blk = slice(n * RNN_BLOCK, (n + 1) * RNN_BLOCK)
            ut = u_ref[:, blk]
            prev = jnp.where(step_id > 0, up_ref[:, blk], 0.0)
            uc = cb_ref[:, blk] + cw_ref[3:4, blk] * ut
            for k in (1, 2, 3):
                uc = uc + cw_ref[3 - k:4 - k, blk] * _shift_down(ut, prev, k)
            ucb = uc.astype(BF16)
            r = _sigmoid(jnp.dot(ucb, wa_ref[n], preferred_element_type=F32) + ba_ref[:, blk])
            ig = _sigmoid(jnp.dot(ucb, wi_ref[n], preferred_element_type=F32) + bi_ref[:, blk])
            log_a = -LRU_C * r * _softplus(-lam_ref[:, blk])
            uc_ref[:, blk] = uc
            r_ref[:, blk] = r
            i_ref[:, blk] = ig
            a_ref[:, blk] = jnp.exp(log_a)
            b_scr[:, blk] = jnp.sqrt(_neg_expm1(2.0 * log_a)) * (ig * uc)

        def scan(t, h):
            h = a_ref[pl.ds(t, 1), :] * h + b_scr[pl.ds(t, 1), :]
            h_ref[pl.ds(t, 1), :] = h
            return h

        h_carry[...] = lax.fori_loop(0, tm, scan, h_carry[...], unroll=8)

    row = pl.BlockSpec((tm, width), lambda i: (i, 0))
    prev8 = pl.BlockSpec((8, width), lambda i: (jnp.maximum(i * (tm // 8) - 1, 0), 0))
    vec = pl.BlockSpec((1, width), lambda i: (0, 0))
    mat = pl.BlockSpec(wa.shape, lambda i: (0, 0, 0))
    return pl.pallas_call(
        body, name="rnn_fwd", grid=(s_len // tm,),
        in_specs=[row, prev8, pl.BlockSpec(conv_w.shape, lambda i: (0, 0)), vec, mat, vec, mat, vec, vec],
        out_specs=[row] * 5,
        out_shape=[jax.ShapeDtypeStruct((s_len, width), F32)] * 5,
        scratch_shapes=[pltpu.VMEM((tm, width), F32), pltpu.VMEM((1, width), F32)],
        compiler_params=pltpu.CompilerParams(dimension_semantics=("arbitrary",)),
    )(u, u, conv_w, conv_b, wa, ba, wi, bi, lam)


def _rnn_bwd(dh, a, h, r, ig, uc, lam, wa, wi):
    s_len, width = dh.shape
    tm = min(256, s_len)
    nt = s_len // tm
    nb = width // RNN_BLOCK

    def body(dh_ref, a_ref, h_ref, hp_ref, r_ref, i_ref, uc_ref, lam_ref, wa_ref, wi_ref,
             duc_ref, dwa_ref, dwi_ref, dba_ref, dbi_ref, dlam_ref, g_scr, c_scr, dsp_scr):
        step_id = pl.program_id(0)
        tile_id = nt - 1 - step_id

        @pl.when(step_id == 0)
        def _():
            c_scr[...] = jnp.zeros_like(c_scr)
            dsp_scr[...] = jnp.zeros_like(dsp_scr)
            dwa_ref[...] = jnp.zeros_like(dwa_ref)
            dwi_ref[...] = jnp.zeros_like(dwi_ref)
            dba_ref[...] = jnp.zeros_like(dba_ref)
            dbi_ref[...] = jnp.zeros_like(dbi_ref)

        def scan(j, c):
            t = tm - 1 - j
            g = dh_ref[pl.ds(t, 1), :] + c
            g_scr[pl.ds(t, 1), :] = g
            return a_ref[pl.ds(t, 1), :] * g

        c_scr[...] = lax.fori_loop(0, tm, scan, c_scr[...], unroll=8)

        for n in range(nb):
            blk = slice(n * RNN_BLOCK, (n + 1) * RNN_BLOCK)
            g_t = g_scr[:, blk]
            hp8 = jnp.where(tile_id > 0, hp_ref[:, blk], 0.0)
            h_prev = _shift_down(h_ref[:, blk], hp8, 1)
            av, rv, iv, ucv = a_ref[:, blk], r_ref[:, blk], i_ref[:, blk], uc_ref[:, blk]
            sp = _softplus(-lam_ref[:, blk])
            mag = jnp.sqrt(_neg_expm1(-2.0 * LRU_C * rv * sp))
            d_iu = g_t * mag
            dla = g_t * h_prev * av - g_t * (iv * ucv) * (av * av) / mag
            dsp_scr[:, blk] += jnp.sum(dla * (-LRU_C * rv), axis=0, keepdims=True)
            dra = dla * (-LRU_C * sp) * rv * (1.0 - rv)
            dia = d_iu * ucv * iv * (1.0 - iv)
            drab, diab, ucb = dra.astype(BF16), dia.astype(BF16), ucv.astype(BF16)
            duc_ref[:, blk] = (d_iu * iv
                               + lax.dot_general(drab, wa_ref[n], NT_DIMS, preferred_element_type=F32)
                               + lax.dot_general(diab, wi_ref[n], NT_DIMS, preferred_element_type=F32))
            dwa_ref[n] += lax.dot_general(ucb, drab, TN_DIMS, preferred_element_type=F32)
            dwi_ref[n] += lax.dot_general(ucb, diab, TN_DIMS, preferred_element_type=F32)
            dba_ref[:, blk] += jnp.sum(dra, axis=0, keepdims=True)
            dbi_ref[:, blk] += jnp.sum(dia, axis=0, keepdims=True)

        @pl.when(step_id == nt - 1)
        def _():
            dlam_ref[...] = -dsp_scr[...] * _sigmoid(-lam_ref[...])

    row = pl.BlockSpec((tm, width), lambda i: (nt - 1 - i, 0))
    prev8 = pl.BlockSpec((8, width), lambda i: (jnp.maximum((nt - 1 - i) * (tm // 8) - 1, 0), 0))
    vec = pl.BlockSpec((1, width), lambda i: (0, 0))
    mat = pl.BlockSpec(wa.shape, lambda i: (0, 0, 0))
    return pl.pallas_call(
        body, name="rnn_bwd", grid=(nt,),
        in_specs=[row, row, row, prev8, row, row, row, vec, mat, mat],
        out_specs=[row, mat, mat, vec, vec, vec],
        out_shape=[jax.ShapeDtypeStruct((s_len, width), F32), jax.ShapeDtypeStruct(wa.shape, F32),
                   jax.ShapeDtypeStruct(wa.shape, F32)] + [jax.ShapeDtypeStruct((1, width), F32)] * 3,
        scratch_shapes=[pltpu.VMEM((tm, width), F32), pltpu.VMEM((1, width), F32), pltpu.VMEM((1, width), F32)],
        compiler_params=pltpu.CompilerParams(dimension_semantics=("arbitrary",)),
    )(dh, a, h, h, r, ig, uc, lam, wa, wi)


def _conv_bwd(duc, u, conv_w):
    s_len, width = duc.shape
    tm = min(256, s_len)
    nt = s_len // tm

    def body(d_ref, dn_ref, u_ref, up_ref, cw_ref, du_ref, dcw_ref, dcb_ref):
        step_id = pl.program_id(0)

        @pl.when(step_id == 0)
        def _():
            dcw_ref[...] = jnp.zeros_like(dcw_ref)
            dcb_ref[...] = jnp.zeros_like(dcb_ref)

        for n in range(width // RNN_BLOCK):
            blk = slice(n * RNN_BLOCK, (n + 1) * RNN_BLOCK)
            dt = d_ref[:, blk]
            ut = u_ref[:, blk]
            nxt = jnp.where(step_id < nt - 1, dn_ref[:, blk], 0.0)
            prev = jnp.where(step_id > 0, up_ref[:, blk], 0.0)
            du = cw_ref[3:4, blk] * dt
            dcw_ref[3:4, blk] += jnp.sum(dt * ut, axis=0, keepdims=True)
            for k in (1, 2, 3):
                du = du + cw_ref[3 - k:4 - k, blk] * _shift_up(dt, nxt, k)
                dcw_ref[3 - k:4 - k, blk] += jnp.sum(dt * _shift_down(ut, prev, k), axis=0, keepdims=True)
            du_ref[:, blk] = du.astype(BF16)
            dcb_ref[:, blk] += jnp.sum(dt, axis=0, keepdims=True)

    row = pl.BlockSpec((tm, width), lambda i: (i, 0))
    prev8 = pl.BlockSpec((8, width), lambda i: (jnp.maximum(i * (tm // 8) - 1, 0), 0))
    next8 = pl.BlockSpec((8, width), lambda i: (jnp.minimum((i + 1) * (tm // 8), s_len // 8 - 1), 0))
    return pl.pallas_call(
        body, name="conv_bwd", grid=(nt,),
        in_specs=[row, next8, row, prev8, pl.BlockSpec(conv_w.shape, lambda i: (0, 0))],
        out_specs=[row, pl.BlockSpec(conv_w.shape, lambda i: (0, 0)), pl.BlockSpec((1, width), lambda i: (0, 0))],
        out_shape=[jax.ShapeDtypeStruct((s_len, width), BF16), jax.ShapeDtypeStruct(conv_w.shape, F32),
                   jax.ShapeDtypeStruct((1, width), F32)],
        compiler_params=pltpu.CompilerParams(dimension_semantics=("arbitrary",)),
    )(duc, duc, u, u, conv_w)


def _pair_sum(core, grads, theirs):
    n, _, half, _ = grads.shape
    tb = 128

    def body(c_ref, a_ref, b_ref, o_ref):
        o_ref[...] = a_ref[...] + b_ref[...]

    blk = pl.BlockSpec((n, tb, LANES), lambda i, c: (0, i, 0))
    return pl.pallas_call(
        body, name="pair_sum",
        grid_spec=pltpu.PrefetchScalarGridSpec(
            num_scalar_prefetch=1, grid=(half // tb,),
            in_specs=[pl.BlockSpec((n, None, tb, LANES), lambda i, c: (0, c[0], i, 0)), blk], out_specs=blk),
        out_shape=jax.ShapeDtypeStruct((n, half, LANES), F32),
        compiler_params=pltpu.CompilerParams(dimension_semantics=("parallel",)),
    )(core, grads, theirs)


def _sum_chips(parts):
    rows = parts.shape[1]
    tb = 128

    def body(p_ref, o_ref):
        o_ref[...] = ((p_ref[0] + p_ref[1]) + p_ref[2]) + p_ref[3]

    return pl.pallas_call(
        body, name="chip_sum", grid=(rows // tb,),
        in_specs=[pl.BlockSpec((N_CHIPS, tb, LANES), lambda i: (0, i, 0))],
        out_specs=pl.BlockSpec((tb, LANES), lambda i: (i, 0)),
        out_shape=jax.ShapeDtypeStruct((rows, LANES), F32),
        compiler_params=pltpu.CompilerParams(dimension_semantics=("parallel",)),
    )(parts)


def _adamw(w, g, m, v, rows_per_block):
    n, rows, cols = w.shape
    blk = pl.BlockSpec((1, rows_per_block, cols), lambda i, j: (i, j, 0))

    def body(w_ref, g_ref, m_ref, v_ref, d_ref, nm_ref, nv_ref):
        gt = g_ref[...]
        nm = ADAM_B1 * m_ref[...] + (1.0 - ADAM_B1) * gt
        nv = ADAM_B2 * v_ref[...] + (1.0 - ADAM_B2) * (gt * gt)
        m_hat = nm / (1.0 - ADAM_B1 ** ADAM_STEP)
        v_hat = nv / (1.0 - ADAM_B2 ** ADAM_STEP)
        d_ref[...] = -ADAM_LR * (m_hat / (jnp.sqrt(v_hat) + ADAM_EPS) + ADAM_WD * w_ref[...])
        nm_ref[...] = nm
        nv_ref[...] = nv

    return pl.pallas_call(
        body, name="adamw", grid=(n, rows // rows_per_block), in_specs=[blk] * 4, out_specs=[blk] * 3,
        out_shape=[jax.ShapeDtypeStruct(w.shape, F32)] * 3,
        compiler_params=pltpu.CompilerParams(dimension_semantics=("parallel", "parallel")),
    )(w, g, m, v)


def _place():
    x, y, c = lax.axis_index("x"), lax.axis_index("y"), lax.axis_index("c")
    return x, y, c, [(1 - x, y), (x, 1 - y), (1 - x, 1 - y)]


ANY = pl.BlockSpec(memory_space=pl.ANY)
COPY_PARTS = 4


def _remote_parts(src_of, dst_of, rows, send_sem, recv_sem, to):
    step = rows // COPY_PARTS
    for i in range(COPY_PARTS):
        pltpu.make_async_remote_copy(src_ref=src_of(i * step, step), dst_ref=dst_of(i * step, step),
                                     send_sem=send_sem, recv_sem=recv_sem, device_id=to, device_id_type=MESH).start()
    return pltpu.make_async_remote_copy(src_ref=src_of(0, rows), dst_ref=dst_of(0, rows), send_sem=send_sem,
                                        recv_sem=recv_sem, device_id=to, device_id_type=MESH)


def _local_parts(src_of, dst_of, rows, sem):
    step = rows // COPY_PARTS
    for i in range(COPY_PARTS):
        pltpu.make_async_copy(src_of(i * step, step), dst_of(i * step, step), sem).start()
    return pltpu.make_async_copy(src_of(0, rows), dst_of(0, rows), sem)


def _gather_chips(shard):
    rows = shard.shape[0]
    half = rows // 2

    def body(p_ref, out_ref, send_sems, recv_sems, local_sem):
        x, y, c, chips = _place()
        me = 2 * x + y

        def rows_of(chip, hc):
            return lambda r0, n: out_ref.at[chip, pl.ds(hc * half + r0, n), :]

        def mine_half(r0, n):
            return p_ref.at[pl.ds(c * half + r0, n), :]

        mine = _local_parts(lambda r0, n: p_ref.at[pl.ds(r0, n), :], lambda r0, n: out_ref.at[me, pl.ds(r0, n), :],
                            rows, local_sem)
        first = [_remote_parts(mine_half, rows_of(me, c), half, send_sems.at[j], recv_sems.at[j], (cx, cy, c))
                 for j, (cx, cy) in enumerate(chips)]
        passed = []
        for j, (cx, cy) in enumerate(chips):
            landed = rows_of(2 * cx + cy, c)
            pltpu.make_async_remote_copy(src_ref=landed(0, half), dst_ref=landed(0, half), send_sem=send_sems.at[j],
                                         recv_sem=recv_sems.at[j], device_id=(x, y, c), device_id_type=MESH).wait_recv()
            passed.append(_remote_parts(landed, landed, half, send_sems.at[3 + j], recv_sems.at[3 + j], (x, y, 1 - c)))
        for j, (cx, cy) in enumerate(chips):
            other = rows_of(2 * cx + cy, 1 - c)(0, half)
            pltpu.make_async_remote_copy(src_ref=other, dst_ref=other, send_sem=send_sems.at[3 + j],
                                         recv_sem=recv_sems.at[3 + j], device_id=(x, y, c), device_id_type=MESH).wait_recv()
        for cp in first + passed:
            cp.wait_send()
        mine.wait()

    return pl.pallas_call(
        body, name="gather_chips", in_specs=[ANY], out_specs=ANY,
        out_shape=jax.ShapeDtypeStruct((N_CHIPS,) + shard.shape, shard.dtype),
        scratch_shapes=[pltpu.SemaphoreType.DMA((6,)), pltpu.SemaphoreType.DMA((6,)), pltpu.SemaphoreType.DMA],
    )(shard)


def _swap_halves(grads):
    n, _, half, lanes = grads.shape

    def body(g_ref, theirs_ref, send_sems, recv_sems):
        x, y, c, _ = _place()
        gives = [_remote_parts(lambda r0, m, j=j: g_ref.at[j, 1 - c, pl.ds(r0, m), :],
                               lambda r0, m, j=j: theirs_ref.at[j, pl.ds(r0, m), :],
                               half, send_sems.at[j], recv_sems.at[j], (x, y, 1 - c)) for j in range(n)]
        for give in gives:
            give.wait()

    return pl.pallas_call(
        body, name="swap_halves", in_specs=[ANY], out_specs=ANY,
        out_shape=jax.ShapeDtypeStruct((n, half, lanes), grads.dtype),
        scratch_shapes=[pltpu.SemaphoreType.DMA((n,)), pltpu.SemaphoreType.DMA((n,))],
    )(grads)


def _scatter_chips(parts):
    rows = parts.shape[1]

    def body(t_ref, out_ref, send_sems, recv_sems, local_sem):
        x, y, c, chips = _place()
        me = 2 * x + y

        def slot(ref, chip):
            return lambda r0, n: ref.at[chip, pl.ds(r0, n), :]

        mine = _local_parts(slot(t_ref, me), slot(out_ref, me), rows, local_sem)
        sends = [_remote_parts(slot(t_ref, 2 * cx + cy), slot(out_ref, me), rows, send_sems.at[j], recv_sems.at[j],
                               (cx, cy, c)) for j, (cx, cy) in enumerate(chips)]
        for j, (cx, cy) in enumerate(chips):
            slot = out_ref.at[2 * cx + cy]
            pltpu.make_async_remote_copy(src_ref=slot, dst_ref=slot, send_sem=send_sems.at[j],
                                         recv_sem=recv_sems.at[j], device_id=(x, y, c), device_id_type=MESH).wait_recv()
        for cp in sends:
            cp.wait_send()
        mine.wait()

    return pl.pallas_call(
        body, name="scatter_chips", in_specs=[ANY], out_specs=ANY,
        out_shape=jax.ShapeDtypeStruct(parts.shape, parts.dtype),
        scratch_shapes=[pltpu.SemaphoreType.DMA((3,)), pltpu.SemaphoreType.DMA((3,)), pltpu.SemaphoreType.DMA],
    )(parts)


def _join_halves(mine):
    half, lanes = mine.shape

    def body(h_ref, out_ref, send_sem, recv_sem, local_sem):
        x, y, c, _ = _place()

        def src(r0, n):
            return h_ref.at[pl.ds(r0, n), :]

        def dst(r0, n):
            return out_ref.at[pl.ds(c * half + r0, n), :]

        keep = _local_parts(src, dst, half, local_sem)
        give = _remote_parts(src, dst, half, send_sem, recv_sem, (x, y, 1 - c))
        give.wait_send()
        theirs = out_ref.at[pl.ds((1 - c) * half, half), :]
        pltpu.make_async_remote_copy(src_ref=theirs, dst_ref=theirs, send_sem=send_sem, recv_sem=recv_sem,
                                     device_id=(x, y, c), device_id_type=MESH).wait_recv()
        keep.wait()

    return pl.pallas_call(
        body, name="join_halves", in_specs=[ANY], out_specs=ANY,
        out_shape=jax.ShapeDtypeStruct((2 * half, lanes), mine.dtype),
        scratch_shapes=[pltpu.SemaphoreType.DMA, pltpu.SemaphoreType.DMA, pltpu.SemaphoreType.DMA],
    )(mine)


def _pack(arrays, dtype, row_align):
    flat = []
    for arr in arrays:
        v = arr.reshape(-1)
        flat.append(jnp.pad(v, (0, (-v.shape[0]) % LANES)))
    total = sum(f.shape[0] for f in flat) // LANES
    pad_rows = (-total) % row_align
    if pad_rows:
        flat.append(jnp.zeros((pad_rows * LANES,), dtype))
    return jnp.concatenate(flat).reshape(-1, LANES)


def _unpack(buf, shapes):
    lead = buf.shape[:-2]
    flat = buf.reshape(lead + (-1,))
    out, off = [], 0
    for shape in shapes:
        size = 1
        for dim in shape:
            size *= dim
        out.append(flat[..., off:off + size].reshape(lead + tuple(shape)))
        off += size + (-size) % LANES
    return out


def _from_chips(parts, axis):
    return jnp.concatenate([parts[j] for j in range(N_CHIPS)], axis=axis)


def kernel(x, ln_g, ln_b, attn_w_in, attn_b_f, attn_w_out, rnn_w_in, rnn_conv_w, rnn_conv_b, rnn_w_a, rnn_b_a, rnn_w_i, rnn_b_i, rnn_lambda, rnn_w_out, loss_target, m_ln_g, m_ln_b, m_attn_w_in, m_attn_b_f, m_attn_w_out, m_rnn_w_in, m_rnn_conv_w, m_rnn_conv_b, m_rnn_w_a, m_rnn_b_a, m_rnn_w_i, m_rnn_b_i, m_rnn_lambda, m_rnn_w_out, v_ln_g, v_ln_b, v_attn_w_in, v_attn_b_f, v_attn_w_out, v_rnn_w_in, v_rnn_conv_w, v_rnn_conv_b, v_rnn_w_a, v_rnn_b_a, v_rnn_w_i, v_rnn_b_i, v_rnn_lambda, v_rnn_w_out):
    s_len, d = x.shape[1], x.shape[2]
    xs = x.reshape(s_len, d)
    target = loss_target.reshape(s_len, d)
    tile = _att_tile(s_len)
    nt = s_len // tile
    heads = attn_b_f.shape[1]
    qkvg = attn_w_in.shape[2] * N_CHIPS - heads

    big = [attn_w_in, attn_w_out, rnn_w_in, rnn_w_a, rnn_w_i, rnn_w_out]
    small = [rnn_conv_w, rnn_conv_b, rnn_b_a, rnn_b_i, rnn_lambda]
    packed = _pack([w.astype(BF16) for w in big] + [lax.bitcast_convert_type(w, BF16) for w in small], BF16,
                   ROW_ALIGN)
    gathered = _gather_chips(packed)
    parts = _unpack(gathered, [w.shape for w in big] + [w.shape + (2,) for w in small])
    w_in_a, w_out_a, w_in_r, w_a, w_i, w_out_r = [
        _from_chips(p, ax) for p, ax in zip(parts[:6], (2, 1, 2, 2, 2, 1))]
    conv_w, conv_b, b_a, b_i, lam = [
        _from_chips(lax.bitcast_convert_type(p, F32), ax) for p, ax in zip(parts[6:], (2, 1, 1, 1, 1))]
    w_cat = jnp.concatenate(
        [w_in_a[:, :, :d] * (HEAD_DIM ** -0.5), w_in_a[:, :, d:qkvg],
         jnp.pad(w_in_a[:, :, qkvg:], ((0, 0), (0, 0), (0, 128 - heads)))], axis=2).astype(BF16)
    b_f = jnp.pad(attn_b_f, ((0, 0), (0, 128 - heads)))

    saved = []
    cur = xs
    for layer in range(DEPTH):
        idx = layer // 2
        g_l, b_l = ln_g[layer:layer + 1], ln_b[layer:layer + 1]
        if layer % 2 == 0:
            q, k, v, gate, fl = _proj(cur, w_cat[idx], [(0, d, BF16), (d, d, BF16), (2 * d, d, BF16),
                                                         (3 * d, d, F32), (4 * d, 128, F32)], "attn_proj")
            cum, sneg = _fcum(fl, b_f[idx:idx + 1])
            cumr = cum.reshape(heads * nt, tile)
            o, lse = _flash_fwd(q, k, v, cumr, tile)
            nxt, xh, rs, yg = _out_ln(o, gate, cur, w_out_a[idx], g_l, b_l, "out_ln")
            saved.append(dict(x=cur, q=q, k=k, v=v, gate=gate, cumr=cumr, sneg=sneg, a=o, lse=lse, xh=xh, rs=rs, yg=yg))
        else:
            u, gate = _proj(cur, w_in_r[idx], [(0, d, F32), (d, d, F32)], "rnn_proj")
            vecs = [t[idx:idx + 1] for t in (conv_b, b_a, b_i, lam)]
            hseq, uc, r, ig, a = _rnn_fwd(u, conv_w[idx], vecs[0], w_a[idx], vecs[1], w_i[idx], vecs[2], vecs[3])
            nxt, xh, rs, yg = _out_ln(hseq, gate, cur, w_out_r[idx], g_l, b_l, "out_ln")
            saved.append(dict(x=cur, u=u, gate=gate, uc=uc, r=r, ig=ig, av=a, a=hseq, xh=xh, rs=rs, yg=yg, lam=vecs[3]))
        cur = nxt

    dout, sq = _loss_grad(cur, target)
    loss = lax.psum(0.5 * jnp.sum(sq) / d, ("x", "y", "c"))

    g_ln_g, g_ln_b = [None] * DEPTH, [None] * DEPTH
    g_attn = [None] * 2
    g_rnn = [None] * 2
    for layer in reversed(range(DEPTH)):
        idx = layer // 2
        sv = saved[layer]
        w_out = w_out_a[idx] if layer % 2 == 0 else w_out_r[idx]
        dz, da, dgate, dg, db = _ln_bwd(dout, sv["xh"], sv["rs"], ln_g[layer:layer + 1], w_out, sv["gate"], sv["a"],
                                        "ln_bwd")
        g_ln_g[layer], g_ln_b[layer] = dg, db
        d_w_out = _mm_tn(sv["yg"], dz, "dw_out")
        if layer % 2 == 0:
            dq, dk, dv, dcumr = _flash_bwd(sv["q"], sv["k"], sv["v"], sv["a"], da, sv["lse"], sv["cumr"], tile)
            dlogit, dbf = _fbwd(dcumr.reshape(heads, s_len), sv["sneg"])
            pieces = [dq, dk, dv, dgate, dlogit]
            dout = _mm_nt(dz, [(p, j * d) for j, p in enumerate(pieces)], w_cat[idx], "attn_dx")
            dws = [_mm_tn(sv["x"], p, "dw_in") for p in pieces[:4]] + [_mm_tn(sv["x"], dlogit, "dw_f")]
            dws[0] = dws[0] * (HEAD_DIM ** -0.5)
            g_attn[idx] = dict(w_in=dws, b_f=dbf[:, 0], w_out=d_w_out)
        else:
            duc, d_wa, d_wi, d_ba, d_bi, d_lam = _rnn_bwd(da, sv["av"], sv["a"], sv["r"], sv["ig"], sv["uc"], sv["lam"],
                                                          w_a[idx], w_i[idx])
            du, d_cw, d_cb = _conv_bwd(duc, sv["u"], conv_w[idx])
            dout = _mm_nt(dz, [(du, 0), (dgate, d)], w_in_r[idx], "rnn_dx")
            d_w_in = [_mm_tn(sv["x"], du, "dw_in"), _mm_tn(sv["x"], dgate, "dw_in")]
            g_rnn[idx] = dict(w_in=d_w_in, conv_w=d_cw, conv_b=d_cb[0], w_a=d_wa, b_a=d_ba[0], w_i=d_wi, b_i=d_bi[0],
                              lam=d_lam[0], w_out=d_w_out)
    grad_x = dout.reshape(x.shape)

    def both(items, key, cut=lambda t: t):
        return [cut(it[key]) for it in items]

    def rows_of(j, n):
        return lambda t: t[j * n:(j + 1) * n]

    def cols_of(j, n):
        return lambda t: t[..., j * n:(j + 1) * n]

    rb = RNN_BLOCK // N_CHIPS
    edges = jnp.stack([jnp.stack([it["w_in"][p][:, :heads] for it in g_attn]) for p in (1, 2, 3, 4)])
    everywhere = [jnp.concatenate(g_ln_g), jnp.concatenate(g_ln_b), jnp.stack(both(g_attn, "b_f")), edges]

    def for_chip(j):
        return (both(g_attn, "w_in", lambda t: t[j])
                + both(g_attn, "w_out", rows_of(j, d // N_CHIPS))
                + both(g_rnn, "w_in", lambda t: cols_of(j % 2, d // 2)(t[j // 2]))
                + [jnp.stack(both(g_rnn, "conv_w", cols_of(j, d // N_CHIPS))),
                   jnp.stack(both(g_rnn, "conv_b", cols_of(j, d // N_CHIPS)))]
                + both(g_rnn, "w_a", lambda t: t[:, j * rb:(j + 1) * rb])
                + [jnp.stack(both(g_rnn, "b_a", cols_of(j, d // N_CHIPS)))]
                + both(g_rnn, "w_i", lambda t: t[:, j * rb:(j + 1) * rb])
                + [jnp.stack(both(g_rnn, "b_i", cols_of(j, d // N_CHIPS))),
                   jnp.stack(both(g_rnn, "lam", cols_of(j, d // N_CHIPS)))]
                + both(g_rnn, "w_out", rows_of(j, d // N_CHIPS))
                + everywhere)

    per_chip = jnp.stack([_pack(for_chip(j), F32, ROW_ALIGN) for j in range(N_CHIPS)])
    rows = per_chip.shape[1]
    per_chip = per_chip.reshape(N_CHIPS, 2, rows // 2, LANES)

    core = lax.axis_index("c").astype(jnp.int32).reshape(1)
    theirs = _swap_halves(per_chip)
    from_chips = _scatter_chips(_pair_sum(core, per_chip, theirs))
    reduced = _join_halves(_sum_chips(from_chips))
    (g_in_group, g_w_out_a, g_w_in_r, g_conv_w, g_conv_b, g_w_a, g_b_a, g_w_i, g_b_i, g_lam, g_w_out_r,
     g_ln_g_sum, g_ln_b_sum, g_b_f, g_edges) = _unpack(reduced, [
        (2, d, d), attn_w_out.shape, rnn_w_in.shape, rnn_conv_w.shape, rnn_conv_b.shape, rnn_w_a.shape,
        rnn_b_a.shape, rnn_w_i.shape, rnn_b_i.shape, rnn_lambda.shape, rnn_w_out.shape, ln_g.shape, ln_b.shape,
        attn_b_f.shape, (N_CHIPS, 2, d, heads)])
    me = 2 * lax.axis_index("x") + lax.axis_index("y")
    wide = jnp.concatenate([g_in_group, lax.dynamic_index_in_dim(g_edges, me, 0, keepdims=False)], axis=2)
    g_w_in_a = lax.dynamic_slice(wide, (0, 0, (heads // N_CHIPS) * me), attn_w_in.shape)

    def adam_nd(w, g, m, v, view, rows_per_block):
        outs = _adamw(w.reshape(view), g.reshape(view), m.reshape(view), v.reshape(view), rows_per_block)
        return [t.reshape(w.shape) for t in outs]

    upd = {
        "attn_w_in": adam_nd(attn_w_in, g_w_in_a, m_attn_w_in, v_attn_w_in, attn_w_in.shape, 256),
        "attn_w_out": adam_nd(attn_w_out, g_w_out_a, m_attn_w_out, v_attn_w_out, attn_w_out.shape, 256),
        "rnn_w_in": adam_nd(rnn_w_in, g_w_in_r, m_rnn_w_in, v_rnn_w_in, rnn_w_in.shape, 512),
        "rnn_w_a": adam_nd(rnn_w_a, g_w_a, m_rnn_w_a, v_rnn_w_a, (1, -1, RNN_BLOCK), 512),
        "rnn_w_i": adam_nd(rnn_w_i, g_w_i, m_rnn_w_i, v_rnn_w_i, (1, -1, RNN_BLOCK), 512),
        "rnn_w_out": adam_nd(rnn_w_out, g_w_out_r, m_rnn_w_out, v_rnn_w_out, rnn_w_out.shape, 256),
    }
    smalls = [rnn_conv_w, rnn_conv_b, rnn_b_a, rnn_b_i, rnn_lambda, ln_g, ln_b, attn_b_f]
    g_small = [g_conv_w, g_conv_b, g_b_a, g_b_i, g_lam, g_ln_g_sum, g_ln_b_sum, g_b_f]
    m_small = [m_rnn_conv_w, m_rnn_conv_b, m_rnn_b_a, m_rnn_b_i, m_rnn_lambda, m_ln_g, m_ln_b, m_attn_b_f]
    v_small = [v_rnn_conv_w, v_rnn_conv_b, v_rnn_b_a, v_rnn_b_i, v_rnn_lambda, v_ln_g, v_ln_b, v_attn_b_f]
    packs = [_pack(t, F32, 16)[None] for t in (smalls, g_small, m_small, v_small)]
    small_out = [_unpack(t[0], [w.shape for w in smalls]) for t in _adamw(*packs, 16)]
    for k, name in enumerate(("rnn_conv_w", "rnn_conv_b", "rnn_b_a", "rnn_b_i", "rnn_lambda", "ln_g", "ln_b",
                              "attn_b_f")):
        upd[name] = [small_out[0][k], small_out[1][k], small_out[2][k]]
    grad = {"attn_w_in": g_w_in_a, "attn_w_out": g_w_out_a, "rnn_w_in": g_w_in_r, "rnn_w_a": g_w_a, "rnn_w_i": g_w_i,
            "rnn_w_out": g_w_out_r, "rnn_conv_w": g_conv_w, "rnn_conv_b": g_conv_b, "rnn_b_a": g_b_a,
            "rnn_b_i": g_b_i, "rnn_lambda": g_lam, "ln_g": g_ln_g_sum, "ln_b": g_ln_b_sum, "attn_b_f": g_b_f}
    names = ("ln_g", "ln_b", "attn_w_in", "attn_b_f", "attn_w_out", "rnn_w_in", "rnn_conv_w", "rnn_conv_b",
             "rnn_w_a", "rnn_b_a", "rnn_w_i", "rnn_b_i", "rnn_lambda", "rnn_w_out")
    return (loss, grad_x, *[grad[n] for n in names], *[upd[n][0] for n in names], *[upd[n][1] for n in names],
            *[upd[n][2] for n in names])
```

```python
import functools

import jax
import jax.numpy as jnp
from jax import lax
from jax.experimental import pallas as pl
from jax.experimental.pallas import tpu as pltpu

F32 = jnp.float32
BF16 = jnp.bfloat16
MESH = pl.DeviceIdType.MESH

DEPTH = 4
HEAD_DIM = 64
HEAD_PAIR = 2 * HEAD_DIM
RNN_BLOCK = 256
CONV_WIDTH = 4
LRU_C = 8.0
ALPHA = (2.0 * DEPTH) ** 0.25
LN_EPS = 1e-5
ADAM_LR, ADAM_B1, ADAM_B2, ADAM_EPS, ADAM_WD, ADAM_STEP = 0.001, 0.9, 0.999, 1e-08, 0.01, 10
N_CHIPS = 4
LANES = 1024
ROW_ALIGN = 256
NEG = -1e30

NT_DIMS = (((1,), (1,)), ((), ()))
TN_DIMS = (((0,), (0,)), ((), ()))


def _sigmoid(z):
    return 1.0 / (1.0 + jnp.exp(-z))


def _softplus(z):
    return jnp.maximum(z, 0.0) + jnp.log(1.0 + jnp.exp(-jnp.abs(z)))


def _neg_expm1(y):
    poly = y * (1.0 + y * (0.5 + y * (1.0 / 6.0 + y * (1.0 / 24.0))))
    return -jnp.where(y > -0.05, poly, jnp.exp(y) - 1.0)


def _shift_down(cur, prev8, k):
    n = cur.shape[0]
    row = lax.broadcasted_iota(jnp.int32, cur.shape, 0)
    fix = jnp.tile(pltpu.roll(prev8, k, 0), (n // 8, 1))
    return jnp.where(row < k, fix, pltpu.roll(cur, k, 0))


def _shift_up(cur, next8, k):
    n = cur.shape[0]
    row = lax.broadcasted_iota(jnp.int32, cur.shape, 0)
    fix = jnp.tile(pltpu.roll(next8, 8 - k, 0), (n // 8, 1))
    return jnp.where(row >= n - k, fix, pltpu.roll(cur, n - k, 0))


def _proj(x, w, outs, name):
    s_len, d = x.shape
    tm = min(512, s_len)

    def body(x_ref, w_ref, *o_refs):
        xb = x_ref[...].astype(BF16)
        for (c0, wd, dt), o_ref in zip(outs, o_refs):
            step = min(wd, 512)
            for cc in range(0, wd, step):
                o_ref[:, cc:cc + step] = jnp.dot(
                    xb, w_ref[:, c0 + cc:c0 + cc + step], preferred_element_type=F32).astype(dt)

    return pl.pallas_call(
        body, name=name, grid=(s_len // tm,),
        in_specs=[pl.BlockSpec((tm, d), lambda i: (i, 0)), pl.BlockSpec(w.shape, lambda i: (0, 0))],
        out_specs=[pl.BlockSpec((tm, wd), lambda i: (i, 0)) for _, wd, _ in outs],
        out_shape=[jax.ShapeDtypeStruct((s_len, wd), dt) for _, wd, dt in outs],
        compiler_params=pltpu.CompilerParams(dimension_semantics=("parallel",)),
    )(x, w)


def _mm_nt(dz, pieces, w, name):
    s_len, d = dz.shape
    tm = min(256, s_len)
    n = len(pieces)
    cols = [(c0, p.shape[1]) for p, c0 in pieces]

    def body(dz_ref, *rest):
        w_ref, o_ref = rest[n], rest[n + 1]
        acc = ALPHA * dz_ref[...]
        for (c0, wd), p_ref in zip(cols, rest[:n]):
            acc = acc + lax.dot_general(p_ref[...], w_ref[:, c0:c0 + wd], NT_DIMS, preferred_element_type=F32)
        o_ref[...] = acc

    return pl.pallas_call(
        body, name=name, grid=(s_len // tm,),
        in_specs=[pl.BlockSpec((tm, d), lambda i: (i, 0))]
        + [pl.BlockSpec((tm, wd), lambda i: (i, 0)) for _, wd in cols]
        + [pl.BlockSpec(w.shape, lambda i: (0, 0))],
        out_specs=pl.BlockSpec((tm, d), lambda i: (i, 0)),
        out_shape=jax.ShapeDtypeStruct((s_len, d), F32),
        compiler_params=pltpu.CompilerParams(dimension_semantics=("parallel",)),
    )(dz, *[p for p, _ in pieces], w)


def _mm_tn(a, b, name):
    s_len, m = a.shape
    n = b.shape[1]
    tn = min(n, 512)
    ts = min(s_len, 512)

    def body(a_ref, b_ref, o_ref):
        @pl.when(pl.program_id(1) == 0)
        def _():
            o_ref[...] = jnp.zeros_like(o_ref)

        o_ref[...] += lax.dot_general(a_ref[...].astype(BF16), b_ref[...].astype(BF16), TN_DIMS,
                                      preferred_element_type=F32)

    return pl.pallas_call(
        body, name=name, grid=(n // tn, s_len // ts),
        in_specs=[pl.BlockSpec((ts, m), lambda j, s: (s, 0)), pl.BlockSpec((ts, tn), lambda j, s: (s, j))],
        out_specs=pl.BlockSpec((m, tn), lambda j, s: (0, j)),
        out_shape=jax.ShapeDtypeStruct((m, n), F32),
        compiler_params=pltpu.CompilerParams(dimension_semantics=("parallel", "arbitrary")),
    )(a, b)


def _fcum(fl, bias):
    s_len = fl.shape[0]

    def body(fl_ref, b_ref, cum_ref, sg_ref):
        z = fl_ref[...] + b_ref[...]
        e = jnp.exp(-jnp.abs(z))
        logf = jnp.minimum(z, 0.0) - jnp.log(1.0 + e)
        sneg = jnp.where(z >= 0, e, 1.0) / (1.0 + e)
        run = logf.T[0:16, :]
        sg_ref[...] = sneg.T[0:16, :]
        lane = lax.broadcasted_iota(jnp.int32, (16, s_len), 1)
        sh = 1
        while sh < s_len:
            run = run + jnp.where(lane >= sh, pltpu.roll(run, sh, 1), 0.0)
            sh *= 2
        cum_ref[...] = run

    return pl.pallas_call(
        body, name="fcum",
        out_shape=[jax.ShapeDtypeStruct((16, s_len), F32), jax.ShapeDtypeStruct((16, s_len), F32)],
    )(fl, bias)


def _fbwd(dcum, sneg):
    s_len = dcum.shape[1]

    def body(dc_ref, sg_ref, dl_ref, db_ref):
        run = dc_ref[...]
        lane = lax.broadcasted_iota(jnp.int32, (16, s_len), 1)
        sh = 1
        while sh < s_len:
            run = run + jnp.where(lane < s_len - sh, pltpu.roll(run, s_len - sh, 1), 0.0)
            sh *= 2
        dlog = run * sg_ref[...]
        db_ref[...] = jnp.broadcast_to(jnp.sum(dlog, axis=1, keepdims=True), (16, 128))
        full = jnp.concatenate([dlog, jnp.zeros((112, s_len), F32)], axis=0)
        dl_ref[...] = full.T.astype(BF16)

    return pl.pallas_call(
        body, name="fbwd",
        out_shape=[jax.ShapeDtypeStruct((s_len, 128), BF16), jax.ShapeDtypeStruct((16, 128), F32)],
    )(dcum, sneg)


def _att_tile(s_len):
    return 256 if s_len >= 512 else s_len // 2


def _flash_fwd(q, k, v, cumr, tile):
    s_len, width = q.shape
    nt = s_len // tile

    def body(q_ref, k_ref, v_ref, cum_ref, o_ref, lse_ref):
        pid = pl.program_id(0)
        is0 = lax.broadcasted_iota(jnp.int32, (tile, HEAD_PAIR), 1) < HEAD_DIM
        causal = (lax.broadcasted_iota(jnp.int32, (tile, tile), 0)
                  >= lax.broadcasted_iota(jnp.int32, (tile, tile), 1))

        def q_loop(qi, carry):
            q0 = pl.multiple_of(qi * tile, tile)
            qt = q_ref[pl.ds(q0, tile), :]
            zq = jnp.zeros_like(qt)
            res = []
            for h in (0, 1):
                qm = jnp.where(is0, qt, zq) if h == 0 else jnp.where(is0, zq, qt)
                hrow = (2 * pid + h) * nt

                def step(kj, m, l, acc, masked, qm=qm, hrow=hrow):
                    k0 = pl.multiple_of(kj * tile, tile)
                    s = lax.dot_general(qm, k_ref[pl.ds(k0, tile), :], NT_DIMS, preferred_element_type=F32)
                    s = s - cum_ref[pl.ds(hrow + kj, 1), :]
                    if masked:
                        s = jnp.where(causal, s, NEG)
                    m_new = jnp.maximum(m, jnp.max(s, axis=1, keepdims=True))
                    p = jnp.exp(s - m_new)
                    scale = jnp.exp(m - m_new)
                    l = scale * l + jnp.sum(p, axis=1, keepdims=True)
                    acc = scale * acc + jnp.dot(p.astype(BF16), v_ref[pl.ds(k0, tile), :],
                                                preferred_element_type=F32)
                    return m_new, l, acc

                init = (jnp.full((tile, 1), NEG, F32), jnp.zeros((tile, 1), F32), jnp.zeros((tile, HEAD_PAIR), F32))
                state = step(qi, *init, True)
                m, l, acc = lax.fori_loop(0, qi, lambda kj, c, step=step: step(kj, *c, False), state)
                res.append((acc / l, m + jnp.log(l)))
            o_ref[pl.ds(q0, tile), :] = jnp.where(is0, res[0][0], res[1][0])
            lse_ref[pl.ds(q0, tile), :] = jnp.where(is0, res[0][1], res[1][1])
            return carry

        lax.fori_loop(0, nt, q_loop, 0)

    blk = pl.BlockSpec((s_len, HEAD_PAIR), lambda p: (0, p))
    return pl.pallas_call(
        body, name="flash_fwd", grid=(width // HEAD_PAIR,),
        in_specs=[blk, blk, blk, pl.BlockSpec(cumr.shape, lambda p: (0, 0))],
        out_specs=[blk, blk],
        out_shape=[jax.ShapeDtypeStruct((s_len, width), F32), jax.ShapeDtypeStruct((s_len, width), F32)],
        compiler_params=pltpu.CompilerParams(dimension_semantics=("parallel",)),
    )(q, k, v, cumr)


def _flash_bwd(q, k, v, o, do, lse, cumr, tile):
    s_len, width = q.shape
    nt = s_len // tile

    def body(q_ref, k_ref, v_ref, o_ref, do_ref, lse_ref, cum_ref, dq_ref, dk_ref, dv_ref, dcum_ref,
             dq_acc, delta, do_bf, row_sum):
        pid = pl.program_id(0)
        is0 = lax.broadcasted_iota(jnp.int32, (tile, HEAD_PAIR), 1) < HEAD_DIM
        causal = (lax.broadcasted_iota(jnp.int32, (tile, tile), 0)
                  >= lax.broadcasted_iota(jnp.int32, (tile, tile), 1))

        def pre(i, carry):
            r0 = pl.multiple_of(i * tile, tile)
            d_o = do_ref[pl.ds(r0, tile), :]
            prod = d_o * o_ref[pl.ds(r0, tile), :]
            d0 = jnp.sum(jnp.where(is0, prod, 0.0), axis=1, keepdims=True)
            d1 = jnp.sum(jnp.where(is0, 0.0, prod), axis=1, keepdims=True)
            delta[pl.ds(r0, tile), :] = jnp.where(is0, d0, d1)
            do_bf[pl.ds(r0, tile), :] = d_o.astype(BF16)
            dq_acc[pl.ds(r0, tile), :] = jnp.zeros((tile, HEAD_PAIR), F32)
            row_sum[pl.ds(r0, tile), :] = jnp.zeros((tile, HEAD_PAIR), F32)
            return carry

        lax.fori_loop(0, nt, pre, 0)

        def kv_loop(kj, carry):
            k0 = pl.multiple_of(kj * tile, tile)
            kt = k_ref[pl.ds(k0, tile), :]
            vt = v_ref[pl.ds(k0, tile), :]
            zb = jnp.zeros_like(kt)
            kms = (jnp.where(is0, kt, zb), jnp.where(is0, zb, kt))
            vms = (jnp.where(is0, vt, zb), jnp.where(is0, zb, vt))
            cums = [cum_ref[pl.ds((2 * pid + h) * nt + kj, 1), :] for h in (0, 1)]

            def step(qi, state, masked):
                dk_a, dv_a, dc0, dc1 = state
                q0 = pl.multiple_of(qi * tile, tile)
                qt = q_ref[pl.ds(q0, tile), :]
                dot = do_bf[pl.ds(q0, tile), :]
                lse_t = lse_ref[pl.ds(q0, tile), :]
                dl_t = delta[pl.ds(q0, tile), :]
                dq_t = jnp.zeros((tile, HEAD_PAIR), F32)
                dcs = [dc0, dc1]
                rsum = []
                for h in (0, 1):
                    c0 = h * HEAD_DIM
                    s = lax.dot_general(qt, kms[h], NT_DIMS, preferred_element_type=F32) - cums[h]
                    if masked:
                        s = jnp.where(causal, s, NEG)
                    p = jnp.exp(s - lse_t[:, c0:c0 + 1])
                    dp = lax.dot_general(dot, vms[h], NT_DIMS, preferred_element_type=F32)
                    ds = p * (dp - dl_t[:, c0:c0 + 1])
                    dcs[h] = dcs[h] + jnp.sum(ds, axis=0, keepdims=True)
                    rsum.append(jnp.sum(ds, axis=1, keepdims=True))
                    pb = p.astype(BF16)
                    dsb = ds.astype(BF16)
                    do_h = jnp.where(is0, dot, zb) if h == 0 else jnp.where(is0, zb, dot)
                    q_h = jnp.where(is0, qt, zb) if h == 0 else jnp.where(is0, zb, qt)
                    dv_a = dv_a + lax.dot_general(pb, do_h, TN_DIMS, preferred_element_type=F32)
                    dk_a = dk_a + lax.dot_general(dsb, q_h, TN_DIMS, preferred_element_type=F32)
                    dq_t = dq_t + jnp.dot(dsb, kms[h], preferred_element_type=F32)
                dq_acc[pl.ds(q0, tile), :] += dq_t
                row_sum[pl.ds(q0, tile), :] += jnp.where(is0, rsum[0], rsum[1])
                return dk_a, dv_a, dcs[0], dcs[1]

            zero = jnp.zeros((tile, HEAD_PAIR), F32)
            zrow = jnp.zeros((1, tile), F32)
            state = step(kj, (zero, zero, zrow, zrow), True)
            dk_a, dv_a, dc0, dc1 = lax.fori_loop(kj + 1, nt, lambda qi, c: step(qi, c, False), state)
            dk_ref[pl.ds(k0, tile), :] = dk_a.astype(BF16)
            dv_ref[pl.ds(k0, tile), :] = dv_a.astype(BF16)
            dcum_ref[pl.ds((2 * pid) * nt + kj, 1), :] = -dc0
            dcum_ref[pl.ds((2 * pid + 1) * nt + kj, 1), :] = -dc1
            return carry

        lax.fori_loop(0, nt, kv_loop, 0)

        def fin(i, carry):
            r0 = pl.multiple_of(i * tile, tile)
            dq_ref[pl.ds(r0, tile), :] = dq_acc[pl.ds(r0, tile), :].astype(BF16)
            return carry

        lax.fori_loop(0, nt, fin, 0)

        by_time = row_sum[...].T
        for h in (0, 1):
            for j in range(nt):
                dcum_ref[pl.ds((2 * pid + h) * nt + j, 1), :] += by_time[h * HEAD_DIM:h * HEAD_DIM + 1,
                                                                          j * tile:(j + 1) * tile]

    blk = pl.BlockSpec((s_len, HEAD_PAIR), lambda p: (0, p))
    full = pl.BlockSpec(cumr.shape, lambda p: (0, 0))
    return pl.pallas_call(
        body, name="flash_bwd", grid=(width // HEAD_PAIR,),
        in_specs=[blk, blk, blk, blk, blk, blk, full],
        out_specs=[blk, blk, blk, full],
        out_shape=[jax.ShapeDtypeStruct((s_len, width), BF16)] * 3 + [jax.ShapeDtypeStruct(cumr.shape, F32)],
        scratch_shapes=[pltpu.VMEM((s_len, HEAD_PAIR), F32), pltpu.VMEM((s_len, HEAD_PAIR), F32),
                        pltpu.VMEM((s_len, HEAD_PAIR), BF16), pltpu.VMEM((s_len, HEAD_PAIR), F32)],
        compiler_params=pltpu.CompilerParams(dimension_semantics=("arbitrary",)),
    )(q, k, v, o, do, lse, cumr)


def _out_ln(a, gate, x, w, g, b, name):
    s_len, d = x.shape
    tm = min(256, s_len)

    def body(a_ref, gate_ref, x_ref, w_ref, g_ref, b_ref, xn_ref, xh_ref, rs_ref, yg_ref):
        gt = gate_ref[...]
        yg = (a_ref[...] * (gt * _sigmoid(gt))).astype(BF16)
        yg_ref[...] = yg
        z = ALPHA * x_ref[...] + jnp.dot(yg, w_ref[...], preferred_element_type=F32)
        zc = z - jnp.mean(z, axis=1, keepdims=True)
        rstd = lax.rsqrt(jnp.mean(zc * zc, axis=1, keepdims=True) + LN_EPS)
        xh = zc * rstd
        xh_ref[...] = xh
        xn_ref[...] = xh * g_ref[...] + b_ref[...]
        rs_ref[...] = jnp.broadcast_to(rstd, (tm, 128))

    row = pl.BlockSpec((tm, d), lambda i: (i, 0))
    vec = pl.BlockSpec((1, d), lambda i: (0, 0))
    return pl.pallas_call(
        body, name=name, grid=(s_len // tm,),
        in_specs=[row, row, row, pl.BlockSpec(w.shape, lambda i: (0, 0)), vec, vec],
        out_specs=[row, row, pl.BlockSpec((tm, 128), lambda i: (i, 0)), row],
        out_shape=[jax.ShapeDtypeStruct((s_len, d), F32), jax.ShapeDtypeStruct((s_len, d), F32),
                   jax.ShapeDtypeStruct((s_len, 128), F32), jax.ShapeDtypeStruct((s_len, d), BF16)],
        compiler_params=pltpu.CompilerParams(dimension_semantics=("parallel",)),
    )(a, gate, x, w, g, b)


def _ln_bwd(dout, xh, rs, g, w, gate, a, name):
    s_len, d = dout.shape
    tm = min(256, s_len)

    def body(do_ref, xh_ref, rs_ref, g_ref, w_ref, gate_ref, a_ref, dz_ref, da_ref, dgate_ref, dg_ref, db_ref):
        @pl.when(pl.program_id(0) == 0)
        def _():
            dg_ref[...] = jnp.zeros_like(dg_ref)
            db_ref[...] = jnp.zeros_like(db_ref)

        dout_t = do_ref[...]
        xh_t = xh_ref[...]
        dg_ref[...] += jnp.sum(dout_t * xh_t, axis=0, keepdims=True)
        db_ref[...] += jnp.sum(dout_t, axis=0, keepdims=True)
        dxh = dout_t * g_ref[...]
        m1 = jnp.mean(dxh, axis=1, keepdims=True)
        m2 = jnp.mean(dxh * xh_t, axis=1, keepdims=True)
        dz = rs_ref[:, 0:1] * (dxh - m1 - xh_t * m2)
        dz_ref[...] = dz
        dyg = lax.dot_general(dz.astype(BF16), w_ref[...], NT_DIMS, preferred_element_type=F32)
        gt = gate_ref[...]
        sg = _sigmoid(gt)
        da_ref[...] = dyg * (gt * sg)
        dgate_ref[...] = (dyg * a_ref[...] * (sg * (1.0 + gt * (1.0 - sg)))).astype(BF16)

    row = pl.BlockSpec((tm, d), lambda i: (i, 0))
    vec = pl.BlockSpec((1, d), lambda i: (0, 0))
    return pl.pallas_call(
        body, name=name, grid=(s_len // tm,),
        in_specs=[row, row, pl.BlockSpec((tm, 128), lambda i: (i, 0)), vec, pl.BlockSpec(w.shape, lambda i: (0, 0)),
                  row, row],
        out_specs=[row, row, row, vec, vec],
        out_shape=[jax.ShapeDtypeStruct((s_len, d), F32), jax.ShapeDtypeStruct((s_len, d), F32),
                   jax.ShapeDtypeStruct((s_len, d), BF16), jax.ShapeDtypeStruct((1, d), F32),
                   jax.ShapeDtypeStruct((1, d), F32)],
        compiler_params=pltpu.CompilerParams(dimension_semantics=("arbitrary",)),
    )(dout, xh, rs, g, w, gate, a)


def _loss_grad(y, target):
    s_len, d = y.shape
    tm = min(512, s_len)

    def body(y_ref, t_ref, dy_ref, acc_ref):
        @pl.when(pl.program_id(0) == 0)
        def _():
            acc_ref[...] = jnp.zeros_like(acc_ref)

        diff = y_ref[...] - t_ref[...]
        dy_ref[...] = diff * (1.0 / d)
        acc_ref[...] += jnp.sum(diff * diff, axis=0, keepdims=True)

    row = pl.BlockSpec((tm, d), lambda i: (i, 0))
    return pl.pallas_call(
        body, name="loss_grad", grid=(s_len // tm,),
        in_specs=[row, row], out_specs=[row, pl.BlockSpec((1, d), lambda i: (0, 0))],
        out_shape=[jax.ShapeDtypeStruct((s_len, d), F32), jax.ShapeDtypeStruct((1, d), F32)],
        compiler_params=pltpu.CompilerParams(dimension_semantics=("arbitrary",)),
    )(y, target)


def _rnn_fwd(u, conv_w, conv_b, wa, ba, wi, bi, lam):
    s_len, width = u.shape
    tm = min(256, s_len)
    nb = width // RNN_BLOCK

    def body(u_ref, up_ref, cw_ref, cb_ref, wa_ref, ba_ref, wi_ref, bi_ref, lam_ref,
             h_ref, uc_ref, r_ref, i_ref, a_ref, b_scr, h_carry):
        step_id = pl.program_id(0)

        @pl.when(step_id == 0)
        def _():
            h_carry[...] = jnp.zeros_like(h_carry)

        for n in range(nb):
            blk = slice(n * RNN_BLOCK, (n + 1) * RNN_BLOCK)
            ut = u_ref[:, blk]
            prev = jnp.where(step_id > 0, up_ref[:, blk], 0.0)
            uc = cb_ref[:, blk] + cw_ref[3:4, blk] * ut
            for k in (1, 2, 3):
                uc = uc + cw_ref[3 - k:4 - k, blk] * _shift_down(ut, prev, k)
            ucb = uc.astype(BF16)
            r = _sigmoid(jnp.dot(ucb, wa_ref[n], preferred_element_type=F32) + ba_ref[:, blk])
            ig = _sigmoid(jnp.dot(ucb, wi_ref[n], preferred_element_type=F32) + bi_ref[:, blk])
            log_a = -LRU_C * r * _softplus(-lam_ref[:, blk])
            uc_ref[:, blk] = uc
            r_ref[:, blk] = r
            i_ref[:, blk] = ig
            a_ref[:, blk] = jnp.exp(log_a)
            b_scr[:, blk] = jnp.sqrt(_neg_expm1(2.0 * log_a)) * (ig * uc)

        def scan(t, h):
            h = a_ref[pl.ds(t, 1), :] * h + b_scr[pl.ds(t, 1), :]
            h_ref[pl.ds(t, 1), :] = h
            return h

        h_carry[...] = lax.fori_loop(0, tm, scan, h_carry[...], unroll=8)

    row = pl.BlockSpec((tm, width), lambda i: (i, 0))
    prev8 = pl.BlockSpec((8, width), lambda i: (jnp.maximum(i * (tm // 8) - 1, 0), 0))
    vec = pl.BlockSpec((1, width), lambda i: (0, 0))
    mat = pl.BlockSpec(wa.shape, lambda i: (0, 0, 0))
    return pl.pallas_call(
        body, name="rnn_fwd", grid=(s_len // tm,),
        in_specs=[row, prev8, pl.BlockSpec(conv_w.shape, lambda i: (0, 0)), vec, mat, vec, mat, vec, vec],
        out_specs=[row] * 5,
        out_shape=[jax.ShapeDtypeStruct((s_len, width), F32)] * 5,
        scratch_shapes=[pltpu.VMEM((tm, width), F32), pltpu.VMEM((1, width), F32)],
        compiler_params=pltpu.CompilerParams(dimension_semantics=("arbitrary",)),
    )(u, u, conv_w, conv_b, wa, ba, wi, bi, lam)


def _rnn_bwd(dh, a, h, r, ig, uc, lam, wa, wi):
    s_len, width = dh.shape
    tm = min(256, s_len)
    nt = s_len // tm
    nb = width // RNN_BLOCK

    def body(dh_ref, a_ref, h_ref, hp_ref, r_ref, i_ref, uc_ref, lam_ref, wa_ref, wi_ref,
             duc_ref, dwa_ref, dwi_ref, dba_ref, dbi_ref, dlam_ref, g_scr, c_scr, dsp_scr):
        step_id = pl.program_id(0)
        tile_id = nt - 1 - step_id

        @pl.when(step_id == 0)
        def _():
            c_scr[...] = jnp.zeros_like(c_scr)
            dsp_scr[...] = jnp.zeros_like(dsp_scr)
            dwa_ref[...] = jnp.zeros_like(dwa_ref)
            dwi_ref[...] = jnp.zeros_like(dwi_ref)
            dba_ref[...] = jnp.zeros_like(dba_ref)
            dbi_ref[...] = jnp.zeros_like(dbi_ref)

        def scan(j, c):
            t = tm - 1 - j
            g = dh_ref[pl.ds(t, 1), :] + c
            g_scr[pl.ds(t, 1), :] = g
            return a_ref[pl.ds(t, 1), :] * g

        c_scr[...] = lax.fori_loop(0, tm, scan, c_scr[...], unroll=8)

        for n in range(nb):
            blk = slice(n * RNN_BLOCK, (n + 1) * RNN_BLOCK)
            g_t = g_scr[:, blk]
            hp8 = jnp.where(tile_id > 0, hp_ref[:, blk], 0.0)
            h_prev = _shift_down(h_ref[:, blk], hp8, 1)
            av, rv, iv, ucv = a_ref[:, blk], r_ref[:, blk], i_ref[:, blk], uc_ref[:, blk]
            sp = _softplus(-lam_ref[:, blk])
            mag = jnp.sqrt(_neg_expm1(-2.0 * LRU_C * rv * sp))
            d_iu = g_t * mag
            dla = g_t * h_prev * av - g_t * (iv * ucv) * (av * av) / mag
            dsp_scr[:, blk] += jnp.sum(dla * (-LRU_C * rv), axis=0, keepdims=True)
            dra = dla * (-LRU_C * sp) * rv * (1.0 - rv)
            dia = d_iu * ucv * iv * (1.0 - iv)
            drab, diab, ucb = dra.astype(BF16), dia.astype(BF16), ucv.astype(BF16)
            duc_ref[:, blk] = (d_iu * iv
                               + lax.dot_general(drab, wa_ref[n], NT_DIMS, preferred_element_type=F32)
                               + lax.dot_general(diab, wi_ref[n], NT_DIMS, preferred_element_type=F32))
            dwa_ref[n] += lax.dot_general(ucb, drab, TN_DIMS, preferred_element_type=F32)
            dwi_ref[n] += lax.dot_general(ucb, diab, TN_DIMS, preferred_element_type=F32)
            dba_ref[:, blk] += jnp.sum(dra, axis=0, keepdims=True)
            dbi_ref[:, blk] += jnp.sum(dia, axis=0, keepdims=True)

        @pl.when(step_id == nt - 1)
        def _():
            dlam_ref[...] = -dsp_scr[...] * _sigmoid(-lam_ref[...])

    row = pl.BlockSpec((tm, width), lambda i: (nt - 1 - i, 0))
    prev8 = pl.BlockSpec((8, width), lambda i: (jnp.maximum((nt - 1 - i) * (tm // 8) - 1, 0), 0))
    vec = pl.BlockSpec((1, width), lambda i: (0, 0))
    mat = pl.BlockSpec(wa.shape, lambda i: (0, 0, 0))
    return pl.pallas_call(
        body, name="rnn_bwd", grid=(nt,),
        in_specs=[row, row, row, prev8, row, row, row, vec, mat, mat],
        out_specs=[row, mat, mat, vec, vec, vec],
        out_shape=[jax.ShapeDtypeStruct((s_len, width), F32), jax.ShapeDtypeStruct(wa.shape, F32),
                   jax.ShapeDtypeStruct(wa.shape, F32)] + [jax.ShapeDtypeStruct((1, width), F32)] * 3,
        scratch_shapes=[pltpu.VMEM((tm, width), F32), pltpu.VMEM((1, width), F32), pltpu.VMEM((1, width), F32)],
        compiler_params=pltpu.CompilerParams(dimension_semantics=("arbitrary",)),
    )(dh, a, h, h, r, ig, uc, lam, wa, wi)


def _conv_bwd(duc, u, conv_w):
    s_len, width = duc.shape
    tm = min(256, s_len)
    nt = s_len // tm

    def body(d_ref, dn_ref, u_ref, up_ref, cw_ref, du_ref, dcw_ref, dcb_ref):
        step_id = pl.program_id(0)

        @pl.when(step_id == 0)
        def _():
            dcw_ref[...] = jnp.zeros_like(dcw_ref)
            dcb_ref[...] = jnp.zeros_like(dcb_ref)

        for n in range(width // RNN_BLOCK):
            blk = slice(n * RNN_BLOCK, (n + 1) * RNN_BLOCK)
            dt = d_ref[:, blk]
            ut = u_ref[:, blk]
            nxt = jnp.where(step_id < nt - 1, dn_ref[:, blk], 0.0)
            prev = jnp.where(step_id > 0, up_ref[:, blk], 0.0)
            du = cw_ref[3:4, blk] * dt
            dcw_ref[3:4, blk] += jnp.sum(dt * ut, axis=0, keepdims=True)
            for k in (1, 2, 3):
                du = du + cw_ref[3 - k:4 - k, blk] * _shift_up(dt, nxt, k)
                dcw_ref[3 - k:4 - k, blk] += jnp.sum(dt * _shift_down(ut, prev, k), axis=0, keepdims=True)
            du_ref[:, blk] = du.astype(BF16)
            dcb_ref[:, blk] += jnp.sum(dt, axis=0, keepdims=True)

    row = pl.BlockSpec((tm, width), lambda i: (i, 0))
    prev8 = pl.BlockSpec((8, width), lambda i: (jnp.maximum(i * (tm // 8) - 1, 0), 0))
    next8 = pl.BlockSpec((8, width), lambda i: (jnp.minimum((i + 1) * (tm // 8), s_len // 8 - 1), 0))
    return pl.pallas_call(
        body, name="conv_bwd", grid=(nt,),
        in_specs=[row, next8, row, prev8, pl.BlockSpec(conv_w.shape, lambda i: (0, 0))],
        out_specs=[row, pl.BlockSpec(conv_w.shape, lambda i: (0, 0)), pl.BlockSpec((1, width), lambda i: (0, 0))],
        out_shape=[jax.ShapeDtypeStruct((s_len, width), BF16), jax.ShapeDtypeStruct(conv_w.shape, F32),
                   jax.ShapeDtypeStruct((1, width), F32)],
        compiler_params=pltpu.CompilerParams(dimension_semantics=("arbitrary",)),
    )(duc, duc, u, u, conv_w)


def _pair_sum(core, grads, theirs, out_dtype):
    n, _, half, _ = grads.shape
    tb = min(128, half)

    def body(c_ref, a_ref, b_ref, o_ref):
        o_ref[...] = (a_ref[...] + b_ref[...]).astype(out_dtype)

    blk = pl.BlockSpec((n, tb, LANES), lambda i, c: (0, i, 0))
    return pl.pallas_call(
        body, name="pair_sum",
        grid_spec=pltpu.PrefetchScalarGridSpec(
            num_scalar_prefetch=1, grid=(half // tb,),
            in_specs=[pl.BlockSpec((n, None, tb, LANES), lambda i, c: (0, c[0], i, 0)), blk], out_specs=blk),
        out_shape=jax.ShapeDtypeStruct((n, half, LANES), out_dtype),
        compiler_params=pltpu.CompilerParams(dimension_semantics=("parallel",)),
    )(core, grads, theirs)


def _sum_chips(parts):
    rows = parts.shape[1]
    tb = min(128, rows)

    def body(p_ref, o_ref):
        o_ref[...] = ((p_ref[0].astype(F32) + p_ref[1].astype(F32)) + p_ref[2].astype(F32)) + p_ref[3].astype(F32)

    return pl.pallas_call(
        body, name="chip_sum", grid=(rows // tb,),
        in_specs=[pl.BlockSpec((N_CHIPS, tb, LANES), lambda i: (0, i, 0))],
        out_specs=pl.BlockSpec((tb, LANES), lambda i: (i, 0)),
        out_shape=jax.ShapeDtypeStruct((rows, LANES), F32),
        compiler_params=pltpu.CompilerParams(dimension_semantics=("parallel",)),
    )(parts)


def _adamw(w, g, m, v, rows_per_block):
    n, rows, cols = w.shape
    blk = pl.BlockSpec((1, rows_per_block, cols), lambda i, j: (i, j, 0))

    def body(w_ref, g_ref, m_ref, v_ref, d_ref, nm_ref, nv_ref):
        gt = g_ref[...]
        nm = ADAM_B1 * m_ref[...] + (1.0 - ADAM_B1) * gt
        nv = ADAM_B2 * v_ref[...] + (1.0 - ADAM_B2) * (gt * gt)
        m_hat = nm / (1.0 - ADAM_B1 ** ADAM_STEP)
        v_hat = nv / (1.0 - ADAM_B2 ** ADAM_STEP)
        d_ref[...] = -ADAM_LR * (m_hat / (jnp.sqrt(v_hat) + ADAM_EPS) + ADAM_WD * w_ref[...])
        nm_ref[...] = nm
        nv_ref[...] = nv

    return pl.pallas_call(
        body, name="adamw", grid=(n, rows // rows_per_block), in_specs=[blk] * 4, out_specs=[blk] * 3,
        out_shape=[jax.ShapeDtypeStruct(w.shape, F32)] * 3,
        compiler_params=pltpu.CompilerParams(dimension_semantics=("parallel", "parallel")),
    )(w, g, m, v)


def _place():
    x, y, c = lax.axis_index("x"), lax.axis_index("y"), lax.axis_index("c")
    return x, y, c, [(1 - x, y), (x, 1 - y), (1 - x, 1 - y)]


ANY = pl.BlockSpec(memory_space=pl.ANY)
COPY_PARTS = 4


def _remote_parts(src_of, dst_of, rows, send_sem, recv_sem, to):
    parts = COPY_PARTS if rows % (16 * COPY_PARTS) == 0 else 1
    step = rows // parts
    for i in range(parts):
        pltpu.make_async_remote_copy(src_ref=src_of(i * step, step), dst_ref=dst_of(i * step, step),
                                     send_sem=send_sem, recv_sem=recv_sem, device_id=to, device_id_type=MESH).start()
    return pltpu.make_async_remote_copy(src_ref=src_of(0, rows), dst_ref=dst_of(0, rows), send_sem=send_sem,
                                        recv_sem=recv_sem, device_id=to, device_id_type=MESH)


def _gather_chips(shard):
    rows = shard.shape[0]
    half = rows // 2

    def body(p_ref, out_ref, send_sems, recv_sems):
        x, y, c, chips = _place()
        me = 2 * x + y

        def rows_of(chip, hc):
            return lambda r0, n: out_ref.at[chip, pl.ds(hc * half + r0, n), :]

        def mine_half(r0, n):
            return p_ref.at[pl.ds(c * half + r0, n), :]

        first = [_remote_parts(mine_half, rows_of(me, c), half, send_sems.at[j], recv_sems.at[j], (cx, cy, c))
                 for j, (cx, cy) in enumerate(chips)]
        passed = []
        for j, (cx, cy) in enumerate(chips):
            landed = rows_of(2 * cx + cy, c)
            pltpu.make_async_remote_copy(src_ref=landed(0, half), dst_ref=landed(0, half), send_sem=send_sems.at[j],
                                         recv_sem=recv_sems.at[j], device_id=(x, y, c), device_id_type=MESH).wait_recv()
            passed.append(_remote_parts(landed, landed, half, send_sems.at[3 + j], recv_sems.at[3 + j], (x, y, 1 - c)))
        for j, (cx, cy) in enumerate(chips):
            other = rows_of(2 * cx + cy, 1 - c)(0, half)
            pltpu.make_async_remote_copy(src_ref=other, dst_ref=other, send_sem=send_sems.at[3 + j],
                                         recv_sem=recv_sems.at[3 + j], device_id=(x, y, c), device_id_type=MESH).wait_recv()
        for cp in first + passed:
            cp.wait_send()

    return pl.pallas_call(
        body, name="gather_chips", in_specs=[ANY], out_specs=ANY,
        out_shape=jax.ShapeDtypeStruct((N_CHIPS,) + shard.shape, shard.dtype),
        scratch_shapes=[pltpu.SemaphoreType.DMA((6,)), pltpu.SemaphoreType.DMA((6,))],
    )(shard)


def _swap_halves(bufs):
    nb = len(bufs)

    def body(*refs):
        g_refs, t_refs, (send_sems, recv_sems) = refs[:nb], refs[nb:2 * nb], refs[2 * nb:]
        x, y, c, _ = _place()
        gives = []
        for b, (g_ref, t_ref) in enumerate(zip(g_refs, t_refs)):
            for j in range(N_CHIPS):
                k = b * N_CHIPS + j
                gives.append(_remote_parts(lambda r0, m, g_ref=g_ref, j=j: g_ref.at[j, 1 - c, pl.ds(r0, m), :],
                                           lambda r0, m, t_ref=t_ref, j=j: t_ref.at[j, pl.ds(r0, m), :],
                                           g_ref.shape[2], send_sems.at[k], recv_sems.at[k], (x, y, 1 - c)))
        for give in gives:
            give.wait()

    return pl.pallas_call(
        body, name="swap_halves", in_specs=[ANY] * nb, out_specs=[ANY] * nb,
        out_shape=[jax.ShapeDtypeStruct((b.shape[0], b.shape[2], b.shape[3]), b.dtype) for b in bufs],
        scratch_shapes=[pltpu.SemaphoreType.DMA((nb * N_CHIPS,)), pltpu.SemaphoreType.DMA((nb * N_CHIPS,))],
    )(*bufs)


def _scatter_chips(bufs):
    nb = len(bufs)

    def body(*refs):
        t_refs, o_refs, (send_sems, recv_sems) = refs[:nb], refs[nb:2 * nb], refs[2 * nb:]
        x, y, c, chips = _place()
        me = 2 * x + y

        def slot(ref, chip):
            return lambda r0, n: ref.at[chip, pl.ds(r0, n), :]

        sends = []
        for b, (t_ref, o_ref) in enumerate(zip(t_refs, o_refs)):
            for j, (cx, cy) in enumerate(chips):
                k = 3 * b + j
                sends.append(_remote_parts(slot(t_ref, 2 * cx + cy), slot(o_ref, me), t_ref.shape[1],
                                           send_sems.at[k], recv_sems.at[k], (cx, cy, c)))
        for b, o_ref in enumerate(o_refs):
            for j, (cx, cy) in enumerate(chips):
                landed = o_ref.at[2 * cx + cy]
                pltpu.make_async_remote_copy(src_ref=landed, dst_ref=landed, send_sem=send_sems.at[3 * b + j],
                                             recv_sem=recv_sems.at[3 * b + j], device_id=(x, y, c),
                                             device_id_type=MESH).wait_recv()
        for cp in sends:
            cp.wait_send()

    return pl.pallas_call(
        body, name="scatter_chips", in_specs=[ANY] * nb, out_specs=[ANY] * nb,
        out_shape=[jax.ShapeDtypeStruct(b.shape, b.dtype) for b in bufs],
        scratch_shapes=[pltpu.SemaphoreType.DMA((3 * nb,)), pltpu.SemaphoreType.DMA((3 * nb,))],
    )(*bufs)


def _give_sibling(bufs):
    nb = len(bufs)

    def body(*refs):
        h_refs, o_refs, (send_sems, recv_sems) = refs[:nb], refs[nb:2 * nb], refs[2 * nb:]
        x, y, c, _ = _place()
        gives = [_remote_parts(lambda r0, n, h_ref=h_ref: h_ref.at[pl.ds(r0, n), :],
                               lambda r0, n, o_ref=o_ref: o_ref.at[pl.ds(r0, n), :],
                               h_ref.shape[0], send_sems.at[b], recv_sems.at[b], (x, y, 1 - c))
                 for b, (h_ref, o_ref) in enumerate(zip(h_refs, o_refs))]
        for give in gives:
            give.wait()

    return pl.pallas_call(
        body, name="give_sibling", in_specs=[ANY] * nb, out_specs=[ANY] * nb,
        out_shape=[jax.ShapeDtypeStruct(b.shape, b.dtype) for b in bufs],
        scratch_shapes=[pltpu.SemaphoreType.DMA((nb,)), pltpu.SemaphoreType.DMA((nb,))],
    )(*bufs)


def _pack(arrays, dtype, row_align):
    flat = []
    for arr in arrays:
        v = arr.reshape(-1)
        flat.append(jnp.pad(v, (0, (-v.shape[0]) % LANES)))
    total = sum(f.shape[0] for f in flat) // LANES
    pad_rows = (-total) % row_align
    if pad_rows:
        flat.append(jnp.zeros((pad_rows * LANES,), dtype))
    return jnp.concatenate(flat).reshape(-1, LANES)


def _unpack(buf, shapes):
    lead = buf.shape[:-2]
    flat = buf.reshape(lead + (-1,))
    out, off = [], 0
    for shape in shapes:
        size = 1
        for dim in shape:
            size *= dim
        out.append(flat[..., off:off + size].reshape(lead + tuple(shape)))
        off += size + (-size) % LANES
    return out


def _from_chips(parts, axis):
    return jnp.concatenate([parts[j] for j in range(N_CHIPS)], axis=axis)


def kernel(x, ln_g, ln_b, attn_w_in, attn_b_f, attn_w_out, rnn_w_in, rnn_conv_w, rnn_conv_b, rnn_w_a, rnn_b_a, rnn_w_i, rnn_b_i, rnn_lambda, rnn_w_out, loss_target, m_ln_g, m_ln_b, m_attn_w_in, m_attn_b_f, m_attn_w_out, m_rnn_w_in, m_rnn_conv_w, m_rnn_conv_b, m_rnn_w_a, m_rnn_b_a, m_rnn_w_i, m_rnn_b_i, m_rnn_lambda, m_rnn_w_out, v_ln_g, v_ln_b, v_attn_w_in, v_attn_b_f, v_attn_w_out, v_rnn_w_in, v_rnn_conv_w, v_rnn_conv_b, v_rnn_w_a, v_rnn_b_a, v_rnn_w_i, v_rnn_b_i, v_rnn_lambda, v_rnn_w_out):
    s_len, d = x.shape[1], x.shape[2]
    xs = x.reshape(s_len, d)
    target = loss_target.reshape(s_len, d)
    tile = _att_tile(s_len)
    nt = s_len // tile
    heads = attn_b_f.shape[1]
    qkvg = attn_w_in.shape[2] * N_CHIPS - heads

    big = [attn_w_in, attn_w_out, rnn_w_in, rnn_w_a, rnn_w_i, rnn_w_out]
    small = [rnn_conv_w, rnn_conv_b, rnn_b_a, rnn_b_i, rnn_lambda]
    packed = _pack([w.astype(BF16) for w in big] + [lax.bitcast_convert_type(w, BF16) for w in small], BF16,
                   ROW_ALIGN)
    me = 2 * lax.axis_index("x") + lax.axis_index("y")
    core = lax.axis_index("c").astype(jnp.int32)
    gathered = lax.dynamic_update_index_in_dim(_gather_chips(packed), packed, me, 0)
    parts = _unpack(gathered, [w.shape for w in big] + [w.shape + (2,) for w in small])
    w_in_a, w_out_a, w_in_r, w_a, w_i, w_out_r = [
        _from_chips(p, ax) for p, ax in zip(parts[:6], (2, 1, 2, 2, 2, 1))]
    conv_w, conv_b, b_a, b_i, lam = [
        _from_chips(lax.bitcast_convert_type(p, F32), ax) for p, ax in zip(parts[6:], (2, 1, 1, 1, 1))]
    w_cat = jnp.concatenate(
        [w_in_a[:, :, :d] * (HEAD_DIM ** -0.5), w_in_a[:, :, d:qkvg],
         jnp.pad(w_in_a[:, :, qkvg:], ((0, 0), (0, 0), (0, 128 - heads)))], axis=2).astype(BF16)
    b_f = jnp.pad(attn_b_f, ((0, 0), (0, 128 - heads)))

    saved = []
    cur = xs
    for layer in range(DEPTH):
        idx = layer // 2
        g_l, b_l = ln_g[layer:layer + 1], ln_b[layer:layer + 1]
        if layer % 2 == 0:
            q, k, v, gate, fl = _proj(cur, w_cat[idx], [(0, d, BF16), (d, d, BF16), (2 * d, d, BF16),
                                                         (3 * d, d, F32), (4 * d, 128, F32)], "attn_proj")
            cum, sneg = _fcum(fl, b_f[idx:idx + 1])
            cumr = cum.reshape(heads * nt, tile)
            o, lse = _flash_fwd(q, k, v, cumr, tile)
            nxt, xh, rs, yg = _out_ln(o, gate, cur, w_out_a[idx], g_l, b_l, "out_ln")
            saved.append(dict(x=cur, q=q, k=k, v=v, gate=gate, cumr=cumr, sneg=sneg, a=o, lse=lse, xh=xh, rs=rs, yg=yg))
        else:
            u, gate = _proj(cur, w_in_r[idx], [(0, d, F32), (d, d, F32)], "rnn_proj")
            vecs = [t[idx:idx + 1] for t in (conv_b, b_a, b_i, lam)]
            hseq, uc, r, ig, a = _rnn_fwd(u, conv_w[idx], vecs[0], w_a[idx], vecs[1], w_i[idx], vecs[2], vecs[3])
            nxt, xh, rs, yg = _out_ln(hseq, gate, cur, w_out_r[idx], g_l, b_l, "out_ln")
            saved.append(dict(x=cur, u=u, gate=gate, uc=uc, r=r, ig=ig, av=a, a=hseq, xh=xh, rs=rs, yg=yg, lam=vecs[3]))
        cur = nxt

    dout, sq = _loss_grad(cur, target)
    loss = lax.psum(0.5 * jnp.sum(sq) / d, ("x", "y", "c"))

    g_ln_g, g_ln_b = [None] * DEPTH, [None] * DEPTH
    g_attn = [None] * 2
    g_rnn = [None] * 2
    for layer in reversed(range(DEPTH)):
        idx = layer // 2
        sv = saved[layer]
        w_out = w_out_a[idx] if layer % 2 == 0 else w_out_r[idx]
        dz, da, dgate, dg, db = _ln_bwd(dout, sv["xh"], sv["rs"], ln_g[layer:layer + 1], w_out, sv["gate"], sv["a"],
                                        "ln_bwd")
        g_ln_g[layer], g_ln_b[layer] = dg, db
        d_w_out = _mm_tn(sv["yg"], dz, "dw_out")
        if layer % 2 == 0:
            dq, dk, dv, dcumr = _flash_bwd(sv["q"], sv["k"], sv["v"], sv["a"], da, sv["lse"], sv["cumr"], tile)
            dlogit, dbf = _fbwd(dcumr.reshape(heads, s_len), sv["sneg"])
            pieces = [dq, dk, dv, dgate, dlogit]
            dout = _mm_nt(dz, [(p, j * d) for j, p in enumerate(pieces)], w_cat[idx], "attn_dx")
            dws = [_mm_tn(sv["x"], p, "dw_in") for p in pieces[:4]] + [_mm_tn(sv["x"], dlogit, "dw_f")]
            dws[0] = dws[0] * (HEAD_DIM ** -0.5)
            g_attn[idx] = dict(w_in=dws, b_f=dbf[:, 0], w_out=d_w_out)
        else:
            duc, d_wa, d_wi, d_ba, d_bi, d_lam = _rnn_bwd(da, sv["av"], sv["a"], sv["r"], sv["ig"], sv["uc"], sv["lam"],
                                                          w_a[idx], w_i[idx])
            du, d_cw, d_cb = _conv_bwd(duc, sv["u"], conv_w[idx])
            dout = _mm_nt(dz, [(du, 0), (dgate, d)], w_in_r[idx], "rnn_dx")
            d_w_in = [_mm_tn(sv["x"], du, "dw_in"), _mm_tn(sv["x"], dgate, "dw_in")]
            g_rnn[idx] = dict(w_in=d_w_in, conv_w=d_cw, conv_b=d_cb[0], w_a=d_wa, b_a=d_ba[0], w_i=d_wi, b_i=d_bi[0],
                              lam=d_lam[0], w_out=d_w_out)
    grad_x = dout.reshape(x.shape)

    def both(items, key, cut=lambda t: t):
        return [cut(it[key]) for it in items]

    def rows_of(j, n):
        return lambda t: t[j * n:(j + 1) * n]

    def cols_of(j, n):
        return lambda t: t[..., j * n:(j + 1) * n]

    rb = RNN_BLOCK // N_CHIPS
    dq4 = d // N_CHIPS
    edges = jnp.stack([jnp.stack([it["w_in"][p][:, :heads] for it in g_attn]) for p in (1, 2, 3, 4)])

    def large_for(j):
        return (both(g_attn, "w_in", lambda t: t[j]) + both(g_attn, "w_out", rows_of(j, dq4))
                + both(g_rnn, "w_in", lambda t: cols_of(j % 2, d // 2)(t[j // 2]))
                + both(g_rnn, "w_a", lambda t: t[:, j * rb:(j + 1) * rb])
                + both(g_rnn, "w_i", lambda t: t[:, j * rb:(j + 1) * rb])
                + both(g_rnn, "w_out", rows_of(j, dq4)) + [edges])

    def small_for(j):
        return ([jnp.stack(both(g_rnn, key, cols_of(j, dq4))) for key in ("conv_w", "conv_b", "b_a", "b_i", "lam")]
                + [jnp.concatenate(g_ln_g), jnp.concatenate(g_ln_b), jnp.stack(both(g_attn, "b_f"))])

    def halves(buf):
        return buf.reshape(N_CHIPS, 2, buf.shape[1] // 2, LANES)

    large = halves(jnp.stack([_pack(large_for(j), F32, ROW_ALIGN) for j in range(N_CHIPS)]))
    small = halves(jnp.stack([_pack(small_for(j), F32, 32) for j in range(N_CHIPS)]))

    core1 = core.reshape(1)
    theirs = _swap_halves([large, small])
    pair = [_pair_sum(core1, large, theirs[0], BF16), _pair_sum(core1, small, theirs[1], F32)]
    summed = []
    for mine_all, got in zip(pair, _scatter_chips(pair)):
        own = lax.dynamic_index_in_dim(mine_all, me, 0, keepdims=False)
        summed.append(_sum_chips(lax.dynamic_update_index_in_dim(got, own, me, 0)))
    reduced = [jnp.concatenate([jnp.where(core == 0, mine, other), jnp.where(core == 0, other, mine)])
               for mine, other in zip(summed, _give_sibling(summed))]
    g_in_group, g_w_out_a, g_w_in_r, g_w_a, g_w_i, g_w_out_r, g_edges = _unpack(reduced[0], [
        (2, d, d), attn_w_out.shape, rnn_w_in.shape, rnn_w_a.shape, rnn_w_i.shape, rnn_w_out.shape,
        (N_CHIPS, 2, d, heads)])
    g_conv_w, g_conv_b, g_b_a, g_b_i, g_lam, g_ln_g_sum, g_ln_b_sum, g_b_f = _unpack(reduced[1], [
        rnn_conv_w.shape, rnn_conv_b.shape, rnn_b_a.shape, rnn_b_i.shape, rnn_lambda.shape, ln_g.shape, ln_b.shape,
        attn_b_f.shape])
    wide = jnp.concatenate([g_in_group, lax.dynamic_index_in_dim(g_edges, me, 0, keepdims=False)], axis=2)
    g_w_in_a = lax.dynamic_slice(wide, (0, 0, (heads // N_CHIPS) * me), attn_w_in.shape)

    def adam_nd(w, g, m, v, view, rows_per_block):
        outs = _adamw(w.reshape(view), g.reshape(view), m.reshape(view), v.reshape(view), rows_per_block)
        return [t.reshape(w.shape) for t in outs]

    upd = {
        "attn_w_in": adam_nd(attn_w_in, g_w_in_a, m_attn_w_in, v_attn_w_in, attn_w_in.shape, 256),
        "attn_w_out": adam_nd(attn_w_out, g_w_out_a, m_attn_w_out, v_attn_w_out, attn_w_out.shape, 256),
        "rnn_w_in": adam_nd(rnn_w_in, g_w_in_r, m_rnn_w_in, v_rnn_w_in, rnn_w_in.shape, 512),
        "rnn_w_a": adam_nd(rnn_w_a, g_w_a, m_rnn_w_a, v_rnn_w_a, (1, -1, RNN_BLOCK), 512),
        "rnn_w_i": adam_nd(rnn_w_i, g_w_i, m_rnn_w_i, v_rnn_w_i, (1, -1, RNN_BLOCK), 512),
        "rnn_w_out": adam_nd(rnn_w_out, g_w_out_r, m_rnn_w_out, v_rnn_w_out, rnn_w_out.shape, 256),
    }
    smalls = [rnn_conv_w, rnn_conv_b, rnn_b_a, rnn_b_i, rnn_lambda, ln_g, ln_b, attn_b_f]
    g_small = [g_conv_w, g_conv_b, g_b_a, g_b_i, g_lam, g_ln_g_sum, g_ln_b_sum, g_b_f]
    m_small = [m_rnn_conv_w, m_rnn_conv_b, m_rnn_b_a, m_rnn_b_i, m_rnn_lambda, m_ln_g, m_ln_b, m_attn_b_f]
    v_small = [v_rnn_conv_w, v_rnn_conv_b, v_rnn_b_a, v_rnn_b_i, v_rnn_lambda, v_ln_g, v_ln_b, v_attn_b_f]
    packs = [_pack(t, F32, 16)[None] for t in (smalls, g_small, m_small, v_small)]
    small_out = [_unpack(t[0], [w.shape for w in smalls]) for t in _adamw(*packs, 16)]
    for k, name in enumerate(("rnn_conv_w", "rnn_conv_b", "rnn_b_a", "rnn_b_i", "rnn_lambda", "ln_g", "ln_b",
                              "attn_b_f")):
        upd[name] = [small_out[0][k], small_out[1][k], small_out[2][k]]
    grad = {"attn_w_in": g_w_in_a, "attn_w_out": g_w_out_a, "rnn_w_in": g_w_in_r, "rnn_w_a": g_w_a, "rnn_w_i": g_w_i,
            "rnn_w_out": g_w_out_r, "rnn_conv_w": g_conv_w, "rnn_conv_b": g_conv_b, "rnn_b_a": g_b_a,
            "rnn_b_i": g_b_i, "rnn_lambda": g_lam, "ln_g": g_ln_g_sum, "ln_b": g_ln_b_sum, "attn_b_f": g_b_f}
    names = ("ln_g", "ln_b", "attn_w_in", "attn_b_f", "attn_w_out", "rnn_w_in", "rnn_conv_w", "rnn_conv_b",
             "rnn_w_a", "rnn_b_a", "rnn_w_i", "rnn_b_i", "rnn_lambda", "rnn_w_out")
    return (loss, grad_x, *[grad[n] for n in names], *[upd[n][0] for n in names], *[upd[n][1] for n in names],
            *[upd[n][2] for n in names])
```

```python
import functools

import jax
import jax.numpy as jnp
from jax import lax
from jax.experimental import pallas as pl
from jax.experimental.pallas import tpu as pltpu

F32 = jnp.float32
BF16 = jnp.bfloat16
MESH = pl.DeviceIdType.MESH

DEPTH = 4
HEAD_DIM = 64
HEAD_PAIR = 2 * HEAD_DIM
RNN_BLOCK = 256
CONV_WIDTH = 4
LRU_C = 8.0
ALPHA = (2.0 * DEPTH) ** 0.25
LN_EPS = 1e-5
ADAM_LR, ADAM_B1, ADAM_B2, ADAM_EPS, ADAM_WD, ADAM_STEP = 0.001, 0.9, 0.999, 1e-08, 0.01, 10
N_CHIPS = 4
LANES = 1024
ROW_ALIGN = 256
NEG = -1e30

NT_DIMS = (((1,), (1,)), ((), ()))
TN_DIMS = (((0,), (0,)), ((), ()))


def _sigmoid(z):
    return 1.0 / (1.0 + jnp.exp(-z))


def _softplus(z):
    return jnp.maximum(z, 0.0) + jnp.log(1.0 + jnp.exp(-jnp.abs(z)))


def _neg_expm1(y):
    poly = y * (1.0 + y * (0.5 + y * (1.0 / 6.0 + y * (1.0 / 24.0))))
    return -jnp.where(y > -0.05, poly, jnp.exp(y) - 1.0)


def _shift_down(cur, prev8, k):
    n = cur.shape[0]
    row = lax.broadcasted_iota(jnp.int32, cur.shape, 0)
    fix = jnp.tile(pltpu.roll(prev8, k, 0), (n // 8, 1))
    return jnp.where(row < k, fix, pltpu.roll(cur, k, 0))


def _shift_up(cur, next8, k):
    n = cur.shape[0]
    row = lax.broadcasted_iota(jnp.int32, cur.shape, 0)
    fix = jnp.tile(pltpu.roll(next8, 8 - k, 0), (n // 8, 1))
    return jnp.where(row >= n - k, fix, pltpu.roll(cur, n - k, 0))


def _proj(x, w, outs, name):
    s_len, d = x.shape
    tm = min(512, s_len)

    def body(x_ref, w_ref, *o_refs):
        xb = x_ref[...].astype(BF16)
        for (c0, wd, dt), o_ref in zip(outs, o_refs):
            step = min(wd, 512)
            for cc in range(0, wd, step):
                o_ref[:, cc:cc + step] = jnp.dot(
                    xb, w_ref[:, c0 + cc:c0 + cc + step], preferred_element_type=F32).astype(dt)

    return pl.pallas_call(
        body, name=name, grid=(s_len // tm,),
        in_specs=[pl.BlockSpec((tm, d), lambda i: (i, 0)), pl.BlockSpec(w.shape, lambda i: (0, 0))],
        out_specs=[pl.BlockSpec((tm, wd), lambda i: (i, 0)) for _, wd, _ in outs],
        out_shape=[jax.ShapeDtypeStruct((s_len, wd), dt) for _, wd, dt in outs],
        compiler_params=pltpu.CompilerParams(dimension_semantics=("parallel",)),
    )(x, w)


def _mm_nt(dz, pieces, w, name):
    s_len, d = dz.shape
    tm = min(256, s_len)
    n = len(pieces)
    cols = [(c0, p.shape[1]) for p, c0 in pieces]

    def body(dz_ref, *rest):
        w_ref, o_ref = rest[n], rest[n + 1]
        acc = ALPHA * dz_ref[...]
        for (c0, wd), p_ref in zip(cols, rest[:n]):
            acc = acc + lax.dot_general(p_ref[...], w_ref[:, c0:c0 + wd], NT_DIMS, preferred_element_type=F32)
        o_ref[...] = acc

    return pl.pallas_call(
        body, name=name, grid=(s_len // tm,),
        in_specs=[pl.BlockSpec((tm, d), lambda i: (i, 0))]
        + [pl.BlockSpec((tm, wd), lambda i: (i, 0)) for _, wd in cols]
        + [pl.BlockSpec(w.shape, lambda i: (0, 0))],
        out_specs=pl.BlockSpec((tm, d), lambda i: (i, 0)),
        out_shape=jax.ShapeDtypeStruct((s_len, d), F32),
        compiler_params=pltpu.CompilerParams(dimension_semantics=("parallel",)),
    )(dz, *[p for p, _ in pieces], w)


def _mm_tn(a, b, name):
    s_len, m = a.shape
    n = b.shape[1]
    tn = min(n, 512)
    ts = min(s_len, 512)

    def body(a_ref, b_ref, o_ref):
        @pl.when(pl.program_id(1) == 0)
        def _():
            o_ref[...] = jnp.zeros_like(o_ref)

        o_ref[...] += lax.dot_general(a_ref[...].astype(BF16), b_ref[...].astype(BF16), TN_DIMS,
                                      preferred_element_type=F32)

    return pl.pallas_call(
        body, name=name, grid=(n // tn, s_len // ts),
        in_specs=[pl.BlockSpec((ts, m), lambda j, s: (s, 0)), pl.BlockSpec((ts, tn), lambda j, s: (s, j))],
        out_specs=pl.BlockSpec((m, tn), lambda j, s: (0, j)),
        out_shape=jax.ShapeDtypeStruct((m, n), F32),
        compiler_params=pltpu.CompilerParams(dimension_semantics=("parallel", "arbitrary")),
    )(a, b)


def _fcum(fl, bias):
    s_len = fl.shape[0]

    def body(fl_ref, b_ref, cum_ref, sg_ref):
        z = fl_ref[...] + b_ref[...]
        e = jnp.exp(-jnp.abs(z))
        logf = jnp.minimum(z, 0.0) - jnp.log(1.0 + e)
        sneg = jnp.where(z >= 0, e, 1.0) / (1.0 + e)
        run = logf.T[0:16, :]
        sg_ref[...] = sneg.T[0:16, :]
        lane = lax.broadcasted_iota(jnp.int32, (16, s_len), 1)
        sh = 1
        while sh < s_len:
            run = run + jnp.where(lane >= sh, pltpu.roll(run, sh, 1), 0.0)
            sh *= 2
        cum_ref[...] = run

    return pl.pallas_call(
        body, name="fcum",
        out_shape=[jax.ShapeDtypeStruct((16, s_len), F32), jax.ShapeDtypeStruct((16, s_len), F32)],
    )(fl, bias)


def _fbwd(dcum, sneg):
    s_len = dcum.shape[1]

    def body(dc_ref, sg_ref, dl_ref, db_ref):
        run = dc_ref[...]
        lane = lax.broadcasted_iota(jnp.int32, (16, s_len), 1)
        sh = 1
        while sh < s_len:
            run = run + jnp.where(lane < s_len - sh, pltpu.roll(run, s_len - sh, 1), 0.0)
            sh *= 2
        dlog = run * sg_ref[...]
        db_ref[...] = jnp.broadcast_to(jnp.sum(dlog, axis=1, keepdims=True), (16, 128))
        full = jnp.concatenate([dlog, jnp.zeros((112, s_len), F32)], axis=0)
        dl_ref[...] = full.T.astype(BF16)

    return pl.pallas_call(
        body, name="fbwd",
        out_shape=[jax.ShapeDtypeStruct((s_len, 128), BF16), jax.ShapeDtypeStruct((16, 128), F32)],
    )(dcum, sneg)


def _att_tile(s_len):
    return 256 if s_len >= 512 else s_len // 2


def _flash_fwd(q, k, v, cumr, tile):
    s_len, width = q.shape
    nt = s_len // tile
    reps = tile // 128

    def body(q_ref, k_ref, v_ref, cum_ref, o_ref, lse_ref, q_t, v_t, cum_col):
        pid = pl.program_id(0)
        top = lax.broadcasted_iota(jnp.int32, (HEAD_PAIR, tile), 0) < HEAD_DIM
        causal = (lax.broadcasted_iota(jnp.int32, (tile, tile), 0)
                  <= lax.broadcasted_iota(jnp.int32, (tile, tile), 1))

        def pre(i, carry):
            r0 = pl.multiple_of(i * tile, tile)
            q_t[i] = q_ref[pl.ds(r0, tile), :].astype(F32).T.astype(BF16)
            v_t[i] = v_ref[pl.ds(r0, tile), :].astype(F32).T.astype(BF16)
            for h in (0, 1):
                row = cum_ref[pl.ds((2 * pid + h) * nt + i, 1), :]
                cum_col[h, pl.ds(r0, tile), :] = jnp.broadcast_to(row, (HEAD_PAIR, tile)).T
            return carry

        lax.fori_loop(0, nt, pre, 0)

        def q_loop(qi, carry):
            qt = q_t[qi]
            zt = jnp.zeros_like(qt)
            qms = (jnp.where(top, qt, zt), jnp.where(top, zt, qt))

            def step(kj, state, masked):
                m0, l0, m1, l1, acc = state
                k0 = pl.multiple_of(kj * tile, tile)
                kt = k_ref[pl.ds(k0, tile), :]
                vt = v_t[kj]
                ms, ls, scales, contrib = [m0, m1], [l0, l1], [], None
                for h in (0, 1):
                    s = jnp.dot(kt, qms[h], preferred_element_type=F32)
                    s = s - jnp.tile(cum_col[h, pl.ds(k0, tile), :], (1, reps))
                    if masked:
                        s = jnp.where(causal, s, NEG)
                    m_new = jnp.maximum(ms[h], jnp.max(s, axis=0, keepdims=True))
                    p = jnp.exp(s - m_new)
                    scale = jnp.exp(ms[h] - m_new)
                    ls[h] = scale * ls[h] + jnp.sum(p, axis=0, keepdims=True)
                    ms[h] = m_new
                    scales.append(scale)
                    vm = jnp.where(top, vt, zt) if h == 0 else jnp.where(top, zt, vt)
                    part = jnp.dot(vm, p.astype(BF16), preferred_element_type=F32)
                    contrib = part if contrib is None else contrib + part
                acc = jnp.where(top, scales[0], scales[1]) * acc + contrib
                return ms[0], ls[0], ms[1], ls[1], acc

            low = jnp.full((1, tile), NEG, F32)
            zero = jnp.zeros((1, tile), F32)
            state = step(qi, (low, zero, low, zero, jnp.zeros((HEAD_PAIR, tile), F32)), True)
            m0, l0, m1, l1, acc = lax.fori_loop(0, qi, lambda kj, c: step(kj, c, False), state)
            q0 = pl.multiple_of(qi * tile, tile)
            o_ref[pl.ds(q0, tile), :] = (acc / jnp.where(top, l0, l1)).T
            lse_ref[pl.ds((2 * pid) * nt + qi, 1), :] = m0 + jnp.log(l0)
            lse_ref[pl.ds((2 * pid + 1) * nt + qi, 1), :] = m1 + jnp.log(l1)
            return carry

        lax.fori_loop(0, nt, q_loop, 0)

    blk = pl.BlockSpec((s_len, HEAD_PAIR), lambda p: (0, p))
    full = pl.BlockSpec(cumr.shape, lambda p: (0, 0))
    return pl.pallas_call(
        body, name="flash_fwd", grid=(width // HEAD_PAIR,),
        in_specs=[blk, blk, blk, full],
        out_specs=[blk, full],
        out_shape=[jax.ShapeDtypeStruct((s_len, width), F32), jax.ShapeDtypeStruct(cumr.shape, F32)],
        scratch_shapes=[pltpu.VMEM((nt, HEAD_PAIR, tile), BF16), pltpu.VMEM((nt, HEAD_PAIR, tile), BF16),
                        pltpu.VMEM((2, s_len, HEAD_PAIR), F32)],
        compiler_params=pltpu.CompilerParams(dimension_semantics=("arbitrary",)),
    )(q, k, v, cumr)


def _flash_bwd(q, k, v, o, do, lse, cumr, tile):
    s_len, width = q.shape
    nt = s_len // tile

    reps = tile // 128

    def body(q_ref, k_ref, v_ref, o_ref, do_ref, lse_ref, cum_ref, dq_ref, dk_ref, dv_ref, dcum_ref,
             q_t, do_t, k_t, do_bf, cum_col, dq_t_acc, delta, q_side, dk_acc, dv_acc, k_side):
        pid = pl.program_id(0)
        top = lax.broadcasted_iota(jnp.int32, (HEAD_PAIR, tile), 0) < HEAD_DIM
        left = lax.broadcasted_iota(jnp.int32, (tile, HEAD_PAIR), 1) < HEAD_DIM
        causal = (lax.broadcasted_iota(jnp.int32, (tile, tile), 0)
                  <= lax.broadcasted_iota(jnp.int32, (tile, tile), 1))

        def pre(i, carry):
            r0 = pl.multiple_of(i * tile, tile)
            d_o = do_ref[pl.ds(r0, tile), :]
            prod_t = (d_o * o_ref[pl.ds(r0, tile), :]).T
            delta[pl.ds(i, 1), :] = jnp.sum(prod_t[:HEAD_DIM], axis=0, keepdims=True)
            delta[pl.ds(nt + i, 1), :] = jnp.sum(prod_t[HEAD_DIM:], axis=0, keepdims=True)
            do_bf[pl.ds(r0, tile), :] = d_o.astype(BF16)
            do_t[i] = d_o.T.astype(BF16)
            q_t[i] = q_ref[pl.ds(r0, tile), :].astype(F32).T.astype(BF16)
            k_t[i] = k_ref[pl.ds(r0, tile), :].astype(F32).T.astype(BF16)
            dq_t_acc[i] = jnp.zeros((HEAD_PAIR, tile), F32)
            for h in (0, 1):
                row = cum_ref[pl.ds((2 * pid + h) * nt + i, 1), :]
                cum_col[h, pl.ds(r0, tile), :] = jnp.broadcast_to(row, (HEAD_PAIR, tile)).T
                q_side[pl.ds(h * nt + i, 1), :] = jnp.zeros((1, tile), F32)
            return carry

        lax.fori_loop(0, nt, pre, 0)

        def kv_loop(kj, carry):
            k0 = pl.multiple_of(kj * tile, tile)
            kt = k_ref[pl.ds(k0, tile), :]
            vt = v_ref[pl.ds(k0, tile), :]
            ktt = k_t[kj]
            zt = jnp.zeros_like(ktt)
            zr = jnp.zeros_like(kt)
            kms = (jnp.where(top, ktt, zt), jnp.where(top, zt, ktt))
            cols = [jnp.tile(cum_col[h, pl.ds(k0, tile), :], (1, reps)) for h in (0, 1)]
            dk_acc[...] = jnp.zeros_like(dk_acc)
            dv_acc[...] = jnp.zeros_like(dv_acc)
            k_side[...] = jnp.zeros_like(k_side)

            def step(qi, carry, masked):
                q0 = pl.multiple_of(qi * tile, tile)
                qtt = q_t[qi]
                dtt = do_t[qi]
                q_rows = q_ref[pl.ds(q0, tile), :]
                do_rows = do_bf[pl.ds(q0, tile), :]
                dq_part = None
                for h in (0, 1):
                    q_m = jnp.where(top, qtt, zt) if h == 0 else jnp.where(top, zt, qtt)
                    do_m = jnp.where(top, dtt, zt) if h == 0 else jnp.where(top, zt, dtt)
                    s = jnp.dot(kt, q_m, preferred_element_type=F32) - cols[h]
                    if masked:
                        s = jnp.where(causal, s, NEG)
                    p = jnp.exp(s - lse_ref[pl.ds((2 * pid + h) * nt + qi, 1), :])
                    dp = jnp.dot(vt, do_m, preferred_element_type=F32)
                    ds = p * (dp - delta[pl.ds(h * nt + qi, 1), :])
                    q_side[pl.ds(h * nt + qi, 1), :] += jnp.sum(ds, axis=0, keepdims=True)
                    folded = ds[:, :128]
                    for b in range(1, reps):
                        folded = folded + ds[:, b * 128:(b + 1) * 128]
                    k_side[h] += folded
                    pb = p.astype(BF16)
                    dsb = ds.astype(BF16)
                    do_h = jnp.where(left, do_rows, zr) if h == 0 else jnp.where(left, zr, do_rows)
                    q_h = jnp.where(left, q_rows, zr) if h == 0 else jnp.where(left, zr, q_rows)
                    dv_acc[...] += jnp.dot(pb, do_h, preferred_element_type=F32)
                    dk_acc[...] += jnp.dot(dsb, q_h, preferred_element_type=F32)
                    part = jnp.dot(kms[h], dsb, preferred_element_type=F32)
                    dq_part = part if dq_part is None else dq_part + part
                dq_t_acc[qi] += dq_part
                return carry

            step(kj, 0, True)
            lax.fori_loop(kj + 1, nt, lambda qi, c: step(qi, c, False), 0)
            dk_ref[pl.ds(k0, tile), :] = dk_acc[...].astype(BF16)
            dv_ref[pl.ds(k0, tile), :] = dv_acc[...].astype(BF16)
            for h in (0, 1):
                dcum_ref[pl.ds((2 * pid + h) * nt + kj, 1), :] = -jnp.sum(k_side[h].T, axis=0, keepdims=True)
            return carry

        lax.fori_loop(0, nt, kv_loop, 0)

        def fin(i, carry):
            r0 = pl.multiple_of(i * tile, tile)
            dq_ref[pl.ds(r0, tile), :] = dq_t_acc[i].T.astype(BF16)
            for h in (0, 1):
                dcum_ref[pl.ds((2 * pid + h) * nt + i, 1), :] += q_side[pl.ds(h * nt + i, 1), :]
            return carry

        lax.fori_loop(0, nt, fin, 0)

    blk = pl.BlockSpec((s_len, HEAD_PAIR), lambda p: (0, p))
    full = pl.BlockSpec(cumr.shape, lambda p: (0, 0))
    transposed = pltpu.VMEM((nt, HEAD_PAIR, tile), BF16)
    return pl.pallas_call(
        body, name="flash_bwd", grid=(width // HEAD_PAIR,),
        in_specs=[blk, blk, blk, blk, blk, full, full],
        out_specs=[blk, blk, blk, full],
        out_shape=[jax.ShapeDtypeStruct((s_len, width), BF16)] * 3 + [jax.ShapeDtypeStruct(cumr.shape, F32)],
        scratch_shapes=[transposed, transposed, transposed, pltpu.VMEM((s_len, HEAD_PAIR), BF16),
                        pltpu.VMEM((2, s_len, HEAD_PAIR), F32), pltpu.VMEM((nt, HEAD_PAIR, tile), F32),
                        pltpu.VMEM((2 * nt, tile), F32), pltpu.VMEM((2 * nt, tile), F32),
                        pltpu.VMEM((tile, HEAD_PAIR), F32), pltpu.VMEM((tile, HEAD_PAIR), F32),
                        pltpu.VMEM((2, tile, HEAD_PAIR), F32)],
        compiler_params=pltpu.CompilerParams(dimension_semantics=("arbitrary",)),
    )(q, k, v, o, do, lse, cumr)


def _out_ln(a, gate, x, w, g, b, name):
    s_len, d = x.shape
    tm = min(256, s_len)

    def body(a_ref, gate_ref, x_ref, w_ref, g_ref, b_ref, xn_ref, xh_ref, rs_ref, yg_ref):
        gt = gate_ref[...]
        yg = (a_ref[...] * (gt * _sigmoid(gt))).astype(BF16)
        yg_ref[...] = yg
        z = ALPHA * x_ref[...] + jnp.dot(yg, w_ref[...], preferred_element_type=F32)
        zc = z - jnp.mean(z, axis=1, keepdims=True)
        rstd = lax.rsqrt(jnp.mean(zc * zc, axis=1, keepdims=True) + LN_EPS)
        xh = zc * rstd
        xh_ref[...] = xh
        xn_ref[...] = xh * g_ref[...] + b_ref[...]
        rs_ref[...] = jnp.broadcast_to(rstd, (tm, 128))

    row = pl.BlockSpec((tm, d), lambda i: (i, 0))
    vec = pl.BlockSpec((1, d), lambda i: (0, 0))
    return pl.pallas_call(
        body, name=name, grid=(s_len // tm,),
        in_specs=[row, row, row, pl.BlockSpec(w.shape, lambda i: (0, 0)), vec, vec],
        out_specs=[row, row, pl.BlockSpec((tm, 128), lambda i: (i, 0)), row],
        out_shape=[jax.ShapeDtypeStruct((s_len, d), F32), jax.ShapeDtypeStruct((s_len, d), F32),
                   jax.ShapeDtypeStruct((s_len, 128), F32), jax.ShapeDtypeStruct((s_len, d), BF16)],
        compiler_params=pltpu.CompilerParams(dimension_semantics=("parallel",)),
    )(a, gate, x, w, g, b)


def _ln_bwd(dout, xh, rs, g, w, gate, a, name):
    s_len, d = dout.shape
    tm = min(256, s_len)

    def body(do_ref, xh_ref, rs_ref, g_ref, w_ref, gate_ref, a_ref, dz_ref, da_ref, dgate_ref, dg_ref, db_ref):
        @pl.when(pl.program_id(0) == 0)
        def _():
            dg_ref[...] = jnp.zeros_like(dg_ref)
            db_ref[...] = jnp.zeros_like(db_ref)

        dout_t = do_ref[...]
        xh_t = xh_ref[...]
        dg_ref[...] += jnp.sum(dout_t * xh_t, axis=0, keepdims=True)
        db_ref[...] += jnp.sum(dout_t, axis=0, keepdims=True)
        dxh = dout_t * g_ref[...]
        m1 = jnp.mean(dxh, axis=1, keepdims=True)
        m2 = jnp.mean(dxh * xh_t, axis=1, keepdims=True)
        dz = rs_ref[:, 0:1] * (dxh - m1 - xh_t * m2)
        dz_ref[...] = dz
        dyg = lax.dot_general(dz.astype(BF16), w_ref[...], NT_DIMS, preferred_element_type=F32)
        gt = gate_ref[...]
        sg = _sigmoid(gt)
        da_ref[...] = dyg * (gt * sg)
        dgate_ref[...] = (dyg * a_ref[...] * (sg * (1.0 + gt * (1.0 - sg)))).astype(BF16)

    row = pl.BlockSpec((tm, d), lambda i: (i, 0))
    vec = pl.BlockSpec((1, d), lambda i: (0, 0))
    return pl.pallas_call(
        body, name=name, grid=(s_len // tm,),
        in_specs=[row, row, pl.BlockSpec((tm, 128), lambda i: (i, 0)), vec, pl.BlockSpec(w.shape, lambda i: (0, 0)),
                  row, row],
        out_specs=[row, row, row, vec, vec],
        out_shape=[jax.ShapeDtypeStruct((s_len, d), F32), jax.ShapeDtypeStruct((s_len, d), F32),
                   jax.ShapeDtypeStruct((s_len, d), BF16), jax.ShapeDtypeStruct((1, d), F32),
                   jax.ShapeDtypeStruct((1, d), F32)],
        compiler_params=pltpu.CompilerParams(dimension_semantics=("arbitrary",)),
    )(dout, xh, rs, g, w, gate, a)


def _loss_grad(y, target):
    s_len, d = y.shape
    tm = min(512, s_len)

    def body(y_ref, t_ref, dy_ref, acc_ref):
        @pl.when(pl.program_id(0) == 0)
        def _():
            acc_ref[...] = jnp.zeros_like(acc_ref)

        diff = y_ref[...] - t_ref[...]
        dy_ref[...] = diff * (1.0 / d)
        acc_ref[...] += jnp.sum(diff * diff, axis=0, keepdims=True)

    row = pl.BlockSpec((tm, d), lambda i: (i, 0))
    return pl.pallas_call(
        body, name="loss_grad", grid=(s_len // tm,),
        in_specs=[row, row], out_specs=[row, pl.BlockSpec((1, d), lambda i: (0, 0))],
        out_shape=[jax.ShapeDtypeStruct((s_len, d), F32), jax.ShapeDtypeStruct((1, d), F32)],
        compiler_params=pltpu.CompilerParams(dimension_semantics=("arbitrary",)),
    )(y, target)


def _rnn_fwd(u, conv_w, conv_b, wa, ba, wi, bi, lam):
    s_len, width = u.shape
    tm = min(256, s_len)
    nb = width // RNN_BLOCK

    def body(u_ref, up_ref, cw_ref, cb_ref, wa_ref, ba_ref, wi_ref, bi_ref, lam_ref,
             h_ref, uc_ref, r_ref, i_ref, a_ref, b_scr, h_carry):
        step_id = pl.program_id(0)

        @pl.when(step_id == 0)
        def _():
            h_carry[...] = jnp.zeros_like(h_carry)

        for n in range(nb):
            blk = slice(n * RNN_BLOCK, (n + 1) * RNN_BLOCK)
            ut = u_ref[:, blk]
            prev = jnp.where(step_id > 0, up_ref[:, blk], 0.0)
            uc = cb_ref[:, blk] + cw_ref[3:4, blk] * ut
            for k in (1, 2, 3):
                uc = uc + cw_ref[3 - k:4 - k, blk] * _shift_down(ut, prev, k)
            ucb = uc.astype(BF16)
            r = _sigmoid(jnp.dot(ucb, wa_ref[n], preferred_element_type=F32) + ba_ref[:, blk])
            ig = _sigmoid(jnp.dot(ucb, wi_ref[n], preferred_element_type=F32) + bi_ref[:, blk])
            log_a = -LRU_C * r * _softplus(-lam_ref[:, blk])
            uc_ref[:, blk] = uc
            r_ref[:, blk] = r
            i_ref[:, blk] = ig
            a_ref[:, blk] = jnp.exp(log_a)
            b_scr[:, blk] = jnp.sqrt(_neg_expm1(2.0 * log_a)) * (ig * uc)

        def scan(t, h):
            h = a_ref[pl.ds(t, 1), :] * h + b_scr[pl.ds(t, 1), :]
            h_ref[pl.ds(t, 1), :] = h
            return h

        h_carry[...] = lax.fori_loop(0, tm, scan, h_carry[...], unroll=8)

    row = pl.BlockSpec((tm, width), lambda i: (i, 0))
    prev8 = pl.BlockSpec((8, width), lambda i: (jnp.maximum(i * (tm // 8) - 1, 0), 0))
    vec = pl.BlockSpec((1, width), lambda i: (0, 0))
    mat = pl.BlockSpec(wa.shape, lambda i: (0, 0, 0))
    return pl.pallas_call(
        body, name="rnn_fwd", grid=(s_len // tm,),
        in_specs=[row, prev8, pl.BlockSpec(conv_w.shape, lambda i: (0, 0)), vec, mat, vec, mat, vec, vec],
        out_specs=[row] * 5,
        out_shape=[jax.ShapeDtypeStruct((s_len, width), F32)] * 5,
        scratch_shapes=[pltpu.VMEM((tm, width), F32), pltpu.VMEM((1, width), F32)],
        compiler_params=pltpu.CompilerParams(dimension_semantics=("arbitrary",)),
    )(u, u, conv_w, conv_b, wa, ba, wi, bi, lam)


def _rnn_bwd(dh, a, h, r, ig, uc, lam, wa, wi):
    s_len, width = dh.shape
    tm = min(256, s_len)
    nt = s_len // tm
    nb = width // RNN_BLOCK

    def body(dh_ref, a_ref, h_ref, hp_ref, r_ref, i_ref, uc_ref, lam_ref, wa_ref, wi_ref,
             duc_ref, dwa_ref, dwi_ref, dba_ref, dbi_ref, dlam_ref, g_scr, c_scr, dsp_scr):
        step_id = pl.program_id(0)
        tile_id = nt - 1 - step_id

        @pl.when(step_id == 0)
        def _():
            c_scr[...] = jnp.zeros_like(c_scr)
            dsp_scr[...] = jnp.zeros_like(dsp_scr)
            dwa_ref[...] = jnp.zeros_like(dwa_ref)
            dwi_ref[...] = jnp.zeros_like(dwi_ref)
            dba_ref[...] = jnp.zeros_like(dba_ref)
            dbi_ref[...] = jnp.zeros_like(dbi_ref)

        def scan(j, c):
            t = tm - 1 - j
            g = dh_ref[pl.ds(t, 1), :] + c
            g_scr[pl.ds(t, 1), :] = g
            return a_ref[pl.ds(t, 1), :] * g

        c_scr[...] = lax.fori_loop(0, tm, scan, c_scr[...], unroll=8)

        for n in range(nb):
            blk = slice(n * RNN_BLOCK, (n + 1) * RNN_BLOCK)
            g_t = g_scr[:, blk]
            hp8 = jnp.where(tile_id > 0, hp_ref[:, blk], 0.0)
            h_prev = _shift_down(h_ref[:, blk], hp8, 1)
            av, rv, iv, ucv = a_ref[:, blk], r_ref[:, blk], i_ref[:, blk], uc_ref[:, blk]
            sp = _softplus(-lam_ref[:, blk])
            mag = jnp.sqrt(_neg_expm1(-2.0 * LRU_C * rv * sp))
            d_iu = g_t * mag
            dla = g_t * h_prev * av - g_t * (iv * ucv) * (av * av) / mag
            dsp_scr[:, blk] += jnp.sum(dla * (-LRU_C * rv), axis=0, keepdims=True)
            dra = dla * (-LRU_C * sp) * rv * (1.0 - rv)
            dia = d_iu * ucv * iv * (1.0 - iv)
            drab, diab, ucb = dra.astype(BF16), dia.astype(BF16), ucv.astype(BF16)
            duc_ref[:, blk] = (d_iu * iv
                               + lax.dot_general(drab, wa_ref[n], NT_DIMS, preferred_element_type=F32)
                               + lax.dot_general(diab, wi_ref[n], NT_DIMS, preferred_element_type=F32))
            dwa_ref[n] += lax.dot_general(ucb, drab, TN_DIMS, preferred_element_type=F32)
            dwi_ref[n] += lax.dot_general(ucb, diab, TN_DIMS, preferred_element_type=F32)
            dba_ref[:, blk] += jnp.sum(dra, axis=0, keepdims=True)
            dbi_ref[:, blk] += jnp.sum(dia, axis=0, keepdims=True)

        @pl.when(step_id == nt - 1)
        def _():
            dlam_ref[...] = -dsp_scr[...] * _sigmoid(-lam_ref[...])

    row = pl.BlockSpec((tm, width), lambda i: (nt - 1 - i, 0))
    prev8 = pl.BlockSpec((8, width), lambda i: (jnp.maximum((nt - 1 - i) * (tm // 8) - 1, 0), 0))
    vec = pl.BlockSpec((1, width), lambda i: (0, 0))
    mat = pl.BlockSpec(wa.shape, lambda i: (0, 0, 0))
    return pl.pallas_call(
        body, name="rnn_bwd", grid=(nt,),
        in_specs=[row, row, row, prev8, row, row, row, vec, mat, mat],
        out_specs=[row, mat, mat, vec, vec, vec],
        out_shape=[jax.ShapeDtypeStruct((s_len, width), F32), jax.ShapeDtypeStruct(wa.shape, F32),
                   jax.ShapeDtypeStruct(wa.shape, F32)] + [jax.ShapeDtypeStruct((1, width), F32)] * 3,
        scratch_shapes=[pltpu.VMEM((tm, width), F32), pltpu.VMEM((1, width), F32), pltpu.VMEM((1, width), F32)],
        compiler_params=pltpu.CompilerParams(dimension_semantics=("arbitrary",)),
    )(dh, a, h, h, r, ig, uc, lam, wa, wi)


def _conv_bwd(duc, u, conv_w):
    s_len, width = duc.shape
    tm = min(256, s_len)
    nt = s_len // tm

    def body(d_ref, dn_ref, u_ref, up_ref, cw_ref, du_ref, dcw_ref, dcb_ref):
        step_id = pl.program_id(0)

        @pl.when(step_id == 0)
        def _():
            dcw_ref[...] = jnp.zeros_like(dcw_ref)
            dcb_ref[...] = jnp.zeros_like(dcb_ref)

        for n in range(width // RNN_BLOCK):
            blk = slice(n * RNN_BLOCK, (n + 1) * RNN_BLOCK)
            dt = d_ref[:, blk]
            ut = u_ref[:, blk]
            nxt = jnp.where(step_id < nt - 1, dn_ref[:, blk], 0.0)
            prev = jnp.where(step_id > 0, up_ref[:, blk], 0.0)
            du = cw_ref[3:4, blk] * dt
            dcw_ref[3:4, blk] += jnp.sum(dt * ut, axis=0, keepdims=True)
            for k in (1, 2, 3):
                du = du + cw_ref[3 - k:4 - k, blk] * _shift_up(dt, nxt, k)
                dcw_ref[3 - k:4 - k, blk] += jnp.sum(dt * _shift_down(ut, prev, k), axis=0, keepdims=True)
            du_ref[:, blk] = du.astype(BF16)
            dcb_ref[:, blk] += jnp.sum(dt, axis=0, keepdims=True)

    row = pl.BlockSpec((tm, width), lambda i: (i, 0))
    prev8 = pl.BlockSpec((8, width), lambda i: (jnp.maximum(i * (tm // 8) - 1, 0), 0))
    next8 = pl.BlockSpec((8, width), lambda i: (jnp.minimum((i + 1) * (tm // 8), s_len // 8 - 1), 0))
    return pl.pallas_call(
        body, name="conv_bwd", grid=(nt,),
        in_specs=[row, next8, row, prev8, pl.BlockSpec(conv_w.shape, lambda i: (0, 0))],
        out_specs=[row, pl.BlockSpec(conv_w.shape, lambda i: (0, 0)), pl.BlockSpec((1, width), lambda i: (0, 0))],
        out_shape=[jax.ShapeDtypeStruct((s_len, width), BF16), jax.ShapeDtypeStruct(conv_w.shape, F32),
                   jax.ShapeDtypeStruct((1, width), F32)],
        compiler_params=pltpu.CompilerParams(dimension_semantics=("arbitrary",)),
    )(duc, duc, u, u, conv_w)


def _pair_sum(core, grads, theirs, out_dtype):
    n, _, half, _ = grads.shape
    tb = min(128, half)

    def body(c_ref, a_ref, b_ref, o_ref):
        o_ref[...] = (a_ref[...] + b_ref[...]).astype(out_dtype)

    blk = pl.BlockSpec((n, tb, LANES), lambda i, c: (0, i, 0))
    return pl.pallas_call(
        body, name="pair_sum",
        grid_spec=pltpu.PrefetchScalarGridSpec(
            num_scalar_prefetch=1, grid=(half // tb,),
            in_specs=[pl.BlockSpec((n, None, tb, LANES), lambda i, c: (0, c[0], i, 0)), blk], out_specs=blk),
        out_shape=jax.ShapeDtypeStruct((n, half, LANES), out_dtype),
        compiler_params=pltpu.CompilerParams(dimension_semantics=("parallel",)),
    )(core, grads, theirs)


def _sum_chips(parts):
    rows = parts.shape[1]
    tb = min(128, rows)

    def body(p_ref, o_ref):
        o_ref[...] = ((p_ref[0].astype(F32) + p_ref[1].astype(F32)) + p_ref[2].astype(F32)) + p_ref[3].astype(F32)

    return pl.pallas_call(
        body, name="chip_sum", grid=(rows // tb,),
        in_specs=[pl.BlockSpec((N_CHIPS, tb, LANES), lambda i: (0, i, 0))],
        out_specs=pl.BlockSpec((tb, LANES), lambda i: (i, 0)),
        out_shape=jax.ShapeDtypeStruct((rows, LANES), F32),
        compiler_params=pltpu.CompilerParams(dimension_semantics=("parallel",)),
    )(parts)


def _adamw(w, g, m, v, rows_per_block):
    n, rows, cols = w.shape
    blk = pl.BlockSpec((1, rows_per_block, cols), lambda i, j: (i, j, 0))

    def body(w_ref, g_ref, m_ref, v_ref, d_ref, nm_ref, nv_ref):
        gt = g_ref[...]
        nm = ADAM_B1 * m_ref[...] + (1.0 - ADAM_B1) * gt
        nv = ADAM_B2 * v_ref[...] + (1.0 - ADAM_B2) * (gt * gt)
        m_hat = nm / (1.0 - ADAM_B1 ** ADAM_STEP)
        v_hat = nv / (1.0 - ADAM_B2 ** ADAM_STEP)
        d_ref[...] = -ADAM_LR * (m_hat / (jnp.sqrt(v_hat) + ADAM_EPS) + ADAM_WD * w_ref[...])
        nm_ref[...] = nm
        nv_ref[...] = nv

    return pl.pallas_call(
        body, name="adamw", grid=(n, rows // rows_per_block), in_specs=[blk] * 4, out_specs=[blk] * 3,
        out_shape=[jax.ShapeDtypeStruct(w.shape, F32)] * 3,
        compiler_params=pltpu.CompilerParams(dimension_semantics=("parallel", "parallel")),
    )(w, g, m, v)


def _place():
    x, y, c = lax.axis_index("x"), lax.axis_index("y"), lax.axis_index("c")
    return x, y, c, [(1 - x, y), (x, 1 - y), (1 - x, 1 - y)]


ANY = pl.BlockSpec(memory_space=pl.ANY)
COPY_PARTS = 4


def _remote_parts(src_of, dst_of, rows, send_sem, recv_sem, to):
    parts = COPY_PARTS if rows % (16 * COPY_PARTS) == 0 else 1
    step = rows // parts
    for i in range(parts):
        pltpu.make_async_remote_copy(src_ref=src_of(i * step, step), dst_ref=dst_of(i * step, step),
                                     send_sem=send_sem, recv_sem=recv_sem, device_id=to, device_id_type=MESH).start()
    return pltpu.make_async_remote_copy(src_ref=src_of(0, rows), dst_ref=dst_of(0, rows), send_sem=send_sem,
                                        recv_sem=recv_sem, device_id=to, device_id_type=MESH)


def _gather_chips(shard):
    rows = shard.shape[0]
    half = rows // 2

    def body(p_ref, out_ref, send_sems, recv_sems):
        x, y, c, chips = _place()
        me = 2 * x + y

        def rows_of(chip, hc):
            return lambda r0, n: out_ref.at[chip, pl.ds(hc * half + r0, n), :]

        def mine_half(r0, n):
            return p_ref.at[pl.ds(c * half + r0, n), :]

        first = [_remote_parts(mine_half, rows_of(me, c), half, send_sems.at[j], recv_sems.at[j], (cx, cy, c))
                 for j, (cx, cy) in enumerate(chips)]
        passed = []
        for j, (cx, cy) in enumerate(chips):
            landed = rows_of(2 * cx + cy, c)
            pltpu.make_async_remote_copy(src_ref=landed(0, half), dst_ref=landed(0, half), send_sem=send_sems.at[j],
                                         recv_sem=recv_sems.at[j], device_id=(x, y, c), device_id_type=MESH).wait_recv()
            passed.append(_remote_parts(landed, landed, half, send_sems.at[3 + j], recv_sems.at[3 + j], (x, y, 1 - c)))
        for j, (cx, cy) in enumerate(chips):
            other = rows_of(2 * cx + cy, 1 - c)(0, half)
            pltpu.make_async_remote_copy(src_ref=other, dst_ref=other, send_sem=send_sems.at[3 + j],
                                         recv_sem=recv_sems.at[3 + j], device_id=(x, y, c), device_id_type=MESH).wait_recv()
        for cp in first + passed:
            cp.wait_send()

    return pl.pallas_call(
        body, name="gather_chips", in_specs=[ANY], out_specs=ANY,
        out_shape=jax.ShapeDtypeStruct((N_CHIPS,) + shard.shape, shard.dtype),
        scratch_shapes=[pltpu.SemaphoreType.DMA((6,)), pltpu.SemaphoreType.DMA((6,))],
    )(shard)


def _swap_halves(bufs):
    nb = len(bufs)

    def body(*refs):
        g_refs, t_refs, (send_sems, recv_sems) = refs[:nb], refs[nb:2 * nb], refs[2 * nb:]
        x, y, c, _ = _place()
        gives = []
        for b, (g_ref, t_ref) in enumerate(zip(g_refs, t_refs)):
            for j in range(N_CHIPS):
                k = b * N_CHIPS + j
                gives.append(_remote_parts(lambda r0, m, g_ref=g_ref, j=j: g_ref.at[j, 1 - c, pl.ds(r0, m), :],
                                           lambda r0, m, t_ref=t_ref, j=j: t_ref.at[j, pl.ds(r0, m), :],
                                           g_ref.shape[2], send_sems.at[k], recv_sems.at[k], (x, y, 1 - c)))
        for give in gives:
            give.wait()

    return pl.pallas_call(
        body, name="swap_halves", in_specs=[ANY] * nb, out_specs=[ANY] * nb,
        out_shape=[jax.ShapeDtypeStruct((b.shape[0], b.shape[2], b.shape[3]), b.dtype) for b in bufs],
        scratch_shapes=[pltpu.SemaphoreType.DMA((nb * N_CHIPS,)), pltpu.SemaphoreType.DMA((nb * N_CHIPS,))],
    )(*bufs)


def _scatter_chips(bufs):
    nb = len(bufs)

    def body(*refs):
        t_refs, o_refs, (send_sems, recv_sems) = refs[:nb], refs[nb:2 * nb], refs[2 * nb:]
        x, y, c, chips = _place()
        me = 2 * x + y

        def slot(ref, chip):
            return lambda r0, n: ref.at[chip, pl.ds(r0, n), :]

        sends = []
        for b, (t_ref, o_ref) in enumerate(zip(t_refs, o_refs)):
            for j, (cx, cy) in enumerate(chips):
                k = 3 * b + j
                sends.append(_remote_parts(slot(t_ref, 2 * cx + cy), slot(o_ref, me), t_ref.shape[1],
                                           send_sems.at[k], recv_sems.at[k], (cx, cy, c)))
        for b, o_ref in enumerate(o_refs):
            for j, (cx, cy) in enumerate(chips):
                landed = o_ref.at[2 * cx + cy]
                pltpu.make_async_remote_copy(src_ref=landed, dst_ref=landed, send_sem=send_sems.at[3 * b + j],
                                             recv_sem=recv_sems.at[3 * b + j], device_id=(x, y, c),
                                             device_id_type=MESH).wait_recv()
        for cp in sends:
            cp.wait_send()

    return pl.pallas_call(
        body, name="scatter_chips", in_specs=[ANY] * nb, out_specs=[ANY] * nb,
        out_shape=[jax.ShapeDtypeStruct(b.shape, b.dtype) for b in bufs],
        scratch_shapes=[pltpu.SemaphoreType.DMA((3 * nb,)), pltpu.SemaphoreType.DMA((3 * nb,))],
    )(*bufs)


def _give_sibling(bufs):
    nb = len(bufs)

    def body(*refs):
        h_refs, o_refs, (send_sems, recv_sems) = refs[:nb], refs[nb:2 * nb], refs[2 * nb:]
        x, y, c, _ = _place()
        gives = [_remote_parts(lambda r0, n, h_ref=h_ref: h_ref.at[pl.ds(r0, n), :],
                               lambda r0, n, o_ref=o_ref: o_ref.at[pl.ds(r0, n), :],
                               h_ref.shape[0], send_sems.at[b], recv_sems.at[b], (x, y, 1 - c))
                 for b, (h_ref, o_ref) in enumerate(zip(h_refs, o_refs))]
        for give in gives:
            give.wait()

    return pl.pallas_call(
        body, name="give_sibling", in_specs=[ANY] * nb, out_specs=[ANY] * nb,
        out_shape=[jax.ShapeDtypeStruct(b.shape, b.dtype) for b in bufs],
        scratch_shapes=[pltpu.SemaphoreType.DMA((nb,)), pltpu.SemaphoreType.DMA((nb,))],
    )(*bufs)


def _pack(arrays, dtype, row_align):
    flat = []
    for arr in arrays:
        v = arr.reshape(-1)
        flat.append(jnp.pad(v, (0, (-v.shape[0]) % LANES)))
    total = sum(f.shape[0] for f in flat) // LANES
    pad_rows = (-total) % row_align
    if pad_rows:
        flat.append(jnp.zeros((pad_rows * LANES,), dtype))
    return jnp.concatenate(flat).reshape(-1, LANES)


def _unpack(buf, shapes):
    lead = buf.shape[:-2]
    flat = buf.reshape(lead + (-1,))
    out, off = [], 0
    for shape in shapes:
        size = 1
        for dim in shape:
            size *= dim
        out.append(flat[..., off:off + size].reshape(lead + tuple(shape)))
        off += size + (-size) % LANES
    return out


def _from_chips(parts, axis):
    return jnp.concatenate([parts[j] for j in range(N_CHIPS)], axis=axis)


def kernel(x, ln_g, ln_b, attn_w_in, attn_b_f, attn_w_out, rnn_w_in, rnn_conv_w, rnn_conv_b, rnn_w_a, rnn_b_a, rnn_w_i, rnn_b_i, rnn_lambda, rnn_w_out, loss_target, m_ln_g, m_ln_b, m_attn_w_in, m_attn_b_f, m_attn_w_out, m_rnn_w_in, m_rnn_conv_w, m_rnn_conv_b, m_rnn_w_a, m_rnn_b_a, m_rnn_w_i, m_rnn_b_i, m_rnn_lambda, m_rnn_w_out, v_ln_g, v_ln_b, v_attn_w_in, v_attn_b_f, v_attn_w_out, v_rnn_w_in, v_rnn_conv_w, v_rnn_conv_b, v_rnn_w_a, v_rnn_b_a, v_rnn_w_i, v_rnn_b_i, v_rnn_lambda, v_rnn_w_out):
    s_len, d = x.shape[1], x.shape[2]
    xs = x.reshape(s_len, d)
    target = loss_target.reshape(s_len, d)
    tile = _att_tile(s_len)
    nt = s_len // tile
    heads = attn_b_f.shape[1]
    qkvg = attn_w_in.shape[2] * N_CHIPS - heads

    big = [attn_w_in, attn_w_out, rnn_w_in, rnn_w_a, rnn_w_i, rnn_w_out]
    small = [rnn_conv_w, rnn_conv_b, rnn_b_a, rnn_b_i, rnn_lambda]
    packed = _pack([w.astype(BF16) for w in big] + [lax.bitcast_convert_type(w, BF16) for w in small], BF16,
                   ROW_ALIGN)
    me = 2 * lax.axis_index("x") + lax.axis_index("y")
    core = lax.axis_index("c").astype(jnp.int32)
    gathered = lax.dynamic_update_index_in_dim(_gather_chips(packed), packed, me, 0)
    parts = _unpack(gathered, [w.shape for w in big] + [w.shape + (2,) for w in small])
    w_in_a, w_out_a, w_in_r, w_a, w_i, w_out_r = [
        _from_chips(p, ax) for p, ax in zip(parts[:6], (2, 1, 2, 2, 2, 1))]
    conv_w, conv_b, b_a, b_i, lam = [
        _from_chips(lax.bitcast_convert_type(p, F32), ax) for p, ax in zip(parts[6:], (2, 1, 1, 1, 1))]
    w_cat = jnp.concatenate(
        [w_in_a[:, :, :d] * (HEAD_DIM ** -0.5), w_in_a[:, :, d:qkvg],
         jnp.pad(w_in_a[:, :, qkvg:], ((0, 0), (0, 0), (0, 128 - heads)))], axis=2).astype(BF16)
    b_f = jnp.pad(attn_b_f, ((0, 0), (0, 128 - heads)))

    saved = []
    cur = xs
    for layer in range(DEPTH):
        idx = layer // 2
        g_l, b_l = ln_g[layer:layer + 1], ln_b[layer:layer + 1]
        if layer % 2 == 0:
            q, k, v, gate, fl = _proj(cur, w_cat[idx], [(0, d, BF16), (d, d, BF16), (2 * d, d, BF16),
                                                         (3 * d, d, F32), (4 * d, 128, F32)], "attn_proj")
            cum, sneg = _fcum(fl, b_f[idx:idx + 1])
            cumr = cum.reshape(heads * nt, tile)
            o, lse = _flash_fwd(q, k, v, cumr, tile)
            nxt, xh, rs, yg = _out_ln(o, gate, cur, w_out_a[idx], g_l, b_l, "out_ln")
            saved.append(dict(x=cur, q=q, k=k, v=v, gate=gate, cumr=cumr, sneg=sneg, a=o, lse=lse, xh=xh, rs=rs, yg=yg))
        else:
            u, gate = _proj(cur, w_in_r[idx], [(0, d, F32), (d, d, F32)], "rnn_proj")
            vecs = [t[idx:idx + 1] for t in (conv_b, b_a, b_i, lam)]
            hseq, uc, r, ig, a = _rnn_fwd(u, conv_w[idx], vecs[0], w_a[idx], vecs[1], w_i[idx], vecs[2], vecs[3])
            nxt, xh, rs, yg = _out_ln(hseq, gate, cur, w_out_r[idx], g_l, b_l, "out_ln")
            saved.append(dict(x=cur, u=u, gate=gate, uc=uc, r=r, ig=ig, av=a, a=hseq, xh=xh, rs=rs, yg=yg, lam=vecs[3]))
        cur = nxt

    dout, sq = _loss_grad(cur, target)
    loss = lax.psum(0.5 * jnp.sum(sq) / d, ("x", "y", "c"))

    g_ln_g, g_ln_b = [None] * DEPTH, [None] * DEPTH
    g_attn = [None] * 2
    g_rnn = [None] * 2
    for layer in reversed(range(DEPTH)):
        idx = layer // 2
        sv = saved[layer]
        w_out = w_out_a[idx] if layer % 2 == 0 else w_out_r[idx]
        dz, da, dgate, dg, db = _ln_bwd(dout, sv["xh"], sv["rs"], ln_g[layer:layer + 1], w_out, sv["gate"], sv["a"],
                                        "ln_bwd")
        g_ln_g[layer], g_ln_b[layer] = dg, db
        d_w_out = _mm_tn(sv["yg"], dz, "dw_out")
        if layer % 2 == 0:
            dq, dk, dv, dcumr = _flash_bwd(sv["q"], sv["k"], sv["v"], sv["a"], da, sv["lse"], sv["cumr"], tile)
            dlogit, dbf = _fbwd(dcumr.reshape(heads, s_len), sv["sneg"])
            pieces = [dq, dk, dv, dgate, dlogit]
            dout = _mm_nt(dz, [(p, j * d) for j, p in enumerate(pieces)], w_cat[idx], "attn_dx")
            dws = [_mm_tn(sv["x"], p, "dw_in") for p in pieces[:4]] + [_mm_tn(sv["x"], dlogit, "dw_f")]
            dws[0] = dws[0] * (HEAD_DIM ** -0.5)
            g_attn[idx] = dict(w_in=dws, b_f=dbf[:, 0], w_out=d_w_out)
        else:
            duc, d_wa, d_wi, d_ba, d_bi, d_lam = _rnn_bwd(da, sv["av"], sv["a"], sv["r"], sv["ig"], sv["uc"], sv["lam"],
                                                          w_a[idx], w_i[idx])
            du, d_cw, d_cb = _conv_bwd(duc, sv["u"], conv_w[idx])
            dout = _mm_nt(dz, [(du, 0), (dgate, d)], w_in_r[idx], "rnn_dx")
            d_w_in = [_mm_tn(sv["x"], du, "dw_in"), _mm_tn(sv["x"], dgate, "dw_in")]
            g_rnn[idx] = dict(w_in=d_w_in, conv_w=d_cw, conv_b=d_cb[0], w_a=d_wa, b_a=d_ba[0], w_i=d_wi, b_i=d_bi[0],
                              lam=d_lam[0], w_out=d_w_out)
    grad_x = dout.reshape(x.shape)

    def both(items, key, cut=lambda t: t):
        return [cut(it[key]) for it in items]

    def rows_of(j, n):
        return lambda t: t[j * n:(j + 1) * n]

    def cols_of(j, n):
        return lambda t: t[..., j * n:(j + 1) * n]

    rb = RNN_BLOCK // N_CHIPS
    dq4 = d // N_CHIPS
    edges = jnp.stack([jnp.stack([it["w_in"][p][:, :heads] for it in g_attn]) for p in (1, 2, 3, 4)])

    def large_for(j):
        return (both(g_attn, "w_in", lambda t: t[j]) + both(g_attn, "w_out", rows_of(j, dq4))
                + both(g_rnn, "w_in", lambda t: cols_of(j % 2, d // 2)(t[j // 2]))
                + both(g_rnn, "w_a", lambda t: t[:, j * rb:(j + 1) * rb])
                + both(g_rnn, "w_i", lambda t: t[:, j * rb:(j + 1) * rb])
                + both(g_rnn, "w_out", rows_of(j, dq4)) + [edges])

    def small_for(j):
        return ([jnp.stack(both(g_rnn, key, cols_of(j, dq4))) for key in ("conv_w", "conv_b", "b_a", "b_i", "lam")]
                + [jnp.concatenate(g_ln_g), jnp.concatenate(g_ln_b), jnp.stack(both(g_attn, "b_f"))])

    def halves(buf):
        return buf.reshape(N_CHIPS, 2, buf.shape[1] // 2, LANES)

    large = halves(jnp.stack([_pack(large_for(j), F32, ROW_ALIGN) for j in range(N_CHIPS)]))
    small = halves(jnp.stack([_pack(small_for(j), F32, 32) for j in range(N_CHIPS)]))

    core1 = core.reshape(1)
    theirs = _swap_halves([large, small])
    pair = [_pair_sum(core1, large, theirs[0], BF16), _pair_sum(core1, small, theirs[1], F32)]
    summed = []
    for mine_all, got in zip(pair, _scatter_chips(pair)):
        own = lax.dynamic_index_in_dim(mine_all, me, 0, keepdims=False)
        summed.append(_sum_chips(lax.dynamic_update_index_in_dim(got, own, me, 0)))
    reduced = [jnp.concatenate([jnp.where(core == 0, mine, other), jnp.where(core == 0, other, mine)])
               for mine, other in zip(summed, _give_sibling(summed))]
    g_in_group, g_w_out_a, g_w_in_r, g_w_a, g_w_i, g_w_out_r, g_edges = _unpack(reduced[0], [
        (2, d, d), attn_w_out.shape, rnn_w_in.shape, rnn_w_a.shape, rnn_w_i.shape, rnn_w_out.shape,
        (N_CHIPS, 2, d, heads)])
    g_conv_w, g_conv_b, g_b_a, g_b_i, g_lam, g_ln_g_sum, g_ln_b_sum, g_b_f = _unpack(reduced[1], [
        rnn_conv_w.shape, rnn_conv_b.shape, rnn_b_a.shape, rnn_b_i.shape, rnn_lambda.shape, ln_g.shape, ln_b.shape,
        attn_b_f.shape])
    wide = jnp.concatenate([g_in_group, lax.dynamic_index_in_dim(g_edges, me, 0, keepdims=False)], axis=2)
    g_w_in_a = lax.dynamic_slice(wide, (0, 0, (heads // N_CHIPS) * me), attn_w_in.shape)

    def adam_nd(w, g, m, v, view, rows_per_block):
        outs = _adamw(w.reshape(view), g.reshape(view), m.reshape(view), v.reshape(view), rows_per_block)
        return [t.reshape(w.shape) for t in outs]

    upd = {
        "attn_w_in": adam_nd(attn_w_in, g_w_in_a, m_attn_w_in, v_attn_w_in, attn_w_in.shape, 256),
        "attn_w_out": adam_nd(attn_w_out, g_w_out_a, m_attn_w_out, v_attn_w_out, attn_w_out.shape, 256),
        "rnn_w_in": adam_nd(rnn_w_in, g_w_in_r, m_rnn_w_in, v_rnn_w_in, rnn_w_in.shape, 512),
        "rnn_w_a": adam_nd(rnn_w_a, g_w_a, m_rnn_w_a, v_rnn_w_a, (1, -1, RNN_BLOCK), 512),
        "rnn_w_i": adam_nd(rnn_w_i, g_w_i, m_rnn_w_i, v_rnn_w_i, (1, -1, RNN_BLOCK), 512),
        "rnn_w_out": adam_nd(rnn_w_out, g_w_out_r, m_rnn_w_out, v_rnn_w_out, rnn_w_out.shape, 256),
    }
    smalls = [rnn_conv_w, rnn_conv_b, rnn_b_a, rnn_b_i, rnn_lambda, ln_g, ln_b, attn_b_f]
    g_small = [g_conv_w, g_conv_b, g_b_a, g_b_i, g_lam, g_ln_g_sum, g_ln_b_sum, g_b_f]
    m_small = [m_rnn_conv_w, m_rnn_conv_b, m_rnn_b_a, m_rnn_b_i, m_rnn_lambda, m_ln_g, m_ln_b, m_attn_b_f]
    v_small = [v_rnn_conv_w, v_rnn_conv_b, v_rnn_b_a, v_rnn_b_i, v_rnn_lambda, v_ln_g, v_ln_b, v_attn_b_f]
    packs = [_pack(t, F32, 16)[None] for t in (smalls, g_small, m_small, v_small)]
    small_out = [_unpack(t[0], [w.shape for w in smalls]) for t in _adamw(*packs, 16)]
    for k, name in enumerate(("rnn_conv_w", "rnn_conv_b", "rnn_b_a", "rnn_b_i", "rnn_lambda", "ln_g", "ln_b",
                              "attn_b_f")):
        upd[name] = [small_out[0][k], small_out[1][k], small_out[2][k]]
    grad = {"attn_w_in": g_w_in_a, "attn_w_out": g_w_out_a, "rnn_w_in": g_w_in_r, "rnn_w_a": g_w_a, "rnn_w_i": g_w_i,
            "rnn_w_out": g_w_out_r, "rnn_conv_w": g_conv_w, "rnn_conv_b": g_conv_b, "rnn_b_a": g_b_a,
            "rnn_b_i": g_b_i, "rnn_lambda": g_lam, "ln_g": g_ln_g_sum, "ln_b": g_ln_b_sum, "attn_b_f": g_b_f}
    names = ("ln_g", "ln_b", "attn_w_in", "attn_b_f", "attn_w_out", "rnn_w_in", "rnn_conv_w", "rnn_conv_b",
             "rnn_w_a", "rnn_b_a", "rnn_w_i", "rnn_b_i", "rnn_lambda", "rnn_w_out")
    return (loss, grad_x, *[grad[n] for n in names], *[upd[n][0] for n in names], *[upd[n][1] for n in names],
            *[upd[n][2] for n in names])
```

```python
import functools

import jax
import jax.numpy as jnp
from jax import lax
from jax.experimental import pallas as pl
from jax.experimental.pallas import tpu as pltpu

F32 = jnp.float32
BF16 = jnp.bfloat16
MESH = pl.DeviceIdType.MESH

DEPTH = 4
HEAD_DIM = 64
HEAD_PAIR = 2 * HEAD_DIM
RNN_BLOCK = 256
CONV_WIDTH = 4
LRU_C = 8.0
ALPHA = (2.0 * DEPTH) ** 0.25
LN_EPS = 1e-5
ADAM_LR, ADAM_B1, ADAM_B2, ADAM_EPS, ADAM_WD, ADAM_STEP = 0.001, 0.9, 0.999, 1e-08, 0.01, 10
N_CHIPS = 4
LANES = 1024
ROW_ALIGN = 256
NEG = -1e30

NT_DIMS = (((1,), (1,)), ((), ()))
TN_DIMS = (((0,), (0,)), ((), ()))


def _sigmoid(z):
    return 1.0 / (1.0 + jnp.exp(-z))


def _softplus(z):
    return jnp.maximum(z, 0.0) + jnp.log(1.0 + jnp.exp(-jnp.abs(z)))


def _neg_expm1(y):
    poly = y * (1.0 + y * (0.5 + y * (1.0 / 6.0 + y * (1.0 / 24.0))))
    return -jnp.where(y > -0.05, poly, jnp.exp(y) - 1.0)


def _shift_down(cur, prev8, k):
    n = cur.shape[0]
    row = lax.broadcasted_iota(jnp.int32, cur.shape, 0)
    fix = jnp.tile(pltpu.roll(prev8, k, 0), (n // 8, 1))
    return jnp.where(row < k, fix, pltpu.roll(cur, k, 0))


def _shift_up(cur, next8, k):
    n = cur.shape[0]
    row = lax.broadcasted_iota(jnp.int32, cur.shape, 0)
    fix = jnp.tile(pltpu.roll(next8, 8 - k, 0), (n // 8, 1))
    return jnp.where(row >= n - k, fix, pltpu.roll(cur, n - k, 0))


def _proj(x, w, outs, name):
    s_len, d = x.shape
    tm = min(512, s_len)

    def body(x_ref, w_ref, *o_refs):
        xb = x_ref[...].astype(BF16)
        for (c0, wd, dt), o_ref in zip(outs, o_refs):
            step = min(wd, 512)
            for cc in range(0, wd, step):
                o_ref[:, cc:cc + step] = jnp.dot(
                    xb, w_ref[:, c0 + cc:c0 + cc + step], preferred_element_type=F32).astype(dt)

    return pl.pallas_call(
        body, name=name, grid=(s_len // tm,),
        in_specs=[pl.BlockSpec((tm, d), lambda i: (i, 0)), pl.BlockSpec(w.shape, lambda i: (0, 0))],
        out_specs=[pl.BlockSpec((tm, wd), lambda i: (i, 0)) for _, wd, _ in outs],
        out_shape=[jax.ShapeDtypeStruct((s_len, wd), dt) for _, wd, dt in outs],
        compiler_params=pltpu.CompilerParams(dimension_semantics=("parallel",)),
    )(x, w)


def _mm_nt(dz, pieces, w, name):
    s_len, d = dz.shape
    tm = min(256, s_len)
    n = len(pieces)
    cols = [(c0, p.shape[1]) for p, c0 in pieces]

    def body(dz_ref, *rest):
        w_ref, o_ref = rest[n], rest[n + 1]
        acc = ALPHA * dz_ref[...]
        for (c0, wd), p_ref in zip(cols, rest[:n]):
            acc = acc + lax.dot_general(p_ref[...], w_ref[:, c0:c0 + wd], NT_DIMS, preferred_element_type=F32)
        o_ref[...] = acc

    return pl.pallas_call(
        body, name=name, grid=(s_len // tm,),
        in_specs=[pl.BlockSpec((tm, d), lambda i: (i, 0))]
        + [pl.BlockSpec((tm, wd), lambda i: (i, 0)) for _, wd in cols]
        + [pl.BlockSpec(w.shape, lambda i: (0, 0))],
        out_specs=pl.BlockSpec((tm, d), lambda i: (i, 0)),
        out_shape=jax.ShapeDtypeStruct((s_len, d), F32),
        compiler_params=pltpu.CompilerParams(dimension_semantics=("parallel",)),
    )(dz, *[p for p, _ in pieces], w)


def _mm_tn(a, b, name):
    s_len, m = a.shape
    n = b.shape[1]
    tn = min(n, 512)
    ts = min(s_len, 512)

    def body(a_ref, b_ref, o_ref):
        @pl.when(pl.program_id(1) == 0)
        def _():
            o_ref[...] = jnp.zeros_like(o_ref)

        o_ref[...] += lax.dot_general(a_ref[...].astype(BF16), b_ref[...].astype(BF16), TN_DIMS,
                                      preferred_element_type=F32)

    return pl.pallas_call(
        body, name=name, grid=(n // tn, s_len // ts),
        in_specs=[pl.BlockSpec((ts, m), lambda j, s: (s, 0)), pl.BlockSpec((ts, tn), lambda j, s: (s, j))],
        out_specs=pl.BlockSpec((m, tn), lambda j, s: (0, j)),
        out_shape=jax.ShapeDtypeStruct((m, n), F32),
        compiler_params=pltpu.CompilerParams(dimension_semantics=("parallel", "arbitrary")),
    )(a, b)


def _fcum(fl, bias):
    s_len = fl.shape[0]

    def body(fl_ref, b_ref, cum_ref, sg_ref):
        z = fl_ref[...] + b_ref[...]
        e = jnp.exp(-jnp.abs(z))
        logf = jnp.minimum(z, 0.0) - jnp.log(1.0 + e)
        sneg = jnp.where(z >= 0, e, 1.0) / (1.0 + e)
        run = logf.T[0:16, :]
        sg_ref[...] = sneg.T[0:16, :]
        lane = lax.broadcasted_iota(jnp.int32, (16, s_len), 1)
        sh = 1
        while sh < s_len:
            run = run + jnp.where(lane >= sh, pltpu.roll(run, sh, 1), 0.0)
            sh *= 2
        cum_ref[...] = run

    return pl.pallas_call(
        body, name="fcum",
        out_shape=[jax.ShapeDtypeStruct((16, s_len), F32), jax.ShapeDtypeStruct((16, s_len), F32)],
    )(fl, bias)


def _fbwd(dcum, sneg):
    s_len = dcum.shape[1]

    def body(dc_ref, sg_ref, dl_ref, db_ref):
        run = dc_ref[...]
        lane = lax.broadcasted_iota(jnp.int32, (16, s_len), 1)
        sh = 1
        while sh < s_len:
            run = run + jnp.where(lane < s_len - sh, pltpu.roll(run, s_len - sh, 1), 0.0)
            sh *= 2
        dlog = run * sg_ref[...]
        db_ref[...] = jnp.broadcast_to(jnp.sum(dlog, axis=1, keepdims=True), (16, 128))
        full = jnp.concatenate([dlog, jnp.zeros((112, s_len), F32)], axis=0)
        dl_ref[...] = full.T.astype(BF16)

    return pl.pallas_call(
        body, name="fbwd",
        out_shape=[jax.ShapeDtypeStruct((s_len, 128), BF16), jax.ShapeDtypeStruct((16, 128), F32)],
    )(dcum, sneg)


FWD_TILE = 1024
BWD_TILE = 512


def _flash_fwd(q, k, v, cum):
    s_len, width = q.shape
    tile = min(FWD_TILE, s_len // 2)
    nt = s_len // tile
    reps = tile // 128
    cumr = cum.reshape(-1, tile)

    def body(q_ref, k_ref, v_ref, cum_ref, o_ref, lse_ref, q_t, v_t, cum_col):
        pid = pl.program_id(0)
        top = lax.broadcasted_iota(jnp.int32, (HEAD_PAIR, tile), 0) < HEAD_DIM
        causal = (lax.broadcasted_iota(jnp.int32, (tile, tile), 0)
                  <= lax.broadcasted_iota(jnp.int32, (tile, tile), 1))

        def pre(i, carry):
            r0 = pl.multiple_of(i * tile, tile)
            q_t[i] = q_ref[pl.ds(r0, tile), :].astype(F32).T.astype(BF16)
            v_t[i] = v_ref[pl.ds(r0, tile), :].astype(F32).T.astype(BF16)
            for h in (0, 1):
                row = cum_ref[pl.ds((2 * pid + h) * nt + i, 1), :]
                cum_col[h, pl.ds(r0, tile), :] = jnp.broadcast_to(row, (HEAD_PAIR, tile)).T
            return carry

        lax.fori_loop(0, nt, pre, 0)

        def q_loop(qi, carry):
            qt = q_t[qi]
            zt = jnp.zeros_like(qt)
            qms = (jnp.where(top, qt, zt), jnp.where(top, zt, qt))

            def step(kj, state, masked):
                m0, l0, m1, l1, acc = state
                k0 = pl.multiple_of(kj * tile, tile)
                kt = k_ref[pl.ds(k0, tile), :]
                vt = v_t[kj]
                ms, ls, scales, contrib = [m0, m1], [l0, l1], [], None
                for h in (0, 1):
                    s = jnp.dot(kt, qms[h], preferred_element_type=F32)
                    s = s - jnp.tile(cum_col[h, pl.ds(k0, tile), :], (1, reps))
                    if masked:
                        s = jnp.where(causal, s, NEG)
                    m_new = jnp.maximum(ms[h], jnp.max(s, axis=0, keepdims=True))
                    p = jnp.exp(s - m_new)
                    scale = jnp.exp(ms[h] - m_new)
                    ls[h] = scale * ls[h] + jnp.sum(p, axis=0, keepdims=True)
                    ms[h] = m_new
                    scales.append(scale)
                    vm = jnp.where(top, vt, zt) if h == 0 else jnp.where(top, zt, vt)
                    part = jnp.dot(vm, p.astype(BF16), preferred_element_type=F32)
                    contrib = part if contrib is None else contrib + part
                acc = jnp.where(top, scales[0], scales[1]) * acc + contrib
                return ms[0], ls[0], ms[1], ls[1], acc

            low = jnp.full((1, tile), NEG, F32)
            zero = jnp.zeros((1, tile), F32)
            state = step(qi, (low, zero, low, zero, jnp.zeros((HEAD_PAIR, tile), F32)), True)
            m0, l0, m1, l1, acc = lax.fori_loop(0, qi, lambda kj, c: step(kj, c, False), state)
            q0 = pl.multiple_of(qi * tile, tile)
            o_ref[pl.ds(q0, tile), :] = (acc / jnp.where(top, l0, l1)).T
            lse_ref[pl.ds((2 * pid) * nt + qi, 1), :] = m0 + jnp.log(l0)
            lse_ref[pl.ds((2 * pid + 1) * nt + qi, 1), :] = m1 + jnp.log(l1)
            return carry

        lax.fori_loop(0, nt, q_loop, 0)

    blk = pl.BlockSpec((s_len, HEAD_PAIR), lambda p: (0, p))
    full = pl.BlockSpec(cumr.shape, lambda p: (0, 0))
    o, lse = pl.pallas_call(
        body, name="flash_fwd", grid=(width // HEAD_PAIR,),
        in_specs=[blk, blk, blk, full],
        out_specs=[blk, full],
        out_shape=[jax.ShapeDtypeStruct((s_len, width), F32), jax.ShapeDtypeStruct(cumr.shape, F32)],
        scratch_shapes=[pltpu.VMEM((nt, HEAD_PAIR, tile), BF16), pltpu.VMEM((nt, HEAD_PAIR, tile), BF16),
                        pltpu.VMEM((2, s_len, HEAD_PAIR), F32)],
        compiler_params=pltpu.CompilerParams(dimension_semantics=("arbitrary",)),
    )(q, k, v, cumr)
    return o, lse.reshape(cum.shape)


def _flash_bwd(q, k, v, o, do, lse, cum):
    s_len, width = q.shape
    tile = min(BWD_TILE, s_len // 2)
    nt = s_len // tile
    reps = tile // 128
    cumr = cum.reshape(-1, tile)
    lse = lse.reshape(-1, tile)

    def body(q_ref, k_ref, v_ref, o_ref, do_ref, lse_ref, cum_ref, dq_ref, dk_ref, dv_ref, dcum_ref,
             q_t, do_t, k_t, do_bf, cum_col, dq_t_acc, delta, q_side, dk_acc, dv_acc, k_side):
        pid = pl.program_id(0)
        top = lax.broadcasted_iota(jnp.int32, (HEAD_PAIR, tile), 0) < HEAD_DIM
        left = lax.broadcasted_iota(jnp.int32, (tile, HEAD_PAIR), 1) < HEAD_DIM
        causal = (lax.broadcasted_iota(jnp.int32, (tile, tile), 0)
                  <= lax.broadcasted_iota(jnp.int32, (tile, tile), 1))

        def pre(i, carry):
            r0 = pl.multiple_of(i * tile, tile)
            d_o = do_ref[pl.ds(r0, tile), :]
            prod_t = (d_o * o_ref[pl.ds(r0, tile), :]).T
            delta[pl.ds(i, 1), :] = jnp.sum(prod_t[:HEAD_DIM], axis=0, keepdims=True)
            delta[pl.ds(nt + i, 1), :] = jnp.sum(prod_t[HEAD_DIM:], axis=0, keepdims=True)
            do_bf[pl.ds(r0, tile), :] = d_o.astype(BF16)
            do_t[i] = d_o.T.astype(BF16)
            q_t[i] = q_ref[pl.ds(r0, tile), :].astype(F32).T.astype(BF16)
            k_t[i] = k_ref[pl.ds(r0, tile), :].astype(F32).T.astype(BF16)
            dq_t_acc[i] = jnp.zeros((HEAD_PAIR, tile), F32)
            for h in (0, 1):
                row = cum_ref[pl.ds((2 * pid + h) * nt + i, 1), :]
                cum_col[h, pl.ds(r0, tile), :] = jnp.broadcast_to(row, (HEAD_PAIR, tile)).T
                q_side[pl.ds(h * nt + i, 1), :] = jnp.zeros((1, tile), F32)
            return carry

        lax.fori_loop(0, nt, pre, 0)

        def kv_loop(kj, carry):
            k0 = pl.multiple_of(kj * tile, tile)
            kt = k_ref[pl.ds(k0, tile), :]
            vt = v_ref[pl.ds(k0, tile), :]
            ktt = k_t[kj]
            zt = jnp.zeros_like(ktt)
            zr = jnp.zeros_like(kt)
            kms = (jnp.where(top, ktt, zt), jnp.where(top, zt, ktt))
            cols = [jnp.tile(cum_col[h, pl.ds(k0, tile), :], (1, reps)) for h in (0, 1)]
            dk_acc[...] = jnp.zeros_like(dk_acc)
            dv_acc[...] = jnp.zeros_like(dv_acc)
            k_side[...] = jnp.zeros_like(k_side)

            def step(qi, carry, masked):
                q0 = pl.multiple_of(qi * tile, tile)
                qtt = q_t[qi]
                dtt = do_t[qi]
                q_rows = q_ref[pl.ds(q0, tile), :]
                do_rows = do_bf[pl.ds(q0, tile), :]
                dq_part = None
                for h in (0, 1):
                    q_m = jnp.where(top, qtt, zt) if h == 0 else jnp.where(top, zt, qtt)
                    do_m = jnp.where(top, dtt, zt) if h == 0 else jnp.where(top, zt, dtt)
                    s = jnp.dot(kt, q_m, preferred_element_type=F32) - cols[h]
                    if masked:
                        s = jnp.where(causal, s, NEG)
                    p = jnp.exp(s - lse_ref[pl.ds((2 * pid + h) * nt + qi, 1), :])
                    dp = jnp.dot(vt, do_m, preferred_element_type=F32)
                    ds = p * (dp - delta[pl.ds(h * nt + qi, 1), :])
                    q_side[pl.ds(h * nt + qi, 1), :] += jnp.sum(ds, axis=0, keepdims=True)
                    folded = ds[:, :128]
                    for b in range(1, reps):
                        folded = folded + ds[:, b * 128:(b + 1) * 128]
                    k_side[h] += folded
                    pb = p.astype(BF16)
                    dsb = ds.astype(BF16)
                    do_h = jnp.where(left, do_rows, zr) if h == 0 else jnp.where(left, zr, do_rows)
                    q_h = jnp.where(left, q_rows, zr) if h == 0 else jnp.where(left, zr, q_rows)
                    dv_acc[...] += jnp.dot(pb, do_h, preferred_element_type=F32)
                    dk_acc[...] += jnp.dot(dsb, q_h, preferred_element_type=F32)
                    part = jnp.dot(kms[h], dsb, preferred_element_type=F32)
                    dq_part = part if dq_part is None else dq_part + part
                dq_t_acc[qi] += dq_part
                return carry

            step(kj, 0, True)
            lax.fori_loop(kj + 1, nt, lambda qi, c: step(qi, c, False), 0)
            dk_ref[pl.ds(k0, tile), :] = dk_acc[...].astype(BF16)
            dv_ref[pl.ds(k0, tile), :] = dv_acc[...].astype(BF16)
            for h in (0, 1):
                dcum_ref[pl.ds((2 * pid + h) * nt + kj, 1), :] = -jnp.sum(k_side[h].T, axis=0, keepdims=True)
            return carry

        lax.fori_loop(0, nt, kv_loop, 0)

        def fin(i, carry):
            r0 = pl.multiple_of(i * tile, tile)
            dq_ref[pl.ds(r0, tile), :] = dq_t_acc[i].T.astype(BF16)
            for h in (0, 1):
                dcum_ref[pl.ds((2 * pid + h) * nt + i, 1), :] += q_side[pl.ds(h * nt + i, 1), :]
            return carry

        lax.fori_loop(0, nt, fin, 0)

    blk = pl.BlockSpec((s_len, HEAD_PAIR), lambda p: (0, p))
    full = pl.BlockSpec(cumr.shape, lambda p: (0, 0))
    transposed = pltpu.VMEM((nt, HEAD_PAIR, tile), BF16)
    dq, dk, dv, dcum = pl.pallas_call(
        body, name="flash_bwd", grid=(width // HEAD_PAIR,),
        in_specs=[blk, blk, blk, blk, blk, full, full],
        out_specs=[blk, blk, blk, full],
        out_shape=[jax.ShapeDtypeStruct((s_len, width), BF16)] * 3 + [jax.ShapeDtypeStruct(cumr.shape, F32)],
        scratch_shapes=[transposed, transposed, transposed, pltpu.VMEM((s_len, HEAD_PAIR), BF16),
                        pltpu.VMEM((2, s_len, HEAD_PAIR), F32), pltpu.VMEM((nt, HEAD_PAIR, tile), F32),
                        pltpu.VMEM((2 * nt, tile), F32), pltpu.VMEM((2 * nt, tile), F32),
                        pltpu.VMEM((tile, HEAD_PAIR), F32), pltpu.VMEM((tile, HEAD_PAIR), F32),
                        pltpu.VMEM((2, tile, HEAD_PAIR), F32)],
        compiler_params=pltpu.CompilerParams(dimension_semantics=("arbitrary",)),
    )(q, k, v, o, do, lse, cumr)
    return dq, dk, dv, dcum.reshape(cum.shape)


def _out_ln(a, gate, x, w, g, b, name):
    s_len, d = x.shape
    tm = min(256, s_len)

    def body(a_ref, gate_ref, x_ref, w_ref, g_ref, b_ref, xn_ref, xh_ref, rs_ref, yg_ref):
        gt = gate_ref[...]
        yg = (a_ref[...] * (gt * _sigmoid(gt))).astype(BF16)
        yg_ref[...] = yg
        z = ALPHA * x_ref[...] + jnp.dot(yg, w_ref[...], preferred_element_type=F32)
        zc = z - jnp.mean(z, axis=1, keepdims=True)
        rstd = lax.rsqrt(jnp.mean(zc * zc, axis=1, keepdims=True) + LN_EPS)
        xh = zc * rstd
        xh_ref[...] = xh
        xn_ref[...] = xh * g_ref[...] + b_ref[...]
        rs_ref[...] = jnp.broadcast_to(rstd, (tm, 128))

    row = pl.BlockSpec((tm, d), lambda i: (i, 0))
    vec = pl.BlockSpec((1, d), lambda i: (0, 0))
    return pl.pallas_call(
        body, name=name, grid=(s_len // tm,),
        in_specs=[row, row, row, pl.BlockSpec(w.shape, lambda i: (0, 0)), vec, vec],
        out_specs=[row, row, pl.BlockSpec((tm, 128), lambda i: (i, 0)), row],
        out_shape=[jax.ShapeDtypeStruct((s_len, d), F32), jax.ShapeDtypeStruct((s_len, d), F32),
                   jax.ShapeDtypeStruct((s_len, 128), F32), jax.ShapeDtypeStruct((s_len, d), BF16)],
        compiler_params=pltpu.CompilerParams(dimension_semantics=("parallel",)),
    )(a, gate, x, w, g, b)


def _ln_bwd(dout, xh, rs, g, w, gate, a, name):
    s_len, d = dout.shape
    tm = min(256, s_len)

    def body(do_ref, xh_ref, rs_ref, g_ref, w_ref, gate_ref, a_ref, dz_ref, da_ref, dgate_ref, dg_ref, db_ref):
        @pl.when(pl.program_id(0) == 0)
        def _():
            dg_ref[...] = jnp.zeros_like(dg_ref)
            db_ref[...] = jnp.zeros_like(db_ref)

        dout_t = do_ref[...]
        xh_t = xh_ref[...]
        dg_ref[...] += jnp.sum(dout_t * xh_t, axis=0, keepdims=True)
        db_ref[...] += jnp.sum(dout_t, axis=0, keepdims=True)
        dxh = dout_t * g_ref[...]
        m1 = jnp.mean(dxh, axis=1, keepdims=True)
        m2 = jnp.mean(dxh * xh_t, axis=1, keepdims=True)
        dz = rs_ref[:, 0:1] * (dxh - m1 - xh_t * m2)
        dz_ref[...] = dz
        dyg = lax.dot_general(dz.astype(BF16), w_ref[...], NT_DIMS, preferred_element_type=F32)
        gt = gate_ref[...]
        sg = _sigmoid(gt)
        da_ref[...] = dyg * (gt * sg)
        dgate_ref[...] = (dyg * a_ref[...] * (sg * (1.0 + gt * (1.0 - sg)))).astype(BF16)

    row = pl.BlockSpec((tm, d), lambda i: (i, 0))
    vec = pl.BlockSpec((1, d), lambda i: (0, 0))
    return pl.pallas_call(
        body, name=name, grid=(s_len // tm,),
        in_specs=[row, row, pl.BlockSpec((tm, 128), lambda i: (i, 0)), vec, pl.BlockSpec(w.shape, lambda i: (0, 0)),
                  row, row],
        out_specs=[row, row, row, vec, vec],
        out_shape=[jax.ShapeDtypeStruct((s_len, d), F32), jax.ShapeDtypeStruct((s_len, d), F32),
                   jax.ShapeDtypeStruct((s_len, d), BF16), jax.ShapeDtypeStruct((1, d), F32),
                   jax.ShapeDtypeStruct((1, d), F32)],
        compiler_params=pltpu.CompilerParams(dimension_semantics=("arbitrary",)),
    )(dout, xh, rs, g, w, gate, a)


def _loss_grad(y, target):
    s_len, d = y.shape
    tm = min(512, s_len)

    def body(y_ref, t_ref, dy_ref, acc_ref):
        @pl.when(pl.program_id(0) == 0)
        def _():
            acc_ref[...] = jnp.zeros_like(acc_ref)

        diff = y_ref[...] - t_ref[...]
        dy_ref[...] = diff * (1.0 / d)
        acc_ref[...] += jnp.sum(diff * diff, axis=0, keepdims=True)

    row = pl.BlockSpec((tm, d), lambda i: (i, 0))
    return pl.pallas_call(
        body, name="loss_grad", grid=(s_len // tm,),
        in_specs=[row, row], out_specs=[row, pl.BlockSpec((1, d), lambda i: (0, 0))],
        out_shape=[jax.ShapeDtypeStruct((s_len, d), F32), jax.ShapeDtypeStruct((1, d), F32)],
        compiler_params=pltpu.CompilerParams(dimension_semantics=("arbitrary",)),
    )(y, target)


def _rnn_fwd(u, conv_w, conv_b, wa, ba, wi, bi, lam):
    s_len, width = u.shape
    tm = min(256, s_len)
    nb = width // RNN_BLOCK

    def body(u_ref, up_ref, cw_ref, cb_ref, wa_ref, ba_ref, wi_ref, bi_ref, lam_ref,
             h_ref, uc_ref, r_ref, i_ref, a_ref, b_scr, h_carry):
        step_id = pl.program_id(0)

        @pl.when(step_id == 0)
        def _():
            h_carry[...] = jnp.zeros_like(h_carry)

        for n in range(nb):
            blk = slice(n * RNN_BLOCK, (n + 1) * RNN_BLOCK)
            ut = u_ref[:, blk]
            prev = jnp.where(step_id > 0, up_ref[:, blk], 0.0)
            uc = cb_ref[:, blk] + cw_ref[3:4, blk] * ut
            for k in (1, 2, 3):
                uc = uc + cw_ref[3 - k:4 - k, blk] * _shift_down(ut, prev, k)
            ucb = uc.astype(BF16)
            r = _sigmoid(jnp.dot(ucb, wa_ref[n], preferred_element_type=F32) + ba_ref[:, blk])
            ig = _sigmoid(jnp.dot(ucb, wi_ref[n], preferred_element_type=F32) + bi_ref[:, blk])
            log_a = -LRU_C * r * _softplus(-lam_ref[:, blk])
            uc_ref[:, blk] = uc
            r_ref[:, blk] = r
            i_ref[:, blk] = ig
            a_ref[:, blk] = jnp.exp(log_a)
            b_scr[:, blk] = jnp.sqrt(_neg_expm1(2.0 * log_a)) * (ig * uc)

        def scan(t, h):
            h = a_ref[pl.ds(t, 1), :] * h + b_scr[pl.ds(t, 1), :]
            h_ref[pl.ds(t, 1), :] = h
            return h

        h_carry[...] = lax.fori_loop(0, tm, scan, h_carry[...], unroll=8)

    row = pl.BlockSpec((tm, width), lambda i: (i, 0))
    prev8 = pl.BlockSpec((8, width), lambda i: (jnp.maximum(i * (tm // 8) - 1, 0), 0))
    vec = pl.BlockSpec((1, width), lambda i: (0, 0))
    mat = pl.BlockSpec(wa.shape, lambda i: (0, 0, 0))
    return pl.pallas_call(
        body, name="rnn_fwd", grid=(s_len // tm,),
        in_specs=[row, prev8, pl.BlockSpec(conv_w.shape, lambda i: (0, 0)), vec, mat, vec, mat, vec, vec],
        out_specs=[row] * 5,
        out_shape=[jax.ShapeDtypeStruct((s_len, width), F32)] * 5,
        scratch_shapes=[pltpu.VMEM((tm, width), F32), pltpu.VMEM((1, width), F32)],
        compiler_params=pltpu.CompilerParams(dimension_semantics=("arbitrary",)),
    )(u, u, conv_w, conv_b, wa, ba, wi, bi, lam)


def _rnn_bwd(dh, a, h, r, ig, uc, lam, wa, wi):
    s_len, width = dh.shape
    tm = min(256, s_len)
    nt = s_len // tm
    nb = width // RNN_BLOCK

    def body(dh_ref, a_ref, h_ref, hp_ref, r_ref, i_ref, uc_ref, lam_ref, wa_ref, wi_ref,
             duc_ref, dwa_ref, dwi_ref, dba_ref, dbi_ref, dlam_ref, g_scr, c_scr, dsp_scr):
        step_id = pl.program_id(0)
        tile_id = nt - 1 - step_id

        @pl.when(step_id == 0)
        def _():
            c_scr[...] = jnp.zeros_like(c_scr)
            dsp_scr[...] = jnp.zeros_like(dsp_scr)
            dwa_ref[...] = jnp.zeros_like(dwa_ref)
            dwi_ref[...] = jnp.zeros_like(dwi_ref)
            dba_ref[...] = jnp.zeros_like(dba_ref)
            dbi_ref[...] = jnp.zeros_like(dbi_ref)

        def scan(j, c):
            t = tm - 1 - j
            g = dh_ref[pl.ds(t, 1), :] + c
            g_scr[pl.ds(t, 1), :] = g
            return a_ref[pl.ds(t, 1), :] * g

        c_scr[...] = lax.fori_loop(0, tm, scan, c_scr[...], unroll=8)

        for n in range(nb):
            blk = slice(n * RNN_BLOCK, (n + 1) * RNN_BLOCK)
            g_t = g_scr[:, blk]
            hp8 = jnp.where(tile_id > 0, hp_ref[:, blk], 0.0)
            h_prev = _shift_down(h_ref[:, blk], hp8, 1)
            av, rv, iv, ucv = a_ref[:, blk], r_ref[:, blk], i_ref[:, blk], uc_ref[:, blk]
            sp = _softplus(-lam_ref[:, blk])
            mag = jnp.sqrt(_neg_expm1(-2.0 * LRU_C * rv * sp))
            d_iu = g_t * mag
            dla = g_t * h_prev * av - g_t * (iv * ucv) * (av * av) / mag
            dsp_scr[:, blk] += jnp.sum(dla * (-LRU_C * rv), axis=0, keepdims=True)
            dra = dla * (-LRU_C * sp) * rv * (1.0 - rv)
            dia = d_iu * ucv * iv * (1.0 - iv)
            drab, diab, ucb = dra.astype(BF16), dia.astype(BF16), ucv.astype(BF16)
            duc_ref[:, blk] = (d_iu * iv
                               + lax.dot_general(drab, wa_ref[n], NT_DIMS, preferred_element_type=F32)
                               + lax.dot_general(diab, wi_ref[n], NT_DIMS, preferred_element_type=F32))
            dwa_ref[n] += lax.dot_general(ucb, drab, TN_DIMS, preferred_element_type=F32)
            dwi_ref[n] += lax.dot_general(ucb, diab, TN_DIMS, preferred_element_type=F32)
            dba_ref[:, blk] += jnp.sum(dra, axis=0, keepdims=True)
            dbi_ref[:, blk] += jnp.sum(dia, axis=0, keepdims=True)

        @pl.when(step_id == nt - 1)
        def _():
            dlam_ref[...] = -dsp_scr[...] * _sigmoid(-lam_ref[...])

    row = pl.BlockSpec((tm, width), lambda i: (nt - 1 - i, 0))
    prev8 = pl.BlockSpec((8, width), lambda i: (jnp.maximum((nt - 1 - i) * (tm // 8) - 1, 0), 0))
    vec = pl.BlockSpec((1, width), lambda i: (0, 0))
    mat = pl.BlockSpec(wa.shape, lambda i: (0, 0, 0))
    return pl.pallas_call(
        body, name="rnn_bwd", grid=(nt,),
        in_specs=[row, row, row, prev8, row, row, row, vec, mat, mat],
        out_specs=[row, mat, mat, vec, vec, vec],
        out_shape=[jax.ShapeDtypeStruct((s_len, width), F32), jax.ShapeDtypeStruct(wa.shape, F32),
                   jax.ShapeDtypeStruct(wa.shape, F32)] + [jax.ShapeDtypeStruct((1, width), F32)] * 3,
        scratch_shapes=[pltpu.VMEM((tm, width), F32), pltpu.VMEM((1, width), F32), pltpu.VMEM((1, width), F32)],
        compiler_params=pltpu.CompilerParams(dimension_semantics=("arbitrary",)),
    )(dh, a, h, h, r, ig, uc, lam, wa, wi)


def _conv_bwd(duc, u, conv_w):
    s_len, width = duc.shape
    tm = min(256, s_len)
    nt = s_len // tm

    def body(d_ref, dn_ref, u_ref, up_ref, cw_ref, du_ref, dcw_ref, dcb_ref):
        step_id = pl.program_id(0)

        @pl.when(step_id == 0)
        def _():
            dcw_ref[...] = jnp.zeros_like(dcw_ref)
            dcb_ref[...] = jnp.zeros_like(dcb_ref)

        for n in range(width // RNN_BLOCK):
            blk = slice(n * RNN_BLOCK, (n + 1) * RNN_BLOCK)
            dt = d_ref[:, blk]
            ut = u_ref[:, blk]
            nxt = jnp.where(step_id < nt - 1, dn_ref[:, blk], 0.0)
            prev = jnp.where(step_id > 0, up_ref[:, blk], 0.0)
            du = cw_ref[3:4, blk] * dt
            dcw_ref[3:4, blk] += jnp.sum(dt * ut, axis=0, keepdims=True)
            for k in (1, 2, 3):
                du = du + cw_ref[3 - k:4 - k, blk] * _shift_up(dt, nxt, k)
                dcw_ref[3 - k:4 - k, blk] += jnp.sum(dt * _shift_down(ut, prev, k), axis=0, keepdims=True)
            du_ref[:, blk] = du.astype(BF16)
            dcb_ref[:, blk] += jnp.sum(dt, axis=0, keepdims=True)

    row = pl.BlockSpec((tm, width), lambda i: (i, 0))
    prev8 = pl.BlockSpec((8, width), lambda i: (jnp.maximum(i * (tm // 8) - 1, 0), 0))
    next8 = pl.BlockSpec((8, width), lambda i: (jnp.minimum((i + 1) * (tm // 8), s_len // 8 - 1), 0))
    return pl.pallas_call(
        body, name="conv_bwd", grid=(nt,),
        in_specs=[row, next8, row, prev8, pl.BlockSpec(conv_w.shape, lambda i: (0, 0))],
        out_specs=[row, pl.BlockSpec(conv_w.shape, lambda i: (0, 0)), pl.BlockSpec((1, width), lambda i: (0, 0))],
        out_shape=[jax.ShapeDtypeStruct((s_len, width), BF16), jax.ShapeDtypeStruct(conv_w.shape, F32),
                   jax.ShapeDtypeStruct((1, width), F32)],
        compiler_params=pltpu.CompilerParams(dimension_semantics=("arbitrary",)),
    )(duc, duc, u, u, conv_w)


def _pair_sum(core, grads, theirs, out_dtype):
    n, _, half, _ = grads.shape
    tb = min(128, half)

    def body(c_ref, a_ref, b_ref, o_ref):
        o_ref[...] = (a_ref[...] + b_ref[...]).astype(out_dtype)

    blk = pl.BlockSpec((n, tb, LANES), lambda i, c: (0, i, 0))
    return pl.pallas_call(
        body, name="pair_sum",
        grid_spec=pltpu.PrefetchScalarGridSpec(
            num_scalar_prefetch=1, grid=(half // tb,),
            in_specs=[pl.BlockSpec((n, None, tb, LANES), lambda i, c: (0, c[0], i, 0)), blk], out_specs=blk),
        out_shape=jax.ShapeDtypeStruct((n, half, LANES), out_dtype),
        compiler_params=pltpu.CompilerParams(dimension_semantics=("parallel",)),
    )(core, grads, theirs)


def _sum_chips(parts):
    rows = parts.shape[1]
    tb = min(128, rows)

    def body(p_ref, o_ref):
        o_ref[...] = ((p_ref[0].astype(F32) + p_ref[1].astype(F32)) + p_ref[2].astype(F32)) + p_ref[3].astype(F32)

    return pl.pallas_call(
        body, name="chip_sum", grid=(rows // tb,),
        in_specs=[pl.BlockSpec((N_CHIPS, tb, LANES), lambda i: (0, i, 0))],
        out_specs=pl.BlockSpec((tb, LANES), lambda i: (i, 0)),
        out_shape=jax.ShapeDtypeStruct((rows, LANES), F32),
        compiler_params=pltpu.CompilerParams(dimension_semantics=("parallel",)),
    )(parts)


def _adamw(w, g, m, v, rows_per_block):
    n, rows, cols = w.shape
    blk = pl.BlockSpec((1, rows_per_block, cols), lambda i, j: (i, j, 0))

    def body(w_ref, g_ref, m_ref, v_ref, d_ref, nm_ref, nv_ref):
        gt = g_ref[...]
        nm = ADAM_B1 * m_ref[...] + (1.0 - ADAM_B1) * gt
        nv = ADAM_B2 * v_ref[...] + (1.0 - ADAM_B2) * (gt * gt)
        m_hat = nm / (1.0 - ADAM_B1 ** ADAM_STEP)
        v_hat = nv / (1.0 - ADAM_B2 ** ADAM_STEP)
        d_ref[...] = -ADAM_LR * (m_hat / (jnp.sqrt(v_hat) + ADAM_EPS) + ADAM_WD * w_ref[...])
        nm_ref[...] = nm
        nv_ref[...] = nv

    return pl.pallas_call(
        body, name="adamw", grid=(n, rows // rows_per_block), in_specs=[blk] * 4, out_specs=[blk] * 3,
        out_shape=[jax.ShapeDtypeStruct(w.shape, F32)] * 3,
        compiler_params=pltpu.CompilerParams(dimension_semantics=("parallel", "parallel")),
    )(w, g, m, v)


def _place():
    x, y, c = lax.axis_index("x"), lax.axis_index("y"), lax.axis_index("c")
    return x, y, c, [(1 - x, y), (x, 1 - y), (1 - x, 1 - y)]


ANY = pl.BlockSpec(memory_space=pl.ANY)
COPY_PARTS = 4


def _remote_parts(src_of, dst_of, rows, send_sem, recv_sem, to):
    parts = COPY_PARTS if rows % (16 * COPY_PARTS) == 0 else 1
    step = rows // parts
    for i in range(parts):
        pltpu.make_async_remote_copy(src_ref=src_of(i * step, step), dst_ref=dst_of(i * step, step),
                                     send_sem=send_sem, recv_sem=recv_sem, device_id=to, device_id_type=MESH).start()
    return pltpu.make_async_remote_copy(src_ref=src_of(0, rows), dst_ref=dst_of(0, rows), send_sem=send_sem,
                                        recv_sem=recv_sem, device_id=to, device_id_type=MESH)


def _gather_chips(shard):
    rows = shard.shape[0]
    half = rows // 2

    def body(p_ref, out_ref, send_sems, recv_sems):
        x, y, c, chips = _place()
        me = 2 * x + y

        def rows_of(chip, hc):
            return lambda r0, n: out_ref.at[chip, pl.ds(hc * half + r0, n), :]

        def mine_half(r0, n):
            return p_ref.at[pl.ds(c * half + r0, n), :]

        first = [_remote_parts(mine_half, rows_of(me, c), half, send_sems.at[j], recv_sems.at[j], (cx, cy, c))
                 for j, (cx, cy) in enumerate(chips)]
        passed = []
        for j, (cx, cy) in enumerate(chips):
            landed = rows_of(2 * cx + cy, c)
            pltpu.make_async_remote_copy(src_ref=landed(0, half), dst_ref=landed(0, half), send_sem=send_sems.at[j],
                                         recv_sem=recv_sems.at[j], device_id=(x, y, c), device_id_type=MESH).wait_recv()
            passed.append(_remote_parts(landed, landed, half, send_sems.at[3 + j], recv_sems.at[3 + j], (x, y, 1 - c)))
        for j, (cx, cy) in enumerate(chips):
            other = rows_of(2 * cx + cy, 1 - c)(0, half)
            pltpu.make_async_remote_copy(src_ref=other, dst_ref=other, send_sem=send_sems.at[3 + j],
                                         recv_sem=recv_sems.at[3 + j], device_id=(x, y, c), device_id_type=MESH).wait_recv()
        for cp in first + passed:
            cp.wait_send()

    return pl.pallas_call(
        body, name="gather_chips", in_specs=[ANY], out_specs=ANY,
        out_shape=jax.ShapeDtypeStruct((N_CHIPS,) + shard.shape, shard.dtype),
        scratch_shapes=[pltpu.SemaphoreType.DMA((6,)), pltpu.SemaphoreType.DMA((6,))],
    )(shard)


def _swap_halves(bufs):
    nb = len(bufs)

    def body(*refs):
        g_refs, t_refs, (send_sems, recv_sems) = refs[:nb], refs[nb:2 * nb], refs[2 * nb:]
        x, y, c, _ = _place()
        gives = []
        for b, (g_ref, t_ref) in enumerate(zip(g_refs, t_refs)):
            for j in range(N_CHIPS):
                k = b * N_CHIPS + j
                gives.append(_remote_parts(lambda r0, m, g_ref=g_ref, j=j: g_ref.at[j, 1 - c, pl.ds(r0, m), :],
                                           lambda r0, m, t_ref=t_ref, j=j: t_ref.at[j, pl.ds(r0, m), :],
                                           g_ref.shape[2], send_sems.at[k], recv_sems.at[k], (x, y, 1 - c)))
        for give in gives:
            give.wait()

    return pl.pallas_call(
        body, name="swap_halves", in_specs=[ANY] * nb, out_specs=[ANY] * nb,
        out_shape=[jax.ShapeDtypeStruct((b.shape[0], b.shape[2], b.shape[3]), b.dtype) for b in bufs],
        scratch_shapes=[pltpu.SemaphoreType.DMA((nb * N_CHIPS,)), pltpu.SemaphoreType.DMA((nb * N_CHIPS,))],
    )(*bufs)


def _scatter_chips(bufs):
    nb = len(bufs)

    def body(*refs):
        t_refs, o_refs, (send_sems, recv_sems) = refs[:nb], refs[nb:2 * nb], refs[2 * nb:]
        x, y, c, chips = _place()
        me = 2 * x + y

        def slot(ref, chip):
            return lambda r0, n: ref.at[chip, pl.ds(r0, n), :]

        sends = []
        for b, (t_ref, o_ref) in enumerate(zip(t_refs, o_refs)):
            for j, (cx, cy) in enumerate(chips):
                k = 3 * b + j
                sends.append(_remote_parts(slot(t_ref, 2 * cx + cy), slot(o_ref, me), t_ref.shape[1],
                                           send_sems.at[k], recv_sems.at[k], (cx, cy, c)))
        for b, o_ref in enumerate(o_refs):
            for j, (cx, cy) in enumerate(chips):
                landed = o_ref.at[2 * cx + cy]
                pltpu.make_async_remote_copy(src_ref=landed, dst_ref=landed, send_sem=send_sems.at[3 * b + j],
                                             recv_sem=recv_sems.at[3 * b + j], device_id=(x, y, c),
                                             device_id_type=MESH).wait_recv()
        for cp in sends:
            cp.wait_send()

    return pl.pallas_call(
        body, name="scatter_chips", in_specs=[ANY] * nb, out_specs=[ANY] * nb,
        out_shape=[jax.ShapeDtypeStruct(b.shape, b.dtype) for b in bufs],
        scratch_shapes=[pltpu.SemaphoreType.DMA((3 * nb,)), pltpu.SemaphoreType.DMA((3 * nb,))],
    )(*bufs)


def _give_sibling(bufs):
    nb = len(bufs)

    def body(*refs):
        h_refs, o_refs, (send_sems, recv_sems) = refs[:nb], refs[nb:2 * nb], refs[2 * nb:]
        x, y, c, _ = _place()
        gives = [_remote_parts(lambda r0, n, h_ref=h_ref: h_ref.at[pl.ds(r0, n), :],
                               lambda r0, n, o_ref=o_ref: o_ref.at[pl.ds(r0, n), :],
                               h_ref.shape[0], send_sems.at[b], recv_sems.at[b], (x, y, 1 - c))
                 for b, (h_ref, o_ref) in enumerate(zip(h_refs, o_refs))]
        for give in gives:
            give.wait()

    return pl.pallas_call(
        body, name="give_sibling", in_specs=[ANY] * nb, out_specs=[ANY] * nb,
        out_shape=[jax.ShapeDtypeStruct(b.shape, b.dtype) for b in bufs],
        scratch_shapes=[pltpu.SemaphoreType.DMA((nb,)), pltpu.SemaphoreType.DMA((nb,))],
    )(*bufs)


def _pack(arrays, dtype, row_align):
    flat = []
    for arr in arrays:
        v = arr.reshape(-1)
        flat.append(jnp.pad(v, (0, (-v.shape[0]) % LANES)))
    total = sum(f.shape[0] for f in flat) // LANES
    pad_rows = (-total) % row_align
    if pad_rows:
        flat.append(jnp.zeros((pad_rows * LANES,), dtype))
    return jnp.concatenate(flat).reshape(-1, LANES)


def _unpack(buf, shapes):
    lead = buf.shape[:-2]
    flat = buf.reshape(lead + (-1,))
    out, off = [], 0
    for shape in shapes:
        size = 1
        for dim in shape:
            size *= dim
        out.append(flat[..., off:off + size].reshape(lead + tuple(shape)))
        off += size + (-size) % LANES
    return out


def _from_chips(parts, axis):
    return jnp.concatenate([parts[j] for j in range(N_CHIPS)], axis=axis)


def kernel(x, ln_g, ln_b, attn_w_in, attn_b_f, attn_w_out, rnn_w_in, rnn_conv_w, rnn_conv_b, rnn_w_a, rnn_b_a, rnn_w_i, rnn_b_i, rnn_lambda, rnn_w_out, loss_target, m_ln_g, m_ln_b, m_attn_w_in, m_attn_b_f, m_attn_w_out, m_rnn_w_in, m_rnn_conv_w, m_rnn_conv_b, m_rnn_w_a, m_rnn_b_a, m_rnn_w_i, m_rnn_b_i, m_rnn_lambda, m_rnn_w_out, v_ln_g, v_ln_b, v_attn_w_in, v_attn_b_f, v_attn_w_out, v_rnn_w_in, v_rnn_conv_w, v_rnn_conv_b, v_rnn_w_a, v_rnn_b_a, v_rnn_w_i, v_rnn_b_i, v_rnn_lambda, v_rnn_w_out):
    s_len, d = x.shape[1], x.shape[2]
    xs = x.reshape(s_len, d)
    target = loss_target.reshape(s_len, d)
    heads = attn_b_f.shape[1]
    qkvg = attn_w_in.shape[2] * N_CHIPS - heads

    big = [attn_w_in, attn_w_out, rnn_w_in, rnn_w_a, rnn_w_i, rnn_w_out]
    small = [rnn_conv_w, rnn_conv_b, rnn_b_a, rnn_b_i, rnn_lambda]
    packed = _pack([w.astype(BF16) for w in big] + [lax.bitcast_convert_type(w, BF16) for w in small], BF16,
                   ROW_ALIGN)
    me = 2 * lax.axis_index("x") + lax.axis_index("y")
    core = lax.axis_index("c").astype(jnp.int32)
    gathered = lax.dynamic_update_index_in_dim(_gather_chips(packed), packed, me, 0)
    parts = _unpack(gathered, [w.shape for w in big] + [w.shape + (2,) for w in small])
    w_in_a, w_out_a, w_in_r, w_a, w_i, w_out_r = [
        _from_chips(p, ax) for p, ax in zip(parts[:6], (2, 1, 2, 2, 2, 1))]
    conv_w, conv_b, b_a, b_i, lam = [
        _from_chips(lax.bitcast_convert_type(p, F32), ax) for p, ax in zip(parts[6:], (2, 1, 1, 1, 1))]
    w_cat = jnp.concatenate(
        [w_in_a[:, :, :d] * (HEAD_DIM ** -0.5), w_in_a[:, :, d:qkvg],
         jnp.pad(w_in_a[:, :, qkvg:], ((0, 0), (0, 0), (0, 128 - heads)))], axis=2).astype(BF16)
    b_f = jnp.pad(attn_b_f, ((0, 0), (0, 128 - heads)))

    saved = []
    cur = xs
    for layer in range(DEPTH):
        idx = layer // 2
        g_l, b_l = ln_g[layer:layer + 1], ln_b[layer:layer + 1]
        if layer % 2 == 0:
            q, k, v, gate, fl = _proj(cur, w_cat[idx], [(0, d, BF16), (d, d, BF16), (2 * d, d, BF16),
                                                         (3 * d, d, F32), (4 * d, 128, F32)], "attn_proj")
            cum, sneg = _fcum(fl, b_f[idx:idx + 1])
            o, lse = _flash_fwd(q, k, v, cum)
            nxt, xh, rs, yg = _out_ln(o, gate, cur, w_out_a[idx], g_l, b_l, "out_ln")
            saved.append(dict(x=cur, q=q, k=k, v=v, gate=gate, cum=cum, sneg=sneg, a=o, lse=lse, xh=xh, rs=rs, yg=yg))
        else:
            u, gate = _proj(cur, w_in_r[idx], [(0, d, F32), (d, d, F32)], "rnn_proj")
            vecs = [t[idx:idx + 1] for t in (conv_b, b_a, b_i, lam)]
            hseq, uc, r, ig, a = _rnn_fwd(u, conv_w[idx], vecs[0], w_a[idx], vecs[1], w_i[idx], vecs[2], vecs[3])
            nxt, xh, rs, yg = _out_ln(hseq, gate, cur, w_out_r[idx], g_l, b_l, "out_ln")
            saved.append(dict(x=cur, u=u, gate=gate, uc=uc, r=r, ig=ig, av=a, a=hseq, xh=xh, rs=rs, yg=yg, lam=vecs[3]))
        cur = nxt

    dout, sq = _loss_grad(cur, target)
    loss = lax.psum(0.5 * jnp.sum(sq) / d, ("x", "y", "c"))

    g_ln_g, g_ln_b = [None] * DEPTH, [None] * DEPTH
    g_attn = [None] * 2
    g_rnn = [None] * 2
    for layer in reversed(range(DEPTH)):
        idx = layer // 2
        sv = saved[layer]
        w_out = w_out_a[idx] if layer % 2 == 0 else w_out_r[idx]
        dz, da, dgate, dg, db = _ln_bwd(dout, sv["xh"], sv["rs"], ln_g[layer:layer + 1], w_out, sv["gate"], sv["a"],
                                        "ln_bwd")
        g_ln_g[layer], g_ln_b[layer] = dg, db
        d_w_out = _mm_tn(sv["yg"], dz, "dw_out")
        if layer % 2 == 0:
            dq, dk, dv, dcum = _flash_bwd(sv["q"], sv["k"], sv["v"], sv["a"], da, sv["lse"], sv["cum"])
            dlogit, dbf = _fbwd(dcum, sv["sneg"])
            pieces = [dq, dk, dv, dgate, dlogit]
            dout = _mm_nt(dz, [(p, j * d) for j, p in enumerate(pieces)], w_cat[idx], "attn_dx")
            dws = [_mm_tn(sv["x"], p, "dw_in") for p in pieces[:4]] + [_mm_tn(sv["x"], dlogit, "dw_f")]
            dws[0] = dws[0] * (HEAD_DIM ** -0.5)
            g_attn[idx] = dict(w_in=dws, b_f=dbf[:, 0], w_out=d_w_out)
        else:
            duc, d_wa, d_wi, d_ba, d_bi, d_lam = _rnn_bwd(da, sv["av"], sv["a"], sv["r"], sv["ig"], sv["uc"], sv["lam"],
                                                          w_a[idx], w_i[idx])
            du, d_cw, d_cb = _conv_bwd(duc, sv["u"], conv_w[idx])
            dout = _mm_nt(dz, [(du, 0), (dgate, d)], w_in_r[idx], "rnn_dx")
            d_w_in = [_mm_tn(sv["x"], du, "dw_in"), _mm_tn(sv["x"], dgate, "dw_in")]
            g_rnn[idx] = dict(w_in=d_w_in, conv_w=d_cw, conv_b=d_cb[0], w_a=d_wa, b_a=d_ba[0], w_i=d_wi, b_i=d_bi[0],
                              lam=d_lam[0], w_out=d_w_out)
    grad_x = dout.reshape(x.shape)

    def both(items, key, cut=lambda t: t):
        return [cut(it[key]) for it in items]

    def rows_of(j, n):
        return lambda t: t[j * n:(j + 1) * n]

    def cols_of(j, n):
        return lambda t: t[..., j * n:(j + 1) * n]

    rb = RNN_BLOCK // N_CHIPS
    dq4 = d // N_CHIPS
    edges = jnp.stack([jnp.stack([it["w_in"][p][:, :heads] for it in g_attn]) for p in (1, 2, 3, 4)])

    def large_for(j):
        return (both(g_attn, "w_in", lambda t: t[j]) + both(g_attn, "w_out", rows_of(j, dq4))
                + both(g_rnn, "w_in", lambda t: cols_of(j % 2, d // 2)(t[j // 2]))
                + both(g_rnn, "w_a", lambda t: t[:, j * rb:(j + 1) * rb])
                + both(g_rnn, "w_i", lambda t: t[:, j * rb:(j + 1) * rb])
                + both(g_rnn, "w_out", rows_of(j, dq4)) + [edges])

    def small_for(j):
        return ([jnp.stack(both(g_rnn, key, cols_of(j, dq4))) for key in ("conv_w", "conv_b", "b_a", "b_i", "lam")]
                + [jnp.concatenate(g_ln_g), jnp.concatenate(g_ln_b), jnp.stack(both(g_attn, "b_f"))])

    def halves(buf):
        return buf.reshape(N_CHIPS, 2, buf.shape[1] // 2, LANES)

    large = halves(jnp.stack([_pack(large_for(j), F32, ROW_ALIGN) for j in range(N_CHIPS)]))
    small = halves(jnp.stack([_pack(small_for(j), F32, 32) for j in range(N_CHIPS)]))

    core1 = core.reshape(1)
    theirs = _swap_halves([large, small])
    pair = [_pair_sum(core1, large, theirs[0], BF16), _pair_sum(core1, small, theirs[1], F32)]
    summed = []
    for mine_all, got in zip(pair, _scatter_chips(pair)):
        own = lax.dynamic_index_in_dim(mine_all, me, 0, keepdims=False)
        summed.append(_sum_chips(lax.dynamic_update_index_in_dim(got, own, me, 0)))
    reduced = [jnp.concatenate([jnp.where(core == 0, mine, other), jnp.where(core == 0, other, mine)])
               for mine, other in zip(summed, _give_sibling(summed))]
    g_in_group, g_w_out_a, g_w_in_r, g_w_a, g_w_i, g_w_out_r, g_edges = _unpack(reduced[0], [
        (2, d, d), attn_w_out.shape, rnn_w_in.shape, rnn_w_a.shape, rnn_w_i.shape, rnn_w_out.shape,
        (N_CHIPS, 2, d, heads)])
    g_conv_w, g_conv_b, g_b_a, g_b_i, g_lam, g_ln_g_sum, g_ln_b_sum, g_b_f = _unpack(reduced[1], [
        rnn_conv_w.shape, rnn_conv_b.shape, rnn_b_a.shape, rnn_b_i.shape, rnn_lambda.shape, ln_g.shape, ln_b.shape,
        attn_b_f.shape])
    wide = jnp.concatenate([g_in_group, lax.dynamic_index_in_dim(g_edges, me, 0, keepdims=False)], axis=2)
    g_w_in_a = lax.dynamic_slice(wide, (0, 0, (heads // N_CHIPS) * me), attn_w_in.shape)

    def adam_nd(w, g, m, v, view, rows_per_block):
        outs = _adamw(w.reshape(view), g.reshape(view), m.reshape(view), v.reshape(view), rows_per_block)
        return [t.reshape(w.shape) for t in outs]

    upd = {
        "attn_w_in": adam_nd(attn_w_in, g_w_in_a, m_attn_w_in, v_attn_w_in, attn_w_in.shape, 256),
        "attn_w_out": adam_nd(attn_w_out, g_w_out_a, m_attn_w_out, v_attn_w_out, attn_w_out.shape, 256),
        "rnn_w_in": adam_nd(rnn_w_in, g_w_in_r, m_rnn_w_in, v_rnn_w_in, rnn_w_in.shape, 512),
        "rnn_w_a": adam_nd(rnn_w_a, g_w_a, m_rnn_w_a, v_rnn_w_a, (1, -1, RNN_BLOCK), 512),
        "rnn_w_i": adam_nd(rnn_w_i, g_w_i, m_rnn_w_i, v_rnn_w_i, (1, -1, RNN_BLOCK), 512),
        "rnn_w_out": adam_nd(rnn_w_out, g_w_out_r, m_rnn_w_out, v_rnn_w_out, rnn_w_out.shape, 256),
    }
    smalls = [rnn_conv_w, rnn_conv_b, rnn_b_a, rnn_b_i, rnn_lambda, ln_g, ln_b, attn_b_f]
    g_small = [g_conv_w, g_conv_b, g_b_a, g_b_i, g_lam, g_ln_g_sum, g_ln_b_sum, g_b_f]
    m_small = [m_rnn_conv_w, m_rnn_conv_b, m_rnn_b_a, m_rnn_b_i, m_rnn_lambda, m_ln_g, m_ln_b, m_attn_b_f]
    v_small = [v_rnn_conv_w, v_rnn_conv_b, v_rnn_b_a, v_rnn_b_i, v_rnn_lambda, v_ln_g, v_ln_b, v_attn_b_f]
    packs = [_pack(t, F32, 16)[None] for t in (smalls, g_small, m_small, v_small)]
    small_out = [_unpack(t[0], [w.shape for w in smalls]) for t in _adamw(*packs, 16)]
    for k, name in enumerate(("rnn_conv_w", "rnn_conv_b", "rnn_b_a", "rnn_b_i", "rnn_lambda", "ln_g", "ln_b",
                              "attn_b_f")):
        upd[name] = [small_out[0][k], small_out[1][k], small_out[2][k]]
    grad = {"attn_w_in": g_w_in_a, "attn_w_out": g_w_out_a, "rnn_w_in": g_w_in_r, "rnn_w_a": g_w_a, "rnn_w_i": g_w_i,
            "rnn_w_out": g_w_out_r, "rnn_conv_w": g_conv_w, "rnn_conv_b": g_conv_b, "rnn_b_a": g_b_a,
            "rnn_b_i": g_b_i, "rnn_lambda": g_lam, "ln_g": g_ln_g_sum, "ln_b": g_ln_b_sum, "attn_b_f": g_b_f}
    names = ("ln_g", "ln_b", "attn_w_in", "attn_b_f", "attn_w_out", "rnn_w_in", "rnn_conv_w", "rnn_conv_b",
             "rnn_w_a", "rnn_b_a", "rnn_w_i", "rnn_b_i", "rnn_lambda", "rnn_w_out")
    return (loss, grad_x, *[grad[n] for n in names], *[upd[n][0] for n in names], *[upd[n][1] for n in names],
            *[upd[n][2] for n in names])
```

```python
import functools

import jax
import jax.numpy as jnp
from jax import lax
from jax.experimental import pallas as pl
from jax.experimental.pallas import tpu as pltpu

F32 = jnp.float32
BF16 = jnp.bfloat16
MESH = pl.DeviceIdType.MESH

DEPTH = 4
HEAD_DIM = 64
HEAD_PAIR = 2 * HEAD_DIM
RNN_BLOCK = 256
CONV_WIDTH = 4
LRU_C = 8.0
ALPHA = (2.0 * DEPTH) ** 0.25
LN_EPS = 1e-5
ADAM_LR, ADAM_B1, ADAM_B2, ADAM_EPS, ADAM_WD, ADAM_STEP = 0.001, 0.9, 0.999, 1e-08, 0.01, 10
N_CHIPS = 4
LANES = 1024
NEG = -1e30

NT_DIMS = (((1,), (1,)), ((), ()))
TN_DIMS = (((0,), (0,)), ((), ()))


def _sigmoid(z):
    return 1.0 / (1.0 + jnp.exp(-z))


def _softplus(z):
    return jnp.maximum(z, 0.0) + jnp.log(1.0 + jnp.exp(-jnp.abs(z)))


def _neg_expm1(y):
    poly = y * (1.0 + y * (0.5 + y * (1.0 / 6.0 + y * (1.0 / 24.0))))
    return -jnp.where(y > -0.05, poly, jnp.exp(y) - 1.0)


def _shift_down(cur, prev8, k):
    n = cur.shape[0]
    row = lax.broadcasted_iota(jnp.int32, cur.shape, 0)
    fix = jnp.tile(pltpu.roll(prev8, k, 0), (n // 8, 1))
    return jnp.where(row < k, fix, pltpu.roll(cur, k, 0))


def _shift_up(cur, next8, k):
    n = cur.shape[0]
    row = lax.broadcasted_iota(jnp.int32, cur.shape, 0)
    fix = jnp.tile(pltpu.roll(next8, 8 - k, 0), (n // 8, 1))
    return jnp.where(row >= n - k, fix, pltpu.roll(cur, n - k, 0))


def _proj(x, w, outs, name):
    s_len, d = x.shape
    tm = min(512, s_len)

    def body(x_ref, w_ref, *o_refs):
        xb = x_ref[...].astype(BF16)
        for (c0, wd, dt), o_ref in zip(outs, o_refs):
            step = min(wd, 512)
            for cc in range(0, wd, step):
                o_ref[:, cc:cc + step] = jnp.dot(
                    xb, w_ref[:, c0 + cc:c0 + cc + step], preferred_element_type=F32).astype(dt)

    return pl.pallas_call(
        body, name=name, grid=(s_len // tm,),
        in_specs=[pl.BlockSpec((tm, d), lambda i: (i, 0)), pl.BlockSpec(w.shape, lambda i: (0, 0))],
        out_specs=[pl.BlockSpec((tm, wd), lambda i: (i, 0)) for _, wd, _ in outs],
        out_shape=[jax.ShapeDtypeStruct((s_len, wd), dt) for _, wd, dt in outs],
        compiler_params=pltpu.CompilerParams(dimension_semantics=("parallel",)),
    )(x, w)


def _mm_nt(dz, pieces, w, name):
    s_len, d = dz.shape
    tm = min(256, s_len)
    n = len(pieces)
    cols = [(c0, p.shape[1]) for p, c0 in pieces]

    def body(dz_ref, *rest):
        w_ref, o_ref = rest[n], rest[n + 1]
        acc = ALPHA * dz_ref[...]
        for (c0, wd), p_ref in zip(cols, rest[:n]):
            acc = acc + lax.dot_general(p_ref[...], w_ref[:, c0:c0 + wd], NT_DIMS, preferred_element_type=F32)
        o_ref[...] = acc

    return pl.pallas_call(
        body, name=name, grid=(s_len // tm,),
        in_specs=[pl.BlockSpec((tm, d), lambda i: (i, 0))]
        + [pl.BlockSpec((tm, wd), lambda i: (i, 0)) for _, wd in cols]
        + [pl.BlockSpec(w.shape, lambda i: (0, 0))],
        out_specs=pl.BlockSpec((tm, d), lambda i: (i, 0)),
        out_shape=jax.ShapeDtypeStruct((s_len, d), F32),
        compiler_params=pltpu.CompilerParams(dimension_semantics=("parallel",)),
    )(dz, *[p for p, _ in pieces], w)


def _mm_tn(a, b, name):
    s_len, m = a.shape
    n = b.shape[1]
    tn = min(n, 512)
    ts = min(s_len, 512)

    def body(a_ref, b_ref, o_ref):
        @pl.when(pl.program_id(1) == 0)
        def _():
            o_ref[...] = jnp.zeros_like(o_ref)

        o_ref[...] += lax.dot_general(a_ref[...].astype(BF16), b_ref[...].astype(BF16), TN_DIMS,
                                      preferred_element_type=F32)

    return pl.pallas_call(
        body, name=name, grid=(n // tn, s_len // ts),
        in_specs=[pl.BlockSpec((ts, m), lambda j, s: (s, 0)), pl.BlockSpec((ts, tn), lambda j, s: (s, j))],
        out_specs=pl.BlockSpec((m, tn), lambda j, s: (0, j)),
        out_shape=jax.ShapeDtypeStruct((m, n), F32),
        compiler_params=pltpu.CompilerParams(dimension_semantics=("parallel", "arbitrary")),
    )(a, b)


ROWS_ATTN_IN = 0
ROWS_ATTN_OUT = 2048
ROWS_RNN_IN = 2560
ROWS_RNN_OUT = 3584
SLOT_ROWS = 4096


def _dw_into(slots, a, b, name, lead, am, bn, a_block, b_block, o_block, scale=None):
    s_len = a.shape[0]
    ts = min(s_len, 512)
    steps = s_len // ts

    def body(*refs):
        a_ref, b_ref, o_ref = refs[-3:]

        @pl.when(pl.program_id(1) == 0)
        def _():
            o_ref[...] = jnp.zeros_like(o_ref)

        o_ref[...] += lax.dot_general(a_ref[...].astype(BF16), b_ref[...].astype(BF16), TN_DIMS,
                                      preferred_element_type=F32)
        if scale is not None:
            @pl.when(pl.program_id(1) == steps - 1)
            def _():
                o_ref[...] = o_ref[...] * scale

    operands = [pl.BlockSpec((ts, am), lambda i, s: (s, a_block(i))), pl.BlockSpec((ts, bn), lambda i, s: (s, b_block(i)))]
    return pl.pallas_call(
        body, name=name, grid=(lead, steps),
        in_specs=operands if slots is None else [ANY] + operands,
        out_specs=pl.BlockSpec((None, am, bn), lambda i, s: o_block(i)),
        out_shape=jax.ShapeDtypeStruct((N_CHIPS, SLOT_ROWS, LANES), F32),
        input_output_aliases={} if slots is None else {0: 0},
        compiler_params=pltpu.CompilerParams(dimension_semantics=("parallel", "arbitrary")),
    )(*([a, b] if slots is None else [slots, a, b]))


def _fcum(fl, bias):
    s_len = fl.shape[0]

    def body(fl_ref, b_ref, cum_ref, sg_ref):
        z = fl_ref[...] + b_ref[...]
        e = jnp.exp(-jnp.abs(z))
        logf = jnp.minimum(z, 0.0) - jnp.log(1.0 + e)
        sneg = jnp.where(z >= 0, e, 1.0) / (1.0 + e)
        run = logf.T[0:16, :]
        sg_ref[...] = sneg.T[0:16, :]
        lane = lax.broadcasted_iota(jnp.int32, (16, s_len), 1)
        sh = 1
        while sh < s_len:
            run = run + jnp.where(lane >= sh, pltpu.roll(run, sh, 1), 0.0)
            sh *= 2
        cum_ref[...] = run

    return pl.pallas_call(
        body, name="fcum",
        out_shape=[jax.ShapeDtypeStruct((16, s_len), F32), jax.ShapeDtypeStruct((16, s_len), F32)],
    )(fl, bias)


def _fbwd(dcum, sneg):
    s_len = dcum.shape[1]

    def body(dc_ref, sg_ref, dl_ref, db_ref):
        run = dc_ref[...]
        lane = lax.broadcasted_iota(jnp.int32, (16, s_len), 1)
        sh = 1
        while sh < s_len:
            run = run + jnp.where(lane < s_len - sh, pltpu.roll(run, s_len - sh, 1), 0.0)
            sh *= 2
        dlog = run * sg_ref[...]
        db_ref[...] = jnp.broadcast_to(jnp.sum(dlog, axis=1, keepdims=True), (16, 128))
        full = jnp.concatenate([dlog, jnp.zeros((112, s_len), F32)], axis=0)
        dl_ref[...] = full.T.astype(BF16)

    return pl.pallas_call(
        body, name="fbwd",
        out_shape=[jax.ShapeDtypeStruct((s_len, 128), BF16), jax.ShapeDtypeStruct((16, 128), F32)],
    )(dcum, sneg)


FWD_TILE = 1024
BWD_TILE = 512


def _flash_fwd(q, k, v, cum):
    s_len, width = q.shape
    tile = min(FWD_TILE, s_len // 2)
    nt = s_len // tile
    reps = tile // 128
    cumr = cum.reshape(-1, tile)

    def body(q_ref, k_ref, v_ref, cum_ref, o_ref, lse_ref, q_t, v_t, cum_col):
        pid = pl.program_id(0)
        top = lax.broadcasted_iota(jnp.int32, (HEAD_PAIR, tile), 0) < HEAD_DIM
        causal = (lax.broadcasted_iota(jnp.int32, (tile, tile), 0)
                  <= lax.broadcasted_iota(jnp.int32, (tile, tile), 1))

        def pre(i, carry):
            r0 = pl.multiple_of(i * tile, tile)
            q_t[i] = q_ref[pl.ds(r0, tile), :].astype(F32).T.astype(BF16)
            v_t[i] = v_ref[pl.ds(r0, tile), :].astype(F32).T.astype(BF16)
            for h in (0, 1):
                row = cum_ref[pl.ds((2 * pid + h) * nt + i, 1), :]
                cum_col[h, pl.ds(r0, tile), :] = jnp.broadcast_to(row, (HEAD_PAIR, tile)).T
            return carry

        lax.fori_loop(0, nt, pre, 0)

        def q_loop(qi, carry):
            qt = q_t[qi]
            zt = jnp.zeros_like(qt)
            qms = (jnp.where(top, qt, zt), jnp.where(top, zt, qt))

            def step(kj, state, masked):
                m0, l0, m1, l1, acc = state
                k0 = pl.multiple_of(kj * tile, tile)
                kt = k_ref[pl.ds(k0, tile), :]
                vt = v_t[kj]
                ms, ls, scales, contrib = [m0, m1], [l0, l1], [], None
                for h in (0, 1):
                    s = jnp.dot(kt, qms[h], preferred_element_type=F32)
                    s = s - jnp.tile(cum_col[h, pl.ds(k0, tile), :], (1, reps))
                    if masked:
                        s = jnp.where(causal, s, NEG)
                    m_new = jnp.maximum(ms[h], jnp.max(s, axis=0, keepdims=True))
                    p = jnp.exp(s - m_new)
                    scale = jnp.exp(ms[h] - m_new)
                    ls[h] = scale * ls[h] + jnp.sum(p, axis=0, keepdims=True)
                    ms[h] = m_new
                    scales.append(scale)
                    vm = jnp.where(top, vt, zt) if h == 0 else jnp.where(top, zt, vt)
                    part = jnp.dot(vm, p.astype(BF16), preferred_element_type=F32)
                    contrib = part if contrib is None else contrib + part
                acc = jnp.where(top, scales[0], scales[1]) * acc + contrib
                return ms[0], ls[0], ms[1], ls[1], acc

            low = jnp.full((1, tile), NEG, F32)
            zero = jnp.zeros((1, tile), F32)
            state = step(qi, (low, zero, low, zero, jnp.zeros((HEAD_PAIR, tile), F32)), True)
            m0, l0, m1, l1, acc = lax.fori_loop(0, qi, lambda kj, c: step(kj, c, False), state)
            q0 = pl.multiple_of(qi * tile, tile)
            o_ref[pl.ds(q0, tile), :] = (acc / jnp.where(top, l0, l1)).T
            lse_ref[pl.ds((2 * pid) * nt + qi, 1), :] = m0 + jnp.log(l0)
            lse_ref[pl.ds((2 * pid + 1) * nt + qi, 1), :] = m1 + jnp.log(l1)
            return carry

        lax.fori_loop(0, nt, q_loop, 0)

    blk = pl.BlockSpec((s_len, HEAD_PAIR), lambda p: (0, p))
    full = pl.BlockSpec(cumr.shape, lambda p: (0, 0))
    o, lse = pl.pallas_call(
        body, name="flash_fwd", grid=(width // HEAD_PAIR,),
        in_specs=[blk, blk, blk, full],
        out_specs=[blk, full],
        out_shape=[jax.ShapeDtypeStruct((s_len, width), F32), jax.ShapeDtypeStruct(cumr.shape, F32)],
        scratch_shapes=[pltpu.VMEM((nt, HEAD_PAIR, tile), BF16), pltpu.VMEM((nt, HEAD_PAIR, tile), BF16),
                        pltpu.VMEM((2, s_len, HEAD_PAIR), F32)],
        compiler_params=pltpu.CompilerParams(dimension_semantics=("arbitrary",)),
    )(q, k, v, cumr)
    return o, lse.reshape(cum.shape)


def _flash_bwd(q, k, v, o, do, lse, cum):
    s_len, width = q.shape
    tile = min(BWD_TILE, s_len // 2)
    nt = s_len // tile
    reps = tile // 128
    cumr = cum.reshape(-1, tile)
    lse = lse.reshape(-1, tile)

    def body(q_ref, k_ref, v_ref, o_ref, do_ref, lse_ref, cum_ref, dq_ref, dk_ref, dv_ref, dcum_ref,
             q_t, do_t, k_t, do_bf, cum_col, dq_t_acc, delta, q_side, dk_acc, dv_acc, k_side):
        pid = pl.program_id(0)
        top = lax.broadcasted_iota(jnp.int32, (HEAD_PAIR, tile), 0) < HEAD_DIM
        left = lax.broadcasted_iota(jnp.int32, (tile, HEAD_PAIR), 1) < HEAD_DIM
        causal = (lax.broadcasted_iota(jnp.int32, (tile, tile), 0)
                  <= lax.broadcasted_iota(jnp.int32, (tile, tile), 1))

        def pre(i, carry):
            r0 = pl.multiple_of(i * tile, tile)
            d_o = do_ref[pl.ds(r0, tile), :]
            prod_t = (d_o * o_ref[pl.ds(r0, tile), :]).T
            delta[pl.ds(i, 1), :] = jnp.sum(prod_t[:HEAD_DIM], axis=0, keepdims=True)
            delta[pl.ds(nt + i, 1), :] = jnp.sum(prod_t[HEAD_DIM:], axis=0, keepdims=True)
            do_bf[pl.ds(r0, tile), :] = d_o.astype(BF16)
            do_t[i] = d_o.T.astype(BF16)
            q_t[i] = q_ref[pl.ds(r0, tile), :].astype(F32).T.astype(BF16)
            k_t[i] = k_ref[pl.ds(r0, tile), :].astype(F32).T.astype(BF16)
            dq_t_acc[i] = jnp.zeros((HEAD_PAIR, tile), F32)
            for h in (0, 1):
                row = cum_ref[pl.ds((2 * pid + h) * nt + i, 1), :]
                cum_col[h, pl.ds(r0, tile), :] = jnp.broadcast_to(row, (HEAD_PAIR, tile)).T
                q_side[pl.ds(h * nt + i, 1), :] = jnp.zeros((1, tile), F32)
            return carry

        lax.fori_loop(0, nt, pre, 0)

        def kv_loop(kj, carry):
            k0 = pl.multiple_of(kj * tile, tile)
            kt = k_ref[pl.ds(k0, tile), :]
            vt = v_ref[pl.ds(k0, tile), :]
            ktt = k_t[kj]
            zt = jnp.zeros_like(ktt)
            zr = jnp.zeros_like(kt)
            kms = (jnp.where(top, ktt, zt), jnp.where(top, zt, ktt))
            cols = [jnp.tile(cum_col[h, pl.ds(k0, tile), :], (1, reps)) for h in (0, 1)]
            dk_acc[...] = jnp.zeros_like(dk_acc)
            dv_acc[...] = jnp.zeros_like(dv_acc)
            k_side[...] = jnp.zeros_like(k_side)

            def step(qi, carry, masked):
                q0 = pl.multiple_of(qi * tile, tile)
                qtt = q_t[qi]
                dtt = do_t[qi]
                q_rows = q_ref[pl.ds(q0, tile), :]
                do_rows = do_bf[pl.ds(q0, tile), :]
                dq_part = None
                for h in (0, 1):
                    q_m = jnp.where(top, qtt, zt) if h == 0 else jnp.where(top, zt, qtt)
                    do_m = jnp.where(top, dtt, zt) if h == 0 else jnp.where(top, zt, dtt)
                    s = jnp.dot(kt, q_m, preferred_element_type=F32) - cols[h]
                    if masked:
                        s = jnp.where(causal, s, NEG)
                    p = jnp.exp(s - lse_ref[pl.ds((2 * pid + h) * nt + qi, 1), :])
                    dp = jnp.dot(vt, do_m, preferred_element_type=F32)
                    ds = p * (dp - delta[pl.ds(h * nt + qi, 1), :])
                    q_side[pl.ds(h * nt + qi, 1), :] += jnp.sum(ds, axis=0, keepdims=True)
                    folded = ds[:, :128]
                    for b in range(1, reps):
                        folded = folded + ds[:, b * 128:(b + 1) * 128]
                    k_side[h] += folded
                    pb = p.astype(BF16)
                    dsb = ds.astype(BF16)
                    do_h = jnp.where(left, do_rows, zr) if h == 0 else jnp.where(left, zr, do_rows)
                    q_h = jnp.where(left, q_rows, zr) if h == 0 else jnp.where(left, zr, q_rows)
                    dv_acc[...] += jnp.dot(pb, do_h, preferred_element_type=F32)
                    dk_acc[...] += jnp.dot(dsb, q_h, preferred_element_type=F32)
                    part = jnp.dot(kms[h], dsb, preferred_element_type=F32)
                    dq_part = part if dq_part is None else dq_part + part
                dq_t_acc[qi] += dq_part
                return carry

            step(kj, 0, True)
            lax.fori_loop(kj + 1, nt, lambda qi, c: step(qi, c, False), 0)
            dk_ref[pl.ds(k0, tile), :] = dk_acc[...].astype(BF16)
            dv_ref[pl.ds(k0, tile), :] = dv_acc[...].astype(BF16)
            for h in (0, 1):
                dcum_ref[pl.ds((2 * pid + h) * nt + kj, 1), :] = -jnp.sum(k_side[h].T, axis=0, keepdims=True)
            return carry

        lax.fori_loop(0, nt, kv_loop, 0)

        def fin(i, carry):
            r0 = pl.multiple_of(i * tile, tile)
            dq_ref[pl.ds(r0, tile), :] = dq_t_acc[i].T.astype(BF16)
            for h in (0, 1):
                dcum_ref[pl.ds((2 * pid + h) * nt + i, 1), :] += q_side[pl.ds(h * nt + i, 1), :]
            return carry

        lax.fori_loop(0, nt, fin, 0)

    blk = pl.BlockSpec((s_len, HEAD_PAIR), lambda p: (0, p))
    full = pl.BlockSpec(cumr.shape, lambda p: (0, 0))
    transposed = pltpu.VMEM((nt, HEAD_PAIR, tile), BF16)
    dq, dk, dv, dcum = pl.pallas_call(
        body, name="flash_bwd", grid=(width // HEAD_PAIR,),
        in_specs=[blk, blk, blk, blk, blk, full, full],
        out_specs=[blk, blk, blk, full],
        out_shape=[jax.ShapeDtypeStruct((s_len, width), BF16)] * 3 + [jax.ShapeDtypeStruct(cumr.shape, F32)],
        scratch_shapes=[transposed, transposed, transposed, pltpu.VMEM((s_len, HEAD_PAIR), BF16),
                        pltpu.VMEM((2, s_len, HEAD_PAIR), F32), pltpu.VMEM((nt, HEAD_PAIR, tile), F32),
                        pltpu.VMEM((2 * nt, tile), F32), pltpu.VMEM((2 * nt, tile), F32),
                        pltpu.VMEM((tile, HEAD_PAIR), F32), pltpu.VMEM((tile, HEAD_PAIR), F32),
                        pltpu.VMEM((2, tile, HEAD_PAIR), F32)],
        compiler_params=pltpu.CompilerParams(dimension_semantics=("arbitrary",)),
    )(q, k, v, o, do, lse, cumr)
    return dq, dk, dv, dcum.reshape(cum.shape)


def _out_ln(a, gate, x, w, g, b, name):
    s_len, d = x.shape
    tm = min(256, s_len)

    def body(a_ref, gate_ref, x_ref, w_ref, g_ref, b_ref, xn_ref, xh_ref, rs_ref, yg_ref):
        gt = gate_ref[...]
        yg = (a_ref[...] * (gt * _sigmoid(gt))).astype(BF16)
        yg_ref[...] = yg
        z = ALPHA * x_ref[...] + jnp.dot(yg, w_ref[...], preferred_element_type=F32)
        zc = z - jnp.mean(z, axis=1, keepdims=True)
        rstd = lax.rsqrt(jnp.mean(zc * zc, axis=1, keepdims=True) + LN_EPS)
        xh = zc * rstd
        xh_ref[...] = xh
        xn_ref[...] = xh * g_ref[...] + b_ref[...]
        rs_ref[...] = jnp.broadcast_to(rstd, (tm, 128))

    row = pl.BlockSpec((tm, d), lambda i: (i, 0))
    vec = pl.BlockSpec((1, d), lambda i: (0, 0))
    return pl.pallas_call(
        body, name=name, grid=(s_len // tm,),
        in_specs=[row, row, row, pl.BlockSpec(w.shape, lambda i: (0, 0)), vec, vec],
        out_specs=[row, row, pl.BlockSpec((tm, 128), lambda i: (i, 0)), row],
        out_shape=[jax.ShapeDtypeStruct((s_len, d), F32), jax.ShapeDtypeStruct((s_len, d), F32),
                   jax.ShapeDtypeStruct((s_len, 128), F32), jax.ShapeDtypeStruct((s_len, d), BF16)],
        compiler_params=pltpu.CompilerParams(dimension_semantics=("parallel",)),
    )(a, gate, x, w, g, b)


def _ln_bwd(dout, xh, rs, g, w, gate, a, name):
    s_len, d = dout.shape
    tm = min(256, s_len)

    def body(do_ref, xh_ref, rs_ref, g_ref, w_ref, gate_ref, a_ref, dz_ref, da_ref, dgate_ref, dg_ref, db_ref):
        @pl.when(pl.program_id(0) == 0)
        def _():
            dg_ref[...] = jnp.zeros_like(dg_ref)
            db_ref[...] = jnp.zeros_like(db_ref)

        dout_t = do_ref[...]
        xh_t = xh_ref[...]
        dg_ref[...] += jnp.sum(dout_t * xh_t, axis=0, keepdims=True)
        db_ref[...] += jnp.sum(dout_t, axis=0, keepdims=True)
        dxh = dout_t * g_ref[...]
        m1 = jnp.mean(dxh, axis=1, keepdims=True)
        m2 = jnp.mean(dxh * xh_t, axis=1, keepdims=True)
        dz = rs_ref[:, 0:1] * (dxh - m1 - xh_t * m2)
        dz_ref[...] = dz
        dyg = lax.dot_general(dz.astype(BF16), w_ref[...], NT_DIMS, preferred_element_type=F32)
        gt = gate_ref[...]
        sg = _sigmoid(gt)
        da_ref[...] = dyg * (gt * sg)
        dgate_ref[...] = (dyg * a_ref[...] * (sg * (1.0 + gt * (1.0 - sg)))).astype(BF16)

    row = pl.BlockSpec((tm, d), lambda i: (i, 0))
    vec = pl.BlockSpec((1, d), lambda i: (0, 0))
    return pl.pallas_call(
        body, name=name, grid=(s_len // tm,),
        in_specs=[row, row, pl.BlockSpec((tm, 128), lambda i: (i, 0)), vec, pl.BlockSpec(w.shape, lambda i: (0, 0)),
                  row, row],
        out_specs=[row, row, row, vec, vec],
        out_shape=[jax.ShapeDtypeStruct((s_len, d), F32), jax.ShapeDtypeStruct((s_len, d), F32),
                   jax.ShapeDtypeStruct((s_len, d), BF16), jax.ShapeDtypeStruct((1, d), F32),
                   jax.ShapeDtypeStruct((1, d), F32)],
        compiler_params=pltpu.CompilerParams(dimension_semantics=("arbitrary",)),
    )(dout, xh, rs, g, w, gate, a)


def _loss_grad(y, target):
    s_len, d = y.shape
    tm = min(512, s_len)

    def body(y_ref, t_ref, dy_ref, acc_ref):
        @pl.when(pl.program_id(0) == 0)
        def _():
            acc_ref[...] = jnp.zeros_like(acc_ref)

        diff = y_ref[...] - t_ref[...]
        dy_ref[...] = diff * (1.0 / d)
        acc_ref[...] += jnp.sum(diff * diff, axis=0, keepdims=True)

    row = pl.BlockSpec((tm, d), lambda i: (i, 0))
    return pl.pallas_call(
        body, name="loss_grad", grid=(s_len // tm,),
        in_specs=[row, row], out_specs=[row, pl.BlockSpec((1, d), lambda i: (0, 0))],
        out_shape=[jax.ShapeDtypeStruct((s_len, d), F32), jax.ShapeDtypeStruct((1, d), F32)],
        compiler_params=pltpu.CompilerParams(dimension_semantics=("arbitrary",)),
    )(y, target)


def _rnn_fwd(u, conv_w, conv_b, wa, ba, wi, bi, lam):
    s_len, width = u.shape
    tm = min(256, s_len)
    nb = width // RNN_BLOCK

    def body(u_ref, up_ref, cw_ref, cb_ref, wa_ref, ba_ref, wi_ref, bi_ref, lam_ref,
             h_ref, uc_ref, r_ref, i_ref, a_ref, b_scr, h_carry):
        step_id = pl.program_id(0)

        @pl.when(step_id == 0)
        def _():
            h_carry[...] = jnp.zeros_like(h_carry)

        for n in range(nb):
            blk = slice(n * RNN_BLOCK, (n + 1) * RNN_BLOCK)
            ut = u_ref[:, blk]
            prev = jnp.where(step_id > 0, up_ref[:, blk], 0.0)
            uc = cb_ref[:, blk] + cw_ref[3:4, blk] * ut
            for k in (1, 2, 3):
                uc = uc + cw_ref[3 - k:4 - k, blk] * _shift_down(ut, prev, k)
            ucb = uc.astype(BF16)
            r = _sigmoid(jnp.dot(ucb, wa_ref[n], preferred_element_type=F32) + ba_ref[:, blk])
            ig = _sigmoid(jnp.dot(ucb, wi_ref[n], preferred_element_type=F32) + bi_ref[:, blk])
            log_a = -LRU_C * r * _softplus(-lam_ref[:, blk])
            uc_ref[:, blk] = uc
            r_ref[:, blk] = r
            i_ref[:, blk] = ig
            a_ref[:, blk] = jnp.exp(log_a)
            b_scr[:, blk] = jnp.sqrt(_neg_expm1(2.0 * log_a)) * (ig * uc)

        def scan(t, h):
            h = a_ref[pl.ds(t, 1), :] * h + b_scr[pl.ds(t, 1), :]
            h_ref[pl.ds(t, 1), :] = h
            return h

        h_carry[...] = lax.fori_loop(0, tm, scan, h_carry[...], unroll=8)

    row = pl.BlockSpec((tm, width), lambda i: (i, 0))
    prev8 = pl.BlockSpec((8, width), lambda i: (jnp.maximum(i * (tm // 8) - 1, 0), 0))
    vec = pl.BlockSpec((1, width), lambda i: (0, 0))
    mat = pl.BlockSpec(wa.shape, lambda i: (0, 0, 0))
    return pl.pallas_call(
        body, name="rnn_fwd", grid=(s_len // tm,),
        in_specs=[row, prev8, pl.BlockSpec(conv_w.shape, lambda i: (0, 0)), vec, mat, vec, mat, vec, vec],
        out_specs=[row] * 5,
        out_shape=[jax.ShapeDtypeStruct((s_len, width), F32)] * 5,
        scratch_shapes=[pltpu.VMEM((tm, width), F32), pltpu.VMEM((1, width), F32)],
        compiler_params=pltpu.CompilerParams(dimension_semantics=("arbitrary",)),
    )(u, u, conv_w, conv_b, wa, ba, wi, bi, lam)


def _rnn_bwd(dh, a, h, r, ig, uc, lam, wa, wi):
    s_len, width = dh.shape
    tm = min(256, s_len)
    nt = s_len // tm
    nb = width // RNN_BLOCK

    def body(dh_ref, a_ref, h_ref, hp_ref, r_ref, i_ref, uc_ref, lam_ref, wa_ref, wi_ref,
             duc_ref, dwa_ref, dwi_ref, dba_ref, dbi_ref, dlam_ref, g_scr, c_scr, dsp_scr):
        step_id = pl.program_id(0)
        tile_id = nt - 1 - step_id

        @pl.when(step_id == 0)
        def _():
            c_scr[...] = jnp.zeros_like(c_scr)
            dsp_scr[...] = jnp.zeros_like(dsp_scr)
            dwa_ref[...] = jnp.zeros_like(dwa_ref)
            dwi_ref[...] = jnp.zeros_like(dwi_ref)
            dba_ref[...] = jnp.zeros_like(dba_ref)
            dbi_ref[...] = jnp.zeros_like(dbi_ref)

        def scan(j, c):
            t = tm - 1 - j
            g = dh_ref[pl.ds(t, 1), :] + c
            g_scr[pl.ds(t, 1), :] = g
            return a_ref[pl.ds(t, 1), :] * g

        c_scr[...] = lax.fori_loop(0, tm, scan, c_scr[...], unroll=8)

        for n in range(nb):
            blk = slice(n * RNN_BLOCK, (n + 1) * RNN_BLOCK)
            g_t = g_scr[:, blk]
            hp8 = jnp.where(tile_id > 0, hp_ref[:, blk], 0.0)
            h_prev = _shift_down(h_ref[:, blk], hp8, 1)
            av, rv, iv, ucv = a_ref[:, blk], r_ref[:, blk], i_ref[:, blk], uc_ref[:, blk]
            sp = _softplus(-lam_ref[:, blk])
            mag = jnp.sqrt(_neg_expm1(-2.0 * LRU_C * rv * sp))
            d_iu = g_t * mag
            dla = g_t * h_prev * av - g_t * (iv * ucv) * (av * av) / mag
            dsp_scr[:, blk] += jnp.sum(dla * (-LRU_C * rv), axis=0, keepdims=True)
            dra = dla * (-LRU_C * sp) * rv * (1.0 - rv)
            dia = d_iu * ucv * iv * (1.0 - iv)
            drab, diab, ucb = dra.astype(BF16), dia.astype(BF16), ucv.astype(BF16)
            duc_ref[:, blk] = (d_iu * iv
                               + lax.dot_general(drab, wa_ref[n], NT_DIMS, preferred_element_type=F32)
                               + lax.dot_general(diab, wi_ref[n], NT_DIMS, preferred_element_type=F32))
            dwa_ref[n] += lax.dot_general(ucb, drab, TN_DIMS, preferred_element_type=F32)
            dwi_ref[n] += lax.dot_general(ucb, diab, TN_DIMS, preferred_element_type=F32)
            dba_ref[:, blk] += jnp.sum(dra, axis=0, keepdims=True)
            dbi_ref[:, blk] += jnp.sum(dia, axis=0, keepdims=True)

        @pl.when(step_id == nt - 1)
        def _():
            dlam_ref[...] = -dsp_scr[...] * _sigmoid(-lam_ref[...])

    row = pl.BlockSpec((tm, width), lambda i: (nt - 1 - i, 0))
    prev8 = pl.BlockSpec((8, width), lambda i: (jnp.maximum((nt - 1 - i) * (tm // 8) - 1, 0), 0))
    vec = pl.BlockSpec((1, width), lambda i: (0, 0))
    mat = pl.BlockSpec(wa.shape, lambda i: (0, 0, 0))
    return pl.pallas_call(
        body, name="rnn_bwd", grid=(nt,),
        in_specs=[row, row, row, prev8, row, row, row, vec, mat, mat],
        out_specs=[row, mat, mat, vec, vec, vec],
        out_shape=[jax.ShapeDtypeStruct((s_len, width), F32), jax.ShapeDtypeStruct(wa.shape, F32),
                   jax.ShapeDtypeStruct(wa.shape, F32)] + [jax.ShapeDtypeStruct((1, width), F32)] * 3,
        scratch_shapes=[pltpu.VMEM((tm, width), F32), pltpu.VMEM((1, width), F32), pltpu.VMEM((1, width), F32)],
        compiler_params=pltpu.CompilerParams(dimension_semantics=("arbitrary",)),
    )(dh, a, h, h, r, ig, uc, lam, wa, wi)


def _conv_bwd(duc, u, conv_w):
    s_len, width = duc.shape
    tm = min(256, s_len)
    nt = s_len // tm

    def body(d_ref, dn_ref, u_ref, up_ref, cw_ref, du_ref, dcw_ref, dcb_ref):
        step_id = pl.program_id(0)

        @pl.when(step_id == 0)
        def _():
            dcw_ref[...] = jnp.zeros_like(dcw_ref)
            dcb_ref[...] = jnp.zeros_like(dcb_ref)

        for n in range(width // RNN_BLOCK):
            blk = slice(n * RNN_BLOCK, (n + 1) * RNN_BLOCK)
            dt = d_ref[:, blk]
            ut = u_ref[:, blk]
            nxt = jnp.where(step_id < nt - 1, dn_ref[:, blk], 0.0)
            prev = jnp.where(step_id > 0, up_ref[:, blk], 0.0)
            du = cw_ref[3:4, blk] * dt
            dcw_ref[3:4, blk] += jnp.sum(dt * ut, axis=0, keepdims=True)
            for k in (1, 2, 3):
                du = du + cw_ref[3 - k:4 - k, blk] * _shift_up(dt, nxt, k)
                dcw_ref[3 - k:4 - k, blk] += jnp.sum(dt * _shift_down(ut, prev, k), axis=0, keepdims=True)
            du_ref[:, blk] = du.astype(BF16)
            dcb_ref[:, blk] += jnp.sum(dt, axis=0, keepdims=True)

    row = pl.BlockSpec((tm, width), lambda i: (i, 0))
    prev8 = pl.BlockSpec((8, width), lambda i: (jnp.maximum(i * (tm // 8) - 1, 0), 0))
    next8 = pl.BlockSpec((8, width), lambda i: (jnp.minimum((i + 1) * (tm // 8), s_len // 8 - 1), 0))
    return pl.pallas_call(
        body, name="conv_bwd", grid=(nt,),
        in_specs=[row, next8, row, prev8, pl.BlockSpec(conv_w.shape, lambda i: (0, 0))],
        out_specs=[row, pl.BlockSpec(conv_w.shape, lambda i: (0, 0)), pl.BlockSpec((1, width), lambda i: (0, 0))],
        out_shape=[jax.ShapeDtypeStruct((s_len, width), BF16), jax.ShapeDtypeStruct(conv_w.shape, F32),
                   jax.ShapeDtypeStruct((1, width), F32)],
        compiler_params=pltpu.CompilerParams(dimension_semantics=("arbitrary",)),
    )(duc, duc, u, u, conv_w)


def _pair_sum(core, grads, theirs, out_dtype):
    n, _, half, _ = grads.shape
    tb = 128 if half % 128 == 0 else half

    def body(c_ref, a_ref, b_ref, o_ref):
        o_ref[...] = (a_ref[...] + b_ref[...]).astype(out_dtype)

    blk = pl.BlockSpec((n, tb, LANES), lambda i, c: (0, i, 0))
    return pl.pallas_call(
        body, name="pair_sum",
        grid_spec=pltpu.PrefetchScalarGridSpec(
            num_scalar_prefetch=1, grid=(half // tb,),
            in_specs=[pl.BlockSpec((n, None, tb, LANES), lambda i, c: (0, c[0], i, 0)), blk], out_specs=blk),
        out_shape=jax.ShapeDtypeStruct((n, half, LANES), out_dtype),
        compiler_params=pltpu.CompilerParams(dimension_semantics=("parallel",)),
    )(core, grads, theirs)


def _sum_chips(parts):
    rows = parts.shape[1]
    tb = 128 if rows % 128 == 0 else rows

    def body(p_ref, o_ref):
        o_ref[...] = ((p_ref[0].astype(F32) + p_ref[1].astype(F32)) + p_ref[2].astype(F32)) + p_ref[3].astype(F32)

    return pl.pallas_call(
        body, name="chip_sum", grid=(rows // tb,),
        in_specs=[pl.BlockSpec((N_CHIPS, tb, LANES), lambda i: (0, i, 0))],
        out_specs=pl.BlockSpec((tb, LANES), lambda i: (i, 0)),
        out_shape=jax.ShapeDtypeStruct((rows, LANES), F32),
        compiler_params=pltpu.CompilerParams(dimension_semantics=("parallel",)),
    )(parts)


def _adamw(w, g, m, v, rows_per_block):
    n, rows, cols = w.shape
    blk = pl.BlockSpec((1, rows_per_block, cols), lambda i, j: (i, j, 0))

    def body(w_ref, g_ref, m_ref, v_ref, d_ref, nm_ref, nv_ref):
        gt = g_ref[...]
        nm = ADAM_B1 * m_ref[...] + (1.0 - ADAM_B1) * gt
        nv = ADAM_B2 * v_ref[...] + (1.0 - ADAM_B2) * (gt * gt)
        m_hat = nm / (1.0 - ADAM_B1 ** ADAM_STEP)
        v_hat = nv / (1.0 - ADAM_B2 ** ADAM_STEP)
        d_ref[...] = -ADAM_LR * (m_hat / (jnp.sqrt(v_hat) + ADAM_EPS) + ADAM_WD * w_ref[...])
        nm_ref[...] = nm
        nv_ref[...] = nv

    return pl.pallas_call(
        body, name="adamw", grid=(n, rows // rows_per_block), in_specs=[blk] * 4, out_specs=[blk] * 3,
        out_shape=[jax.ShapeDtypeStruct(w.shape, F32)] * 3,
        compiler_params=pltpu.CompilerParams(dimension_semantics=("parallel", "parallel")),
    )(w, g, m, v)


def _place():
    x, y, c = lax.axis_index("x"), lax.axis_index("y"), lax.axis_index("c")
    return x, y, c, [(1 - x, y), (x, 1 - y), (1 - x, 1 - y)]


ANY = pl.BlockSpec(memory_space=pl.ANY)
COPY_PARTS = 4


def _remote_parts(src_of, dst_of, rows, send_sem, recv_sem, to):
    parts = COPY_PARTS if rows % (16 * COPY_PARTS) == 0 else 1
    step = rows // parts
    for i in range(parts):
        pltpu.make_async_remote_copy(src_ref=src_of(i * step, step), dst_ref=dst_of(i * step, step),
                                     send_sem=send_sem, recv_sem=recv_sem, device_id=to, device_id_type=MESH).start()
    return pltpu.make_async_remote_copy(src_ref=src_of(0, rows), dst_ref=dst_of(0, rows), send_sem=send_sem,
                                        recv_sem=recv_sem, device_id=to, device_id_type=MESH)


def _gather_chips(shards):
    nb = len(shards)
    per = 7

    def body(*refs):
        p_refs, o_refs, (send_sems, recv_sems) = refs[:nb], refs[nb:2 * nb], refs[2 * nb:]
        x, y, c, chips = _place()
        me = 2 * x + y

        def copy(k, src, dst, to):
            return pltpu.make_async_remote_copy(src_ref=src, dst_ref=dst, send_sem=send_sems.at[k],
                                                recv_sem=recv_sems.at[k], device_id=to, device_id_type=MESH)

        sent = []
        for b, (p_ref, o_ref) in enumerate(zip(p_refs, o_refs)):
            for j, (cx, cy) in enumerate(chips):
                sent.append(copy(per * b + j, p_ref.at[c], o_ref.at[me, c], (cx, cy, c)))
            sent.append(copy(per * b + 6, p_ref, o_ref.at[me], (x, y, 1 - c)))
        for cp in sent:
            cp.start()
        for b, o_ref in enumerate(o_refs):
            for j, (cx, cy) in enumerate(chips):
                landed = o_ref.at[2 * cx + cy, c]
                copy(per * b + j, landed, landed, (x, y, c)).wait_recv()
                passed = copy(per * b + 3 + j, landed, landed, (x, y, 1 - c))
                passed.start()
                sent.append(passed)
        for b, o_ref in enumerate(o_refs):
            for j, (cx, cy) in enumerate(chips):
                other = o_ref.at[2 * cx + cy, 1 - c]
                copy(per * b + 3 + j, other, other, (x, y, c)).wait_recv()
            copy(per * b + 6, o_ref.at[me], o_ref.at[me], (x, y, c)).wait_recv()
        for cp in sent:
            cp.wait_send()

    return pl.pallas_call(
        body, name="gather_chips", in_specs=[ANY] * nb, out_specs=[ANY] * nb,
        out_shape=[jax.ShapeDtypeStruct((N_CHIPS,) + s.shape, s.dtype) for s in shards],
        scratch_shapes=[pltpu.SemaphoreType.DMA((per * nb,)), pltpu.SemaphoreType.DMA((per * nb,))],
    )(*shards)


def _swap_halves(bufs):
    nb = len(bufs)

    def body(*refs):
        g_refs, t_refs, (send_sems, recv_sems) = refs[:nb], refs[nb:2 * nb], refs[2 * nb:]
        x, y, c, _ = _place()
        gives = []
        for b, (g_ref, t_ref) in enumerate(zip(g_refs, t_refs)):
            for j in range(N_CHIPS):
                k = b * N_CHIPS + j
                gives.append(_remote_parts(lambda r0, m, g_ref=g_ref, j=j: g_ref.at[j, 1 - c, pl.ds(r0, m), :],
                                           lambda r0, m, t_ref=t_ref, j=j: t_ref.at[j, pl.ds(r0, m), :],
                                           g_ref.shape[2], send_sems.at[k], recv_sems.at[k], (x, y, 1 - c)))
        for give in gives:
            give.wait()

    return pl.pallas_call(
        body, name="swap_halves", in_specs=[ANY] * nb, out_specs=[ANY] * nb,
        out_shape=[jax.ShapeDtypeStruct((b.shape[0], b.shape[2], b.shape[3]), b.dtype) for b in bufs],
        scratch_shapes=[pltpu.SemaphoreType.DMA((nb * N_CHIPS,)), pltpu.SemaphoreType.DMA((nb * N_CHIPS,))],
    )(*bufs)


def _scatter_chips(bufs):
    nb = len(bufs)

    def body(*refs):
        t_refs, o_refs, (send_sems, recv_sems) = refs[:nb], refs[nb:2 * nb], refs[2 * nb:]
        x, y, c, chips = _place()
        me = 2 * x + y

        def slot(ref, chip):
            return lambda r0, n: ref.at[chip, pl.ds(r0, n), :]

        sends = []
        for b, (t_ref, o_ref) in enumerate(zip(t_refs, o_refs)):
            for j, (cx, cy) in enumerate(chips):
                k = 3 * b + j
                sends.append(_remote_parts(slot(t_ref, 2 * cx + cy), slot(o_ref, me), t_ref.shape[1],
                                           send_sems.at[k], recv_sems.at[k], (cx, cy, c)))
        for b, o_ref in enumerate(o_refs):
            for j, (cx, cy) in enumerate(chips):
                landed = o_ref.at[2 * cx + cy]
                pltpu.make_async_remote_copy(src_ref=landed, dst_ref=landed, send_sem=send_sems.at[3 * b + j],
                                             recv_sem=recv_sems.at[3 * b + j], device_id=(x, y, c),
                                             device_id_type=MESH).wait_recv()
        for cp in sends:
            cp.wait_send()

    return pl.pallas_call(
        body, name="scatter_chips", in_specs=[ANY] * nb, out_specs=[ANY] * nb,
        out_shape=[jax.ShapeDtypeStruct(b.shape, b.dtype) for b in bufs],
        scratch_shapes=[pltpu.SemaphoreType.DMA((3 * nb,)), pltpu.SemaphoreType.DMA((3 * nb,))],
    )(*bufs)


def _give_sibling(bufs):
    nb = len(bufs)

    def body(*refs):
        h_refs, o_refs, (send_sems, recv_sems) = refs[:nb], refs[nb:2 * nb], refs[2 * nb:]
        x, y, c, _ = _place()
        gives = [_remote_parts(lambda r0, n, h_ref=h_ref: h_ref.at[pl.ds(r0, n), :],
                               lambda r0, n, o_ref=o_ref: o_ref.at[pl.ds(r0, n), :],
                               h_ref.shape[0], send_sems.at[b], recv_sems.at[b], (x, y, 1 - c))
                 for b, (h_ref, o_ref) in enumerate(zip(h_refs, o_refs))]
        for give in gives:
            give.wait()

    return pl.pallas_call(
        body, name="give_sibling", in_specs=[ANY] * nb, out_specs=[ANY] * nb,
        out_shape=[jax.ShapeDtypeStruct(b.shape, b.dtype) for b in bufs],
        scratch_shapes=[pltpu.SemaphoreType.DMA((nb,)), pltpu.SemaphoreType.DMA((nb,))],
    )(*bufs)


def _pack(arrays, dtype, row_align):
    flat = []
    for arr in arrays:
        v = arr.reshape(-1)
        flat.append(jnp.pad(v, (0, (-v.shape[0]) % LANES)))
    total = sum(f.shape[0] for f in flat) // LANES
    pad_rows = (-total) % row_align
    if pad_rows:
        flat.append(jnp.zeros((pad_rows * LANES,), dtype))
    return jnp.concatenate(flat).reshape(-1, LANES)


def _unpack(buf, shapes, rows_align=1):
    lead = buf.shape[:-2]
    flat = buf.reshape(lead + (-1,))
    out, off = [], 0
    for shape in shapes:
        size = 1
        for dim in shape:
            size *= dim
        out.append(flat[..., off:off + size].reshape(lead + tuple(shape)))
        off += size + (-size) % (LANES * rows_align)
    return out


def _from_chips(parts, axis):
    return jnp.concatenate([parts[j] for j in range(N_CHIPS)], axis=axis)


def kernel(x, ln_g, ln_b, attn_w_in, attn_b_f, attn_w_out, rnn_w_in, rnn_conv_w, rnn_conv_b, rnn_w_a, rnn_b_a, rnn_w_i, rnn_b_i, rnn_lambda, rnn_w_out, loss_target, m_ln_g, m_ln_b, m_attn_w_in, m_attn_b_f, m_attn_w_out, m_rnn_w_in, m_rnn_conv_w, m_rnn_conv_b, m_rnn_w_a, m_rnn_b_a, m_rnn_w_i, m_rnn_b_i, m_rnn_lambda, m_rnn_w_out, v_ln_g, v_ln_b, v_attn_w_in, v_attn_b_f, v_attn_w_out, v_rnn_w_in, v_rnn_conv_w, v_rnn_conv_b, v_rnn_w_a, v_rnn_b_a, v_rnn_w_i, v_rnn_b_i, v_rnn_lambda, v_rnn_w_out):
    s_len, d = x.shape[1], x.shape[2]
    xs = x.reshape(s_len, d)
    target = loss_target.reshape(s_len, d)
    heads = attn_b_f.shape[1]
    qkvg = attn_w_in.shape[2] * N_CHIPS - heads

    me = 2 * lax.axis_index("x") + lax.axis_index("y")
    core = lax.axis_index("c").astype(jnp.int32)
    big = [attn_w_in, attn_w_out, rnn_w_in, rnn_w_a, rnn_w_i, rnn_w_out]
    small = [rnn_conv_w, rnn_conv_b, rnn_b_a, rnn_b_i, rnn_lambda]
    got = _gather_chips([w.astype(BF16) for w in big] + [_pack(small, F32, 16).reshape(2, -1, LANES)])
    w_in_a = jnp.concatenate([got[0][j] for j in range(N_CHIPS)], axis=2)
    w_out_a = jnp.moveaxis(got[1], 0, 1).reshape(2, d, d)
    w_in_r = jnp.moveaxis(got[2], 0, 2).reshape(2, d, 2 * d)
    w_a = jnp.transpose(got[3], (1, 2, 0, 3, 4)).reshape(2, -1, RNN_BLOCK, RNN_BLOCK)
    w_i = jnp.transpose(got[4], (1, 2, 0, 3, 4)).reshape(2, -1, RNN_BLOCK, RNN_BLOCK)
    w_out_r = jnp.moveaxis(got[5], 0, 1).reshape(2, d, d)
    conv_w, conv_b, b_a, b_i, lam = [
        _from_chips(p, ax) for p, ax in zip(_unpack(got[6].reshape(N_CHIPS, -1, LANES), [w.shape for w in small]),
                                            (2, 1, 1, 1, 1))]
    w_cat = jnp.concatenate(
        [w_in_a[:, :, :d] * (HEAD_DIM ** -0.5), w_in_a[:, :, d:qkvg],
         jnp.pad(w_in_a[:, :, qkvg:], ((0, 0), (0, 0), (0, 128 - heads)))], axis=2).astype(BF16)
    b_f = jnp.pad(attn_b_f, ((0, 0), (0, 128 - heads)))

    saved = []
    cur = xs
    for layer in range(DEPTH):
        idx = layer // 2
        g_l, b_l = ln_g[layer:layer + 1], ln_b[layer:layer + 1]
        if layer % 2 == 0:
            q, k, v, gate, fl = _proj(cur, w_cat[idx], [(0, d, BF16), (d, d, BF16), (2 * d, d, BF16),
                                                         (3 * d, d, F32), (4 * d, 128, F32)], "attn_proj")
            cum, sneg = _fcum(fl, b_f[idx:idx + 1])
            o, lse = _flash_fwd(q, k, v, cum)
            nxt, xh, rs, yg = _out_ln(o, gate, cur, w_out_a[idx], g_l, b_l, "out_ln")
            saved.append(dict(x=cur, q=q, k=k, v=v, gate=gate, cum=cum, sneg=sneg, a=o, lse=lse, xh=xh, rs=rs, yg=yg))
        else:
            u, gate = _proj(cur, w_in_r[idx], [(0, d, F32), (d, d, F32)], "rnn_proj")
            vecs = [t[idx:idx + 1] for t in (conv_b, b_a, b_i, lam)]
            hseq, uc, r, ig, a = _rnn_fwd(u, conv_w[idx], vecs[0], w_a[idx], vecs[1], w_i[idx], vecs[2], vecs[3])
            nxt, xh, rs, yg = _out_ln(hseq, gate, cur, w_out_r[idx], g_l, b_l, "out_ln")
            saved.append(dict(x=cur, u=u, gate=gate, uc=uc, r=r, ig=ig, av=a, a=hseq, xh=xh, rs=rs, yg=yg, lam=vecs[3]))
        cur = nxt

    dout, sq = _loss_grad(cur, target)
    loss = lax.psum(0.5 * jnp.sum(sq) / d, ("x", "y", "c"))

    g_ln_g, g_ln_b = [None] * DEPTH, [None] * DEPTH
    g_attn = [None] * 2
    g_rnn = [None] * 2
    slots = None
    for layer in reversed(range(DEPTH)):
        idx = layer // 2
        sv = saved[layer]
        w_out = w_out_a[idx] if layer % 2 == 0 else w_out_r[idx]
        dz, da, dgate, dg, db = _ln_bwd(dout, sv["xh"], sv["rs"], ln_g[layer:layer + 1], w_out, sv["gate"], sv["a"],
                                        "ln_bwd")
        g_ln_g[layer], g_ln_b[layer] = dg, db
        out_rows = (ROWS_ATTN_OUT if layer % 2 == 0 else ROWS_RNN_OUT) + idx * (d // N_CHIPS)
        slots = _dw_into(slots, sv["yg"], dz, "dw_out", N_CHIPS, d // N_CHIPS, d, lambda i: i, lambda i: 0,
                         lambda i, rb=out_rows // (d // N_CHIPS): (i, rb, 0))
        if layer % 2 == 0:
            dq, dk, dv, dcum = _flash_bwd(sv["q"], sv["k"], sv["v"], sv["a"], da, sv["lse"], sv["cum"])
            dlogit, dbf = _fbwd(dcum, sv["sneg"])
            pieces = [dq, dk, dv, dgate, dlogit]
            dout = _mm_nt(dz, [(p, j * d) for j, p in enumerate(pieces)], w_cat[idx], "attn_dx")
            for j, piece in enumerate(pieces[:4]):
                slots = _dw_into(slots, sv["x"], piece, "dw_in", 2, d, d // 2, lambda i: 0, lambda i: i,
                                 lambda i, j=j, rb=ROWS_ATTN_IN // d + idx: (j, rb, i),
                                 scale=HEAD_DIM ** -0.5 if j == 0 else None)
            g_attn[idx] = dict(w_f=_mm_tn(sv["x"], dlogit, "dw_f")[:, :heads], b_f=dbf[:, 0])
        else:
            duc, d_wa, d_wi, d_ba, d_bi, d_lam = _rnn_bwd(da, sv["av"], sv["a"], sv["r"], sv["ig"], sv["uc"], sv["lam"],
                                                          w_a[idx], w_i[idx])
            du, d_cw, d_cb = _conv_bwd(duc, sv["u"], conv_w[idx])
            dout = _mm_nt(dz, [(du, 0), (dgate, d)], w_in_r[idx], "rnn_dx")
            for p, part in enumerate((du, dgate)):
                slots = _dw_into(slots, sv["x"], part, "dw_rnn_in", N_CHIPS, d // 2, d // 2, lambda i: i % 2,
                                 lambda i: i // 2,
                                 lambda i, p=p, rb=ROWS_RNN_IN // (d // 2) + idx: (2 * p + i // 2, rb, i % 2))
            g_rnn[idx] = dict(conv_w=d_cw, conv_b=d_cb[0], w_a=d_wa, b_a=d_ba[0], w_i=d_wi, b_i=d_bi[0], lam=d_lam[0])
    grad_x = dout.reshape(x.shape)

    def by_chip(t, axis):
        shape = t.shape[:axis] + (N_CHIPS, t.shape[axis] // N_CHIPS) + t.shape[axis + 1:]
        flat = jnp.moveaxis(t.reshape(shape), axis, 0).reshape(N_CHIPS, -1)
        return jnp.pad(flat, ((0, 0), (0, (-flat.shape[1]) % (8 * LANES)))).reshape(N_CHIPS, -1, LANES)

    def everywhere(t):
        flat = t.reshape(-1)
        flat = jnp.pad(flat, (0, (-flat.shape[0]) % (8 * LANES))).reshape(-1, LANES)
        return jnp.broadcast_to(flat[None], (N_CHIPS,) + flat.shape)

    def both(key):
        return jnp.stack([it[key] for it in g_rnn])

    in_rows = slots[1:, ROWS_ATTN_IN:ROWS_ATTN_IN + 2 * d, :heads].reshape(3, 2, d, heads)
    edges = jnp.concatenate([in_rows, jnp.stack([it["w_f"] for it in g_attn])[None]])
    rows = [by_chip(both("w_a"), 2), by_chip(both("w_i"), 2), everywhere(edges), by_chip(both("conv_w"), 2),
            by_chip(both("conv_b"), 1), by_chip(both("b_a"), 1), by_chip(both("b_i"), 1), by_chip(both("lam"), 1),
            everywhere(jnp.concatenate(g_ln_g)), everywhere(jnp.concatenate(g_ln_b)),
            everywhere(jnp.stack([it["b_f"] for it in g_attn]))]
    used = sum(r.shape[1] for r in rows)
    small = jnp.concatenate(rows + [jnp.zeros((N_CHIPS, (-used) % 32, LANES), F32)], axis=1)

    def halves(buf):
        return buf.reshape(N_CHIPS, 2, buf.shape[1] // 2, LANES)

    large, small = halves(slots), halves(small)

    core1 = core.reshape(1)
    theirs = _swap_halves([large, small])
    pair = [_pair_sum(core1, large, theirs[0], BF16), _pair_sum(core1, small, theirs[1], F32)]
    summed = []
    for mine_all, got in zip(pair, _scatter_chips(pair)):
        own = lax.dynamic_index_in_dim(mine_all, me, 0, keepdims=False)
        summed.append(_sum_chips(lax.dynamic_update_index_in_dim(got, own, me, 0)))
    reduced = [jnp.concatenate([jnp.where(core == 0, mine, other), jnp.where(core == 0, other, mine)])
               for mine, other in zip(summed, _give_sibling(summed))]
    g_in_group = reduced[0][ROWS_ATTN_IN:ROWS_ATTN_IN + 2 * d].reshape(2, d, d)
    g_w_out_a = reduced[0][ROWS_ATTN_OUT:ROWS_ATTN_OUT + d // 2].reshape(attn_w_out.shape)
    folded = reduced[0][ROWS_RNN_IN:ROWS_RNN_IN + d].reshape(2, d // 2, d)
    g_w_in_r = jnp.concatenate([folded[:, :, :d // 2], folded[:, :, d // 2:]], axis=1)
    g_w_out_r = reduced[0][ROWS_RNN_OUT:ROWS_RNN_OUT + d // 2].reshape(rnn_w_out.shape)
    (g_w_a, g_w_i, g_edges, g_conv_w, g_conv_b, g_b_a, g_b_i, g_lam, g_ln_g_sum, g_ln_b_sum,
     g_b_f) = _unpack(reduced[1], [
        rnn_w_a.shape, rnn_w_i.shape, (N_CHIPS, 2, d, heads), rnn_conv_w.shape, rnn_conv_b.shape, rnn_b_a.shape,
        rnn_b_i.shape, rnn_lambda.shape, ln_g.shape, ln_b.shape, attn_b_f.shape], rows_align=8)
    wide = jnp.concatenate([g_in_group, lax.dynamic_index_in_dim(g_edges, me, 0, keepdims=False)], axis=2)
    g_w_in_a = lax.dynamic_slice(wide, (0, 0, (heads // N_CHIPS) * me), attn_w_in.shape)

    def adam_nd(w, g, m, v, view, rows_per_block):
        outs = _adamw(w.reshape(view), g.reshape(view), m.reshape(view), v.reshape(view), rows_per_block)
        return [t.reshape(w.shape) for t in outs]

    upd = {
        "attn_w_in": adam_nd(attn_w_in, g_w_in_a, m_attn_w_in, v_attn_w_in, attn_w_in.shape, 256),
        "attn_w_out": adam_nd(attn_w_out, g_w_out_a, m_attn_w_out, v_attn_w_out, attn_w_out.shape, 256),
        "rnn_w_in": adam_nd(rnn_w_in, g_w_in_r, m_rnn_w_in, v_rnn_w_in, rnn_w_in.shape, 512),
        "rnn_w_a": adam_nd(rnn_w_a, g_w_a, m_rnn_w_a, v_rnn_w_a, (1, -1, RNN_BLOCK), 512),
        "rnn_w_i": adam_nd(rnn_w_i, g_w_i, m_rnn_w_i, v_rnn_w_i, (1, -1, RNN_BLOCK), 512),
        "rnn_w_out": adam_nd(rnn_w_out, g_w_out_r, m_rnn_w_out, v_rnn_w_out, rnn_w_out.shape, 256),
    }
    smalls = [rnn_conv_w, rnn_conv_b, rnn_b_a, rnn_b_i, rnn_lambda, ln_g, ln_b, attn_b_f]
    g_small = [g_conv_w, g_conv_b, g_b_a, g_b_i, g_lam, g_ln_g_sum, g_ln_b_sum, g_b_f]
    m_small = [m_rnn_conv_w, m_rnn_conv_b, m_rnn_b_a, m_rnn_b_i, m_rnn_lambda, m_ln_g, m_ln_b, m_attn_b_f]
    v_small = [v_rnn_conv_w, v_rnn_conv_b, v_rnn_b_a, v_rnn_b_i, v_rnn_lambda, v_ln_g, v_ln_b, v_attn_b_f]
    packs = [_pack(t, F32, 16)[None] for t in (smalls, g_small, m_small, v_small)]
    small_out = [_unpack(t[0], [w.shape for w in smalls]) for t in _adamw(*packs, 16)]
    for k, name in enumerate(("rnn_conv_w", "rnn_conv_b", "rnn_b_a", "rnn_b_i", "rnn_lambda", "ln_g", "ln_b",
                              "attn_b_f")):
        upd[name] = [small_out[0][k], small_out[1][k], small_out[2][k]]
    grad = {"attn_w_in": g_w_in_a, "attn_w_out": g_w_out_a, "rnn_w_in": g_w_in_r, "rnn_w_a": g_w_a, "rnn_w_i": g_w_i,
            "rnn_w_out": g_w_out_r, "rnn_conv_w": g_conv_w, "rnn_conv_b": g_conv_b, "rnn_b_a": g_b_a,
            "rnn_b_i": g_b_i, "rnn_lambda": g_lam, "ln_g": g_ln_g_sum, "ln_b": g_ln_b_sum, "attn_b_f": g_b_f}
    names = ("ln_g", "ln_b", "attn_w_in", "attn_b_f", "attn_w_out", "rnn_w_in", "rnn_conv_w", "rnn_conv_b",
             "rnn_w_a", "rnn_b_a", "rnn_w_i", "rnn_b_i", "rnn_lambda", "rnn_w_out")
    return (loss, grad_x, *[grad[n] for n in names], *[upd[n][0] for n in names], *[upd[n][1] for n in names],
            *[upd[n][2] for n in names])
```

```python
import functools

import jax
import jax.numpy as jnp
from jax import lax
from jax.experimental import pallas as pl
from jax.experimental.pallas import tpu as pltpu

F32 = jnp.float32
BF16 = jnp.bfloat16
MESH = pl.DeviceIdType.MESH

DEPTH = 4
HEAD_DIM = 64
HEAD_PAIR = 2 * HEAD_DIM
RNN_BLOCK = 256
CONV_WIDTH = 4
LRU_C = 8.0
ALPHA = (2.0 * DEPTH) ** 0.25
LN_EPS = 1e-5
ADAM_LR, ADAM_B1, ADAM_B2, ADAM_EPS, ADAM_WD, ADAM_STEP = 0.001, 0.9, 0.999, 1e-08, 0.01, 10
N_CHIPS = 4
LANES = 1024
NEG = -1e30

NT_DIMS = (((1,), (1,)), ((), ()))
TN_DIMS = (((0,), (0,)), ((), ()))


def _sigmoid(z):
    return 1.0 / (1.0 + jnp.exp(-z))


def _softplus(z):
    return jnp.maximum(z, 0.0) + jnp.log(1.0 + jnp.exp(-jnp.abs(z)))


def _neg_expm1(y):
    poly = y * (1.0 + y * (0.5 + y * (1.0 / 6.0 + y * (1.0 / 24.0))))
    return -jnp.where(y > -0.05, poly, jnp.exp(y) - 1.0)


def _shift_down(cur, prev8, k):
    n = cur.shape[0]
    row = lax.broadcasted_iota(jnp.int32, cur.shape, 0)
    fix = jnp.tile(pltpu.roll(prev8, k, 0), (n // 8, 1))
    return jnp.where(row < k, fix, pltpu.roll(cur, k, 0))


def _shift_up(cur, next8, k):
    n = cur.shape[0]
    row = lax.broadcasted_iota(jnp.int32, cur.shape, 0)
    fix = jnp.tile(pltpu.roll(next8, 8 - k, 0), (n // 8, 1))
    return jnp.where(row >= n - k, fix, pltpu.roll(cur, n - k, 0))


def _proj(x, w, outs, name):
    s_len, d = x.shape
    tm = min(512, s_len)

    def body(x_ref, w_ref, *o_refs):
        xb = x_ref[...].astype(BF16)
        for (c0, wd, dt), o_ref in zip(outs, o_refs):
            step = min(wd, 512)
            for cc in range(0, wd, step):
                o_ref[:, cc:cc + step] = jnp.dot(
                    xb, w_ref[:, c0 + cc:c0 + cc + step], preferred_element_type=F32).astype(dt)

    return pl.pallas_call(
        body, name=name, grid=(s_len // tm,),
        in_specs=[pl.BlockSpec((tm, d), lambda i: (i, 0)), pl.BlockSpec(w.shape, lambda i: (0, 0))],
        out_specs=[pl.BlockSpec((tm, wd), lambda i: (i, 0)) for _, wd, _ in outs],
        out_shape=[jax.ShapeDtypeStruct((s_len, wd), dt) for _, wd, dt in outs],
        compiler_params=pltpu.CompilerParams(dimension_semantics=("parallel",)),
    )(x, w)


def _mm_nt(dz, pieces, w, name):
    s_len, d = dz.shape
    tm = min(256, s_len)
    n = len(pieces)
    cols = [(c0, p.shape[1]) for p, c0 in pieces]

    def body(dz_ref, *rest):
        w_ref, o_ref = rest[n], rest[n + 1]
        acc = ALPHA * dz_ref[...]
        for (c0, wd), p_ref in zip(cols, rest[:n]):
            acc = acc + lax.dot_general(p_ref[...], w_ref[:, c0:c0 + wd], NT_DIMS, preferred_element_type=F32)
        o_ref[...] = acc

    return pl.pallas_call(
        body, name=name, grid=(s_len // tm,),
        in_specs=[pl.BlockSpec((tm, d), lambda i: (i, 0))]
        + [pl.BlockSpec((tm, wd), lambda i: (i, 0)) for _, wd in cols]
        + [pl.BlockSpec(w.shape, lambda i: (0, 0))],
        out_specs=pl.BlockSpec((tm, d), lambda i: (i, 0)),
        out_shape=jax.ShapeDtypeStruct((s_len, d), F32),
        compiler_params=pltpu.CompilerParams(dimension_semantics=("parallel",)),
    )(dz, *[p for p, _ in pieces], w)


def _mm_tn(a, b, name):
    s_len, m = a.shape
    n = b.shape[1]
    tn = min(n, 512)
    ts = min(s_len, 512)

    def body(a_ref, b_ref, o_ref):
        @pl.when(pl.program_id(1) == 0)
        def _():
            o_ref[...] = jnp.zeros_like(o_ref)

        o_ref[...] += lax.dot_general(a_ref[...].astype(BF16), b_ref[...].astype(BF16), TN_DIMS,
                                      preferred_element_type=F32)

    return pl.pallas_call(
        body, name=name, grid=(n // tn, s_len // ts),
        in_specs=[pl.BlockSpec((ts, m), lambda j, s: (s, 0)), pl.BlockSpec((ts, tn), lambda j, s: (s, j))],
        out_specs=pl.BlockSpec((m, tn), lambda j, s: (0, j)),
        out_shape=jax.ShapeDtypeStruct((m, n), F32),
        compiler_params=pltpu.CompilerParams(dimension_semantics=("parallel", "arbitrary")),
    )(a, b)


ROWS_ATTN_IN = 0
ROWS_ATTN_OUT = 2048
ROWS_RNN_IN = 2560
ROWS_RNN_OUT = 3584
SLOT_ROWS = 4096


DW_ROWS = 1024


def _dw_call(body, name, slots, operands, specs, out_block, out_index, grid, semantics):
    return pl.pallas_call(
        body, name=name, grid=grid,
        in_specs=specs if slots is None else [ANY] + specs,
        out_specs=pl.BlockSpec(out_block, out_index),
        out_shape=jax.ShapeDtypeStruct((N_CHIPS, SLOT_ROWS, LANES), F32),
        input_output_aliases={} if slots is None else {0: 0},
        compiler_params=pltpu.CompilerParams(dimension_semantics=semantics),
    )(*(operands if slots is None else [slots] + operands))


def _dw_attn_in(slots, x, pieces, layer):
    s_len, d = x.shape
    ts = min(s_len, DW_ROWS)
    steps = s_len // ts
    half = d // 2

    def body(*refs):
        x_ref, p_refs, o_ref = refs[-6], refs[-5:-1], refs[-1]

        @pl.when(pl.program_id(1) == 0)
        def _():
            o_ref[...] = jnp.zeros_like(o_ref)

        xb = x_ref[...].astype(BF16)
        for j, p_ref in enumerate(p_refs):
            o_ref[j] += lax.dot_general(xb, p_ref[...], TN_DIMS, preferred_element_type=F32)

        @pl.when(pl.program_id(1) == steps - 1)
        def _():
            o_ref[0] = o_ref[0] * (HEAD_DIM ** -0.5)

    specs = [pl.BlockSpec((ts, d), lambda i, s: (s, 0))] + [pl.BlockSpec((ts, half), lambda i, s: (s, i))] * 4
    return _dw_call(body, "dw_attn_in", slots, [x] + list(pieces), specs, (N_CHIPS, d, half),
                    lambda i, s: (0, ROWS_ATTN_IN // d + layer, i), (2, steps), ("parallel", "arbitrary"))


def _dw_out(slots, yg, dz, first_row):
    s_len, d = dz.shape
    ts = min(s_len, DW_ROWS)
    rows = d // N_CHIPS

    def body(*refs):
        a_ref, b_ref, o_ref = refs[-3:]

        @pl.when(pl.program_id(0) == 0)
        def _():
            o_ref[...] = jnp.zeros_like(o_ref)

        prod = lax.dot_general(a_ref[...], b_ref[...].astype(BF16), TN_DIMS, preferred_element_type=F32)
        o_ref[...] += prod.reshape(N_CHIPS, rows, d)

    specs = [pl.BlockSpec((ts, d), lambda s: (s, 0))] * 2
    return _dw_call(body, "dw_out", slots, [yg, dz], specs, (N_CHIPS, rows, d), lambda s: (0, first_row // rows, 0),
                    (s_len // ts,), ("arbitrary",))


def _dw_rnn_in(slots, x, parts, layer):
    s_len, d = x.shape
    ts = min(s_len, DW_ROWS)
    half = d // 2

    def body(*refs):
        x_ref, p_refs, o_ref = refs[-4], refs[-3:-1], refs[-1]

        @pl.when(pl.program_id(0) == 0)
        def _():
            o_ref[...] = jnp.zeros_like(o_ref)

        xb = x_ref[...].astype(BF16)
        for p, p_ref in enumerate(p_refs):
            for jj in (0, 1):
                for r in (0, 1):
                    o_ref[2 * p + jj, :, r * half:(r + 1) * half] += lax.dot_general(
                        xb[:, r * half:(r + 1) * half], p_ref[:, jj * half:(jj + 1) * half], TN_DIMS,
                        preferred_element_type=F32)

    specs = [pl.BlockSpec((ts, d), lambda s: (s, 0))] * 3
    return _dw_call(body, "dw_rnn_in", slots, [x] + list(parts), specs, (N_CHIPS, half, d),
                    lambda s: (0, ROWS_RNN_IN // half + layer, 0), (s_len // ts,), ("arbitrary",))


def _fcum(fl, bias):
    s_len = fl.shape[0]

    def body(fl_ref, b_ref, cum_ref, sg_ref):
        z = fl_ref[...] + b_ref[...]
        e = jnp.exp(-jnp.abs(z))
        logf = jnp.minimum(z, 0.0) - jnp.log(1.0 + e)
        sneg = jnp.where(z >= 0, e, 1.0) / (1.0 + e)
        run = logf.T[0:16, :]
        sg_ref[...] = sneg.T[0:16, :]
        lane = lax.broadcasted_iota(jnp.int32, (16, s_len), 1)
        sh = 1
        while sh < s_len:
            run = run + jnp.where(lane >= sh, pltpu.roll(run, sh, 1), 0.0)
            sh *= 2
        cum_ref[...] = run

    return pl.pallas_call(
        body, name="fcum",
        out_shape=[jax.ShapeDtypeStruct((16, s_len), F32), jax.ShapeDtypeStruct((16, s_len), F32)],
    )(fl, bias)


def _fbwd(dcum, sneg):
    s_len = dcum.shape[1]

    def body(dc_ref, sg_ref, dl_ref, db_ref):
        run = dc_ref[...]
        lane = lax.broadcasted_iota(jnp.int32, (16, s_len), 1)
        sh = 1
        while sh < s_len:
            run = run + jnp.where(lane < s_len - sh, pltpu.roll(run, s_len - sh, 1), 0.0)
            sh *= 2
        dlog = run * sg_ref[...]
        db_ref[...] = jnp.broadcast_to(jnp.sum(dlog, axis=1, keepdims=True), (16, 128))
        full = jnp.concatenate([dlog, jnp.zeros((112, s_len), F32)], axis=0)
        dl_ref[...] = full.T.astype(BF16)

    return pl.pallas_call(
        body, name="fbwd",
        out_shape=[jax.ShapeDtypeStruct((s_len, 128), BF16), jax.ShapeDtypeStruct((16, 128), F32)],
    )(dcum, sneg)


FWD_TILE = 1024
BWD_TILE = 512


def _flash_fwd(q, k, v, cum):
    s_len, width = q.shape
    tile = min(FWD_TILE, s_len // 2)
    nt = s_len // tile
    reps = tile // 128
    cumr = cum.reshape(-1, tile)

    def body(q_ref, k_ref, v_ref, cum_ref, o_ref, lse_ref, q_t, v_t, cum_col):
        pid = pl.program_id(0)
        top = lax.broadcasted_iota(jnp.int32, (HEAD_PAIR, tile), 0) < HEAD_DIM
        causal = (lax.broadcasted_iota(jnp.int32, (tile, tile), 0)
                  <= lax.broadcasted_iota(jnp.int32, (tile, tile), 1))

        def pre(i, carry):
            r0 = pl.multiple_of(i * tile, tile)
            q_t[i] = q_ref[pl.ds(r0, tile), :].astype(F32).T.astype(BF16)
            v_t[i] = v_ref[pl.ds(r0, tile), :].astype(F32).T.astype(BF16)
            for h in (0, 1):
                row = cum_ref[pl.ds((2 * pid + h) * nt + i, 1), :]
                cum_col[h, pl.ds(r0, tile), :] = jnp.broadcast_to(row, (HEAD_PAIR, tile)).T
            return carry

        lax.fori_loop(0, nt, pre, 0)

        def q_loop(qi, carry):
            qt = q_t[qi]
            zt = jnp.zeros_like(qt)
            qms = (jnp.where(top, qt, zt), jnp.where(top, zt, qt))

            def step(kj, state, masked):
                m0, l0, m1, l1, acc = state
                k0 = pl.multiple_of(kj * tile, tile)
                kt = k_ref[pl.ds(k0, tile), :]
                vt = v_t[kj]
                ms, ls, scales, contrib = [m0, m1], [l0, l1], [], None
                for h in (0, 1):
                    s = jnp.dot(kt, qms[h], preferred_element_type=F32)
                    s = s - jnp.tile(cum_col[h, pl.ds(k0, tile), :], (1, reps))
                    if masked:
                        s = jnp.where(causal, s, NEG)
                    m_new = jnp.maximum(ms[h], jnp.max(s, axis=0, keepdims=True))
                    p = jnp.exp(s - m_new)
                    scale = jnp.exp(ms[h] - m_new)
                    ls[h] = scale * ls[h] + jnp.sum(p, axis=0, keepdims=True)
                    ms[h] = m_new
                    scales.append(scale)
                    vm = jnp.where(top, vt, zt) if h == 0 else jnp.where(top, zt, vt)
                    part = jnp.dot(vm, p.astype(BF16), preferred_element_type=F32)
                    contrib = part if contrib is None else contrib + part
                acc = jnp.where(top, scales[0], scales[1]) * acc + contrib
                return ms[0], ls[0], ms[1], ls[1], acc

            low = jnp.full((1, tile), NEG, F32)
            zero = jnp.zeros((1, tile), F32)
            state = step(qi, (low, zero, low, zero, jnp.zeros((HEAD_PAIR, tile), F32)), True)
            m0, l0, m1, l1, acc = lax.fori_loop(0, qi, lambda kj, c: step(kj, c, False), state)
            q0 = pl.multiple_of(qi * tile, tile)
            o_ref[pl.ds(q0, tile), :] = (acc / jnp.where(top, l0, l1)).T
            lse_ref[pl.ds((2 * pid) * nt + qi, 1), :] = m0 + jnp.log(l0)
            lse_ref[pl.ds((2 * pid + 1) * nt + qi, 1), :] = m1 + jnp.log(l1)
            return carry

        lax.fori_loop(0, nt, q_loop, 0)

    blk = pl.BlockSpec((s_len, HEAD_PAIR), lambda p: (0, p))
    full = pl.BlockSpec(cumr.shape, lambda p: (0, 0))
    o, lse = pl.pallas_call(
        body, name="flash_fwd", grid=(width // HEAD_PAIR,),
        in_specs=[blk, blk, blk, full],
        out_specs=[blk, full],
        out_shape=[jax.ShapeDtypeStruct((s_len, width), F32), jax.ShapeDtypeStruct(cumr.shape, F32)],
        scratch_shapes=[pltpu.VMEM((nt, HEAD_PAIR, tile), BF16), pltpu.VMEM((nt, HEAD_PAIR, tile), BF16),
                        pltpu.VMEM((2, s_len, HEAD_PAIR), F32)],
        compiler_params=pltpu.CompilerParams(dimension_semantics=("arbitrary",)),
    )(q, k, v, cumr)
    return o, lse.reshape(cum.shape)


def _flash_bwd(q, k, v, o, do, lse, cum):
    s_len, width = q.shape
    tile = min(BWD_TILE, s_len // 2)
    nt = s_len // tile
    reps = tile // 128
    cumr = cum.reshape(-1, tile)
    lse = lse.reshape(-1, tile)

    def body(q_ref, k_ref, v_ref, o_ref, do_ref, lse_ref, cum_ref, dq_ref, dk_ref, dv_ref, dcum_ref,
             q_t, do_t, k_t, do_bf, cum_col, dq_t_acc, delta, q_side, dk_acc, dv_acc, k_side):
        pid = pl.program_id(0)
        top = lax.broadcasted_iota(jnp.int32, (HEAD_PAIR, tile), 0) < HEAD_DIM
        left = lax.broadcasted_iota(jnp.int32, (tile, HEAD_PAIR), 1) < HEAD_DIM
        causal = (lax.broadcasted_iota(jnp.int32, (tile, tile), 0)
                  <= lax.broadcasted_iota(jnp.int32, (tile, tile), 1))

        def pre(i, carry):
            r0 = pl.multiple_of(i * tile, tile)
            d_o = do_ref[pl.ds(r0, tile), :]
            prod_t = (d_o * o_ref[pl.ds(r0, tile), :]).T
            delta[pl.ds(i, 1), :] = jnp.sum(prod_t[:HEAD_DIM], axis=0, keepdims=True)
            delta[pl.ds(nt + i, 1), :] = jnp.sum(prod_t[HEAD_DIM:], axis=0, keepdims=True)
            do_bf[pl.ds(r0, tile), :] = d_o.astype(BF16)
            do_t[i] = d_o.T.astype(BF16)
            q_t[i] = q_ref[pl.ds(r0, tile), :].astype(F32).T.astype(BF16)
            k_t[i] = k_ref[pl.ds(r0, tile), :].astype(F32).T.astype(BF16)
            dq_t_acc[i] = jnp.zeros((HEAD_PAIR, tile), F32)
            for h in (0, 1):
                row = cum_ref[pl.ds((2 * pid + h) * nt + i, 1), :]
                cum_col[h, pl.ds(r0, tile), :] = jnp.broadcast_to(row, (HEAD_PAIR, tile)).T
                q_side[pl.ds(h * nt + i, 1), :] = jnp.zeros((1, tile), F32)
            return carry

        lax.fori_loop(0, nt, pre, 0)

        def kv_loop(kj, carry):
            k0 = pl.multiple_of(kj * tile, tile)
            kt = k_ref[pl.ds(k0, tile), :]
            vt = v_ref[pl.ds(k0, tile), :]
            ktt = k_t[kj]
            zt = jnp.zeros_like(ktt)
            zr = jnp.zeros_like(kt)
            kms = (jnp.where(top, ktt, zt), jnp.where(top, zt, ktt))
            cols = [jnp.tile(cum_col[h, pl.ds(k0, tile), :], (1, reps)) for h in (0, 1)]
            dk_acc[...] = jnp.zeros_like(dk_acc)
            dv_acc[...] = jnp.zeros_like(dv_acc)
            k_side[...] = jnp.zeros_like(k_side)

            def step(qi, carry, masked):
                q0 = pl.multiple_of(qi * tile, tile)
                qtt = q_t[qi]
                dtt = do_t[qi]
                q_rows = q_ref[pl.ds(q0, tile), :]
                do_rows = do_bf[pl.ds(q0, tile), :]
                dq_part = None
                for h in (0, 1):
                    q_m = jnp.where(top, qtt, zt) if h == 0 else jnp.where(top, zt, qtt)
                    do_m = jnp.where(top, dtt, zt) if h == 0 else jnp.where(top, zt, dtt)
                    s = jnp.dot(kt, q_m, preferred_element_type=F32) - cols[h]
                    if masked:
                        s = jnp.where(causal, s, NEG)
                    p = jnp.exp(s - lse_ref[pl.ds((2 * pid + h) * nt + qi, 1), :])
                    dp = jnp.dot(vt, do_m, preferred_element_type=F32)
                    ds = p * (dp - delta[pl.ds(h * nt + qi, 1), :])
                    q_side[pl.ds(h * nt + qi, 1), :] += jnp.sum(ds, axis=0, keepdims=True)
                    folded = ds[:, :128]
                    for b in range(1, reps):
                        folded = folded + ds[:, b * 128:(b + 1) * 128]
                    k_side[h] += folded
                    pb = p.astype(BF16)
                    dsb = ds.astype(BF16)
                    do_h = jnp.where(left, do_rows, zr) if h == 0 else jnp.where(left, zr, do_rows)
                    q_h = jnp.where(left, q_rows, zr) if h == 0 else jnp.where(left, zr, q_rows)
                    dv_acc[...] += jnp.dot(pb, do_h, preferred_element_type=F32)
                    dk_acc[...] += jnp.dot(dsb, q_h, preferred_element_type=F32)
                    part = jnp.dot(kms[h], dsb, preferred_element_type=F32)
                    dq_part = part if dq_part is None else dq_part + part
                dq_t_acc[qi] += dq_part
                return carry

            step(kj, 0, True)
            lax.fori_loop(kj + 1, nt, lambda qi, c: step(qi, c, False), 0)
            dk_ref[pl.ds(k0, tile), :] = dk_acc[...].astype(BF16)
            dv_ref[pl.ds(k0, tile), :] = dv_acc[...].astype(BF16)
            for h in (0, 1):
                dcum_ref[pl.ds((2 * pid + h) * nt + kj, 1), :] = -jnp.sum(k_side[h].T, axis=0, keepdims=True)
            return carry

        lax.fori_loop(0, nt, kv_loop, 0)

        def fin(i, carry):
            r0 = pl.multiple_of(i * tile, tile)
            dq_ref[pl.ds(r0, tile), :] = dq_t_acc[i].T.astype(BF16)
            for h in (0, 1):
                dcum_ref[pl.ds((2 * pid + h) * nt + i, 1), :] += q_side[pl.ds(h * nt + i, 1), :]
            return carry

        lax.fori_loop(0, nt, fin, 0)

    blk = pl.BlockSpec((s_len, HEAD_PAIR), lambda p: (0, p))
    full = pl.BlockSpec(cumr.shape, lambda p: (0, 0))
    transposed = pltpu.VMEM((nt, HEAD_PAIR, tile), BF16)
    dq, dk, dv, dcum = pl.pallas_call(
        body, name="flash_bwd", grid=(width // HEAD_PAIR,),
        in_specs=[blk, blk, blk, blk, blk, full, full],
        out_specs=[blk, blk, blk, full],
        out_shape=[jax.ShapeDtypeStruct((s_len, width), BF16)] * 3 + [jax.ShapeDtypeStruct(cumr.shape, F32)],
        scratch_shapes=[transposed, transposed, transposed, pltpu.VMEM((s_len, HEAD_PAIR), BF16),
                        pltpu.VMEM((2, s_len, HEAD_PAIR), F32), pltpu.VMEM((nt, HEAD_PAIR, tile), F32),
                        pltpu.VMEM((2 * nt, tile), F32), pltpu.VMEM((2 * nt, tile), F32),
                        pltpu.VMEM((tile, HEAD_PAIR), F32), pltpu.VMEM((tile, HEAD_PAIR), F32),
                        pltpu.VMEM((2, tile, HEAD_PAIR), F32)],
        compiler_params=pltpu.CompilerParams(dimension_semantics=("arbitrary",)),
    )(q, k, v, o, do, lse, cumr)
    return dq, dk, dv, dcum.reshape(cum.shape)


def _out_ln(a, gate, x, w, g, b, name):
    s_len, d = x.shape
    tm = min(256, s_len)

    def body(a_ref, gate_ref, x_ref, w_ref, g_ref, b_ref, xn_ref, xh_ref, rs_ref, yg_ref):
        gt = gate_ref[...]
        yg = (a_ref[...] * (gt * _sigmoid(gt))).astype(BF16)
        yg_ref[...] = yg
        z = ALPHA * x_ref[...] + jnp.dot(yg, w_ref[...], preferred_element_type=F32)
        zc = z - jnp.mean(z, axis=1, keepdims=True)
        rstd = lax.rsqrt(jnp.mean(zc * zc, axis=1, keepdims=True) + LN_EPS)
        xh = zc * rstd
        xh_ref[...] = xh
        xn_ref[...] = xh * g_ref[...] + b_ref[...]
        rs_ref[...] = jnp.broadcast_to(rstd, (tm, 128))

    row = pl.BlockSpec((tm, d), lambda i: (i, 0))
    vec = pl.BlockSpec((1, d), lambda i: (0, 0))
    return pl.pallas_call(
        body, name=name, grid=(s_len // tm,),
        in_specs=[row, row, row, pl.BlockSpec(w.shape, lambda i: (0, 0)), vec, vec],
        out_specs=[row, row, pl.BlockSpec((tm, 128), lambda i: (i, 0)), row],
        out_shape=[jax.ShapeDtypeStruct((s_len, d), F32), jax.ShapeDtypeStruct((s_len, d), F32),
                   jax.ShapeDtypeStruct((s_len, 128), F32), jax.ShapeDtypeStruct((s_len, d), BF16)],
        compiler_params=pltpu.CompilerParams(dimension_semantics=("parallel",)),
    )(a, gate, x, w, g, b)


def _ln_bwd(dout, xh, rs, g, w, gate, a, name):
    s_len, d = dout.shape
    tm = min(256, s_len)

    def body(do_ref, xh_ref, rs_ref, g_ref, w_ref, gate_ref, a_ref, dz_ref, da_ref, dgate_ref, dg_ref, db_ref):
        @pl.when(pl.program_id(0) == 0)
        def _():
            dg_ref[...] = jnp.zeros_like(dg_ref)
            db_ref[...] = jnp.zeros_like(db_ref)

        dout_t = do_ref[...]
        xh_t = xh_ref[...]
        dg_ref[...] += jnp.sum(dout_t * xh_t, axis=0, keepdims=True)
        db_ref[...] += jnp.sum(dout_t, axis=0, keepdims=True)
        dxh = dout_t * g_ref[...]
        m1 = jnp.mean(dxh, axis=1, keepdims=True)
        m2 = jnp.mean(dxh * xh_t, axis=1, keepdims=True)
        dz = rs_ref[:, 0:1] * (dxh - m1 - xh_t * m2)
        dz_ref[...] = dz
        dyg = lax.dot_general(dz.astype(BF16), w_ref[...], NT_DIMS, preferred_element_type=F32)
        gt = gate_ref[...]
        sg = _sigmoid(gt)
        da_ref[...] = dyg * (gt * sg)
        dgate_ref[...] = (dyg * a_ref[...] * (sg * (1.0 + gt * (1.0 - sg)))).astype(BF16)

    row = pl.BlockSpec((tm, d), lambda i: (i, 0))
    vec = pl.BlockSpec((1, d), lambda i: (0, 0))
    return pl.pallas_call(
        body, name=name, grid=(s_len // tm,),
        in_specs=[row, row, pl.BlockSpec((tm, 128), lambda i: (i, 0)), vec, pl.BlockSpec(w.shape, lambda i: (0, 0)),
                  row, row],
        out_specs=[row, row, row, vec, vec],
        out_shape=[jax.ShapeDtypeStruct((s_len, d), F32), jax.ShapeDtypeStruct((s_len, d), F32),
                   jax.ShapeDtypeStruct((s_len, d), BF16), jax.ShapeDtypeStruct((1, d), F32),
                   jax.ShapeDtypeStruct((1, d), F32)],
        compiler_params=pltpu.CompilerParams(dimension_semantics=("arbitrary",)),
    )(dout, xh, rs, g, w, gate, a)


def _loss_grad(y, target):
    s_len, d = y.shape
    tm = min(512, s_len)

    def body(y_ref, t_ref, dy_ref, acc_ref):
        @pl.when(pl.program_id(0) == 0)
        def _():
            acc_ref[...] = jnp.zeros_like(acc_ref)

        diff = y_ref[...] - t_ref[...]
        dy_ref[...] = diff * (1.0 / d)
        acc_ref[...] += jnp.sum(diff * diff, axis=0, keepdims=True)

    row = pl.BlockSpec((tm, d), lambda i: (i, 0))
    return pl.pallas_call(
        body, name="loss_grad", grid=(s_len // tm,),
        in_specs=[row, row], out_specs=[row, pl.BlockSpec((1, d), lambda i: (0, 0))],
        out_shape=[jax.ShapeDtypeStruct((s_len, d), F32), jax.ShapeDtypeStruct((1, d), F32)],
        compiler_params=pltpu.CompilerParams(dimension_semantics=("arbitrary",)),
    )(y, target)


def _rnn_fwd(u, conv_w, conv_b, wa, ba, wi, bi, lam):
    s_len, width = u.shape
    tm = min(256, s_len)
    nb = width // RNN_BLOCK

    def body(u_ref, up_ref, cw_ref, cb_ref, wa_ref, ba_ref, wi_ref, bi_ref, lam_ref,
             h_ref, uc_ref, r_ref, i_ref, a_ref, b_scr, h_carry):
        step_id = pl.program_id(0)

        @pl.when(step_id == 0)
        def _():
            h_carry[...] = jnp.zeros_like(h_carry)

        for n in range(nb):
            blk = slice(n * RNN_BLOCK, (n + 1) * RNN_BLOCK)
            ut = u_ref[:, blk]
            prev = jnp.where(step_id > 0, up_ref[:, blk], 0.0)
            uc = cb_ref[:, blk] + cw_ref[3:4, blk] * ut
            for k in (1, 2, 3):
                uc = uc + cw_ref[3 - k:4 - k, blk] * _shift_down(ut, prev, k)
            ucb = uc.astype(BF16)
            r = _sigmoid(jnp.dot(ucb, wa_ref[n], preferred_element_type=F32) + ba_ref[:, blk])
            ig = _sigmoid(jnp.dot(ucb, wi_ref[n], preferred_element_type=F32) + bi_ref[:, blk])
            log_a = -LRU_C * r * _softplus(-lam_ref[:, blk])
            uc_ref[:, blk] = uc
            r_ref[:, blk] = r
            i_ref[:, blk] = ig
            a_ref[:, blk] = jnp.exp(log_a)
            b_scr[:, blk] = jnp.sqrt(_neg_expm1(2.0 * log_a)) * (ig * uc)

        def scan(t, h):
            h = a_ref[pl.ds(t, 1), :] * h + b_scr[pl.ds(t, 1), :]
            h_ref[pl.ds(t, 1), :] = h
            return h

        h_carry[...] = lax.fori_loop(0, tm, scan, h_carry[...], unroll=8)

    row = pl.BlockSpec((tm, width), lambda i: (i, 0))
    prev8 = pl.BlockSpec((8, width), lambda i: (jnp.maximum(i * (tm // 8) - 1, 0), 0))
    vec = pl.BlockSpec((1, width), lambda i: (0, 0))
    mat = pl.BlockSpec(wa.shape, lambda i: (0, 0, 0))
    return pl.pallas_call(
        body, name="rnn_fwd", grid=(s_len // tm,),
        in_specs=[row, prev8, pl.BlockSpec(conv_w.shape, lambda i: (0, 0)), vec, mat, vec, mat, vec, vec],
        out_specs=[row] * 5,
        out_shape=[jax.ShapeDtypeStruct((s_len, width), F32)] * 5,
        scratch_shapes=[pltpu.VMEM((tm, width), F32), pltpu.VMEM((1, width), F32)],
        compiler_params=pltpu.CompilerParams(dimension_semantics=("arbitrary",)),
    )(u, u, conv_w, conv_b, wa, ba, wi, bi, lam)


def _rnn_bwd(dh, a, h, r, ig, uc, lam, wa, wi):
    s_len, width = dh.shape
    tm = min(256, s_len)
    nt = s_len // tm
    nb = width // RNN_BLOCK

    def body(dh_ref, a_ref, h_ref, hp_ref, r_ref, i_ref, uc_ref, lam_ref, wa_ref, wi_ref,
             duc_ref, dwa_ref, dwi_ref, dba_ref, dbi_ref, dlam_ref, g_scr, c_scr, dsp_scr):
        step_id = pl.program_id(0)
        tile_id = nt - 1 - step_id

        @pl.when(step_id == 0)
        def _():
            c_scr[...] = jnp.zeros_like(c_scr)
            dsp_scr[...] = jnp.zeros_like(dsp_scr)
            dwa_ref[...] = jnp.zeros_like(dwa_ref)
            dwi_ref[...] = jnp.zeros_like(dwi_ref)
            dba_ref[...] = jnp.zeros_like(dba_ref)
            dbi_ref[...] = jnp.zeros_like(dbi_ref)

        def scan(j, c):
            t = tm - 1 - j
            g = dh_ref[pl.ds(t, 1), :] + c
            g_scr[pl.ds(t, 1), :] = g
            return a_ref[pl.ds(t, 1), :] * g

        c_scr[...] = lax.fori_loop(0, tm, scan, c_scr[...], unroll=8)

        for n in range(nb):
            blk = slice(n * RNN_BLOCK, (n + 1) * RNN_BLOCK)
            g_t = g_scr[:, blk]
            hp8 = jnp.where(tile_id > 0, hp_ref[:, blk], 0.0)
            h_prev = _shift_down(h_ref[:, blk], hp8, 1)
            av, rv, iv, ucv = a_ref[:, blk], r_ref[:, blk], i_ref[:, blk], uc_ref[:, blk]
            sp = _softplus(-lam_ref[:, blk])
            mag = jnp.sqrt(_neg_expm1(-2.0 * LRU_C * rv * sp))
            d_iu = g_t * mag
            dla = g_t * h_prev * av - g_t * (iv * ucv) * (av * av) / mag
            dsp_scr[:, blk] += jnp.sum(dla * (-LRU_C * rv), axis=0, keepdims=True)
            dra = dla * (-LRU_C * sp) * rv * (1.0 - rv)
            dia = d_iu * ucv * iv * (1.0 - iv)
            drab, diab, ucb = dra.astype(BF16), dia.astype(BF16), ucv.astype(BF16)
            duc_ref[:, blk] = (d_iu * iv
                               + lax.dot_general(drab, wa_ref[n], NT_DIMS, preferred_element_type=F32)
                               + lax.dot_general(diab, wi_ref[n], NT_DIMS, preferred_element_type=F32))
            dwa_ref[n] += lax.dot_general(ucb, drab, TN_DIMS, preferred_element_type=F32)
            dwi_ref[n] += lax.dot_general(ucb, diab, TN_DIMS, preferred_element_type=F32)
            dba_ref[:, blk] += jnp.sum(dra, axis=0, keepdims=True)
            dbi_ref[:, blk] += jnp.sum(dia, axis=0, keepdims=True)

        @pl.when(step_id == nt - 1)
        def _():
            dlam_ref[...] = -dsp_scr[...] * _sigmoid(-lam_ref[...])

    row = pl.BlockSpec((tm, width), lambda i: (nt - 1 - i, 0))
    prev8 = pl.BlockSpec((8, width), lambda i: (jnp.maximum((nt - 1 - i) * (tm // 8) - 1, 0), 0))
    vec = pl.BlockSpec((1, width), lambda i: (0, 0))
    mat = pl.BlockSpec(wa.shape, lambda i: (0, 0, 0))
    return pl.pallas_call(
        body, name="rnn_bwd", grid=(nt,),
        in_specs=[row, row, row, prev8, row, row, row, vec, mat, mat],
        out_specs=[row, mat, mat, vec, vec, vec],
        out_shape=[jax.ShapeDtypeStruct((s_len, width), F32), jax.ShapeDtypeStruct(wa.shape, F32),
                   jax.ShapeDtypeStruct(wa.shape, F32)] + [jax.ShapeDtypeStruct((1, width), F32)] * 3,
        scratch_shapes=[pltpu.VMEM((tm, width), F32), pltpu.VMEM((1, width), F32), pltpu.VMEM((1, width), F32)],
        compiler_params=pltpu.CompilerParams(dimension_semantics=("arbitrary",)),
    )(dh, a, h, h, r, ig, uc, lam, wa, wi)


def _conv_bwd(duc, u, conv_w):
    s_len, width = duc.shape
    tm = min(256, s_len)
    nt = s_len // tm

    def body(d_ref, dn_ref, u_ref, up_ref, cw_ref, du_ref, dcw_ref, dcb_ref):
        step_id = pl.program_id(0)

        @pl.when(step_id == 0)
        def _():
            dcw_ref[...] = jnp.zeros_like(dcw_ref)
            dcb_ref[...] = jnp.zeros_like(dcb_ref)

        for n in range(width // RNN_BLOCK):
            blk = slice(n * RNN_BLOCK, (n + 1) * RNN_BLOCK)
            dt = d_ref[:, blk]
            ut = u_ref[:, blk]
            nxt = jnp.where(step_id < nt - 1, dn_ref[:, blk], 0.0)
            prev = jnp.where(step_id > 0, up_ref[:, blk], 0.0)
            du = cw_ref[3:4, blk] * dt
            dcw_ref[3:4, blk] += jnp.sum(dt * ut, axis=0, keepdims=True)
            for k in (1, 2, 3):
                du = du + cw_ref[3 - k:4 - k, blk] * _shift_up(dt, nxt, k)
                dcw_ref[3 - k:4 - k, blk] += jnp.sum(dt * _shift_down(ut, prev, k), axis=0, keepdims=True)
            du_ref[:, blk] = du.astype(BF16)
            dcb_ref[:, blk] += jnp.sum(dt, axis=0, keepdims=True)

    row = pl.BlockSpec((tm, width), lambda i: (i, 0))
    prev8 = pl.BlockSpec((8, width), lambda i: (jnp.maximum(i * (tm // 8) - 1, 0), 0))
    next8 = pl.BlockSpec((8, width), lambda i: (jnp.minimum((i + 1) * (tm // 8), s_len // 8 - 1), 0))
    return pl.pallas_call(
        body, name="conv_bwd", grid=(nt,),
        in_specs=[row, next8, row, prev8, pl.BlockSpec(conv_w.shape, lambda i: (0, 0))],
        out_specs=[row, pl.BlockSpec(conv_w.shape, lambda i: (0, 0)), pl.BlockSpec((1, width), lambda i: (0, 0))],
        out_shape=[jax.ShapeDtypeStruct((s_len, width), BF16), jax.ShapeDtypeStruct(conv_w.shape, F32),
                   jax.ShapeDtypeStruct((1, width), F32)],
        compiler_params=pltpu.CompilerParams(dimension_semantics=("arbitrary",)),
    )(duc, duc, u, u, conv_w)


def _pair_sum(core, grads, theirs, out_dtype):
    n, _, half, _ = grads.shape
    tb = 128 if half % 128 == 0 else half

    def body(c_ref, a_ref, b_ref, o_ref):
        o_ref[...] = (a_ref[...] + b_ref[...]).astype(out_dtype)

    blk = pl.BlockSpec((n, tb, LANES), lambda i, c: (0, i, 0))
    return pl.pallas_call(
        body, name="pair_sum",
        grid_spec=pltpu.PrefetchScalarGridSpec(
            num_scalar_prefetch=1, grid=(half // tb,),
            in_specs=[pl.BlockSpec((n, None, tb, LANES), lambda i, c: (0, c[0], i, 0)), blk], out_specs=blk),
        out_shape=jax.ShapeDtypeStruct((n, half, LANES), out_dtype),
        compiler_params=pltpu.CompilerParams(dimension_semantics=("parallel",)),
    )(core, grads, theirs)


def _sum_chips(parts):
    rows = parts.shape[1]
    tb = 128 if rows % 128 == 0 else rows

    def body(p_ref, o_ref):
        o_ref[...] = ((p_ref[0].astype(F32) + p_ref[1].astype(F32)) + p_ref[2].astype(F32)) + p_ref[3].astype(F32)

    return pl.pallas_call(
        body, name="chip_sum", grid=(rows // tb,),
        in_specs=[pl.BlockSpec((N_CHIPS, tb, LANES), lambda i: (0, i, 0))],
        out_specs=pl.BlockSpec((tb, LANES), lambda i: (i, 0)),
        out_shape=jax.ShapeDtypeStruct((rows, LANES), F32),
        compiler_params=pltpu.CompilerParams(dimension_semantics=("parallel",)),
    )(parts)


def _adamw(w, g, m, v, rows_per_block):
    n, rows, cols = w.shape
    blk = pl.BlockSpec((1, rows_per_block, cols), lambda i, j: (i, j, 0))

    def body(w_ref, g_ref, m_ref, v_ref, d_ref, nm_ref, nv_ref):
        gt = g_ref[...]
        nm = ADAM_B1 * m_ref[...] + (1.0 - ADAM_B1) * gt
        nv = ADAM_B2 * v_ref[...] + (1.0 - ADAM_B2) * (gt * gt)
        m_hat = nm / (1.0 - ADAM_B1 ** ADAM_STEP)
        v_hat = nv / (1.0 - ADAM_B2 ** ADAM_STEP)
        d_ref[...] = -ADAM_LR * (m_hat / (jnp.sqrt(v_hat) + ADAM_EPS) + ADAM_WD * w_ref[...])
        nm_ref[...] = nm
        nv_ref[...] = nv

    return pl.pallas_call(
        body, name="adamw", grid=(n, rows // rows_per_block), in_specs=[blk] * 4, out_specs=[blk] * 3,
        out_shape=[jax.ShapeDtypeStruct(w.shape, F32)] * 3,
        compiler_params=pltpu.CompilerParams(dimension_semantics=("parallel", "parallel")),
    )(w, g, m, v)


def _place():
    x, y, c = lax.axis_index("x"), lax.axis_index("y"), lax.axis_index("c")
    return x, y, c, [(1 - x, y), (x, 1 - y), (1 - x, 1 - y)]


ANY = pl.BlockSpec(memory_space=pl.ANY)
COPY_PARTS = 4


def _remote_parts(src_of, dst_of, rows, send_sem, recv_sem, to):
    parts = COPY_PARTS if rows % (16 * COPY_PARTS) == 0 else 1
    step = rows // parts
    for i in range(parts):
        pltpu.make_async_remote_copy(src_ref=src_of(i * step, step), dst_ref=dst_of(i * step, step),
                                     send_sem=send_sem, recv_sem=recv_sem, device_id=to, device_id_type=MESH).start()
    return pltpu.make_async_remote_copy(src_ref=src_of(0, rows), dst_ref=dst_of(0, rows), send_sem=send_sem,
                                        recv_sem=recv_sem, device_id=to, device_id_type=MESH)


def _gather_chips(shards):
    nb = len(shards)
    per = 7

    def body(*refs):
        p_refs, o_refs, (send_sems, recv_sems) = refs[:nb], refs[nb:2 * nb], refs[2 * nb:]
        x, y, c, chips = _place()
        me = 2 * x + y

        def copy(k, src, dst, to):
            return pltpu.make_async_remote_copy(src_ref=src, dst_ref=dst, send_sem=send_sems.at[k],
                                                recv_sem=recv_sems.at[k], device_id=to, device_id_type=MESH)

        sent = []
        for b, (p_ref, o_ref) in enumerate(zip(p_refs, o_refs)):
            for j, (cx, cy) in enumerate(chips):
                sent.append(copy(per * b + j, p_ref.at[c], o_ref.at[me, c], (cx, cy, c)))
            sent.append(copy(per * b + 6, p_ref, o_ref.at[me], (x, y, 1 - c)))
        for cp in sent:
            cp.start()
        for b, o_ref in enumerate(o_refs):
            for j, (cx, cy) in enumerate(chips):
                landed = o_ref.at[2 * cx + cy, c]
                copy(per * b + j, landed, landed, (x, y, c)).wait_recv()
                passed = copy(per * b + 3 + j, landed, landed, (x, y, 1 - c))
                passed.start()
                sent.append(passed)
        for b, o_ref in enumerate(o_refs):
            for j, (cx, cy) in enumerate(chips):
                other = o_ref.at[2 * cx + cy, 1 - c]
                copy(per * b + 3 + j, other, other, (x, y, c)).wait_recv()
            copy(per * b + 6, o_ref.at[me], o_ref.at[me], (x, y, c)).wait_recv()
        for cp in sent:
            cp.wait_send()

    return pl.pallas_call(
        body, name="gather_chips", in_specs=[ANY] * nb, out_specs=[ANY] * nb,
        out_shape=[jax.ShapeDtypeStruct((N_CHIPS,) + s.shape, s.dtype) for s in shards],
        scratch_shapes=[pltpu.SemaphoreType.DMA((per * nb,)), pltpu.SemaphoreType.DMA((per * nb,))],
    )(*shards)


def _swap_halves(bufs):
    nb = len(bufs)

    def body(*refs):
        g_refs, t_refs, (send_sems, recv_sems) = refs[:nb], refs[nb:2 * nb], refs[2 * nb:]
        x, y, c, _ = _place()
        gives = []
        for b, (g_ref, t_ref) in enumerate(zip(g_refs, t_refs)):
            for j in range(N_CHIPS):
                k = b * N_CHIPS + j
                gives.append(_remote_parts(lambda r0, m, g_ref=g_ref, j=j: g_ref.at[j, 1 - c, pl.ds(r0, m), :],
                                           lambda r0, m, t_ref=t_ref, j=j: t_ref.at[j, pl.ds(r0, m), :],
                                           g_ref.shape[2], send_sems.at[k], recv_sems.at[k], (x, y, 1 - c)))
        for give in gives:
            give.wait()

    return pl.pallas_call(
        body, name="swap_halves", in_specs=[ANY] * nb, out_specs=[ANY] * nb,
        out_shape=[jax.ShapeDtypeStruct((b.shape[0], b.shape[2], b.shape[3]), b.dtype) for b in bufs],
        scratch_shapes=[pltpu.SemaphoreType.DMA((nb * N_CHIPS,)), pltpu.SemaphoreType.DMA((nb * N_CHIPS,))],
    )(*bufs)


def _scatter_chips(bufs):
    nb = len(bufs)

    def body(*refs):
        t_refs, o_refs, (send_sems, recv_sems) = refs[:nb], refs[nb:2 * nb], refs[2 * nb:]
        x, y, c, chips = _place()
        me = 2 * x + y

        def slot(ref, chip):
            return lambda r0, n: ref.at[chip, pl.ds(r0, n), :]

        sends = []
        for b, (t_ref, o_ref) in enumerate(zip(t_refs, o_refs)):
            for j, (cx, cy) in enumerate(chips):
                k = 3 * b + j
                sends.append(_remote_parts(slot(t_ref, 2 * cx + cy), slot(o_ref, me), t_ref.shape[1],
                                           send_sems.at[k], recv_sems.at[k], (cx, cy, c)))
        for b, o_ref in enumerate(o_refs):
            for j, (cx, cy) in enumerate(chips):
                landed = o_ref.at[2 * cx + cy]
                pltpu.make_async_remote_copy(src_ref=landed, dst_ref=landed, send_sem=send_sems.at[3 * b + j],
                                             recv_sem=recv_sems.at[3 * b + j], device_id=(x, y, c),
                                             device_id_type=MESH).wait_recv()
        for cp in sends:
            cp.wait_send()

    return pl.pallas_call(
        body, name="scatter_chips", in_specs=[ANY] * nb, out_specs=[ANY] * nb,
        out_shape=[jax.ShapeDtypeStruct(b.shape, b.dtype) for b in bufs],
        scratch_shapes=[pltpu.SemaphoreType.DMA((3 * nb,)), pltpu.SemaphoreType.DMA((3 * nb,))],
    )(*bufs)


def _give_sibling(bufs):
    nb = len(bufs)

    def body(*refs):
        h_refs, o_refs, (send_sems, recv_sems) = refs[:nb], refs[nb:2 * nb], refs[2 * nb:]
        x, y, c, _ = _place()
        gives = [_remote_parts(lambda r0, n, h_ref=h_ref: h_ref.at[pl.ds(r0, n), :],
                               lambda r0, n, o_ref=o_ref: o_ref.at[pl.ds(r0, n), :],
                               h_ref.shape[0], send_sems.at[b], recv_sems.at[b], (x, y, 1 - c))
                 for b, (h_ref, o_ref) in enumerate(zip(h_refs, o_refs))]
        for give in gives:
            give.wait()

    return pl.pallas_call(
        body, name="give_sibling", in_specs=[ANY] * nb, out_specs=[ANY] * nb,
        out_shape=[jax.ShapeDtypeStruct(b.shape, b.dtype) for b in bufs],
        scratch_shapes=[pltpu.SemaphoreType.DMA((nb,)), pltpu.SemaphoreType.DMA((nb,))],
    )(*bufs)


def _pack(arrays, dtype, row_align):
    flat = []
    for arr in arrays:
        v = arr.reshape(-1)
        flat.append(jnp.pad(v, (0, (-v.shape[0]) % LANES)))
    total = sum(f.shape[0] for f in flat) // LANES
    pad_rows = (-total) % row_align
    if pad_rows:
        flat.append(jnp.zeros((pad_rows * LANES,), dtype))
    return jnp.concatenate(flat).reshape(-1, LANES)


def _unpack(buf, shapes, rows_align=1):
    lead = buf.shape[:-2]
    flat = buf.reshape(lead + (-1,))
    out, off = [], 0
    for shape in shapes:
        size = 1
        for dim in shape:
            size *= dim
        out.append(flat[..., off:off + size].reshape(lead + tuple(shape)))
        off += size + (-size) % (LANES * rows_align)
    return out


def _from_chips(parts, axis):
    return jnp.concatenate([parts[j] for j in range(N_CHIPS)], axis=axis)


def kernel(x, ln_g, ln_b, attn_w_in, attn_b_f, attn_w_out, rnn_w_in, rnn_conv_w, rnn_conv_b, rnn_w_a, rnn_b_a, rnn_w_i, rnn_b_i, rnn_lambda, rnn_w_out, loss_target, m_ln_g, m_ln_b, m_attn_w_in, m_attn_b_f, m_attn_w_out, m_rnn_w_in, m_rnn_conv_w, m_rnn_conv_b, m_rnn_w_a, m_rnn_b_a, m_rnn_w_i, m_rnn_b_i, m_rnn_lambda, m_rnn_w_out, v_ln_g, v_ln_b, v_attn_w_in, v_attn_b_f, v_attn_w_out, v_rnn_w_in, v_rnn_conv_w, v_rnn_conv_b, v_rnn_w_a, v_rnn_b_a, v_rnn_w_i, v_rnn_b_i, v_rnn_lambda, v_rnn_w_out):
    s_len, d = x.shape[1], x.shape[2]
    xs = x.reshape(s_len, d)
    target = loss_target.reshape(s_len, d)
    heads = attn_b_f.shape[1]
    qkvg = attn_w_in.shape[2] * N_CHIPS - heads

    me = 2 * lax.axis_index("x") + lax.axis_index("y")
    core = lax.axis_index("c").astype(jnp.int32)
    big = [attn_w_in, attn_w_out, rnn_w_in, rnn_w_a, rnn_w_i, rnn_w_out]
    small = [rnn_conv_w, rnn_conv_b, rnn_b_a, rnn_b_i, rnn_lambda]
    got = _gather_chips([w.astype(BF16) for w in big] + [_pack(small, F32, 16).reshape(2, -1, LANES)])
    w_in_a = jnp.concatenate([got[0][j] for j in range(N_CHIPS)], axis=2)
    w_out_a = jnp.moveaxis(got[1], 0, 1).reshape(2, d, d)
    w_in_r = jnp.moveaxis(got[2], 0, 2).reshape(2, d, 2 * d)
    w_a = jnp.transpose(got[3], (1, 2, 0, 3, 4)).reshape(2, -1, RNN_BLOCK, RNN_BLOCK)
    w_i = jnp.transpose(got[4], (1, 2, 0, 3, 4)).reshape(2, -1, RNN_BLOCK, RNN_BLOCK)
    w_out_r = jnp.moveaxis(got[5], 0, 1).reshape(2, d, d)
    conv_w, conv_b, b_a, b_i, lam = [
        _from_chips(p, ax) for p, ax in zip(_unpack(got[6].reshape(N_CHIPS, -1, LANES), [w.shape for w in small]),
                                            (2, 1, 1, 1, 1))]
    w_cat = jnp.concatenate(
        [w_in_a[:, :, :d] * (HEAD_DIM ** -0.5), w_in_a[:, :, d:qkvg],
         jnp.pad(w_in_a[:, :, qkvg:], ((0, 0), (0, 0), (0, 128 - heads)))], axis=2).astype(BF16)
    b_f = jnp.pad(attn_b_f, ((0, 0), (0, 128 - heads)))

    saved = []
    cur = xs
    for layer in range(DEPTH):
        idx = layer // 2
        g_l, b_l = ln_g[layer:layer + 1], ln_b[layer:layer + 1]
        if layer % 2 == 0:
            q, k, v, gate, fl = _proj(cur, w_cat[idx], [(0, d, BF16), (d, d, BF16), (2 * d, d, BF16),
                                                         (3 * d, d, F32), (4 * d, 128, F32)], "attn_proj")
            cum, sneg = _fcum(fl, b_f[idx:idx + 1])
            o, lse = _flash_fwd(q, k, v, cum)
            nxt, xh, rs, yg = _out_ln(o, gate, cur, w_out_a[idx], g_l, b_l, "out_ln")
            saved.append(dict(x=cur, q=q, k=k, v=v, gate=gate, cum=cum, sneg=sneg, a=o, lse=lse, xh=xh, rs=rs, yg=yg))
        else:
            u, gate = _proj(cur, w_in_r[idx], [(0, d, F32), (d, d, F32)], "rnn_proj")
            vecs = [t[idx:idx + 1] for t in (conv_b, b_a, b_i, lam)]
            hseq, uc, r, ig, a = _rnn_fwd(u, conv_w[idx], vecs[0], w_a[idx], vecs[1], w_i[idx], vecs[2], vecs[3])
            nxt, xh, rs, yg = _out_ln(hseq, gate, cur, w_out_r[idx], g_l, b_l, "out_ln")
            saved.append(dict(x=cur, u=u, gate=gate, uc=uc, r=r, ig=ig, av=a, a=hseq, xh=xh, rs=rs, yg=yg, lam=vecs[3]))
        cur = nxt

    dout, sq = _loss_grad(cur, target)
    loss = lax.psum(0.5 * jnp.sum(sq) / d, ("x", "y", "c"))

    g_ln_g, g_ln_b = [None] * DEPTH, [None] * DEPTH
    g_attn = [None] * 2
    g_rnn = [None] * 2
    slots = None
    for layer in reversed(range(DEPTH)):
        idx = layer // 2
        sv = saved[layer]
        w_out = w_out_a[idx] if layer % 2 == 0 else w_out_r[idx]
        dz, da, dgate, dg, db = _ln_bwd(dout, sv["xh"], sv["rs"], ln_g[layer:layer + 1], w_out, sv["gate"], sv["a"],
                                        "ln_bwd")
        g_ln_g[layer], g_ln_b[layer] = dg, db
        out_rows = (ROWS_ATTN_OUT if layer % 2 == 0 else ROWS_RNN_OUT) + idx * (d // N_CHIPS)
        slots = _dw_out(slots, sv["yg"], dz, out_rows)
        if layer % 2 == 0:
            dq, dk, dv, dcum = _flash_bwd(sv["q"], sv["k"], sv["v"], sv["a"], da, sv["lse"], sv["cum"])
            dlogit, dbf = _fbwd(dcum, sv["sneg"])
            pieces = [dq, dk, dv, dgate, dlogit]
            dout = _mm_nt(dz, [(p, j * d) for j, p in enumerate(pieces)], w_cat[idx], "attn_dx")
            slots = _dw_attn_in(slots, sv["x"], pieces[:4], idx)
            g_attn[idx] = dict(w_f=_mm_tn(sv["x"], dlogit, "dw_f")[:, :heads], b_f=dbf[:, 0])
        else:
            duc, d_wa, d_wi, d_ba, d_bi, d_lam = _rnn_bwd(da, sv["av"], sv["a"], sv["r"], sv["ig"], sv["uc"], sv["lam"],
                                                          w_a[idx], w_i[idx])
            du, d_cw, d_cb = _conv_bwd(duc, sv["u"], conv_w[idx])
            dout = _mm_nt(dz, [(du, 0), (dgate, d)], w_in_r[idx], "rnn_dx")
            slots = _dw_rnn_in(slots, sv["x"], (du, dgate), idx)
            g_rnn[idx] = dict(conv_w=d_cw, conv_b=d_cb[0], w_a=d_wa, b_a=d_ba[0], w_i=d_wi, b_i=d_bi[0], lam=d_lam[0])
    grad_x = dout.reshape(x.shape)

    def by_chip(t, axis):
        shape = t.shape[:axis] + (N_CHIPS, t.shape[axis] // N_CHIPS) + t.shape[axis + 1:]
        flat = jnp.moveaxis(t.reshape(shape), axis, 0).reshape(N_CHIPS, -1)
        return jnp.pad(flat, ((0, 0), (0, (-flat.shape[1]) % (8 * LANES)))).reshape(N_CHIPS, -1, LANES)

    def everywhere(t):
        flat = t.reshape(-1)
        flat = jnp.pad(flat, (0, (-flat.shape[0]) % (8 * LANES))).reshape(-1, LANES)
        return jnp.broadcast_to(flat[None], (N_CHIPS,) + flat.shape)

    def both(key):
        return jnp.stack([it[key] for it in g_rnn])

    in_rows = slots[1:, ROWS_ATTN_IN:ROWS_ATTN_IN + 2 * d, :heads].reshape(3, 2, d, heads)
    edges = jnp.concatenate([in_rows, jnp.stack([it["w_f"] for it in g_attn])[None]])
    rows = [by_chip(both("w_a"), 2), by_chip(both("w_i"), 2), everywhere(edges), by_chip(both("conv_w"), 2),
            by_chip(both("conv_b"), 1), by_chip(both("b_a"), 1), by_chip(both("b_i"), 1), by_chip(both("lam"), 1),
            everywhere(jnp.concatenate(g_ln_g)), everywhere(jnp.concatenate(g_ln_b)),
            everywhere(jnp.stack([it["b_f"] for it in g_attn]))]
    used = sum(r.shape[1] for r in rows)
    small = jnp.concatenate(rows + [jnp.zeros((N_CHIPS, (-used) % 32, LANES), F32)], axis=1)

    def halves(buf):
        return buf.reshape(N_CHIPS, 2, buf.shape[1] // 2, LANES)

    large, small = halves(slots), halves(small)

    core1 = core.reshape(1)
    theirs = _swap_halves([large, small])
    pair = [_pair_sum(core1, large, theirs[0], BF16), _pair_sum(core1, small, theirs[1], F32)]
    summed = []
    for mine_all, got in zip(pair, _scatter_chips(pair)):
        own = lax.dynamic_index_in_dim(mine_all, me, 0, keepdims=False)
        summed.append(_sum_chips(lax.dynamic_update_index_in_dim(got, own, me, 0)))
    reduced = [jnp.concatenate([jnp.where(core == 0, mine, other), jnp.where(core == 0, other, mine)])
               for mine, other in zip(summed, _give_sibling(summed))]
    g_in_group = reduced[0][ROWS_ATTN_IN:ROWS_ATTN_IN + 2 * d].reshape(2, d, d)
    g_w_out_a = reduced[0][ROWS_ATTN_OUT:ROWS_ATTN_OUT + d // 2].reshape(attn_w_out.shape)
    folded = reduced[0][ROWS_RNN_IN:ROWS_RNN_IN + d].reshape(2, d // 2, d)
    g_w_in_r = jnp.concatenate([folded[:, :, :d // 2], folded[:, :, d // 2:]], axis=1)
    g_w_out_r = reduced[0][ROWS_RNN_OUT:ROWS_RNN_OUT + d // 2].reshape(rnn_w_out.shape)
    (g_w_a, g_w_i, g_edges, g_conv_w, g_conv_b, g_b_a, g_b_i, g_lam, g_ln_g_sum, g_ln_b_sum,
     g_b_f) = _unpack(reduced[1], [
        rnn_w_a.shape, rnn_w_i.shape, (N_CHIPS, 2, d, heads), rnn_conv_w.shape, rnn_conv_b.shape, rnn_b_a.shape,
        rnn_b_i.shape, rnn_lambda.shape, ln_g.shape, ln_b.shape, attn_b_f.shape], rows_align=8)
    wide = jnp.concatenate([g_in_group, lax.dynamic_index_in_dim(g_edges, me, 0, keepdims=False)], axis=2)
    g_w_in_a = lax.dynamic_slice(wide, (0, 0, (heads // N_CHIPS) * me), attn_w_in.shape)

    def adam_nd(w, g, m, v, view, rows_per_block):
        outs = _adamw(w.reshape(view), g.reshape(view), m.reshape(view), v.reshape(view), rows_per_block)
        return [t.reshape(w.shape) for t in outs]

    upd = {
        "attn_w_in": adam_nd(attn_w_in, g_w_in_a, m_attn_w_in, v_attn_w_in, attn_w_in.shape, 256),
        "attn_w_out": adam_nd(attn_w_out, g_w_out_a, m_attn_w_out, v_attn_w_out, attn_w_out.shape, 256),
        "rnn_w_in": adam_nd(rnn_w_in, g_w_in_r, m_rnn_w_in, v_rnn_w_in, rnn_w_in.shape, 512),
        "rnn_w_a": adam_nd(rnn_w_a, g_w_a, m_rnn_w_a, v_rnn_w_a, (1, -1, RNN_BLOCK), 512),
        "rnn_w_i": adam_nd(rnn_w_i, g_w_i, m_rnn_w_i, v_rnn_w_i, (1, -1, RNN_BLOCK), 512),
        "rnn_w_out": adam_nd(rnn_w_out, g_w_out_r, m_rnn_w_out, v_rnn_w_out, rnn_w_out.shape, 256),
    }
    smalls = [rnn_conv_w, rnn_conv_b, rnn_b_a, rnn_b_i, rnn_lambda, ln_g, ln_b, attn_b_f]
    g_small = [g_conv_w, g_conv_b, g_b_a, g_b_i, g_lam, g_ln_g_sum, g_ln_b_sum, g_b_f]
    m_small = [m_rnn_conv_w, m_rnn_conv_b, m_rnn_b_a, m_rnn_b_i, m_rnn_lambda, m_ln_g, m_ln_b, m_attn_b_f]
    v_small = [v_rnn_conv_w, v_rnn_conv_b, v_rnn_b_a, v_rnn_b_i, v_rnn_lambda, v_ln_g, v_ln_b, v_attn_b_f]
    packs = [_pack(t, F32, 16)[None] for t in (smalls, g_small, m_small, v_small)]
    small_out = [_unpack(t[0], [w.shape for w in smalls]) for t in _adamw(*packs, 16)]
    for k, name in enumerate(("rnn_conv_w", "rnn_conv_b", "rnn_b_a", "rnn_b_i", "rnn_lambda", "ln_g", "ln_b",
                              "attn_b_f")):
        upd[name] = [small_out[0][k], small_out[1][k], small_out[2][k]]
    grad = {"attn_w_in": g_w_in_a, "attn_w_out": g_w_out_a, "rnn_w_in": g_w_in_r, "rnn_w_a": g_w_a, "rnn_w_i": g_w_i,
            "rnn_w_out": g_w_out_r, "rnn_conv_w": g_conv_w, "rnn_conv_b": g_conv_b, "rnn_b_a": g_b_a,
            "rnn_b_i": g_b_i, "rnn_lambda": g_lam, "ln_g": g_ln_g_sum, "ln_b": g_ln_b_sum, "attn_b_f": g_b_f}
    names = ("ln_g", "ln_b", "attn_w_in", "attn_b_f", "attn_w_out", "rnn_w_in", "rnn_conv_w", "rnn_conv_b",
             "rnn_w_a", "rnn_b_a", "rnn_w_i", "rnn_b_i", "rnn_lambda", "rnn_w_out")
    return (loss, grad_x, *[grad[n] for n in names], *[upd[n][0] for n in names], *[upd[n][1] for n in names],
            *[upd[n][2] for n in names])
```

```python
import functools

import jax
import jax.numpy as jnp
from jax import lax
from jax.experimental import pallas as pl
from jax.experimental.pallas import tpu as pltpu

F32 = jnp.float32
BF16 = jnp.bfloat16
MESH = pl.DeviceIdType.MESH

DEPTH = 4
HEAD_DIM = 64
HEAD_PAIR = 2 * HEAD_DIM
RNN_BLOCK = 256
CONV_WIDTH = 4
LRU_C = 8.0
ALPHA = (2.0 * DEPTH) ** 0.25
LN_EPS = 1e-5
ADAM_LR, ADAM_B1, ADAM_B2, ADAM_EPS, ADAM_WD, ADAM_STEP = 0.001, 0.9, 0.999, 1e-08, 0.01, 10
N_CHIPS = 4
LANES = 1024
NEG = -1e30

NT_DIMS = (((1,), (1,)), ((), ()))
TN_DIMS = (((0,), (0,)), ((), ()))


def _sigmoid(z):
    return 1.0 / (1.0 + jnp.exp(-z))


def _softplus(z):
    return jnp.maximum(z, 0.0) + jnp.log(1.0 + jnp.exp(-jnp.abs(z)))


def _neg_expm1(y):
    poly = y * (1.0 + y * (0.5 + y * (1.0 / 6.0 + y * (1.0 / 24.0))))
    return -jnp.where(y > -0.05, poly, jnp.exp(y) - 1.0)


def _shift_down(cur, prev8, k):
    n = cur.shape[0]
    row = lax.broadcasted_iota(jnp.int32, cur.shape, 0)
    fix = jnp.tile(pltpu.roll(prev8, k, 0), (n // 8, 1))
    return jnp.where(row < k, fix, pltpu.roll(cur, k, 0))


def _shift_up(cur, next8, k):
    n = cur.shape[0]
    row = lax.broadcasted_iota(jnp.int32, cur.shape, 0)
    fix = jnp.tile(pltpu.roll(next8, 8 - k, 0), (n // 8, 1))
    return jnp.where(row >= n - k, fix, pltpu.roll(cur, n - k, 0))


def _proj(x, w, outs, name):
    s_len, d = x.shape
    tm = min(512, s_len)

    def body(x_ref, w_ref, *o_refs):
        xb = x_ref[...].astype(BF16)
        for (c0, wd, dt), o_ref in zip(outs, o_refs):
            step = min(wd, 512)
            for cc in range(0, wd, step):
                o_ref[:, cc:cc + step] = jnp.dot(
                    xb, w_ref[:, c0 + cc:c0 + cc + step], preferred_element_type=F32).astype(dt)

    return pl.pallas_call(
        body, name=name, grid=(s_len // tm,),
        in_specs=[pl.BlockSpec((tm, d), lambda i: (i, 0)), pl.BlockSpec(w.shape, lambda i: (0, 0))],
        out_specs=[pl.BlockSpec((tm, wd), lambda i: (i, 0)) for _, wd, _ in outs],
        out_shape=[jax.ShapeDtypeStruct((s_len, wd), dt) for _, wd, dt in outs],
        compiler_params=pltpu.CompilerParams(dimension_semantics=("parallel",)),
    )(x, w)


def _mm_nt(dz, pieces, w, name):
    s_len, d = dz.shape
    tm = min(256, s_len)
    n = len(pieces)
    cols = [(c0, p.shape[1]) for p, c0 in pieces]

    def body(dz_ref, *rest):
        w_ref, o_ref = rest[n], rest[n + 1]
        acc = ALPHA * dz_ref[...]
        for (c0, wd), p_ref in zip(cols, rest[:n]):
            acc = acc + lax.dot_general(p_ref[...], w_ref[:, c0:c0 + wd], NT_DIMS, preferred_element_type=F32)
        o_ref[...] = acc

    return pl.pallas_call(
        body, name=name, grid=(s_len // tm,),
        in_specs=[pl.BlockSpec((tm, d), lambda i: (i, 0))]
        + [pl.BlockSpec((tm, wd), lambda i: (i, 0)) for _, wd in cols]
        + [pl.BlockSpec(w.shape, lambda i: (0, 0))],
        out_specs=pl.BlockSpec((tm, d), lambda i: (i, 0)),
        out_shape=jax.ShapeDtypeStruct((s_len, d), F32),
        compiler_params=pltpu.CompilerParams(dimension_semantics=("parallel",)),
    )(dz, *[p for p, _ in pieces], w)


def _mm_tn(a, b, name):
    s_len, m = a.shape
    n = b.shape[1]
    tn = min(n, 512)
    ts = min(s_len, 512)

    def body(a_ref, b_ref, o_ref):
        @pl.when(pl.program_id(1) == 0)
        def _():
            o_ref[...] = jnp.zeros_like(o_ref)

        o_ref[...] += lax.dot_general(a_ref[...].astype(BF16), b_ref[...].astype(BF16), TN_DIMS,
                                      preferred_element_type=F32)

    return pl.pallas_call(
        body, name=name, grid=(n // tn, s_len // ts),
        in_specs=[pl.BlockSpec((ts, m), lambda j, s: (s, 0)), pl.BlockSpec((ts, tn), lambda j, s: (s, j))],
        out_specs=pl.BlockSpec((m, tn), lambda j, s: (0, j)),
        out_shape=jax.ShapeDtypeStruct((m, n), F32),
        compiler_params=pltpu.CompilerParams(dimension_semantics=("parallel", "arbitrary")),
    )(a, b)


ROWS_ATTN_IN = 0
ROWS_ATTN_OUT = 2048
ROWS_RNN_IN = 2560
ROWS_RNN_OUT = 3584
SLOT_ROWS = 4096


DW_ROWS = 1024


def _dw_call(body, name, slots, operands, specs, out_block, out_index, grid, semantics):
    return pl.pallas_call(
        body, name=name, grid=grid,
        in_specs=specs if slots is None else [ANY] + specs,
        out_specs=pl.BlockSpec(out_block, out_index),
        out_shape=jax.ShapeDtypeStruct((N_CHIPS, SLOT_ROWS, LANES), F32),
        input_output_aliases={} if slots is None else {0: 0},
        compiler_params=pltpu.CompilerParams(dimension_semantics=semantics),
    )(*(operands if slots is None else [slots] + operands))


def _dw_attn_in(slots, x, pieces, layer):
    s_len, d = x.shape
    ts = min(s_len, DW_ROWS)
    steps = s_len // ts
    half = d // 2

    def body(*refs):
        x_ref, p_refs, o_ref = refs[-6], refs[-5:-1], refs[-1]

        @pl.when(pl.program_id(1) == 0)
        def _():
            o_ref[...] = jnp.zeros_like(o_ref)

        xb = x_ref[...].astype(BF16)
        for j, p_ref in enumerate(p_refs):
            o_ref[j] += lax.dot_general(xb, p_ref[...], TN_DIMS, preferred_element_type=F32)

        @pl.when(pl.program_id(1) == steps - 1)
        def _():
            o_ref[0] = o_ref[0] * (HEAD_DIM ** -0.5)

    specs = [pl.BlockSpec((ts, d), lambda i, s: (s, 0))] + [pl.BlockSpec((ts, half), lambda i, s: (s, i))] * 4
    return _dw_call(body, "dw_attn_in", slots, [x] + list(pieces), specs, (N_CHIPS, d, half),
                    lambda i, s: (0, ROWS_ATTN_IN // d + layer, i), (2, steps), ("parallel", "arbitrary"))


def _dw_out(slots, yg, dz, first_row):
    s_len, d = dz.shape
    ts = min(s_len, DW_ROWS)
    rows = d // N_CHIPS

    def body(*refs):
        a_ref, b_ref, o_ref = refs[-3:]

        @pl.when(pl.program_id(0) == 0)
        def _():
            o_ref[...] = jnp.zeros_like(o_ref)

        prod = lax.dot_general(a_ref[...], b_ref[...].astype(BF16), TN_DIMS, preferred_element_type=F32)
        o_ref[...] += prod.reshape(N_CHIPS, rows, d)

    specs = [pl.BlockSpec((ts, d), lambda s: (s, 0))] * 2
    return _dw_call(body, "dw_out", slots, [yg, dz], specs, (N_CHIPS, rows, d), lambda s: (0, first_row // rows, 0),
                    (s_len // ts,), ("arbitrary",))


def _dw_rnn_in(slots, x, parts, layer):
    s_len, d = x.shape
    ts = min(s_len, DW_ROWS)
    half = d // 2

    def body(*refs):
        x_ref, p_refs, o_ref = refs[-4], refs[-3:-1], refs[-1]

        @pl.when(pl.program_id(0) == 0)
        def _():
            o_ref[...] = jnp.zeros_like(o_ref)

        xb = x_ref[...].astype(BF16)
        for p, p_ref in enumerate(p_refs):
            for jj in (0, 1):
                for r in (0, 1):
                    o_ref[2 * p + jj, :, r * half:(r + 1) * half] += lax.dot_general(
                        xb[:, r * half:(r + 1) * half], p_ref[:, jj * half:(jj + 1) * half], TN_DIMS,
                        preferred_element_type=F32)

    specs = [pl.BlockSpec((ts, d), lambda s: (s, 0))] * 3
    return _dw_call(body, "dw_rnn_in", slots, [x] + list(parts), specs, (N_CHIPS, half, d),
                    lambda s: (0, ROWS_RNN_IN // half + layer, 0), (s_len // ts,), ("arbitrary",))


def _fcum(fl, bias):
    s_len = fl.shape[0]

    def body(fl_ref, b_ref, cum_ref, sg_ref):
        z = fl_ref[...] + b_ref[...]
        e = jnp.exp(-jnp.abs(z))
        logf = jnp.minimum(z, 0.0) - jnp.log(1.0 + e)
        sneg = jnp.where(z >= 0, e, 1.0) / (1.0 + e)
        run = logf.T[0:16, :]
        sg_ref[...] = sneg.T[0:16, :]
        lane = lax.broadcasted_iota(jnp.int32, (16, s_len), 1)
        sh = 1
        while sh < s_len:
            run = run + jnp.where(lane >= sh, pltpu.roll(run, sh, 1), 0.0)
            sh *= 2
        cum_ref[...] = run

    return pl.pallas_call(
        body, name="fcum",
        out_shape=[jax.ShapeDtypeStruct((16, s_len), F32), jax.ShapeDtypeStruct((16, s_len), F32)],
    )(fl, bias)


def _fbwd(dcum, sneg):
    s_len = dcum.shape[1]

    def body(dc_ref, sg_ref, dl_ref, db_ref):
        run = dc_ref[...]
        lane = lax.broadcasted_iota(jnp.int32, (16, s_len), 1)
        sh = 1
        while sh < s_len:
            run = run + jnp.where(lane < s_len - sh, pltpu.roll(run, s_len - sh, 1), 0.0)
            sh *= 2
        dlog = run * sg_ref[...]
        db_ref[...] = jnp.broadcast_to(jnp.sum(dlog, axis=1, keepdims=True), (16, 128))
        full = jnp.concatenate([dlog, jnp.zeros((112, s_len), F32)], axis=0)
        dl_ref[...] = full.T.astype(BF16)

    return pl.pallas_call(
        body, name="fbwd",
        out_shape=[jax.ShapeDtypeStruct((s_len, 128), BF16), jax.ShapeDtypeStruct((16, 128), F32)],
    )(dcum, sneg)


FWD_TILE = 1024
BWD_TILE = 512


def _flash_fwd(q, k, v, cum, shards=()):
    s_len, width = q.shape
    tile = min(FWD_TILE, s_len // 2)
    nt = s_len // tile
    reps = tile // 128
    cumr = cum.reshape(-1, tile)
    ns = len(shards)
    pairs = width // HEAD_PAIR

    def body(*refs):
        q_ref, k_ref, v_ref, cum_ref = refs[:4]
        o_ref, lse_ref = refs[4 + ns:6 + ns]
        q_t, v_t, cum_col = refs[6 + 2 * ns:9 + 2 * ns]
        pid = pl.program_id(0)
        if ns:
            start, forward, finish = _gather_steps(refs[4:4 + ns], refs[6 + ns:6 + 2 * ns], *refs[9 + 2 * ns:])
            pl.when(pid == 0)(start)
            pl.when(pid == pairs // 2)(forward)
        top = lax.broadcasted_iota(jnp.int32, (HEAD_PAIR, tile), 0) < HEAD_DIM
        causal = (lax.broadcasted_iota(jnp.int32, (tile, tile), 0)
                  <= lax.broadcasted_iota(jnp.int32, (tile, tile), 1))

        def pre(i, carry):
            r0 = pl.multiple_of(i * tile, tile)
            q_t[i] = q_ref[pl.ds(r0, tile), :].astype(F32).T.astype(BF16)
            v_t[i] = v_ref[pl.ds(r0, tile), :].astype(F32).T.astype(BF16)
            for h in (0, 1):
                row = cum_ref[pl.ds((2 * pid + h) * nt + i, 1), :]
                cum_col[h, pl.ds(r0, tile), :] = jnp.broadcast_to(row, (HEAD_PAIR, tile)).T
            return carry

        lax.fori_loop(0, nt, pre, 0)

        def q_loop(qi, carry):
            qt = q_t[qi]
            zt = jnp.zeros_like(qt)
            qms = (jnp.where(top, qt, zt), jnp.where(top, zt, qt))

            def step(kj, state, masked):
                m0, l0, m1, l1, acc = state
                k0 = pl.multiple_of(kj * tile, tile)
                kt = k_ref[pl.ds(k0, tile), :]
                vt = v_t[kj]
                ms, ls, scales, contrib = [m0, m1], [l0, l1], [], None
                for h in (0, 1):
                    s = jnp.dot(kt, qms[h], preferred_element_type=F32)
                    s = s - jnp.tile(cum_col[h, pl.ds(k0, tile), :], (1, reps))
                    if masked:
                        s = jnp.where(causal, s, NEG)
                    m_new = jnp.maximum(ms[h], jnp.max(s, axis=0, keepdims=True))
                    p = jnp.exp(s - m_new)
                    scale = jnp.exp(ms[h] - m_new)
                    ls[h] = scale * ls[h] + jnp.sum(p, axis=0, keepdims=True)
                    ms[h] = m_new
                    scales.append(scale)
                    vm = jnp.where(top, vt, zt) if h == 0 else jnp.where(top, zt, vt)
                    part = jnp.dot(vm, p.astype(BF16), preferred_element_type=F32)
                    contrib = part if contrib is None else contrib + part
                acc = jnp.where(top, scales[0], scales[1]) * acc + contrib
                return ms[0], ls[0], ms[1], ls[1], acc

            low = jnp.full((1, tile), NEG, F32)
            zero = jnp.zeros((1, tile), F32)
            state = step(qi, (low, zero, low, zero, jnp.zeros((HEAD_PAIR, tile), F32)), True)
            m0, l0, m1, l1, acc = lax.fori_loop(0, qi, lambda kj, c: step(kj, c, False), state)
            q0 = pl.multiple_of(qi * tile, tile)
            o_ref[pl.ds(q0, tile), :] = (acc / jnp.where(top, l0, l1)).T
            lse_ref[pl.ds((2 * pid) * nt + qi, 1), :] = m0 + jnp.log(l0)
            lse_ref[pl.ds((2 * pid + 1) * nt + qi, 1), :] = m1 + jnp.log(l1)
            return carry

        lax.fori_loop(0, nt, q_loop, 0)
        if ns:
            pl.when(pid == pairs - 1)(finish)

    blk = pl.BlockSpec((s_len, HEAD_PAIR), lambda p: (0, p))
    full = pl.BlockSpec(cumr.shape, lambda p: (0, 0))
    sems = [pltpu.SemaphoreType.DMA((GATHER_SEMS * ns,))] * 2 if ns else []
    o, lse, *gathered = pl.pallas_call(
        body, name="flash_fwd_gather" if ns else "flash_fwd", grid=(pairs,),
        in_specs=[blk, blk, blk, full] + [ANY] * ns,
        out_specs=[blk, full] + [ANY] * ns,
        out_shape=[jax.ShapeDtypeStruct((s_len, width), F32), jax.ShapeDtypeStruct(cumr.shape, F32)]
        + [jax.ShapeDtypeStruct((N_CHIPS,) + s.shape, s.dtype) for s in shards],
        scratch_shapes=[pltpu.VMEM((nt, HEAD_PAIR, tile), BF16), pltpu.VMEM((nt, HEAD_PAIR, tile), BF16),
                        pltpu.VMEM((2, s_len, HEAD_PAIR), F32)] + sems,
        compiler_params=pltpu.CompilerParams(dimension_semantics=("arbitrary",)),
    )(q, k, v, cumr, *shards)
    return (o, lse.reshape(cum.shape), *gathered)


def _flash_bwd(q, k, v, o, do, lse, cum):
    s_len, width = q.shape
    tile = min(BWD_TILE, s_len // 2)
    nt = s_len // tile
    reps = tile // 128
    cumr = cum.reshape(-1, tile)
    lse = lse.reshape(-1, tile)

    def body(q_ref, k_ref, v_ref, o_ref, do_ref, lse_ref, cum_ref, dq_ref, dk_ref, dv_ref, dcum_ref,
             q_t, do_t, k_t, do_bf, cum_col, dq_t_acc, delta, q_side, dk_acc, dv_acc, k_side):
        pid = pl.program_id(0)
        top = lax.broadcasted_iota(jnp.int32, (HEAD_PAIR, tile), 0) < HEAD_DIM
        left = lax.broadcasted_iota(jnp.int32, (tile, HEAD_PAIR), 1) < HEAD_DIM
        causal = (lax.broadcasted_iota(jnp.int32, (tile, tile), 0)
                  <= lax.broadcasted_iota(jnp.int32, (tile, tile), 1))

        def pre(i, carry):
            r0 = pl.multiple_of(i * tile, tile)
            d_o = do_ref[pl.ds(r0, tile), :]
            prod_t = (d_o * o_ref[pl.ds(r0, tile), :]).T
            delta[pl.ds(i, 1), :] = jnp.sum(prod_t[:HEAD_DIM], axis=0, keepdims=True)
            delta[pl.ds(nt + i, 1), :] = jnp.sum(prod_t[HEAD_DIM:], axis=0, keepdims=True)
            do_bf[pl.ds(r0, tile), :] = d_o.astype(BF16)
            do_t[i] = d_o.T.astype(BF16)
            q_t[i] = q_ref[pl.ds(r0, tile), :].astype(F32).T.astype(BF16)
            k_t[i] = k_ref[pl.ds(r0, tile), :].astype(F32).T.astype(BF16)
            dq_t_acc[i] = jnp.zeros((HEAD_PAIR, tile), F32)
            for h in (0, 1):
                row = cum_ref[pl.ds((2 * pid + h) * nt + i, 1), :]
                cum_col[h, pl.ds(r0, tile), :] = jnp.broadcast_to(row, (HEAD_PAIR, tile)).T
                q_side[pl.ds(h * nt + i, 1), :] = jnp.zeros((1, tile), F32)
            return carry

        lax.fori_loop(0, nt, pre, 0)

        def kv_loop(kj, carry):
            k0 = pl.multiple_of(kj * tile, tile)
            kt = k_ref[pl.ds(k0, tile), :]
            vt = v_ref[pl.ds(k0, tile), :]
            ktt = k_t[kj]
            zt = jnp.zeros_like(ktt)
            zr = jnp.zeros_like(kt)
            kms = (jnp.where(top, ktt, zt), jnp.where(top, zt, ktt))
            cols = [jnp.tile(cum_col[h, pl.ds(k0, tile), :], (1, reps)) for h in (0, 1)]
            dk_acc[...] = jnp.zeros_like(dk_acc)
            dv_acc[...] = jnp.zeros_like(dv_acc)
            k_side[...] = jnp.zeros_like(k_side)

            def step(qi, carry, masked):
                q0 = pl.multiple_of(qi * tile, tile)
                qtt = q_t[qi]
                dtt = do_t[qi]
                q_rows = q_ref[pl.ds(q0, tile), :]
                do_rows = do_bf[pl.ds(q0, tile), :]
                dq_part = None
                for h in (0, 1):
                    q_m = jnp.where(top, qtt, zt) if h == 0 else jnp.where(top, zt, qtt)
                    do_m = jnp.where(top, dtt, zt) if h == 0 else jnp.where(top, zt, dtt)
                    s = jnp.dot(kt, q_m, preferred_element_type=F32) - cols[h]
                    if masked:
                        s = jnp.where(causal, s, NEG)
                    p = jnp.exp(s - lse_ref[pl.ds((2 * pid + h) * nt + qi, 1), :])
                    dp = jnp.dot(vt, do_m, preferred_element_type=F32)
                    ds = p * (dp - delta[pl.ds(h * nt + qi, 1), :])
                    q_side[pl.ds(h * nt + qi, 1), :] += jnp.sum(ds, axis=0, keepdims=True)
                    folded = ds[:, :128]
                    for b in range(1, reps):
                        folded = folded + ds[:, b * 128:(b + 1) * 128]
                    k_side[h] += folded
                    pb = p.astype(BF16)
                    dsb = ds.astype(BF16)
                    do_h = jnp.where(left, do_rows, zr) if h == 0 else jnp.where(left, zr, do_rows)
                    q_h = jnp.where(left, q_rows, zr) if h == 0 else jnp.where(left, zr, q_rows)
                    dv_acc[...] += jnp.dot(pb, do_h, preferred_element_type=F32)
                    dk_acc[...] += jnp.dot(dsb, q_h, preferred_element_type=F32)
                    part = jnp.dot(kms[h], dsb, preferred_element_type=F32)
                    dq_part = part if dq_part is None else dq_part + part
                dq_t_acc[qi] += dq_part
                return carry

            step(kj, 0, True)
            lax.fori_loop(kj + 1, nt, lambda qi, c: step(qi, c, False), 0)
            dk_ref[pl.ds(k0, tile), :] = dk_acc[...].astype(BF16)
            dv_ref[pl.ds(k0, tile), :] = dv_acc[...].astype(BF16)
            for h in (0, 1):
                dcum_ref[pl.ds((2 * pid + h) * nt + kj, 1), :] = -jnp.sum(k_side[h].T, axis=0, keepdims=True)
            return carry

        lax.fori_loop(0, nt, kv_loop, 0)

        def fin(i, carry):
            r0 = pl.multiple_of(i * tile, tile)
            dq_ref[pl.ds(r0, tile), :] = dq_t_acc[i].T.astype(BF16)
            for h in (0, 1):
                dcum_ref[pl.ds((2 * pid + h) * nt + i, 1), :] += q_side[pl.ds(h * nt + i, 1), :]
            return carry

        lax.fori_loop(0, nt, fin, 0)

    blk = pl.BlockSpec((s_len, HEAD_PAIR), lambda p: (0, p))
    full = pl.BlockSpec(cumr.shape, lambda p: (0, 0))
    transposed = pltpu.VMEM((nt, HEAD_PAIR, tile), BF16)
    dq, dk, dv, dcum = pl.pallas_call(
        body, name="flash_bwd", grid=(width // HEAD_PAIR,),
        in_specs=[blk, blk, blk, blk, blk, full, full],
        out_specs=[blk, blk, blk, full],
        out_shape=[jax.ShapeDtypeStruct((s_len, width), BF16)] * 3 + [jax.ShapeDtypeStruct(cumr.shape, F32)],
        scratch_shapes=[transposed, transposed, transposed, pltpu.VMEM((s_len, HEAD_PAIR), BF16),
                        pltpu.VMEM((2, s_len, HEAD_PAIR), F32), pltpu.VMEM((nt, HEAD_PAIR, tile), F32),
                        pltpu.VMEM((2 * nt, tile), F32), pltpu.VMEM((2 * nt, tile), F32),
                        pltpu.VMEM((tile, HEAD_PAIR), F32), pltpu.VMEM((tile, HEAD_PAIR), F32),
                        pltpu.VMEM((2, tile, HEAD_PAIR), F32)],
        compiler_params=pltpu.CompilerParams(dimension_semantics=("arbitrary",)),
    )(q, k, v, o, do, lse, cumr)
    return dq, dk, dv, dcum.reshape(cum.shape)


def _out_ln(a, gate, x, w, g, b, name):
    s_len, d = x.shape
    tm = min(256, s_len)

    def body(a_ref, gate_ref, x_ref, w_ref, g_ref, b_ref, xn_ref, xh_ref, rs_ref, yg_ref):
        gt = gate_ref[...]
        yg = (a_ref[...] * (gt * _sigmoid(gt))).astype(BF16)
        yg_ref[...] = yg
        z = ALPHA * x_ref[...] + jnp.dot(yg, w_ref[...], preferred_element_type=F32)
        zc = z - jnp.mean(z, axis=1, keepdims=True)
        rstd = lax.rsqrt(jnp.mean(zc * zc, axis=1, keepdims=True) + LN_EPS)
        xh = zc * rstd
        xh_ref[...] = xh
        xn_ref[...] = xh * g_ref[...] + b_ref[...]
        rs_ref[...] = jnp.broadcast_to(rstd, (tm, 128))

    row = pl.BlockSpec((tm, d), lambda i: (i, 0))
    vec = pl.BlockSpec((1, d), lambda i: (0, 0))
    return pl.pallas_call(
        body, name=name, grid=(s_len // tm,),
        in_specs=[row, row, row, pl.BlockSpec(w.shape, lambda i: (0, 0)), vec, vec],
        out_specs=[row, row, pl.BlockSpec((tm, 128), lambda i: (i, 0)), row],
        out_shape=[jax.ShapeDtypeStruct((s_len, d), F32), jax.ShapeDtypeStruct((s_len, d), F32),
                   jax.ShapeDtypeStruct((s_len, 128), F32), jax.ShapeDtypeStruct((s_len, d), BF16)],
        compiler_params=pltpu.CompilerParams(dimension_semantics=("parallel",)),
    )(a, gate, x, w, g, b)


def _ln_bwd(dout, xh, rs, g, w, gate, a, name):
    s_len, d = dout.shape
    tm = min(256, s_len)

    def body(do_ref, xh_ref, rs_ref, g_ref, w_ref, gate_ref, a_ref, dz_ref, da_ref, dgate_ref, dg_ref, db_ref):
        @pl.when(pl.program_id(0) == 0)
        def _():
            dg_ref[...] = jnp.zeros_like(dg_ref)
            db_ref[...] = jnp.zeros_like(db_ref)

        dout_t = do_ref[...]
        xh_t = xh_ref[...]
        dg_ref[...] += jnp.sum(dout_t * xh_t, axis=0, keepdims=True)
        db_ref[...] += jnp.sum(dout_t, axis=0, keepdims=True)
        dxh = dout_t * g_ref[...]
        m1 = jnp.mean(dxh, axis=1, keepdims=True)
        m2 = jnp.mean(dxh * xh_t, axis=1, keepdims=True)
        dz = rs_ref[:, 0:1] * (dxh - m1 - xh_t * m2)
        dz_ref[...] = dz
        dyg = lax.dot_general(dz.astype(BF16), w_ref[...], NT_DIMS, preferred_element_type=F32)
        gt = gate_ref[...]
        sg = _sigmoid(gt)
        da_ref[...] = dyg * (gt * sg)
        dgate_ref[...] = (dyg * a_ref[...] * (sg * (1.0 + gt * (1.0 - sg)))).astype(BF16)

    row = pl.BlockSpec((tm, d), lambda i: (i, 0))
    vec = pl.BlockSpec((1, d), lambda i: (0, 0))
    return pl.pallas_call(
        body, name=name, grid=(s_len // tm,),
        in_specs=[row, row, pl.BlockSpec((tm, 128), lambda i: (i, 0)), vec, pl.BlockSpec(w.shape, lambda i: (0, 0)),
                  row, row],
        out_specs=[row, row, row, vec, vec],
        out_shape=[jax.ShapeDtypeStruct((s_len, d), F32), jax.ShapeDtypeStruct((s_len, d), F32),
                   jax.ShapeDtypeStruct((s_len, d), BF16), jax.ShapeDtypeStruct((1, d), F32),
                   jax.ShapeDtypeStruct((1, d), F32)],
        compiler_params=pltpu.CompilerParams(dimension_semantics=("arbitrary",)),
    )(dout, xh, rs, g, w, gate, a)


def _loss_grad(y, target):
    s_len, d = y.shape
    tm = min(512, s_len)

    def body(y_ref, t_ref, dy_ref, acc_ref):
        @pl.when(pl.program_id(0) == 0)
        def _():
            acc_ref[...] = jnp.zeros_like(acc_ref)

        diff = y_ref[...] - t_ref[...]
        dy_ref[...] = diff * (1.0 / d)
        acc_ref[...] += jnp.sum(diff * diff, axis=0, keepdims=True)

    row = pl.BlockSpec((tm, d), lambda i: (i, 0))
    return pl.pallas_call(
        body, name="loss_grad", grid=(s_len // tm,),
        in_specs=[row, row], out_specs=[row, pl.BlockSpec((1, d), lambda i: (0, 0))],
        out_shape=[jax.ShapeDtypeStruct((s_len, d), F32), jax.ShapeDtypeStruct((1, d), F32)],
        compiler_params=pltpu.CompilerParams(dimension_semantics=("arbitrary",)),
    )(y, target)


def _rnn_fwd(u, conv_w, conv_b, wa, ba, wi, bi, lam):
    s_len, width = u.shape
    tm = min(256, s_len)
    nb = width // RNN_BLOCK

    def body(u_ref, up_ref, cw_ref, cb_ref, wa_ref, ba_ref, wi_ref, bi_ref, lam_ref,
             h_ref, uc_ref, r_ref, i_ref, a_ref, b_scr, h_carry):
        step_id = pl.program_id(0)

        @pl.when(step_id == 0)
        def _():
            h_carry[...] = jnp.zeros_like(h_carry)

        for n in range(nb):
            blk = slice(n * RNN_BLOCK, (n + 1) * RNN_BLOCK)
            ut = u_ref[:, blk]
            prev = jnp.where(step_id > 0, up_ref[:, blk], 0.0)
            uc = cb_ref[:, blk] + cw_ref[3:4, blk] * ut
            for k in (1, 2, 3):
                uc = uc + cw_ref[3 - k:4 - k, blk] * _shift_down(ut, prev, k)
            ucb = uc.astype(BF16)
            r = _sigmoid(jnp.dot(ucb, wa_ref[n], preferred_element_type=F32) + ba_ref[:, blk])
            ig = _sigmoid(jnp.dot(ucb, wi_ref[n], preferred_element_type=F32) + bi_ref[:, blk])
            log_a = -LRU_C * r * _softplus(-lam_ref[:, blk])
            uc_ref[:, blk] = uc
            r_ref[:, blk] = r
            i_ref[:, blk] = ig
            a_ref[:, blk] = jnp.exp(log_a)
            b_scr[:, blk] = jnp.sqrt(_neg_expm1(2.0 * log_a)) * (ig * uc)

        def scan(t, h):
            h = a_ref[pl.ds(t, 1), :] * h + b_scr[pl.ds(t, 1), :]
            h_ref[pl.ds(t, 1), :] = h
            return h

        h_carry[...] = lax.fori_loop(0, tm, scan, h_carry[...], unroll=8)

    row = pl.BlockSpec((tm, width), lambda i: (i, 0))
    prev8 = pl.BlockSpec((8, width), lambda i: (jnp.maximum(i * (tm // 8) - 1, 0), 0))
    vec = pl.BlockSpec((1, width), lambda i: (0, 0))
    mat = pl.BlockSpec(wa.shape, lambda i: (0, 0, 0))
    return pl.pallas_call(
        body, name="rnn_fwd", grid=(s_len // tm,),
        in_specs=[row, prev8, pl.BlockSpec(conv_w.shape, lambda i: (0, 0)), vec, mat, vec, mat, vec, vec],
        out_specs=[row] * 5,
        out_shape=[jax.ShapeDtypeStruct((s_len, width), F32)] * 5,
        scratch_shapes=[pltpu.VMEM((tm, width), F32), pltpu.VMEM((1, width), F32)],
        compiler_params=pltpu.CompilerParams(dimension_semantics=("arbitrary",)),
    )(u, u, conv_w, conv_b, wa, ba, wi, bi, lam)


def _rnn_bwd(dh, a, h, r, ig, uc, lam, wa, wi):
    s_len, width = dh.shape
    tm = min(256, s_len)
    nt = s_len // tm
    nb = width // RNN_BLOCK

    def body(dh_ref, a_ref, h_ref, hp_ref, r_ref, i_ref, uc_ref, lam_ref, wa_ref, wi_ref,
             duc_ref, dwa_ref, dwi_ref, dba_ref, dbi_ref, dlam_ref, g_scr, c_scr, dsp_scr):
        step_id = pl.program_id(0)
        tile_id = nt - 1 - step_id

        @pl.when(step_id == 0)
        def _():
            c_scr[...] = jnp.zeros_like(c_scr)
            dsp_scr[...] = jnp.zeros_like(dsp_scr)
            dwa_ref[...] = jnp.zeros_like(dwa_ref)
            dwi_ref[...] = jnp.zeros_like(dwi_ref)
            dba_ref[...] = jnp.zeros_like(dba_ref)
            dbi_ref[...] = jnp.zeros_like(dbi_ref)

        def scan(j, c):
            t = tm - 1 - j
            g = dh_ref[pl.ds(t, 1), :] + c
            g_scr[pl.ds(t, 1), :] = g
            return a_ref[pl.ds(t, 1), :] * g

        c_scr[...] = lax.fori_loop(0, tm, scan, c_scr[...], unroll=8)

        for n in range(nb):
            blk = slice(n * RNN_BLOCK, (n + 1) * RNN_BLOCK)
            g_t = g_scr[:, blk]
            hp8 = jnp.where(tile_id > 0, hp_ref[:, blk], 0.0)
            h_prev = _shift_down(h_ref[:, blk], hp8, 1)
            av, rv, iv, ucv = a_ref[:, blk], r_ref[:, blk], i_ref[:, blk], uc_ref[:, blk]
            sp = _softplus(-lam_ref[:, blk])
            mag = jnp.sqrt(_neg_expm1(-2.0 * LRU_C * rv * sp))
            d_iu = g_t * mag
            dla = g_t * h_prev * av - g_t * (iv * ucv) * (av * av) / mag
            dsp_scr[:, blk] += jnp.sum(dla * (-LRU_C * rv), axis=0, keepdims=True)
            dra = dla * (-LRU_C * sp) * rv * (1.0 - rv)
            dia = d_iu * ucv * iv * (1.0 - iv)
            drab, diab, ucb = dra.astype(BF16), dia.astype(BF16), ucv.astype(BF16)
            duc_ref[:, blk] = (d_iu * iv
                               + lax.dot_general(drab, wa_ref[n], NT_DIMS, preferred_element_type=F32)
                               + lax.dot_general(diab, wi_ref[n], NT_DIMS, preferred_element_type=F32))
            dwa_ref[n] += lax.dot_general(ucb, drab, TN_DIMS, preferred_element_type=F32)
            dwi_ref[n] += lax.dot_general(ucb, diab, TN_DIMS, preferred_element_type=F32)
            dba_ref[:, blk] += jnp.sum(dra, axis=0, keepdims=True)
            dbi_ref[:, blk] += jnp.sum(dia, axis=0, keepdims=True)

        @pl.when(step_id == nt - 1)
        def _():
            dlam_ref[...] = -dsp_scr[...] * _sigmoid(-lam_ref[...])

    row = pl.BlockSpec((tm, width), lambda i: (nt - 1 - i, 0))
    prev8 = pl.BlockSpec((8, width), lambda i: (jnp.maximum((nt - 1 - i) * (tm // 8) - 1, 0), 0))
    vec = pl.BlockSpec((1, width), lambda i: (0, 0))
    mat = pl.BlockSpec(wa.shape, lambda i: (0, 0, 0))
    return pl.pallas_call(
        body, name="rnn_bwd", grid=(nt,),
        in_specs=[row, row, row, prev8, row, row, row, vec, mat, mat],
        out_specs=[row, mat, mat, vec, vec, vec],
        out_shape=[jax.ShapeDtypeStruct((s_len, width), F32), jax.ShapeDtypeStruct(wa.shape, F32),
                   jax.ShapeDtypeStruct(wa.shape, F32)] + [jax.ShapeDtypeStruct((1, width), F32)] * 3,
        scratch_shapes=[pltpu.VMEM((tm, width), F32), pltpu.VMEM((1, width), F32), pltpu.VMEM((1, width), F32)],
        compiler_params=pltpu.CompilerParams(dimension_semantics=("arbitrary",)),
    )(dh, a, h, h, r, ig, uc, lam, wa, wi)


def _conv_bwd(duc, u, conv_w):
    s_len, width = duc.shape
    tm = min(256, s_len)
    nt = s_len // tm

    def body(d_ref, dn_ref, u_ref, up_ref, cw_ref, du_ref, dcw_ref, dcb_ref):
        step_id = pl.program_id(0)

        @pl.when(step_id == 0)
        def _():
            dcw_ref[...] = jnp.zeros_like(dcw_ref)
            dcb_ref[...] = jnp.zeros_like(dcb_ref)

        for n in range(width // RNN_BLOCK):
            blk = slice(n * RNN_BLOCK, (n + 1) * RNN_BLOCK)
            dt = d_ref[:, blk]
            ut = u_ref[:, blk]
            nxt = jnp.where(step_id < nt - 1, dn_ref[:, blk], 0.0)
            prev = jnp.where(step_id > 0, up_ref[:, blk], 0.0)
            du = cw_ref[3:4, blk] * dt
            dcw_ref[3:4, blk] += jnp.sum(dt * ut, axis=0, keepdims=True)
            for k in (1, 2, 3):
                du = du + cw_ref[3 - k:4 - k, blk] * _shift_up(dt, nxt, k)
                dcw_ref[3 - k:4 - k, blk] += jnp.sum(dt * _shift_down(ut, prev, k), axis=0, keepdims=True)
            du_ref[:, blk] = du.astype(BF16)
            dcb_ref[:, blk] += jnp.sum(dt, axis=0, keepdims=True)

    row = pl.BlockSpec((tm, width), lambda i: (i, 0))
    prev8 = pl.BlockSpec((8, width), lambda i: (jnp.maximum(i * (tm // 8) - 1, 0), 0))
    next8 = pl.BlockSpec((8, width), lambda i: (jnp.minimum((i + 1) * (tm // 8), s_len // 8 - 1), 0))
    return pl.pallas_call(
        body, name="conv_bwd", grid=(nt,),
        in_specs=[row, next8, row, prev8, pl.BlockSpec(conv_w.shape, lambda i: (0, 0))],
        out_specs=[row, pl.BlockSpec(conv_w.shape, lambda i: (0, 0)), pl.BlockSpec((1, width), lambda i: (0, 0))],
        out_shape=[jax.ShapeDtypeStruct((s_len, width), BF16), jax.ShapeDtypeStruct(conv_w.shape, F32),
                   jax.ShapeDtypeStruct((1, width), F32)],
        compiler_params=pltpu.CompilerParams(dimension_semantics=("arbitrary",)),
    )(duc, duc, u, u, conv_w)


def _pair_sum(core, grads, theirs, out_dtype):
    n, _, half, _ = grads.shape
    tb = 128 if half % 128 == 0 else half

    def body(c_ref, a_ref, b_ref, o_ref):
        o_ref[...] = (a_ref[...] + b_ref[...]).astype(out_dtype)

    blk = pl.BlockSpec((n, tb, LANES), lambda i, c: (0, i, 0))
    return pl.pallas_call(
        body, name="pair_sum",
        grid_spec=pltpu.PrefetchScalarGridSpec(
            num_scalar_prefetch=1, grid=(half // tb,),
            in_specs=[pl.BlockSpec((n, None, tb, LANES), lambda i, c: (0, c[0], i, 0)), blk], out_specs=blk),
        out_shape=jax.ShapeDtypeStruct((n, half, LANES), out_dtype),
        compiler_params=pltpu.CompilerParams(dimension_semantics=("parallel",)),
    )(core, grads, theirs)


def _sum_chips(parts):
    rows = parts.shape[1]
    tb = 128 if rows % 128 == 0 else rows

    def body(p_ref, o_ref):
        o_ref[...] = ((p_ref[0].astype(F32) + p_ref[1].astype(F32)) + p_ref[2].astype(F32)) + p_ref[3].astype(F32)

    return pl.pallas_call(
        body, name="chip_sum", grid=(rows // tb,),
        in_specs=[pl.BlockSpec((N_CHIPS, tb, LANES), lambda i: (0, i, 0))],
        out_specs=pl.BlockSpec((tb, LANES), lambda i: (i, 0)),
        out_shape=jax.ShapeDtypeStruct((rows, LANES), F32),
        compiler_params=pltpu.CompilerParams(dimension_semantics=("parallel",)),
    )(parts)


def _adamw(w, g, m, v, lead_per_block, rows_per_block):
    n, rows, cols = w.shape
    blk = pl.BlockSpec((lead_per_block, rows_per_block, cols), lambda i, j: (i, j, 0))

    def body(w_ref, g_ref, m_ref, v_ref, d_ref, nm_ref, nv_ref):
        gt = g_ref[...]
        nm = ADAM_B1 * m_ref[...] + (1.0 - ADAM_B1) * gt
        nv = ADAM_B2 * v_ref[...] + (1.0 - ADAM_B2) * (gt * gt)
        m_hat = nm / (1.0 - ADAM_B1 ** ADAM_STEP)
        v_hat = nv / (1.0 - ADAM_B2 ** ADAM_STEP)
        d_ref[...] = -ADAM_LR * (m_hat / (jnp.sqrt(v_hat) + ADAM_EPS) + ADAM_WD * w_ref[...])
        nm_ref[...] = nm
        nv_ref[...] = nv

    return pl.pallas_call(
        body, name="adamw", grid=(n // lead_per_block, rows // rows_per_block), in_specs=[blk] * 4,
        out_specs=[blk] * 3,
        out_shape=[jax.ShapeDtypeStruct(w.shape, F32)] * 3,
        compiler_params=pltpu.CompilerParams(dimension_semantics=("parallel", "parallel")),
    )(w, g, m, v)


def _place():
    x, y, c = lax.axis_index("x"), lax.axis_index("y"), lax.axis_index("c")
    return x, y, c, [(1 - x, y), (x, 1 - y), (1 - x, 1 - y)]


ANY = pl.BlockSpec(memory_space=pl.ANY)
COPY_PARTS = 4


def _remote_parts(src_of, dst_of, rows, send_sem, recv_sem, to):
    parts = COPY_PARTS if rows % (16 * COPY_PARTS) == 0 else 1
    step = rows // parts
    for i in range(parts):
        pltpu.make_async_remote_copy(src_ref=src_of(i * step, step), dst_ref=dst_of(i * step, step),
                                     send_sem=send_sem, recv_sem=recv_sem, device_id=to, device_id_type=MESH).start()
    return pltpu.make_async_remote_copy(src_ref=src_of(0, rows), dst_ref=dst_of(0, rows), send_sem=send_sem,
                                        recv_sem=recv_sem, device_id=to, device_id_type=MESH)


GATHER_SEMS = 7


def _gather_steps(p_refs, o_refs, send_sems, recv_sems):
    x, y, c, chips = _place()
    me = 2 * x + y

    def half(ref, which):
        rows = ref.shape[0] // 2
        return ref.at[pl.ds(which * rows, rows)]

    def copy(k, src, dst, to):
        return pltpu.make_async_remote_copy(src_ref=src, dst_ref=dst, send_sem=send_sems.at[k],
                                            recv_sem=recv_sems.at[k], device_id=to, device_id_type=MESH)

    def first_copies():
        out = []
        for b, (p_ref, o_ref) in enumerate(zip(p_refs, o_refs)):
            for j, (cx, cy) in enumerate(chips):
                out.append(copy(GATHER_SEMS * b + j, half(p_ref, c), half(o_ref.at[me], c), (cx, cy, c)))
            out.append(copy(GATHER_SEMS * b + 6, p_ref, o_ref.at[me], (x, y, 1 - c)))
        return out

    def passed_copies():
        out = []
        for b, o_ref in enumerate(o_refs):
            for j, (cx, cy) in enumerate(chips):
                landed = half(o_ref.at[2 * cx + cy], c)
                out.append(copy(GATHER_SEMS * b + 3 + j, landed, landed, (x, y, 1 - c)))
        return out

    def start():
        for cp in first_copies():
            cp.start()

    def forward():
        for b, o_ref in enumerate(o_refs):
            for j, (cx, cy) in enumerate(chips):
                landed = half(o_ref.at[2 * cx + cy], c)
                copy(GATHER_SEMS * b + j, landed, landed, (x, y, c)).wait_recv()
        for cp in passed_copies():
            cp.start()

    def finish():
        for b, o_ref in enumerate(o_refs):
            for j, (cx, cy) in enumerate(chips):
                other = half(o_ref.at[2 * cx + cy], 1 - c)
                copy(GATHER_SEMS * b + 3 + j, other, other, (x, y, c)).wait_recv()
            copy(GATHER_SEMS * b + 6, o_ref.at[me], o_ref.at[me], (x, y, c)).wait_recv()
        for cp in first_copies() + passed_copies():
            cp.wait_send()

    return start, forward, finish


def _gather_chips(shards):
    nb = len(shards)

    def body(*refs):
        for step in _gather_steps(refs[:nb], refs[nb:2 * nb], *refs[2 * nb:]):
            step()

    return pl.pallas_call(
        body, name="gather_chips", in_specs=[ANY] * nb, out_specs=[ANY] * nb,
        out_shape=[jax.ShapeDtypeStruct((N_CHIPS,) + s.shape, s.dtype) for s in shards],
        scratch_shapes=[pltpu.SemaphoreType.DMA((GATHER_SEMS * nb,)), pltpu.SemaphoreType.DMA((GATHER_SEMS * nb,))],
    )(*shards)


def _swap_halves(bufs):
    nb = len(bufs)

    def body(*refs):
        g_refs, t_refs, (send_sems, recv_sems) = refs[:nb], refs[nb:2 * nb], refs[2 * nb:]
        x, y, c, _ = _place()
        gives = []
        for b, (g_ref, t_ref) in enumerate(zip(g_refs, t_refs)):
            for j in range(N_CHIPS):
                k = b * N_CHIPS + j
                gives.append(_remote_parts(lambda r0, m, g_ref=g_ref, j=j: g_ref.at[j, 1 - c, pl.ds(r0, m), :],
                                           lambda r0, m, t_ref=t_ref, j=j: t_ref.at[j, pl.ds(r0, m), :],
                                           g_ref.shape[2], send_sems.at[k], recv_sems.at[k], (x, y, 1 - c)))
        for give in gives:
            give.wait()

    return pl.pallas_call(
        body, name="swap_halves", in_specs=[ANY] * nb, out_specs=[ANY] * nb,
        out_shape=[jax.ShapeDtypeStruct((b.shape[0], b.shape[2], b.shape[3]), b.dtype) for b in bufs],
        scratch_shapes=[pltpu.SemaphoreType.DMA((nb * N_CHIPS,)), pltpu.SemaphoreType.DMA((nb * N_CHIPS,))],
    )(*bufs)


def _scatter_chips(bufs):
    nb = len(bufs)

    def body(*refs):
        t_refs, o_refs, (send_sems, recv_sems) = refs[:nb], refs[nb:2 * nb], refs[2 * nb:]
        x, y, c, chips = _place()
        me = 2 * x + y

        def slot(ref, chip):
            return lambda r0, n: ref.at[chip, pl.ds(r0, n), :]

        sends = []
        for b, (t_ref, o_ref) in enumerate(zip(t_refs, o_refs)):
            for j, (cx, cy) in enumerate(chips):
                k = 3 * b + j
                sends.append(_remote_parts(slot(t_ref, 2 * cx + cy), slot(o_ref, me), t_ref.shape[1],
                                           send_sems.at[k], recv_sems.at[k], (cx, cy, c)))
        for b, o_ref in enumerate(o_refs):
            for j, (cx, cy) in enumerate(chips):
                landed = o_ref.at[2 * cx + cy]
                pltpu.make_async_remote_copy(src_ref=landed, dst_ref=landed, send_sem=send_sems.at[3 * b + j],
                                             recv_sem=recv_sems.at[3 * b + j], device_id=(x, y, c),
                                             device_id_type=MESH).wait_recv()
        for cp in sends:
            cp.wait_send()

    return pl.pallas_call(
        body, name="scatter_chips", in_specs=[ANY] * nb, out_specs=[ANY] * nb,
        out_shape=[jax.ShapeDtypeStruct(b.shape, b.dtype) for b in bufs],
        scratch_shapes=[pltpu.SemaphoreType.DMA((3 * nb,)), pltpu.SemaphoreType.DMA((3 * nb,))],
    )(*bufs)


def _give_sibling(bufs):
    nb = len(bufs)

    def body(*refs):
        h_refs, o_refs, (send_sems, recv_sems) = refs[:nb], refs[nb:2 * nb], refs[2 * nb:]
        x, y, c, _ = _place()
        gives = [_remote_parts(lambda r0, n, h_ref=h_ref: h_ref.at[pl.ds(r0, n), :],
                               lambda r0, n, o_ref=o_ref: o_ref.at[pl.ds(r0, n), :],
                               h_ref.shape[0], send_sems.at[b], recv_sems.at[b], (x, y, 1 - c))
                 for b, (h_ref, o_ref) in enumerate(zip(h_refs, o_refs))]
        for give in gives:
            give.wait()

    return pl.pallas_call(
        body, name="give_sibling", in_specs=[ANY] * nb, out_specs=[ANY] * nb,
        out_shape=[jax.ShapeDtypeStruct(b.shape, b.dtype) for b in bufs],
        scratch_shapes=[pltpu.SemaphoreType.DMA((nb,)), pltpu.SemaphoreType.DMA((nb,))],
    )(*bufs)


def _pack(arrays, dtype, row_align):
    flat = []
    for arr in arrays:
        v = arr.reshape(-1)
        flat.append(jnp.pad(v, (0, (-v.shape[0]) % LANES)))
    total = sum(f.shape[0] for f in flat) // LANES
    pad_rows = (-total) % row_align
    if pad_rows:
        flat.append(jnp.zeros((pad_rows * LANES,), dtype))
    return jnp.concatenate(flat).reshape(-1, LANES)


def _unpack(buf, shapes, rows_align=1):
    lead = buf.shape[:-2]
    flat = buf.reshape(lead + (-1,))
    out, off = [], 0
    for shape in shapes:
        size = 1
        for dim in shape:
            size *= dim
        out.append(flat[..., off:off + size].reshape(lead + tuple(shape)))
        off += size + (-size) % (LANES * rows_align)
    return out


def _from_chips(parts, axis):
    return jnp.concatenate([parts[j] for j in range(N_CHIPS)], axis=axis)


def kernel(x, ln_g, ln_b, attn_w_in, attn_b_f, attn_w_out, rnn_w_in, rnn_conv_w, rnn_conv_b, rnn_w_a, rnn_b_a, rnn_w_i, rnn_b_i, rnn_lambda, rnn_w_out, loss_target, m_ln_g, m_ln_b, m_attn_w_in, m_attn_b_f, m_attn_w_out, m_rnn_w_in, m_rnn_conv_w, m_rnn_conv_b, m_rnn_w_a, m_rnn_b_a, m_rnn_w_i, m_rnn_b_i, m_rnn_lambda, m_rnn_w_out, v_ln_g, v_ln_b, v_attn_w_in, v_attn_b_f, v_attn_w_out, v_rnn_w_in, v_rnn_conv_w, v_rnn_conv_b, v_rnn_w_a, v_rnn_b_a, v_rnn_w_i, v_rnn_b_i, v_rnn_lambda, v_rnn_w_out):
    s_len, d = x.shape[1], x.shape[2]
    xs = x.reshape(s_len, d)
    target = loss_target.reshape(s_len, d)
    heads = attn_b_f.shape[1]
    qkvg = attn_w_in.shape[2] * N_CHIPS - heads

    me = 2 * lax.axis_index("x") + lax.axis_index("y")
    core = lax.axis_index("c").astype(jnp.int32)
    small = [rnn_conv_w, rnn_conv_b, rnn_b_a, rnn_b_i, rnn_lambda]
    a_in, a_out = attn_w_in.astype(BF16), attn_w_out.astype(BF16)
    later = [a_in[1], a_out[1]] + [w.astype(BF16) for w in (rnn_w_in, rnn_w_a, rnn_w_i, rnn_w_out)]
    got_in, got_out, got_small = _gather_chips([a_in[0], a_out[0], _pack(small, F32, 16)])
    conv_w, conv_b, b_a, b_i, lam = [
        _from_chips(p, ax) for p, ax in zip(_unpack(got_small, [w.shape for w in small]), (2, 1, 1, 1, 1))]

    def attn_weights(w_in, w_out):
        full = jnp.concatenate([w_in[j] for j in range(N_CHIPS)], axis=1)
        cat = jnp.concatenate([full[:, :d] * (HEAD_DIM ** -0.5), full[:, d:qkvg],
                               jnp.pad(full[:, qkvg:], ((0, 0), (0, 128 - heads)))], axis=1).astype(BF16)
        return cat, w_out.reshape(d, d)

    w_cat, w_out_a = [None, None], [None, None]
    w_cat[0], w_out_a[0] = attn_weights(got_in, got_out)
    b_f = jnp.pad(attn_b_f, ((0, 0), (0, 128 - heads)))

    saved = []
    cur = xs
    for layer in range(DEPTH):
        idx = layer // 2
        g_l, b_l = ln_g[layer:layer + 1], ln_b[layer:layer + 1]
        if layer % 2 == 0:
            q, k, v, gate, fl = _proj(cur, w_cat[idx], [(0, d, BF16), (d, d, BF16), (2 * d, d, BF16),
                                                         (3 * d, d, F32), (4 * d, 128, F32)], "attn_proj")
            cum, sneg = _fcum(fl, b_f[idx:idx + 1])
            o, lse, *rest = _flash_fwd(q, k, v, cum, later if layer == 0 else ())
            if layer == 0:
                w_cat[1], w_out_a[1] = attn_weights(rest[0], rest[1])
                w_in_r = jnp.moveaxis(rest[2], 0, 2).reshape(2, d, 2 * d)
                w_a = jnp.transpose(rest[3], (1, 2, 0, 3, 4)).reshape(2, -1, RNN_BLOCK, RNN_BLOCK)
                w_i = jnp.transpose(rest[4], (1, 2, 0, 3, 4)).reshape(2, -1, RNN_BLOCK, RNN_BLOCK)
                w_out_r = jnp.moveaxis(rest[5], 0, 1).reshape(2, d, d)
            nxt, xh, rs, yg = _out_ln(o, gate, cur, w_out_a[idx], g_l, b_l, "out_ln")
            saved.append(dict(x=cur, q=q, k=k, v=v, gate=gate, cum=cum, sneg=sneg, a=o, lse=lse, xh=xh, rs=rs, yg=yg))
        else:
            u, gate = _proj(cur, w_in_r[idx], [(0, d, F32), (d, d, F32)], "rnn_proj")
            vecs = [t[idx:idx + 1] for t in (conv_b, b_a, b_i, lam)]
            hseq, uc, r, ig, a = _rnn_fwd(u, conv_w[idx], vecs[0], w_a[idx], vecs[1], w_i[idx], vecs[2], vecs[3])
            nxt, xh, rs, yg = _out_ln(hseq, gate, cur, w_out_r[idx], g_l, b_l, "out_ln")
            saved.append(dict(x=cur, u=u, gate=gate, uc=uc, r=r, ig=ig, av=a, a=hseq, xh=xh, rs=rs, yg=yg, lam=vecs[3]))
        cur = nxt

    dout, sq = _loss_grad(cur, target)
    loss = lax.psum(0.5 * jnp.sum(sq) / d, ("x", "y", "c"))

    g_ln_g, g_ln_b = [None] * DEPTH, [None] * DEPTH
    g_attn = [None] * 2
    g_rnn = [None] * 2
    slots = None
    for layer in reversed(range(DEPTH)):
        idx = layer // 2
        sv = saved[layer]
        w_out = w_out_a[idx] if layer % 2 == 0 else w_out_r[idx]
        dz, da, dgate, dg, db = _ln_bwd(dout, sv["xh"], sv["rs"], ln_g[layer:layer + 1], w_out, sv["gate"], sv["a"],
                                        "ln_bwd")
        g_ln_g[layer], g_ln_b[layer] = dg, db
        out_rows = (ROWS_ATTN_OUT if layer % 2 == 0 else ROWS_RNN_OUT) + idx * (d // N_CHIPS)
        slots = _dw_out(slots, sv["yg"], dz, out_rows)
        if layer % 2 == 0:
            dq, dk, dv, dcum = _flash_bwd(sv["q"], sv["k"], sv["v"], sv["a"], da, sv["lse"], sv["cum"])
            dlogit, dbf = _fbwd(dcum, sv["sneg"])
            pieces = [dq, dk, dv, dgate, dlogit]
            dout = _mm_nt(dz, [(p, j * d) for j, p in enumerate(pieces)], w_cat[idx], "attn_dx")
            slots = _dw_attn_in(slots, sv["x"], pieces[:4], idx)
            g_attn[idx] = dict(w_f=_mm_tn(sv["x"], dlogit, "dw_f")[:, :heads], b_f=dbf[:, 0])
        else:
            duc, d_wa, d_wi, d_ba, d_bi, d_lam = _rnn_bwd(da, sv["av"], sv["a"], sv["r"], sv["ig"], sv["uc"], sv["lam"],
                                                          w_a[idx], w_i[idx])
            du, d_cw, d_cb = _conv_bwd(duc, sv["u"], conv_w[idx])
            dout = _mm_nt(dz, [(du, 0), (dgate, d)], w_in_r[idx], "rnn_dx")
            slots = _dw_rnn_in(slots, sv["x"], (du, dgate), idx)
            g_rnn[idx] = dict(conv_w=d_cw, conv_b=d_cb[0], w_a=d_wa, b_a=d_ba[0], w_i=d_wi, b_i=d_bi[0], lam=d_lam[0])
    grad_x = dout.reshape(x.shape)

    def by_chip(t, axis):
        shape = t.shape[:axis] + (N_CHIPS, t.shape[axis] // N_CHIPS) + t.shape[axis + 1:]
        flat = jnp.moveaxis(t.reshape(shape), axis, 0).reshape(N_CHIPS, -1)
        return jnp.pad(flat, ((0, 0), (0, (-flat.shape[1]) % (8 * LANES)))).reshape(N_CHIPS, -1, LANES)

    def everywhere(t):
        flat = t.reshape(-1)
        flat = jnp.pad(flat, (0, (-flat.shape[0]) % (8 * LANES))).reshape(-1, LANES)
        return jnp.broadcast_to(flat[None], (N_CHIPS,) + flat.shape)

    def both(key):
        return jnp.stack([it[key] for it in g_rnn])

    in_rows = slots[1:, ROWS_ATTN_IN:ROWS_ATTN_IN + 2 * d, :heads].reshape(3, 2, d, heads)
    edges = jnp.concatenate([in_rows, jnp.stack([it["w_f"] for it in g_attn])[None]])
    rows = [by_chip(both("w_a"), 2), by_chip(both("w_i"), 2), everywhere(edges), by_chip(both("conv_w"), 2),
            by_chip(both("conv_b"), 1), by_chip(both("b_a"), 1), by_chip(both("b_i"), 1), by_chip(both("lam"), 1),
            everywhere(jnp.concatenate(g_ln_g)), everywhere(jnp.concatenate(g_ln_b)),
            everywhere(jnp.stack([it["b_f"] for it in g_attn]))]
    used = sum(r.shape[1] for r in rows)
    small = jnp.concatenate(rows + [jnp.zeros((N_CHIPS, (-used) % 32, LANES), F32)], axis=1)

    def halves(buf):
        return buf.reshape(N_CHIPS, 2, buf.shape[1] // 2, LANES)

    large, small = halves(slots), halves(small)

    core1 = core.reshape(1)
    theirs = _swap_halves([large, small])
    pair = [_pair_sum(core1, large, theirs[0], BF16), _pair_sum(core1, small, theirs[1], F32)]
    summed = []
    for mine_all, got in zip(pair, _scatter_chips(pair)):
        own = lax.dynamic_index_in_dim(mine_all, me, 0, keepdims=False)
        summed.append(_sum_chips(lax.dynamic_update_index_in_dim(got, own, me, 0)))
    reduced = [jnp.concatenate([jnp.where(core == 0, mine, other), jnp.where(core == 0, other, mine)])
               for mine, other in zip(summed, _give_sibling(summed))]
    g_in_group = reduced[0][ROWS_ATTN_IN:ROWS_ATTN_IN + 2 * d].reshape(2, d, d)
    g_w_out_a = reduced[0][ROWS_ATTN_OUT:ROWS_ATTN_OUT + d // 2].reshape(attn_w_out.shape)
    folded = reduced[0][ROWS_RNN_IN:ROWS_RNN_IN + d].reshape(2, d // 2, d)
    g_w_in_r = jnp.concatenate([folded[:, :, :d // 2], folded[:, :, d // 2:]], axis=1)
    g_w_out_r = reduced[0][ROWS_RNN_OUT:ROWS_RNN_OUT + d // 2].reshape(rnn_w_out.shape)
    (g_w_a, g_w_i, g_edges, g_conv_w, g_conv_b, g_b_a, g_b_i, g_lam, g_ln_g_sum, g_ln_b_sum,
     g_b_f) = _unpack(reduced[1], [
        rnn_w_a.shape, rnn_w_i.shape, (N_CHIPS, 2, d, heads), rnn_conv_w.shape, rnn_conv_b.shape, rnn_b_a.shape,
        rnn_b_i.shape, rnn_lambda.shape, ln_g.shape, ln_b.shape, attn_b_f.shape], rows_align=8)
    wide = jnp.concatenate([g_in_group, lax.dynamic_index_in_dim(g_edges, me, 0, keepdims=False)], axis=2)
    g_w_in_a = lax.dynamic_slice(wide, (0, 0, (heads // N_CHIPS) * me), attn_w_in.shape)

    def adam_nd(w, g, m, v, view, rows_per_block):
        outs = _adamw(w.reshape(view), g.reshape(view), m.reshape(view), v.reshape(view), 1, rows_per_block)
        return [t.reshape(w.shape) for t in outs]

    turned = [jnp.transpose(t, (2, 0, 1)) for t in (attn_w_in, g_w_in_a, m_attn_w_in, v_attn_w_in)]
    upd = {
        "attn_w_in": [jnp.transpose(t, (1, 2, 0)) for t in _adamw(*turned, attn_w_in.shape[2] // N_CHIPS, 2)],
        "attn_w_out": adam_nd(attn_w_out, g_w_out_a, m_attn_w_out, v_attn_w_out, attn_w_out.shape, 256),
        "rnn_w_in": adam_nd(rnn_w_in, g_w_in_r, m_rnn_w_in, v_rnn_w_in, rnn_w_in.shape, 512),
        "rnn_w_a": adam_nd(rnn_w_a, g_w_a, m_rnn_w_a, v_rnn_w_a, (1, -1, RNN_BLOCK), 512),
        "rnn_w_i": adam_nd(rnn_w_i, g_w_i, m_rnn_w_i, v_rnn_w_i, (1, -1, RNN_BLOCK), 512),
        "rnn_w_out": adam_nd(rnn_w_out, g_w_out_r, m_rnn_w_out, v_rnn_w_out, rnn_w_out.shape, 256),
    }
    smalls = [rnn_conv_w, rnn_conv_b, rnn_b_a, rnn_b_i, rnn_lambda, ln_g, ln_b, attn_b_f]
    g_small = [g_conv_w, g_conv_b, g_b_a, g_b_i, g_lam, g_ln_g_sum, g_ln_b_sum, g_b_f]
    m_small = [m_rnn_conv_w, m_rnn_conv_b, m_rnn_b_a, m_rnn_b_i, m_rnn_lambda, m_ln_g, m_ln_b, m_attn_b_f]
    v_small = [v_rnn_conv_w, v_rnn_conv_b, v_rnn_b_a, v_rnn_b_i, v_rnn_lambda, v_ln_g, v_ln_b, v_attn_b_f]
    packs = [_pack(t, F32, 16)[None] for t in (smalls, g_small, m_small, v_small)]
    small_out = [_unpack(t[0], [w.shape for w in smalls]) for t in _adamw(*packs, 1, 16)]
    for k, name in enumerate(("rnn_conv_w", "rnn_conv_b", "rnn_b_a", "rnn_b_i", "rnn_lambda", "ln_g", "ln_b",
                              "attn_b_f")):
        upd[name] = [small_out[0][k], small_out[1][k], small_out[2][k]]
    grad = {"attn_w_in": g_w_in_a, "attn_w_out": g_w_out_a, "rnn_w_in": g_w_in_r, "rnn_w_a": g_w_a, "rnn_w_i": g_w_i,
            "rnn_w_out": g_w_out_r, "rnn_conv_w": g_conv_w, "rnn_conv_b": g_conv_b, "rnn_b_a": g_b_a,
            "rnn_b_i": g_b_i, "rnn_lambda": g_lam, "ln_g": g_ln_g_sum, "ln_b": g_ln_b_sum, "attn_b_f": g_b_f}
    names = ("ln_g", "ln_b", "attn_w_in", "attn_b_f", "attn_w_out", "rnn_w_in", "rnn_conv_w", "rnn_conv_b",
             "rnn_w_a", "rnn_b_a", "rnn_w_i", "rnn_b_i", "rnn_lambda", "rnn_w_out")
    return (loss, grad_x, *[grad[n] for n in names], *[upd[n][0] for n in names], *[upd[n][1] for n in names],
            *[upd[n][2] for n in names])
```

```python
import functools

import jax
import jax.numpy as jnp
from jax import lax
from jax.experimental import pallas as pl
from jax.experimental.pallas import tpu as pltpu

F32 = jnp.float32
BF16 = jnp.bfloat16
MESH = pl.DeviceIdType.MESH

DEPTH = 4
HEAD_DIM = 64
HEAD_PAIR = 2 * HEAD_DIM
RNN_BLOCK = 256
CONV_WIDTH = 4
LRU_C = 8.0
ALPHA = (2.0 * DEPTH) ** 0.25
LN_EPS = 1e-5
ADAM_LR, ADAM_B1, ADAM_B2, ADAM_EPS, ADAM_WD, ADAM_STEP = 0.001, 0.9, 0.999, 1e-08, 0.01, 10
N_CHIPS = 4
LANES = 1024
NEG = -1e30

NT_DIMS = (((1,), (1,)), ((), ()))
TN_DIMS = (((0,), (0,)), ((), ()))


def _sigmoid(z):
    return 1.0 / (1.0 + jnp.exp(-z))


def _softplus(z):
    return jnp.maximum(z, 0.0) + jnp.log(1.0 + jnp.exp(-jnp.abs(z)))


def _neg_expm1(y):
    poly = y * (1.0 + y * (0.5 + y * (1.0 / 6.0 + y * (1.0 / 24.0))))
    return -jnp.where(y > -0.05, poly, jnp.exp(y) - 1.0)


def _shift_down(cur, prev8, k):
    n = cur.shape[0]
    row = lax.broadcasted_iota(jnp.int32, cur.shape, 0)
    fix = jnp.tile(pltpu.roll(prev8, k, 0), (n // 8, 1))
    return jnp.where(row < k, fix, pltpu.roll(cur, k, 0))


def _shift_up(cur, next8, k):
    n = cur.shape[0]
    row = lax.broadcasted_iota(jnp.int32, cur.shape, 0)
    fix = jnp.tile(pltpu.roll(next8, 8 - k, 0), (n // 8, 1))
    return jnp.where(row >= n - k, fix, pltpu.roll(cur, n - k, 0))


def _proj(x, w, outs, name):
    s_len, d = x.shape
    tm = min(512, s_len)

    def body(x_ref, w_ref, *o_refs):
        xb = x_ref[...].astype(BF16)
        for (c0, wd, dt), o_ref in zip(outs, o_refs):
            step = min(wd, 512)
            for cc in range(0, wd, step):
                o_ref[:, cc:cc + step] = jnp.dot(
                    xb, w_ref[:, c0 + cc:c0 + cc + step], preferred_element_type=F32).astype(dt)

    return pl.pallas_call(
        body, name=name, grid=(s_len // tm,),
        in_specs=[pl.BlockSpec((tm, d), lambda i: (i, 0)), pl.BlockSpec(w.shape, lambda i: (0, 0))],
        out_specs=[pl.BlockSpec((tm, wd), lambda i: (i, 0)) for _, wd, _ in outs],
        out_shape=[jax.ShapeDtypeStruct((s_len, wd), dt) for _, wd, dt in outs],
        compiler_params=pltpu.CompilerParams(dimension_semantics=("parallel",)),
    )(x, w)


def _mm_nt(dz, pieces, w, name):
    s_len, d = dz.shape
    tm = min(256, s_len)
    n = len(pieces)
    cols = [(c0, p.shape[1]) for p, c0 in pieces]

    def body(dz_ref, *rest):
        w_ref, o_ref = rest[n], rest[n + 1]
        acc = ALPHA * dz_ref[...]
        for (c0, wd), p_ref in zip(cols, rest[:n]):
            acc = acc + lax.dot_general(p_ref[...], w_ref[:, c0:c0 + wd], NT_DIMS, preferred_element_type=F32)
        o_ref[...] = acc

    return pl.pallas_call(
        body, name=name, grid=(s_len // tm,),
        in_specs=[pl.BlockSpec((tm, d), lambda i: (i, 0))]
        + [pl.BlockSpec((tm, wd), lambda i: (i, 0)) for _, wd in cols]
        + [pl.BlockSpec(w.shape, lambda i: (0, 0))],
        out_specs=pl.BlockSpec((tm, d), lambda i: (i, 0)),
        out_shape=jax.ShapeDtypeStruct((s_len, d), F32),
        compiler_params=pltpu.CompilerParams(dimension_semantics=("parallel",)),
    )(dz, *[p for p, _ in pieces], w)


def _mm_tn(a, b, name):
    s_len, m = a.shape
    n = b.shape[1]
    tn = min(n, 512)
    ts = min(s_len, 512)

    def body(a_ref, b_ref, o_ref):
        @pl.when(pl.program_id(1) == 0)
        def _():
            o_ref[...] = jnp.zeros_like(o_ref)

        o_ref[...] += lax.dot_general(a_ref[...].astype(BF16), b_ref[...].astype(BF16), TN_DIMS,
                                      preferred_element_type=F32)

    return pl.pallas_call(
        body, name=name, grid=(n // tn, s_len // ts),
        in_specs=[pl.BlockSpec((ts, m), lambda j, s: (s, 0)), pl.BlockSpec((ts, tn), lambda j, s: (s, j))],
        out_specs=pl.BlockSpec((m, tn), lambda j, s: (0, j)),
        out_shape=jax.ShapeDtypeStruct((m, n), F32),
        compiler_params=pltpu.CompilerParams(dimension_semantics=("parallel", "arbitrary")),
    )(a, b)


ROWS_ATTN_IN = (0, 2048)
ROWS_ATTN_OUT = (1024, 1280)
ROWS_RNN_OUT = (1536, 1792)
ROWS_RNN_IN = (3072, 3584)
LATE_ROWS = 1280
SLOT_ROWS = 4096


DW_ROWS = 1024


def _dw_call(body, name, slots, operands, specs, out_block, out_index, grid, semantics):
    return pl.pallas_call(
        body, name=name, grid=grid,
        in_specs=specs if slots is None else [ANY] + specs,
        out_specs=pl.BlockSpec(out_block, out_index),
        out_shape=jax.ShapeDtypeStruct((N_CHIPS, SLOT_ROWS, LANES), F32),
        input_output_aliases={} if slots is None else {0: 0},
        compiler_params=pltpu.CompilerParams(dimension_semantics=semantics),
    )(*(operands if slots is None else [slots] + operands))


def _dw_attn_in(slots, x, pieces, layer):
    s_len, d = x.shape
    ts = min(s_len, DW_ROWS)
    steps = s_len // ts
    half = d // 2

    def body(*refs):
        x_ref, p_refs, o_ref = refs[-6], refs[-5:-1], refs[-1]

        @pl.when(pl.program_id(1) == 0)
        def _():
            o_ref[...] = jnp.zeros_like(o_ref)

        xb = x_ref[...].astype(BF16)
        for j, p_ref in enumerate(p_refs):
            o_ref[j] += lax.dot_general(xb, p_ref[...], TN_DIMS, preferred_element_type=F32)

        @pl.when(pl.program_id(1) == steps - 1)
        def _():
            o_ref[0] = o_ref[0] * (HEAD_DIM ** -0.5)

    specs = [pl.BlockSpec((ts, d), lambda i, s: (s, 0))] + [pl.BlockSpec((ts, half), lambda i, s: (s, i))] * 4
    return _dw_call(body, "dw_attn_in", slots, [x] + list(pieces), specs, (N_CHIPS, d, half),
                    lambda i, s: (0, ROWS_ATTN_IN[layer] // d, i), (2, steps), ("parallel", "arbitrary"))


def _dw_out(slots, yg, dz, first_row):
    s_len, d = dz.shape
    ts = min(s_len, DW_ROWS)
    rows = d // N_CHIPS

    def body(*refs):
        a_ref, b_ref, o_ref = refs[-3:]

        @pl.when(pl.program_id(0) == 0)
        def _():
            o_ref[...] = jnp.zeros_like(o_ref)

        prod = lax.dot_general(a_ref[...], b_ref[...].astype(BF16), TN_DIMS, preferred_element_type=F32)
        o_ref[...] += prod.reshape(N_CHIPS, rows, d)

    specs = [pl.BlockSpec((ts, d), lambda s: (s, 0))] * 2
    return _dw_call(body, "dw_out", slots, [yg, dz], specs, (N_CHIPS, rows, d), lambda s: (0, first_row // rows, 0),
                    (s_len // ts,), ("arbitrary",))


def _dw_rnn_in(slots, x, parts, layer):
    s_len, d = x.shape
    ts = min(s_len, DW_ROWS)
    half = d // 2

    def body(*refs):
        x_ref, p_refs, o_ref = refs[-4], refs[-3:-1], refs[-1]

        @pl.when(pl.program_id(0) == 0)
        def _():
            o_ref[...] = jnp.zeros_like(o_ref)

        xb = x_ref[...].astype(BF16)
        for p, p_ref in enumerate(p_refs):
            for jj in (0, 1):
                for r in (0, 1):
                    o_ref[2 * p + jj, :, r * half:(r + 1) * half] += lax.dot_general(
                        xb[:, r * half:(r + 1) * half], p_ref[:, jj * half:(jj + 1) * half], TN_DIMS,
                        preferred_element_type=F32)

    specs = [pl.BlockSpec((ts, d), lambda s: (s, 0))] * 3
    return _dw_call(body, "dw_rnn_in", slots, [x] + list(parts), specs, (N_CHIPS, half, d),
                    lambda s: (0, ROWS_RNN_IN[layer] // half, 0), (s_len // ts,), ("arbitrary",))


def _fcum(fl, bias):
    s_len = fl.shape[0]

    def body(fl_ref, b_ref, cum_ref, sg_ref):
        z = fl_ref[...] + b_ref[...]
        e = jnp.exp(-jnp.abs(z))
        logf = jnp.minimum(z, 0.0) - jnp.log(1.0 + e)
        sneg = jnp.where(z >= 0, e, 1.0) / (1.0 + e)
        run = logf.T[0:16, :]
        sg_ref[...] = sneg.T[0:16, :]
        lane = lax.broadcasted_iota(jnp.int32, (16, s_len), 1)
        sh = 1
        while sh < s_len:
            run = run + jnp.where(lane >= sh, pltpu.roll(run, sh, 1), 0.0)
            sh *= 2
        cum_ref[...] = run

    return pl.pallas_call(
        body, name="fcum",
        out_shape=[jax.ShapeDtypeStruct((16, s_len), F32), jax.ShapeDtypeStruct((16, s_len), F32)],
    )(fl, bias)


def _fbwd(dcum, sneg):
    s_len = dcum.shape[1]

    def body(dc_ref, sg_ref, dl_ref, db_ref):
        run = dc_ref[...]
        lane = lax.broadcasted_iota(jnp.int32, (16, s_len), 1)
        sh = 1
        while sh < s_len:
            run = run + jnp.where(lane < s_len - sh, pltpu.roll(run, s_len - sh, 1), 0.0)
            sh *= 2
        dlog = run * sg_ref[...]
        db_ref[...] = jnp.broadcast_to(jnp.sum(dlog, axis=1, keepdims=True), (16, 128))
        full = jnp.concatenate([dlog, jnp.zeros((112, s_len), F32)], axis=0)
        dl_ref[...] = full.T.astype(BF16)

    return pl.pallas_call(
        body, name="fbwd",
        out_shape=[jax.ShapeDtypeStruct((s_len, 128), BF16), jax.ShapeDtypeStruct((16, 128), F32)],
    )(dcum, sneg)


FWD_TILE = 1024
BWD_TILE = 512


def _flash_fwd(q, k, v, cum, shards=()):
    s_len, width = q.shape
    tile = min(FWD_TILE, s_len // 2)
    nt = s_len // tile
    reps = tile // 128
    cumr = cum.reshape(-1, tile)
    ns = len(shards)
    pairs = width // HEAD_PAIR

    def body(*refs):
        q_ref, k_ref, v_ref, cum_ref = refs[:4]
        o_ref, lse_ref = refs[4 + ns:6 + ns]
        q_t, v_t, cum_col = refs[6 + 2 * ns:9 + 2 * ns]
        pid = pl.program_id(0)
        if ns:
            start, forward, finish = _gather_steps(refs[4:4 + ns], refs[6 + ns:6 + 2 * ns], *refs[9 + 2 * ns:])
            pl.when(pid == 0)(start)
            pl.when(pid == pairs // 2)(forward)
        top = lax.broadcasted_iota(jnp.int32, (HEAD_PAIR, tile), 0) < HEAD_DIM
        causal = (lax.broadcasted_iota(jnp.int32, (tile, tile), 0)
                  <= lax.broadcasted_iota(jnp.int32, (tile, tile), 1))

        def pre(i, carry):
            r0 = pl.multiple_of(i * tile, tile)
            q_t[i] = q_ref[pl.ds(r0, tile), :].astype(F32).T.astype(BF16)
            v_t[i] = v_ref[pl.ds(r0, tile), :].astype(F32).T.astype(BF16)
            for h in (0, 1):
                row = cum_ref[pl.ds((2 * pid + h) * nt + i, 1), :]
                cum_col[h, pl.ds(r0, tile), :] = jnp.broadcast_to(row, (HEAD_PAIR, tile)).T
            return carry

        lax.fori_loop(0, nt, pre, 0)

        def q_loop(qi, carry):
            qt = q_t[qi]
            zt = jnp.zeros_like(qt)
            qms = (jnp.where(top, qt, zt), jnp.where(top, zt, qt))

            def step(kj, state, masked):
                m0, l0, m1, l1, acc = state
                k0 = pl.multiple_of(kj * tile, tile)
                kt = k_ref[pl.ds(k0, tile), :]
                vt = v_t[kj]
                ms, ls, scales, contrib = [m0, m1], [l0, l1], [], None
                for h in (0, 1):
                    s = jnp.dot(kt, qms[h], preferred_element_type=F32)
                    s = s - jnp.tile(cum_col[h, pl.ds(k0, tile), :], (1, reps))
                    if masked:
                        s = jnp.where(causal, s, NEG)
                    m_new = jnp.maximum(ms[h], jnp.max(s, axis=0, keepdims=True))
                    p = jnp.exp(s - m_new)
                    scale = jnp.exp(ms[h] - m_new)
                    ls[h] = scale * ls[h] + jnp.sum(p, axis=0, keepdims=True)
                    ms[h] = m_new
                    scales.append(scale)
                    vm = jnp.where(top, vt, zt) if h == 0 else jnp.where(top, zt, vt)
                    part = jnp.dot(vm, p.astype(BF16), preferred_element_type=F32)
                    contrib = part if contrib is None else contrib + part
                acc = jnp.where(top, scales[0], scales[1]) * acc + contrib
                return ms[0], ls[0], ms[1], ls[1], acc

            low = jnp.full((1, tile), NEG, F32)
            zero = jnp.zeros((1, tile), F32)
            state = step(qi, (low, zero, low, zero, jnp.zeros((HEAD_PAIR, tile), F32)), True)
            m0, l0, m1, l1, acc = lax.fori_loop(0, qi, lambda kj, c: step(kj, c, False), state)
            q0 = pl.multiple_of(qi * tile, tile)
            o_ref[pl.ds(q0, tile), :] = (acc / jnp.where(top, l0, l1)).T
            lse_ref[pl.ds((2 * pid) * nt + qi, 1), :] = m0 + jnp.log(l0)
            lse_ref[pl.ds((2 * pid + 1) * nt + qi, 1), :] = m1 + jnp.log(l1)
            return carry

        lax.fori_loop(0, nt, q_loop, 0)
        if ns:
            pl.when(pid == pairs - 1)(finish)

    blk = pl.BlockSpec((s_len, HEAD_PAIR), lambda p: (0, p))
    full = pl.BlockSpec(cumr.shape, lambda p: (0, 0))
    sems = [pltpu.SemaphoreType.DMA((GATHER_SEMS * ns,))] * 2 if ns else []
    o, lse, *gathered = pl.pallas_call(
        body, name="flash_fwd_gather" if ns else "flash_fwd", grid=(pairs,),
        in_specs=[blk, blk, blk, full] + [ANY] * ns,
        out_specs=[blk, full] + [ANY] * ns,
        out_shape=[jax.ShapeDtypeStruct((s_len, width), F32), jax.ShapeDtypeStruct(cumr.shape, F32)]
        + [jax.ShapeDtypeStruct((N_CHIPS,) + s.shape, s.dtype) for s in shards],
        scratch_shapes=[pltpu.VMEM((nt, HEAD_PAIR, tile), BF16), pltpu.VMEM((nt, HEAD_PAIR, tile), BF16),
                        pltpu.VMEM((2, s_len, HEAD_PAIR), F32)] + sems,
        compiler_params=pltpu.CompilerParams(dimension_semantics=("arbitrary",)),
    )(q, k, v, cumr, *shards)
    return (o, lse.reshape(cum.shape), *gathered)


def _flash_bwd(q, k, v, o, do, lse, cum, outgoing=()):
    s_len, width = q.shape
    tile = min(BWD_TILE, s_len // 2)
    nt = s_len // tile
    reps = tile // 128
    cumr = cum.reshape(-1, tile)
    lse = lse.reshape(-1, tile)
    ns = len(outgoing)
    pairs = width // HEAD_PAIR

    def body(*refs):
        q_ref, k_ref, v_ref, o_ref, do_ref, lse_ref, cum_ref = refs[:7]
        dq_ref, dk_ref, dv_ref, dcum_ref = refs[7 + ns:11 + ns]
        (q_t, do_t, k_t, do_bf, cum_col, dq_t_acc, delta, q_side, dk_acc, dv_acc,
         k_side) = refs[11 + 2 * ns:22 + 2 * ns]
        pid = pl.program_id(0)
        if ns:
            start, finish = _scatter_steps(refs[7:7 + ns], refs[11 + ns:11 + 2 * ns], *refs[22 + 2 * ns:])
            pl.when(pid == 0)(start)
        top = lax.broadcasted_iota(jnp.int32, (HEAD_PAIR, tile), 0) < HEAD_DIM
        left = lax.broadcasted_iota(jnp.int32, (tile, HEAD_PAIR), 1) < HEAD_DIM
        causal = (lax.broadcasted_iota(jnp.int32, (tile, tile), 0)
                  <= lax.broadcasted_iota(jnp.int32, (tile, tile), 1))

        def pre(i, carry):
            r0 = pl.multiple_of(i * tile, tile)
            d_o = do_ref[pl.ds(r0, tile), :]
            prod_t = (d_o * o_ref[pl.ds(r0, tile), :]).T
            delta[pl.ds(i, 1), :] = jnp.sum(prod_t[:HEAD_DIM], axis=0, keepdims=True)
            delta[pl.ds(nt + i, 1), :] = jnp.sum(prod_t[HEAD_DIM:], axis=0, keepdims=True)
            do_bf[pl.ds(r0, tile), :] = d_o.astype(BF16)
            do_t[i] = d_o.T.astype(BF16)
            q_t[i] = q_ref[pl.ds(r0, tile), :].astype(F32).T.astype(BF16)
            k_t[i] = k_ref[pl.ds(r0, tile), :].astype(F32).T.astype(BF16)
            dq_t_acc[i] = jnp.zeros((HEAD_PAIR, tile), F32)
            for h in (0, 1):
                row = cum_ref[pl.ds((2 * pid + h) * nt + i, 1), :]
                cum_col[h, pl.ds(r0, tile), :] = jnp.broadcast_to(row, (HEAD_PAIR, tile)).T
                q_side[pl.ds(h * nt + i, 1), :] = jnp.zeros((1, tile), F32)
            return carry

        lax.fori_loop(0, nt, pre, 0)

        def kv_loop(kj, carry):
            k0 = pl.multiple_of(kj * tile, tile)
            kt = k_ref[pl.ds(k0, tile), :]
            vt = v_ref[pl.ds(k0, tile), :]
            ktt = k_t[kj]
            zt = jnp.zeros_like(ktt)
            zr = jnp.zeros_like(kt)
            kms = (jnp.where(top, ktt, zt), jnp.where(top, zt, ktt))
            cols = [jnp.tile(cum_col[h, pl.ds(k0, tile), :], (1, reps)) for h in (0, 1)]
            dk_acc[...] = jnp.zeros_like(dk_acc)
            dv_acc[...] = jnp.zeros_like(dv_acc)
            k_side[...] = jnp.zeros_like(k_side)

            def step(qi, carry, masked):
                q0 = pl.multiple_of(qi * tile, tile)
                qtt = q_t[qi]
                dtt = do_t[qi]
                q_rows = q_ref[pl.ds(q0, tile), :]
                do_rows = do_bf[pl.ds(q0, tile), :]
                dq_part = None
                for h in (0, 1):
                    q_m = jnp.where(top, qtt, zt) if h == 0 else jnp.where(top, zt, qtt)
                    do_m = jnp.where(top, dtt, zt) if h == 0 else jnp.where(top, zt, dtt)
                    s = jnp.dot(kt, q_m, preferred_element_type=F32) - cols[h]
                    if masked:
                        s = jnp.where(causal, s, NEG)
                    p = jnp.exp(s - lse_ref[pl.ds((2 * pid + h) * nt + qi, 1), :])
                    dp = jnp.dot(vt, do_m, preferred_element_type=F32)
                    ds = p * (dp - delta[pl.ds(h * nt + qi, 1), :])
                    q_side[pl.ds(h * nt + qi, 1), :] += jnp.sum(ds, axis=0, keepdims=True)
                    folded = ds[:, :128]
                    for b in range(1, reps):
                        folded = folded + ds[:, b * 128:(b + 1) * 128]
                    k_side[h] += folded
                    pb = p.astype(BF16)
                    dsb = ds.astype(BF16)
                    do_h = jnp.where(left, do_rows, zr) if h == 0 else jnp.where(left, zr, do_rows)
                    q_h = jnp.where(left, q_rows, zr) if h == 0 else jnp.where(left, zr, q_rows)
                    dv_acc[...] += jnp.dot(pb, do_h, preferred_element_type=F32)
                    dk_acc[...] += jnp.dot(dsb, q_h, preferred_element_type=F32)
                    part = jnp.dot(kms[h], dsb, preferred_element_type=F32)
                    dq_part = part if dq_part is None else dq_part + part
                dq_t_acc[qi] += dq_part
                return carry

            step(kj, 0, True)
            lax.fori_loop(kj + 1, nt, lambda qi, c: step(qi, c, False), 0)
            dk_ref[pl.ds(k0, tile), :] = dk_acc[...].astype(BF16)
            dv_ref[pl.ds(k0, tile), :] = dv_acc[...].astype(BF16)
            for h in (0, 1):
                dcum_ref[pl.ds((2 * pid + h) * nt + kj, 1), :] = -jnp.sum(k_side[h].T, axis=0, keepdims=True)
            return carry

        lax.fori_loop(0, nt, kv_loop, 0)

        def fin(i, carry):
            r0 = pl.multiple_of(i * tile, tile)
            dq_ref[pl.ds(r0, tile), :] = dq_t_acc[i].T.astype(BF16)
            for h in (0, 1):
                dcum_ref[pl.ds((2 * pid + h) * nt + i, 1), :] += q_side[pl.ds(h * nt + i, 1), :]
            return carry

        lax.fori_loop(0, nt, fin, 0)
        if ns:
            pl.when(pid == pairs - 1)(finish)

    blk = pl.BlockSpec((s_len, HEAD_PAIR), lambda p: (0, p))
    full = pl.BlockSpec(cumr.shape, lambda p: (0, 0))
    transposed = pltpu.VMEM((nt, HEAD_PAIR, tile), BF16)
    sems = [pltpu.SemaphoreType.DMA((3 * ns,))] * 2 if ns else []
    dq, dk, dv, dcum, *arrived = pl.pallas_call(
        body, name="flash_bwd_scatter" if ns else "flash_bwd", grid=(pairs,),
        in_specs=[blk, blk, blk, blk, blk, full, full] + [ANY] * ns,
        out_specs=[blk, blk, blk, full] + [ANY] * ns,
        out_shape=[jax.ShapeDtypeStruct((s_len, width), BF16)] * 3 + [jax.ShapeDtypeStruct(cumr.shape, F32)]
        + [jax.ShapeDtypeStruct(t.shape, t.dtype) for t in outgoing],
        scratch_shapes=[transposed, transposed, transposed, pltpu.VMEM((s_len, HEAD_PAIR), BF16),
                        pltpu.VMEM((2, s_len, HEAD_PAIR), F32), pltpu.VMEM((nt, HEAD_PAIR, tile), F32),
                        pltpu.VMEM((2 * nt, tile), F32), pltpu.VMEM((2 * nt, tile), F32),
                        pltpu.VMEM((tile, HEAD_PAIR), F32), pltpu.VMEM((tile, HEAD_PAIR), F32),
                        pltpu.VMEM((2, tile, HEAD_PAIR), F32)] + sems,
        compiler_params=pltpu.CompilerParams(dimension_semantics=("arbitrary",)),
    )(q, k, v, o, do, lse, cumr, *outgoing)
    return (dq, dk, dv, dcum.reshape(cum.shape), *arrived)


def _out_ln(a, gate, x, w, g, b, name):
    s_len, d = x.shape
    tm = min(256, s_len)

    def body(a_ref, gate_ref, x_ref, w_ref, g_ref, b_ref, xn_ref, xh_ref, rs_ref, yg_ref):
        gt = gate_ref[...]
        yg = (a_ref[...] * (gt * _sigmoid(gt))).astype(BF16)
        yg_ref[...] = yg
        z = ALPHA * x_ref[...] + jnp.dot(yg, w_ref[...], preferred_element_type=F32)
        zc = z - jnp.mean(z, axis=1, keepdims=True)
        rstd = lax.rsqrt(jnp.mean(zc * zc, axis=1, keepdims=True) + LN_EPS)
        xh = zc * rstd
        xh_ref[...] = xh
        xn_ref[...] = xh * g_ref[...] + b_ref[...]
        rs_ref[...] = jnp.broadcast_to(rstd, (tm, 128))

    row = pl.BlockSpec((tm, d), lambda i: (i, 0))
    vec = pl.BlockSpec((1, d), lambda i: (0, 0))
    return pl.pallas_call(
        body, name=name, grid=(s_len // tm,),
        in_specs=[row, row, row, pl.BlockSpec(w.shape, lambda i: (0, 0)), vec, vec],
        out_specs=[row, row, pl.BlockSpec((tm, 128), lambda i: (i, 0)), row],
        out_shape=[jax.ShapeDtypeStruct((s_len, d), F32), jax.ShapeDtypeStruct((s_len, d), F32),
                   jax.ShapeDtypeStruct((s_len, 128), F32), jax.ShapeDtypeStruct((s_len, d), BF16)],
        compiler_params=pltpu.CompilerParams(dimension_semantics=("parallel",)),
    )(a, gate, x, w, g, b)


def _ln_bwd(dout, xh, rs, g, w, gate, a, name):
    s_len, d = dout.shape
    tm = min(256, s_len)

    def body(do_ref, xh_ref, rs_ref, g_ref, w_ref, gate_ref, a_ref, dz_ref, da_ref, dgate_ref, dg_ref, db_ref):
        @pl.when(pl.program_id(0) == 0)
        def _():
            dg_ref[...] = jnp.zeros_like(dg_ref)
            db_ref[...] = jnp.zeros_like(db_ref)

        dout_t = do_ref[...]
        xh_t = xh_ref[...]
        dg_ref[...] += jnp.sum(dout_t * xh_t, axis=0, keepdims=True)
        db_ref[...] += jnp.sum(dout_t, axis=0, keepdims=True)
        dxh = dout_t * g_ref[...]
        m1 = jnp.mean(dxh, axis=1, keepdims=True)
        m2 = jnp.mean(dxh * xh_t, axis=1, keepdims=True)
        dz = rs_ref[:, 0:1] * (dxh - m1 - xh_t * m2)
        dz_ref[...] = dz
        dyg = lax.dot_general(dz.astype(BF16), w_ref[...], NT_DIMS, preferred_element_type=F32)
        gt = gate_ref[...]
        sg = _sigmoid(gt)
        da_ref[...] = dyg * (gt * sg)
        dgate_ref[...] = (dyg * a_ref[...] * (sg * (1.0 + gt * (1.0 - sg)))).astype(BF16)

    row = pl.BlockSpec((tm, d), lambda i: (i, 0))
    vec = pl.BlockSpec((1, d), lambda i: (0, 0))
    return pl.pallas_call(
        body, name=name, grid=(s_len // tm,),
        in_specs=[row, row, pl.BlockSpec((tm, 128), lambda i: (i, 0)), vec, pl.BlockSpec(w.shape, lambda i: (0, 0)),
                  row, row],
        out_specs=[row, row, row, vec, vec],
        out_shape=[jax.ShapeDtypeStruct((s_len, d), F32), jax.ShapeDtypeStruct((s_len, d), F32),
                   jax.ShapeDtypeStruct((s_len, d), BF16), jax.ShapeDtypeStruct((1, d), F32),
                   jax.ShapeDtypeStruct((1, d), F32)],
        compiler_params=pltpu.CompilerParams(dimension_semantics=("arbitrary",)),
    )(dout, xh, rs, g, w, gate, a)


def _loss_grad(y, target):
    s_len, d = y.shape
    tm = min(512, s_len)

    def body(y_ref, t_ref, dy_ref, acc_ref):
        @pl.when(pl.program_id(0) == 0)
        def _():
            acc_ref[...] = jnp.zeros_like(acc_ref)

        diff = y_ref[...] - t_ref[...]
        dy_ref[...] = diff * (1.0 / d)
        acc_ref[...] += jnp.sum(diff * diff, axis=0, keepdims=True)

    row = pl.BlockSpec((tm, d), lambda i: (i, 0))
    return pl.pallas_call(
        body, name="loss_grad", grid=(s_len // tm,),
        in_specs=[row, row], out_specs=[row, pl.BlockSpec((1, d), lambda i: (0, 0))],
        out_shape=[jax.ShapeDtypeStruct((s_len, d), F32), jax.ShapeDtypeStruct((1, d), F32)],
        compiler_params=pltpu.CompilerParams(dimension_semantics=("arbitrary",)),
    )(y, target)


def _rnn_fwd(u, conv_w, conv_b, wa, ba, wi, bi, lam):
    s_len, width = u.shape
    tm = min(256, s_len)
    nb = width // RNN_BLOCK

    def body(u_ref, up_ref, cw_ref, cb_ref, wa_ref, ba_ref, wi_ref, bi_ref, lam_ref,
             h_ref, uc_ref, r_ref, i_ref, a_ref, b_scr, h_carry):
        step_id = pl.program_id(0)

        @pl.when(step_id == 0)
        def _():
            h_carry[...] = jnp.zeros_like(h_carry)

        for n in range(nb):
            blk = slice(n * RNN_BLOCK, (n + 1) * RNN_BLOCK)
            ut = u_ref[:, blk]
            prev = jnp.where(step_id > 0, up_ref[:, blk], 0.0)
            uc = cb_ref[:, blk] + cw_ref[3:4, blk] * ut
            for k in (1, 2, 3):
                uc = uc + cw_ref[3 - k:4 - k, blk] * _shift_down(ut, prev, k)
            ucb = uc.astype(BF16)
            r = _sigmoid(jnp.dot(ucb, wa_ref[n], preferred_element_type=F32) + ba_ref[:, blk])
            ig = _sigmoid(jnp.dot(ucb, wi_ref[n], preferred_element_type=F32) + bi_ref[:, blk])
            log_a = -LRU_C * r * _softplus(-lam_ref[:, blk])
            uc_ref[:, blk] = uc
            r_ref[:, blk] = r
            i_ref[:, blk] = ig
            a_ref[:, blk] = jnp.exp(log_a)
            b_scr[:, blk] = jnp.sqrt(_neg_expm1(2.0 * log_a)) * (ig * uc)

        def scan(t, h):
            h = a_ref[pl.ds(t, 1), :] * h + b_scr[pl.ds(t, 1), :]
            h_ref[pl.ds(t, 1), :] = h
            return h

        h_carry[...] = lax.fori_loop(0, tm, scan, h_carry[...], unroll=8)

    row = pl.BlockSpec((tm, width), lambda i: (i, 0))
    prev8 = pl.BlockSpec((8, width), lambda i: (jnp.maximum(i * (tm // 8) - 1, 0), 0))
    vec = pl.BlockSpec((1, width), lambda i: (0, 0))
    mat = pl.BlockSpec(wa.shape, lambda i: (0, 0, 0))
    return pl.pallas_call(
        body, name="rnn_fwd", grid=(s_len // tm,),
        in_specs=[row, prev8, pl.BlockSpec(conv_w.shape, lambda i: (0, 0)), vec, mat, vec, mat, vec, vec],
        out_specs=[row] * 5,
        out_shape=[jax.ShapeDtypeStruct((s_len, width), F32)] * 5,
        scratch_shapes=[pltpu.VMEM((tm, width), F32), pltpu.VMEM((1, width), F32)],
        compiler_params=pltpu.CompilerParams(dimension_semantics=("arbitrary",)),
    )(u, u, conv_w, conv_b, wa, ba, wi, bi, lam)


def _rnn_bwd(dh, a, h, r, ig, uc, lam, wa, wi):
    s_len, width = dh.shape
    tm = min(256, s_len)
    nt = s_len // tm
    nb = width // RNN_BLOCK

    def body(dh_ref, a_ref, h_ref, hp_ref, r_ref, i_ref, uc_ref, lam_ref, wa_ref, wi_ref,
             duc_ref, dwa_ref, dwi_ref, dba_ref, dbi_ref, dlam_ref, g_scr, c_scr, dsp_scr):
        step_id = pl.program_id(0)
        tile_id = nt - 1 - step_id

        @pl.when(step_id == 0)
        def _():
            c_scr[...] = jnp.zeros_like(c_scr)
            dsp_scr[...] = jnp.zeros_like(dsp_scr)
            dwa_ref[...] = jnp.zeros_like(dwa_ref)
            dwi_ref[...] = jnp.zeros_like(dwi_ref)
            dba_ref[...] = jnp.zeros_like(dba_ref)
            dbi_ref[...] = jnp.zeros_like(dbi_ref)

        def scan(j, c):
            t = tm - 1 - j
            g = dh_ref[pl.ds(t, 1), :] + c
            g_scr[pl.ds(t, 1), :] = g
            return a_ref[pl.ds(t, 1), :] * g

        c_scr[...] = lax.fori_loop(0, tm, scan, c_scr[...], unroll=8)

        for n in range(nb):
            blk = slice(n * RNN_BLOCK, (n + 1) * RNN_BLOCK)
            g_t = g_scr[:, blk]
            hp8 = jnp.where(tile_id > 0, hp_ref[:, blk], 0.0)
            h_prev = _shift_down(h_ref[:, blk], hp8, 1)
            av, rv, iv, ucv = a_ref[:, blk], r_ref[:, blk], i_ref[:, blk], uc_ref[:, blk]
            sp = _softplus(-lam_ref[:, blk])
            mag = jnp.sqrt(_neg_expm1(-2.0 * LRU_C * rv * sp))
            d_iu = g_t * mag
            dla = g_t * h_prev * av - g_t * (iv * ucv) * (av * av) / mag
            dsp_scr[:, blk] += jnp.sum(dla * (-LRU_C * rv), axis=0, keepdims=True)
            dra = dla * (-LRU_C * sp) * rv * (1.0 - rv)
            dia = d_iu * ucv * iv * (1.0 - iv)
            drab, diab, ucb = dra.astype(BF16), dia.astype(BF16), ucv.astype(BF16)
            duc_ref[:, blk] = (d_iu * iv
                               + lax.dot_general(drab, wa_ref[n], NT_DIMS, preferred_element_type=F32)
                               + lax.dot_general(diab, wi_ref[n], NT_DIMS, preferred_element_type=F32))
            dwa_ref[n] += lax.dot_general(ucb, drab, TN_DIMS, preferred_element_type=F32)
            dwi_ref[n] += lax.dot_general(ucb, diab, TN_DIMS, preferred_element_type=F32)
            dba_ref[:, blk] += jnp.sum(dra, axis=0, keepdims=True)
            dbi_ref[:, blk] += jnp.sum(dia, axis=0, keepdims=True)

        @pl.when(step_id == nt - 1)
        def _():
            dlam_ref[...] = -dsp_scr[...] * _sigmoid(-lam_ref[...])

    row = pl.BlockSpec((tm, width), lambda i: (nt - 1 - i, 0))
    prev8 = pl.BlockSpec((8, width), lambda i: (jnp.maximum((nt - 1 - i) * (tm // 8) - 1, 0), 0))
    vec = pl.BlockSpec((1, width), lambda i: (0, 0))
    mat = pl.BlockSpec(wa.shape, lambda i: (0, 0, 0))
    return pl.pallas_call(
        body, name="rnn_bwd", grid=(nt,),
        in_specs=[row, row, row, prev8, row, row, row, vec, mat, mat],
        out_specs=[row, mat, mat, vec, vec, vec],
        out_shape=[jax.ShapeDtypeStruct((s_len, width), F32), jax.ShapeDtypeStruct(wa.shape, F32),
                   jax.ShapeDtypeStruct(wa.shape, F32)] + [jax.ShapeDtypeStruct((1, width), F32)] * 3,
        scratch_shapes=[pltpu.VMEM((tm, width), F32), pltpu.VMEM((1, width), F32), pltpu.VMEM((1, width), F32)],
        compiler_params=pltpu.CompilerParams(dimension_semantics=("arbitrary",)),
    )(dh, a, h, h, r, ig, uc, lam, wa, wi)


def _conv_bwd(duc, u, conv_w):
    s_len, width = duc.shape
    tm = min(256, s_len)
    nt = s_len // tm

    def body(d_ref, dn_ref, u_ref, up_ref, cw_ref, du_ref, dcw_ref, dcb_ref):
        step_id = pl.program_id(0)

        @pl.when(step_id == 0)
        def _():
            dcw_ref[...] = jnp.zeros_like(dcw_ref)
            dcb_ref[...] = jnp.zeros_like(dcb_ref)

        for n in range(width // RNN_BLOCK):
            blk = slice(n * RNN_BLOCK, (n + 1) * RNN_BLOCK)
            dt = d_ref[:, blk]
            ut = u_ref[:, blk]
            nxt = jnp.where(step_id < nt - 1, dn_ref[:, blk], 0.0)
            prev = jnp.where(step_id > 0, up_ref[:, blk], 0.0)
            du = cw_ref[3:4, blk] * dt
            dcw_ref[3:4, blk] += jnp.sum(dt * ut, axis=0, keepdims=True)
            for k in (1, 2, 3):
                du = du + cw_ref[3 - k:4 - k, blk] * _shift_up(dt, nxt, k)
                dcw_ref[3 - k:4 - k, blk] += jnp.sum(dt * _shift_down(ut, prev, k), axis=0, keepdims=True)
            du_ref[:, blk] = du.astype(BF16)
            dcb_ref[:, blk] += jnp.sum(dt, axis=0, keepdims=True)

    row = pl.BlockSpec((tm, width), lambda i: (i, 0))
    prev8 = pl.BlockSpec((8, width), lambda i: (jnp.maximum(i * (tm // 8) - 1, 0), 0))
    next8 = pl.BlockSpec((8, width), lambda i: (jnp.minimum((i + 1) * (tm // 8), s_len // 8 - 1), 0))
    return pl.pallas_call(
        body, name="conv_bwd", grid=(nt,),
        in_specs=[row, next8, row, prev8, pl.BlockSpec(conv_w.shape, lambda i: (0, 0))],
        out_specs=[row, pl.BlockSpec(conv_w.shape, lambda i: (0, 0)), pl.BlockSpec((1, width), lambda i: (0, 0))],
        out_shape=[jax.ShapeDtypeStruct((s_len, width), BF16), jax.ShapeDtypeStruct(conv_w.shape, F32),
                   jax.ShapeDtypeStruct((1, width), F32)],
        compiler_params=pltpu.CompilerParams(dimension_semantics=("arbitrary",)),
    )(duc, duc, u, u, conv_w)


def _pair_sum(core, grads, theirs, first_row, out_dtype):
    n, half, _ = theirs.shape
    tb = 128 if half % 128 == 0 and first_row % 128 == 0 else half
    assert first_row % tb == 0 and half % tb == 0

    def body(c_ref, a_ref, b_ref, o_ref):
        o_ref[...] = (a_ref[...] + b_ref[...]).astype(out_dtype)

    blk = pl.BlockSpec((n, tb, LANES), lambda i, c: (0, i, 0))
    mine = pl.BlockSpec((n, tb, LANES), lambda i, c: (0, first_row // tb + c[0] * (half // tb) + i, 0))
    return pl.pallas_call(
        body, name="pair_sum",
        grid_spec=pltpu.PrefetchScalarGridSpec(
            num_scalar_prefetch=1, grid=(half // tb,), in_specs=[mine, blk], out_specs=blk),
        out_shape=jax.ShapeDtypeStruct((n, half, LANES), out_dtype),
        compiler_params=pltpu.CompilerParams(dimension_semantics=("parallel",)),
    )(core, grads, theirs)


def _sum_chips(parts):
    rows = parts.shape[1]
    tb = 128 if rows % 128 == 0 else rows

    def body(p_ref, o_ref):
        o_ref[...] = ((p_ref[0].astype(F32) + p_ref[1].astype(F32)) + p_ref[2].astype(F32)) + p_ref[3].astype(F32)

    return pl.pallas_call(
        body, name="chip_sum", grid=(rows // tb,),
        in_specs=[pl.BlockSpec((N_CHIPS, tb, LANES), lambda i: (0, i, 0))],
        out_specs=pl.BlockSpec((tb, LANES), lambda i: (i, 0)),
        out_shape=jax.ShapeDtypeStruct((rows, LANES), F32),
        compiler_params=pltpu.CompilerParams(dimension_semantics=("parallel",)),
    )(parts)


def _adamw(w, g, m, v, lead_per_block, rows_per_block):
    n, rows, cols = w.shape
    blk = pl.BlockSpec((lead_per_block, rows_per_block, cols), lambda i, j: (i, j, 0))

    def body(w_ref, g_ref, m_ref, v_ref, d_ref, nm_ref, nv_ref):
        gt = g_ref[...]
        nm = ADAM_B1 * m_ref[...] + (1.0 - ADAM_B1) * gt
        nv = ADAM_B2 * v_ref[...] + (1.0 - ADAM_B2) * (gt * gt)
        m_hat = nm / (1.0 - ADAM_B1 ** ADAM_STEP)
        v_hat = nv / (1.0 - ADAM_B2 ** ADAM_STEP)
        d_ref[...] = -ADAM_LR * (m_hat / (jnp.sqrt(v_hat) + ADAM_EPS) + ADAM_WD * w_ref[...])
        nm_ref[...] = nm
        nv_ref[...] = nv

    return pl.pallas_call(
        body, name="adamw", grid=(n // lead_per_block, rows // rows_per_block), in_specs=[blk] * 4,
        out_specs=[blk] * 3,
        out_shape=[jax.ShapeDtypeStruct(w.shape, F32)] * 3,
        compiler_params=pltpu.CompilerParams(dimension_semantics=("parallel", "parallel")),
    )(w, g, m, v)


def _place():
    x, y, c = lax.axis_index("x"), lax.axis_index("y"), lax.axis_index("c")
    return x, y, c, [(1 - x, y), (x, 1 - y), (1 - x, 1 - y)]


ANY = pl.BlockSpec(memory_space=pl.ANY)
COPY_PARTS = 4


def _remote_parts(src_of, dst_of, rows, send_sem, recv_sem, to, begin=True):
    parts = COPY_PARTS if rows % (16 * COPY_PARTS) == 0 else 1
    step = rows // parts
    for i in range(parts if begin is not False else 0):
        pltpu.make_async_remote_copy(src_ref=src_of(i * step, step), dst_ref=dst_of(i * step, step),
                                     send_sem=send_sem, recv_sem=recv_sem, device_id=to, device_id_type=MESH).start()
    if begin == "only":
        return None
    return pltpu.make_async_remote_copy(src_ref=src_of(0, rows), dst_ref=dst_of(0, rows), send_sem=send_sem,
                                        recv_sem=recv_sem, device_id=to, device_id_type=MESH)


GATHER_SEMS = 7


def _gather_steps(p_refs, o_refs, send_sems, recv_sems):
    x, y, c, chips = _place()
    me = 2 * x + y

    def half(ref, which):
        rows = ref.shape[0] // 2
        return ref.at[pl.ds(which * rows, rows)]

    def copy(k, src, dst, to):
        return pltpu.make_async_remote_copy(src_ref=src, dst_ref=dst, send_sem=send_sems.at[k],
                                            recv_sem=recv_sems.at[k], device_id=to, device_id_type=MESH)

    def first_copies():
        out = []
        for b, (p_ref, o_ref) in enumerate(zip(p_refs, o_refs)):
            for j, (cx, cy) in enumerate(chips):
                out.append(copy(GATHER_SEMS * b + j, half(p_ref, c), half(o_ref.at[me], c), (cx, cy, c)))
            out.append(copy(GATHER_SEMS * b + 6, p_ref, o_ref.at[me], (x, y, 1 - c)))
        return out

    def passed_copies():
        out = []
        for b, o_ref in enumerate(o_refs):
            for j, (cx, cy) in enumerate(chips):
                landed = half(o_ref.at[2 * cx + cy], c)
                out.append(copy(GATHER_SEMS * b + 3 + j, landed, landed, (x, y, 1 - c)))
        return out

    def start():
        for cp in first_copies():
            cp.start()

    def forward():
        for b, o_ref in enumerate(o_refs):
            for j, (cx, cy) in enumerate(chips):
                landed = half(o_ref.at[2 * cx + cy], c)
                copy(GATHER_SEMS * b + j, landed, landed, (x, y, c)).wait_recv()
        for cp in passed_copies():
            cp.start()

    def finish():
        for b, o_ref in enumerate(o_refs):
            for j, (cx, cy) in enumerate(chips):
                other = half(o_ref.at[2 * cx + cy], 1 - c)
                copy(GATHER_SEMS * b + 3 + j, other, other, (x, y, c)).wait_recv()
            copy(GATHER_SEMS * b + 6, o_ref.at[me], o_ref.at[me], (x, y, c)).wait_recv()
        for cp in first_copies() + passed_copies():
            cp.wait_send()

    return start, forward, finish


def _gather_chips(shards):
    nb = len(shards)

    def body(*refs):
        for step in _gather_steps(refs[:nb], refs[nb:2 * nb], *refs[2 * nb:]):
            step()

    return pl.pallas_call(
        body, name="gather_chips", in_specs=[ANY] * nb, out_specs=[ANY] * nb,
        out_shape=[jax.ShapeDtypeStruct((N_CHIPS,) + s.shape, s.dtype) for s in shards],
        scratch_shapes=[pltpu.SemaphoreType.DMA((GATHER_SEMS * nb,)), pltpu.SemaphoreType.DMA((GATHER_SEMS * nb,))],
    )(*shards)


def _swap_halves(bufs, spans):
    nb = len(bufs)

    def body(*refs):
        g_refs, t_refs, (send_sems, recv_sems) = refs[:nb], refs[nb:2 * nb], refs[2 * nb:]
        x, y, c, _ = _place()
        gives = []
        for b, (g_ref, t_ref, (first_row, half)) in enumerate(zip(g_refs, t_refs, spans)):
            for j in range(N_CHIPS):
                k = b * N_CHIPS + j
                gives.append(_remote_parts(
                    lambda r0, m, g_ref=g_ref, j=j, base=first_row + (1 - c) * half: g_ref.at[j, pl.ds(base + r0, m), :],
                    lambda r0, m, t_ref=t_ref, j=j: t_ref.at[j, pl.ds(r0, m), :],
                    half, send_sems.at[k], recv_sems.at[k], (x, y, 1 - c)))
        for give in gives:
            give.wait()

    return pl.pallas_call(
        body, name="swap_halves", in_specs=[ANY] * nb, out_specs=[ANY] * nb,
        out_shape=[jax.ShapeDtypeStruct((b.shape[0], half, b.shape[2]), b.dtype) for b, (_, half) in zip(bufs, spans)],
        scratch_shapes=[pltpu.SemaphoreType.DMA((nb * N_CHIPS,)), pltpu.SemaphoreType.DMA((nb * N_CHIPS,))],
    )(*bufs)


def _scatter_steps(t_refs, o_refs, send_sems, recv_sems):
    x, y, c, chips = _place()
    me = 2 * x + y

    def slot(ref, chip):
        return lambda r0, n: ref.at[chip, pl.ds(r0, n), :]

    def sends(begin):
        return [_remote_parts(slot(t_ref, 2 * cx + cy), slot(o_ref, me), t_ref.shape[1], send_sems.at[3 * b + j],
                              recv_sems.at[3 * b + j], (cx, cy, c), begin)
                for b, (t_ref, o_ref) in enumerate(zip(t_refs, o_refs)) for j, (cx, cy) in enumerate(chips)]

    def start():
        sends("only")

    def finish():
        for b, o_ref in enumerate(o_refs):
            for j, (cx, cy) in enumerate(chips):
                landed = o_ref.at[2 * cx + cy]
                pltpu.make_async_remote_copy(src_ref=landed, dst_ref=landed, send_sem=send_sems.at[3 * b + j],
                                             recv_sem=recv_sems.at[3 * b + j], device_id=(x, y, c),
                                             device_id_type=MESH).wait_recv()
        for cp in sends(False):
            cp.wait_send()

    return start, finish


def _scatter_chips(bufs):
    nb = len(bufs)

    def body(*refs):
        for step in _scatter_steps(refs[:nb], refs[nb:2 * nb], *refs[2 * nb:]):
            step()

    return pl.pallas_call(
        body, name="scatter_chips", in_specs=[ANY] * nb, out_specs=[ANY] * nb,
        out_shape=[jax.ShapeDtypeStruct(b.shape, b.dtype) for b in bufs],
        scratch_shapes=[pltpu.SemaphoreType.DMA((3 * nb,)), pltpu.SemaphoreType.DMA((3 * nb,))],
    )(*bufs)


def _give_sibling(bufs):
    nb = len(bufs)

    def body(*refs):
        h_refs, o_refs, (send_sems, recv_sems) = refs[:nb], refs[nb:2 * nb], refs[2 * nb:]
        x, y, c, _ = _place()
        gives = [_remote_parts(lambda r0, n, h_ref=h_ref: h_ref.at[pl.ds(r0, n), :],
                               lambda r0, n, o_ref=o_ref: o_ref.at[pl.ds(r0, n), :],
                               h_ref.shape[0], send_sems.at[b], recv_sems.at[b], (x, y, 1 - c))
                 for b, (h_ref, o_ref) in enumerate(zip(h_refs, o_refs))]
        for give in gives:
            give.wait()

    return pl.pallas_call(
        body, name="give_sibling", in_specs=[ANY] * nb, out_specs=[ANY] * nb,
        out_shape=[jax.ShapeDtypeStruct(b.shape, b.dtype) for b in bufs],
        scratch_shapes=[pltpu.SemaphoreType.DMA((nb,)), pltpu.SemaphoreType.DMA((nb,))],
    )(*bufs)


def _pack(arrays, dtype, row_align):
    flat = []
    for arr in arrays:
        v = arr.reshape(-1)
        flat.append(jnp.pad(v, (0, (-v.shape[0]) % LANES)))
    total = sum(f.shape[0] for f in flat) // LANES
    pad_rows = (-total) % row_align
    if pad_rows:
        flat.append(jnp.zeros((pad_rows * LANES,), dtype))
    return jnp.concatenate(flat).reshape(-1, LANES)


def _unpack(buf, shapes, rows_align=1):
    lead = buf.shape[:-2]
    flat = buf.reshape(lead + (-1,))
    out, off = [], 0
    for shape in shapes:
        size = 1
        for dim in shape:
            size *= dim
        out.append(flat[..., off:off + size].reshape(lead + tuple(shape)))
        off += size + (-size) % (LANES * rows_align)
    return out


def _from_chips(parts, axis):
    return jnp.concatenate([parts[j] for j in range(N_CHIPS)], axis=axis)


def kernel(x, ln_g, ln_b, attn_w_in, attn_b_f, attn_w_out, rnn_w_in, rnn_conv_w, rnn_conv_b, rnn_w_a, rnn_b_a, rnn_w_i, rnn_b_i, rnn_lambda, rnn_w_out, loss_target, m_ln_g, m_ln_b, m_attn_w_in, m_attn_b_f, m_attn_w_out, m_rnn_w_in, m_rnn_conv_w, m_rnn_conv_b, m_rnn_w_a, m_rnn_b_a, m_rnn_w_i, m_rnn_b_i, m_rnn_lambda, m_rnn_w_out, v_ln_g, v_ln_b, v_attn_w_in, v_attn_b_f, v_attn_w_out, v_rnn_w_in, v_rnn_conv_w, v_rnn_conv_b, v_rnn_w_a, v_rnn_b_a, v_rnn_w_i, v_rnn_b_i, v_rnn_lambda, v_rnn_w_out):
    s_len, d = x.shape[1], x.shape[2]
    xs = x.reshape(s_len, d)
    target = loss_target.reshape(s_len, d)
    heads = attn_b_f.shape[1]
    qkvg = attn_w_in.shape[2] * N_CHIPS - heads

    me = 2 * lax.axis_index("x") + lax.axis_index("y")
    core = lax.axis_index("c").astype(jnp.int32)
    small = [rnn_conv_w, rnn_conv_b, rnn_b_a, rnn_b_i, rnn_lambda]
    a_in, a_out = attn_w_in.astype(BF16), attn_w_out.astype(BF16)
    later = [a_in[1], a_out[1]] + [w.astype(BF16) for w in (rnn_w_in, rnn_w_a, rnn_w_i, rnn_w_out)]
    got_in, got_out, got_small = _gather_chips([a_in[0], a_out[0], _pack(small, F32, 16)])
    conv_w, conv_b, b_a, b_i, lam = [
        _from_chips(p, ax) for p, ax in zip(_unpack(got_small, [w.shape for w in small]), (2, 1, 1, 1, 1))]

    def attn_weights(w_in, w_out):
        full = jnp.concatenate([w_in[j] for j in range(N_CHIPS)], axis=1)
        cat = jnp.concatenate([full[:, :d] * (HEAD_DIM ** -0.5), full[:, d:qkvg],
                               jnp.pad(full[:, qkvg:], ((0, 0), (0, 128 - heads)))], axis=1).astype(BF16)
        return cat, w_out.reshape(d, d)

    w_cat, w_out_a = [None, None], [None, None]
    w_cat[0], w_out_a[0] = attn_weights(got_in, got_out)
    b_f = jnp.pad(attn_b_f, ((0, 0), (0, 128 - heads)))

    saved = []
    cur = xs
    for layer in range(DEPTH):
        idx = layer // 2
        g_l, b_l = ln_g[layer:layer + 1], ln_b[layer:layer + 1]
        if layer % 2 == 0:
            q, k, v, gate, fl = _proj(cur, w_cat[idx], [(0, d, BF16), (d, d, BF16), (2 * d, d, BF16),
                                                         (3 * d, d, F32), (4 * d, 128, F32)], "attn_proj")
            cum, sneg = _fcum(fl, b_f[idx:idx + 1])
            o, lse, *rest = _flash_fwd(q, k, v, cum, later if layer == 0 else ())
            if layer == 0:
                w_cat[1], w_out_a[1] = attn_weights(rest[0], rest[1])
                w_in_r = jnp.moveaxis(rest[2], 0, 2).reshape(2, d, 2 * d)
                w_a = jnp.transpose(rest[3], (1, 2, 0, 3, 4)).reshape(2, -1, RNN_BLOCK, RNN_BLOCK)
                w_i = jnp.transpose(rest[4], (1, 2, 0, 3, 4)).reshape(2, -1, RNN_BLOCK, RNN_BLOCK)
                w_out_r = jnp.moveaxis(rest[5], 0, 1).reshape(2, d, d)
            nxt, xh, rs, yg = _out_ln(o, gate, cur, w_out_a[idx], g_l, b_l, "out_ln")
            saved.append(dict(x=cur, q=q, k=k, v=v, gate=gate, cum=cum, sneg=sneg, a=o, lse=lse, xh=xh, rs=rs, yg=yg))
        else:
            u, gate = _proj(cur, w_in_r[idx], [(0, d, F32), (d, d, F32)], "rnn_proj")
            vecs = [t[idx:idx + 1] for t in (conv_b, b_a, b_i, lam)]
            hseq, uc, r, ig, a = _rnn_fwd(u, conv_w[idx], vecs[0], w_a[idx], vecs[1], w_i[idx], vecs[2], vecs[3])
            nxt, xh, rs, yg = _out_ln(hseq, gate, cur, w_out_r[idx], g_l, b_l, "out_ln")
            saved.append(dict(x=cur, u=u, gate=gate, uc=uc, r=r, ig=ig, av=a, a=hseq, xh=xh, rs=rs, yg=yg, lam=vecs[3]))
        cur = nxt

    dout, sq = _loss_grad(cur, target)
    loss = lax.psum(0.5 * jnp.sum(sq) / d, ("x", "y", "c"))

    g_ln_g, g_ln_b = [None] * DEPTH, [None] * DEPTH
    g_attn = [None] * 2
    g_rnn = [None] * 2
    slots = None
    core1 = core.reshape(1)
    late_half = (SLOT_ROWS - LATE_ROWS) // 2
    for layer in reversed(range(DEPTH)):
        idx = layer // 2
        sv = saved[layer]
        if layer == 0:
            theirs_late, = _swap_halves([slots], [(LATE_ROWS, late_half)])
            pair_late = _pair_sum(core1, slots, theirs_late, LATE_ROWS, BF16)
        w_out = w_out_a[idx] if layer % 2 == 0 else w_out_r[idx]
        dz, da, dgate, dg, db = _ln_bwd(dout, sv["xh"], sv["rs"], ln_g[layer:layer + 1], w_out, sv["gate"], sv["a"],
                                        "ln_bwd")
        g_ln_g[layer], g_ln_b[layer] = dg, db
        slots = _dw_out(slots, sv["yg"], dz, (ROWS_ATTN_OUT if layer % 2 == 0 else ROWS_RNN_OUT)[idx])
        if layer % 2 == 0:
            dq, dk, dv, dcum, *arrived = _flash_bwd(sv["q"], sv["k"], sv["v"], sv["a"], da, sv["lse"], sv["cum"],
                                                    [pair_late] if layer == 0 else ())
            dlogit, dbf = _fbwd(dcum, sv["sneg"])
            pieces = [dq, dk, dv, dgate, dlogit]
            dout = _mm_nt(dz, [(p, j * d) for j, p in enumerate(pieces)], w_cat[idx], "attn_dx")
            slots = _dw_attn_in(slots, sv["x"], pieces[:4], idx)
            g_attn[idx] = dict(w_f=_mm_tn(sv["x"], dlogit, "dw_f")[:, :heads], b_f=dbf[:, 0])
        else:
            duc, d_wa, d_wi, d_ba, d_bi, d_lam = _rnn_bwd(da, sv["av"], sv["a"], sv["r"], sv["ig"], sv["uc"], sv["lam"],
                                                          w_a[idx], w_i[idx])
            du, d_cw, d_cb = _conv_bwd(duc, sv["u"], conv_w[idx])
            dout = _mm_nt(dz, [(du, 0), (dgate, d)], w_in_r[idx], "rnn_dx")
            slots = _dw_rnn_in(slots, sv["x"], (du, dgate), idx)
            g_rnn[idx] = dict(conv_w=d_cw, conv_b=d_cb[0], w_a=d_wa, b_a=d_ba[0], w_i=d_wi, b_i=d_bi[0], lam=d_lam[0])
    grad_x = dout.reshape(x.shape)

    def by_chip(t, axis):
        shape = t.shape[:axis] + (N_CHIPS, t.shape[axis] // N_CHIPS) + t.shape[axis + 1:]
        flat = jnp.moveaxis(t.reshape(shape), axis, 0).reshape(N_CHIPS, -1)
        return jnp.pad(flat, ((0, 0), (0, (-flat.shape[1]) % (8 * LANES)))).reshape(N_CHIPS, -1, LANES)

    def everywhere(t):
        flat = t.reshape(-1)
        flat = jnp.pad(flat, (0, (-flat.shape[0]) % (8 * LANES))).reshape(-1, LANES)
        return jnp.broadcast_to(flat[None], (N_CHIPS,) + flat.shape)

    def both(key):
        return jnp.stack([it[key] for it in g_rnn])

    in_rows = jnp.stack([slots[1:, r:r + d, :heads] for r in ROWS_ATTN_IN], axis=1)
    edges = jnp.concatenate([in_rows, jnp.stack([it["w_f"] for it in g_attn])[None]])
    rows = [by_chip(both("w_a"), 2), by_chip(both("w_i"), 2), everywhere(edges), by_chip(both("conv_w"), 2),
            by_chip(both("conv_b"), 1), by_chip(both("b_a"), 1), by_chip(both("b_i"), 1), by_chip(both("lam"), 1),
            everywhere(jnp.concatenate(g_ln_g)), everywhere(jnp.concatenate(g_ln_b)),
            everywhere(jnp.stack([it["b_f"] for it in g_attn]))]
    used = sum(r.shape[1] for r in rows)
    small = jnp.concatenate(rows + [jnp.zeros((N_CHIPS, (-used) % 32, LANES), F32)], axis=1)

    spans = [(0, LATE_ROWS // 2), (0, small.shape[1] // 2)]
    theirs = _swap_halves([slots, small], spans)
    pair = [_pair_sum(core1, slots, theirs[0], 0, BF16), _pair_sum(core1, small, theirs[1], 0, F32)]
    summed = []
    for mine_all, got in zip([pair_late] + pair, list(arrived) + list(_scatter_chips(pair))):
        own = lax.dynamic_index_in_dim(mine_all, me, 0, keepdims=False)
        summed.append(_sum_chips(lax.dynamic_update_index_in_dim(got, own, me, 0)))
    late, early, small_sum = [
        jnp.concatenate([jnp.where(core == 0, mine, other), jnp.where(core == 0, other, mine)])
        for mine, other in zip(summed, _give_sibling(summed))]

    def rows_at(first, n):
        return early[first:first + n] if first < LATE_ROWS else late[first - LATE_ROWS:first - LATE_ROWS + n]

    g_in_group = jnp.stack([rows_at(r, d) for r in ROWS_ATTN_IN])
    g_w_out_a = jnp.stack([rows_at(r, d // N_CHIPS) for r in ROWS_ATTN_OUT])
    g_w_out_r = jnp.stack([rows_at(r, d // N_CHIPS) for r in ROWS_RNN_OUT])
    folded = jnp.stack([rows_at(r, d // 2) for r in ROWS_RNN_IN])
    g_w_in_r = jnp.concatenate([folded[:, :, :d // 2], folded[:, :, d // 2:]], axis=1)
    (g_w_a, g_w_i, g_edges, g_conv_w, g_conv_b, g_b_a, g_b_i, g_lam, g_ln_g_sum, g_ln_b_sum,
     g_b_f) = _unpack(small_sum, [
        rnn_w_a.shape, rnn_w_i.shape, (N_CHIPS, 2, d, heads), rnn_conv_w.shape, rnn_conv_b.shape, rnn_b_a.shape,
        rnn_b_i.shape, rnn_lambda.shape, ln_g.shape, ln_b.shape, attn_b_f.shape], rows_align=8)
    wide = jnp.concatenate([g_in_group, lax.dynamic_index_in_dim(g_edges, me, 0, keepdims=False)], axis=2)
    g_w_in_a = lax.dynamic_slice(wide, (0, 0, (heads // N_CHIPS) * me), attn_w_in.shape)

    def adam_nd(w, g, m, v, view, rows_per_block):
        outs = _adamw(w.reshape(view), g.reshape(view), m.reshape(view), v.reshape(view), 1, rows_per_block)
        return [t.reshape(w.shape) for t in outs]

    turned = [jnp.transpose(t, (2, 0, 1)) for t in (attn_w_in, g_w_in_a, m_attn_w_in, v_attn_w_in)]
    upd = {
        "attn_w_in": [jnp.transpose(t, (1, 2, 0)) for t in _adamw(*turned, attn_w_in.shape[2] // N_CHIPS, 2)],
        "attn_w_out": adam_nd(attn_w_out, g_w_out_a, m_attn_w_out, v_attn_w_out, attn_w_out.shape, 256),
        "rnn_w_in": adam_nd(rnn_w_in, g_w_in_r, m_rnn_w_in, v_rnn_w_in, rnn_w_in.shape, 512),
        "rnn_w_a": adam_nd(rnn_w_a, g_w_a, m_rnn_w_a, v_rnn_w_a, (1, -1, RNN_BLOCK), 512),
        "rnn_w_i": adam_nd(rnn_w_i, g_w_i, m_rnn_w_i, v_rnn_w_i, (1, -1, RNN_BLOCK), 512),
        "rnn_w_out": adam_nd(rnn_w_out, g_w_out_r, m_rnn_w_out, v_rnn_w_out, rnn_w_out.shape, 256),
    }
    smalls = [rnn_conv_w, rnn_conv_b, rnn_b_a, rnn_b_i, rnn_lambda, ln_g, ln_b, attn_b_f]
    g_small = [g_conv_w, g_conv_b, g_b_a, g_b_i, g_lam, g_ln_g_sum, g_ln_b_sum, g_b_f]
    m_small = [m_rnn_conv_w, m_rnn_conv_b, m_rnn_b_a, m_rnn_b_i, m_rnn_lambda, m_ln_g, m_ln_b, m_attn_b_f]
    v_small = [v_rnn_conv_w, v_rnn_conv_b, v_rnn_b_a, v_rnn_b_i, v_rnn_lambda, v_ln_g, v_ln_b, v_attn_b_f]
    packs = [_pack(t, F32, 16)[None] for t in (smalls, g_small, m_small, v_small)]
    small_out = [_unpack(t[0], [w.shape for w in smalls]) for t in _adamw(*packs, 1, 16)]
    for k, name in enumerate(("rnn_conv_w", "rnn_conv_b", "rnn_b_a", "rnn_b_i", "rnn_lambda", "ln_g", "ln_b",
                              "attn_b_f")):
        upd[name] = [small_out[0][k], small_out[1][k], small_out[2][k]]
    grad = {"attn_w_in": g_w_in_a, "attn_w_out": g_w_out_a, "rnn_w_in": g_w_in_r, "rnn_w_a": g_w_a, "rnn_w_i": g_w_i,
            "rnn_w_out": g_w_out_r, "rnn_conv_w": g_conv_w, "rnn_conv_b": g_conv_b, "rnn_b_a": g_b_a,
            "rnn_b_i": g_b_i, "rnn_lambda": g_lam, "ln_g": g_ln_g_sum, "ln_b": g_ln_b_sum, "attn_b_f": g_b_f}
    names = ("ln_g", "ln_b", "attn_w_in", "attn_b_f", "attn_w_out", "rnn_w_in", "rnn_conv_w", "rnn_conv_b",
             "rnn_w_a", "rnn_b_a", "rnn_w_i", "rnn_b_i", "rnn_lambda", "rnn_w_out")
    return (loss, grad_x, *[grad[n] for n in names], *[upd[n][0] for n in names], *[upd[n][1] for n in names],
            *[upd[n][2] for n in names])
```

```python
import functools

import jax
import jax.numpy as jnp
from jax import lax
from jax.experimental import pallas as pl
from jax.experimental.pallas import tpu as pltpu

F32 = jnp.float32
BF16 = jnp.bfloat16
MESH = pl.DeviceIdType.MESH

DEPTH = 4
HEAD_DIM = 64
HEAD_PAIR = 2 * HEAD_DIM
RNN_BLOCK = 256
CONV_WIDTH = 4
LRU_C = 8.0
ALPHA = (2.0 * DEPTH) ** 0.25
LN_EPS = 1e-5
ADAM_LR, ADAM_B1, ADAM_B2, ADAM_EPS, ADAM_WD, ADAM_STEP = 0.001, 0.9, 0.999, 1e-08, 0.01, 10
N_CHIPS = 4
LANES = 1024
NEG = -1e30

NT_DIMS = (((1,), (1,)), ((), ()))
TN_DIMS = (((0,), (0,)), ((), ()))


def _sigmoid(z):
    return 1.0 / (1.0 + jnp.exp(-z))


def _softplus(z):
    return jnp.maximum(z, 0.0) + jnp.log(1.0 + jnp.exp(-jnp.abs(z)))


def _neg_expm1(y):
    poly = y * (1.0 + y * (0.5 + y * (1.0 / 6.0 + y * (1.0 / 24.0))))
    return -jnp.where(y > -0.05, poly, jnp.exp(y) - 1.0)


def _shift_down(cur, prev8, k):
    n = cur.shape[0]
    row = lax.broadcasted_iota(jnp.int32, cur.shape, 0)
    fix = jnp.tile(pltpu.roll(prev8, k, 0), (n // 8, 1))
    return jnp.where(row < k, fix, pltpu.roll(cur, k, 0))


def _shift_up(cur, next8, k):
    n = cur.shape[0]
    row = lax.broadcasted_iota(jnp.int32, cur.shape, 0)
    fix = jnp.tile(pltpu.roll(next8, 8 - k, 0), (n // 8, 1))
    return jnp.where(row >= n - k, fix, pltpu.roll(cur, n - k, 0))


def _proj(x, w, outs, name):
    s_len, d = x.shape
    tm = min(512, s_len)

    def body(x_ref, w_ref, *o_refs):
        xb = x_ref[...].astype(BF16)
        for (c0, wd, dt), o_ref in zip(outs, o_refs):
            step = min(wd, 512)
            for cc in range(0, wd, step):
                o_ref[:, cc:cc + step] = jnp.dot(
                    xb, w_ref[:, c0 + cc:c0 + cc + step], preferred_element_type=F32).astype(dt)

    return pl.pallas_call(
        body, name=name, grid=(s_len // tm,),
        in_specs=[pl.BlockSpec((tm, d), lambda i: (i, 0)), pl.BlockSpec(w.shape, lambda i: (0, 0))],
        out_specs=[pl.BlockSpec((tm, wd), lambda i: (i, 0)) for _, wd, _ in outs],
        out_shape=[jax.ShapeDtypeStruct((s_len, wd), dt) for _, wd, dt in outs],
        compiler_params=pltpu.CompilerParams(dimension_semantics=("parallel",)),
    )(x, w)


def _mm_nt(dz, pieces, w, name):
    s_len, d = dz.shape
    tm = min(256, s_len)
    n = len(pieces)
    cols = [(c0, p.shape[1]) for p, c0 in pieces]

    def body(dz_ref, *rest):
        w_ref, o_ref = rest[n], rest[n + 1]
        acc = ALPHA * dz_ref[...]
        for (c0, wd), p_ref in zip(cols, rest[:n]):
            acc = acc + lax.dot_general(p_ref[...], w_ref[:, c0:c0 + wd], NT_DIMS, preferred_element_type=F32)
        o_ref[...] = acc

    return pl.pallas_call(
        body, name=name, grid=(s_len // tm,),
        in_specs=[pl.BlockSpec((tm, d), lambda i: (i, 0))]
        + [pl.BlockSpec((tm, wd), lambda i: (i, 0)) for _, wd in cols]
        + [pl.BlockSpec(w.shape, lambda i: (0, 0))],
        out_specs=pl.BlockSpec((tm, d), lambda i: (i, 0)),
        out_shape=jax.ShapeDtypeStruct((s_len, d), F32),
        compiler_params=pltpu.CompilerParams(dimension_semantics=("parallel",)),
    )(dz, *[p for p, _ in pieces], w)


def _mm_tn(a, b, name):
    s_len, m = a.shape
    n = b.shape[1]
    tn = min(n, 512)
    ts = min(s_len, 512)

    def body(a_ref, b_ref, o_ref):
        @pl.when(pl.program_id(1) == 0)
        def _():
            o_ref[...] = jnp.zeros_like(o_ref)

        o_ref[...] += lax.dot_general(a_ref[...].astype(BF16), b_ref[...].astype(BF16), TN_DIMS,
                                      preferred_element_type=F32)

    return pl.pallas_call(
        body, name=name, grid=(n // tn, s_len // ts),
        in_specs=[pl.BlockSpec((ts, m), lambda j, s: (s, 0)), pl.BlockSpec((ts, tn), lambda j, s: (s, j))],
        out_specs=pl.BlockSpec((m, tn), lambda j, s: (0, j)),
        out_shape=jax.ShapeDtypeStruct((m, n), F32),
        compiler_params=pltpu.CompilerParams(dimension_semantics=("parallel", "arbitrary")),
    )(a, b)


ROWS_ATTN_IN = (0, 2048)
ROWS_ATTN_OUT = (1024, 1280)
ROWS_RNN_OUT = (1536, 1792)
ROWS_RNN_IN = (3072, 3584)
ROWS_GATES = 4096
LATE_ROWS = 1280
SLOT_ROWS = 4352


DW_ROWS = 1024


def _dw_call(body, name, slots, operands, specs, out_block, out_index, grid, semantics):
    return pl.pallas_call(
        body, name=name, grid=grid,
        in_specs=specs if slots is None else [ANY] + specs,
        out_specs=pl.BlockSpec(out_block, out_index),
        out_shape=jax.ShapeDtypeStruct((N_CHIPS, SLOT_ROWS, LANES), F32),
        input_output_aliases={} if slots is None else {0: 0},
        compiler_params=pltpu.CompilerParams(dimension_semantics=semantics),
    )(*(operands if slots is None else [slots] + operands))


def _dw_attn_in(slots, x, pieces, layer):
    s_len, d = x.shape
    ts = min(s_len, DW_ROWS)
    steps = s_len // ts
    half = d // 2

    def body(*refs):
        x_ref, p_refs, o_ref = refs[-6], refs[-5:-1], refs[-1]

        @pl.when(pl.program_id(1) == 0)
        def _():
            o_ref[...] = jnp.zeros_like(o_ref)

        xb = x_ref[...].astype(BF16)
        for j, p_ref in enumerate(p_refs):
            o_ref[j] += lax.dot_general(xb, p_ref[...], TN_DIMS, preferred_element_type=F32)

        @pl.when(pl.program_id(1) == steps - 1)
        def _():
            o_ref[0] = o_ref[0] * (HEAD_DIM ** -0.5)

    specs = [pl.BlockSpec((ts, d), lambda i, s: (s, 0))] + [pl.BlockSpec((ts, half), lambda i, s: (s, i))] * 4
    return _dw_call(body, "dw_attn_in", slots, [x] + list(pieces), specs, (N_CHIPS, d, half),
                    lambda i, s: (0, ROWS_ATTN_IN[layer] // d, i), (2, steps), ("parallel", "arbitrary"))


def _dw_out(slots, yg, dz, first_row):
    s_len, d = dz.shape
    ts = min(s_len, DW_ROWS)
    rows = d // N_CHIPS

    def body(*refs):
        a_ref, b_ref, o_ref = refs[-3:]

        @pl.when(pl.program_id(0) == 0)
        def _():
            o_ref[...] = jnp.zeros_like(o_ref)

        prod = lax.dot_general(a_ref[...], b_ref[...].astype(BF16), TN_DIMS, preferred_element_type=F32)
        o_ref[...] += prod.reshape(N_CHIPS, rows, d)

    specs = [pl.BlockSpec((ts, d), lambda s: (s, 0))] * 2
    return _dw_call(body, "dw_out", slots, [yg, dz], specs, (N_CHIPS, rows, d), lambda s: (0, first_row // rows, 0),
                    (s_len // ts,), ("arbitrary",))


def _dw_rnn_in(slots, x, parts, layer):
    s_len, d = x.shape
    ts = min(s_len, DW_ROWS)
    half = d // 2

    def body(*refs):
        x_ref, p_refs, o_ref = refs[-4], refs[-3:-1], refs[-1]

        @pl.when(pl.program_id(0) == 0)
        def _():
            o_ref[...] = jnp.zeros_like(o_ref)

        xb = x_ref[...].astype(BF16)
        for p, p_ref in enumerate(p_refs):
            for jj in (0, 1):
                for r in (0, 1):
                    o_ref[2 * p + jj, :, r * half:(r + 1) * half] += lax.dot_general(
                        xb[:, r * half:(r + 1) * half], p_ref[:, jj * half:(jj + 1) * half], TN_DIMS,
                        preferred_element_type=F32)

    specs = [pl.BlockSpec((ts, d), lambda s: (s, 0))] * 3
    return _dw_call(body, "dw_rnn_in", slots, [x] + list(parts), specs, (N_CHIPS, half, d),
                    lambda s: (0, ROWS_RNN_IN[layer] // half, 0), (s_len // ts,), ("arbitrary",))


def _fcum(fl, bias):
    s_len = fl.shape[0]

    def body(fl_ref, b_ref, cum_ref, sg_ref):
        z = fl_ref[...] + b_ref[...]
        e = jnp.exp(-jnp.abs(z))
        logf = jnp.minimum(z, 0.0) - jnp.log(1.0 + e)
        sneg = jnp.where(z >= 0, e, 1.0) / (1.0 + e)
        run = logf.T[0:16, :]
        sg_ref[...] = sneg.T[0:16, :]
        lane = lax.broadcasted_iota(jnp.int32, (16, s_len), 1)
        sh = 1
        while sh < s_len:
            run = run + jnp.where(lane >= sh, pltpu.roll(run, sh, 1), 0.0)
            sh *= 2
        cum_ref[...] = run

    return pl.pallas_call(
        body, name="fcum",
        out_shape=[jax.ShapeDtypeStruct((16, s_len), F32), jax.ShapeDtypeStruct((16, s_len), F32)],
    )(fl, bias)


def _fbwd(dcum, sneg):
    s_len = dcum.shape[1]

    def body(dc_ref, sg_ref, dl_ref, db_ref):
        run = dc_ref[...]
        lane = lax.broadcasted_iota(jnp.int32, (16, s_len), 1)
        sh = 1
        while sh < s_len:
            run = run + jnp.where(lane < s_len - sh, pltpu.roll(run, s_len - sh, 1), 0.0)
            sh *= 2
        dlog = run * sg_ref[...]
        db_ref[...] = jnp.broadcast_to(jnp.sum(dlog, axis=1, keepdims=True), (16, 128))
        full = jnp.concatenate([dlog, jnp.zeros((112, s_len), F32)], axis=0)
        dl_ref[...] = full.T.astype(BF16)

    return pl.pallas_call(
        body, name="fbwd",
        out_shape=[jax.ShapeDtypeStruct((s_len, 128), BF16), jax.ShapeDtypeStruct((16, 128), F32)],
    )(dcum, sneg)


FWD_TILE = 1024
BWD_TILE = 512


def _flash_fwd(q, k, v, cum, shards=()):
    s_len, width = q.shape
    tile = min(FWD_TILE, s_len // 2)
    nt = s_len // tile
    reps = tile // 128
    cumr = cum.reshape(-1, tile)
    ns = len(shards)
    pairs = width // HEAD_PAIR

    def body(*refs):
        q_ref, k_ref, v_ref, cum_ref = refs[:4]
        o_ref, lse_ref = refs[4 + ns:6 + ns]
        q_t, v_t, cum_col = refs[6 + 2 * ns:9 + 2 * ns]
        pid = pl.program_id(0)
        if ns:
            start, forward, finish = _gather_steps(refs[4:4 + ns], refs[6 + ns:6 + 2 * ns], *refs[9 + 2 * ns:])
            pl.when(pid == 0)(start)
            pl.when(pid == pairs // 2)(forward)
        top = lax.broadcasted_iota(jnp.int32, (HEAD_PAIR, tile), 0) < HEAD_DIM
        causal = (lax.broadcasted_iota(jnp.int32, (tile, tile), 0)
                  <= lax.broadcasted_iota(jnp.int32, (tile, tile), 1))

        def pre(i, carry):
            r0 = pl.multiple_of(i * tile, tile)
            q_t[i] = q_ref[pl.ds(r0, tile), :].astype(F32).T.astype(BF16)
            v_t[i] = v_ref[pl.ds(r0, tile), :].astype(F32).T.astype(BF16)
            for h in (0, 1):
                row = cum_ref[pl.ds((2 * pid + h) * nt + i, 1), :]
                cum_col[h, pl.ds(r0, tile), :] = jnp.broadcast_to(row, (HEAD_PAIR, tile)).T
            return carry

        lax.fori_loop(0, nt, pre, 0)

        def q_loop(qi, carry):
            qt = q_t[qi]
            zt = jnp.zeros_like(qt)
            qms = (jnp.where(top, qt, zt), jnp.where(top, zt, qt))

            def step(kj, state, masked):
                m0, l0, m1, l1, acc = state
                k0 = pl.multiple_of(kj * tile, tile)
                kt = k_ref[pl.ds(k0, tile), :]
                vt = v_t[kj]
                ms, ls, scales, contrib = [m0, m1], [l0, l1], [], None
                for h in (0, 1):
                    s = jnp.dot(kt, qms[h], preferred_element_type=F32)
                    s = s - jnp.tile(cum_col[h, pl.ds(k0, tile), :], (1, reps))
                    if masked:
                        s = jnp.where(causal, s, NEG)
                    m_new = jnp.maximum(ms[h], jnp.max(s, axis=0, keepdims=True))
                    p = jnp.exp(s - m_new)
                    scale = jnp.exp(ms[h] - m_new)
                    ls[h] = scale * ls[h] + jnp.sum(p, axis=0, keepdims=True)
                    ms[h] = m_new
                    scales.append(scale)
                    vm = jnp.where(top, vt, zt) if h == 0 else jnp.where(top, zt, vt)
                    part = jnp.dot(vm, p.astype(BF16), preferred_element_type=F32)
                    contrib = part if contrib is None else contrib + part
                acc = jnp.where(top, scales[0], scales[1]) * acc + contrib
                return ms[0], ls[0], ms[1], ls[1], acc

            low = jnp.full((1, tile), NEG, F32)
            zero = jnp.zeros((1, tile), F32)
            state = step(qi, (low, zero, low, zero, jnp.zeros((HEAD_PAIR, tile), F32)), True)
            m0, l0, m1, l1, acc = lax.fori_loop(0, qi, lambda kj, c: step(kj, c, False), state)
            q0 = pl.multiple_of(qi * tile, tile)
            o_ref[pl.ds(q0, tile), :] = (acc / jnp.where(top, l0, l1)).T
            lse_ref[pl.ds((2 * pid) * nt + qi, 1), :] = m0 + jnp.log(l0)
            lse_ref[pl.ds((2 * pid + 1) * nt + qi, 1), :] = m1 + jnp.log(l1)
            return carry

        lax.fori_loop(0, nt, q_loop, 0)
        if ns:
            pl.when(pid == pairs - 1)(finish)

    blk = pl.BlockSpec((s_len, HEAD_PAIR), lambda p: (0, p))
    full = pl.BlockSpec(cumr.shape, lambda p: (0, 0))
    sems = [pltpu.SemaphoreType.DMA((GATHER_SEMS * ns,))] * 2 if ns else []
    o, lse, *gathered = pl.pallas_call(
        body, name="flash_fwd_gather" if ns else "flash_fwd", grid=(pairs,),
        in_specs=[blk, blk, blk, full] + [ANY] * ns,
        out_specs=[blk, full] + [ANY] * ns,
        out_shape=[jax.ShapeDtypeStruct((s_len, width), F32), jax.ShapeDtypeStruct(cumr.shape, F32)]
        + [jax.ShapeDtypeStruct((N_CHIPS,) + s.shape, s.dtype) for s in shards],
        scratch_shapes=[pltpu.VMEM((nt, HEAD_PAIR, tile), BF16), pltpu.VMEM((nt, HEAD_PAIR, tile), BF16),
                        pltpu.VMEM((2, s_len, HEAD_PAIR), F32)] + sems,
        compiler_params=pltpu.CompilerParams(dimension_semantics=("arbitrary",)),
    )(q, k, v, cumr, *shards)
    return (o, lse.reshape(cum.shape), *gathered)


def _flash_bwd(q, k, v, o, do, lse, cum, outgoing=()):
    s_len, width = q.shape
    tile = min(BWD_TILE, s_len // 2)
    nt = s_len // tile
    reps = tile // 128
    cumr = cum.reshape(-1, tile)
    lse = lse.reshape(-1, tile)
    ns = len(outgoing)
    pairs = width // HEAD_PAIR

    def body(*refs):
        q_ref, k_ref, v_ref, o_ref, do_ref, lse_ref, cum_ref = refs[:7]
        dq_ref, dk_ref, dv_ref, dcum_ref = refs[7 + ns:11 + ns]
        (q_t, do_t, k_t, cum_col, dq_t_acc, delta, q_side, dk_acc, dv_acc,
         k_side) = refs[11 + 2 * ns:21 + 2 * ns]
        pid = pl.program_id(0)
        if ns:
            start, finish = _scatter_steps(refs[7:7 + ns], refs[11 + ns:11 + 2 * ns], *refs[21 + 2 * ns:])
            pl.when(pid == 0)(start)
        top = lax.broadcasted_iota(jnp.int32, (HEAD_PAIR, tile), 0) < HEAD_DIM
        causal = (lax.broadcasted_iota(jnp.int32, (tile, tile), 0)
                  <= lax.broadcasted_iota(jnp.int32, (tile, tile), 1))

        def pre(i, carry):
            r0 = pl.multiple_of(i * tile, tile)
            d_o = do_ref[pl.ds(r0, tile), :]
            prod_t = (d_o * o_ref[pl.ds(r0, tile), :]).T
            delta[pl.ds(i, 1), :] = jnp.sum(prod_t[:HEAD_DIM], axis=0, keepdims=True)
            delta[pl.ds(nt + i, 1), :] = jnp.sum(prod_t[HEAD_DIM:], axis=0, keepdims=True)
            do_t[i] = d_o.T.astype(BF16)
            q_t[i] = q_ref[pl.ds(r0, tile), :].astype(F32).T.astype(BF16)
            k_t[i] = k_ref[pl.ds(r0, tile), :].astype(F32).T.astype(BF16)
            dq_t_acc[i] = jnp.zeros((HEAD_PAIR, tile), F32)
            for h in (0, 1):
                row = cum_ref[pl.ds((2 * pid + h) * nt + i, 1), :]
                cum_col[h, pl.ds(r0, tile), :] = jnp.broadcast_to(row, (HEAD_PAIR, tile)).T
                q_side[pl.ds(h * nt + i, 1), :] = jnp.zeros((1, tile), F32)
            return carry

        lax.fori_loop(0, nt, pre, 0)

        def kv_loop(kj, carry):
            k0 = pl.multiple_of(kj * tile, tile)
            kt = k_ref[pl.ds(k0, tile), :]
            vt = v_ref[pl.ds(k0, tile), :]
            ktt = k_t[kj]
            zt = jnp.zeros_like(ktt)
            kms =(jnp.where(top, ktt, zt), jnp.where(top, zt, ktt))
            cols = [jnp.tile(cum_col[h, pl.ds(k0, tile), :], (1, reps)) for h in (0, 1)]
            dk_acc[...] = jnp.zeros_like(dk_acc)
            dv_acc[...] = jnp.zeros_like(dv_acc)
            k_side[...] = jnp.zeros_like(k_side)

            def step(qi, carry, masked):
                q0 = pl.multiple_of(qi * tile, tile)
                qtt = q_t[qi]
                dtt = do_t[qi]
                dq_part = None
                for h in (0, 1):
                    q_m = jnp.where(top, qtt, zt) if h == 0 else jnp.where(top, zt, qtt)
                    do_m = jnp.where(top, dtt, zt) if h == 0 else jnp.where(top, zt, dtt)
                    s = jnp.dot(kt, q_m, preferred_element_type=F32) - cols[h]
                    if masked:
                        s = jnp.where(causal, s, NEG)
                    p = jnp.exp(s - lse_ref[pl.ds((2 * pid + h) * nt + qi, 1), :])
                    dp = jnp.dot(vt, do_m, preferred_element_type=F32)
                    ds = p * (dp - delta[pl.ds(h * nt + qi, 1), :])
                    q_side[pl.ds(h * nt + qi, 1), :] += jnp.sum(ds, axis=0, keepdims=True)
                    folded = ds[:, :128]
                    for b in range(1, reps):
                        folded = folded + ds[:, b * 128:(b + 1) * 128]
                    k_side[h] += folded
                    pb = p.astype(BF16)
                    dsb = ds.astype(BF16)
                    dv_acc[...] += lax.dot_general(do_m, pb, NT_DIMS, preferred_element_type=F32)
                    dk_acc[...] += lax.dot_general(q_m, dsb, NT_DIMS, preferred_element_type=F32)
                    part = jnp.dot(kms[h], dsb, preferred_element_type=F32)
                    dq_part = part if dq_part is None else dq_part + part
                dq_t_acc[qi] += dq_part
                return carry

            step(kj, 0, True)
            lax.fori_loop(kj + 1, nt, lambda qi, c: step(qi, c, False), 0)
            dk_ref[pl.ds(k0, tile), :] = dk_acc[...].T.astype(BF16)
            dv_ref[pl.ds(k0, tile), :] = dv_acc[...].T.astype(BF16)
            for h in (0, 1):
                dcum_ref[pl.ds((2 * pid + h) * nt + kj, 1), :] = -jnp.sum(k_side[h].T, axis=0, keepdims=True)
            return carry

        lax.fori_loop(0, nt, kv_loop, 0)

        def fin(i, carry):
            r0 = pl.multiple_of(i * tile, tile)
            dq_ref[pl.ds(r0, tile), :] = dq_t_acc[i].T.astype(BF16)
            for h in (0, 1):
                dcum_ref[pl.ds((2 * pid + h) * nt + i, 1), :] += q_side[pl.ds(h * nt + i, 1), :]
            return carry

        lax.fori_loop(0, nt, fin, 0)
        if ns:
            pl.when(pid == pairs - 1)(finish)

    blk = pl.BlockSpec((s_len, HEAD_PAIR), lambda p: (0, p))
    full = pl.BlockSpec(cumr.shape, lambda p: (0, 0))
    transposed = pltpu.VMEM((nt, HEAD_PAIR, tile), BF16)
    sems = [pltpu.SemaphoreType.DMA((3 * ns,))] * 2 if ns else []
    dq, dk, dv, dcum, *arrived = pl.pallas_call(
        body, name="flash_bwd_scatter" if ns else "flash_bwd", grid=(pairs,),
        in_specs=[blk, blk, blk, blk, blk, full, full] + [ANY] * ns,
        out_specs=[blk, blk, blk, full] + [ANY] * ns,
        out_shape=[jax.ShapeDtypeStruct((s_len, width), BF16)] * 3 + [jax.ShapeDtypeStruct(cumr.shape, F32)]
        + [jax.ShapeDtypeStruct(t.shape, t.dtype) for t in outgoing],
        scratch_shapes=[transposed, transposed, transposed,
                        pltpu.VMEM((2, s_len, HEAD_PAIR), F32), pltpu.VMEM((nt, HEAD_PAIR, tile), F32),
                        pltpu.VMEM((2 * nt, tile), F32), pltpu.VMEM((2 * nt, tile), F32),
                        pltpu.VMEM((HEAD_PAIR, tile), F32), pltpu.VMEM((HEAD_PAIR, tile), F32),
                        pltpu.VMEM((2, tile, HEAD_PAIR), F32)] + sems,
        compiler_params=pltpu.CompilerParams(dimension_semantics=("arbitrary",)),
    )(q, k, v, o, do, lse, cumr, *outgoing)
    return (dq, dk, dv, dcum.reshape(cum.shape), *arrived)


def _out_ln(a, gate, x, w, g, b, name):
    s_len, d = x.shape
    tm = min(256, s_len)

    def body(a_ref, gate_ref, x_ref, w_ref, g_ref, b_ref, xn_ref, xh_ref, rs_ref, yg_ref):
        gt = gate_ref[...]
        yg = (a_ref[...] * (gt * _sigmoid(gt))).astype(BF16)
        yg_ref[...] = yg
        z = ALPHA * x_ref[...] + jnp.dot(yg, w_ref[...], preferred_element_type=F32)
        zc = z - jnp.mean(z, axis=1, keepdims=True)
        rstd = lax.rsqrt(jnp.mean(zc * zc, axis=1, keepdims=True) + LN_EPS)
        xh = zc * rstd
        xh_ref[...] = xh
        xn_ref[...] = xh * g_ref[...] + b_ref[...]
        rs_ref[...] = jnp.broadcast_to(rstd, (tm, 128))

    row = pl.BlockSpec((tm, d), lambda i: (i, 0))
    vec = pl.BlockSpec((1, d), lambda i: (0, 0))
    return pl.pallas_call(
        body, name=name, grid=(s_len // tm,),
        in_specs=[row, row, row, pl.BlockSpec(w.shape, lambda i: (0, 0)), vec, vec],
        out_specs=[row, row, pl.BlockSpec((tm, 128), lambda i: (i, 0)), row],
        out_shape=[jax.ShapeDtypeStruct((s_len, d), F32), jax.ShapeDtypeStruct((s_len, d), F32),
                   jax.ShapeDtypeStruct((s_len, 128), F32), jax.ShapeDtypeStruct((s_len, d), BF16)],
        compiler_params=pltpu.CompilerParams(dimension_semantics=("parallel",)),
    )(a, gate, x, w, g, b)


def _ln_bwd(dout, xh, rs, g, w, gate, a, name):
    s_len, d = dout.shape
    tm = min(256, s_len)

    def body(do_ref, xh_ref, rs_ref, g_ref, w_ref, gate_ref, a_ref, dz_ref, da_ref, dgate_ref, dg_ref, db_ref):
        @pl.when(pl.program_id(0) == 0)
        def _():
            dg_ref[...] = jnp.zeros_like(dg_ref)
            db_ref[...] = jnp.zeros_like(db_ref)

        dout_t = do_ref[...]
        xh_t = xh_ref[...]
        dg_ref[...] += jnp.sum(dout_t * xh_t, axis=0, keepdims=True)
        db_ref[...] += jnp.sum(dout_t, axis=0, keepdims=True)
        dxh = dout_t * g_ref[...]
        m1 = jnp.mean(dxh, axis=1, keepdims=True)
        m2 = jnp.mean(dxh * xh_t, axis=1, keepdims=True)
        dz = rs_ref[:, 0:1] * (dxh - m1 - xh_t * m2)
        dz_ref[...] = dz
        dyg = lax.dot_general(dz.astype(BF16), w_ref[...], NT_DIMS, preferred_element_type=F32)
        gt = gate_ref[...]
        sg = _sigmoid(gt)
        da_ref[...] = dyg * (gt * sg)
        dgate_ref[...] = (dyg * a_ref[...] * (sg * (1.0 + gt * (1.0 - sg)))).astype(BF16)

    row = pl.BlockSpec((tm, d), lambda i: (i, 0))
    vec = pl.BlockSpec((1, d), lambda i: (0, 0))
    return pl.pallas_call(
        body, name=name, grid=(s_len // tm,),
        in_specs=[row, row, pl.BlockSpec((tm, 128), lambda i: (i, 0)), vec, pl.BlockSpec(w.shape, lambda i: (0, 0)),
                  row, row],
        out_specs=[row, row, row, vec, vec],
        out_shape=[jax.ShapeDtypeStruct((s_len, d), F32), jax.ShapeDtypeStruct((s_len, d), F32),
                   jax.ShapeDtypeStruct((s_len, d), BF16), jax.ShapeDtypeStruct((1, d), F32),
                   jax.ShapeDtypeStruct((1, d), F32)],
        compiler_params=pltpu.CompilerParams(dimension_semantics=("arbitrary",)),
    )(dout, xh, rs, g, w, gate, a)


def _loss_grad(y, target):
    s_len, d = y.shape
    tm = min(512, s_len)

    def body(y_ref, t_ref, dy_ref, acc_ref):
        @pl.when(pl.program_id(0) == 0)
        def _():
            acc_ref[...] = jnp.zeros_like(acc_ref)

        diff = y_ref[...] - t_ref[...]
        dy_ref[...] = diff * (1.0 / d)
        acc_ref[...] += jnp.sum(diff * diff, axis=0, keepdims=True)

    row = pl.BlockSpec((tm, d), lambda i: (i, 0))
    return pl.pallas_call(
        body, name="loss_grad", grid=(s_len // tm,),
        in_specs=[row, row], out_specs=[row, pl.BlockSpec((1, d), lambda i: (0, 0))],
        out_shape=[jax.ShapeDtypeStruct((s_len, d), F32), jax.ShapeDtypeStruct((1, d), F32)],
        compiler_params=pltpu.CompilerParams(dimension_semantics=("arbitrary",)),
    )(y, target)


def _rnn_fwd(u, conv_w, conv_b, wa, ba, wi, bi, lam):
    s_len, width = u.shape
    tm = min(256, s_len)
    nb = width // RNN_BLOCK

    def body(u_ref, up_ref, cw_ref, cb_ref, wa_ref, ba_ref, wi_ref, bi_ref, lam_ref,
             h_ref, uc_ref, r_ref, i_ref, a_ref, b_scr, h_carry):
        step_id = pl.program_id(0)

        @pl.when(step_id == 0)
        def _():
            h_carry[...] = jnp.zeros_like(h_carry)

        for n in range(nb):
            blk = slice(n * RNN_BLOCK, (n + 1) * RNN_BLOCK)
            ut = u_ref[:, blk]
            prev = jnp.where(step_id > 0, up_ref[:, blk], 0.0)
            uc = cb_ref[:, blk] + cw_ref[3:4, blk] * ut
            for k in (1, 2, 3):
                uc = uc + cw_ref[3 - k:4 - k, blk] * _shift_down(ut, prev, k)
            ucb = uc.astype(BF16)
            r = _sigmoid(jnp.dot(ucb, wa_ref[n], preferred_element_type=F32) + ba_ref[:, blk])
            ig = _sigmoid(jnp.dot(ucb, wi_ref[n], preferred_element_type=F32) + bi_ref[:, blk])
            log_a = -LRU_C * r * _softplus(-lam_ref[:, blk])
            uc_ref[:, blk] = uc
            r_ref[:, blk] = r
            i_ref[:, blk] = ig
            a_ref[:, blk] = jnp.exp(log_a)
            b_scr[:, blk] = jnp.sqrt(_neg_expm1(2.0 * log_a)) * (ig * uc)

        def scan(t, h):
            h = a_ref[pl.ds(t, 1), :] * h + b_scr[pl.ds(t, 1), :]
            h_ref[pl.ds(t, 1), :] = h
            return h

        h_carry[...] = lax.fori_loop(0, tm, scan, h_carry[...], unroll=8)

    row = pl.BlockSpec((tm, width), lambda i: (i, 0))
    prev8 = pl.BlockSpec((8, width), lambda i: (jnp.maximum(i * (tm // 8) - 1, 0), 0))
    vec = pl.BlockSpec((1, width), lambda i: (0, 0))
    mat = pl.BlockSpec(wa.shape, lambda i: (0, 0, 0))
    return pl.pallas_call(
        body, name="rnn_fwd", grid=(s_len // tm,),
        in_specs=[row, prev8, pl.BlockSpec(conv_w.shape, lambda i: (0, 0)), vec, mat, vec, mat, vec, vec],
        out_specs=[row] * 5,
        out_shape=[jax.ShapeDtypeStruct((s_len, width), F32)] * 5,
        scratch_shapes=[pltpu.VMEM((tm, width), F32), pltpu.VMEM((1, width), F32)],
        compiler_params=pltpu.CompilerParams(dimension_semantics=("arbitrary",)),
    )(u, u, conv_w, conv_b, wa, ba, wi, bi, lam)


def _rnn_bwd(dh, a, h, r, ig, uc, lam, wa, wi):
    s_len, width = dh.shape
    tm = min(256, s_len)
    nt = s_len // tm
    nb = width // RNN_BLOCK

    def body(dh_ref, a_ref, h_ref, hp_ref, r_ref, i_ref, uc_ref, lam_ref, wa_ref, wi_ref,
             duc_ref, dwa_ref, dwi_ref, dba_ref, dbi_ref, dlam_ref, g_scr, c_scr, dsp_scr):
        step_id = pl.program_id(0)
        tile_id = nt - 1 - step_id

        @pl.when(step_id == 0)
        def _():
            c_scr[...] = jnp.zeros_like(c_scr)
            dsp_scr[...] = jnp.zeros_like(dsp_scr)
            dwa_ref[...] = jnp.zeros_like(dwa_ref)
            dwi_ref[...] = jnp.zeros_like(dwi_ref)
            dba_ref[...] = jnp.zeros_like(dba_ref)
            dbi_ref[...] = jnp.zeros_like(dbi_ref)

        def scan(j, c):
            t = tm - 1 - j
            g = dh_ref[pl.ds(t, 1), :] + c
            g_scr[pl.ds(t, 1), :] = g
            return a_ref[pl.ds(t, 1), :] * g

        c_scr[...] = lax.fori_loop(0, tm, scan, c_scr[...], unroll=8)

        for n in range(nb):
            blk = slice(n * RNN_BLOCK, (n + 1) * RNN_BLOCK)
            g_t = g_scr[:, blk]
            hp8 = jnp.where(tile_id > 0, hp_ref[:, blk], 0.0)
            h_prev = _shift_down(h_ref[:, blk], hp8, 1)
            av, rv, iv, ucv = a_ref[:, blk], r_ref[:, blk], i_ref[:, blk], uc_ref[:, blk]
            sp = _softplus(-lam_ref[:, blk])
            mag = jnp.sqrt(_neg_expm1(-2.0 * LRU_C * rv * sp))
            d_iu = g_t * mag
            dla = g_t * h_prev * av - g_t * (iv * ucv) * (av * av) / mag
            dsp_scr[:, blk] += jnp.sum(dla * (-LRU_C * rv), axis=0, keepdims=True)
            dra = dla * (-LRU_C * sp) * rv * (1.0 - rv)
            dia = d_iu * ucv * iv * (1.0 - iv)
            drab, diab, ucb = dra.astype(BF16), dia.astype(BF16), ucv.astype(BF16)
            duc_ref[:, blk] = (d_iu * iv
                               + lax.dot_general(drab, wa_ref[n], NT_DIMS, preferred_element_type=F32)
                               + lax.dot_general(diab, wi_ref[n], NT_DIMS, preferred_element_type=F32))
            dwa_ref[n] += lax.dot_general(ucb, drab, TN_DIMS, preferred_element_type=F32)
            dwi_ref[n] += lax.dot_general(ucb, diab, TN_DIMS, preferred_element_type=F32)
            dba_ref[:, blk] += jnp.sum(dra, axis=0, keepdims=True)
            dbi_ref[:, blk] += jnp.sum(dia, axis=0, keepdims=True)

        @pl.when(step_id == nt - 1)
        def _():
            dlam_ref[...] = -dsp_scr[...] * _sigmoid(-lam_ref[...])

    row = pl.BlockSpec((tm, width), lambda i: (nt - 1 - i, 0))
    prev8 = pl.BlockSpec((8, width), lambda i: (jnp.maximum((nt - 1 - i) * (tm // 8) - 1, 0), 0))
    vec = pl.BlockSpec((1, width), lambda i: (0, 0))
    mat = pl.BlockSpec(wa.shape, lambda i: (0, 0, 0))
    return pl.pallas_call(
        body, name="rnn_bwd", grid=(nt,),
        in_specs=[row, row, row, prev8, row, row, row, vec, mat, mat],
        out_specs=[row, mat, mat, vec, vec, vec],
        out_shape=[jax.ShapeDtypeStruct((s_len, width), F32), jax.ShapeDtypeStruct(wa.shape, F32),
                   jax.ShapeDtypeStruct(wa.shape, F32)] + [jax.ShapeDtypeStruct((1, width), F32)] * 3,
        scratch_shapes=[pltpu.VMEM((tm, width), F32), pltpu.VMEM((1, width), F32), pltpu.VMEM((1, width), F32)],
        compiler_params=pltpu.CompilerParams(dimension_semantics=("arbitrary",)),
    )(dh, a, h, h, r, ig, uc, lam, wa, wi)


def _conv_bwd(duc, u, conv_w):
    s_len, width = duc.shape
    tm = min(256, s_len)
    nt = s_len // tm

    def body(d_ref, dn_ref, u_ref, up_ref, cw_ref, du_ref, dcw_ref, dcb_ref):
        step_id = pl.program_id(0)

        @pl.when(step_id == 0)
        def _():
            dcw_ref[...] = jnp.zeros_like(dcw_ref)
            dcb_ref[...] = jnp.zeros_like(dcb_ref)

        for n in range(width // RNN_BLOCK):
            blk = slice(n * RNN_BLOCK, (n + 1) * RNN_BLOCK)
            dt = d_ref[:, blk]
            ut = u_ref[:, blk]
            nxt = jnp.where(step_id < nt - 1, dn_ref[:, blk], 0.0)
            prev = jnp.where(step_id > 0, up_ref[:, blk], 0.0)
            du = cw_ref[3:4, blk] * dt
            dcw_ref[3:4, blk] += jnp.sum(dt * ut, axis=0, keepdims=True)
            for k in (1, 2, 3):
                du = du + cw_ref[3 - k:4 - k, blk] * _shift_up(dt, nxt, k)
                dcw_ref[3 - k:4 - k, blk] += jnp.sum(dt * _shift_down(ut, prev, k), axis=0, keepdims=True)
            du_ref[:, blk] = du.astype(BF16)
            dcb_ref[:, blk] += jnp.sum(dt, axis=0, keepdims=True)

    row = pl.BlockSpec((tm, width), lambda i: (i, 0))
    prev8 = pl.BlockSpec((8, width), lambda i: (jnp.maximum(i * (tm // 8) - 1, 0), 0))
    next8 = pl.BlockSpec((8, width), lambda i: (jnp.minimum((i + 1) * (tm // 8), s_len // 8 - 1), 0))
    return pl.pallas_call(
        body, name="conv_bwd", grid=(nt,),
        in_specs=[row, next8, row, prev8, pl.BlockSpec(conv_w.shape, lambda i: (0, 0))],
        out_specs=[row, pl.BlockSpec(conv_w.shape, lambda i: (0, 0)), pl.BlockSpec((1, width), lambda i: (0, 0))],
        out_shape=[jax.ShapeDtypeStruct((s_len, width), BF16), jax.ShapeDtypeStruct(conv_w.shape, F32),
                   jax.ShapeDtypeStruct((1, width), F32)],
        compiler_params=pltpu.CompilerParams(dimension_semantics=("arbitrary",)),
    )(duc, duc, u, u, conv_w)


def _pair_sum(core, grads, theirs, first_row, out_dtype):
    n, half, _ = theirs.shape
    tb = 128 if half % 128 == 0 and first_row % 128 == 0 else half
    assert first_row % tb == 0 and half % tb == 0

    def body(c_ref, a_ref, b_ref, o_ref):
        o_ref[...] = (a_ref[...] + b_ref[...]).astype(out_dtype)

    blk = pl.BlockSpec((n, tb, LANES), lambda i, c: (0, i, 0))
    mine = pl.BlockSpec((n, tb, LANES), lambda i, c: (0, first_row // tb + c[0] * (half // tb) + i, 0))
    return pl.pallas_call(
        body, name="pair_sum",
        grid_spec=pltpu.PrefetchScalarGridSpec(
            num_scalar_prefetch=1, grid=(half // tb,), in_specs=[mine, blk], out_specs=blk),
        out_shape=jax.ShapeDtypeStruct((n, half, LANES), out_dtype),
        compiler_params=pltpu.CompilerParams(dimension_semantics=("parallel",)),
    )(core, grads, theirs)


def _sum_chips(parts):
    rows = parts.shape[1]
    tb = 128 if rows % 128 == 0 else rows

    def body(p_ref, o_ref):
        o_ref[...] = ((p_ref[0].astype(F32) + p_ref[1].astype(F32)) + p_ref[2].astype(F32)) + p_ref[3].astype(F32)

    return pl.pallas_call(
        body, name="chip_sum", grid=(rows // tb,),
        in_specs=[pl.BlockSpec((N_CHIPS, tb, LANES), lambda i: (0, i, 0))],
        out_specs=pl.BlockSpec((tb, LANES), lambda i: (i, 0)),
        out_shape=jax.ShapeDtypeStruct((rows, LANES), F32),
        compiler_params=pltpu.CompilerParams(dimension_semantics=("parallel",)),
    )(parts)


def _adamw(w, g, m, v, lead_per_block, rows_per_block):
    n, rows, cols = w.shape
    blk = pl.BlockSpec((lead_per_block, rows_per_block, cols), lambda i, j: (i, j, 0))

    def body(w_ref, g_ref, m_ref, v_ref, d_ref, nm_ref, nv_ref):
        gt = g_ref[...]
        nm = ADAM_B1 * m_ref[...] + (1.0 - ADAM_B1) * gt
        nv = ADAM_B2 * v_ref[...] + (1.0 - ADAM_B2) * (gt * gt)
        m_hat = nm / (1.0 - ADAM_B1 ** ADAM_STEP)
        v_hat = nv / (1.0 - ADAM_B2 ** ADAM_STEP)
        d_ref[...] = -ADAM_LR * (m_hat / (jnp.sqrt(v_hat) + ADAM_EPS) + ADAM_WD * w_ref[...])
        nm_ref[...] = nm
        nv_ref[...] = nv

    return pl.pallas_call(
        body, name="adamw", grid=(n // lead_per_block, rows // rows_per_block), in_specs=[blk] * 4,
        out_specs=[blk] * 3,
        out_shape=[jax.ShapeDtypeStruct(w.shape, F32)] * 3,
        compiler_params=pltpu.CompilerParams(dimension_semantics=("parallel", "parallel")),
    )(w, g, m, v)


def _place():
    x, y, c = lax.axis_index("x"), lax.axis_index("y"), lax.axis_index("c")
    return x, y, c, [(1 - x, y), (x, 1 - y), (1 - x, 1 - y)]


ANY = pl.BlockSpec(memory_space=pl.ANY)
COPY_PARTS = 4


def _remote_parts(src_of, dst_of, rows, send_sem, recv_sem, to, begin=True):
    parts = COPY_PARTS if rows % (16 * COPY_PARTS) == 0 else 1
    step = rows // parts
    for i in range(parts if begin is not False else 0):
        pltpu.make_async_remote_copy(src_ref=src_of(i * step, step), dst_ref=dst_of(i * step, step),
                                     send_sem=send_sem, recv_sem=recv_sem, device_id=to, device_id_type=MESH).start()
    if begin == "only":
        return None
    return pltpu.make_async_remote_copy(src_ref=src_of(0, rows), dst_ref=dst_of(0, rows), send_sem=send_sem,
                                        recv_sem=recv_sem, device_id=to, device_id_type=MESH)


GATHER_SEMS = 7


def _gather_steps(p_refs, o_refs, send_sems, recv_sems):
    x, y, c, chips = _place()
    me = 2 * x + y

    def half(ref, which):
        rows = ref.shape[0] // 2
        return ref.at[pl.ds(which * rows, rows)]

    def copy(k, src, dst, to):
        return pltpu.make_async_remote_copy(src_ref=src, dst_ref=dst, send_sem=send_sems.at[k],
                                            recv_sem=recv_sems.at[k], device_id=to, device_id_type=MESH)

    def first_copies():
        out = []
        for b, (p_ref, o_ref) in enumerate(zip(p_refs, o_refs)):
            for j, (cx, cy) in enumerate(chips):
                out.append(copy(GATHER_SEMS * b + j, half(p_ref, c), half(o_ref.at[me], c), (cx, cy, c)))
            out.append(copy(GATHER_SEMS * b + 6, p_ref, o_ref.at[me], (x, y, 1 - c)))
        return out

    def passed_copies():
        out = []
        for b, o_ref in enumerate(o_refs):
            for j, (cx, cy) in enumerate(chips):
                landed = half(o_ref.at[2 * cx + cy], c)
                out.append(copy(GATHER_SEMS * b + 3 + j, landed, landed, (x, y, 1 - c)))
        return out

    def start():
        for cp in first_copies():
            cp.start()

    def forward():
        for b, o_ref in enumerate(o_refs):
            for j, (cx, cy) in enumerate(chips):
                landed = half(o_ref.at[2 * cx + cy], c)
                copy(GATHER_SEMS * b + j, landed, landed, (x, y, c)).wait_recv()
        for cp in passed_copies():
            cp.start()

    def finish():
        for b, o_ref in enumerate(o_refs):
            for j, (cx, cy) in enumerate(chips):
                other = half(o_ref.at[2 * cx + cy], 1 - c)
                copy(GATHER_SEMS * b + 3 + j, other, other, (x, y, c)).wait_recv()
            copy(GATHER_SEMS * b + 6, o_ref.at[me], o_ref.at[me], (x, y, c)).wait_recv()
        for cp in first_copies() + passed_copies():
            cp.wait_send()

    return start, forward, finish


def _gather_chips(shards):
    nb = len(shards)

    def body(*refs):
        for step in _gather_steps(refs[:nb], refs[nb:2 * nb], *refs[2 * nb:]):
            step()

    return pl.pallas_call(
        body, name="gather_chips", in_specs=[ANY] * nb, out_specs=[ANY] * nb,
        out_shape=[jax.ShapeDtypeStruct((N_CHIPS,) + s.shape, s.dtype) for s in shards],
        scratch_shapes=[pltpu.SemaphoreType.DMA((GATHER_SEMS * nb,)), pltpu.SemaphoreType.DMA((GATHER_SEMS * nb,))],
    )(*shards)


def _swap_halves(bufs, spans):
    nb = len(bufs)

    def body(*refs):
        g_refs, t_refs, (send_sems, recv_sems) = refs[:nb], refs[nb:2 * nb], refs[2 * nb:]
        x, y, c, _ = _place()
        gives = []
        for b, (g_ref, t_ref, (first_row, half)) in enumerate(zip(g_refs, t_refs, spans)):
            for j in range(N_CHIPS):
                k = b * N_CHIPS + j
                gives.append(_remote_parts(
                    lambda r0, m, g_ref=g_ref, j=j, base=first_row + (1 - c) * half: g_ref.at[j, pl.ds(base + r0, m), :],
                    lambda r0, m, t_ref=t_ref, j=j: t_ref.at[j, pl.ds(r0, m), :],
                    half, send_sems.at[k], recv_sems.at[k], (x, y, 1 - c)))
        for give in gives:
            give.wait()

    return pl.pallas_call(
        body, name="swap_halves", in_specs=[ANY] * nb, out_specs=[ANY] * nb,
        out_shape=[jax.ShapeDtypeStruct((b.shape[0], half, b.shape[2]), b.dtype) for b, (_, half) in zip(bufs, spans)],
        scratch_shapes=[pltpu.SemaphoreType.DMA((nb * N_CHIPS,)), pltpu.SemaphoreType.DMA((nb * N_CHIPS,))],
    )(*bufs)


def _scatter_steps(t_refs, o_refs, send_sems, recv_sems):
    x, y, c, chips = _place()
    me = 2 * x + y

    def slot(ref, chip):
        return lambda r0, n: ref.at[chip, pl.ds(r0, n), :]

    def sends(begin):
        return [_remote_parts(slot(t_ref, 2 * cx + cy), slot(o_ref, me), t_ref.shape[1], send_sems.at[3 * b + j],
                              recv_sems.at[3 * b + j], (cx, cy, c), begin)
                for b, (t_ref, o_ref) in enumerate(zip(t_refs, o_refs)) for j, (cx, cy) in enumerate(chips)]

    def start():
        sends("only")

    def finish():
        for b, o_ref in enumerate(o_refs):
            for j, (cx, cy) in enumerate(chips):
                landed = o_ref.at[2 * cx + cy]
                pltpu.make_async_remote_copy(src_ref=landed, dst_ref=landed, send_sem=send_sems.at[3 * b + j],
                                             recv_sem=recv_sems.at[3 * b + j], device_id=(x, y, c),
                                             device_id_type=MESH).wait_recv()
        for cp in sends(False):
            cp.wait_send()

    return start, finish


def _scatter_chips(bufs):
    nb = len(bufs)

    def body(*refs):
        for step in _scatter_steps(refs[:nb], refs[nb:2 * nb], *refs[2 * nb:]):
            step()

    return pl.pallas_call(
        body, name="scatter_chips", in_specs=[ANY] * nb, out_specs=[ANY] * nb,
        out_shape=[jax.ShapeDtypeStruct(b.shape, b.dtype) for b in bufs],
        scratch_shapes=[pltpu.SemaphoreType.DMA((3 * nb,)), pltpu.SemaphoreType.DMA((3 * nb,))],
    )(*bufs)


def _give_sibling(bufs):
    nb = len(bufs)

    def body(*refs):
        h_refs, o_refs, (send_sems, recv_sems) = refs[:nb], refs[nb:2 * nb], refs[2 * nb:]
        x, y, c, _ = _place()
        gives = [_remote_parts(lambda r0, n, h_ref=h_ref: h_ref.at[pl.ds(r0, n), :],
                               lambda r0, n, o_ref=o_ref: o_ref.at[pl.ds(r0, n), :],
                               h_ref.shape[0], send_sems.at[b], recv_sems.at[b], (x, y, 1 - c))
                 for b, (h_ref, o_ref) in enumerate(zip(h_refs, o_refs))]
        for give in gives:
            give.wait()

    return pl.pallas_call(
        body, name="give_sibling", in_specs=[ANY] * nb, out_specs=[ANY] * nb,
        out_shape=[jax.ShapeDtypeStruct(b.shape, b.dtype) for b in bufs],
        scratch_shapes=[pltpu.SemaphoreType.DMA((nb,)), pltpu.SemaphoreType.DMA((nb,))],
    )(*bufs)


def _pack(arrays, dtype, row_align):
    flat = []
    for arr in arrays:
        v = arr.reshape(-1)
        flat.append(jnp.pad(v, (0, (-v.shape[0]) % LANES)))
    total = sum(f.shape[0] for f in flat) // LANES
    pad_rows = (-total) % row_align
    if pad_rows:
        flat.append(jnp.zeros((pad_rows * LANES,), dtype))
    return jnp.concatenate(flat).reshape(-1, LANES)


def _unpack(buf, shapes, rows_align=1):
    lead = buf.shape[:-2]
    flat = buf.reshape(lead + (-1,))
    out, off = [], 0
    for shape in shapes:
        size = 1
        for dim in shape:
            size *= dim
        out.append(flat[..., off:off + size].reshape(lead + tuple(shape)))
        off += size + (-size) % (LANES * rows_align)
    return out


def _from_chips(parts, axis):
    return jnp.concatenate([parts[j] for j in range(N_CHIPS)], axis=axis)


def kernel(x, ln_g, ln_b, attn_w_in, attn_b_f, attn_w_out, rnn_w_in, rnn_conv_w, rnn_conv_b, rnn_w_a, rnn_b_a, rnn_w_i, rnn_b_i, rnn_lambda, rnn_w_out, loss_target, m_ln_g, m_ln_b, m_attn_w_in, m_attn_b_f, m_attn_w_out, m_rnn_w_in, m_rnn_conv_w, m_rnn_conv_b, m_rnn_w_a, m_rnn_b_a, m_rnn_w_i, m_rnn_b_i, m_rnn_lambda, m_rnn_w_out, v_ln_g, v_ln_b, v_attn_w_in, v_attn_b_f, v_attn_w_out, v_rnn_w_in, v_rnn_conv_w, v_rnn_conv_b, v_rnn_w_a, v_rnn_b_a, v_rnn_w_i, v_rnn_b_i, v_rnn_lambda, v_rnn_w_out):
    s_len, d = x.shape[1], x.shape[2]
    xs = x.reshape(s_len, d)
    target = loss_target.reshape(s_len, d)
    heads = attn_b_f.shape[1]
    qkvg = attn_w_in.shape[2] * N_CHIPS - heads

    me = 2 * lax.axis_index("x") + lax.axis_index("y")
    core = lax.axis_index("c").astype(jnp.int32)
    small = [rnn_conv_w, rnn_conv_b, rnn_b_a, rnn_b_i, rnn_lambda]
    a_in, a_out = attn_w_in.astype(BF16), attn_w_out.astype(BF16)
    later = [a_in[1], a_out] + [w.astype(BF16) for w in (rnn_w_in, rnn_w_a, rnn_w_i, rnn_w_out)]
    got_in, got_small = _gather_chips([a_in[0], _pack(small, F32, 16)])
    conv_w, conv_b, b_a, b_i, lam = [
        _from_chips(p, ax) for p, ax in zip(_unpack(got_small, [w.shape for w in small]), (2, 1, 1, 1, 1))]

    def attn_in_weights(w_in):
        full = jnp.concatenate([w_in[j] for j in range(N_CHIPS)], axis=1)
        return jnp.concatenate([full[:, :d] * (HEAD_DIM ** -0.5), full[:, d:qkvg],
                                jnp.pad(full[:, qkvg:], ((0, 0), (0, 128 - heads)))], axis=1).astype(BF16)

    w_cat = [attn_in_weights(got_in), None]
    b_f = jnp.pad(attn_b_f, ((0, 0), (0, 128 - heads)))

    saved = []
    cur = xs
    for layer in range(DEPTH):
        idx = layer // 2
        g_l, b_l = ln_g[layer:layer + 1], ln_b[layer:layer + 1]
        if layer % 2 == 0:
            q, k, v, gate, fl = _proj(cur, w_cat[idx], [(0, d, BF16), (d, d, BF16), (2 * d, d, BF16),
                                                         (3 * d, d, F32), (4 * d, 128, F32)], "attn_proj")
            cum, sneg = _fcum(fl, b_f[idx:idx + 1])
            o, lse, *rest = _flash_fwd(q, k, v, cum, later if layer == 0 else ())
            if layer == 0:
                w_cat[1] = attn_in_weights(rest[0])
                w_out_a = jnp.moveaxis(rest[1], 0, 1).reshape(2, d, d)
                w_in_r = jnp.moveaxis(rest[2], 0, 2).reshape(2, d, 2 * d)
                w_a = jnp.transpose(rest[3], (1, 2, 0, 3, 4)).reshape(2, -1, RNN_BLOCK, RNN_BLOCK)
                w_i = jnp.transpose(rest[4], (1, 2, 0, 3, 4)).reshape(2, -1, RNN_BLOCK, RNN_BLOCK)
                w_out_r = jnp.moveaxis(rest[5], 0, 1).reshape(2, d, d)
            nxt, xh, rs, yg = _out_ln(o, gate, cur, w_out_a[idx], g_l, b_l, "out_ln")
            saved.append(dict(x=cur, q=q, k=k, v=v, gate=gate, cum=cum, sneg=sneg, a=o, lse=lse, xh=xh, rs=rs, yg=yg))
        else:
            u, gate = _proj(cur, w_in_r[idx], [(0, d, F32), (d, d, F32)], "rnn_proj")
            vecs = [t[idx:idx + 1] for t in (conv_b, b_a, b_i, lam)]
            hseq, uc, r, ig, a = _rnn_fwd(u, conv_w[idx], vecs[0], w_a[idx], vecs[1], w_i[idx], vecs[2], vecs[3])
            nxt, xh, rs, yg = _out_ln(hseq, gate, cur, w_out_r[idx], g_l, b_l, "out_ln")
            saved.append(dict(x=cur, u=u, gate=gate, uc=uc, r=r, ig=ig, av=a, a=hseq, xh=xh, rs=rs, yg=yg, lam=vecs[3]))
        cur = nxt

    dout, sq = _loss_grad(cur, target)
    loss = lax.psum(0.5 * jnp.sum(sq) / d, ("x", "y", "c"))

    g_ln_g, g_ln_b = [None] * DEPTH, [None] * DEPTH
    g_attn = [None] * 2
    g_rnn = [None] * 2
    slots = None
    core1 = core.reshape(1)
    late_half = (SLOT_ROWS - LATE_ROWS) // 2

    def by_chip(t, axis):
        shape = t.shape[:axis] + (N_CHIPS, t.shape[axis] // N_CHIPS) + t.shape[axis + 1:]
        flat = jnp.moveaxis(t.reshape(shape), axis, 0).reshape(N_CHIPS, -1)
        return jnp.pad(flat, ((0, 0), (0, (-flat.shape[1]) % (8 * LANES)))).reshape(N_CHIPS, -1, LANES)

    def both(key):
        return jnp.stack([it[key] for it in g_rnn])

    for layer in reversed(range(DEPTH)):
        idx = layer // 2
        sv = saved[layer]
        if layer == 0:
            gates = jnp.concatenate([by_chip(both("w_a"), 2), by_chip(both("w_i"), 2)], axis=1)
            slots = lax.dynamic_update_slice(slots, gates, (0, ROWS_GATES, 0))
            theirs_late, = _swap_halves([slots], [(LATE_ROWS, late_half)])
            pair_late = _pair_sum(core1, slots, theirs_late, LATE_ROWS, BF16)
        w_out = w_out_a[idx] if layer % 2 == 0 else w_out_r[idx]
        dz, da, dgate, dg, db = _ln_bwd(dout, sv["xh"], sv["rs"], ln_g[layer:layer + 1], w_out, sv["gate"], sv["a"],
                                        "ln_bwd")
        g_ln_g[layer], g_ln_b[layer] = dg, db
        slots = _dw_out(slots, sv["yg"], dz, (ROWS_ATTN_OUT if layer % 2 == 0 else ROWS_RNN_OUT)[idx])
        if layer % 2 == 0:
            dq, dk, dv, dcum, *arrived = _flash_bwd(sv["q"], sv["k"], sv["v"], sv["a"], da, sv["lse"], sv["cum"],
                                                    [pair_late] if layer == 0 else ())
            dlogit, dbf = _fbwd(dcum, sv["sneg"])
            pieces = [dq, dk, dv, dgate, dlogit]
            dout = _mm_nt(dz, [(p, j * d) for j, p in enumerate(pieces)], w_cat[idx], "attn_dx")
            slots = _dw_attn_in(slots, sv["x"], pieces[:4], idx)
            g_attn[idx] = dict(w_f=_mm_tn(sv["x"], dlogit, "dw_f")[:, :heads], b_f=dbf[:, 0])
        else:
            duc, d_wa, d_wi, d_ba, d_bi, d_lam = _rnn_bwd(da, sv["av"], sv["a"], sv["r"], sv["ig"], sv["uc"], sv["lam"],
                                                          w_a[idx], w_i[idx])
            du, d_cw, d_cb = _conv_bwd(duc, sv["u"], conv_w[idx])
            dout = _mm_nt(dz, [(du, 0), (dgate, d)], w_in_r[idx], "rnn_dx")
            slots = _dw_rnn_in(slots, sv["x"], (du, dgate), idx)
            g_rnn[idx] = dict(conv_w=d_cw, conv_b=d_cb[0], w_a=d_wa, b_a=d_ba[0], w_i=d_wi, b_i=d_bi[0], lam=d_lam[0])
    grad_x = dout.reshape(x.shape)

    def everywhere(t):
        flat = t.reshape(-1)
        flat = jnp.pad(flat, (0, (-flat.shape[0]) % (8 * LANES))).reshape(-1, LANES)
        return jnp.broadcast_to(flat[None], (N_CHIPS,) + flat.shape)

    in_rows = jnp.stack([slots[1:, r:r + d, :heads] for r in ROWS_ATTN_IN], axis=1)
    edges = jnp.concatenate([in_rows, jnp.stack([it["w_f"] for it in g_attn])[None]])
    rows = [everywhere(edges), by_chip(both("conv_w"), 2),
            by_chip(both("conv_b"), 1), by_chip(both("b_a"), 1), by_chip(both("b_i"), 1), by_chip(both("lam"), 1),
            everywhere(jnp.concatenate(g_ln_g)), everywhere(jnp.concatenate(g_ln_b)),
            everywhere(jnp.stack([it["b_f"] for it in g_attn]))]
    used = sum(r.shape[1] for r in rows)
    small = jnp.concatenate(rows + [jnp.zeros((N_CHIPS, (-used) % 32, LANES), F32)], axis=1)

    spans = [(0, LATE_ROWS // 2), (0, small.shape[1] // 2)]
    theirs = _swap_halves([slots, small], spans)
    pair = [_pair_sum(core1, slots, theirs[0], 0, BF16), _pair_sum(core1, small, theirs[1], 0, F32)]
    summed = []
    for mine_all, got in zip([pair_late] + pair, list(arrived) + list(_scatter_chips(pair))):
        own = lax.dynamic_index_in_dim(mine_all, me, 0, keepdims=False)
        summed.append(_sum_chips(lax.dynamic_update_index_in_dim(got, own, me, 0)))
    late, early, small_sum = [
        jnp.concatenate([jnp.where(core == 0, mine, other), jnp.where(core == 0, other, mine)])
        for mine, other in zip(summed, _give_sibling(summed))]

    def rows_at(first, n):
        return early[first:first + n] if first < LATE_ROWS else late[first - LATE_ROWS:first - LATE_ROWS + n]

    g_in_group = jnp.stack([rows_at(r, d) for r in ROWS_ATTN_IN])
    g_w_out_a = jnp.stack([rows_at(r, d // N_CHIPS) for r in ROWS_ATTN_OUT])
    g_w_out_r = jnp.stack([rows_at(r, d // N_CHIPS) for r in ROWS_RNN_OUT])
    folded = jnp.stack([rows_at(r, d // 2) for r in ROWS_RNN_IN])
    g_w_in_r = jnp.concatenate([folded[:, :, :d // 2], folded[:, :, d // 2:]], axis=1)
    gate_rows = rnn_w_a.size // LANES
    g_w_a = rows_at(ROWS_GATES, gate_rows).reshape(rnn_w_a.shape)
    g_w_i = rows_at(ROWS_GATES + gate_rows, gate_rows).reshape(rnn_w_i.shape)
    (g_edges, g_conv_w, g_conv_b, g_b_a, g_b_i, g_lam, g_ln_g_sum, g_ln_b_sum,
     g_b_f) = _unpack(small_sum, [
        (N_CHIPS, 2, d, heads), rnn_conv_w.shape, rnn_conv_b.shape, rnn_b_a.shape,
        rnn_b_i.shape, rnn_lambda.shape, ln_g.shape, ln_b.shape, attn_b_f.shape], rows_align=8)
    wide = jnp.concatenate([g_in_group, lax.dynamic_index_in_dim(g_edges, me, 0, keepdims=False)], axis=2)
    g_w_in_a = lax.dynamic_slice(wide, (0, 0, (heads // N_CHIPS) * me), attn_w_in.shape)

    def adam_nd(w, g, m, v, view, rows_per_block):
        outs = _adamw(w.reshape(view), g.reshape(view), m.reshape(view), v.reshape(view), 1, rows_per_block)
        return [t.reshape(w.shape) for t in outs]

    turned = [jnp.transpose(t, (2, 0, 1)) for t in (attn_w_in, g_w_in_a, m_attn_w_in, v_attn_w_in)]
    upd = {
        "attn_w_in": [jnp.transpose(t, (1, 2, 0)) for t in _adamw(*turned, attn_w_in.shape[2] // N_CHIPS, 2)],
        "attn_w_out": adam_nd(attn_w_out, g_w_out_a, m_attn_w_out, v_attn_w_out, attn_w_out.shape, 256),
        "rnn_w_in": adam_nd(rnn_w_in, g_w_in_r, m_rnn_w_in, v_rnn_w_in, rnn_w_in.shape, 512),
        "rnn_w_a": adam_nd(rnn_w_a, g_w_a, m_rnn_w_a, v_rnn_w_a, (1, -1, RNN_BLOCK), 512),
        "rnn_w_i": adam_nd(rnn_w_i, g_w_i, m_rnn_w_i, v_rnn_w_i, (1, -1, RNN_BLOCK), 512),
        "rnn_w_out": adam_nd(rnn_w_out, g_w_out_r, m_rnn_w_out, v_rnn_w_out, rnn_w_out.shape, 256),
    }
    smalls = [rnn_conv_w, rnn_conv_b, rnn_b_a, rnn_b_i, rnn_lambda, ln_g, ln_b, attn_b_f]
    g_small = [g_conv_w, g_conv_b, g_b_a, g_b_i, g_lam, g_ln_g_sum, g_ln_b_sum, g_b_f]
    m_small = [m_rnn_conv_w, m_rnn_conv_b, m_rnn_b_a, m_rnn_b_i, m_rnn_lambda, m_ln_g, m_ln_b, m_attn_b_f]
    v_small = [v_rnn_conv_w, v_rnn_conv_b, v_rnn_b_a, v_rnn_b_i, v_rnn_lambda, v_ln_g, v_ln_b, v_attn_b_f]
    packs = [_pack(t, F32, 16)[None] for t in (smalls, g_small, m_small, v_small)]
    small_out = [_unpack(t[0], [w.shape for w in smalls]) for t in _adamw(*packs, 1, 16)]
    for k, name in enumerate(("rnn_conv_w", "rnn_conv_b", "rnn_b_a", "rnn_b_i", "rnn_lambda", "ln_g", "ln_b",
                              "attn_b_f")):
        upd[name] = [small_out[0][k], small_out[1][k], small_out[2][k]]
    grad = {"attn_w_in": g_w_in_a, "attn_w_out": g_w_out_a, "rnn_w_in": g_w_in_r, "rnn_w_a": g_w_a, "rnn_w_i": g_w_i,
            "rnn_w_out": g_w_out_r, "rnn_conv_w": g_conv_w, "rnn_conv_b": g_conv_b, "rnn_b_a": g_b_a,
            "rnn_b_i": g_b_i, "rnn_lambda": g_lam, "ln_g": g_ln_g_sum, "ln_b": g_ln_b_sum, "attn_b_f": g_b_f}
    names = ("ln_g", "ln_b", "attn_w_in", "attn_b_f", "attn_w_out", "rnn_w_in", "rnn_conv_w", "rnn_conv_b",
             "rnn_w_a", "rnn_b_a", "rnn_w_i", "rnn_b_i", "rnn_lambda", "rnn_w_out")
    return (loss, grad_x, *[grad[n] for n in names], *[upd[n][0] for n in names], *[upd[n][1] for n in names],
            *[upd[n][2] for n in names])
```

```python
import functools

import jax
import jax.numpy as jnp
from jax import lax
from jax.experimental import pallas as pl
from jax.experimental.pallas import tpu as pltpu

F32 = jnp.float32
BF16 = jnp.bfloat16
MESH = pl.DeviceIdType.MESH

DEPTH = 4
HEAD_DIM = 64
HEAD_PAIR = 2 * HEAD_DIM
RNN_BLOCK = 256
CONV_WIDTH = 4
LRU_C = 8.0
ALPHA = (2.0 * DEPTH) ** 0.25
LN_EPS = 1e-5
ADAM_LR, ADAM_B1, ADAM_B2, ADAM_EPS, ADAM_WD, ADAM_STEP = 0.001, 0.9, 0.999, 1e-08, 0.01, 10
N_CHIPS = 4
LANES = 1024
NEG = -1e30

NT_DIMS = (((1,), (1,)), ((), ()))
TN_DIMS = (((0,), (0,)), ((), ()))


def _sigmoid(z):
    return 1.0 / (1.0 + jnp.exp(-z))


def _softplus(z):
    return jnp.maximum(z, 0.0) + jnp.log(1.0 + jnp.exp(-jnp.abs(z)))


def _neg_expm1(y):
    poly = y * (1.0 + y * (0.5 + y * (1.0 / 6.0 + y * (1.0 / 24.0))))
    return -jnp.where(y > -0.05, poly, jnp.exp(y) - 1.0)


def _shift_down(cur, prev8, k):
    n = cur.shape[0]
    row = lax.broadcasted_iota(jnp.int32, cur.shape, 0)
    fix = jnp.tile(pltpu.roll(prev8, k, 0), (n // 8, 1))
    return jnp.where(row < k, fix, pltpu.roll(cur, k, 0))


def _shift_up(cur, next8, k):
    n = cur.shape[0]
    row = lax.broadcasted_iota(jnp.int32, cur.shape, 0)
    fix = jnp.tile(pltpu.roll(next8, 8 - k, 0), (n // 8, 1))
    return jnp.where(row >= n - k, fix, pltpu.roll(cur, n - k, 0))


def _proj(x, w, outs, name):
    s_len, d = x.shape
    tm = min(512, s_len)

    def body(x_ref, w_ref, *o_refs):
        xb = x_ref[...].astype(BF16)
        for (c0, wd, dt), o_ref in zip(outs, o_refs):
            step = min(wd, 512)
            for cc in range(0, wd, step):
                o_ref[:, cc:cc + step] = jnp.dot(
                    xb, w_ref[:, c0 + cc:c0 + cc + step], preferred_element_type=F32).astype(dt)

    return pl.pallas_call(
        body, name=name, grid=(s_len // tm,),
        in_specs=[pl.BlockSpec((tm, d), lambda i: (i, 0)), pl.BlockSpec(w.shape, lambda i: (0, 0))],
        out_specs=[pl.BlockSpec((tm, wd), lambda i: (i, 0)) for _, wd, _ in outs],
        out_shape=[jax.ShapeDtypeStruct((s_len, wd), dt) for _, wd, dt in outs],
        compiler_params=pltpu.CompilerParams(dimension_semantics=("parallel",)),
    )(x, w)


def _mm_nt(dz, pieces, w, name):
    s_len, d = dz.shape
    tm = min(256, s_len)
    n = len(pieces)
    cols = [(c0, p.shape[1]) for p, c0 in pieces]

    def body(dz_ref, *rest):
        w_ref, o_ref = rest[n], rest[n + 1]
        acc = ALPHA * dz_ref[...]
        for (c0, wd), p_ref in zip(cols, rest[:n]):
            acc = acc + lax.dot_general(p_ref[...], w_ref[:, c0:c0 + wd], NT_DIMS, preferred_element_type=F32)
        o_ref[...] = acc

    return pl.pallas_call(
        body, name=name, grid=(s_len // tm,),
        in_specs=[pl.BlockSpec((tm, d), lambda i: (i, 0))]
        + [pl.BlockSpec((tm, wd), lambda i: (i, 0)) for _, wd in cols]
        + [pl.BlockSpec(w.shape, lambda i: (0, 0))],
        out_specs=pl.BlockSpec((tm, d), lambda i: (i, 0)),
        out_shape=jax.ShapeDtypeStruct((s_len, d), F32),
        compiler_params=pltpu.CompilerParams(dimension_semantics=("parallel",)),
    )(dz, *[p for p, _ in pieces], w)


def _mm_tn(a, b, name):
    s_len, m = a.shape
    n = b.shape[1]
    tn = min(n, 512)
    ts = min(s_len, 512)

    def body(a_ref, b_ref, o_ref):
        @pl.when(pl.program_id(1) == 0)
        def _():
            o_ref[...] = jnp.zeros_like(o_ref)

        o_ref[...] += lax.dot_general(a_ref[...].astype(BF16), b_ref[...].astype(BF16), TN_DIMS,
                                      preferred_element_type=F32)

    return pl.pallas_call(
        body, name=name, grid=(n // tn, s_len // ts),
        in_specs=[pl.BlockSpec((ts, m), lambda j, s: (s, 0)), pl.BlockSpec((ts, tn), lambda j, s: (s, j))],
        out_specs=pl.BlockSpec((m, tn), lambda j, s: (0, j)),
        out_shape=jax.ShapeDtypeStruct((m, n), F32),
        compiler_params=pltpu.CompilerParams(dimension_semantics=("parallel", "arbitrary")),
    )(a, b)


ROWS_ATTN_IN = (0, 2048)
ROWS_ATTN_OUT = (1024, 1280)
ROWS_RNN_OUT = (1536, 1792)
ROWS_RNN_IN = (3072, 3584)
ROWS_GATES = 4096
LATE_ROWS = 1280
SLOT_ROWS = 4352


DW_ROWS = 1024


def _dw_call(body, name, slots, operands, specs, out_block, out_index, grid, semantics):
    return pl.pallas_call(
        body, name=name, grid=grid,
        in_specs=specs if slots is None else [ANY] + specs,
        out_specs=pl.BlockSpec(out_block, out_index),
        out_shape=jax.ShapeDtypeStruct((N_CHIPS, SLOT_ROWS, LANES), F32),
        input_output_aliases={} if slots is None else {0: 0},
        compiler_params=pltpu.CompilerParams(dimension_semantics=semantics),
    )(*(operands if slots is None else [slots] + operands))


def _dw_attn_in(slots, x, pieces, layer):
    s_len, d = x.shape
    ts = min(s_len, DW_ROWS)
    steps = s_len // ts
    half = d // 2

    def body(*refs):
        x_ref, p_refs, o_ref = refs[-6], refs[-5:-1], refs[-1]

        @pl.when(pl.program_id(1) == 0)
        def _():
            o_ref[...] = jnp.zeros_like(o_ref)

        xb = x_ref[...].astype(BF16)
        for j, p_ref in enumerate(p_refs):
            o_ref[j] += lax.dot_general(xb, p_ref[...], TN_DIMS, preferred_element_type=F32)

        @pl.when(pl.program_id(1) == steps - 1)
        def _():
            o_ref[0] = o_ref[0] * (HEAD_DIM ** -0.5)

    specs = [pl.BlockSpec((ts, d), lambda i, s: (s, 0))] + [pl.BlockSpec((ts, half), lambda i, s: (s, i))] * 4
    return _dw_call(body, "dw_attn_in", slots, [x] + list(pieces), specs, (N_CHIPS, d, half),
                    lambda i, s: (0, ROWS_ATTN_IN[layer] // d, i), (2, steps), ("parallel", "arbitrary"))


def _dw_out(slots, yg, dz, first_row):
    s_len, d = dz.shape
    ts = min(s_len, DW_ROWS)
    rows = d // N_CHIPS

    def body(*refs):
        a_ref, b_ref, o_ref = refs[-3:]

        @pl.when(pl.program_id(0) == 0)
        def _():
            o_ref[...] = jnp.zeros_like(o_ref)

        prod = lax.dot_general(a_ref[...], b_ref[...].astype(BF16), TN_DIMS, preferred_element_type=F32)
        o_ref[...] += prod.reshape(N_CHIPS, rows, d)

    specs = [pl.BlockSpec((ts, d), lambda s: (s, 0))] * 2
    return _dw_call(body, "dw_out", slots, [yg, dz], specs, (N_CHIPS, rows, d), lambda s: (0, first_row // rows, 0),
                    (s_len // ts,), ("arbitrary",))


def _dw_rnn_in(slots, x, parts, layer):
    s_len, d = x.shape
    ts = min(s_len, DW_ROWS)
    half = d // 2

    def body(*refs):
        x_ref, p_refs, o_ref = refs[-4], refs[-3:-1], refs[-1]

        @pl.when(pl.program_id(0) == 0)
        def _():
            o_ref[...] = jnp.zeros_like(o_ref)

        xb = x_ref[...].astype(BF16)
        for p, p_ref in enumerate(p_refs):
            for jj in (0, 1):
                for r in (0, 1):
                    o_ref[2 * p + jj, :, r * half:(r + 1) * half] += lax.dot_general(
                        xb[:, r * half:(r + 1) * half], p_ref[:, jj * half:(jj + 1) * half], TN_DIMS,
                        preferred_element_type=F32)

    specs = [pl.BlockSpec((ts, d), lambda s: (s, 0))] * 3
    return _dw_call(body, "dw_rnn_in", slots, [x] + list(parts), specs, (N_CHIPS, half, d),
                    lambda s: (0, ROWS_RNN_IN[layer] // half, 0), (s_len // ts,), ("arbitrary",))


def _fcum(fl, bias):
    s_len = fl.shape[0]

    def body(fl_ref, b_ref, cum_ref, sg_ref):
        z = fl_ref[...] + b_ref[...]
        e = jnp.exp(-jnp.abs(z))
        logf = jnp.minimum(z, 0.0) - jnp.log(1.0 + e)
        sneg = jnp.where(z >= 0, e, 1.0) / (1.0 + e)
        run = logf.T[0:16, :]
        sg_ref[...] = sneg.T[0:16, :]
        lane = lax.broadcasted_iota(jnp.int32, (16, s_len), 1)
        sh = 1
        while sh < s_len:
            run = run + jnp.where(lane >= sh, pltpu.roll(run, sh, 1), 0.0)
            sh *= 2
        cum_ref[...] = run

    return pl.pallas_call(
        body, name="fcum",
        out_shape=[jax.ShapeDtypeStruct((16, s_len), F32), jax.ShapeDtypeStruct((16, s_len), F32)],
    )(fl, bias)


def _fbwd(dcum, sneg):
    s_len = dcum.shape[1]

    def body(dc_ref, sg_ref, dl_ref, db_ref):
        run = dc_ref[...]
        lane = lax.broadcasted_iota(jnp.int32, (16, s_len), 1)
        sh = 1
        while sh < s_len:
            run = run + jnp.where(lane < s_len - sh, pltpu.roll(run, s_len - sh, 1), 0.0)
            sh *= 2
        dlog = run * sg_ref[...]
        db_ref[...] = jnp.broadcast_to(jnp.sum(dlog, axis=1, keepdims=True), (16, 128))
        full = jnp.concatenate([dlog, jnp.zeros((112, s_len), F32)], axis=0)
        dl_ref[...] = full.T.astype(BF16)

    return pl.pallas_call(
        body, name="fbwd",
        out_shape=[jax.ShapeDtypeStruct((s_len, 128), BF16), jax.ShapeDtypeStruct((16, 128), F32)],
    )(dcum, sneg)


FWD_TILE = 1024
BWD_TILE = 512


def _flash_fwd(q, k, v, cum, shards=()):
    s_len, width = q.shape
    tile = min(FWD_TILE, s_len // 2)
    nt = s_len // tile
    reps = tile // 128
    cumr = cum.reshape(-1, tile)
    ns = len(shards)
    pairs = width // HEAD_PAIR

    def body(*refs):
        q_ref, k_ref, v_ref, cum_ref = refs[:4]
        o_ref, lse_ref = refs[4 + ns:6 + ns]
        q_t, v_t, cum_col = refs[6 + 2 * ns:9 + 2 * ns]
        pid = pl.program_id(0)
        if ns:
            start, forward, finish = _gather_steps(refs[4:4 + ns], refs[6 + ns:6 + 2 * ns], *refs[9 + 2 * ns:])
            pl.when(pid == 0)(start)
            pl.when(pid == pairs // 2)(forward)
        top = lax.broadcasted_iota(jnp.int32, (HEAD_PAIR, tile), 0) < HEAD_DIM
        causal = (lax.broadcasted_iota(jnp.int32, (tile, tile), 0)
                  <= lax.broadcasted_iota(jnp.int32, (tile, tile), 1))

        def pre(i, carry):
            r0 = pl.multiple_of(i * tile, tile)
            q_t[i] = q_ref[pl.ds(r0, tile), :].astype(F32).T.astype(BF16)
            v_t[i] = v_ref[pl.ds(r0, tile), :].astype(F32).T.astype(BF16)
            for h in (0, 1):
                row = cum_ref[pl.ds((2 * pid + h) * nt + i, 1), :]
                cum_col[h, pl.ds(r0, tile), :] = jnp.broadcast_to(row, (HEAD_PAIR, tile)).T
            return carry

        lax.fori_loop(0, nt, pre, 0)

        def q_loop(qi, carry):
            qt = q_t[qi]
            zt = jnp.zeros_like(qt)
            qms = (jnp.where(top, qt, zt), jnp.where(top, zt, qt))

            def step(kj, state, masked):
                m0, l0, m1, l1, acc = state
                k0 = pl.multiple_of(kj * tile, tile)
                kt = k_ref[pl.ds(k0, tile), :]
                vt = v_t[kj]
                ms, ls, scales, contrib = [m0, m1], [l0, l1], [], None
                for h in (0, 1):
                    s = jnp.dot(kt, qms[h], preferred_element_type=F32)
                    s = s - jnp.tile(cum_col[h, pl.ds(k0, tile), :], (1, reps))
                    if masked:
                        s = jnp.where(causal, s, NEG)
                    m_new = jnp.maximum(ms[h], jnp.max(s, axis=0, keepdims=True))
                    p = jnp.exp(s - m_new)
                    scale = jnp.exp(ms[h] - m_new)
                    ls[h] = scale * ls[h] + jnp.sum(p, axis=0, keepdims=True)
                    ms[h] = m_new
                    scales.append(scale)
                    vm = jnp.where(top, vt, zt) if h == 0 else jnp.where(top, zt, vt)
                    part = jnp.dot(vm, p.astype(BF16), preferred_element_type=F32)
                    contrib = part if contrib is None else contrib + part
                acc = jnp.where(top, scales[0], scales[1]) * acc + contrib
                return ms[0], ls[0], ms[1], ls[1], acc

            low = jnp.full((1, tile), NEG, F32)
            zero = jnp.zeros((1, tile), F32)
            state = step(qi, (low, zero, low, zero, jnp.zeros((HEAD_PAIR, tile), F32)), True)
            m0, l0, m1, l1, acc = lax.fori_loop(0, qi, lambda kj, c: step(kj, c, False), state)
            q0 = pl.multiple_of(qi * tile, tile)
            o_ref[pl.ds(q0, tile), :] = (acc / jnp.where(top, l0, l1)).T
            lse_ref[pl.ds((2 * pid) * nt + qi, 1), :] = m0 + jnp.log(l0)
            lse_ref[pl.ds((2 * pid + 1) * nt + qi, 1), :] = m1 + jnp.log(l1)
            return carry

        lax.fori_loop(0, nt, q_loop, 0)
        if ns:
            pl.when(pid == pairs - 1)(finish)

    blk = pl.BlockSpec((s_len, HEAD_PAIR), lambda p: (0, p))
    full = pl.BlockSpec(cumr.shape, lambda p: (0, 0))
    sems = [pltpu.SemaphoreType.DMA((GATHER_SEMS * ns,))] * 2 if ns else []
    o, lse, *gathered = pl.pallas_call(
        body, name="flash_fwd_gather" if ns else "flash_fwd", grid=(pairs,),
        in_specs=[blk, blk, blk, full] + [ANY] * ns,
        out_specs=[blk, full] + [ANY] * ns,
        out_shape=[jax.ShapeDtypeStruct((s_len, width), F32), jax.ShapeDtypeStruct(cumr.shape, F32)]
        + [jax.ShapeDtypeStruct((N_CHIPS,) + s.shape, s.dtype) for s in shards],
        scratch_shapes=[pltpu.VMEM((nt, HEAD_PAIR, tile), BF16), pltpu.VMEM((nt, HEAD_PAIR, tile), BF16),
                        pltpu.VMEM((2, s_len, HEAD_PAIR), F32)] + sems,
        compiler_params=pltpu.CompilerParams(dimension_semantics=("arbitrary",)),
    )(q, k, v, cumr, *shards)
    return (o, lse.reshape(cum.shape), *gathered)


def _flash_bwd(q, k, v, o, do, lse, cum, outgoing=()):
    s_len, width = q.shape
    tile = min(BWD_TILE, s_len // 2)
    nt = s_len // tile
    reps = tile // 128
    cumr = cum.reshape(-1, tile)
    lse = lse.reshape(-1, tile)
    ns = len(outgoing)
    pairs = width // HEAD_PAIR

    def body(*refs):
        q_ref, k_ref, v_ref, o_ref, do_ref, lse_ref, cum_ref = refs[:7]
        dq_ref, dk_ref, dv_ref, dcum_ref = refs[7 + ns:11 + ns]
        (q_t, do_t, k_t, do_bf, cum_col, dq_t_acc, delta, q_side, dk_acc, dv_acc,
         k_side) = refs[11 + 2 * ns:22 + 2 * ns]
        pid = pl.program_id(0)
        if ns:
            start, finish = _scatter_steps(refs[7:7 + ns], refs[11 + ns:11 + 2 * ns], *refs[22 + 2 * ns:])
            pl.when(pid == 0)(start)
        top = lax.broadcasted_iota(jnp.int32, (HEAD_PAIR, tile), 0) < HEAD_DIM
        left = lax.broadcasted_iota(jnp.int32, (tile, HEAD_PAIR), 1) < HEAD_DIM
        causal = (lax.broadcasted_iota(jnp.int32, (tile, tile), 0)
                  <= lax.broadcasted_iota(jnp.int32, (tile, tile), 1))

        def pre(i, carry):
            r0 = pl.multiple_of(i * tile, tile)
            d_o = do_ref[pl.ds(r0, tile), :]
            prod_t = (d_o * o_ref[pl.ds(r0, tile), :]).T
            delta[pl.ds(i, 1), :] = jnp.sum(prod_t[:HEAD_DIM], axis=0, keepdims=True)
            delta[pl.ds(nt + i, 1), :] = jnp.sum(prod_t[HEAD_DIM:], axis=0, keepdims=True)
            do_bf[pl.ds(r0, tile), :] = d_o.astype(BF16)
            do_t[i] = d_o.T.astype(BF16)
            q_t[i] = q_ref[pl.ds(r0, tile), :].astype(F32).T.astype(BF16)
            k_t[i] = k_ref[pl.ds(r0, tile), :].astype(F32).T.astype(BF16)
            dq_t_acc[i] = jnp.zeros((HEAD_PAIR, tile), F32)
            for h in (0, 1):
                row = cum_ref[pl.ds((2 * pid + h) * nt + i, 1), :]
                cum_col[h, pl.ds(r0, tile), :] = jnp.broadcast_to(row, (HEAD_PAIR, tile)).T
                q_side[pl.ds(h * nt + i, 1), :] = jnp.zeros((1, tile), F32)
            return carry

        lax.fori_loop(0, nt, pre, 0)

        def kv_loop(kj, carry):
            k0 = pl.multiple_of(kj * tile, tile)
            kt = k_ref[pl.ds(k0, tile), :]
            vt = v_ref[pl.ds(k0, tile), :]
            ktt = k_t[kj]
            zt = jnp.zeros_like(ktt)
            zr = jnp.zeros_like(kt)
            kms = (jnp.where(top, ktt, zt), jnp.where(top, zt, ktt))
            cols = [jnp.tile(cum_col[h, pl.ds(k0, tile), :], (1, reps)) for h in (0, 1)]
            dk_acc[...] = jnp.zeros_like(dk_acc)
            dv_acc[...] = jnp.zeros_like(dv_acc)
            k_side[...] = jnp.zeros_like(k_side)

            def step(qi, carry, masked):
                q0 = pl.multiple_of(qi * tile, tile)
                qtt = q_t[qi]
                dtt = do_t[qi]
                q_rows = q_ref[pl.ds(q0, tile), :]
                do_rows = do_bf[pl.ds(q0, tile), :]
                dq_part = None
                for h in (0, 1):
                    q_m = jnp.where(top, qtt, zt) if h == 0 else jnp.where(top, zt, qtt)
                    do_m = jnp.where(top, dtt, zt) if h == 0 else jnp.where(top, zt, dtt)
                    s = jnp.dot(kt, q_m, preferred_element_type=F32) - cols[h]
                    if masked:
                        s = jnp.where(causal, s, NEG)
                    p = jnp.exp(s - lse_ref[pl.ds((2 * pid + h) * nt + qi, 1), :])
                    dp = jnp.dot(vt, do_m, preferred_element_type=F32)
                    ds = p * (dp - delta[pl.ds(h * nt + qi, 1), :])
                    q_side[pl.ds(h * nt + qi, 1), :] += jnp.sum(ds, axis=0, keepdims=True)
                    folded = ds[:, :128]
                    for b in range(1, reps):
                        folded = folded + ds[:, b * 128:(b + 1) * 128]
                    k_side[h] += folded
                    pb = p.astype(BF16)
                    dsb = ds.astype(BF16)
                    do_h = jnp.where(left, do_rows, zr) if h == 0 else jnp.where(left, zr, do_rows)
                    q_h = jnp.where(left, q_rows, zr) if h == 0 else jnp.where(left, zr, q_rows)
                    dv_acc[...] += jnp.dot(pb, do_h, preferred_element_type=F32)
                    dk_acc[...] += jnp.dot(dsb, q_h, preferred_element_type=F32)
                    part = jnp.dot(kms[h], dsb, preferred_element_type=F32)
                    dq_part = part if dq_part is None else dq_part + part
                dq_t_acc[qi] += dq_part
                return carry

            step(kj, 0, True)
            lax.fori_loop(kj + 1, nt, lambda qi, c: step(qi, c, False), 0)
            dk_ref[pl.ds(k0, tile), :] = dk_acc[...].astype(BF16)
            dv_ref[pl.ds(k0, tile), :] = dv_acc[...].astype(BF16)
            for h in (0, 1):
                dcum_ref[pl.ds((2 * pid + h) * nt + kj, 1), :] = -jnp.sum(k_side[h].T, axis=0, keepdims=True)
            return carry

        lax.fori_loop(0, nt, kv_loop, 0)

        def fin(i, carry):
            r0 = pl.multiple_of(i * tile, tile)
            dq_ref[pl.ds(r0, tile), :] = dq_t_acc[i].T.astype(BF16)
            for h in (0, 1):
                dcum_ref[pl.ds((2 * pid + h) * nt + i, 1), :] += q_side[pl.ds(h * nt + i, 1), :]
            return carry

        lax.fori_loop(0, nt, fin, 0)
        if ns:
            pl.when(pid == pairs - 1)(finish)

    blk = pl.BlockSpec((s_len, HEAD_PAIR), lambda p: (0, p))
    full = pl.BlockSpec(cumr.shape, lambda p: (0, 0))
    transposed = pltpu.VMEM((nt, HEAD_PAIR, tile), BF16)
    sems = [pltpu.SemaphoreType.DMA((3 * ns,))] * 2 if ns else []
    dq, dk, dv, dcum, *arrived = pl.pallas_call(
        body, name="flash_bwd_scatter" if ns else "flash_bwd", grid=(pairs,),
        in_specs=[blk, blk, blk, blk, blk, full, full] + [ANY] * ns,
        out_specs=[blk, blk, blk, full] + [ANY] * ns,
        out_shape=[jax.ShapeDtypeStruct((s_len, width), BF16)] * 3 + [jax.ShapeDtypeStruct(cumr.shape, F32)]
        + [jax.ShapeDtypeStruct(t.shape, t.dtype) for t in outgoing],
        scratch_shapes=[transposed, transposed, transposed, pltpu.VMEM((s_len, HEAD_PAIR), BF16),
                        pltpu.VMEM((2, s_len, HEAD_PAIR), F32), pltpu.VMEM((nt, HEAD_PAIR, tile), F32),
                        pltpu.VMEM((2 * nt, tile), F32), pltpu.VMEM((2 * nt, tile), F32),
                        pltpu.VMEM((tile, HEAD_PAIR), F32), pltpu.VMEM((tile, HEAD_PAIR), F32),
                        pltpu.VMEM((2, tile, HEAD_PAIR), F32)] + sems,
        compiler_params=pltpu.CompilerParams(dimension_semantics=("arbitrary",)),
    )(q, k, v, o, do, lse, cumr, *outgoing)
    return (dq, dk, dv, dcum.reshape(cum.shape), *arrived)


def _out_ln(a, gate, x, w, g, b, name):
    s_len, d = x.shape
    tm = min(256, s_len)

    def body(a_ref, gate_ref, x_ref, w_ref, g_ref, b_ref, xn_ref, xh_ref, rs_ref, yg_ref):
        gt = gate_ref[...]
        yg = (a_ref[...] * (gt * _sigmoid(gt))).astype(BF16)
        yg_ref[...] = yg
        z = ALPHA * x_ref[...] + jnp.dot(yg, w_ref[...], preferred_element_type=F32)
        zc = z - jnp.mean(z, axis=1, keepdims=True)
        rstd = lax.rsqrt(jnp.mean(zc * zc, axis=1, keepdims=True) + LN_EPS)
        xh = zc * rstd
        xh_ref[...] = xh
        xn_ref[...] = xh * g_ref[...] + b_ref[...]
        rs_ref[...] = jnp.broadcast_to(rstd, (tm, 128))

    row = pl.BlockSpec((tm, d), lambda i: (i, 0))
    vec = pl.BlockSpec((1, d), lambda i: (0, 0))
    return pl.pallas_call(
        body, name=name, grid=(s_len // tm,),
        in_specs=[row, row, row, pl.BlockSpec(w.shape, lambda i: (0, 0)), vec, vec],
        out_specs=[row, row, pl.BlockSpec((tm, 128), lambda i: (i, 0)), row],
        out_shape=[jax.ShapeDtypeStruct((s_len, d), F32), jax.ShapeDtypeStruct((s_len, d), F32),
                   jax.ShapeDtypeStruct((s_len, 128), F32), jax.ShapeDtypeStruct((s_len, d), BF16)],
        compiler_params=pltpu.CompilerParams(dimension_semantics=("parallel",)),
    )(a, gate, x, w, g, b)


def _ln_bwd(dout, xh, rs, g, w, gate, a, name):
    s_len, d = dout.shape
    tm = min(256, s_len)

    def body(do_ref, xh_ref, rs_ref, g_ref, w_ref, gate_ref, a_ref, dz_ref, da_ref, dgate_ref, dg_ref, db_ref):
        @pl.when(pl.program_id(0) == 0)
        def _():
            dg_ref[...] = jnp.zeros_like(dg_ref)
            db_ref[...] = jnp.zeros_like(db_ref)

        dout_t = do_ref[...]
        xh_t = xh_ref[...]
        dg_ref[...] += jnp.sum(dout_t * xh_t, axis=0, keepdims=True)
        db_ref[...] += jnp.sum(dout_t, axis=0, keepdims=True)
        dxh = dout_t * g_ref[...]
        m1 = jnp.mean(dxh, axis=1, keepdims=True)
        m2 = jnp.mean(dxh * xh_t, axis=1, keepdims=True)
        dz = rs_ref[:, 0:1] * (dxh - m1 - xh_t * m2)
        dz_ref[...] = dz
        dyg = lax.dot_general(dz.astype(BF16), w_ref[...], NT_DIMS, preferred_element_type=F32)
        gt = gate_ref[...]
        sg = _sigmoid(gt)
        da_ref[...] = dyg * (gt * sg)
        dgate_ref[...] = (dyg * a_ref[...] * (sg * (1.0 + gt * (1.0 - sg)))).astype(BF16)

    row = pl.BlockSpec((tm, d), lambda i: (i, 0))
    vec = pl.BlockSpec((1, d), lambda i: (0, 0))
    return pl.pallas_call(
        body, name=name, grid=(s_len // tm,),
        in_specs=[row, row, pl.BlockSpec((tm, 128), lambda i: (i, 0)), vec, pl.BlockSpec(w.shape, lambda i: (0, 0)),
                  row, row],
        out_specs=[row, row, row, vec, vec],
        out_shape=[jax.ShapeDtypeStruct((s_len, d), F32), jax.ShapeDtypeStruct((s_len, d), F32),
                   jax.ShapeDtypeStruct((s_len, d), BF16), jax.ShapeDtypeStruct((1, d), F32),
                   jax.ShapeDtypeStruct((1, d), F32)],
        compiler_params=pltpu.CompilerParams(dimension_semantics=("arbitrary",)),
    )(dout, xh, rs, g, w, gate, a)


def _loss_grad(y, target):
    s_len, d = y.shape
    tm = min(512, s_len)

    def body(y_ref, t_ref, dy_ref, acc_ref):
        @pl.when(pl.program_id(0) == 0)
        def _():
            acc_ref[...] = jnp.zeros_like(acc_ref)

        diff = y_ref[...] - t_ref[...]
        dy_ref[...] = diff * (1.0 / d)
        acc_ref[...] += jnp.sum(diff * diff, axis=0, keepdims=True)

    row = pl.BlockSpec((tm, d), lambda i: (i, 0))
    return pl.pallas_call(
        body, name="loss_grad", grid=(s_len // tm,),
        in_specs=[row, row], out_specs=[row, pl.BlockSpec((1, d), lambda i: (0, 0))],
        out_shape=[jax.ShapeDtypeStruct((s_len, d), F32), jax.ShapeDtypeStruct((1, d), F32)],
        compiler_params=pltpu.CompilerParams(dimension_semantics=("arbitrary",)),
    )(y, target)


def _rnn_fwd(u, conv_w, conv_b, wa, ba, wi, bi, lam):
    s_len, width = u.shape
    tm = min(256, s_len)
    nb = width // RNN_BLOCK

    def body(u_ref, up_ref, cw_ref, cb_ref, wa_ref, ba_ref, wi_ref, bi_ref, lam_ref,
             h_ref, uc_ref, r_ref, i_ref, a_ref, b_scr, h_carry):
        step_id = pl.program_id(0)

        @pl.when(step_id == 0)
        def _():
            h_carry[...] = jnp.zeros_like(h_carry)

        for n in range(nb):
            blk = slice(n * RNN_BLOCK, (n + 1) * RNN_BLOCK)
            ut = u_ref[:, blk]
            prev = jnp.where(step_id > 0, up_ref[:, blk], 0.0)
            uc = cb_ref[:, blk] + cw_ref[3:4, blk] * ut
            for k in (1, 2, 3):
                uc = uc + cw_ref[3 - k:4 - k, blk] * _shift_down(ut, prev, k)
            ucb = uc.astype(BF16)
            r = _sigmoid(jnp.dot(ucb, wa_ref[n], preferred_element_type=F32) + ba_ref[:, blk])
            ig = _sigmoid(jnp.dot(ucb, wi_ref[n], preferred_element_type=F32) + bi_ref[:, blk])
            log_a = -LRU_C * r * _softplus(-lam_ref[:, blk])
            uc_ref[:, blk] = uc
            r_ref[:, blk] = r
            i_ref[:, blk] = ig
            a_ref[:, blk] = jnp.exp(log_a)
            b_scr[:, blk] = jnp.sqrt(_neg_expm1(2.0 * log_a)) * (ig * uc)

        def scan(t, h):
            h = a_ref[pl.ds(t, 1), :] * h + b_scr[pl.ds(t, 1), :]
            h_ref[pl.ds(t, 1), :] = h
            return h

        h_carry[...] = lax.fori_loop(0, tm, scan, h_carry[...], unroll=8)

    row = pl.BlockSpec((tm, width), lambda i: (i, 0))
    prev8 = pl.BlockSpec((8, width), lambda i: (jnp.maximum(i * (tm // 8) - 1, 0), 0))
    vec = pl.BlockSpec((1, width), lambda i: (0, 0))
    mat = pl.BlockSpec(wa.shape, lambda i: (0, 0, 0))
    return pl.pallas_call(
        body, name="rnn_fwd", grid=(s_len // tm,),
        in_specs=[row, prev8, pl.BlockSpec(conv_w.shape, lambda i: (0, 0)), vec, mat, vec, mat, vec, vec],
        out_specs=[row] * 5,
        out_shape=[jax.ShapeDtypeStruct((s_len, width), F32)] * 5,
        scratch_shapes=[pltpu.VMEM((tm, width), F32), pltpu.VMEM((1, width), F32)],
        compiler_params=pltpu.CompilerParams(dimension_semantics=("arbitrary",)),
    )(u, u, conv_w, conv_b, wa, ba, wi, bi, lam)


def _rnn_bwd(dh, a, h, r, ig, uc, lam, wa, wi):
    s_len, width = dh.shape
    tm = min(256, s_len)
    nt = s_len // tm
    nb = width // RNN_BLOCK

    def body(dh_ref, a_ref, h_ref, hp_ref, r_ref, i_ref, uc_ref, lam_ref, wa_ref, wi_ref,
             duc_ref, dwa_ref, dwi_ref, dba_ref, dbi_ref, dlam_ref, g_scr, c_scr, dsp_scr):
        step_id = pl.program_id(0)
        tile_id = nt - 1 - step_id

        @pl.when(step_id == 0)
        def _():
            c_scr[...] = jnp.zeros_like(c_scr)
            dsp_scr[...] = jnp.zeros_like(dsp_scr)
            dwa_ref[...] = jnp.zeros_like(dwa_ref)
            dwi_ref[...] = jnp.zeros_like(dwi_ref)
            dba_ref[...] = jnp.zeros_like(dba_ref)
            dbi_ref[...] = jnp.zeros_like(dbi_ref)

        def scan(j, c):
            t = tm - 1 - j
            g = dh_ref[pl.ds(t, 1), :] + c
            g_scr[pl.ds(t, 1), :] = g
            return a_ref[pl.ds(t, 1), :] * g

        c_scr[...] = lax.fori_loop(0, tm, scan, c_scr[...], unroll=8)

        for n in range(nb):
            blk = slice(n * RNN_BLOCK, (n + 1) * RNN_BLOCK)
            g_t = g_scr[:, blk]
            hp8 = jnp.where(tile_id > 0, hp_ref[:, blk], 0.0)
            h_prev = _shift_down(h_ref[:, blk], hp8, 1)
            av, rv, iv, ucv = a_ref[:, blk], r_ref[:, blk], i_ref[:, blk], uc_ref[:, blk]
            sp = _softplus(-lam_ref[:, blk])
            mag = jnp.sqrt(_neg_expm1(-2.0 * LRU_C * rv * sp))
            d_iu = g_t * mag
            dla = g_t * h_prev * av - g_t * (iv * ucv) * (av * av) / mag
            dsp_scr[:, blk] += jnp.sum(dla * (-LRU_C * rv), axis=0, keepdims=True)
            dra = dla * (-LRU_C * sp) * rv * (1.0 - rv)
            dia = d_iu * ucv * iv * (1.0 - iv)
            drab, diab, ucb = dra.astype(BF16), dia.astype(BF16), ucv.astype(BF16)
            duc_ref[:, blk] = (d_iu * iv
                               + lax.dot_general(drab, wa_ref[n], NT_DIMS, preferred_element_type=F32)
                               + lax.dot_general(diab, wi_ref[n], NT_DIMS, preferred_element_type=F32))
            dwa_ref[n] += lax.dot_general(ucb, drab, TN_DIMS, preferred_element_type=F32)
            dwi_ref[n] += lax.dot_general(ucb, diab, TN_DIMS, preferred_element_type=F32)
            dba_ref[:, blk] += jnp.sum(dra, axis=0, keepdims=True)
            dbi_ref[:, blk] += jnp.sum(dia, axis=0, keepdims=True)

        @pl.when(step_id == nt - 1)
        def _():
            dlam_ref[...] = -dsp_scr[...] * _sigmoid(-lam_ref[...])

    row = pl.BlockSpec((tm, width), lambda i: (nt - 1 - i, 0))
    prev8 = pl.BlockSpec((8, width), lambda i: (jnp.maximum((nt - 1 - i) * (tm // 8) - 1, 0), 0))
    vec = pl.BlockSpec((1, width), lambda i: (0, 0))
    mat = pl.BlockSpec(wa.shape, lambda i: (0, 0, 0))
    return pl.pallas_call(
        body, name="rnn_bwd", grid=(nt,),
        in_specs=[row, row, row, prev8, row, row, row, vec, mat, mat],
        out_specs=[row, mat, mat, vec, vec, vec],
        out_shape=[jax.ShapeDtypeStruct((s_len, width), F32), jax.ShapeDtypeStruct(wa.shape, F32),
                   jax.ShapeDtypeStruct(wa.shape, F32)] + [jax.ShapeDtypeStruct((1, width), F32)] * 3,
        scratch_shapes=[pltpu.VMEM((tm, width), F32), pltpu.VMEM((1, width), F32), pltpu.VMEM((1, width), F32)],
        compiler_params=pltpu.CompilerParams(dimension_semantics=("arbitrary",)),
    )(dh, a, h, h, r, ig, uc, lam, wa, wi)


def _conv_bwd(duc, u, conv_w):
    s_len, width = duc.shape
    tm = min(256, s_len)
    nt = s_len // tm

    def body(d_ref, dn_ref, u_ref, up_ref, cw_ref, du_ref, dcw_ref, dcb_ref):
        step_id = pl.program_id(0)

        @pl.when(step_id == 0)
        def _():
            dcw_ref[...] = jnp.zeros_like(dcw_ref)
            dcb_ref[...] = jnp.zeros_like(dcb_ref)

        for n in range(width // RNN_BLOCK):
            blk = slice(n * RNN_BLOCK, (n + 1) * RNN_BLOCK)
            dt = d_ref[:, blk]
            ut = u_ref[:, blk]
            nxt = jnp.where(step_id < nt - 1, dn_ref[:, blk], 0.0)
            prev = jnp.where(step_id > 0, up_ref[:, blk], 0.0)
            du = cw_ref[3:4, blk] * dt
            dcw_ref[3:4, blk] += jnp.sum(dt * ut, axis=0, keepdims=True)
            for k in (1, 2, 3):
                du = du + cw_ref[3 - k:4 - k, blk] * _shift_up(dt, nxt, k)
                dcw_ref[3 - k:4 - k, blk] += jnp.sum(dt * _shift_down(ut, prev, k), axis=0, keepdims=True)
            du_ref[:, blk] = du.astype(BF16)
            dcb_ref[:, blk] += jnp.sum(dt, axis=0, keepdims=True)

    row = pl.BlockSpec((tm, width), lambda i: (i, 0))
    prev8 = pl.BlockSpec((8, width), lambda i: (jnp.maximum(i * (tm // 8) - 1, 0), 0))
    next8 = pl.BlockSpec((8, width), lambda i: (jnp.minimum((i + 1) * (tm // 8), s_len // 8 - 1), 0))
    return pl.pallas_call(
        body, name="conv_bwd", grid=(nt,),
        in_specs=[row, next8, row, prev8, pl.BlockSpec(conv_w.shape, lambda i: (0, 0))],
        out_specs=[row, pl.BlockSpec(conv_w.shape, lambda i: (0, 0)), pl.BlockSpec((1, width), lambda i: (0, 0))],
        out_shape=[jax.ShapeDtypeStruct((s_len, width), BF16), jax.ShapeDtypeStruct(conv_w.shape, F32),
                   jax.ShapeDtypeStruct((1, width), F32)],
        compiler_params=pltpu.CompilerParams(dimension_semantics=("arbitrary",)),
    )(duc, duc, u, u, conv_w)


def _pair_sum(core, grads, theirs, first_row, out_dtype):
    n, half, _ = theirs.shape
    tb = 128 if half % 128 == 0 and first_row % 128 == 0 else half
    assert first_row % tb == 0 and half % tb == 0

    def body(c_ref, a_ref, b_ref, o_ref):
        o_ref[...] = (a_ref[...] + b_ref[...]).astype(out_dtype)

    blk = pl.BlockSpec((n, tb, LANES), lambda i, c: (0, i, 0))
    mine = pl.BlockSpec((n, tb, LANES), lambda i, c: (0, first_row // tb + c[0] * (half // tb) + i, 0))
    return pl.pallas_call(
        body, name="pair_sum",
        grid_spec=pltpu.PrefetchScalarGridSpec(
            num_scalar_prefetch=1, grid=(half // tb,), in_specs=[mine, blk], out_specs=blk),
        out_shape=jax.ShapeDtypeStruct((n, half, LANES), out_dtype),
        compiler_params=pltpu.CompilerParams(dimension_semantics=("parallel",)),
    )(core, grads, theirs)


def _sum_chips(parts):
    rows = parts.shape[1]
    tb = 128 if rows % 128 == 0 else rows

    def body(p_ref, o_ref):
        o_ref[...] = ((p_ref[0].astype(F32) + p_ref[1].astype(F32)) + p_ref[2].astype(F32)) + p_ref[3].astype(F32)

    return pl.pallas_call(
        body, name="chip_sum", grid=(rows // tb,),
        in_specs=[pl.BlockSpec((N_CHIPS, tb, LANES), lambda i: (0, i, 0))],
        out_specs=pl.BlockSpec((tb, LANES), lambda i: (i, 0)),
        out_shape=jax.ShapeDtypeStruct((rows, LANES), F32),
        compiler_params=pltpu.CompilerParams(dimension_semantics=("parallel",)),
    )(parts)


def _adamw(w, g, m, v, lead_per_block, rows_per_block):
    n, rows, cols = w.shape
    blk = pl.BlockSpec((lead_per_block, rows_per_block, cols), lambda i, j: (i, j, 0))

    def body(w_ref, g_ref, m_ref, v_ref, d_ref, nm_ref, nv_ref):
        gt = g_ref[...]
        nm = ADAM_B1 * m_ref[...] + (1.0 - ADAM_B1) * gt
        nv = ADAM_B2 * v_ref[...] + (1.0 - ADAM_B2) * (gt * gt)
        m_hat = nm / (1.0 - ADAM_B1 ** ADAM_STEP)
        v_hat = nv / (1.0 - ADAM_B2 ** ADAM_STEP)
        d_ref[...] = -ADAM_LR * (m_hat / (jnp.sqrt(v_hat) + ADAM_EPS) + ADAM_WD * w_ref[...])
        nm_ref[...] = nm
        nv_ref[...] = nv

    return pl.pallas_call(
        body, name="adamw", grid=(n // lead_per_block, rows // rows_per_block), in_specs=[blk] * 4,
        out_specs=[blk] * 3,
        out_shape=[jax.ShapeDtypeStruct(w.shape, F32)] * 3,
        compiler_params=pltpu.CompilerParams(dimension_semantics=("parallel", "parallel")),
    )(w, g, m, v)


def _place():
    x, y, c = lax.axis_index("x"), lax.axis_index("y"), lax.axis_index("c")
    return x, y, c, [(1 - x, y), (x, 1 - y), (1 - x, 1 - y)]


ANY = pl.BlockSpec(memory_space=pl.ANY)
COPY_PARTS = 4


def _remote_parts(src_of, dst_of, rows, send_sem, recv_sem, to, begin=True):
    parts = COPY_PARTS if rows % (16 * COPY_PARTS) == 0 else 1
    step = rows // parts
    for i in range(parts if begin is not False else 0):
        pltpu.make_async_remote_copy(src_ref=src_of(i * step, step), dst_ref=dst_of(i * step, step),
                                     send_sem=send_sem, recv_sem=recv_sem, device_id=to, device_id_type=MESH).start()
    if begin == "only":
        return None
    return pltpu.make_async_remote_copy(src_ref=src_of(0, rows), dst_ref=dst_of(0, rows), send_sem=send_sem,
                                        recv_sem=recv_sem, device_id=to, device_id_type=MESH)


GATHER_SEMS = 7


def _gather_steps(p_refs, o_refs, send_sems, recv_sems):
    x, y, c, chips = _place()
    me = 2 * x + y

    def half(ref, which):
        rows = ref.shape[0] // 2
        return ref.at[pl.ds(which * rows, rows)]

    def copy(k, src, dst, to):
        return pltpu.make_async_remote_copy(src_ref=src, dst_ref=dst, send_sem=send_sems.at[k],
                                            recv_sem=recv_sems.at[k], device_id=to, device_id_type=MESH)

    def first_copies():
        out = []
        for b, (p_ref, o_ref) in enumerate(zip(p_refs, o_refs)):
            for j, (cx, cy) in enumerate(chips):
                out.append(copy(GATHER_SEMS * b + j, half(p_ref, c), half(o_ref.at[me], c), (cx, cy, c)))
            out.append(copy(GATHER_SEMS * b + 6, p_ref, o_ref.at[me], (x, y, 1 - c)))
        return out

    def passed_copies():
        out = []
        for b, o_ref in enumerate(o_refs):
            for j, (cx, cy) in enumerate(chips):
                landed = half(o_ref.at[2 * cx + cy], c)
                out.append(copy(GATHER_SEMS * b + 3 + j, landed, landed, (x, y, 1 - c)))
        return out

    def start():
        for cp in first_copies():
            cp.start()

    def forward():
        for b, o_ref in enumerate(o_refs):
            for j, (cx, cy) in enumerate(chips):
                landed = half(o_ref.at[2 * cx + cy], c)
                copy(GATHER_SEMS * b + j, landed, landed, (x, y, c)).wait_recv()
        for cp in passed_copies():
            cp.start()

    def finish():
        for b, o_ref in enumerate(o_refs):
            for j, (cx, cy) in enumerate(chips):
                other = half(o_ref.at[2 * cx + cy], 1 - c)
                copy(GATHER_SEMS * b + 3 + j, other, other, (x, y, c)).wait_recv()
            copy(GATHER_SEMS * b + 6, o_ref.at[me], o_ref.at[me], (x, y, c)).wait_recv()
        for cp in first_copies() + passed_copies():
            cp.wait_send()

    return start, forward, finish


def _gather_chips(shards):
    nb = len(shards)

    def body(*refs):
        for step in _gather_steps(refs[:nb], refs[nb:2 * nb], *refs[2 * nb:]):
            step()

    return pl.pallas_call(
        body, name="gather_chips", in_specs=[ANY] * nb, out_specs=[ANY] * nb,
        out_shape=[jax.ShapeDtypeStruct((N_CHIPS,) + s.shape, s.dtype) for s in shards],
        scratch_shapes=[pltpu.SemaphoreType.DMA((GATHER_SEMS * nb,)), pltpu.SemaphoreType.DMA((GATHER_SEMS * nb,))],
    )(*shards)


def _swap_halves(bufs, spans):
    nb = len(bufs)

    def body(*refs):
        g_refs, t_refs, (send_sems, recv_sems) = refs[:nb], refs[nb:2 * nb], refs[2 * nb:]
        x, y, c, _ = _place()
        gives = []
        for b, (g_ref, t_ref, (first_row, half)) in enumerate(zip(g_refs, t_refs, spans)):
            for j in range(N_CHIPS):
                k = b * N_CHIPS + j
                gives.append(_remote_parts(
                    lambda r0, m, g_ref=g_ref, j=j, base=first_row + (1 - c) * half: g_ref.at[j, pl.ds(base + r0, m), :],
                    lambda r0, m, t_ref=t_ref, j=j: t_ref.at[j, pl.ds(r0, m), :],
                    half, send_sems.at[k], recv_sems.at[k], (x, y, 1 - c)))
        for give in gives:
            give.wait()

    return pl.pallas_call(
        body, name="swap_halves", in_specs=[ANY] * nb, out_specs=[ANY] * nb,
        out_shape=[jax.ShapeDtypeStruct((b.shape[0], half, b.shape[2]), b.dtype) for b, (_, half) in zip(bufs, spans)],
        scratch_shapes=[pltpu.SemaphoreType.DMA((nb * N_CHIPS,)), pltpu.SemaphoreType.DMA((nb * N_CHIPS,))],
    )(*bufs)


def _scatter_steps(t_refs, o_refs, send_sems, recv_sems):
    x, y, c, chips = _place()
    me = 2 * x + y

    def slot(ref, chip):
        return lambda r0, n: ref.at[chip, pl.ds(r0, n), :]

    def sends(begin):
        return [_remote_parts(slot(t_ref, 2 * cx + cy), slot(o_ref, me), t_ref.shape[1], send_sems.at[3 * b + j],
                              recv_sems.at[3 * b + j], (cx, cy, c), begin)
                for b, (t_ref, o_ref) in enumerate(zip(t_refs, o_refs)) for j, (cx, cy) in enumerate(chips)]

    def start():
        sends("only")

    def finish():
        for b, o_ref in enumerate(o_refs):
            for j, (cx, cy) in enumerate(chips):
                landed = o_ref.at[2 * cx + cy]
                pltpu.make_async_remote_copy(src_ref=landed, dst_ref=landed, send_sem=send_sems.at[3 * b + j],
                                             recv_sem=recv_sems.at[3 * b + j], device_id=(x, y, c),
                                             device_id_type=MESH).wait_recv()
        for cp in sends(False):
            cp.wait_send()

    return start, finish


def _scatter_chips(bufs):
    nb = len(bufs)

    def body(*refs):
        for step in _scatter_steps(refs[:nb], refs[nb:2 * nb], *refs[2 * nb:]):
            step()

    return pl.pallas_call(
        body, name="scatter_chips", in_specs=[ANY] * nb, out_specs=[ANY] * nb,
        out_shape=[jax.ShapeDtypeStruct(b.shape, b.dtype) for b in bufs],
        scratch_shapes=[pltpu.SemaphoreType.DMA((3 * nb,)), pltpu.SemaphoreType.DMA((3 * nb,))],
    )(*bufs)


def _give_sibling(bufs):
    nb = len(bufs)

    def body(*refs):
        h_refs, o_refs, (send_sems, recv_sems) = refs[:nb], refs[nb:2 * nb], refs[2 * nb:]
        x, y, c, _ = _place()
        gives = [_remote_parts(lambda r0, n, h_ref=h_ref: h_ref.at[pl.ds(r0, n), :],
                               lambda r0, n, o_ref=o_ref: o_ref.at[pl.ds(r0, n), :],
                               h_ref.shape[0], send_sems.at[b], recv_sems.at[b], (x, y, 1 - c))
                 for b, (h_ref, o_ref) in enumerate(zip(h_refs, o_refs))]
        for give in gives:
            give.wait()

    return pl.pallas_call(
        body, name="give_sibling", in_specs=[ANY] * nb, out_specs=[ANY] * nb,
        out_shape=[jax.ShapeDtypeStruct(b.shape, b.dtype) for b in bufs],
        scratch_shapes=[pltpu.SemaphoreType.DMA((nb,)), pltpu.SemaphoreType.DMA((nb,))],
    )(*bufs)


def _pack(arrays, dtype, row_align):
    flat = []
    for arr in arrays:
        v = arr.reshape(-1)
        flat.append(jnp.pad(v, (0, (-v.shape[0]) % LANES)))
    total = sum(f.shape[0] for f in flat) // LANES
    pad_rows = (-total) % row_align
    if pad_rows:
        flat.append(jnp.zeros((pad_rows * LANES,), dtype))
    return jnp.concatenate(flat).reshape(-1, LANES)


def _unpack(buf, shapes, rows_align=1):
    lead = buf.shape[:-2]
    flat = buf.reshape(lead + (-1,))
    out, off = [], 0
    for shape in shapes:
        size = 1
        for dim in shape:
            size *= dim
        out.append(flat[..., off:off + size].reshape(lead + tuple(shape)))
        off += size + (-size) % (LANES * rows_align)
    return out


def _from_chips(parts, axis):
    return jnp.concatenate([parts[j] for j in range(N_CHIPS)], axis=axis)


def kernel(x, ln_g, ln_b, attn_w_in, attn_b_f, attn_w_out, rnn_w_in, rnn_conv_w, rnn_conv_b, rnn_w_a, rnn_b_a, rnn_w_i, rnn_b_i, rnn_lambda, rnn_w_out, loss_target, m_ln_g, m_ln_b, m_attn_w_in, m_attn_b_f, m_attn_w_out, m_rnn_w_in, m_rnn_conv_w, m_rnn_conv_b, m_rnn_w_a, m_rnn_b_a, m_rnn_w_i, m_rnn_b_i, m_rnn_lambda, m_rnn_w_out, v_ln_g, v_ln_b, v_attn_w_in, v_attn_b_f, v_attn_w_out, v_rnn_w_in, v_rnn_conv_w, v_rnn_conv_b, v_rnn_w_a, v_rnn_b_a, v_rnn_w_i, v_rnn_b_i, v_rnn_lambda, v_rnn_w_out):
    s_len, d = x.shape[1], x.shape[2]
    xs = x.reshape(s_len, d)
    target = loss_target.reshape(s_len, d)
    heads = attn_b_f.shape[1]
    qkvg = attn_w_in.shape[2] * N_CHIPS - heads

    me = 2 * lax.axis_index("x") + lax.axis_index("y")
    core = lax.axis_index("c").astype(jnp.int32)
    small = [rnn_conv_w, rnn_conv_b, rnn_b_a, rnn_b_i, rnn_lambda]
    a_in, a_out = attn_w_in.astype(BF16), attn_w_out.astype(BF16)
    later = [a_in[1], a_out] + [w.astype(BF16) for w in (rnn_w_in, rnn_w_a, rnn_w_i, rnn_w_out)]
    got_in, got_small = _gather_chips([a_in[0], _pack(small, F32, 16)])
    conv_w, conv_b, b_a, b_i, lam = [
        _from_chips(p, ax) for p, ax in zip(_unpack(got_small, [w.shape for w in small]), (2, 1, 1, 1, 1))]

    def attn_in_weights(w_in):
        full = jnp.concatenate([w_in[j] for j in range(N_CHIPS)], axis=1)
        return jnp.concatenate([full[:, :d] * (HEAD_DIM ** -0.5), full[:, d:qkvg],
                                jnp.pad(full[:, qkvg:], ((0, 0), (0, 128 - heads)))], axis=1).astype(BF16)

    w_cat = [attn_in_weights(got_in), None]
    b_f = jnp.pad(attn_b_f, ((0, 0), (0, 128 - heads)))

    saved = []
    cur = xs
    for layer in range(DEPTH):
        idx = layer // 2
        g_l, b_l = ln_g[layer:layer + 1], ln_b[layer:layer + 1]
        if layer % 2 == 0:
            q, k, v, gate, fl = _proj(cur, w_cat[idx], [(0, d, BF16), (d, d, BF16), (2 * d, d, BF16),
                                                         (3 * d, d, F32), (4 * d, 128, F32)], "attn_proj")
            cum, sneg = _fcum(fl, b_f[idx:idx + 1])
            o, lse, *rest = _flash_fwd(q, k, v, cum, later if layer == 0 else ())
            if layer == 0:
                w_cat[1] = attn_in_weights(rest[0])
                w_out_a = jnp.moveaxis(rest[1], 0, 1).reshape(2, d, d)
                w_in_r = jnp.moveaxis(rest[2], 0, 2).reshape(2, d, 2 * d)
                w_a = jnp.transpose(rest[3], (1, 2, 0, 3, 4)).reshape(2, -1, RNN_BLOCK, RNN_BLOCK)
                w_i = jnp.transpose(rest[4], (1, 2, 0, 3, 4)).reshape(2, -1, RNN_BLOCK, RNN_BLOCK)
                w_out_r = jnp.moveaxis(rest[5], 0, 1).reshape(2, d, d)
            nxt, xh, rs, yg = _out_ln(o, gate, cur, w_out_a[idx], g_l, b_l, "out_ln")
            saved.append(dict(x=cur, q=q, k=k, v=v, gate=gate, cum=cum, sneg=sneg, a=o, lse=lse, xh=xh, rs=rs, yg=yg))
        else:
            u, gate = _proj(cur, w_in_r[idx], [(0, d, F32), (d, d, F32)], "rnn_proj")
            vecs = [t[idx:idx + 1] for t in (conv_b, b_a, b_i, lam)]
            hseq, uc, r, ig, a = _rnn_fwd(u, conv_w[idx], vecs[0], w_a[idx], vecs[1], w_i[idx], vecs[2], vecs[3])
            nxt, xh, rs, yg = _out_ln(hseq, gate, cur, w_out_r[idx], g_l, b_l, "out_ln")
            saved.append(dict(x=cur, u=u, gate=gate, uc=uc, r=r, ig=ig, av=a, a=hseq, xh=xh, rs=rs, yg=yg, lam=vecs[3]))
        cur = nxt

    dout, sq = _loss_grad(cur, target)
    loss = lax.psum(0.5 * jnp.sum(sq) / d, ("x", "y", "c"))

    g_ln_g, g_ln_b = [None] * DEPTH, [None] * DEPTH
    g_attn = [None] * 2
    g_rnn = [None] * 2
    slots = None
    core1 = core.reshape(1)
    late_half = (SLOT_ROWS - LATE_ROWS) // 2

    def by_chip(t, axis):
        shape = t.shape[:axis] + (N_CHIPS, t.shape[axis] // N_CHIPS) + t.shape[axis + 1:]
        flat = jnp.moveaxis(t.reshape(shape), axis, 0).reshape(N_CHIPS, -1)
        return jnp.pad(flat, ((0, 0), (0, (-flat.shape[1]) % (8 * LANES)))).reshape(N_CHIPS, -1, LANES)

    def both(key):
        return jnp.stack([it[key] for it in g_rnn])

    for layer in reversed(range(DEPTH)):
        idx = layer // 2
        sv = saved[layer]
        if layer == 0:
            gates = jnp.concatenate([by_chip(both("w_a"), 2), by_chip(both("w_i"), 2)], axis=1)
            slots = lax.dynamic_update_slice(slots, gates, (0, ROWS_GATES, 0))
            theirs_late, = _swap_halves([slots], [(LATE_ROWS, late_half)])
            pair_late = _pair_sum(core1, slots, theirs_late, LATE_ROWS, BF16)
        w_out = w_out_a[idx] if layer % 2 == 0 else w_out_r[idx]
        dz, da, dgate, dg, db = _ln_bwd(dout, sv["xh"], sv["rs"], ln_g[layer:layer + 1], w_out, sv["gate"], sv["a"],
                                        "ln_bwd")
        g_ln_g[layer], g_ln_b[layer] = dg, db
        slots = _dw_out(slots, sv["yg"], dz, (ROWS_ATTN_OUT if layer % 2 == 0 else ROWS_RNN_OUT)[idx])
        if layer % 2 == 0:
            dq, dk, dv, dcum, *arrived = _flash_bwd(sv["q"], sv["k"], sv["v"], sv["a"], da, sv["lse"], sv["cum"],
                                                    [pair_late] if layer == 0 else ())
            dlogit, dbf = _fbwd(dcum, sv["sneg"])
            pieces = [dq, dk, dv, dgate, dlogit]
            dout = _mm_nt(dz, [(p, j * d) for j, p in enumerate(pieces)], w_cat[idx], "attn_dx")
            slots = _dw_attn_in(slots, sv["x"], pieces[:4], idx)
            g_attn[idx] = dict(w_f=_mm_tn(sv["x"], dlogit, "dw_f")[:, :heads], b_f=dbf[:, 0])
        else:
            duc, d_wa, d_wi, d_ba, d_bi, d_lam = _rnn_bwd(da, sv["av"], sv["a"], sv["r"], sv["ig"], sv["uc"], sv["lam"],
                                                          w_a[idx], w_i[idx])
            du, d_cw, d_cb = _conv_bwd(duc, sv["u"], conv_w[idx])
            dout = _mm_nt(dz, [(du, 0), (dgate, d)], w_in_r[idx], "rnn_dx")
            slots = _dw_rnn_in(slots, sv["x"], (du, dgate), idx)
            g_rnn[idx] = dict(conv_w=d_cw, conv_b=d_cb[0], w_a=d_wa, b_a=d_ba[0], w_i=d_wi, b_i=d_bi[0], lam=d_lam[0])
    grad_x = dout.reshape(x.shape)

    def everywhere(t):
        flat = t.reshape(-1)
        flat = jnp.pad(flat, (0, (-flat.shape[0]) % (8 * LANES))).reshape(-1, LANES)
        return jnp.broadcast_to(flat[None], (N_CHIPS,) + flat.shape)

    in_rows = jnp.stack([slots[1:, r:r + d, :heads] for r in ROWS_ATTN_IN], axis=1)
    edges = jnp.concatenate([in_rows, jnp.stack([it["w_f"] for it in g_attn])[None]])
    rows = [everywhere(edges), by_chip(both("conv_w"), 2),
            by_chip(both("conv_b"), 1), by_chip(both("b_a"), 1), by_chip(both("b_i"), 1), by_chip(both("lam"), 1),
            everywhere(jnp.concatenate(g_ln_g)), everywhere(jnp.concatenate(g_ln_b)),
            everywhere(jnp.stack([it["b_f"] for it in g_attn]))]
    used = sum(r.shape[1] for r in rows)
    small = jnp.concatenate(rows + [jnp.zeros((N_CHIPS, (-used) % 32, LANES), F32)], axis=1)

    spans = [(0, LATE_ROWS // 2), (0, small.shape[1] // 2)]
    theirs = _swap_halves([slots, small], spans)
    pair = [_pair_sum(core1, slots, theirs[0], 0, BF16), _pair_sum(core1, small, theirs[1], 0, F32)]
    summed = []
    for mine_all, got in zip([pair_late] + pair, list(arrived) + list(_scatter_chips(pair))):
        own = lax.dynamic_index_in_dim(mine_all, me, 0, keepdims=False)
        summed.append(_sum_chips(lax.dynamic_update_index_in_dim(got, own, me, 0)))
    late, early, small_sum = [
        jnp.concatenate([jnp.where(core == 0, mine, other), jnp.where(core == 0, other, mine)])
        for mine, other in zip(summed, _give_sibling(summed))]

    def rows_at(first, n):
        return early[first:first + n] if first < LATE_ROWS else late[first - LATE_ROWS:first - LATE_ROWS + n]

    g_in_group = jnp.stack([rows_at(r, d) for r in ROWS_ATTN_IN])
    g_w_out_a = jnp.stack([rows_at(r, d // N_CHIPS) for r in ROWS_ATTN_OUT])
    g_w_out_r = jnp.stack([rows_at(r, d // N_CHIPS) for r in ROWS_RNN_OUT])
    folded = jnp.stack([rows_at(r, d // 2) for r in ROWS_RNN_IN])
    g_w_in_r = jnp.concatenate([folded[:, :, :d // 2], folded[:, :, d // 2:]], axis=1)
    gate_rows = rnn_w_a.size // LANES
    g_w_a = rows_at(ROWS_GATES, gate_rows).reshape(rnn_w_a.shape)
    g_w_i = rows_at(ROWS_GATES + gate_rows, gate_rows).reshape(rnn_w_i.shape)
    (g_edges, g_conv_w, g_conv_b, g_b_a, g_b_i, g_lam, g_ln_g_sum, g_ln_b_sum,
     g_b_f) = _unpack(small_sum, [
        (N_CHIPS, 2, d, heads), rnn_conv_w.shape, rnn_conv_b.shape, rnn_b_a.shape,
        rnn_b_i.shape, rnn_lambda.shape, ln_g.shape, ln_b.shape, attn_b_f.shape], rows_align=8)
    wide = jnp.concatenate([g_in_group, lax.dynamic_index_in_dim(g_edges, me, 0, keepdims=False)], axis=2)
    g_w_in_a = lax.dynamic_slice(wide, (0, 0, (heads // N_CHIPS) * me), attn_w_in.shape)

    def adam_nd(w, g, m, v, view, rows_per_block):
        outs = _adamw(w.reshape(view), g.reshape(view), m.reshape(view), v.reshape(view), 1, rows_per_block)
        return [t.reshape(w.shape) for t in outs]

    turned = [jnp.transpose(t, (2, 0, 1)) for t in (attn_w_in, g_w_in_a, m_attn_w_in, v_attn_w_in)]
    upd = {
        "attn_w_in": [jnp.transpose(t, (1, 2, 0)) for t in _adamw(*turned, attn_w_in.shape[2] // N_CHIPS, 2)],
        "attn_w_out": adam_nd(attn_w_out, g_w_out_a, m_attn_w_out, v_attn_w_out, attn_w_out.shape, 256),
        "rnn_w_in": adam_nd(rnn_w_in, g_w_in_r, m_rnn_w_in, v_rnn_w_in, rnn_w_in.shape, 512),
        "rnn_w_a": adam_nd(rnn_w_a, g_w_a, m_rnn_w_a, v_rnn_w_a, (1, -1, RNN_BLOCK), 512),
        "rnn_w_i": adam_nd(rnn_w_i, g_w_i, m_rnn_w_i, v_rnn_w_i, (1, -1, RNN_BLOCK), 512),
        "rnn_w_out": adam_nd(rnn_w_out, g_w_out_r, m_rnn_w_out, v_rnn_w_out, rnn_w_out.shape, 256),
    }
    smalls = [rnn_conv_w, rnn_conv_b, rnn_b_a, rnn_b_i, rnn_lambda, ln_g, ln_b, attn_b_f]
    g_small = [g_conv_w, g_conv_b, g_b_a, g_b_i, g_lam, g_ln_g_sum, g_ln_b_sum, g_b_f]
    m_small = [m_rnn_conv_w, m_rnn_conv_b, m_rnn_b_a, m_rnn_b_i, m_rnn_lambda, m_ln_g, m_ln_b, m_attn_b_f]
    v_small = [v_rnn_conv_w, v_rnn_conv_b, v_rnn_b_a, v_rnn_b_i, v_rnn_lambda, v_ln_g, v_ln_b, v_attn_b_f]
    packs = [_pack(t, F32, 16)[None] for t in (smalls, g_small, m_small, v_small)]
    small_out = [_unpack(t[0], [w.shape for w in smalls]) for t in _adamw(*packs, 1, 16)]
    for k, name in enumerate(("rnn_conv_w", "rnn_conv_b", "rnn_b_a", "rnn_b_i", "rnn_lambda", "ln_g", "ln_b",
                              "attn_b_f")):
        upd[name] = [small_out[0][k], small_out[1][k], small_out[2][k]]
    grad = {"attn_w_in": g_w_in_a, "attn_w_out": g_w_out_a, "rnn_w_in": g_w_in_r, "rnn_w_a": g_w_a, "rnn_w_i": g_w_i,
            "rnn_w_out": g_w_out_r, "rnn_conv_w": g_conv_w, "rnn_conv_b": g_conv_b, "rnn_b_a": g_b_a,
            "rnn_b_i": g_b_i, "rnn_lambda": g_lam, "ln_g": g_ln_g_sum, "ln_b": g_ln_b_sum, "attn_b_f": g_b_f}
    names = ("ln_g", "ln_b", "attn_w_in", "attn_b_f", "attn_w_out", "rnn_w_in", "rnn_conv_w", "rnn_conv_b",
             "rnn_w_a", "rnn_b_a", "rnn_w_i", "rnn_b_i", "rnn_lambda", "rnn_w_out")
    return (loss, grad_x, *[grad[n] for n in names], *[upd[n][0] for n in names], *[upd[n][1] for n in names],
            *[upd[n][2] for n in names])
```

```python
import functools

import jax
import jax.numpy as jnp
from jax import lax
from jax.experimental import pallas as pl
from jax.experimental.pallas import tpu as pltpu

F32 = jnp.float32
BF16 = jnp.bfloat16
MESH = pl.DeviceIdType.MESH

DEPTH = 4
HEAD_DIM = 64
HEAD_PAIR = 2 * HEAD_DIM
RNN_BLOCK = 256
CONV_WIDTH = 4
LRU_C = 8.0
ALPHA = (2.0 * DEPTH) ** 0.25
LN_EPS = 1e-5
ADAM_LR, ADAM_B1, ADAM_B2, ADAM_EPS, ADAM_WD, ADAM_STEP = 0.001, 0.9, 0.999, 1e-08, 0.01, 10
N_CHIPS = 4
LANES = 1024
NEG = -1e30

NT_DIMS = (((1,), (1,)), ((), ()))
TN_DIMS = (((0,), (0,)), ((), ()))


def _sigmoid(z):
    return 1.0 / (1.0 + jnp.exp(-z))


def _softplus(z):
    return jnp.maximum(z, 0.0) + jnp.log(1.0 + jnp.exp(-jnp.abs(z)))


def _neg_expm1(y):
    poly = y * (1.0 + y * (0.5 + y * (1.0 / 6.0 + y * (1.0 / 24.0))))
    return -jnp.where(y > -0.05, poly, jnp.exp(y) - 1.0)


def _shift_down(cur, prev8, k):
    n = cur.shape[0]
    row = lax.broadcasted_iota(jnp.int32, cur.shape, 0)
    fix = jnp.tile(pltpu.roll(prev8, k, 0), (n // 8, 1))
    return jnp.where(row < k, fix, pltpu.roll(cur, k, 0))


def _shift_up(cur, next8, k):
    n = cur.shape[0]
    row = lax.broadcasted_iota(jnp.int32, cur.shape, 0)
    fix = jnp.tile(pltpu.roll(next8, 8 - k, 0), (n // 8, 1))
    return jnp.where(row >= n - k, fix, pltpu.roll(cur, n - k, 0))


def _proj(x, w, outs, name):
    s_len, d = x.shape
    tm = min(512, s_len)

    def body(x_ref, w_ref, *o_refs):
        xb = x_ref[...].astype(BF16)
        for (c0, wd, dt), o_ref in zip(outs, o_refs):
            step = min(wd, 512)
            for cc in range(0, wd, step):
                o_ref[:, cc:cc + step] = jnp.dot(
                    xb, w_ref[:, c0 + cc:c0 + cc + step], preferred_element_type=F32).astype(dt)

    return pl.pallas_call(
        body, name=name, grid=(s_len // tm,),
        in_specs=[pl.BlockSpec((tm, d), lambda i: (i, 0)), pl.BlockSpec(w.shape, lambda i: (0, 0))],
        out_specs=[pl.BlockSpec((tm, wd), lambda i: (i, 0)) for _, wd, _ in outs],
        out_shape=[jax.ShapeDtypeStruct((s_len, wd), dt) for _, wd, dt in outs],
        compiler_params=pltpu.CompilerParams(dimension_semantics=("parallel",)),
    )(x, w)


def _mm_nt(dz, pieces, w, name):
    s_len, d = dz.shape
    tm = min(256, s_len)
    n = len(pieces)
    cols = [(c0, p.shape[1]) for p, c0 in pieces]

    def body(dz_ref, *rest):
        w_ref, o_ref = rest[n], rest[n + 1]
        acc = ALPHA * dz_ref[...]
        for (c0, wd), p_ref in zip(cols, rest[:n]):
            acc = acc + lax.dot_general(p_ref[...], w_ref[:, c0:c0 + wd], NT_DIMS, preferred_element_type=F32)
        o_ref[...] = acc

    return pl.pallas_call(
        body, name=name, grid=(s_len // tm,),
        in_specs=[pl.BlockSpec((tm, d), lambda i: (i, 0))]
        + [pl.BlockSpec((tm, wd), lambda i: (i, 0)) for _, wd in cols]
        + [pl.BlockSpec(w.shape, lambda i: (0, 0))],
        out_specs=pl.BlockSpec((tm, d), lambda i: (i, 0)),
        out_shape=jax.ShapeDtypeStruct((s_len, d), F32),
        compiler_params=pltpu.CompilerParams(dimension_semantics=("parallel",)),
    )(dz, *[p for p, _ in pieces], w)


def _mm_tn(a, b, name):
    s_len, m = a.shape
    n = b.shape[1]
    tn = min(n, 512)
    ts = min(s_len, 512)

    def body(a_ref, b_ref, o_ref):
        @pl.when(pl.program_id(1) == 0)
        def _():
            o_ref[...] = jnp.zeros_like(o_ref)

        o_ref[...] += lax.dot_general(a_ref[...].astype(BF16), b_ref[...].astype(BF16), TN_DIMS,
                                      preferred_element_type=F32)

    return pl.pallas_call(
        body, name=name, grid=(n // tn, s_len // ts),
        in_specs=[pl.BlockSpec((ts, m), lambda j, s: (s, 0)), pl.BlockSpec((ts, tn), lambda j, s: (s, j))],
        out_specs=pl.BlockSpec((m, tn), lambda j, s: (0, j)),
        out_shape=jax.ShapeDtypeStruct((m, n), F32),
        compiler_params=pltpu.CompilerParams(dimension_semantics=("parallel", "arbitrary")),
    )(a, b)


ROWS_ATTN_IN = (0, 2048)
ROWS_ATTN_OUT = (1024, 1280)
ROWS_RNN_OUT = (1536, 1792)
ROWS_RNN_IN = (3072, 3584)
ROWS_GATES = 4096
LATE_ROWS = 1280
SLOT_ROWS = 4352


DW_ROWS = 1024


def _dw_call(body, name, slots, operands, specs, out_block, out_index, grid, semantics):
    return pl.pallas_call(
        body, name=name, grid=grid,
        in_specs=specs if slots is None else [ANY] + specs,
        out_specs=pl.BlockSpec(out_block, out_index),
        out_shape=jax.ShapeDtypeStruct((N_CHIPS, SLOT_ROWS, LANES), F32),
        input_output_aliases={} if slots is None else {0: 0},
        compiler_params=pltpu.CompilerParams(dimension_semantics=semantics),
    )(*(operands if slots is None else [slots] + operands))


def _dw_attn_in(slots, x, pieces, layer):
    s_len, d = x.shape
    ts = min(s_len, DW_ROWS)
    steps = s_len // ts
    half = d // 2

    def body(*refs):
        x_ref, p_refs, o_ref = refs[-6], refs[-5:-1], refs[-1]

        @pl.when(pl.program_id(1) == 0)
        def _():
            o_ref[...] = jnp.zeros_like(o_ref)

        xb = x_ref[...].astype(BF16)
        for j, p_ref in enumerate(p_refs):
            o_ref[j] += lax.dot_general(xb, p_ref[...], TN_DIMS, preferred_element_type=F32)

        @pl.when(pl.program_id(1) == steps - 1)
        def _():
            o_ref[0] = o_ref[0] * (HEAD_DIM ** -0.5)

    specs = [pl.BlockSpec((ts, d), lambda i, s: (s, 0))] + [pl.BlockSpec((ts, half), lambda i, s: (s, i))] * 4
    return _dw_call(body, "dw_attn_in", slots, [x] + list(pieces), specs, (N_CHIPS, d, half),
                    lambda i, s: (0, ROWS_ATTN_IN[layer] // d, i), (2, steps), ("parallel", "arbitrary"))


def _dw_out(slots, yg, dz, first_row):
    s_len, d = dz.shape
    ts = min(s_len, DW_ROWS)
    rows = d // N_CHIPS

    def body(*refs):
        a_ref, b_ref, o_ref = refs[-3:]

        @pl.when(pl.program_id(0) == 0)
        def _():
            o_ref[...] = jnp.zeros_like(o_ref)

        prod = lax.dot_general(a_ref[...], b_ref[...].astype(BF16), TN_DIMS, preferred_element_type=F32)
        o_ref[...] += prod.reshape(N_CHIPS, rows, d)

    specs = [pl.BlockSpec((ts, d), lambda s: (s, 0))] * 2
    return _dw_call(body, "dw_out", slots, [yg, dz], specs, (N_CHIPS, rows, d), lambda s: (0, first_row // rows, 0),
                    (s_len // ts,), ("arbitrary",))


def _dw_rnn_in(slots, x, parts, layer):
    s_len, d = x.shape
    ts = min(s_len, DW_ROWS)
    half = d // 2

    def body(*refs):
        x_ref, p_refs, o_ref = refs[-4], refs[-3:-1], refs[-1]

        @pl.when(pl.program_id(0) == 0)
        def _():
            o_ref[...] = jnp.zeros_like(o_ref)

        xb = x_ref[...].astype(BF16)
        for p, p_ref in enumerate(p_refs):
            for jj in (0, 1):
                for r in (0, 1):
                    o_ref[2 * p + jj, :, r * half:(r + 1) * half] += lax.dot_general(
                        xb[:, r * half:(r + 1) * half], p_ref[:, jj * half:(jj + 1) * half], TN_DIMS,
                        preferred_element_type=F32)

    specs = [pl.BlockSpec((ts, d), lambda s: (s, 0))] * 3
    return _dw_call(body, "dw_rnn_in", slots, [x] + list(parts), specs, (N_CHIPS, half, d),
                    lambda s: (0, ROWS_RNN_IN[layer] // half, 0), (s_len // ts,), ("arbitrary",))


def _fcum(fl, bias):
    s_len = fl.shape[0]

    def body(fl_ref, b_ref, cum_ref, sg_ref):
        z = fl_ref[...] + b_ref[...]
        e = jnp.exp(-jnp.abs(z))
        logf = jnp.minimum(z, 0.0) - jnp.log(1.0 + e)
        sneg = jnp.where(z >= 0, e, 1.0) / (1.0 + e)
        run = logf.T[0:16, :]
        sg_ref[...] = sneg.T[0:16, :]
        lane = lax.broadcasted_iota(jnp.int32, (16, s_len), 1)
        sh = 1
        while sh < s_len:
            run = run + jnp.where(lane >= sh, pltpu.roll(run, sh, 1), 0.0)
            sh *= 2
        cum_ref[...] = run

    return pl.pallas_call(
        body, name="fcum",
        out_shape=[jax.ShapeDtypeStruct((16, s_len), F32), jax.ShapeDtypeStruct((16, s_len), F32)],
    )(fl, bias)


def _fbwd(dcum, sneg):
    s_len = dcum.shape[1]

    def body(dc_ref, sg_ref, dl_ref, db_ref):
        run = dc_ref[...]
        lane = lax.broadcasted_iota(jnp.int32, (16, s_len), 1)
        sh = 1
        while sh < s_len:
            run = run + jnp.where(lane < s_len - sh, pltpu.roll(run, s_len - sh, 1), 0.0)
            sh *= 2
        dlog = run * sg_ref[...]
        db_ref[...] = jnp.broadcast_to(jnp.sum(dlog, axis=1, keepdims=True), (16, 128))
        full = jnp.concatenate([dlog, jnp.zeros((112, s_len), F32)], axis=0)
        dl_ref[...] = full.T.astype(BF16)

    return pl.pallas_call(
        body, name="fbwd",
        out_shape=[jax.ShapeDtypeStruct((s_len, 128), BF16), jax.ShapeDtypeStruct((16, 128), F32)],
    )(dcum, sneg)


FWD_TILE = 1024
BWD_TILE = 512


def _flash_fwd(q, k, v, cum, shards=()):
    s_len, width = q.shape
    tile = min(FWD_TILE, s_len // 2)
    nt = s_len // tile
    reps = tile // 128
    cumr = cum.reshape(-1, tile)
    ns = len(shards)
    pairs = width // HEAD_PAIR

    def body(*refs):
        q_ref, k_ref, v_ref, cum_ref = refs[:4]
        o_ref, lse_ref = refs[4 + ns:6 + ns]
        q_t, v_t, cum_col = refs[6 + 2 * ns:9 + 2 * ns]
        pid = pl.program_id(0)
        if ns:
            start, forward, finish = _gather_steps(refs[4:4 + ns], refs[6 + ns:6 + 2 * ns], *refs[9 + 2 * ns:])
            pl.when(pid == 0)(start)
            pl.when(pid == pairs // 2)(forward)
        top = lax.broadcasted_iota(jnp.int32, (HEAD_PAIR, tile), 0) < HEAD_DIM
        causal = (lax.broadcasted_iota(jnp.int32, (tile, tile), 0)
                  <= lax.broadcasted_iota(jnp.int32, (tile, tile), 1))

        def pre(i, carry):
            r0 = pl.multiple_of(i * tile, tile)
            q_t[i] = q_ref[pl.ds(r0, tile), :].astype(F32).T.astype(BF16)
            v_t[i] = v_ref[pl.ds(r0, tile), :].astype(F32).T.astype(BF16)
            for h in (0, 1):
                row = cum_ref[pl.ds((2 * pid + h) * nt + i, 1), :]
                cum_col[h, pl.ds(r0, tile), :] = jnp.broadcast_to(row, (HEAD_PAIR, tile)).T
            return carry

        lax.fori_loop(0, nt, pre, 0)

        def q_loop(qi, carry):
            qt = q_t[qi]
            zt = jnp.zeros_like(qt)
            qms = (jnp.where(top, qt, zt), jnp.where(top, zt, qt))

            def step(kj, state, masked):
                m0, l0, m1, l1, acc = state
                k0 = pl.multiple_of(kj * tile, tile)
                kt = k_ref[pl.ds(k0, tile), :]
                vt = v_t[kj]
                ms, ls, scales, contrib = [m0, m1], [l0, l1], [], None
                for h in (0, 1):
                    s = jnp.dot(kt, qms[h], preferred_element_type=F32)
                    s = s - jnp.tile(cum_col[h, pl.ds(k0, tile), :], (1, reps))
                    if masked:
                        s = jnp.where(causal, s, NEG)
                    m_new = jnp.maximum(ms[h], jnp.max(s, axis=0, keepdims=True))
                    p = jnp.exp(s - m_new)
                    scale = jnp.exp(ms[h] - m_new)
                    ls[h] = scale * ls[h] + jnp.sum(p, axis=0, keepdims=True)
                    ms[h] = m_new
                    scales.append(scale)
                    vm = jnp.where(top, vt, zt) if h == 0 else jnp.where(top, zt, vt)
                    part = jnp.dot(vm, p.astype(BF16), preferred_element_type=F32)
                    contrib = part if contrib is None else contrib + part
                acc = jnp.where(top, scales[0], scales[1]) * acc + contrib
                return ms[0], ls[0], ms[1], ls[1], acc

            low = jnp.full((1, tile), NEG, F32)
            zero = jnp.zeros((1, tile), F32)
            state = step(qi, (low, zero, low, zero, jnp.zeros((HEAD_PAIR, tile), F32)), True)
            m0, l0, m1, l1, acc = lax.fori_loop(0, qi, lambda kj, c: step(kj, c, False), state)
            q0 = pl.multiple_of(qi * tile, tile)
            o_ref[pl.ds(q0, tile), :] = (acc / jnp.where(top, l0, l1)).T
            lse_ref[pl.ds((2 * pid) * nt + qi, 1), :] = m0 + jnp.log(l0)
            lse_ref[pl.ds((2 * pid + 1) * nt + qi, 1), :] = m1 + jnp.log(l1)
            return carry

        lax.fori_loop(0, nt, q_loop, 0)
        if ns:
            pl.when(pid == pairs - 1)(finish)

    blk = pl.BlockSpec((s_len, HEAD_PAIR), lambda p: (0, p))
    full = pl.BlockSpec(cumr.shape, lambda p: (0, 0))
    sems = [pltpu.SemaphoreType.DMA((GATHER_SEMS * ns,))] * 2 if ns else []
    o, lse, *gathered = pl.pallas_call(
        body, name="flash_fwd_gather" if ns else "flash_fwd", grid=(pairs,),
        in_specs=[blk, blk, blk, full] + [ANY] * ns,
        out_specs=[blk, full] + [ANY] * ns,
        out_shape=[jax.ShapeDtypeStruct((s_len, width), F32), jax.ShapeDtypeStruct(cumr.shape, F32)]
        + [jax.ShapeDtypeStruct((N_CHIPS,) + s.shape, s.dtype) for s in shards],
        scratch_shapes=[pltpu.VMEM((nt, HEAD_PAIR, tile), BF16), pltpu.VMEM((nt, HEAD_PAIR, tile), BF16),
                        pltpu.VMEM((2, s_len, HEAD_PAIR), F32)] + sems,
        compiler_params=pltpu.CompilerParams(dimension_semantics=("arbitrary",)),
    )(q, k, v, cumr, *shards)
    return (o, lse.reshape(cum.shape), *gathered)


def _flash_bwd(q, k, v, o, do, lse, cum, outgoing=()):
    s_len, width = q.shape
    tile = min(BWD_TILE, s_len // 2)
    nt = s_len // tile
    reps = tile // 128
    cumr = cum.reshape(-1, tile)
    lse = lse.reshape(-1, tile)
    ns = len(outgoing)
    pairs = width // HEAD_PAIR

    def body(*refs):
        q_ref, k_ref, v_ref, o_ref, do_ref, lse_ref, cum_ref = refs[:7]
        dq_ref, dk_ref, dv_ref, dcum_ref = refs[7 + ns:11 + ns]
        (q_t, do_t, k_t, do_bf, cum_col, dq_t_acc, delta, q_side, dk_acc, dv_acc,
         k_side) = refs[11 + 2 * ns:22 + 2 * ns]
        pid = pl.program_id(0)
        if ns:
            start, finish = _scatter_steps(refs[7:7 + ns], refs[11 + ns:11 + 2 * ns], *refs[22 + 2 * ns:])
            pl.when(pid == 0)(start)
        top = lax.broadcasted_iota(jnp.int32, (HEAD_PAIR, tile), 0) < HEAD_DIM
        left = lax.broadcasted_iota(jnp.int32, (tile, HEAD_PAIR), 1) < HEAD_DIM
        causal = (lax.broadcasted_iota(jnp.int32, (tile, tile), 0)
                  <= lax.broadcasted_iota(jnp.int32, (tile, tile), 1))

        def pre(i, carry):
            r0 = pl.multiple_of(i * tile, tile)
            d_o = do_ref[pl.ds(r0, tile), :]
            prod_t = (d_o * o_ref[pl.ds(r0, tile), :]).T
            delta[pl.ds(i, 1), :] = jnp.sum(prod_t[:HEAD_DIM], axis=0, keepdims=True)
            delta[pl.ds(nt + i, 1), :] = jnp.sum(prod_t[HEAD_DIM:], axis=0, keepdims=True)
            do_bf[pl.ds(r0, tile), :] = d_o.astype(BF16)
            do_t[i] = d_o.T.astype(BF16)
            q_t[i] = q_ref[pl.ds(r0, tile), :].astype(F32).T.astype(BF16)
            k_t[i] = k_ref[pl.ds(r0, tile), :].astype(F32).T.astype(BF16)
            dq_t_acc[i] = jnp.zeros((HEAD_PAIR, tile), F32)
            for h in (0, 1):
                row = cum_ref[pl.ds((2 * pid + h) * nt + i, 1), :]
                cum_col[h, pl.ds(r0, tile), :] = jnp.broadcast_to(row, (HEAD_PAIR, tile)).T
                q_side[pl.ds(h * nt + i, 1), :] = jnp.zeros((1, tile), F32)
            return carry

        lax.fori_loop(0, nt, pre, 0)

        def kv_loop(kj, carry):
            k0 = pl.multiple_of(kj * tile, tile)
            kt = k_ref[pl.ds(k0, tile), :]
            vt = v_ref[pl.ds(k0, tile), :]
            ktt = k_t[kj]
            zt = jnp.zeros_like(ktt)
            zr = jnp.zeros_like(kt)
            kms = (jnp.where(top, ktt, zt), jnp.where(top, zt, ktt))
            cols = [jnp.tile(cum_col[h, pl.ds(k0, tile), :], (1, reps)) for h in (0, 1)]
            dk_acc[...] = jnp.zeros_like(dk_acc)
            dv_acc[...] = jnp.zeros_like(dv_acc)
            k_side[...] = jnp.zeros_like(k_side)

            def step(qi, carry, masked):
                q0 = pl.multiple_of(qi * tile, tile)
                qtt = q_t[qi]
                dtt = do_t[qi]
                q_rows = q_ref[pl.ds(q0, tile), :]
                do_rows = do_bf[pl.ds(q0, tile), :]
                dq_part = None
                for h in (0, 1):
                    q_m = jnp.where(top, qtt, zt) if h == 0 else jnp.where(top, zt, qtt)
                    do_m = jnp.where(top, dtt, zt) if h == 0 else jnp.where(top, zt, dtt)
                    s = jnp.dot(kt, q_m, preferred_element_type=F32) - cols[h]
                    if masked:
                        s = jnp.where(causal, s, NEG)
                    p = jnp.exp(s - lse_ref[pl.ds((2 * pid + h) * nt + qi, 1), :])
                    dp = jnp.dot(vt, do_m, preferred_element_type=F32)
                    ds = p * (dp - delta[pl.ds(h * nt + qi, 1), :])
                    q_side[pl.ds(h * nt + qi, 1), :] += jnp.sum(ds, axis=0, keepdims=True)
                    folded = ds[:, :128]
                    for b in range(1, reps):
                        folded = folded + ds[:, b * 128:(b + 1) * 128]
                    k_side[h] += folded
                    pb = p.astype(BF16)
                    dsb = ds.astype(BF16)
                    do_h = jnp.where(left, do_rows, zr) if h == 0 else jnp.where(left, zr, do_rows)
                    q_h = jnp.where(left, q_rows, zr) if h == 0 else jnp.where(left, zr, q_rows)
                    dv_acc[...] += jnp.dot(pb, do_h, preferred_element_type=F32)
                    dk_acc[...] += jnp.dot(dsb, q_h, preferred_element_type=F32)
                    part = jnp.dot(kms[h], dsb, preferred_element_type=F32)
                    dq_part = part if dq_part is None else dq_part + part
                dq_t_acc[qi] += dq_part
                return carry

            step(kj, 0, True)
            lax.fori_loop(kj + 1, nt, lambda qi, c: step(qi, c, False), 0)
            dk_ref[pl.ds(k0, tile), :] = dk_acc[...].astype(BF16)
            dv_ref[pl.ds(k0, tile), :] = dv_acc[...].astype(BF16)
            for h in (0, 1):
                dcum_ref[pl.ds((2 * pid + h) * nt + kj, 1), :] = -jnp.sum(k_side[h].T, axis=0, keepdims=True)
            return carry

        lax.fori_loop(0, nt, kv_loop, 0)

        def fin(i, carry):
            r0 = pl.multiple_of(i * tile, tile)
            dq_ref[pl.ds(r0, tile), :] = dq_t_acc[i].T.astype(BF16)
            for h in (0, 1):
                dcum_ref[pl.ds((2 * pid + h) * nt + i, 1), :] += q_side[pl.ds(h * nt + i, 1), :]
            return carry

        lax.fori_loop(0, nt, fin, 0)
        if ns:
            pl.when(pid == pairs - 1)(finish)

    blk = pl.BlockSpec((s_len, HEAD_PAIR), lambda p: (0, p))
    full = pl.BlockSpec(cumr.shape, lambda p: (0, 0))
    transposed = pltpu.VMEM((nt, HEAD_PAIR, tile), BF16)
    sems = [pltpu.SemaphoreType.DMA((3 * ns,))] * 2 if ns else []
    dq, dk, dv, dcum, *arrived = pl.pallas_call(
        body, name="flash_bwd_scatter" if ns else "flash_bwd", grid=(pairs,),
        in_specs=[blk, blk, blk, blk, blk, full, full] + [ANY] * ns,
        out_specs=[blk, blk, blk, full] + [ANY] * ns,
        out_shape=[jax.ShapeDtypeStruct((s_len, width), BF16)] * 3 + [jax.ShapeDtypeStruct(cumr.shape, F32)]
        + [jax.ShapeDtypeStruct(t.shape, t.dtype) for t in outgoing],
        scratch_shapes=[transposed, transposed, transposed, pltpu.VMEM((s_len, HEAD_PAIR), BF16),
                        pltpu.VMEM((2, s_len, HEAD_PAIR), F32), pltpu.VMEM((nt, HEAD_PAIR, tile), F32),
                        pltpu.VMEM((2 * nt, tile), F32), pltpu.VMEM((2 * nt, tile), F32),
                        pltpu.VMEM((tile, HEAD_PAIR), F32), pltpu.VMEM((tile, HEAD_PAIR), F32),
                        pltpu.VMEM((2, tile, HEAD_PAIR), F32)] + sems,
        compiler_params=pltpu.CompilerParams(dimension_semantics=("arbitrary",)),
    )(q, k, v, o, do, lse, cumr, *outgoing)
    return (dq, dk, dv, dcum.reshape(cum.shape), *arrived)


def _out_ln(a, gate, x, w, g, b, name):
    s_len, d = x.shape
    tm = min(512, s_len)

    def body(a_ref, gate_ref, x_ref, w_ref, g_ref, b_ref, xn_ref, xh_ref, rs_ref, yg_ref):
        gt = gate_ref[...]
        yg = (a_ref[...] * (gt * _sigmoid(gt))).astype(BF16)
        yg_ref[...] = yg
        z = ALPHA * x_ref[...] + jnp.dot(yg, w_ref[...], preferred_element_type=F32)
        zc = z - jnp.mean(z, axis=1, keepdims=True)
        rstd = lax.rsqrt(jnp.mean(zc * zc, axis=1, keepdims=True) + LN_EPS)
        xh = zc * rstd
        xh_ref[...] = xh
        xn_ref[...] = xh * g_ref[...] + b_ref[...]
        rs_ref[...] = jnp.broadcast_to(rstd, (tm, 128))

    row = pl.BlockSpec((tm, d), lambda i: (i, 0))
    vec = pl.BlockSpec((1, d), lambda i: (0, 0))
    return pl.pallas_call(
        body, name=name, grid=(s_len // tm,),
        in_specs=[row, row, row, pl.BlockSpec(w.shape, lambda i: (0, 0)), vec, vec],
        out_specs=[row, row, pl.BlockSpec((tm, 128), lambda i: (i, 0)), row],
        out_shape=[jax.ShapeDtypeStruct((s_len, d), F32), jax.ShapeDtypeStruct((s_len, d), F32),
                   jax.ShapeDtypeStruct((s_len, 128), F32), jax.ShapeDtypeStruct((s_len, d), BF16)],
        compiler_params=pltpu.CompilerParams(dimension_semantics=("parallel",)),
    )(a, gate, x, w, g, b)


def _ln_bwd(dout, xh, rs, g, w, gate, a, name, swap=None):
    s_len, d = dout.shape
    tm = min(512, s_len)
    steps = s_len // tm
    ns = 0 if swap is None else 1

    def body(*refs):
        do_ref, xh_ref, rs_ref, g_ref, w_ref, gate_ref, a_ref = refs[:7]
        dz_ref, da_ref, dgate_ref, dg_ref, db_ref = refs[7 + ns:12 + ns]
        if ns:
            start, finish = _swap_steps(refs[7:8], refs[12 + ns:13 + ns], [swap[1]], *refs[13 + ns:])
            pl.when(pl.program_id(0) == 0)(start)

        @pl.when(pl.program_id(0) == 0)
        def _():
            dg_ref[...] = jnp.zeros_like(dg_ref)
            db_ref[...] = jnp.zeros_like(db_ref)

        dout_t = do_ref[...]
        xh_t = xh_ref[...]
        dg_ref[...] += jnp.sum(dout_t * xh_t, axis=0, keepdims=True)
        db_ref[...] += jnp.sum(dout_t, axis=0, keepdims=True)
        dxh = dout_t * g_ref[...]
        m1 = jnp.mean(dxh, axis=1, keepdims=True)
        m2 = jnp.mean(dxh * xh_t, axis=1, keepdims=True)
        dz = rs_ref[:, 0:1] * (dxh - m1 - xh_t * m2)
        dz_ref[...] = dz
        dyg = lax.dot_general(dz.astype(BF16), w_ref[...], NT_DIMS, preferred_element_type=F32)
        gt = gate_ref[...]
        sg = _sigmoid(gt)
        da_ref[...] = dyg * (gt * sg)
        dgate_ref[...] = (dyg * a_ref[...] * (sg * (1.0 + gt * (1.0 - sg)))).astype(BF16)
        if ns:
            pl.when(pl.program_id(0) == steps - 1)(finish)

    row = pl.BlockSpec((tm, d), lambda i: (i, 0))
    vec = pl.BlockSpec((1, d), lambda i: (0, 0))
    extra_out = [jax.ShapeDtypeStruct((N_CHIPS, swap[1][1], LANES), swap[0].dtype)] if ns else []
    return pl.pallas_call(
        body, name=name + "_swap" if ns else name, grid=(steps,),
        in_specs=[row, row, pl.BlockSpec((tm, 128), lambda i: (i, 0)), vec, pl.BlockSpec(w.shape, lambda i: (0, 0)),
                  row, row] + [ANY] * ns,
        out_specs=[row, row, row, vec, vec] + [ANY] * ns,
        out_shape=[jax.ShapeDtypeStruct((s_len, d), F32), jax.ShapeDtypeStruct((s_len, d), F32),
                   jax.ShapeDtypeStruct((s_len, d), BF16), jax.ShapeDtypeStruct((1, d), F32),
                   jax.ShapeDtypeStruct((1, d), F32)] + extra_out,
        scratch_shapes=[pltpu.SemaphoreType.DMA((N_CHIPS,))] * 2 if ns else [],
        compiler_params=pltpu.CompilerParams(dimension_semantics=("arbitrary",)),
    )(dout, xh, rs, g, w, gate, a, *([swap[0]] if ns else []))


def _loss_grad(y, target):
    s_len, d = y.shape
    tm = min(512, s_len)

    def body(y_ref, t_ref, dy_ref, acc_ref):
        @pl.when(pl.program_id(0) == 0)
        def _():
            acc_ref[...] = jnp.zeros_like(acc_ref)

        diff = y_ref[...] - t_ref[...]
        dy_ref[...] = diff * (1.0 / d)
        acc_ref[...] += jnp.sum(diff * diff, axis=0, keepdims=True)

    row = pl.BlockSpec((tm, d), lambda i: (i, 0))
    return pl.pallas_call(
        body, name="loss_grad", grid=(s_len // tm,),
        in_specs=[row, row], out_specs=[row, pl.BlockSpec((1, d), lambda i: (0, 0))],
        out_shape=[jax.ShapeDtypeStruct((s_len, d), F32), jax.ShapeDtypeStruct((1, d), F32)],
        compiler_params=pltpu.CompilerParams(dimension_semantics=("arbitrary",)),
    )(y, target)


def _rnn_fwd(u, conv_w, conv_b, wa, ba, wi, bi, lam):
    s_len, width = u.shape
    tm = min(256, s_len)
    nb = width // RNN_BLOCK

    def body(u_ref, up_ref, cw_ref, cb_ref, wa_ref, ba_ref, wi_ref, bi_ref, lam_ref,
             h_ref, uc_ref, r_ref, i_ref, a_ref, b_scr, h_carry):
        step_id = pl.program_id(0)

        @pl.when(step_id == 0)
        def _():
            h_carry[...] = jnp.zeros_like(h_carry)

        for n in range(nb):
            blk = slice(n * RNN_BLOCK, (n + 1) * RNN_BLOCK)
            ut = u_ref[:, blk]
            prev = jnp.where(step_id > 0, up_ref[:, blk], 0.0)
            uc = cb_ref[:, blk] + cw_ref[3:4, blk] * ut
            for k in (1, 2, 3):
                uc = uc + cw_ref[3 - k:4 - k, blk] * _shift_down(ut, prev, k)
            ucb = uc.astype(BF16)
            r = _sigmoid(jnp.dot(ucb, wa_ref[n], preferred_element_type=F32) + ba_ref[:, blk])
            ig = _sigmoid(jnp.dot(ucb, wi_ref[n], preferred_element_type=F32) + bi_ref[:, blk])
            log_a = -LRU_C * r * _softplus(-lam_ref[:, blk])
            uc_ref[:, blk] = uc
            r_ref[:, blk] = r
            i_ref[:, blk] = ig
            a_ref[:, blk] = jnp.exp(log_a)
            b_scr[:, blk] = jnp.sqrt(_neg_expm1(2.0 * log_a)) * (ig * uc)

        def scan(t, h):
            h = a_ref[pl.ds(t, 1), :] * h + b_scr[pl.ds(t, 1), :]
            h_ref[pl.ds(t, 1), :] = h
            return h

        h_carry[...] = lax.fori_loop(0, tm, scan, h_carry[...], unroll=8)

    row = pl.BlockSpec((tm, width), lambda i: (i, 0))
    prev8 = pl.BlockSpec((8, width), lambda i: (jnp.maximum(i * (tm // 8) - 1, 0), 0))
    vec = pl.BlockSpec((1, width), lambda i: (0, 0))
    mat = pl.BlockSpec(wa.shape, lambda i: (0, 0, 0))
    return pl.pallas_call(
        body, name="rnn_fwd", grid=(s_len // tm,),
        in_specs=[row, prev8, pl.BlockSpec(conv_w.shape, lambda i: (0, 0)), vec, mat, vec, mat, vec, vec],
        out_specs=[row] * 5,
        out_shape=[jax.ShapeDtypeStruct((s_len, width), F32)] * 5,
        scratch_shapes=[pltpu.VMEM((tm, width), F32), pltpu.VMEM((1, width), F32)],
        compiler_params=pltpu.CompilerParams(dimension_semantics=("arbitrary",)),
    )(u, u, conv_w, conv_b, wa, ba, wi, bi, lam)


def _rnn_bwd(dh, a, h, r, ig, uc, lam, wa, wi):
    s_len, width = dh.shape
    tm = min(256, s_len)
    nt = s_len // tm
    nb = width // RNN_BLOCK

    def body(dh_ref, a_ref, h_ref, hp_ref, r_ref, i_ref, uc_ref, lam_ref, wa_ref, wi_ref,
             duc_ref, dwa_ref, dwi_ref, dba_ref, dbi_ref, dlam_ref, g_scr, c_scr, dsp_scr):
        step_id = pl.program_id(0)
        tile_id = nt - 1 - step_id

        @pl.when(step_id == 0)
        def _():
            c_scr[...] = jnp.zeros_like(c_scr)
            dsp_scr[...] = jnp.zeros_like(dsp_scr)
            dwa_ref[...] = jnp.zeros_like(dwa_ref)
            dwi_ref[...] = jnp.zeros_like(dwi_ref)
            dba_ref[...] = jnp.zeros_like(dba_ref)
            dbi_ref[...] = jnp.zeros_like(dbi_ref)

        def scan(j, c):
            t = tm - 1 - j
            g = dh_ref[pl.ds(t, 1), :] + c
            g_scr[pl.ds(t, 1), :] = g
            return a_ref[pl.ds(t, 1), :] * g

        c_scr[...] = lax.fori_loop(0, tm, scan, c_scr[...], unroll=8)

        for n in range(nb):
            blk = slice(n * RNN_BLOCK, (n + 1) * RNN_BLOCK)
            g_t = g_scr[:, blk]
            hp8 = jnp.where(tile_id > 0, hp_ref[:, blk], 0.0)
            h_prev = _shift_down(h_ref[:, blk], hp8, 1)
            av, rv, iv, ucv = a_ref[:, blk], r_ref[:, blk], i_ref[:, blk], uc_ref[:, blk]
            sp = _softplus(-lam_ref[:, blk])
            mag = jnp.sqrt(_neg_expm1(-2.0 * LRU_C * rv * sp))
            d_iu = g_t * mag
            dla = g_t * h_prev * av - g_t * (iv * ucv) * (av * av) / mag
            dsp_scr[:, blk] += jnp.sum(dla * (-LRU_C * rv), axis=0, keepdims=True)
            dra = dla * (-LRU_C * sp) * rv * (1.0 - rv)
            dia = d_iu * ucv * iv * (1.0 - iv)
            drab, diab, ucb = dra.astype(BF16), dia.astype(BF16), ucv.astype(BF16)
            duc_ref[:, blk] = (d_iu * iv
                               + lax.dot_general(drab, wa_ref[n], NT_DIMS, preferred_element_type=F32)
                               + lax.dot_general(diab, wi_ref[n], NT_DIMS, preferred_element_type=F32))
            dwa_ref[n] += lax.dot_general(ucb, drab, TN_DIMS, preferred_element_type=F32)
            dwi_ref[n] += lax.dot_general(ucb, diab, TN_DIMS, preferred_element_type=F32)
            dba_ref[:, blk] += jnp.sum(dra, axis=0, keepdims=True)
            dbi_ref[:, blk] += jnp.sum(dia, axis=0, keepdims=True)

        @pl.when(step_id == nt - 1)
        def _():
            dlam_ref[...] = -dsp_scr[...] * _sigmoid(-lam_ref[...])

    row = pl.BlockSpec((tm, width), lambda i: (nt - 1 - i, 0))
    prev8 = pl.BlockSpec((8, width), lambda i: (jnp.maximum((nt - 1 - i) * (tm // 8) - 1, 0), 0))
    vec = pl.BlockSpec((1, width), lambda i: (0, 0))
    mat = pl.BlockSpec(wa.shape, lambda i: (0, 0, 0))
    return pl.pallas_call(
        body, name="rnn_bwd", grid=(nt,),
        in_specs=[row, row, row, prev8, row, row, row, vec, mat, mat],
        out_specs=[row, mat, mat, vec, vec, vec],
        out_shape=[jax.ShapeDtypeStruct((s_len, width), F32), jax.ShapeDtypeStruct(wa.shape, F32),
                   jax.ShapeDtypeStruct(wa.shape, F32)] + [jax.ShapeDtypeStruct((1, width), F32)] * 3,
        scratch_shapes=[pltpu.VMEM((tm, width), F32), pltpu.VMEM((1, width), F32), pltpu.VMEM((1, width), F32)],
        compiler_params=pltpu.CompilerParams(dimension_semantics=("arbitrary",)),
    )(dh, a, h, h, r, ig, uc, lam, wa, wi)


def _conv_bwd(duc, u, conv_w):
    s_len, width = duc.shape
    tm = min(256, s_len)
    nt = s_len // tm

    def body(d_ref, dn_ref, u_ref, up_ref, cw_ref, du_ref, dcw_ref, dcb_ref):
        step_id = pl.program_id(0)

        @pl.when(step_id == 0)
        def _():
            dcw_ref[...] = jnp.zeros_like(dcw_ref)
            dcb_ref[...] = jnp.zeros_like(dcb_ref)

        for n in range(width // RNN_BLOCK):
            blk = slice(n * RNN_BLOCK, (n + 1) * RNN_BLOCK)
            dt = d_ref[:, blk]
            ut = u_ref[:, blk]
            nxt = jnp.where(step_id < nt - 1, dn_ref[:, blk], 0.0)
            prev = jnp.where(step_id > 0, up_ref[:, blk], 0.0)
            du = cw_ref[3:4, blk] * dt
            dcw_ref[3:4, blk] += jnp.sum(dt * ut, axis=0, keepdims=True)
            for k in (1, 2, 3):
                du = du + cw_ref[3 - k:4 - k, blk] * _shift_up(dt, nxt, k)
                dcw_ref[3 - k:4 - k, blk] += jnp.sum(dt * _shift_down(ut, prev, k), axis=0, keepdims=True)
            du_ref[:, blk] = du.astype(BF16)
            dcb_ref[:, blk] += jnp.sum(dt, axis=0, keepdims=True)

    row = pl.BlockSpec((tm, width), lambda i: (i, 0))
    prev8 = pl.BlockSpec((8, width), lambda i: (jnp.maximum(i * (tm // 8) - 1, 0), 0))
    next8 = pl.BlockSpec((8, width), lambda i: (jnp.minimum((i + 1) * (tm // 8), s_len // 8 - 1), 0))
    return pl.pallas_call(
        body, name="conv_bwd", grid=(nt,),
        in_specs=[row, next8, row, prev8, pl.BlockSpec(conv_w.shape, lambda i: (0, 0))],
        out_specs=[row, pl.BlockSpec(conv_w.shape, lambda i: (0, 0)), pl.BlockSpec((1, width), lambda i: (0, 0))],
        out_shape=[jax.ShapeDtypeStruct((s_len, width), BF16), jax.ShapeDtypeStruct(conv_w.shape, F32),
                   jax.ShapeDtypeStruct((1, width), F32)],
        compiler_params=pltpu.CompilerParams(dimension_semantics=("arbitrary",)),
    )(duc, duc, u, u, conv_w)


def _pair_sum(core, grads, theirs, first_row, out_dtype):
    n, half, _ = theirs.shape
    tb = 128 if half % 128 == 0 and first_row % 128 == 0 else half
    assert first_row % tb == 0 and half % tb == 0

    def body(c_ref, a_ref, b_ref, o_ref):
        o_ref[...] = (a_ref[...] + b_ref[...]).astype(out_dtype)

    blk = pl.BlockSpec((n, tb, LANES), lambda i, c: (0, i, 0))
    mine = pl.BlockSpec((n, tb, LANES), lambda i, c: (0, first_row // tb + c[0] * (half // tb) + i, 0))
    return pl.pallas_call(
        body, name="pair_sum",
        grid_spec=pltpu.PrefetchScalarGridSpec(
            num_scalar_prefetch=1, grid=(half // tb,), in_specs=[mine, blk], out_specs=blk),
        out_shape=jax.ShapeDtypeStruct((n, half, LANES), out_dtype),
        compiler_params=pltpu.CompilerParams(dimension_semantics=("parallel",)),
    )(core, grads, theirs)


def _sum_chips(parts):
    rows = parts.shape[1]
    tb = 128 if rows % 128 == 0 else rows

    def body(p_ref, o_ref):
        o_ref[...] = ((p_ref[0].astype(F32) + p_ref[1].astype(F32)) + p_ref[2].astype(F32)) + p_ref[3].astype(F32)

    return pl.pallas_call(
        body, name="chip_sum", grid=(rows // tb,),
        in_specs=[pl.BlockSpec((N_CHIPS, tb, LANES), lambda i: (0, i, 0))],
        out_specs=pl.BlockSpec((tb, LANES), lambda i: (i, 0)),
        out_shape=jax.ShapeDtypeStruct((rows, LANES), F32),
        compiler_params=pltpu.CompilerParams(dimension_semantics=("parallel",)),
    )(parts)


def _adamw(w, g, m, v, lead_per_block, rows_per_block):
    n, rows, cols = w.shape
    blk = pl.BlockSpec((lead_per_block, rows_per_block, cols), lambda i, j: (i, j, 0))

    def body(w_ref, g_ref, m_ref, v_ref, d_ref, nm_ref, nv_ref):
        gt = g_ref[...]
        nm = ADAM_B1 * m_ref[...] + (1.0 - ADAM_B1) * gt
        nv = ADAM_B2 * v_ref[...] + (1.0 - ADAM_B2) * (gt * gt)
        m_hat = nm / (1.0 - ADAM_B1 ** ADAM_STEP)
        v_hat = nv / (1.0 - ADAM_B2 ** ADAM_STEP)
        d_ref[...] = -ADAM_LR * (m_hat / (jnp.sqrt(v_hat) + ADAM_EPS) + ADAM_WD * w_ref[...])
        nm_ref[...] = nm
        nv_ref[...] = nv

    return pl.pallas_call(
        body, name="adamw", grid=(n // lead_per_block, rows // rows_per_block), in_specs=[blk] * 4,
        out_specs=[blk] * 3,
        out_shape=[jax.ShapeDtypeStruct(w.shape, F32)] * 3,
        compiler_params=pltpu.CompilerParams(dimension_semantics=("parallel", "parallel")),
    )(w, g, m, v)


def _place():
    x, y, c = lax.axis_index("x"), lax.axis_index("y"), lax.axis_index("c")
    return x, y, c, [(1 - x, y), (x, 1 - y), (1 - x, 1 - y)]


ANY = pl.BlockSpec(memory_space=pl.ANY)
COPY_PARTS = 4


def _remote_parts(src_of, dst_of, rows, send_sem, recv_sem, to, begin=True):
    parts = COPY_PARTS if rows % (16 * COPY_PARTS) == 0 else 1
    step = rows // parts
    for i in range(parts if begin is not False else 0):
        pltpu.make_async_remote_copy(src_ref=src_of(i * step, step), dst_ref=dst_of(i * step, step),
                                     send_sem=send_sem, recv_sem=recv_sem, device_id=to, device_id_type=MESH).start()
    if begin == "only":
        return None
    return pltpu.make_async_remote_copy(src_ref=src_of(0, rows), dst_ref=dst_of(0, rows), send_sem=send_sem,
                                        recv_sem=recv_sem, device_id=to, device_id_type=MESH)


GATHER_SEMS = 7


def _gather_steps(p_refs, o_refs, send_sems, recv_sems):
    x, y, c, chips = _place()
    me = 2 * x + y

    def half(ref, which):
        rows = ref.shape[0] // 2
        return ref.at[pl.ds(which * rows, rows)]

    def copy(k, src, dst, to):
        return pltpu.make_async_remote_copy(src_ref=src, dst_ref=dst, send_sem=send_sems.at[k],
                                            recv_sem=recv_sems.at[k], device_id=to, device_id_type=MESH)

    def first_copies():
        out = []
        for b, (p_ref, o_ref) in enumerate(zip(p_refs, o_refs)):
            for j, (cx, cy) in enumerate(chips):
                out.append(copy(GATHER_SEMS * b + j, half(p_ref, c), half(o_ref.at[me], c), (cx, cy, c)))
            out.append(copy(GATHER_SEMS * b + 6, p_ref, o_ref.at[me], (x, y, 1 - c)))
        return out

    def passed_copies():
        out = []
        for b, o_ref in enumerate(o_refs):
            for j, (cx, cy) in enumerate(chips):
                landed = half(o_ref.at[2 * cx + cy], c)
                out.append(copy(GATHER_SEMS * b + 3 + j, landed, landed, (x, y, 1 - c)))
        return out

    def start():
        for cp in first_copies():
            cp.start()

    def forward():
        for b, o_ref in enumerate(o_refs):
            for j, (cx, cy) in enumerate(chips):
                landed = half(o_ref.at[2 * cx + cy], c)
                copy(GATHER_SEMS * b + j, landed, landed, (x, y, c)).wait_recv()
        for cp in passed_copies():
            cp.start()

    def finish():
        for b, o_ref in enumerate(o_refs):
            for j, (cx, cy) in enumerate(chips):
                other = half(o_ref.at[2 * cx + cy], 1 - c)
                copy(GATHER_SEMS * b + 3 + j, other, other, (x, y, c)).wait_recv()
            copy(GATHER_SEMS * b + 6, o_ref.at[me], o_ref.at[me], (x, y, c)).wait_recv()
        for cp in first_copies() + passed_copies():
            cp.wait_send()

    return start, forward, finish


def _gather_chips(shards):
    nb = len(shards)

    def body(*refs):
        for step in _gather_steps(refs[:nb], refs[nb:2 * nb], *refs[2 * nb:]):
            step()

    return pl.pallas_call(
        body, name="gather_chips", in_specs=[ANY] * nb, out_specs=[ANY] * nb,
        out_shape=[jax.ShapeDtypeStruct((N_CHIPS,) + s.shape, s.dtype) for s in shards],
        scratch_shapes=[pltpu.SemaphoreType.DMA((GATHER_SEMS * nb,)), pltpu.SemaphoreType.DMA((GATHER_SEMS * nb,))],
    )(*shards)


def _swap_steps(g_refs, t_refs, spans, send_sems, recv_sems):
    x, y, c, _ = _place()

    def gives(begin):
        return [_remote_parts(
            lambda r0, m, g_ref=g_ref, j=j, base=first_row + (1 - c) * half: g_ref.at[j, pl.ds(base + r0, m), :],
            lambda r0, m, t_ref=t_ref, j=j: t_ref.at[j, pl.ds(r0, m), :],
            half, send_sems.at[b * N_CHIPS + j], recv_sems.at[b * N_CHIPS + j], (x, y, 1 - c), begin)
            for b, (g_ref, t_ref, (first_row, half)) in enumerate(zip(g_refs, t_refs, spans)) for j in range(N_CHIPS)]

    def start():
        gives("only")

    def finish():
        for give in gives(False):
            give.wait()

    return start, finish


def _swap_halves(bufs, spans):
    nb = len(bufs)

    def body(*refs):
        for step in _swap_steps(refs[:nb], refs[nb:2 * nb], spans, *refs[2 * nb:]):
            step()

    return pl.pallas_call(
        body, name="swap_halves", in_specs=[ANY] * nb, out_specs=[ANY] * nb,
        out_shape=[jax.ShapeDtypeStruct((b.shape[0], half, b.shape[2]), b.dtype) for b, (_, half) in zip(bufs, spans)],
        scratch_shapes=[pltpu.SemaphoreType.DMA((nb * N_CHIPS,)), pltpu.SemaphoreType.DMA((nb * N_CHIPS,))],
    )(*bufs)


def _scatter_steps(t_refs, o_refs, send_sems, recv_sems):
    x, y, c, chips = _place()
    me = 2 * x + y

    def slot(ref, chip):
        return lambda r0, n: ref.at[chip, pl.ds(r0, n), :]

    def sends(begin):
        return [_remote_parts(slot(t_ref, 2 * cx + cy), slot(o_ref, me), t_ref.shape[1], send_sems.at[3 * b + j],
                              recv_sems.at[3 * b + j], (cx, cy, c), begin)
                for b, (t_ref, o_ref) in enumerate(zip(t_refs, o_refs)) for j, (cx, cy) in enumerate(chips)]

    def start():
        sends("only")

    def finish():
        for b, o_ref in enumerate(o_refs):
            for j, (cx, cy) in enumerate(chips):
                landed = o_ref.at[2 * cx + cy]
                pltpu.make_async_remote_copy(src_ref=landed, dst_ref=landed, send_sem=send_sems.at[3 * b + j],
                                             recv_sem=recv_sems.at[3 * b + j], device_id=(x, y, c),
                                             device_id_type=MESH).wait_recv()
        for cp in sends(False):
            cp.wait_send()

    return start, finish


def _scatter_chips(bufs):
    nb = len(bufs)

    def body(*refs):
        for step in _scatter_steps(refs[:nb], refs[nb:2 * nb], *refs[2 * nb:]):
            step()

    return pl.pallas_call(
        body, name="scatter_chips", in_specs=[ANY] * nb, out_specs=[ANY] * nb,
        out_shape=[jax.ShapeDtypeStruct(b.shape, b.dtype) for b in bufs],
        scratch_shapes=[pltpu.SemaphoreType.DMA((3 * nb,)), pltpu.SemaphoreType.DMA((3 * nb,))],
    )(*bufs)


def _give_sibling(bufs):
    nb = len(bufs)

    def body(*refs):
        h_refs, o_refs, (send_sems, recv_sems) = refs[:nb], refs[nb:2 * nb], refs[2 * nb:]
        x, y, c, _ = _place()
        gives = [_remote_parts(lambda r0, n, h_ref=h_ref: h_ref.at[pl.ds(r0, n), :],
                               lambda r0, n, o_ref=o_ref: o_ref.at[pl.ds(r0, n), :],
                               h_ref.shape[0], send_sems.at[b], recv_sems.at[b], (x, y, 1 - c))
                 for b, (h_ref, o_ref) in enumerate(zip(h_refs, o_refs))]
        for give in gives:
            give.wait()

    return pl.pallas_call(
        body, name="give_sibling", in_specs=[ANY] * nb, out_specs=[ANY] * nb,
        out_shape=[jax.ShapeDtypeStruct(b.shape, b.dtype) for b in bufs],
        scratch_shapes=[pltpu.SemaphoreType.DMA((nb,)), pltpu.SemaphoreType.DMA((nb,))],
    )(*bufs)


def _pack(arrays, dtype, row_align):
    flat = []
    for arr in arrays:
        v = arr.reshape(-1)
        flat.append(jnp.pad(v, (0, (-v.shape[0]) % LANES)))
    total = sum(f.shape[0] for f in flat) // LANES
    pad_rows = (-total) % row_align
    if pad_rows:
        flat.append(jnp.zeros((pad_rows * LANES,), dtype))
    return jnp.concatenate(flat).reshape(-1, LANES)


def _unpack(buf, shapes, rows_align=1):
    lead = buf.shape[:-2]
    flat = buf.reshape(lead + (-1,))
    out, off = [], 0
    for shape in shapes:
        size = 1
        for dim in shape:
            size *= dim
        out.append(flat[..., off:off + size].reshape(lead + tuple(shape)))
        off += size + (-size) % (LANES * rows_align)
    return out


def _from_chips(parts, axis):
    return jnp.concatenate([parts[j] for j in range(N_CHIPS)], axis=axis)


def kernel(x, ln_g, ln_b, attn_w_in, attn_b_f, attn_w_out, rnn_w_in, rnn_conv_w, rnn_conv_b, rnn_w_a, rnn_b_a, rnn_w_i, rnn_b_i, rnn_lambda, rnn_w_out, loss_target, m_ln_g, m_ln_b, m_attn_w_in, m_attn_b_f, m_attn_w_out, m_rnn_w_in, m_rnn_conv_w, m_rnn_conv_b, m_rnn_w_a, m_rnn_b_a, m_rnn_w_i, m_rnn_b_i, m_rnn_lambda, m_rnn_w_out, v_ln_g, v_ln_b, v_attn_w_in, v_attn_b_f, v_attn_w_out, v_rnn_w_in, v_rnn_conv_w, v_rnn_conv_b, v_rnn_w_a, v_rnn_b_a, v_rnn_w_i, v_rnn_b_i, v_rnn_lambda, v_rnn_w_out):
    s_len, d = x.shape[1], x.shape[2]
    xs = x.reshape(s_len, d)
    target = loss_target.reshape(s_len, d)
    heads = attn_b_f.shape[1]
    qkvg = attn_w_in.shape[2] * N_CHIPS - heads

    me = 2 * lax.axis_index("x") + lax.axis_index("y")
    core = lax.axis_index("c").astype(jnp.int32)
    small = [rnn_conv_w, rnn_conv_b, rnn_b_a, rnn_b_i, rnn_lambda]
    a_in, a_out = attn_w_in.astype(BF16), attn_w_out.astype(BF16)
    later = [a_in[1], a_out] + [w.astype(BF16) for w in (rnn_w_in, rnn_w_a, rnn_w_i, rnn_w_out)]
    got_in, got_small = _gather_chips([a_in[0], _pack(small, F32, 16)])
    conv_w, conv_b, b_a, b_i, lam = [
        _from_chips(p, ax) for p, ax in zip(_unpack(got_small, [w.shape for w in small]), (2, 1, 1, 1, 1))]

    def attn_in_weights(w_in):
        full = jnp.concatenate([w_in[j] for j in range(N_CHIPS)], axis=1)
        return jnp.concatenate([full[:, :d] * (HEAD_DIM ** -0.5), full[:, d:qkvg],
                                jnp.pad(full[:, qkvg:], ((0, 0), (0, 128 - heads)))], axis=1).astype(BF16)

    w_cat = [attn_in_weights(got_in), None]
    b_f = jnp.pad(attn_b_f, ((0, 0), (0, 128 - heads)))

    saved = []
    cur = xs
    for layer in range(DEPTH):
        idx = layer // 2
        g_l, b_l = ln_g[layer:layer + 1], ln_b[layer:layer + 1]
        if layer % 2 == 0:
            q, k, v, gate, fl = _proj(cur, w_cat[idx], [(0, d, BF16), (d, d, BF16), (2 * d, d, BF16),
                                                         (3 * d, d, F32), (4 * d, 128, F32)], "attn_proj")
            cum, sneg = _fcum(fl, b_f[idx:idx + 1])
            o, lse, *rest = _flash_fwd(q, k, v, cum, later if layer == 0 else ())
            if layer == 0:
                w_cat[1] = attn_in_weights(rest[0])
                w_out_a = jnp.moveaxis(rest[1], 0, 1).reshape(2, d, d)
                w_in_r = jnp.moveaxis(rest[2], 0, 2).reshape(2, d, 2 * d)
                w_a = jnp.transpose(rest[3], (1, 2, 0, 3, 4)).reshape(2, -1, RNN_BLOCK, RNN_BLOCK)
                w_i = jnp.transpose(rest[4], (1, 2, 0, 3, 4)).reshape(2, -1, RNN_BLOCK, RNN_BLOCK)
                w_out_r = jnp.moveaxis(rest[5], 0, 1).reshape(2, d, d)
            nxt, xh, rs, yg = _out_ln(o, gate, cur, w_out_a[idx], g_l, b_l, "out_ln")
            saved.append(dict(x=cur, q=q, k=k, v=v, gate=gate, cum=cum, sneg=sneg, a=o, lse=lse, xh=xh, rs=rs, yg=yg))
        else:
            u, gate = _proj(cur, w_in_r[idx], [(0, d, F32), (d, d, F32)], "rnn_proj")
            vecs = [t[idx:idx + 1] for t in (conv_b, b_a, b_i, lam)]
            hseq, uc, r, ig, a = _rnn_fwd(u, conv_w[idx], vecs[0], w_a[idx], vecs[1], w_i[idx], vecs[2], vecs[3])
            nxt, xh, rs, yg = _out_ln(hseq, gate, cur, w_out_r[idx], g_l, b_l, "out_ln")
            saved.append(dict(x=cur, u=u, gate=gate, uc=uc, r=r, ig=ig, av=a, a=hseq, xh=xh, rs=rs, yg=yg, lam=vecs[3]))
        cur = nxt

    dout, sq = _loss_grad(cur, target)
    loss = lax.psum(0.5 * jnp.sum(sq) / d, ("x", "y", "c"))

    g_ln_g, g_ln_b = [None] * DEPTH, [None] * DEPTH
    g_attn = [None] * 2
    g_rnn = [None] * 2
    slots = None
    core1 = core.reshape(1)
    late_half = (SLOT_ROWS - LATE_ROWS) // 2

    def by_chip(t, axis):
        shape = t.shape[:axis] + (N_CHIPS, t.shape[axis] // N_CHIPS) + t.shape[axis + 1:]
        flat = jnp.moveaxis(t.reshape(shape), axis, 0).reshape(N_CHIPS, -1)
        return jnp.pad(flat, ((0, 0), (0, (-flat.shape[1]) % (8 * LANES)))).reshape(N_CHIPS, -1, LANES)

    def both(key):
        return jnp.stack([it[key] for it in g_rnn])

    for layer in reversed(range(DEPTH)):
        idx = layer // 2
        sv = saved[layer]
        swap = None
        if layer == 0:
            gates = jnp.concatenate([by_chip(both("w_a"), 2), by_chip(both("w_i"), 2)], axis=1)
            slots = lax.dynamic_update_slice(slots, gates, (0, ROWS_GATES, 0))
            swap = (slots, (LATE_ROWS, late_half))
        w_out = w_out_a[idx] if layer % 2 == 0 else w_out_r[idx]
        dz, da, dgate, dg, db, *theirs_late = _ln_bwd(dout, sv["xh"], sv["rs"], ln_g[layer:layer + 1], w_out,
                                                      sv["gate"], sv["a"], "ln_bwd", swap)
        if layer == 0:
            pair_late = _pair_sum(core1, slots, theirs_late[0], LATE_ROWS, BF16)
        g_ln_g[layer], g_ln_b[layer] = dg, db
        slots = _dw_out(slots, sv["yg"], dz, (ROWS_ATTN_OUT if layer % 2 == 0 else ROWS_RNN_OUT)[idx])
        if layer % 2 == 0:
            dq, dk, dv, dcum, *arrived = _flash_bwd(sv["q"], sv["k"], sv["v"], sv["a"], da, sv["lse"], sv["cum"],
                                                    [pair_late] if layer == 0 else ())
            dlogit, dbf = _fbwd(dcum, sv["sneg"])
            pieces = [dq, dk, dv, dgate, dlogit]
            dout = _mm_nt(dz, [(p, j * d) for j, p in enumerate(pieces)], w_cat[idx], "attn_dx")
            slots = _dw_attn_in(slots, sv["x"], pieces[:4], idx)
            g_attn[idx] = dict(w_f=_mm_tn(sv["x"], dlogit, "dw_f")[:, :heads], b_f=dbf[:, 0])
        else:
            duc, d_wa, d_wi, d_ba, d_bi, d_lam = _rnn_bwd(da, sv["av"], sv["a"], sv["r"], sv["ig"], sv["uc"], sv["lam"],
                                                          w_a[idx], w_i[idx])
            du, d_cw, d_cb = _conv_bwd(duc, sv["u"], conv_w[idx])
            dout = _mm_nt(dz, [(du, 0), (dgate, d)], w_in_r[idx], "rnn_dx")
            slots = _dw_rnn_in(slots, sv["x"], (du, dgate), idx)
            g_rnn[idx] = dict(conv_w=d_cw, conv_b=d_cb[0], w_a=d_wa, b_a=d_ba[0], w_i=d_wi, b_i=d_bi[0], lam=d_lam[0])
    grad_x = dout.reshape(x.shape)

    def everywhere(t):
        flat = t.reshape(-1)
        flat = jnp.pad(flat, (0, (-flat.shape[0]) % (8 * LANES))).reshape(-1, LANES)
        return jnp.broadcast_to(flat[None], (N_CHIPS,) + flat.shape)

    in_rows = jnp.stack([slots[1:, r:r + d, :heads] for r in ROWS_ATTN_IN], axis=1)
    edges = jnp.concatenate([in_rows, jnp.stack([it["w_f"] for it in g_attn])[None]])
    rows = [everywhere(edges), by_chip(both("conv_w"), 2),
            by_chip(both("conv_b"), 1), by_chip(both("b_a"), 1), by_chip(both("b_i"), 1), by_chip(both("lam"), 1),
            everywhere(jnp.concatenate(g_ln_g)), everywhere(jnp.concatenate(g_ln_b)),
            everywhere(jnp.stack([it["b_f"] for it in g_attn]))]
    used = sum(r.shape[1] for r in rows)
    small = jnp.concatenate(rows + [jnp.zeros((N_CHIPS, (-used) % 32, LANES), F32)], axis=1)

    spans = [(0, LATE_ROWS // 2), (0, small.shape[1] // 2)]
    theirs = _swap_halves([slots, small], spans)
    pair = [_pair_sum(core1, slots, theirs[0], 0, BF16), _pair_sum(core1, small, theirs[1], 0, F32)]
    summed = []
    for mine_all, got in zip([pair_late] + pair, list(arrived) + list(_scatter_chips(pair))):
        own = lax.dynamic_index_in_dim(mine_all, me, 0, keepdims=False)
        summed.append(_sum_chips(lax.dynamic_update_index_in_dim(got, own, me, 0)))
    late, early, small_sum = [
        jnp.concatenate([jnp.where(core == 0, mine, other), jnp.where(core == 0, other, mine)])
        for mine, other in zip(summed, _give_sibling(summed))]

    def rows_at(first, n):
        return early[first:first + n] if first < LATE_ROWS else late[first - LATE_ROWS:first - LATE_ROWS + n]

    g_in_group = jnp.stack([rows_at(r, d) for r in ROWS_ATTN_IN])
    g_w_out_a = jnp.stack([rows_at(r, d // N_CHIPS) for r in ROWS_ATTN_OUT])
    g_w_out_r = jnp.stack([rows_at(r, d // N_CHIPS) for r in ROWS_RNN_OUT])
    folded = jnp.stack([rows_at(r, d // 2) for r in ROWS_RNN_IN])
    g_w_in_r = jnp.concatenate([folded[:, :, :d // 2], folded[:, :, d // 2:]], axis=1)
    gate_rows = rnn_w_a.size // LANES
    g_w_a = rows_at(ROWS_GATES, gate_rows).reshape(rnn_w_a.shape)
    g_w_i = rows_at(ROWS_GATES + gate_rows, gate_rows).reshape(rnn_w_i.shape)
    (g_edges, g_conv_w, g_conv_b, g_b_a, g_b_i, g_lam, g_ln_g_sum, g_ln_b_sum,
     g_b_f) = _unpack(small_sum, [
        (N_CHIPS, 2, d, heads), rnn_conv_w.shape, rnn_conv_b.shape, rnn_b_a.shape,
        rnn_b_i.shape, rnn_lambda.shape, ln_g.shape, ln_b.shape, attn_b_f.shape], rows_align=8)
    wide = jnp.concatenate([g_in_group, lax.dynamic_index_in_dim(g_edges, me, 0, keepdims=False)], axis=2)
    g_w_in_a = lax.dynamic_slice(wide, (0, 0, (heads // N_CHIPS) * me), attn_w_in.shape)

    def adam_nd(w, g, m, v, view, rows_per_block):
        outs = _adamw(w.reshape(view), g.reshape(view), m.reshape(view), v.reshape(view), 1, rows_per_block)
        return [t.reshape(w.shape) for t in outs]

    turned = [jnp.transpose(t, (2, 0, 1)) for t in (attn_w_in, g_w_in_a, m_attn_w_in, v_attn_w_in)]
    upd = {
        "attn_w_in": [jnp.transpose(t, (1, 2, 0)) for t in _adamw(*turned, attn_w_in.shape[2] // N_CHIPS, 2)],
        "attn_w_out": adam_nd(attn_w_out, g_w_out_a, m_attn_w_out, v_attn_w_out, attn_w_out.shape, 256),
        "rnn_w_in": adam_nd(rnn_w_in, g_w_in_r, m_rnn_w_in, v_rnn_w_in, rnn_w_in.shape, 512),
        "rnn_w_a": adam_nd(rnn_w_a, g_w_a, m_rnn_w_a, v_rnn_w_a, (1, -1, RNN_BLOCK), 512),
        "rnn_w_i": adam_nd(rnn_w_i, g_w_i, m_rnn_w_i, v_rnn_w_i, (1, -1, RNN_BLOCK), 512),
        "rnn_w_out": adam_nd(rnn_w_out, g_w_out_r, m_rnn_w_out, v_rnn_w_out, rnn_w_out.shape, 256),
    }
    smalls = [rnn_conv_w, rnn_conv_b, rnn_b_a, rnn_b_i, rnn_lambda, ln_g, ln_b, attn_b_f]
    g_small = [g_conv_w, g_conv_b, g_b_a, g_b_i, g_lam, g_ln_g_sum, g_ln_b_sum, g_b_f]
    m_small = [m_rnn_conv_w, m_rnn_conv_b, m_rnn_b_a, m_rnn_b_i, m_rnn_lambda, m_ln_g, m_ln_b, m_attn_b_f]
    v_small = [v_rnn_conv_w, v_rnn_conv_b, v_rnn_b_a, v_rnn_b_i, v_rnn_lambda, v_ln_g, v_ln_b, v_attn_b_f]
    packs = [_pack(t, F32, 16)[None] for t in (smalls, g_small, m_small, v_small)]
    small_out = [_unpack(t[0], [w.shape for w in smalls]) for t in _adamw(*packs, 1, 16)]
    for k, name in enumerate(("rnn_conv_w", "rnn_conv_b", "rnn_b_a", "rnn_b_i", "rnn_lambda", "ln_g", "ln_b",
                              "attn_b_f")):
        upd[name] = [small_out[0][k], small_out[1][k], small_out[2][k]]
    grad = {"attn_w_in": g_w_in_a, "attn_w_out": g_w_out_a, "rnn_w_in": g_w_in_r, "rnn_w_a": g_w_a, "rnn_w_i": g_w_i,
            "rnn_w_out": g_w_out_r, "rnn_conv_w": g_conv_w, "rnn_conv_b": g_conv_b, "rnn_b_a": g_b_a,
            "rnn_b_i": g_b_i, "rnn_lambda": g_lam, "ln_g": g_ln_g_sum, "ln_b": g_ln_b_sum, "attn_b_f": g_b_f}
    names = ("ln_g", "ln_b", "attn_w_in", "attn_b_f", "attn_w_out", "rnn_w_in", "rnn_conv_w", "rnn_conv_b",
             "rnn_w_a", "rnn_b_a", "rnn_w_i", "rnn_b_i", "rnn_lambda", "rnn_w_out")
    return (loss, grad_x, *[grad[n] for n in names], *[upd[n][0] for n in names], *[upd[n][1] for n in names],
            *[upd[n][2] for n in names])
```

```python
import jax
import jax.numpy as jnp
from jax import lax
from jax.experimental import pallas as pl
from jax.experimental.pallas import tpu as pltpu

F32 = jnp.float32
BF16 = jnp.bfloat16
MESH = pl.DeviceIdType.MESH

DEPTH = 4
HEAD_DIM = 64
HEAD_PAIR = 2 * HEAD_DIM
RNN_BLOCK = 256
CONV_WIDTH = 4
LRU_C = 8.0
ALPHA = (2.0 * DEPTH) ** 0.25
LN_EPS = 1e-5
ADAM_LR, ADAM_B1, ADAM_B2, ADAM_EPS, ADAM_WD, ADAM_STEP = 0.001, 0.9, 0.999, 1e-08, 0.01, 10
N_CHIPS = 4
LANES = 1024
NEG = -1e30

NT_DIMS = (((1,), (1,)), ((), ()))
TN_DIMS = (((0,), (0,)), ((), ()))


def _sigmoid(z):
    return 1.0 / (1.0 + jnp.exp(-z))


def _softplus(z):
    return jnp.maximum(z, 0.0) + jnp.log(1.0 + jnp.exp(-jnp.abs(z)))


def _neg_expm1(y):
    poly = y * (1.0 + y * (0.5 + y * (1.0 / 6.0 + y * (1.0 / 24.0))))
    return -jnp.where(y > -0.05, poly, jnp.exp(y) - 1.0)


def _shift_down(cur, prev8, k):
    n = cur.shape[0]
    row = lax.broadcasted_iota(jnp.int32, cur.shape, 0)
    fix = jnp.tile(pltpu.roll(prev8, k, 0), (n // 8, 1))
    return jnp.where(row < k, fix, pltpu.roll(cur, k, 0))


def _shift_up(cur, next8, k):
    n = cur.shape[0]
    row = lax.broadcasted_iota(jnp.int32, cur.shape, 0)
    fix = jnp.tile(pltpu.roll(next8, 8 - k, 0), (n // 8, 1))
    return jnp.where(row >= n - k, fix, pltpu.roll(cur, n - k, 0))


def _proj(x, w, outs, name):
    s_len, d = x.shape
    tm = min(512, s_len)

    def body(x_ref, w_ref, *o_refs):
        xb = x_ref[...].astype(BF16)
        for (c0, wd, dt), o_ref in zip(outs, o_refs):
            step = min(wd, 512)
            for cc in range(0, wd, step):
                o_ref[:, cc:cc + step] = jnp.dot(
                    xb, w_ref[:, c0 + cc:c0 + cc + step], preferred_element_type=F32).astype(dt)

    return pl.pallas_call(
        body, name=name, grid=(s_len // tm,),
        in_specs=[pl.BlockSpec((tm, d), lambda i: (i, 0)), pl.BlockSpec(w.shape, lambda i: (0, 0))],
        out_specs=[pl.BlockSpec((tm, wd), lambda i: (i, 0)) for _, wd, _ in outs],
        out_shape=[jax.ShapeDtypeStruct((s_len, wd), dt) for _, wd, dt in outs],
        compiler_params=pltpu.CompilerParams(dimension_semantics=("parallel",)),
    )(x, w)


def _mm_nt(dz, pieces, w, name):
    s_len, d = dz.shape
    tm = min(256, s_len)
    n = len(pieces)
    cols = [(c0, p.shape[1]) for p, c0 in pieces]

    def body(dz_ref, *rest):
        w_ref, o_ref = rest[n], rest[n + 1]
        acc = ALPHA * dz_ref[...]
        for (c0, wd), p_ref in zip(cols, rest[:n]):
            acc = acc + lax.dot_general(p_ref[...], w_ref[:, c0:c0 + wd], NT_DIMS, preferred_element_type=F32)
        o_ref[...] = acc

    return pl.pallas_call(
        body, name=name, grid=(s_len // tm,),
        in_specs=[pl.BlockSpec((tm, d), lambda i: (i, 0))]
        + [pl.BlockSpec((tm, wd), lambda i: (i, 0)) for _, wd in cols]
        + [pl.BlockSpec(w.shape, lambda i: (0, 0))],
        out_specs=pl.BlockSpec((tm, d), lambda i: (i, 0)),
        out_shape=jax.ShapeDtypeStruct((s_len, d), F32),
        compiler_params=pltpu.CompilerParams(dimension_semantics=("parallel",)),
    )(dz, *[p for p, _ in pieces], w)


ROWS_ATTN_IN = (0, 2048)
ROWS_ATTN_OUT = (1024, 1280)
ROWS_RNN_OUT = (1536, 1792)
ROWS_RNN_IN = (3072, 3584)
ROWS_GATES = 4096
LATE_ROWS = 1280
SLOT_ROWS = 4352


DW_ROWS = 1024


def _dw_call(body, name, slots, operands, specs, out_block, out_index, grid, semantics):
    return pl.pallas_call(
        body, name=name, grid=grid,
        in_specs=specs if slots is None else [ANY] + specs,
        out_specs=pl.BlockSpec(out_block, out_index),
        out_shape=jax.ShapeDtypeStruct((N_CHIPS, SLOT_ROWS, LANES), F32),
        input_output_aliases={} if slots is None else {0: 0},
        compiler_params=pltpu.CompilerParams(dimension_semantics=semantics),
    )(*(operands if slots is None else [slots] + operands))


def _dw_attn_in(slots, x, pieces, forget, layer):
    s_len, d = x.shape
    ts = min(s_len, DW_ROWS)
    steps = s_len // ts
    half = d // 2

    def body(g_ref, x_ref, p0, p1, p2, p3, f_ref, o_ref, wf_ref):
        lanes, step = pl.program_id(0), pl.program_id(1)

        @pl.when(step == 0)
        def _():
            o_ref[...] = jnp.zeros_like(o_ref)

        xb = x_ref[...].astype(BF16)
        for j, p_ref in enumerate((p0, p1, p2, p3)):
            o_ref[j] += lax.dot_general(xb, p_ref[...], TN_DIMS, preferred_element_type=F32)

        @pl.when(step == steps - 1)
        def _():
            o_ref[0] = o_ref[0] * (HEAD_DIM ** -0.5)

        @pl.when(lanes == 0)
        def _():
            @pl.when(step == 0)
            def _():
                wf_ref[...] = jnp.zeros_like(wf_ref)

            wf_ref[...] += lax.dot_general(xb, f_ref[...], TN_DIMS, preferred_element_type=F32)

    return pl.pallas_call(
        body, name="dw_attn_in", grid=(2, steps),
        in_specs=[ANY, pl.BlockSpec((ts, d), lambda i, s: (s, 0))] + [pl.BlockSpec((ts, half), lambda i, s: (s, i))] * 4
        + [pl.BlockSpec((ts, 128), lambda i, s: (s, 0))],
        out_specs=[pl.BlockSpec((N_CHIPS, d, half), lambda i, s: (0, ROWS_ATTN_IN[layer] // d, i)),
                   pl.BlockSpec((d, 128), lambda i, s: (0, 0))],
        out_shape=[jax.ShapeDtypeStruct((N_CHIPS, SLOT_ROWS, LANES), F32), jax.ShapeDtypeStruct((d, 128), F32)],
        input_output_aliases={0: 0},
        compiler_params=pltpu.CompilerParams(dimension_semantics=("arbitrary", "arbitrary")),
    )(slots, x, *pieces, forget)


def _dw_out(slots, yg, dz, first_row):
    s_len, d = dz.shape
    ts = min(s_len, DW_ROWS)
    rows = d // N_CHIPS

    def body(*refs):
        a_ref, b_ref, o_ref = refs[-3:]

        @pl.when(pl.program_id(0) == 0)
        def _():
            o_ref[...] = jnp.zeros_like(o_ref)

        prod = lax.dot_general(a_ref[...], b_ref[...].astype(BF16), TN_DIMS, preferred_element_type=F32)
        o_ref[...] += prod.reshape(N_CHIPS, rows, d)

    specs = [pl.BlockSpec((ts, d), lambda s: (s, 0))] * 2
    return _dw_call(body, "dw_out", slots, [yg, dz], specs, (N_CHIPS, rows, d), lambda s: (0, first_row // rows, 0),
                    (s_len // ts,), ("arbitrary",))


def _dw_rnn_in(slots, x, parts, layer):
    s_len, d = x.shape
    ts = min(s_len, DW_ROWS)
    half = d // 2

    def body(*refs):
        x_ref, p_refs, o_ref = refs[-4], refs[-3:-1], refs[-1]

        @pl.when(pl.program_id(0) == 0)
        def _():
            o_ref[...] = jnp.zeros_like(o_ref)

        xb = x_ref[...].astype(BF16)
        for p, p_ref in enumerate(p_refs):
            for jj in (0, 1):
                for r in (0, 1):
                    o_ref[2 * p + jj, :, r * half:(r + 1) * half] += lax.dot_general(
                        xb[:, r * half:(r + 1) * half], p_ref[:, jj * half:(jj + 1) * half], TN_DIMS,
                        preferred_element_type=F32)

    specs = [pl.BlockSpec((ts, d), lambda s: (s, 0))] * 3
    return _dw_call(body, "dw_rnn_in", slots, [x] + list(parts), specs, (N_CHIPS, half, d),
                    lambda s: (0, ROWS_RNN_IN[layer] // half, 0), (s_len // ts,), ("arbitrary",))


def _fcum(fl, bias):
    s_len = fl.shape[0]

    def body(fl_ref, b_ref, cum_ref, sg_ref):
        z = fl_ref[...] + b_ref[...]
        e = jnp.exp(-jnp.abs(z))
        logf = jnp.minimum(z, 0.0) - jnp.log(1.0 + e)
        sneg = jnp.where(z >= 0, e, 1.0) / (1.0 + e)
        run = logf.T[0:16, :]
        sg_ref[...] = sneg.T[0:16, :]
        lane = lax.broadcasted_iota(jnp.int32, (16, s_len), 1)
        sh = 1
        while sh < s_len:
            run = run + jnp.where(lane >= sh, pltpu.roll(run, sh, 1), 0.0)
            sh *= 2
        cum_ref[...] = run

    return pl.pallas_call(
        body, name="fcum",
        out_shape=[jax.ShapeDtypeStruct((16, s_len), F32), jax.ShapeDtypeStruct((16, s_len), F32)],
    )(fl, bias)


def _fbwd(dcum, sneg):
    s_len = dcum.shape[1]

    def body(dc_ref, sg_ref, dl_ref, db_ref):
        run = dc_ref[...]
        lane = lax.broadcasted_iota(jnp.int32, (16, s_len), 1)
        sh = 1
        while sh < s_len:
            run = run + jnp.where(lane < s_len - sh, pltpu.roll(run, s_len - sh, 1), 0.0)
            sh *= 2
        dlog = run * sg_ref[...]
        db_ref[...] = jnp.broadcast_to(jnp.sum(dlog, axis=1, keepdims=True), (16, 128))
        full = jnp.concatenate([dlog, jnp.zeros((112, s_len), F32)], axis=0)
        dl_ref[...] = full.T.astype(BF16)

    return pl.pallas_call(
        body, name="fbwd",
        out_shape=[jax.ShapeDtypeStruct((s_len, 128), BF16), jax.ShapeDtypeStruct((16, 128), F32)],
    )(dcum, sneg)


FWD_TILE = 1024
BWD_TILE = 512


def _flash_fwd(q, k, v, cum, shards=()):
    s_len, width = q.shape
    tile = min(FWD_TILE, s_len // 2)
    nt = s_len // tile
    reps = tile // 128
    cumr = cum.reshape(-1, tile)
    ns = len(shards)
    pairs = width // HEAD_PAIR

    def body(*refs):
        q_ref, k_ref, v_ref, cum_ref = refs[:4]
        o_ref, lse_ref = refs[4 + ns:6 + ns]
        q_t, v_t, cum_col = refs[6 + 2 * ns:9 + 2 * ns]
        pid = pl.program_id(0)
        if ns:
            start, forward, finish = _gather_steps(refs[4:4 + ns], refs[6 + ns:6 + 2 * ns], *refs[9 + 2 * ns:])
            pl.when(pid == 0)(start)
            pl.when(pid == pairs // 2)(forward)
        top = lax.broadcasted_iota(jnp.int32, (HEAD_PAIR, tile), 0) < HEAD_DIM
        causal = (lax.broadcasted_iota(jnp.int32, (tile, tile), 0)
                  <= lax.broadcasted_iota(jnp.int32, (tile, tile), 1))

        def pre(i, carry):
            r0 = pl.multiple_of(i * tile, tile)
            q_t[i] = q_ref[pl.ds(r0, tile), :].astype(F32).T.astype(BF16)
            v_t[i] = v_ref[pl.ds(r0, tile), :].astype(F32).T.astype(BF16)
            for h in (0, 1):
                row = cum_ref[pl.ds((2 * pid + h) * nt + i, 1), :]
                cum_col[h, pl.ds(r0, tile), :] = jnp.broadcast_to(row, (HEAD_PAIR, tile)).T
            return carry

        lax.fori_loop(0, nt, pre, 0)

        def q_loop(qi, carry):
            qt = q_t[qi]
            zt = jnp.zeros_like(qt)
            qms = (jnp.where(top, qt, zt), jnp.where(top, zt, qt))

            def step(kj, state, masked):
                m0, l0, m1, l1, acc = state
                k0 = pl.multiple_of(kj * tile, tile)
                kt = k_ref[pl.ds(k0, tile), :]
                vt = v_t[kj]
                ms, ls, scales, contrib = [m0, m1], [l0, l1], [], None
                for h in (0, 1):
                    s = jnp.dot(kt, qms[h], preferred_element_type=F32)
                    s = s - jnp.tile(cum_col[h, pl.ds(k0, tile), :], (1, reps))
                    if masked:
                        s = jnp.where(causal, s, NEG)
                    m_new = jnp.maximum(ms[h], jnp.max(s, axis=0, keepdims=True))
                    p = jnp.exp(s - m_new)
                    scale = jnp.exp(ms[h] - m_new)
                    ls[h] = scale * ls[h] + jnp.sum(p, axis=0, keepdims=True)
                    ms[h] = m_new
                    scales.append(scale)
                    vm = jnp.where(top, vt, zt) if h == 0 else jnp.where(top, zt, vt)
                    part = jnp.dot(vm, p.astype(BF16), preferred_element_type=F32)
                    contrib = part if contrib is None else contrib + part
                acc = jnp.where(top, scales[0], scales[1]) * acc + contrib
                return ms[0], ls[0], ms[1], ls[1], acc

            low = jnp.full((1, tile), NEG, F32)
            zero = jnp.zeros((1, tile), F32)
            state = step(qi, (low, zero, low, zero, jnp.zeros((HEAD_PAIR, tile), F32)), True)
            m0, l0, m1, l1, acc = lax.fori_loop(0, qi, lambda kj, c: step(kj, c, False), state)
            q0 = pl.multiple_of(qi * tile, tile)
            o_ref[pl.ds(q0, tile), :] = (acc / jnp.where(top, l0, l1)).T
            lse_ref[pl.ds((2 * pid) * nt + qi, 1), :] = m0 + jnp.log(l0)
            lse_ref[pl.ds((2 * pid + 1) * nt + qi, 1), :] = m1 + jnp.log(l1)
            return carry

        lax.fori_loop(0, nt, q_loop, 0)
        if ns:
            pl.when(pid == pairs - 1)(finish)

    blk = pl.BlockSpec((s_len, HEAD_PAIR), lambda p: (0, p))
    full = pl.BlockSpec(cumr.shape, lambda p: (0, 0))
    sems = [pltpu.SemaphoreType.DMA((GATHER_SEMS * ns,))] * 2 if ns else []
    o, lse, *gathered = pl.pallas_call(
        body, name="flash_fwd_gather" if ns else "flash_fwd", grid=(pairs,),
        in_specs=[blk, blk, blk, full] + [ANY] * ns,
        out_specs=[blk, full] + [ANY] * ns,
        out_shape=[jax.ShapeDtypeStruct((s_len, width), F32), jax.ShapeDtypeStruct(cumr.shape, F32)]
        + [jax.ShapeDtypeStruct((N_CHIPS,) + s.shape, s.dtype) for s in shards],
        scratch_shapes=[pltpu.VMEM((nt, HEAD_PAIR, tile), BF16), pltpu.VMEM((nt, HEAD_PAIR, tile), BF16),
                        pltpu.VMEM((2, s_len, HEAD_PAIR), F32)] + sems,
        compiler_params=pltpu.CompilerParams(dimension_semantics=("arbitrary",)),
    )(q, k, v, cumr, *shards)
    return (o, lse.reshape(cum.shape), *gathered)


def _flash_bwd(q, k, v, o, do, lse, cum, outgoing=()):
    s_len, width = q.shape
    tile = min(BWD_TILE, s_len // 2)
    nt = s_len // tile
    reps = tile // 128
    cumr = cum.reshape(-1, tile)
    lse = lse.reshape(-1, tile)
    ns = len(outgoing)
    pairs = width // HEAD_PAIR

    def body(*refs):
        q_ref, k_ref, v_ref, o_ref, do_ref, lse_ref, cum_ref = refs[:7]
        dq_ref, dk_ref, dv_ref, dcum_ref = refs[7 + ns:11 + ns]
        (q_t, do_t, k_t, do_bf, cum_col, dq_t_acc, delta, q_side, dk_acc, dv_acc,
         k_side) = refs[11 + 2 * ns:22 + 2 * ns]
        pid = pl.program_id(0)
        if ns:
            start, finish = _scatter_steps(refs[7:7 + ns], refs[11 + ns:11 + 2 * ns], *refs[22 + 2 * ns:])
            pl.when(pid == 0)(start)
        top = lax.broadcasted_iota(jnp.int32, (HEAD_PAIR, tile), 0) < HEAD_DIM
        left = lax.broadcasted_iota(jnp.int32, (tile, HEAD_PAIR), 1) < HEAD_DIM
        causal = (lax.broadcasted_iota(jnp.int32, (tile, tile), 0)
                  <= lax.broadcasted_iota(jnp.int32, (tile, tile), 1))

        def pre(i, carry):
            r0 = pl.multiple_of(i * tile, tile)
            d_o = do_ref[pl.ds(r0, tile), :]
            prod_t = (d_o * o_ref[pl.ds(r0, tile), :]).T
            delta[pl.ds(i, 1), :] = jnp.sum(prod_t[:HEAD_DIM], axis=0, keepdims=True)
            delta[pl.ds(nt + i, 1), :] = jnp.sum(prod_t[HEAD_DIM:], axis=0, keepdims=True)
            do_bf[pl.ds(r0, tile), :] = d_o.astype(BF16)
            do_t[i] = d_o.T.astype(BF16)
            q_t[i] = q_ref[pl.ds(r0, tile), :].astype(F32).T.astype(BF16)
            k_t[i] = k_ref[pl.ds(r0, tile), :].astype(F32).T.astype(BF16)
            dq_t_acc[i] = jnp.zeros((HEAD_PAIR, tile), F32)
            for h in (0, 1):
                row = cum_ref[pl.ds((2 * pid + h) * nt + i, 1), :]
                cum_col[h, pl.ds(r0, tile), :] = jnp.broadcast_to(row, (HEAD_PAIR, tile)).T
                q_side[pl.ds(h * nt + i, 1), :] = jnp.zeros((1, tile), F32)
            return carry

        lax.fori_loop(0, nt, pre, 0)

        def kv_loop(kj, carry):
            k0 = pl.multiple_of(kj * tile, tile)
            kt = k_ref[pl.ds(k0, tile), :]
            vt = v_ref[pl.ds(k0, tile), :]
            ktt = k_t[kj]
            zt = jnp.zeros_like(ktt)
            zr = jnp.zeros_like(kt)
            kms = (jnp.where(top, ktt, zt), jnp.where(top, zt, ktt))
            cols = [jnp.tile(cum_col[h, pl.ds(k0, tile), :], (1, reps)) for h in (0, 1)]
            dk_acc[...] = jnp.zeros_like(dk_acc)
            dv_acc[...] = jnp.zeros_like(dv_acc)
            k_side[...] = jnp.zeros_like(k_side)

            def step(qi, carry, masked):
                q0 = pl.multiple_of(qi * tile, tile)
                qtt = q_t[qi]
                dtt = do_t[qi]
                q_rows = q_ref[pl.ds(q0, tile), :]
                do_rows = do_bf[pl.ds(q0, tile), :]
                dq_part = None
                for h in (0, 1):
                    q_m = jnp.where(top, qtt, zt) if h == 0 else jnp.where(top, zt, qtt)
                    do_m = jnp.where(top, dtt, zt) if h == 0 else jnp.where(top, zt, dtt)
                    s = jnp.dot(kt, q_m, preferred_element_type=F32) - cols[h]
                    if masked:
                        s = jnp.where(causal, s, NEG)
                    p = jnp.exp(s - lse_ref[pl.ds((2 * pid + h) * nt + qi, 1), :])
                    dp = jnp.dot(vt, do_m, preferred_element_type=F32)
                    ds = p * (dp - delta[pl.ds(h * nt + qi, 1), :])
                    q_side[pl.ds(h * nt + qi, 1), :] += jnp.sum(ds, axis=0, keepdims=True)
                    folded = ds[:, :128]
                    for b in range(1, reps):
                        folded = folded + ds[:, b * 128:(b + 1) * 128]
                    k_side[h] += folded
                    pb = p.astype(BF16)
                    dsb = ds.astype(BF16)
                    do_h = jnp.where(left, do_rows, zr) if h == 0 else jnp.where(left, zr, do_rows)
                    q_h = jnp.where(left, q_rows, zr) if h == 0 else jnp.where(left, zr, q_rows)
                    dv_acc[...] += jnp.dot(pb, do_h, preferred_element_type=F32)
                    dk_acc[...] += jnp.dot(dsb, q_h, preferred_element_type=F32)
                    part = jnp.dot(kms[h], dsb, preferred_element_type=F32)
                    dq_part = part if dq_part is None else dq_part + part
                dq_t_acc[qi] += dq_part
                return carry

            step(kj, 0, True)
            lax.fori_loop(kj + 1, nt, lambda qi, c: step(qi, c, False), 0)
            dk_ref[pl.ds(k0, tile), :] = dk_acc[...].astype(BF16)
            dv_ref[pl.ds(k0, tile), :] = dv_acc[...].astype(BF16)
            for h in (0, 1):
                dcum_ref[pl.ds((2 * pid + h) * nt + kj, 1), :] = -jnp.sum(k_side[h].T, axis=0, keepdims=True)
            return carry

        lax.fori_loop(0, nt, kv_loop, 0)

        def fin(i, carry):
            r0 = pl.multiple_of(i * tile, tile)
            dq_ref[pl.ds(r0, tile), :] = dq_t_acc[i].T.astype(BF16)
            for h in (0, 1):
                dcum_ref[pl.ds((2 * pid + h) * nt + i, 1), :] += q_side[pl.ds(h * nt + i, 1), :]
            return carry

        lax.fori_loop(0, nt, fin, 0)
        if ns:
            pl.when(pid == pairs - 1)(finish)

    blk = pl.BlockSpec((s_len, HEAD_PAIR), lambda p: (0, p))
    full = pl.BlockSpec(cumr.shape, lambda p: (0, 0))
    transposed = pltpu.VMEM((nt, HEAD_PAIR, tile), BF16)
    sems = [pltpu.SemaphoreType.DMA((3 * ns,))] * 2 if ns else []
    dq, dk, dv, dcum, *arrived = pl.pallas_call(
        body, name="flash_bwd_scatter" if ns else "flash_bwd", grid=(pairs,),
        in_specs=[blk, blk, blk, blk, blk, full, full] + [ANY] * ns,
        out_specs=[blk, blk, blk, full] + [ANY] * ns,
        out_shape=[jax.ShapeDtypeStruct((s_len, width), BF16)] * 3 + [jax.ShapeDtypeStruct(cumr.shape, F32)]
        + [jax.ShapeDtypeStruct(t.shape, t.dtype) for t in outgoing],
        scratch_shapes=[transposed, transposed, transposed, pltpu.VMEM((s_len, HEAD_PAIR), BF16),
                        pltpu.VMEM((2, s_len, HEAD_PAIR), F32), pltpu.VMEM((nt, HEAD_PAIR, tile), F32),
                        pltpu.VMEM((2 * nt, tile), F32), pltpu.VMEM((2 * nt, tile), F32),
                        pltpu.VMEM((tile, HEAD_PAIR), F32), pltpu.VMEM((tile, HEAD_PAIR), F32),
                        pltpu.VMEM((2, tile, HEAD_PAIR), F32)] + sems,
        compiler_params=pltpu.CompilerParams(dimension_semantics=("arbitrary",)),
    )(q, k, v, o, do, lse, cumr, *outgoing)
    return (dq, dk, dv, dcum.reshape(cum.shape), *arrived)


def _out_ln(a, gate, x, w, g, b, name):
    s_len, d = x.shape
    tm = min(512, s_len)

    def body(a_ref, gate_ref, x_ref, w_ref, g_ref, b_ref, xn_ref, xh_ref, rs_ref, yg_ref):
        gt = gate_ref[...]
        yg = (a_ref[...] * (gt * _sigmoid(gt))).astype(BF16)
        yg_ref[...] = yg
        z = ALPHA * x_ref[...] + jnp.dot(yg, w_ref[...], preferred_element_type=F32)
        zc = z - jnp.mean(z, axis=1, keepdims=True)
        rstd = lax.rsqrt(jnp.mean(zc * zc, axis=1, keepdims=True) + LN_EPS)
        xh = zc * rstd
        xh_ref[...] = xh
        xn_ref[...] = xh * g_ref[...] + b_ref[...]
        rs_ref[...] = jnp.broadcast_to(rstd, (tm, 128))

    row = pl.BlockSpec((tm, d), lambda i: (i, 0))
    vec = pl.BlockSpec((1, d), lambda i: (0, 0))
    return pl.pallas_call(
        body, name=name, grid=(s_len // tm,),
        in_specs=[row, row, row, pl.BlockSpec(w.shape, lambda i: (0, 0)), vec, vec],
        out_specs=[row, row, pl.BlockSpec((tm, 128), lambda i: (i, 0)), row],
        out_shape=[jax.ShapeDtypeStruct((s_len, d), F32), jax.ShapeDtypeStruct((s_len, d), F32),
                   jax.ShapeDtypeStruct((s_len, 128), F32), jax.ShapeDtypeStruct((s_len, d), BF16)],
        compiler_params=pltpu.CompilerParams(dimension_semantics=("parallel",)),
    )(a, gate, x, w, g, b)


def _ln_bwd(dout, xh, rs, g, w, gate, a, name, swap=None):
    s_len, d = dout.shape
    tm = min(512, s_len)
    steps = s_len // tm
    ns = 0 if swap is None else 1

    def body(*refs):
        do_ref, xh_ref, rs_ref, g_ref, w_ref, gate_ref, a_ref = refs[:7]
        dz_ref, da_ref, dgate_ref, dg_ref, db_ref = refs[7 + ns:12 + ns]
        if ns:
            start, finish = _swap_steps(refs[7:8], refs[12 + ns:13 + ns], [swap[1]], *refs[13 + ns:])
            pl.when(pl.program_id(0) == 0)(start)

        @pl.when(pl.program_id(0) == 0)
        def _():
            dg_ref[...] = jnp.zeros_like(dg_ref)
            db_ref[...] = jnp.zeros_like(db_ref)

        dout_t = do_ref[...]
        xh_t = xh_ref[...]
        dg_ref[...] += jnp.sum(dout_t * xh_t, axis=0, keepdims=True)
        db_ref[...] += jnp.sum(dout_t, axis=0, keepdims=True)
        dxh = dout_t * g_ref[...]
        m1 = jnp.mean(dxh, axis=1, keepdims=True)
        m2 = jnp.mean(dxh * xh_t, axis=1, keepdims=True)
        dz = rs_ref[:, 0:1] * (dxh - m1 - xh_t * m2)
        dz_ref[...] = dz
        dyg = lax.dot_general(dz.astype(BF16), w_ref[...], NT_DIMS, preferred_element_type=F32)
        gt = gate_ref[...]
        sg = _sigmoid(gt)
        da_ref[...] = dyg * (gt * sg)
        dgate_ref[...] = (dyg * a_ref[...] * (sg * (1.0 + gt * (1.0 - sg)))).astype(BF16)
        if ns:
            pl.when(pl.program_id(0) == steps - 1)(finish)

    row = pl.BlockSpec((tm, d), lambda i: (i, 0))
    vec = pl.BlockSpec((1, d), lambda i: (0, 0))
    extra_out = [jax.ShapeDtypeStruct((N_CHIPS, swap[1][1], LANES), swap[0].dtype)] if ns else []
    return pl.pallas_call(
        body, name=name + "_swap" if ns else name, grid=(steps,),
        in_specs=[row, row, pl.BlockSpec((tm, 128), lambda i: (i, 0)), vec, pl.BlockSpec(w.shape, lambda i: (0, 0)),
                  row, row] + [ANY] * ns,
        out_specs=[row, row, row, vec, vec] + [ANY] * ns,
        out_shape=[jax.ShapeDtypeStruct((s_len, d), F32), jax.ShapeDtypeStruct((s_len, d), F32),
                   jax.ShapeDtypeStruct((s_len, d), BF16), jax.ShapeDtypeStruct((1, d), F32),
                   jax.ShapeDtypeStruct((1, d), F32)] + extra_out,
        scratch_shapes=[pltpu.SemaphoreType.DMA((N_CHIPS,))] * 2 if ns else [],
        compiler_params=pltpu.CompilerParams(dimension_semantics=("arbitrary",)),
    )(dout, xh, rs, g, w, gate, a, *([swap[0]] if ns else []))


def _loss_grad(y, target):
    s_len, d = y.shape
    tm = min(512, s_len)

    def body(y_ref, t_ref, dy_ref, acc_ref):
        @pl.when(pl.program_id(0) == 0)
        def _():
            acc_ref[...] = jnp.zeros_like(acc_ref)

        diff = y_ref[...] - t_ref[...]
        dy_ref[...] = diff * (1.0 / d)
        acc_ref[...] += jnp.sum(diff * diff, axis=0, keepdims=True)

    row = pl.BlockSpec((tm, d), lambda i: (i, 0))
    return pl.pallas_call(
        body, name="loss_grad", grid=(s_len // tm,),
        in_specs=[row, row], out_specs=[row, pl.BlockSpec((1, d), lambda i: (0, 0))],
        out_shape=[jax.ShapeDtypeStruct((s_len, d), F32), jax.ShapeDtypeStruct((1, d), F32)],
        compiler_params=pltpu.CompilerParams(dimension_semantics=("arbitrary",)),
    )(y, target)


def _rnn_fwd(u, conv_w, conv_b, wa, ba, wi, bi, lam):
    s_len, width = u.shape
    tm = min(512, s_len // 2)
    nb = width // RNN_BLOCK

    def body(u_ref, up_ref, cw_ref, cb_ref, wa_ref, ba_ref, wi_ref, bi_ref, lam_ref,
             h_ref, uc_ref, r_ref, i_ref, a_ref, b_scr, h_carry):
        step_id = pl.program_id(0)

        @pl.when(step_id == 0)
        def _():
            h_carry[...] = jnp.zeros_like(h_carry)

        for n in range(nb):
            blk = slice(n * RNN_BLOCK, (n + 1) * RNN_BLOCK)
            ut = u_ref[:, blk]
            prev = jnp.where(step_id > 0, up_ref[:, blk], 0.0)
            uc = cb_ref[:, blk] + cw_ref[3:4, blk] * ut
            for k in (1, 2, 3):
                uc = uc + cw_ref[3 - k:4 - k, blk] * _shift_down(ut, prev, k)
            ucb = uc.astype(BF16)
            r = _sigmoid(jnp.dot(ucb, wa_ref[n], preferred_element_type=F32) + ba_ref[:, blk])
            ig = _sigmoid(jnp.dot(ucb, wi_ref[n], preferred_element_type=F32) + bi_ref[:, blk])
            log_a = -LRU_C * r * _softplus(-lam_ref[:, blk])
            uc_ref[:, blk] = uc
            r_ref[:, blk] = r
            i_ref[:, blk] = ig
            a_ref[:, blk] = jnp.exp(log_a)
            b_scr[:, blk] = jnp.sqrt(_neg_expm1(2.0 * log_a)) * (ig * uc)

        def scan(t, h):
            h = a_ref[pl.ds(t, 1), :] * h + b_scr[pl.ds(t, 1), :]
            h_ref[pl.ds(t, 1), :] = h
            return h

        h_carry[...] = lax.fori_loop(0, tm, scan, h_carry[...], unroll=8)

    row = pl.BlockSpec((tm, width), lambda i: (i, 0))
    prev8 = pl.BlockSpec((8, width), lambda i: (jnp.maximum(i * (tm // 8) - 1, 0), 0))
    vec = pl.BlockSpec((1, width), lambda i: (0, 0))
    mat = pl.BlockSpec(wa.shape, lambda i: (0, 0, 0))
    return pl.pallas_call(
        body, name="rnn_fwd", grid=(s_len // tm,),
        in_specs=[row, prev8, pl.BlockSpec(conv_w.shape, lambda i: (0, 0)), vec, mat, vec, mat, vec, vec],
        out_specs=[row] * 5,
        out_shape=[jax.ShapeDtypeStruct((s_len, width), F32)] * 5,
        scratch_shapes=[pltpu.VMEM((tm, width), F32), pltpu.VMEM((1, width), F32)],
        compiler_params=pltpu.CompilerParams(dimension_semantics=("arbitrary",)),
    )(u, u, conv_w, conv_b, wa, ba, wi, bi, lam)


def _rnn_bwd(dh, a, h, r, ig, uc, lam, wa, wi):
    s_len, width = dh.shape
    tm = min(512, s_len // 2)
    nt = s_len // tm
    nb = width // RNN_BLOCK

    def body(dh_ref, a_ref, h_ref, hp_ref, r_ref, i_ref, uc_ref, lam_ref, wa_ref, wi_ref,
             duc_ref, dwa_ref, dwi_ref, dba_ref, dbi_ref, dlam_ref, g_scr, c_scr, dsp_scr):
        step_id = pl.program_id(0)
        tile_id = nt - 1 - step_id

        @pl.when(step_id == 0)
        def _():
            c_scr[...] = jnp.zeros_like(c_scr)
            dsp_scr[...] = jnp.zeros_like(dsp_scr)
            dwa_ref[...] = jnp.zeros_like(dwa_ref)
            dwi_ref[...] = jnp.zeros_like(dwi_ref)
            dba_ref[...] = jnp.zeros_like(dba_ref)
            dbi_ref[...] = jnp.zeros_like(dbi_ref)

        def scan(j, c):
            t = tm - 1 - j
            g = dh_ref[pl.ds(t, 1), :] + c
            g_scr[pl.ds(t, 1), :] = g
            return a_ref[pl.ds(t, 1), :] * g

        c_scr[...] = lax.fori_loop(0, tm, scan, c_scr[...], unroll=8)

        for n in range(nb):
            blk = slice(n * RNN_BLOCK, (n + 1) * RNN_BLOCK)
            g_t = g_scr[:, blk]
            hp8 = jnp.where(tile_id > 0, hp_ref[:, blk], 0.0)
            h_prev = _shift_down(h_ref[:, blk], hp8, 1)
            av, rv, iv, ucv = a_ref[:, blk], r_ref[:, blk], i_ref[:, blk], uc_ref[:, blk]
            sp = _softplus(-lam_ref[:, blk])
            mag = jnp.sqrt(_neg_expm1(-2.0 * LRU_C * rv * sp))
            d_iu = g_t * mag
            dla = g_t * h_prev * av - g_t * (iv * ucv) * (av * av) / mag
            dsp_scr[:, blk] += jnp.sum(dla * (-LRU_C * rv), axis=0, keepdims=True)
            dra = dla * (-LRU_C * sp) * rv * (1.0 - rv)
            dia = d_iu * ucv * iv * (1.0 - iv)
            drab, diab, ucb = dra.astype(BF16), dia.astype(BF16), ucv.astype(BF16)
            duc_ref[:, blk] = (d_iu * iv
                               + lax.dot_general(drab, wa_ref[n], NT_DIMS, preferred_element_type=F32)
                               + lax.dot_general(diab, wi_ref[n], NT_DIMS, preferred_element_type=F32))
            dwa_ref[n] += lax.dot_general(ucb, drab, TN_DIMS, preferred_element_type=F32)
            dwi_ref[n] += lax.dot_general(ucb, diab, TN_DIMS, preferred_element_type=F32)
            dba_ref[:, blk] += jnp.sum(dra, axis=0, keepdims=True)
            dbi_ref[:, blk] += jnp.sum(dia, axis=0, keepdims=True)

        @pl.when(step_id == nt - 1)
        def _():
            dlam_ref[...] = -dsp_scr[...] * _sigmoid(-lam_ref[...])

    row = pl.BlockSpec((tm, width), lambda i: (nt - 1 - i, 0))
    prev8 = pl.BlockSpec((8, width), lambda i: (jnp.maximum((nt - 1 - i) * (tm // 8) - 1, 0), 0))
    vec = pl.BlockSpec((1, width), lambda i: (0, 0))
    mat = pl.BlockSpec(wa.shape, lambda i: (0, 0, 0))
    return pl.pallas_call(
        body, name="rnn_bwd", grid=(nt,),
        in_specs=[row, row, row, prev8, row, row, row, vec, mat, mat],
        out_specs=[row, mat, mat, vec, vec, vec],
        out_shape=[jax.ShapeDtypeStruct((s_len, width), F32), jax.ShapeDtypeStruct(wa.shape, F32),
                   jax.ShapeDtypeStruct(wa.shape, F32)] + [jax.ShapeDtypeStruct((1, width), F32)] * 3,
        scratch_shapes=[pltpu.VMEM((tm, width), F32), pltpu.VMEM((1, width), F32), pltpu.VMEM((1, width), F32)],
        compiler_params=pltpu.CompilerParams(dimension_semantics=("arbitrary",)),
    )(dh, a, h, h, r, ig, uc, lam, wa, wi)


def _conv_bwd(duc, u, conv_w):
    s_len, width = duc.shape
    tm = min(256, s_len)
    nt = s_len // tm

    def body(d_ref, dn_ref, u_ref, up_ref, cw_ref, du_ref, dcw_ref, dcb_ref):
        step_id = pl.program_id(0)

        @pl.when(step_id == 0)
        def _():
            dcw_ref[...] = jnp.zeros_like(dcw_ref)
            dcb_ref[...] = jnp.zeros_like(dcb_ref)

        for n in range(width // RNN_BLOCK):
            blk = slice(n * RNN_BLOCK, (n + 1) * RNN_BLOCK)
            dt = d_ref[:, blk]
            ut = u_ref[:, blk]
            nxt = jnp.where(step_id < nt - 1, dn_ref[:, blk], 0.0)
            prev = jnp.where(step_id > 0, up_ref[:, blk], 0.0)
            du = cw_ref[3:4, blk] * dt
            dcw_ref[3:4, blk] += jnp.sum(dt * ut, axis=0, keepdims=True)
            for k in (1, 2, 3):
                du = du + cw_ref[3 - k:4 - k, blk] * _shift_up(dt, nxt, k)
                dcw_ref[3 - k:4 - k, blk] += jnp.sum(dt * _shift_down(ut, prev, k), axis=0, keepdims=True)
            du_ref[:, blk] = du.astype(BF16)
            dcb_ref[:, blk] += jnp.sum(dt, axis=0, keepdims=True)

    row = pl.BlockSpec((tm, width), lambda i: (i, 0))
    prev8 = pl.BlockSpec((8, width), lambda i: (jnp.maximum(i * (tm // 8) - 1, 0), 0))
    next8 = pl.BlockSpec((8, width), lambda i: (jnp.minimum((i + 1) * (tm // 8), s_len // 8 - 1), 0))
    return pl.pallas_call(
        body, name="conv_bwd", grid=(nt,),
        in_specs=[row, next8, row, prev8, pl.BlockSpec(conv_w.shape, lambda i: (0, 0))],
        out_specs=[row, pl.BlockSpec(conv_w.shape, lambda i: (0, 0)), pl.BlockSpec((1, width), lambda i: (0, 0))],
        out_shape=[jax.ShapeDtypeStruct((s_len, width), BF16), jax.ShapeDtypeStruct(conv_w.shape, F32),
                   jax.ShapeDtypeStruct((1, width), F32)],
        compiler_params=pltpu.CompilerParams(dimension_semantics=("arbitrary",)),
    )(duc, duc, u, u, conv_w)


def _pair_sum(core, grads, theirs, first_row, out_dtype):
    n, half, _ = theirs.shape
    tb = 128 if half % 128 == 0 and first_row % 128 == 0 else half
    assert first_row % tb == 0 and half % tb == 0

    def body(c_ref, a_ref, b_ref, o_ref):
        o_ref[...] = (a_ref[...] + b_ref[...]).astype(out_dtype)

    blk = pl.BlockSpec((n, tb, LANES), lambda i, c: (0, i, 0))
    mine = pl.BlockSpec((n, tb, LANES), lambda i, c: (0, first_row // tb + c[0] * (half // tb) + i, 0))
    return pl.pallas_call(
        body, name="pair_sum",
        grid_spec=pltpu.PrefetchScalarGridSpec(
            num_scalar_prefetch=1, grid=(half // tb,), in_specs=[mine, blk], out_specs=blk),
        out_shape=jax.ShapeDtypeStruct((n, half, LANES), out_dtype),
        compiler_params=pltpu.CompilerParams(dimension_semantics=("parallel",)),
    )(core, grads, theirs)


def _sum_chips(parts):
    rows = parts.shape[1]
    tb = 128 if rows % 128 == 0 else rows

    def body(p_ref, o_ref):
        o_ref[...] = ((p_ref[0].astype(F32) + p_ref[1].astype(F32)) + p_ref[2].astype(F32)) + p_ref[3].astype(F32)

    return pl.pallas_call(
        body, name="chip_sum", grid=(rows // tb,),
        in_specs=[pl.BlockSpec((N_CHIPS, tb, LANES), lambda i: (0, i, 0))],
        out_specs=pl.BlockSpec((tb, LANES), lambda i: (i, 0)),
        out_shape=jax.ShapeDtypeStruct((rows, LANES), F32),
        compiler_params=pltpu.CompilerParams(dimension_semantics=("parallel",)),
    )(parts)


def _adamw(w, g, m, v, lead_per_block, rows_per_block):
    n, rows, cols = w.shape
    blk = pl.BlockSpec((lead_per_block, rows_per_block, cols), lambda i, j: (i, j, 0))

    def body(w_ref, g_ref, m_ref, v_ref, d_ref, nm_ref, nv_ref):
        gt = g_ref[...]
        nm = ADAM_B1 * m_ref[...] + (1.0 - ADAM_B1) * gt
        nv = ADAM_B2 * v_ref[...] + (1.0 - ADAM_B2) * (gt * gt)
        m_hat = nm / (1.0 - ADAM_B1 ** ADAM_STEP)
        v_hat = nv / (1.0 - ADAM_B2 ** ADAM_STEP)
        d_ref[...] = -ADAM_LR * (m_hat / (jnp.sqrt(v_hat) + ADAM_EPS) + ADAM_WD * w_ref[...])
        nm_ref[...] = nm
        nv_ref[...] = nv

    return pl.pallas_call(
        body, name="adamw", grid=(n // lead_per_block, rows // rows_per_block), in_specs=[blk] * 4,
        out_specs=[blk] * 3,
        out_shape=[jax.ShapeDtypeStruct(w.shape, F32)] * 3,
        compiler_params=pltpu.CompilerParams(dimension_semantics=("parallel", "parallel")),
    )(w, g, m, v)


def _place():
    x, y, c = lax.axis_index("x"), lax.axis_index("y"), lax.axis_index("c")
    return x, y, c, [(1 - x, y), (x, 1 - y), (1 - x, 1 - y)]


ANY = pl.BlockSpec(memory_space=pl.ANY)
COPY_PARTS = 4


def _remote_parts(src_of, dst_of, rows, send_sem, recv_sem, to, begin=True):
    parts = COPY_PARTS if rows % (16 * COPY_PARTS) == 0 else 1
    step = rows // parts
    for i in range(parts if begin is not False else 0):
        pltpu.make_async_remote_copy(src_ref=src_of(i * step, step), dst_ref=dst_of(i * step, step),
                                     send_sem=send_sem, recv_sem=recv_sem, device_id=to, device_id_type=MESH).start()
    if begin == "only":
        return None
    return pltpu.make_async_remote_copy(src_ref=src_of(0, rows), dst_ref=dst_of(0, rows), send_sem=send_sem,
                                        recv_sem=recv_sem, device_id=to, device_id_type=MESH)


GATHER_SEMS = 7


def _gather_steps(p_refs, o_refs, send_sems, recv_sems):
    x, y, c, chips = _place()
    me = 2 * x + y

    def half(ref, which):
        rows = ref.shape[0] // 2
        return ref.at[pl.ds(which * rows, rows)]

    def copy(k, src, dst, to):
        return pltpu.make_async_remote_copy(src_ref=src, dst_ref=dst, send_sem=send_sems.at[k],
                                            recv_sem=recv_sems.at[k], device_id=to, device_id_type=MESH)

    def first_copies():
        out = []
        for b, (p_ref, o_ref) in enumerate(zip(p_refs, o_refs)):
            for j, (cx, cy) in enumerate(chips):
                out.append(copy(GATHER_SEMS * b + j, half(p_ref, c), half(o_ref.at[me], c), (cx, cy, c)))
            out.append(copy(GATHER_SEMS * b + 6, p_ref, o_ref.at[me], (x, y, 1 - c)))
        return out

    def passed_copies():
        out = []
        for b, o_ref in enumerate(o_refs):
            for j, (cx, cy) in enumerate(chips):
                landed = half(o_ref.at[2 * cx + cy], c)
                out.append(copy(GATHER_SEMS * b + 3 + j, landed, landed, (x, y, 1 - c)))
        return out

    def start():
        for cp in first_copies():
            cp.start()

    def forward():
        for b, o_ref in enumerate(o_refs):
            for j, (cx, cy) in enumerate(chips):
                landed = half(o_ref.at[2 * cx + cy], c)
                copy(GATHER_SEMS * b + j, landed, landed, (x, y, c)).wait_recv()
        for cp in passed_copies():
            cp.start()

    def finish():
        for b, o_ref in enumerate(o_refs):
            for j, (cx, cy) in enumerate(chips):
                other = half(o_ref.at[2 * cx + cy], 1 - c)
                copy(GATHER_SEMS * b + 3 + j, other, other, (x, y, c)).wait_recv()
            copy(GATHER_SEMS * b + 6, o_ref.at[me], o_ref.at[me], (x, y, c)).wait_recv()
        for cp in first_copies() + passed_copies():
            cp.wait_send()

    return start, forward, finish


def _gather_chips(shards):
    nb = len(shards)

    def body(*refs):
        for step in _gather_steps(refs[:nb], refs[nb:2 * nb], *refs[2 * nb:]):
            step()

    return pl.pallas_call(
        body, name="gather_chips", in_specs=[ANY] * nb, out_specs=[ANY] * nb,
        out_shape=[jax.ShapeDtypeStruct((N_CHIPS,) + s.shape, s.dtype) for s in shards],
        scratch_shapes=[pltpu.SemaphoreType.DMA((GATHER_SEMS * nb,)), pltpu.SemaphoreType.DMA((GATHER_SEMS * nb,))],
    )(*shards)


def _swap_steps(g_refs, t_refs, spans, send_sems, recv_sems):
    x, y, c, _ = _place()

    def gives(begin):
        return [_remote_parts(
            lambda r0, m, g_ref=g_ref, j=j, base=first_row + (1 - c) * half: g_ref.at[j, pl.ds(base + r0, m), :],
            lambda r0, m, t_ref=t_ref, j=j: t_ref.at[j, pl.ds(r0, m), :],
            half, send_sems.at[b * N_CHIPS + j], recv_sems.at[b * N_CHIPS + j], (x, y, 1 - c), begin)
            for b, (g_ref, t_ref, (first_row, half)) in enumerate(zip(g_refs, t_refs, spans)) for j in range(N_CHIPS)]

    def start():
        gives("only")

    def finish():
        for give in gives(False):
            give.wait()

    return start, finish


def _swap_halves(bufs, spans):
    nb = len(bufs)

    def body(*refs):
        for step in _swap_steps(refs[:nb], refs[nb:2 * nb], spans, *refs[2 * nb:]):
            step()

    return pl.pallas_call(
        body, name="swap_halves", in_specs=[ANY] * nb, out_specs=[ANY] * nb,
        out_shape=[jax.ShapeDtypeStruct((b.shape[0], half, b.shape[2]), b.dtype) for b, (_, half) in zip(bufs, spans)],
        scratch_shapes=[pltpu.SemaphoreType.DMA((nb * N_CHIPS,)), pltpu.SemaphoreType.DMA((nb * N_CHIPS,))],
    )(*bufs)


def _scatter_steps(t_refs, o_refs, send_sems, recv_sems):
    x, y, c, chips = _place()
    me = 2 * x + y

    def slot(ref, chip):
        return lambda r0, n: ref.at[chip, pl.ds(r0, n), :]

    def sends(begin):
        return [_remote_parts(slot(t_ref, 2 * cx + cy), slot(o_ref, me), t_ref.shape[1], send_sems.at[3 * b + j],
                              recv_sems.at[3 * b + j], (cx, cy, c), begin)
                for b, (t_ref, o_ref) in enumerate(zip(t_refs, o_refs)) for j, (cx, cy) in enumerate(chips)]

    def start():
        sends("only")

    def finish():
        for b, o_ref in enumerate(o_refs):
            for j, (cx, cy) in enumerate(chips):
                landed = o_ref.at[2 * cx + cy]
                pltpu.make_async_remote_copy(src_ref=landed, dst_ref=landed, send_sem=send_sems.at[3 * b + j],
                                             recv_sem=recv_sems.at[3 * b + j], device_id=(x, y, c),
                                             device_id_type=MESH).wait_recv()
        for cp in sends(False):
            cp.wait_send()

    return start, finish


def _scatter_chips(bufs):
    nb = len(bufs)

    def body(*refs):
        for step in _scatter_steps(refs[:nb], refs[nb:2 * nb], *refs[2 * nb:]):
            step()

    return pl.pallas_call(
        body, name="scatter_chips", in_specs=[ANY] * nb, out_specs=[ANY] * nb,
        out_shape=[jax.ShapeDtypeStruct(b.shape, b.dtype) for b in bufs],
        scratch_shapes=[pltpu.SemaphoreType.DMA((3 * nb,)), pltpu.SemaphoreType.DMA((3 * nb,))],
    )(*bufs)


def _give_sibling(bufs):
    nb = len(bufs)

    def body(*refs):
        h_refs, o_refs, (send_sems, recv_sems) = refs[:nb], refs[nb:2 * nb], refs[2 * nb:]
        x, y, c, _ = _place()
        gives = [_remote_parts(lambda r0, n, h_ref=h_ref: h_ref.at[pl.ds(r0, n), :],
                               lambda r0, n, o_ref=o_ref: o_ref.at[pl.ds(r0, n), :],
                               h_ref.shape[0], send_sems.at[b], recv_sems.at[b], (x, y, 1 - c))
                 for b, (h_ref, o_ref) in enumerate(zip(h_refs, o_refs))]
        for give in gives:
            give.wait()

    return pl.pallas_call(
        body, name="give_sibling", in_specs=[ANY] * nb, out_specs=[ANY] * nb,
        out_shape=[jax.ShapeDtypeStruct(b.shape, b.dtype) for b in bufs],
        scratch_shapes=[pltpu.SemaphoreType.DMA((nb,)), pltpu.SemaphoreType.DMA((nb,))],
    )(*bufs)


def _pack(arrays, dtype, row_align):
    flat = []
    for arr in arrays:
        v = arr.reshape(-1)
        flat.append(jnp.pad(v, (0, (-v.shape[0]) % LANES)))
    total = sum(f.shape[0] for f in flat) // LANES
    pad_rows = (-total) % row_align
    if pad_rows:
        flat.append(jnp.zeros((pad_rows * LANES,), dtype))
    return jnp.concatenate(flat).reshape(-1, LANES)


def _unpack(buf, shapes, rows_align=1):
    lead = buf.shape[:-2]
    flat = buf.reshape(lead + (-1,))
    out, off = [], 0
    for shape in shapes:
        size = 1
        for dim in shape:
            size *= dim
        out.append(flat[..., off:off + size].reshape(lead + tuple(shape)))
        off += size + (-size) % (LANES * rows_align)
    return out


def _from_chips(parts, axis):
    return jnp.concatenate([parts[j] for j in range(N_CHIPS)], axis=axis)


def kernel(x, ln_g, ln_b, attn_w_in, attn_b_f, attn_w_out, rnn_w_in, rnn_conv_w, rnn_conv_b, rnn_w_a, rnn_b_a, rnn_w_i, rnn_b_i, rnn_lambda, rnn_w_out, loss_target, m_ln_g, m_ln_b, m_attn_w_in, m_attn_b_f, m_attn_w_out, m_rnn_w_in, m_rnn_conv_w, m_rnn_conv_b, m_rnn_w_a, m_rnn_b_a, m_rnn_w_i, m_rnn_b_i, m_rnn_lambda, m_rnn_w_out, v_ln_g, v_ln_b, v_attn_w_in, v_attn_b_f, v_attn_w_out, v_rnn_w_in, v_rnn_conv_w, v_rnn_conv_b, v_rnn_w_a, v_rnn_b_a, v_rnn_w_i, v_rnn_b_i, v_rnn_lambda, v_rnn_w_out):
    s_len, d = x.shape[1], x.shape[2]
    xs = x.reshape(s_len, d)
    target = loss_target.reshape(s_len, d)
    heads = attn_b_f.shape[1]
    qkvg = attn_w_in.shape[2] * N_CHIPS - heads

    me = 2 * lax.axis_index("x") + lax.axis_index("y")
    core = lax.axis_index("c").astype(jnp.int32)
    small = [rnn_conv_w, rnn_conv_b, rnn_b_a, rnn_b_i, rnn_lambda]
    a_in, a_out = attn_w_in.astype(BF16), attn_w_out.astype(BF16)
    later = [a_in[1], a_out] + [w.astype(BF16) for w in (rnn_w_in, rnn_w_a, rnn_w_i, rnn_w_out)]
    got_in, got_small = _gather_chips([a_in[0], _pack(small, F32, 16)])
    conv_w, conv_b, b_a, b_i, lam = [
        _from_chips(p, ax) for p, ax in zip(_unpack(got_small, [w.shape for w in small]), (2, 1, 1, 1, 1))]

    def attn_in_weights(w_in):
        full = jnp.concatenate([w_in[j] for j in range(N_CHIPS)], axis=1)
        return jnp.concatenate([full[:, :d] * (HEAD_DIM ** -0.5), full[:, d:qkvg],
                                jnp.pad(full[:, qkvg:], ((0, 0), (0, 128 - heads)))], axis=1).astype(BF16)

    w_cat = [attn_in_weights(got_in), None]
    b_f = jnp.pad(attn_b_f, ((0, 0), (0, 128 - heads)))

    saved = []
    cur = xs
    for layer in range(DEPTH):
        idx = layer // 2
        g_l, b_l = ln_g[layer:layer + 1], ln_b[layer:layer + 1]
        if layer % 2 == 0:
            q, k, v, gate, fl = _proj(cur, w_cat[idx], [(0, d, BF16), (d, d, BF16), (2 * d, d, BF16),
                                                         (3 * d, d, F32), (4 * d, 128, F32)], "attn_proj")
            cum, sneg = _fcum(fl, b_f[idx:idx + 1])
            o, lse, *rest = _flash_fwd(q, k, v, cum, later if layer == 0 else ())
            if layer == 0:
                w_cat[1] = attn_in_weights(rest[0])
                w_out_a = jnp.moveaxis(rest[1], 0, 1).reshape(2, d, d)
                w_in_r = jnp.moveaxis(rest[2], 0, 2).reshape(2, d, 2 * d)
                w_a = jnp.transpose(rest[3], (1, 2, 0, 3, 4)).reshape(2, -1, RNN_BLOCK, RNN_BLOCK)
                w_i = jnp.transpose(rest[4], (1, 2, 0, 3, 4)).reshape(2, -1, RNN_BLOCK, RNN_BLOCK)
                w_out_r = jnp.moveaxis(rest[5], 0, 1).reshape(2, d, d)
            nxt, xh, rs, yg = _out_ln(o, gate, cur, w_out_a[idx], g_l, b_l, "out_ln")
            saved.append(dict(x=cur, q=q, k=k, v=v, gate=gate, cum=cum, sneg=sneg, a=o, lse=lse, xh=xh, rs=rs, yg=yg))
        else:
            u, gate = _proj(cur, w_in_r[idx], [(0, d, F32), (d, d, F32)], "rnn_proj")
            vecs = [t[idx:idx + 1] for t in (conv_b, b_a, b_i, lam)]
            hseq, uc, r, ig, a = _rnn_fwd(u, conv_w[idx], vecs[0], w_a[idx], vecs[1], w_i[idx], vecs[2], vecs[3])
            nxt, xh, rs, yg = _out_ln(hseq, gate, cur, w_out_r[idx], g_l, b_l, "out_ln")
            saved.append(dict(x=cur, u=u, gate=gate, uc=uc, r=r, ig=ig, av=a, a=hseq, xh=xh, rs=rs, yg=yg, lam=vecs[3]))
        cur = nxt

    dout, sq = _loss_grad(cur, target)
    loss = lax.psum(0.5 * jnp.sum(sq) / d, ("x", "y", "c"))

    g_ln_g, g_ln_b = [None] * DEPTH, [None] * DEPTH
    g_attn = [None] * 2
    g_rnn = [None] * 2
    slots = None
    core1 = core.reshape(1)
    late_half = (SLOT_ROWS - LATE_ROWS) // 2

    def by_chip(t, axis):
        shape = t.shape[:axis] + (N_CHIPS, t.shape[axis] // N_CHIPS) + t.shape[axis + 1:]
        flat = jnp.moveaxis(t.reshape(shape), axis, 0).reshape(N_CHIPS, -1)
        return jnp.pad(flat, ((0, 0), (0, (-flat.shape[1]) % (8 * LANES)))).reshape(N_CHIPS, -1, LANES)

    def both(key):
        return jnp.stack([it[key] for it in g_rnn])

    for layer in reversed(range(DEPTH)):
        idx = layer // 2
        sv = saved[layer]
        swap = None
        if layer == 0:
            gates = jnp.concatenate([by_chip(both("w_a"), 2), by_chip(both("w_i"), 2)], axis=1)
            slots = lax.dynamic_update_slice(slots, gates, (0, ROWS_GATES, 0))
            swap = (slots, (LATE_ROWS, late_half))
        w_out = w_out_a[idx] if layer % 2 == 0 else w_out_r[idx]
        dz, da, dgate, dg, db, *theirs_late = _ln_bwd(dout, sv["xh"], sv["rs"], ln_g[layer:layer + 1], w_out,
                                                      sv["gate"], sv["a"], "ln_bwd", swap)
        if layer == 0:
            pair_late = _pair_sum(core1, slots, theirs_late[0], LATE_ROWS, BF16)
        g_ln_g[layer], g_ln_b[layer] = dg, db
        slots = _dw_out(slots, sv["yg"], dz, (ROWS_ATTN_OUT if layer % 2 == 0 else ROWS_RNN_OUT)[idx])
        if layer % 2 == 0:
            dq, dk, dv, dcum, *arrived = _flash_bwd(sv["q"], sv["k"], sv["v"], sv["a"], da, sv["lse"], sv["cum"],
                                                    [pair_late] if layer == 0 else ())
            dlogit, dbf = _fbwd(dcum, sv["sneg"])
            pieces = [dq, dk, dv, dgate, dlogit]
            dout = _mm_nt(dz, [(p, j * d) for j, p in enumerate(pieces)], w_cat[idx], "attn_dx")
            slots, d_w_f = _dw_attn_in(slots, sv["x"], pieces[:4], dlogit, idx)
            g_attn[idx] = dict(w_f=d_w_f[:, :heads], b_f=dbf[:, 0])
        else:
            duc, d_wa, d_wi, d_ba, d_bi, d_lam = _rnn_bwd(da, sv["av"], sv["a"], sv["r"], sv["ig"], sv["uc"], sv["lam"],
                                                          w_a[idx], w_i[idx])
            du, d_cw, d_cb = _conv_bwd(duc, sv["u"], conv_w[idx])
            dout = _mm_nt(dz, [(du, 0), (dgate, d)], w_in_r[idx], "rnn_dx")
            slots = _dw_rnn_in(slots, sv["x"], (du, dgate), idx)
            g_rnn[idx] = dict(conv_w=d_cw, conv_b=d_cb[0], w_a=d_wa, b_a=d_ba[0], w_i=d_wi, b_i=d_bi[0], lam=d_lam[0])
    grad_x = dout.reshape(x.shape)

    def everywhere(t):
        flat = t.reshape(-1)
        flat = jnp.pad(flat, (0, (-flat.shape[0]) % (8 * LANES))).reshape(-1, LANES)
        return jnp.broadcast_to(flat[None], (N_CHIPS,) + flat.shape)

    in_rows = jnp.stack([slots[1:, r:r + d, :heads] for r in ROWS_ATTN_IN], axis=1)
    edges = jnp.concatenate([in_rows, jnp.stack([it["w_f"] for it in g_attn])[None]])
    rows = [everywhere(edges), by_chip(both("conv_w"), 2),
            by_chip(both("conv_b"), 1), by_chip(both("b_a"), 1), by_chip(both("b_i"), 1), by_chip(both("lam"), 1),
            everywhere(jnp.concatenate(g_ln_g)), everywhere(jnp.concatenate(g_ln_b)),
            everywhere(jnp.stack([it["b_f"] for it in g_attn]))]
    used = sum(r.shape[1] for r in rows)
    small = jnp.concatenate(rows + [jnp.zeros((N_CHIPS, (-used) % 32, LANES), F32)], axis=1)

    spans = [(0, LATE_ROWS // 2), (0, small.shape[1] // 2)]
    theirs = _swap_halves([slots, small], spans)
    pair = [_pair_sum(core1, slots, theirs[0], 0, BF16), _pair_sum(core1, small, theirs[1], 0, F32)]
    summed = []
    for mine_all, got in zip([pair_late] + pair, list(arrived) + list(_scatter_chips(pair))):
        own = lax.dynamic_index_in_dim(mine_all, me, 0, keepdims=False)
        summed.append(_sum_chips(lax.dynamic_update_index_in_dim(got, own, me, 0)))
    late, early, small_sum = [
        jnp.concatenate([jnp.where(core == 0, mine, other), jnp.where(core == 0, other, mine)])
        for mine, other in zip(summed, _give_sibling(summed))]

    def rows_at(first, n):
        return early[first:first + n] if first < LATE_ROWS else late[first - LATE_ROWS:first - LATE_ROWS + n]

    g_in_group = jnp.stack([rows_at(r, d) for r in ROWS_ATTN_IN])
    g_w_out_a = jnp.stack([rows_at(r, d // N_CHIPS) for r in ROWS_ATTN_OUT])
    g_w_out_r = jnp.stack([rows_at(r, d // N_CHIPS) for r in ROWS_RNN_OUT])
    folded = jnp.stack([rows_at(r, d // 2) for r in ROWS_RNN_IN])
    g_w_in_r = jnp.concatenate([folded[:, :, :d // 2], folded[:, :, d // 2:]], axis=1)
    gate_rows = rnn_w_a.size // LANES
    g_w_a = rows_at(ROWS_GATES, gate_rows).reshape(rnn_w_a.shape)
    g_w_i = rows_at(ROWS_GATES + gate_rows, gate_rows).reshape(rnn_w_i.shape)
    (g_edges, g_conv_w, g_conv_b, g_b_a, g_b_i, g_lam, g_ln_g_sum, g_ln_b_sum,
     g_b_f) = _unpack(small_sum, [
        (N_CHIPS, 2, d, heads), rnn_conv_w.shape, rnn_conv_b.shape, rnn_b_a.shape,
        rnn_b_i.shape, rnn_lambda.shape, ln_g.shape, ln_b.shape, attn_b_f.shape], rows_align=8)
    wide = jnp.concatenate([g_in_group, lax.dynamic_index_in_dim(g_edges, me, 0, keepdims=False)], axis=2)
    g_w_in_a = lax.dynamic_slice(wide, (0, 0, (heads // N_CHIPS) * me), attn_w_in.shape)

    def adam_nd(w, g, m, v, view, rows_per_block):
        outs = _adamw(w.reshape(view), g.reshape(view), m.reshape(view), v.reshape(view), 1, rows_per_block)
        return [t.reshape(w.shape) for t in outs]

    turned = [jnp.transpose(t, (2, 0, 1)) for t in (attn_w_in, g_w_in_a, m_attn_w_in, v_attn_w_in)]
    upd = {
        "attn_w_in": [jnp.transpose(t, (1, 2, 0)) for t in _adamw(*turned, attn_w_in.shape[2] // N_CHIPS, 2)],
        "attn_w_out": adam_nd(attn_w_out, g_w_out_a, m_attn_w_out, v_attn_w_out, attn_w_out.shape, 256),
        "rnn_w_in": adam_nd(rnn_w_in, g_w_in_r, m_rnn_w_in, v_rnn_w_in, rnn_w_in.shape, 512),
        "rnn_w_a": adam_nd(rnn_w_a, g_w_a, m_rnn_w_a, v_rnn_w_a, (1, -1, RNN_BLOCK), 512),
        "rnn_w_i": adam_nd(rnn_w_i, g_w_i, m_rnn_w_i, v_rnn_w_i, (1, -1, RNN_BLOCK), 512),
        "rnn_w_out": adam_nd(rnn_w_out, g_w_out_r, m_rnn_w_out, v_rnn_w_out, rnn_w_out.shape, 256),
    }
    smalls = [rnn_conv_w, rnn_conv_b, rnn_b_a, rnn_b_i, rnn_lambda, ln_g, ln_b, attn_b_f]
    g_small = [g_conv_w, g_conv_b, g_b_a, g_b_i, g_lam, g_ln_g_sum, g_ln_b_sum, g_b_f]
    m_small = [m_rnn_conv_w, m_rnn_conv_b, m_rnn_b_a, m_rnn_b_i, m_rnn_lambda, m_ln_g, m_ln_b, m_attn_b_f]
    v_small = [v_rnn_conv_w, v_rnn_conv_b, v_rnn_b_a, v_rnn_b_i, v_rnn_lambda, v_ln_g, v_ln_b, v_attn_b_f]
    packs = [_pack(t, F32, 16)[None] for t in (smalls, g_small, m_small, v_small)]
    small_out = [_unpack(t[0], [w.shape for w in smalls]) for t in _adamw(*packs, 1, 16)]
    for k, name in enumerate(("rnn_conv_w", "rnn_conv_b", "rnn_b_a", "rnn_b_i", "rnn_lambda", "ln_g", "ln_b",
                              "attn_b_f")):
        upd[name] = [small_out[0][k], small_out[1][k], small_out[2][k]]
    grad = {"attn_w_in": g_w_in_a, "attn_w_out": g_w_out_a, "rnn_w_in": g_w_in_r, "rnn_w_a": g_w_a, "rnn_w_i": g_w_i,
            "rnn_w_out": g_w_out_r, "rnn_conv_w": g_conv_w, "rnn_conv_b": g_conv_b, "rnn_b_a": g_b_a,
            "rnn_b_i": g_b_i, "rnn_lambda": g_lam, "ln_g": g_ln_g_sum, "ln_b": g_ln_b_sum, "attn_b_f": g_b_f}
    names = ("ln_g", "ln_b", "attn_w_in", "attn_b_f", "attn_w_out", "rnn_w_in", "rnn_conv_w", "rnn_conv_b",
             "rnn_w_a", "rnn_b_a", "rnn_w_i", "rnn_b_i", "rnn_lambda", "rnn_w_out")
    return (loss, grad_x, *[grad[n] for n in names], *[upd[n][0] for n in names], *[upd[n][1] for n in names],
            *[upd[n][2] for n in names])
```

```python
import jax
import jax.numpy as jnp
from jax import lax
from jax.experimental import pallas as pl
from jax.experimental.pallas import tpu as pltpu

F32 = jnp.float32
BF16 = jnp.bfloat16
MESH = pl.DeviceIdType.MESH

DEPTH = 4
HEAD_DIM = 64
HEAD_PAIR = 2 * HEAD_DIM
RNN_BLOCK = 256
CONV_WIDTH = 4
LRU_C = 8.0
ALPHA = (2.0 * DEPTH) ** 0.25
LN_EPS = 1e-5
ADAM_LR, ADAM_B1, ADAM_B2, ADAM_EPS, ADAM_WD, ADAM_STEP = 0.001, 0.9, 0.999, 1e-08, 0.01, 10
N_CHIPS = 4
LANES = 1024
NEG = -1e30

NT_DIMS = (((1,), (1,)), ((), ()))
TN_DIMS = (((0,), (0,)), ((), ()))


def _sigmoid(z):
    return 1.0 / (1.0 + jnp.exp(-z))


def _softplus(z):
    return jnp.maximum(z, 0.0) + jnp.log(1.0 + jnp.exp(-jnp.abs(z)))


def _neg_expm1(y):
    poly = y * (1.0 + y * (0.5 + y * (1.0 / 6.0 + y * (1.0 / 24.0))))
    return -jnp.where(y > -0.05, poly, jnp.exp(y) - 1.0)


def _shift_down(cur, prev8, k):
    n = cur.shape[0]
    row = lax.broadcasted_iota(jnp.int32, cur.shape, 0)
    fix = jnp.tile(pltpu.roll(prev8, k, 0), (n // 8, 1))
    return jnp.where(row < k, fix, pltpu.roll(cur, k, 0))


def _shift_up(cur, next8, k):
    n = cur.shape[0]
    row = lax.broadcasted_iota(jnp.int32, cur.shape, 0)
    fix = jnp.tile(pltpu.roll(next8, 8 - k, 0), (n // 8, 1))
    return jnp.where(row >= n - k, fix, pltpu.roll(cur, n - k, 0))


def _proj(x, w, outs, name):
    s_len, d = x.shape
    tm = min(512, s_len)

    def body(x_ref, w_ref, *o_refs):
        xb = x_ref[...].astype(BF16)
        for (c0, wd, dt), o_ref in zip(outs, o_refs):
            step = min(wd, 512)
            for cc in range(0, wd, step):
                o_ref[:, cc:cc + step] = jnp.dot(
                    xb, w_ref[:, c0 + cc:c0 + cc + step], preferred_element_type=F32).astype(dt)

    return pl.pallas_call(
        body, name=name, grid=(s_len // tm,),
        in_specs=[pl.BlockSpec((tm, d), lambda i: (i, 0)), pl.BlockSpec(w.shape, lambda i: (0, 0))],
        out_specs=[pl.BlockSpec((tm, wd), lambda i: (i, 0)) for _, wd, _ in outs],
        out_shape=[jax.ShapeDtypeStruct((s_len, wd), dt) for _, wd, dt in outs],
        compiler_params=pltpu.CompilerParams(dimension_semantics=("parallel",)),
    )(x, w)


def _mm_nt(dz, pieces, w, name, outgoing=()):
    s_len, d = dz.shape
    tm = min(256, s_len)
    steps = s_len // tm
    n = len(pieces)
    ns = len(outgoing)
    cols = [(c0, p.shape[1]) for p, c0 in pieces]

    def body(*refs):
        dz_ref, p_refs, w_ref, o_ref = refs[0], refs[1:1 + n], refs[1 + n], refs[2 + n + ns]
        if ns:
            start, finish = _scatter_steps(refs[2 + n:2 + n + ns], refs[3 + n + ns:3 + n + 2 * ns],
                                           *refs[3 + n + 2 * ns:])
            pl.when(pl.program_id(0) == 0)(start)
        acc = ALPHA * dz_ref[...]
        for (c0, wd), p_ref in zip(cols, p_refs):
            acc = acc + lax.dot_general(p_ref[...], w_ref[:, c0:c0 + wd], NT_DIMS, preferred_element_type=F32)
        o_ref[...] = acc
        if ns:
            pl.when(pl.program_id(0) == steps - 1)(finish)

    out, *arrived = pl.pallas_call(
        body, name=name + "_scatter" if ns else name, grid=(steps,),
        in_specs=[pl.BlockSpec((tm, d), lambda i: (i, 0))]
        + [pl.BlockSpec((tm, wd), lambda i: (i, 0)) for _, wd in cols]
        + [pl.BlockSpec(w.shape, lambda i: (0, 0))] + [ANY] * ns,
        out_specs=[pl.BlockSpec((tm, d), lambda i: (i, 0))] + [ANY] * ns,
        out_shape=[jax.ShapeDtypeStruct((s_len, d), F32)] + [jax.ShapeDtypeStruct(t.shape, t.dtype) for t in outgoing],
        scratch_shapes=[pltpu.SemaphoreType.DMA((3 * ns,))] * 2 if ns else [],
        compiler_params=pltpu.CompilerParams(dimension_semantics=("arbitrary" if ns else "parallel",)),
    )(dz, *[p for p, _ in pieces], w, *outgoing)
    return (out, *arrived)


ROWS_ATTN_IN = (0, 2048)
ROWS_ATTN_OUT = (1024, 1280)
ROWS_RNN_OUT = (1536, 1792)
ROWS_RNN_IN = (3072, 3584)
ROWS_GATES = 4096
LATE_ROWS = 1280
SLOT_ROWS = 4352


DW_ROWS = 1024


def _dw_call(body, name, slots, operands, specs, out_block, out_index, grid, semantics):
    return pl.pallas_call(
        body, name=name, grid=grid,
        in_specs=specs if slots is None else [ANY] + specs,
        out_specs=pl.BlockSpec(out_block, out_index),
        out_shape=jax.ShapeDtypeStruct((N_CHIPS, SLOT_ROWS, LANES), F32),
        input_output_aliases={} if slots is None else {0: 0},
        compiler_params=pltpu.CompilerParams(dimension_semantics=semantics),
    )(*(operands if slots is None else [slots] + operands))


def _dw_attn_in(slots, x, pieces, forget, layer):
    s_len, d = x.shape
    ts = min(s_len, DW_ROWS)
    steps = s_len // ts
    half = d // 2

    def body(g_ref, x_ref, p0, p1, p2, p3, f_ref, o_ref, wf_ref):
        lanes, step = pl.program_id(0), pl.program_id(1)

        @pl.when(step == 0)
        def _():
            o_ref[...] = jnp.zeros_like(o_ref)

        xb = x_ref[...].astype(BF16)
        for j, p_ref in enumerate((p0, p1, p2, p3)):
            o_ref[j] += lax.dot_general(xb, p_ref[...], TN_DIMS, preferred_element_type=F32)

        @pl.when(step == steps - 1)
        def _():
            o_ref[0] = o_ref[0] * (HEAD_DIM ** -0.5)

        @pl.when(lanes == 0)
        def _():
            @pl.when(step == 0)
            def _():
                wf_ref[...] = jnp.zeros_like(wf_ref)

            wf_ref[...] += lax.dot_general(xb, f_ref[...], TN_DIMS, preferred_element_type=F32)

    return pl.pallas_call(
        body, name="dw_attn_in", grid=(2, steps),
        in_specs=[ANY, pl.BlockSpec((ts, d), lambda i, s: (s, 0))] + [pl.BlockSpec((ts, half), lambda i, s: (s, i))] * 4
        + [pl.BlockSpec((ts, 128), lambda i, s: (s, 0))],
        out_specs=[pl.BlockSpec((N_CHIPS, d, half), lambda i, s: (0, ROWS_ATTN_IN[layer] // d, i)),
                   pl.BlockSpec((d, 128), lambda i, s: (0, 0))],
        out_shape=[jax.ShapeDtypeStruct((N_CHIPS, SLOT_ROWS, LANES), F32), jax.ShapeDtypeStruct((d, 128), F32)],
        input_output_aliases={0: 0},
        compiler_params=pltpu.CompilerParams(dimension_semantics=("arbitrary", "arbitrary")),
    )(slots, x, *pieces, forget)


def _dw_out(slots, yg, dz, first_row):
    s_len, d = dz.shape
    ts = min(s_len, DW_ROWS)
    rows = d // N_CHIPS

    def body(*refs):
        a_ref, b_ref, o_ref = refs[-3:]

        @pl.when(pl.program_id(0) == 0)
        def _():
            o_ref[...] = jnp.zeros_like(o_ref)

        prod = lax.dot_general(a_ref[...], b_ref[...].astype(BF16), TN_DIMS, preferred_element_type=F32)
        o_ref[...] += prod.reshape(N_CHIPS, rows, d)

    specs = [pl.BlockSpec((ts, d), lambda s: (s, 0))] * 2
    return _dw_call(body, "dw_out", slots, [yg, dz], specs, (N_CHIPS, rows, d), lambda s: (0, first_row // rows, 0),
                    (s_len // ts,), ("arbitrary",))


def _dw_rnn_in(slots, x, parts, layer):
    s_len, d = x.shape
    ts = min(s_len, DW_ROWS)
    half = d // 2

    def body(*refs):
        x_ref, p_refs, o_ref = refs[-4], refs[-3:-1], refs[-1]

        @pl.when(pl.program_id(0) == 0)
        def _():
            o_ref[...] = jnp.zeros_like(o_ref)

        xb = x_ref[...].astype(BF16)
        for p, p_ref in enumerate(p_refs):
            for jj in (0, 1):
                for r in (0, 1):
                    o_ref[2 * p + jj, :, r * half:(r + 1) * half] += lax.dot_general(
                        xb[:, r * half:(r + 1) * half], p_ref[:, jj * half:(jj + 1) * half], TN_DIMS,
                        preferred_element_type=F32)

    specs = [pl.BlockSpec((ts, d), lambda s: (s, 0))] * 3
    return _dw_call(body, "dw_rnn_in", slots, [x] + list(parts), specs, (N_CHIPS, half, d),
                    lambda s: (0, ROWS_RNN_IN[layer] // half, 0), (s_len // ts,), ("arbitrary",))


def _fcum(fl, bias):
    s_len = fl.shape[0]

    def body(fl_ref, b_ref, cum_ref, sg_ref):
        z = fl_ref[...] + b_ref[...]
        e = jnp.exp(-jnp.abs(z))
        logf = jnp.minimum(z, 0.0) - jnp.log(1.0 + e)
        sneg = jnp.where(z >= 0, e, 1.0) / (1.0 + e)
        run = logf.T[0:16, :]
        sg_ref[...] = sneg.T[0:16, :]
        lane = lax.broadcasted_iota(jnp.int32, (16, s_len), 1)
        sh = 1
        while sh < s_len:
            run = run + jnp.where(lane >= sh, pltpu.roll(run, sh, 1), 0.0)
            sh *= 2
        cum_ref[...] = run

    return pl.pallas_call(
        body, name="fcum",
        out_shape=[jax.ShapeDtypeStruct((16, s_len), F32), jax.ShapeDtypeStruct((16, s_len), F32)],
    )(fl, bias)


def _fbwd(dcum, sneg):
    s_len = dcum.shape[1]

    def body(dc_ref, sg_ref, dl_ref, db_ref):
        run = dc_ref[...]
        lane = lax.broadcasted_iota(jnp.int32, (16, s_len), 1)
        sh = 1
        while sh < s_len:
            run = run + jnp.where(lane < s_len - sh, pltpu.roll(run, s_len - sh, 1), 0.0)
            sh *= 2
        dlog = run * sg_ref[...]
        db_ref[...] = jnp.broadcast_to(jnp.sum(dlog, axis=1, keepdims=True), (16, 128))
        full = jnp.concatenate([dlog, jnp.zeros((112, s_len), F32)], axis=0)
        dl_ref[...] = full.T.astype(BF16)

    return pl.pallas_call(
        body, name="fbwd",
        out_shape=[jax.ShapeDtypeStruct((s_len, 128), BF16), jax.ShapeDtypeStruct((16, 128), F32)],
    )(dcum, sneg)


FWD_TILE = 1024
BWD_TILE = 512


def _flash_fwd(q, k, v, cum, shards=()):
    s_len, width = q.shape
    tile = min(FWD_TILE, s_len // 2)
    nt = s_len // tile
    reps = tile // 128
    cumr = cum.reshape(-1, tile)
    ns = len(shards)
    pairs = width // HEAD_PAIR

    def body(*refs):
        q_ref, k_ref, v_ref, cum_ref = refs[:4]
        o_ref, lse_ref = refs[4 + ns:6 + ns]
        q_t, v_t, cum_col = refs[6 + 2 * ns:9 + 2 * ns]
        pid = pl.program_id(0)
        if ns:
            start, forward, finish = _gather_steps(refs[4:4 + ns], refs[6 + ns:6 + 2 * ns], *refs[9 + 2 * ns:])
            pl.when(pid == 0)(start)
            pl.when(pid == pairs // 2)(forward)
        top = lax.broadcasted_iota(jnp.int32, (HEAD_PAIR, tile), 0) < HEAD_DIM
        causal = (lax.broadcasted_iota(jnp.int32, (tile, tile), 0)
                  <= lax.broadcasted_iota(jnp.int32, (tile, tile), 1))

        def pre(i, carry):
            r0 = pl.multiple_of(i * tile, tile)
            q_t[i] = q_ref[pl.ds(r0, tile), :].astype(F32).T.astype(BF16)
            v_t[i] = v_ref[pl.ds(r0, tile), :].astype(F32).T.astype(BF16)
            for h in (0, 1):
                row = cum_ref[pl.ds((2 * pid + h) * nt + i, 1), :]
                cum_col[h, pl.ds(r0, tile), :] = jnp.broadcast_to(row, (HEAD_PAIR, tile)).T
            return carry

        lax.fori_loop(0, nt, pre, 0)

        def q_loop(qi, carry):
            qt = q_t[qi]
            zt = jnp.zeros_like(qt)
            qms = (jnp.where(top, qt, zt), jnp.where(top, zt, qt))

            def step(kj, state, masked):
                m0, l0, m1, l1, acc = state
                k0 = pl.multiple_of(kj * tile, tile)
                kt = k_ref[pl.ds(k0, tile), :]
                vt = v_t[kj]
                ms, ls, scales, contrib = [m0, m1], [l0, l1], [], None
                for h in (0, 1):
                    s = jnp.dot(kt, qms[h], preferred_element_type=F32)
                    s = s - jnp.tile(cum_col[h, pl.ds(k0, tile), :], (1, reps))
                    if masked:
                        s = jnp.where(causal, s, NEG)
                    m_new = jnp.maximum(ms[h], jnp.max(s, axis=0, keepdims=True))
                    p = jnp.exp(s - m_new)
                    scale = jnp.exp(ms[h] - m_new)
                    ls[h] = scale * ls[h] + jnp.sum(p, axis=0, keepdims=True)
                    ms[h] = m_new
                    scales.append(scale)
                    vm = jnp.where(top, vt, zt) if h == 0 else jnp.where(top, zt, vt)
                    part = jnp.dot(vm, p.astype(BF16), preferred_element_type=F32)
                    contrib = part if contrib is None else contrib + part
                acc = jnp.where(top, scales[0], scales[1]) * acc + contrib
                return ms[0], ls[0], ms[1], ls[1], acc

            low = jnp.full((1, tile), NEG, F32)
            zero = jnp.zeros((1, tile), F32)
            state = step(qi, (low, zero, low, zero, jnp.zeros((HEAD_PAIR, tile), F32)), True)
            m0, l0, m1, l1, acc = lax.fori_loop(0, qi, lambda kj, c: step(kj, c, False), state)
            q0 = pl.multiple_of(qi * tile, tile)
            o_ref[pl.ds(q0, tile), :] = (acc / jnp.where(top, l0, l1)).T
            lse_ref[pl.ds((2 * pid) * nt + qi, 1), :] = m0 + jnp.log(l0)
            lse_ref[pl.ds((2 * pid + 1) * nt + qi, 1), :] = m1 + jnp.log(l1)
            return carry

        lax.fori_loop(0, nt, q_loop, 0)
        if ns:
            pl.when(pid == pairs - 1)(finish)

    blk = pl.BlockSpec((s_len, HEAD_PAIR), lambda p: (0, p))
    full = pl.BlockSpec(cumr.shape, lambda p: (0, 0))
    sems = [pltpu.SemaphoreType.DMA((GATHER_SEMS * ns,))] * 2 if ns else []
    o, lse, *gathered = pl.pallas_call(
        body, name="flash_fwd_gather" if ns else "flash_fwd", grid=(pairs,),
        in_specs=[blk, blk, blk, full] + [ANY] * ns,
        out_specs=[blk, full] + [ANY] * ns,
        out_shape=[jax.ShapeDtypeStruct((s_len, width), F32), jax.ShapeDtypeStruct(cumr.shape, F32)]
        + [jax.ShapeDtypeStruct((N_CHIPS,) + s.shape, s.dtype) for s in shards],
        scratch_shapes=[pltpu.VMEM((nt, HEAD_PAIR, tile), BF16), pltpu.VMEM((nt, HEAD_PAIR, tile), BF16),
                        pltpu.VMEM((2, s_len, HEAD_PAIR), F32)] + sems,
        compiler_params=pltpu.CompilerParams(dimension_semantics=("arbitrary",)),
    )(q, k, v, cumr, *shards)
    return (o, lse.reshape(cum.shape), *gathered)


def _flash_bwd(q, k, v, o, do, lse, cum, outgoing=()):
    s_len, width = q.shape
    tile = min(BWD_TILE, s_len // 2)
    nt = s_len // tile
    reps = tile // 128
    cumr = cum.reshape(-1, tile)
    lse = lse.reshape(-1, tile)
    ns = len(outgoing)
    pairs = width // HEAD_PAIR

    def body(*refs):
        q_ref, k_ref, v_ref, o_ref, do_ref, lse_ref, cum_ref = refs[:7]
        dq_ref, dk_ref, dv_ref, dcum_ref = refs[7 + ns:11 + ns]
        (q_t, do_t, k_t, do_bf, cum_col, dq_t_acc, delta, q_side, dk_acc, dv_acc,
         k_side) = refs[11 + 2 * ns:22 + 2 * ns]
        pid = pl.program_id(0)
        if ns:
            start, finish = _scatter_steps(refs[7:7 + ns], refs[11 + ns:11 + 2 * ns], *refs[22 + 2 * ns:])
            pl.when(pid == 0)(start)
        top = lax.broadcasted_iota(jnp.int32, (HEAD_PAIR, tile), 0) < HEAD_DIM
        left = lax.broadcasted_iota(jnp.int32, (tile, HEAD_PAIR), 1) < HEAD_DIM
        causal = (lax.broadcasted_iota(jnp.int32, (tile, tile), 0)
                  <= lax.broadcasted_iota(jnp.int32, (tile, tile), 1))

        def pre(i, carry):
            r0 = pl.multiple_of(i * tile, tile)
            d_o = do_ref[pl.ds(r0, tile), :]
            prod_t = (d_o * o_ref[pl.ds(r0, tile), :]).T
            delta[pl.ds(i, 1), :] = jnp.sum(prod_t[:HEAD_DIM], axis=0, keepdims=True)
            delta[pl.ds(nt + i, 1), :] = jnp.sum(prod_t[HEAD_DIM:], axis=0, keepdims=True)
            do_bf[pl.ds(r0, tile), :] = d_o.astype(BF16)
            do_t[i] = d_o.T.astype(BF16)
            q_t[i] = q_ref[pl.ds(r0, tile), :].astype(F32).T.astype(BF16)
            k_t[i] = k_ref[pl.ds(r0, tile), :].astype(F32).T.astype(BF16)
            dq_t_acc[i] = jnp.zeros((HEAD_PAIR, tile), F32)
            for h in (0, 1):
                row = cum_ref[pl.ds((2 * pid + h) * nt + i, 1), :]
                cum_col[h, pl.ds(r0, tile), :] = jnp.broadcast_to(row, (HEAD_PAIR, tile)).T
                q_side[pl.ds(h * nt + i, 1), :] = jnp.zeros((1, tile), F32)
            return carry

        lax.fori_loop(0, nt, pre, 0)

        def kv_loop(kj, carry):
            k0 = pl.multiple_of(kj * tile, tile)
            kt = k_ref[pl.ds(k0, tile), :]
            vt = v_ref[pl.ds(k0, tile), :]
            ktt = k_t[kj]
            zt = jnp.zeros_like(ktt)
            zr = jnp.zeros_like(kt)
            kms = (jnp.where(top, ktt, zt), jnp.where(top, zt, ktt))
            cols = [jnp.tile(cum_col[h, pl.ds(k0, tile), :], (1, reps)) for h in (0, 1)]
            dk_acc[...] = jnp.zeros_like(dk_acc)
            dv_acc[...] = jnp.zeros_like(dv_acc)
            k_side[...] = jnp.zeros_like(k_side)

            def step(qi, carry, masked):
                q0 = pl.multiple_of(qi * tile, tile)
                qtt = q_t[qi]
                dtt = do_t[qi]
                q_rows = q_ref[pl.ds(q0, tile), :]
                do_rows = do_bf[pl.ds(q0, tile), :]
                dq_part = None
                for h in (0, 1):
                    q_m = jnp.where(top, qtt, zt) if h == 0 else jnp.where(top, zt, qtt)
                    do_m = jnp.where(top, dtt, zt) if h == 0 else jnp.where(top, zt, dtt)
                    s = jnp.dot(kt, q_m, preferred_element_type=F32) - cols[h]
                    if masked:
                        s = jnp.where(causal, s, NEG)
                    p = jnp.exp(s - lse_ref[pl.ds((2 * pid + h) * nt + qi, 1), :])
                    dp = jnp.dot(vt, do_m, preferred_element_type=F32)
                    ds = p * (dp - delta[pl.ds(h * nt + qi, 1), :])
                    q_side[pl.ds(h * nt + qi, 1), :] += jnp.sum(ds, axis=0, keepdims=True)
                    folded = ds[:, :128]
                    for b in range(1, reps):
                        folded = folded + ds[:, b * 128:(b + 1) * 128]
                    k_side[h] += folded
                    pb = p.astype(BF16)
                    dsb = ds.astype(BF16)
                    do_h = jnp.where(left, do_rows, zr) if h == 0 else jnp.where(left, zr, do_rows)
                    q_h = jnp.where(left, q_rows, zr) if h == 0 else jnp.where(left, zr, q_rows)
                    dv_acc[...] += jnp.dot(pb, do_h, preferred_element_type=F32)
                    dk_acc[...] += jnp.dot(dsb, q_h, preferred_element_type=F32)
                    part = jnp.dot(kms[h], dsb, preferred_element_type=F32)
                    dq_part = part if dq_part is None else dq_part + part
                dq_t_acc[qi] += dq_part
                return carry

            step(kj, 0, True)
            lax.fori_loop(kj + 1, nt, lambda qi, c: step(qi, c, False), 0)
            dk_ref[pl.ds(k0, tile), :] = dk_acc[...].astype(BF16)
            dv_ref[pl.ds(k0, tile), :] = dv_acc[...].astype(BF16)
            for h in (0, 1):
                dcum_ref[pl.ds((2 * pid + h) * nt + kj, 1), :] = -jnp.sum(k_side[h].T, axis=0, keepdims=True)
            return carry

        lax.fori_loop(0, nt, kv_loop, 0)

        def fin(i, carry):
            r0 = pl.multiple_of(i * tile, tile)
            dq_ref[pl.ds(r0, tile), :] = dq_t_acc[i].T.astype(BF16)
            for h in (0, 1):
                dcum_ref[pl.ds((2 * pid + h) * nt + i, 1), :] += q_side[pl.ds(h * nt + i, 1), :]
            return carry

        lax.fori_loop(0, nt, fin, 0)
        if ns:
            pl.when(pid == pairs - 1)(finish)

    blk = pl.BlockSpec((s_len, HEAD_PAIR), lambda p: (0, p))
    full = pl.BlockSpec(cumr.shape, lambda p: (0, 0))
    transposed = pltpu.VMEM((nt, HEAD_PAIR, tile), BF16)
    sems = [pltpu.SemaphoreType.DMA((3 * ns,))] * 2 if ns else []
    dq, dk, dv, dcum, *arrived = pl.pallas_call(
        body, name="flash_bwd_scatter" if ns else "flash_bwd", grid=(pairs,),
        in_specs=[blk, blk, blk, blk, blk, full, full] + [ANY] * ns,
        out_specs=[blk, blk, blk, full] + [ANY] * ns,
        out_shape=[jax.ShapeDtypeStruct((s_len, width), BF16)] * 3 + [jax.ShapeDtypeStruct(cumr.shape, F32)]
        + [jax.ShapeDtypeStruct(t.shape, t.dtype) for t in outgoing],
        scratch_shapes=[transposed, transposed, transposed, pltpu.VMEM((s_len, HEAD_PAIR), BF16),
                        pltpu.VMEM((2, s_len, HEAD_PAIR), F32), pltpu.VMEM((nt, HEAD_PAIR, tile), F32),
                        pltpu.VMEM((2 * nt, tile), F32), pltpu.VMEM((2 * nt, tile), F32),
                        pltpu.VMEM((tile, HEAD_PAIR), F32), pltpu.VMEM((tile, HEAD_PAIR), F32),
                        pltpu.VMEM((2, tile, HEAD_PAIR), F32)] + sems,
        compiler_params=pltpu.CompilerParams(dimension_semantics=("arbitrary",)),
    )(q, k, v, o, do, lse, cumr, *outgoing)
    return (dq, dk, dv, dcum.reshape(cum.shape), *arrived)


def _out_ln(a, gate, x, w, g, b, name):
    s_len, d = x.shape
    tm = min(512, s_len)

    def body(a_ref, gate_ref, x_ref, w_ref, g_ref, b_ref, xn_ref, xh_ref, rs_ref, yg_ref):
        gt = gate_ref[...]
        yg = (a_ref[...] * (gt * _sigmoid(gt))).astype(BF16)
        yg_ref[...] = yg
        z = ALPHA * x_ref[...] + jnp.dot(yg, w_ref[...], preferred_element_type=F32)
        zc = z - jnp.mean(z, axis=1, keepdims=True)
        rstd = lax.rsqrt(jnp.mean(zc * zc, axis=1, keepdims=True) + LN_EPS)
        xh = zc * rstd
        xh_ref[...] = xh
        xn_ref[...] = xh * g_ref[...] + b_ref[...]
        rs_ref[...] = jnp.broadcast_to(rstd, (tm, 128))

    row = pl.BlockSpec((tm, d), lambda i: (i, 0))
    vec = pl.BlockSpec((1, d), lambda i: (0, 0))
    return pl.pallas_call(
        body, name=name, grid=(s_len // tm,),
        in_specs=[row, row, row, pl.BlockSpec(w.shape, lambda i: (0, 0)), vec, vec],
        out_specs=[row, row, pl.BlockSpec((tm, 128), lambda i: (i, 0)), row],
        out_shape=[jax.ShapeDtypeStruct((s_len, d), F32), jax.ShapeDtypeStruct((s_len, d), F32),
                   jax.ShapeDtypeStruct((s_len, 128), F32), jax.ShapeDtypeStruct((s_len, d), BF16)],
        compiler_params=pltpu.CompilerParams(dimension_semantics=("parallel",)),
    )(a, gate, x, w, g, b)


def _ln_bwd(dout, xh, rs, g, w, gate, a, name, swap=None):
    s_len, d = dout.shape
    tm = min(512, s_len)
    steps = s_len // tm
    ns = 0 if swap is None else 1

    def body(*refs):
        do_ref, xh_ref, rs_ref, g_ref, w_ref, gate_ref, a_ref = refs[:7]
        dz_ref, da_ref, dgate_ref, dg_ref, db_ref = refs[7 + ns:12 + ns]
        if ns:
            start, finish = _swap_steps(refs[7:8], refs[12 + ns:13 + ns], [swap[1]], *refs[13 + ns:])
            pl.when(pl.program_id(0) == 0)(start)

        @pl.when(pl.program_id(0) == 0)
        def _():
            dg_ref[...] = jnp.zeros_like(dg_ref)
            db_ref[...] = jnp.zeros_like(db_ref)

        dout_t = do_ref[...]
        xh_t = xh_ref[...]
        dg_ref[...] += jnp.sum(dout_t * xh_t, axis=0, keepdims=True)
        db_ref[...] += jnp.sum(dout_t, axis=0, keepdims=True)
        dxh = dout_t * g_ref[...]
        m1 = jnp.mean(dxh, axis=1, keepdims=True)
        m2 = jnp.mean(dxh * xh_t, axis=1, keepdims=True)
        dz = rs_ref[:, 0:1] * (dxh - m1 - xh_t * m2)
        dz_ref[...] = dz
        dyg = lax.dot_general(dz.astype(BF16), w_ref[...], NT_DIMS, preferred_element_type=F32)
        gt = gate_ref[...]
        sg = _sigmoid(gt)
        da_ref[...] = dyg * (gt * sg)
        dgate_ref[...] = (dyg * a_ref[...] * (sg * (1.0 + gt * (1.0 - sg)))).astype(BF16)
        if ns:
            pl.when(pl.program_id(0) == steps - 1)(finish)

    row = pl.BlockSpec((tm, d), lambda i: (i, 0))
    vec = pl.BlockSpec((1, d), lambda i: (0, 0))
    extra_out = [jax.ShapeDtypeStruct((N_CHIPS, swap[1][1], LANES), swap[0].dtype)] if ns else []
    return pl.pallas_call(
        body, name=name + "_swap" if ns else name, grid=(steps,),
        in_specs=[row, row, pl.BlockSpec((tm, 128), lambda i: (i, 0)), vec, pl.BlockSpec(w.shape, lambda i: (0, 0)),
                  row, row] + [ANY] * ns,
        out_specs=[row, row, row, vec, vec] + [ANY] * ns,
        out_shape=[jax.ShapeDtypeStruct((s_len, d), F32), jax.ShapeDtypeStruct((s_len, d), F32),
                   jax.ShapeDtypeStruct((s_len, d), BF16), jax.ShapeDtypeStruct((1, d), F32),
                   jax.ShapeDtypeStruct((1, d), F32)] + extra_out,
        scratch_shapes=[pltpu.SemaphoreType.DMA((N_CHIPS,))] * 2 if ns else [],
        compiler_params=pltpu.CompilerParams(dimension_semantics=("arbitrary",)),
    )(dout, xh, rs, g, w, gate, a, *([swap[0]] if ns else []))


def _loss_grad(y, target):
    s_len, d = y.shape
    tm = min(512, s_len)

    def body(y_ref, t_ref, dy_ref, acc_ref):
        @pl.when(pl.program_id(0) == 0)
        def _():
            acc_ref[...] = jnp.zeros_like(acc_ref)

        diff = y_ref[...] - t_ref[...]
        dy_ref[...] = diff * (1.0 / d)
        acc_ref[...] += jnp.sum(diff * diff, axis=0, keepdims=True)

    row = pl.BlockSpec((tm, d), lambda i: (i, 0))
    return pl.pallas_call(
        body, name="loss_grad", grid=(s_len // tm,),
        in_specs=[row, row], out_specs=[row, pl.BlockSpec((1, d), lambda i: (0, 0))],
        out_shape=[jax.ShapeDtypeStruct((s_len, d), F32), jax.ShapeDtypeStruct((1, d), F32)],
        compiler_params=pltpu.CompilerParams(dimension_semantics=("arbitrary",)),
    )(y, target)


def _rnn_fwd(u, conv_w, conv_b, wa, ba, wi, bi, lam):
    s_len, width = u.shape
    tm = min(512, s_len // 2)
    nb = width // RNN_BLOCK

    def body(u_ref, up_ref, cw_ref, cb_ref, wa_ref, ba_ref, wi_ref, bi_ref, lam_ref,
             h_ref, uc_ref, r_ref, i_ref, a_ref, b_scr, h_carry):
        step_id = pl.program_id(0)

        @pl.when(step_id == 0)
        def _():
            h_carry[...] = jnp.zeros_like(h_carry)

        for n in range(nb):
            blk = slice(n * RNN_BLOCK, (n + 1) * RNN_BLOCK)
            ut = u_ref[:, blk]
            prev = jnp.where(step_id > 0, up_ref[:, blk], 0.0)
            uc = cb_ref[:, blk] + cw_ref[3:4, blk] * ut
            for k in (1, 2, 3):
                uc = uc + cw_ref[3 - k:4 - k, blk] * _shift_down(ut, prev, k)
            ucb = uc.astype(BF16)
            r = _sigmoid(jnp.dot(ucb, wa_ref[n], preferred_element_type=F32) + ba_ref[:, blk])
            ig = _sigmoid(jnp.dot(ucb, wi_ref[n], preferred_element_type=F32) + bi_ref[:, blk])
            log_a = -LRU_C * r * _softplus(-lam_ref[:, blk])
            uc_ref[:, blk] = uc
            r_ref[:, blk] = r
            i_ref[:, blk] = ig
            a_ref[:, blk] = jnp.exp(log_a)
            b_scr[:, blk] = jnp.sqrt(_neg_expm1(2.0 * log_a)) * (ig * uc)

        def scan(t, h):
            h = a_ref[pl.ds(t, 1), :] * h + b_scr[pl.ds(t, 1), :]
            h_ref[pl.ds(t, 1), :] = h
            return h

        h_carry[...] = lax.fori_loop(0, tm, scan, h_carry[...], unroll=8)

    row = pl.BlockSpec((tm, width), lambda i: (i, 0))
    prev8 = pl.BlockSpec((8, width), lambda i: (jnp.maximum(i * (tm // 8) - 1, 0), 0))
    vec = pl.BlockSpec((1, width), lambda i: (0, 0))
    mat = pl.BlockSpec(wa.shape, lambda i: (0, 0, 0))
    return pl.pallas_call(
        body, name="rnn_fwd", grid=(s_len // tm,),
        in_specs=[row, prev8, pl.BlockSpec(conv_w.shape, lambda i: (0, 0)), vec, mat, vec, mat, vec, vec],
        out_specs=[row] * 5,
        out_shape=[jax.ShapeDtypeStruct((s_len, width), F32)] * 5,
        scratch_shapes=[pltpu.VMEM((tm, width), F32), pltpu.VMEM((1, width), F32)],
        compiler_params=pltpu.CompilerParams(dimension_semantics=("arbitrary",)),
    )(u, u, conv_w, conv_b, wa, ba, wi, bi, lam)


def _rnn_bwd(dh, a, h, r, ig, uc, lam, wa, wi):
    s_len, width = dh.shape
    tm = min(512, s_len // 2)
    nt = s_len // tm
    nb = width // RNN_BLOCK

    def body(dh_ref, a_ref, h_ref, hp_ref, r_ref, i_ref, uc_ref, lam_ref, wa_ref, wi_ref,
             duc_ref, dwa_ref, dwi_ref, dba_ref, dbi_ref, dlam_ref, g_scr, c_scr, dsp_scr):
        step_id = pl.program_id(0)
        tile_id = nt - 1 - step_id

        @pl.when(step_id == 0)
        def _():
            c_scr[...] = jnp.zeros_like(c_scr)
            dsp_scr[...] = jnp.zeros_like(dsp_scr)
            dwa_ref[...] = jnp.zeros_like(dwa_ref)
            dwi_ref[...] = jnp.zeros_like(dwi_ref)
            dba_ref[...] = jnp.zeros_like(dba_ref)
            dbi_ref[...] = jnp.zeros_like(dbi_ref)

        def scan(j, c):
            t = tm - 1 - j
            g = dh_ref[pl.ds(t, 1), :] + c
            g_scr[pl.ds(t, 1), :] = g
            return a_ref[pl.ds(t, 1), :] * g

        c_scr[...] = lax.fori_loop(0, tm, scan, c_scr[...], unroll=8)

        for n in range(nb):
            blk = slice(n * RNN_BLOCK, (n + 1) * RNN_BLOCK)
            g_t = g_scr[:, blk]
            hp8 = jnp.where(tile_id > 0, hp_ref[:, blk], 0.0)
            h_prev = _shift_down(h_ref[:, blk], hp8, 1)
            av, rv, iv, ucv = a_ref[:, blk], r_ref[:, blk], i_ref[:, blk], uc_ref[:, blk]
            sp = _softplus(-lam_ref[:, blk])
            mag = jnp.sqrt(_neg_expm1(-2.0 * LRU_C * rv * sp))
            d_iu = g_t * mag
            dla = g_t * h_prev * av - g_t * (iv * ucv) * (av * av) / mag
            dsp_scr[:, blk] += jnp.sum(dla * (-LRU_C * rv), axis=0, keepdims=True)
            dra = dla * (-LRU_C * sp) * rv * (1.0 - rv)
            dia = d_iu * ucv * iv * (1.0 - iv)
            drab, diab, ucb = dra.astype(BF16), dia.astype(BF16), ucv.astype(BF16)
            duc_ref[:, blk] = (d_iu * iv
                               + lax.dot_general(drab, wa_ref[n], NT_DIMS, preferred_element_type=F32)
                               + lax.dot_general(diab, wi_ref[n], NT_DIMS, preferred_element_type=F32))
            dwa_ref[n] += lax.dot_general(ucb, drab, TN_DIMS, preferred_element_type=F32)
            dwi_ref[n] += lax.dot_general(ucb, diab, TN_DIMS, preferred_element_type=F32)
            dba_ref[:, blk] += jnp.sum(dra, axis=0, keepdims=True)
            dbi_ref[:, blk] += jnp.sum(dia, axis=0, keepdims=True)

        @pl.when(step_id == nt - 1)
        def _():
            dlam_ref[...] = -dsp_scr[...] * _sigmoid(-lam_ref[...])

    row = pl.BlockSpec((tm, width), lambda i: (nt - 1 - i, 0))
    prev8 = pl.BlockSpec((8, width), lambda i: (jnp.maximum((nt - 1 - i) * (tm // 8) - 1, 0), 0))
    vec = pl.BlockSpec((1, width), lambda i: (0, 0))
    mat = pl.BlockSpec(wa.shape, lambda i: (0, 0, 0))
    return pl.pallas_call(
        body, name="rnn_bwd", grid=(nt,),
        in_specs=[row, row, row, prev8, row, row, row, vec, mat, mat],
        out_specs=[row, mat, mat, vec, vec, vec],
        out_shape=[jax.ShapeDtypeStruct((s_len, width), F32), jax.ShapeDtypeStruct(wa.shape, F32),
                   jax.ShapeDtypeStruct(wa.shape, F32)] + [jax.ShapeDtypeStruct((1, width), F32)] * 3,
        scratch_shapes=[pltpu.VMEM((tm, width), F32), pltpu.VMEM((1, width), F32), pltpu.VMEM((1, width), F32)],
        compiler_params=pltpu.CompilerParams(dimension_semantics=("arbitrary",)),
    )(dh, a, h, h, r, ig, uc, lam, wa, wi)


def _conv_bwd(duc, u, conv_w):
    s_len, width = duc.shape
    tm = min(256, s_len)
    nt = s_len // tm

    def body(d_ref, dn_ref, u_ref, up_ref, cw_ref, du_ref, dcw_ref, dcb_ref):
        step_id = pl.program_id(0)

        @pl.when(step_id == 0)
        def _():
            dcw_ref[...] = jnp.zeros_like(dcw_ref)
            dcb_ref[...] = jnp.zeros_like(dcb_ref)

        for n in range(width // RNN_BLOCK):
            blk = slice(n * RNN_BLOCK, (n + 1) * RNN_BLOCK)
            dt = d_ref[:, blk]
            ut = u_ref[:, blk]
            nxt = jnp.where(step_id < nt - 1, dn_ref[:, blk], 0.0)
            prev = jnp.where(step_id > 0, up_ref[:, blk], 0.0)
            du = cw_ref[3:4, blk] * dt
            dcw_ref[3:4, blk] += jnp.sum(dt * ut, axis=0, keepdims=True)
            for k in (1, 2, 3):
                du = du + cw_ref[3 - k:4 - k, blk] * _shift_up(dt, nxt, k)
                dcw_ref[3 - k:4 - k, blk] += jnp.sum(dt * _shift_down(ut, prev, k), axis=0, keepdims=True)
            du_ref[:, blk] = du.astype(BF16)
            dcb_ref[:, blk] += jnp.sum(dt, axis=0, keepdims=True)

    row = pl.BlockSpec((tm, width), lambda i: (i, 0))
    prev8 = pl.BlockSpec((8, width), lambda i: (jnp.maximum(i * (tm // 8) - 1, 0), 0))
    next8 = pl.BlockSpec((8, width), lambda i: (jnp.minimum((i + 1) * (tm // 8), s_len // 8 - 1), 0))
    return pl.pallas_call(
        body, name="conv_bwd", grid=(nt,),
        in_specs=[row, next8, row, prev8, pl.BlockSpec(conv_w.shape, lambda i: (0, 0))],
        out_specs=[row, pl.BlockSpec(conv_w.shape, lambda i: (0, 0)), pl.BlockSpec((1, width), lambda i: (0, 0))],
        out_shape=[jax.ShapeDtypeStruct((s_len, width), BF16), jax.ShapeDtypeStruct(conv_w.shape, F32),
                   jax.ShapeDtypeStruct((1, width), F32)],
        compiler_params=pltpu.CompilerParams(dimension_semantics=("arbitrary",)),
    )(duc, duc, u, u, conv_w)


def _pair_sum(core, grads, theirs, first_row, out_dtype):
    n, half, _ = theirs.shape
    tb = 128 if half % 128 == 0 and first_row % 128 == 0 else half
    assert first_row % tb == 0 and half % tb == 0

    def body(c_ref, a_ref, b_ref, o_ref):
        o_ref[...] = (a_ref[...] + b_ref[...]).astype(out_dtype)

    blk = pl.BlockSpec((n, tb, LANES), lambda i, c: (0, i, 0))
    mine = pl.BlockSpec((n, tb, LANES), lambda i, c: (0, first_row // tb + c[0] * (half // tb) + i, 0))
    return pl.pallas_call(
        body, name="pair_sum",
        grid_spec=pltpu.PrefetchScalarGridSpec(
            num_scalar_prefetch=1, grid=(half // tb,), in_specs=[mine, blk], out_specs=blk),
        out_shape=jax.ShapeDtypeStruct((n, half, LANES), out_dtype),
        compiler_params=pltpu.CompilerParams(dimension_semantics=("parallel",)),
    )(core, grads, theirs)


def _sum_chips(parts):
    rows = parts.shape[1]
    tb = 128 if rows % 128 == 0 else rows

    def body(p_ref, o_ref):
        o_ref[...] = ((p_ref[0].astype(F32) + p_ref[1].astype(F32)) + p_ref[2].astype(F32)) + p_ref[3].astype(F32)

    return pl.pallas_call(
        body, name="chip_sum", grid=(rows // tb,),
        in_specs=[pl.BlockSpec((N_CHIPS, tb, LANES), lambda i: (0, i, 0))],
        out_specs=pl.BlockSpec((tb, LANES), lambda i: (i, 0)),
        out_shape=jax.ShapeDtypeStruct((rows, LANES), F32),
        compiler_params=pltpu.CompilerParams(dimension_semantics=("parallel",)),
    )(parts)


def _adamw(w, g, m, v, lead_per_block, rows_per_block):
    n, rows, cols = w.shape
    blk = pl.BlockSpec((lead_per_block, rows_per_block, cols), lambda i, j: (i, j, 0))

    def body(w_ref, g_ref, m_ref, v_ref, d_ref, nm_ref, nv_ref):
        gt = g_ref[...]
        nm = ADAM_B1 * m_ref[...] + (1.0 - ADAM_B1) * gt
        nv = ADAM_B2 * v_ref[...] + (1.0 - ADAM_B2) * (gt * gt)
        m_hat = nm / (1.0 - ADAM_B1 ** ADAM_STEP)
        v_hat = nv / (1.0 - ADAM_B2 ** ADAM_STEP)
        d_ref[...] = -ADAM_LR * (m_hat / (jnp.sqrt(v_hat) + ADAM_EPS) + ADAM_WD * w_ref[...])
        nm_ref[...] = nm
        nv_ref[...] = nv

    return pl.pallas_call(
        body, name="adamw", grid=(n // lead_per_block, rows // rows_per_block), in_specs=[blk] * 4,
        out_specs=[blk] * 3,
        out_shape=[jax.ShapeDtypeStruct(w.shape, F32)] * 3,
        compiler_params=pltpu.CompilerParams(dimension_semantics=("parallel", "parallel")),
    )(w, g, m, v)


def _place():
    x, y, c = lax.axis_index("x"), lax.axis_index("y"), lax.axis_index("c")
    return x, y, c, [(1 - x, y), (x, 1 - y), (1 - x, 1 - y)]


ANY = pl.BlockSpec(memory_space=pl.ANY)
COPY_PARTS = 4


def _remote_parts(src_of, dst_of, rows, send_sem, recv_sem, to, begin=True):
    parts = COPY_PARTS if rows % (16 * COPY_PARTS) == 0 else 1
    step = rows // parts
    for i in range(parts if begin is not False else 0):
        pltpu.make_async_remote_copy(src_ref=src_of(i * step, step), dst_ref=dst_of(i * step, step),
                                     send_sem=send_sem, recv_sem=recv_sem, device_id=to, device_id_type=MESH).start()
    if begin == "only":
        return None
    return pltpu.make_async_remote_copy(src_ref=src_of(0, rows), dst_ref=dst_of(0, rows), send_sem=send_sem,
                                        recv_sem=recv_sem, device_id=to, device_id_type=MESH)


GATHER_SEMS = 7


def _gather_steps(p_refs, o_refs, send_sems, recv_sems):
    x, y, c, chips = _place()
    me = 2 * x + y

    def half(ref, which):
        rows = ref.shape[0] // 2
        return ref.at[pl.ds(which * rows, rows)]

    def copy(k, src, dst, to):
        return pltpu.make_async_remote_copy(src_ref=src, dst_ref=dst, send_sem=send_sems.at[k],
                                            recv_sem=recv_sems.at[k], device_id=to, device_id_type=MESH)

    def first_copies():
        out = []
        for b, (p_ref, o_ref) in enumerate(zip(p_refs, o_refs)):
            for j, (cx, cy) in enumerate(chips):
                out.append(copy(GATHER_SEMS * b + j, half(p_ref, c), half(o_ref.at[me], c), (cx, cy, c)))
            out.append(copy(GATHER_SEMS * b + 6, p_ref, o_ref.at[me], (x, y, 1 - c)))
        return out

    def passed_copies():
        out = []
        for b, o_ref in enumerate(o_refs):
            for j, (cx, cy) in enumerate(chips):
                landed = half(o_ref.at[2 * cx + cy], c)
                out.append(copy(GATHER_SEMS * b + 3 + j, landed, landed, (x, y, 1 - c)))
        return out

    def start():
        for cp in first_copies():
            cp.start()

    def forward():
        for b, o_ref in enumerate(o_refs):
            for j, (cx, cy) in enumerate(chips):
                landed = half(o_ref.at[2 * cx + cy], c)
                copy(GATHER_SEMS * b + j, landed, landed, (x, y, c)).wait_recv()
        for cp in passed_copies():
            cp.start()

    def finish():
        for b, o_ref in enumerate(o_refs):
            for j, (cx, cy) in enumerate(chips):
                other = half(o_ref.at[2 * cx + cy], 1 - c)
                copy(GATHER_SEMS * b + 3 + j, other, other, (x, y, c)).wait_recv()
            copy(GATHER_SEMS * b + 6, o_ref.at[me], o_ref.at[me], (x, y, c)).wait_recv()
        for cp in first_copies() + passed_copies():
            cp.wait_send()

    return start, forward, finish


def _gather_chips(shards):
    nb = len(shards)

    def body(*refs):
        for step in _gather_steps(refs[:nb], refs[nb:2 * nb], *refs[2 * nb:]):
            step()

    return pl.pallas_call(
        body, name="gather_chips", in_specs=[ANY] * nb, out_specs=[ANY] * nb,
        out_shape=[jax.ShapeDtypeStruct((N_CHIPS,) + s.shape, s.dtype) for s in shards],
        scratch_shapes=[pltpu.SemaphoreType.DMA((GATHER_SEMS * nb,)), pltpu.SemaphoreType.DMA((GATHER_SEMS * nb,))],
    )(*shards)


def _swap_steps(g_refs, t_refs, spans, send_sems, recv_sems):
    x, y, c, _ = _place()

    def gives(begin):
        return [_remote_parts(
            lambda r0, m, g_ref=g_ref, j=j, base=first_row + (1 - c) * half: g_ref.at[j, pl.ds(base + r0, m), :],
            lambda r0, m, t_ref=t_ref, j=j: t_ref.at[j, pl.ds(r0, m), :],
            half, send_sems.at[b * N_CHIPS + j], recv_sems.at[b * N_CHIPS + j], (x, y, 1 - c), begin)
            for b, (g_ref, t_ref, (first_row, half)) in enumerate(zip(g_refs, t_refs, spans)) for j in range(N_CHIPS)]

    def start():
        gives("only")

    def finish():
        for give in gives(False):
            give.wait()

    return start, finish


def _swap_halves(bufs, spans):
    nb = len(bufs)

    def body(*refs):
        for step in _swap_steps(refs[:nb], refs[nb:2 * nb], spans, *refs[2 * nb:]):
            step()

    return pl.pallas_call(
        body, name="swap_halves", in_specs=[ANY] * nb, out_specs=[ANY] * nb,
        out_shape=[jax.ShapeDtypeStruct((b.shape[0], half, b.shape[2]), b.dtype) for b, (_, half) in zip(bufs, spans)],
        scratch_shapes=[pltpu.SemaphoreType.DMA((nb * N_CHIPS,)), pltpu.SemaphoreType.DMA((nb * N_CHIPS,))],
    )(*bufs)


def _scatter_steps(t_refs, o_refs, send_sems, recv_sems):
    x, y, c, chips = _place()
    me = 2 * x + y

    def slot(ref, chip):
        return lambda r0, n: ref.at[chip, pl.ds(r0, n), :]

    def sends(begin):
        return [_remote_parts(slot(t_ref, 2 * cx + cy), slot(o_ref, me), t_ref.shape[1], send_sems.at[3 * b + j],
                              recv_sems.at[3 * b + j], (cx, cy, c), begin)
                for b, (t_ref, o_ref) in enumerate(zip(t_refs, o_refs)) for j, (cx, cy) in enumerate(chips)]

    def start():
        sends("only")

    def finish():
        for b, o_ref in enumerate(o_refs):
            for j, (cx, cy) in enumerate(chips):
                landed = o_ref.at[2 * cx + cy]
                pltpu.make_async_remote_copy(src_ref=landed, dst_ref=landed, send_sem=send_sems.at[3 * b + j],
                                             recv_sem=recv_sems.at[3 * b + j], device_id=(x, y, c),
                                             device_id_type=MESH).wait_recv()
        for cp in sends(False):
            cp.wait_send()

    return start, finish


def _give_sibling(bufs):
    nb = len(bufs)

    def body(*refs):
        h_refs, o_refs, (send_sems, recv_sems) = refs[:nb], refs[nb:2 * nb], refs[2 * nb:]
        x, y, c, _ = _place()
        gives = [_remote_parts(lambda r0, n, h_ref=h_ref: h_ref.at[pl.ds(r0, n), :],
                               lambda r0, n, o_ref=o_ref: o_ref.at[pl.ds(r0, n), :],
                               h_ref.shape[0], send_sems.at[b], recv_sems.at[b], (x, y, 1 - c))
                 for b, (h_ref, o_ref) in enumerate(zip(h_refs, o_refs))]
        for give in gives:
            give.wait()

    return pl.pallas_call(
        body, name="give_sibling", in_specs=[ANY] * nb, out_specs=[ANY] * nb,
        out_shape=[jax.ShapeDtypeStruct(b.shape, b.dtype) for b in bufs],
        scratch_shapes=[pltpu.SemaphoreType.DMA((nb,)), pltpu.SemaphoreType.DMA((nb,))],
    )(*bufs)


def _pack(arrays, dtype, row_align):
    flat = []
    for arr in arrays:
        v = arr.reshape(-1)
        flat.append(jnp.pad(v, (0, (-v.shape[0]) % LANES)))
    total = sum(f.shape[0] for f in flat) // LANES
    pad_rows = (-total) % row_align
    if pad_rows:
        flat.append(jnp.zeros((pad_rows * LANES,), dtype))
    return jnp.concatenate(flat).reshape(-1, LANES)


def _unpack(buf, shapes, rows_align=1):
    lead = buf.shape[:-2]
    flat = buf.reshape(lead + (-1,))
    out, off = [], 0
    for shape in shapes:
        size = 1
        for dim in shape:
            size *= dim
        out.append(flat[..., off:off + size].reshape(lead + tuple(shape)))
        off += size + (-size) % (LANES * rows_align)
    return out


def _from_chips(parts, axis):
    return jnp.concatenate([parts[j] for j in range(N_CHIPS)], axis=axis)


def kernel(x, ln_g, ln_b, attn_w_in, attn_b_f, attn_w_out, rnn_w_in, rnn_conv_w, rnn_conv_b, rnn_w_a, rnn_b_a, rnn_w_i, rnn_b_i, rnn_lambda, rnn_w_out, loss_target, m_ln_g, m_ln_b, m_attn_w_in, m_attn_b_f, m_attn_w_out, m_rnn_w_in, m_rnn_conv_w, m_rnn_conv_b, m_rnn_w_a, m_rnn_b_a, m_rnn_w_i, m_rnn_b_i, m_rnn_lambda, m_rnn_w_out, v_ln_g, v_ln_b, v_attn_w_in, v_attn_b_f, v_attn_w_out, v_rnn_w_in, v_rnn_conv_w, v_rnn_conv_b, v_rnn_w_a, v_rnn_b_a, v_rnn_w_i, v_rnn_b_i, v_rnn_lambda, v_rnn_w_out):
    s_len, d = x.shape[1], x.shape[2]
    xs = x.reshape(s_len, d)
    target = loss_target.reshape(s_len, d)
    heads = attn_b_f.shape[1]
    qkvg = attn_w_in.shape[2] * N_CHIPS - heads

    me = 2 * lax.axis_index("x") + lax.axis_index("y")
    core = lax.axis_index("c").astype(jnp.int32)
    small = [rnn_conv_w, rnn_conv_b, rnn_b_a, rnn_b_i, rnn_lambda]
    a_in, a_out = attn_w_in.astype(BF16), attn_w_out.astype(BF16)
    later = [a_in[1], a_out] + [w.astype(BF16) for w in (rnn_w_in, rnn_w_a, rnn_w_i, rnn_w_out)]
    got_in, got_small = _gather_chips([a_in[0], _pack(small, F32, 16)])
    conv_w, conv_b, b_a, b_i, lam = [
        _from_chips(p, ax) for p, ax in zip(_unpack(got_small, [w.shape for w in small]), (2, 1, 1, 1, 1))]

    def attn_in_weights(w_in):
        full = jnp.concatenate([w_in[j] for j in range(N_CHIPS)], axis=1)
        return jnp.concatenate([full[:, :d] * (HEAD_DIM ** -0.5), full[:, d:qkvg],
                                jnp.pad(full[:, qkvg:], ((0, 0), (0, 128 - heads)))], axis=1).astype(BF16)

    w_cat = [attn_in_weights(got_in), None]
    b_f = jnp.pad(attn_b_f, ((0, 0), (0, 128 - heads)))

    saved = []
    cur = xs
    for layer in range(DEPTH):
        idx = layer // 2
        g_l, b_l = ln_g[layer:layer + 1], ln_b[layer:layer + 1]
        if layer % 2 == 0:
            q, k, v, gate, fl = _proj(cur, w_cat[idx], [(0, d, BF16), (d, d, BF16), (2 * d, d, BF16),
                                                         (3 * d, d, F32), (4 * d, 128, F32)], "attn_proj")
            cum, sneg = _fcum(fl, b_f[idx:idx + 1])
            o, lse, *rest = _flash_fwd(q, k, v, cum, later if layer == 0 else ())
            if layer == 0:
                w_cat[1] = attn_in_weights(rest[0])
                w_out_a = jnp.moveaxis(rest[1], 0, 1).reshape(2, d, d)
                w_in_r = jnp.moveaxis(rest[2], 0, 2).reshape(2, d, 2 * d)
                w_a = jnp.transpose(rest[3], (1, 2, 0, 3, 4)).reshape(2, -1, RNN_BLOCK, RNN_BLOCK)
                w_i = jnp.transpose(rest[4], (1, 2, 0, 3, 4)).reshape(2, -1, RNN_BLOCK, RNN_BLOCK)
                w_out_r = jnp.moveaxis(rest[5], 0, 1).reshape(2, d, d)
            nxt, xh, rs, yg = _out_ln(o, gate, cur, w_out_a[idx], g_l, b_l, "out_ln")
            saved.append(dict(x=cur, q=q, k=k, v=v, gate=gate, cum=cum, sneg=sneg, a=o, lse=lse, xh=xh, rs=rs, yg=yg))
        else:
            u, gate = _proj(cur, w_in_r[idx], [(0, d, F32), (d, d, F32)], "rnn_proj")
            vecs = [t[idx:idx + 1] for t in (conv_b, b_a, b_i, lam)]
            hseq, uc, r, ig, a = _rnn_fwd(u, conv_w[idx], vecs[0], w_a[idx], vecs[1], w_i[idx], vecs[2], vecs[3])
            nxt, xh, rs, yg = _out_ln(hseq, gate, cur, w_out_r[idx], g_l, b_l, "out_ln")
            saved.append(dict(x=cur, u=u, gate=gate, uc=uc, r=r, ig=ig, av=a, a=hseq, xh=xh, rs=rs, yg=yg, lam=vecs[3]))
        cur = nxt

    dout, sq = _loss_grad(cur, target)
    loss = lax.psum(0.5 * jnp.sum(sq) / d, ("x", "y", "c"))

    g_ln_g, g_ln_b = [None] * DEPTH, [None] * DEPTH
    g_attn = [None] * 2
    g_rnn = [None] * 2
    slots = None
    core1 = core.reshape(1)
    late_half = (SLOT_ROWS - LATE_ROWS) // 2

    def by_chip(t, axis):
        shape = t.shape[:axis] + (N_CHIPS, t.shape[axis] // N_CHIPS) + t.shape[axis + 1:]
        flat = jnp.moveaxis(t.reshape(shape), axis, 0).reshape(N_CHIPS, -1)
        return jnp.pad(flat, ((0, 0), (0, (-flat.shape[1]) % (8 * LANES)))).reshape(N_CHIPS, -1, LANES)

    def both(key):
        return jnp.stack([it[key] for it in g_rnn])

    for layer in reversed(range(DEPTH)):
        idx = layer // 2
        sv = saved[layer]
        swap = None
        if layer == 0:
            gates = jnp.concatenate([by_chip(both("w_a"), 2), by_chip(both("w_i"), 2)], axis=1)
            slots = lax.dynamic_update_slice(slots, gates, (0, ROWS_GATES, 0))
            swap = (slots, (LATE_ROWS, late_half))
        w_out = w_out_a[idx] if layer % 2 == 0 else w_out_r[idx]
        dz, da, dgate, dg, db, *theirs_late = _ln_bwd(dout, sv["xh"], sv["rs"], ln_g[layer:layer + 1], w_out,
                                                      sv["gate"], sv["a"], "ln_bwd", swap)
        if layer == 0:
            pair_late = _pair_sum(core1, slots, theirs_late[0], LATE_ROWS, BF16)
        g_ln_g[layer], g_ln_b[layer] = dg, db
        slots = _dw_out(slots, sv["yg"], dz, (ROWS_ATTN_OUT if layer % 2 == 0 else ROWS_RNN_OUT)[idx])
        if layer % 2 == 0:
            dq, dk, dv, dcum, *arrived = _flash_bwd(sv["q"], sv["k"], sv["v"], sv["a"], da, sv["lse"], sv["cum"],
                                                    [pair_late] if layer == 0 else ())
            dlogit, dbf = _fbwd(dcum, sv["sneg"])
            pieces = [dq, dk, dv, dgate, dlogit]
            if layer > 0:
                dout, = _mm_nt(dz, [(p, j * d) for j, p in enumerate(pieces)], w_cat[idx], "attn_dx")
            slots, d_w_f = _dw_attn_in(slots, sv["x"], pieces[:4], dlogit, idx)
            g_attn[idx] = dict(w_f=d_w_f[:, :heads], b_f=dbf[:, 0])
        else:
            duc, d_wa, d_wi, d_ba, d_bi, d_lam = _rnn_bwd(da, sv["av"], sv["a"], sv["r"], sv["ig"], sv["uc"], sv["lam"],
                                                          w_a[idx], w_i[idx])
            du, d_cw, d_cb = _conv_bwd(duc, sv["u"], conv_w[idx])
            dout, = _mm_nt(dz, [(du, 0), (dgate, d)], w_in_r[idx], "rnn_dx")
            slots = _dw_rnn_in(slots, sv["x"], (du, dgate), idx)
            g_rnn[idx] = dict(conv_w=d_cw, conv_b=d_cb[0], w_a=d_wa, b_a=d_ba[0], w_i=d_wi, b_i=d_bi[0], lam=d_lam[0])

    def everywhere(t):
        flat = t.reshape(-1)
        flat = jnp.pad(flat, (0, (-flat.shape[0]) % (8 * LANES))).reshape(-1, LANES)
        return jnp.broadcast_to(flat[None], (N_CHIPS,) + flat.shape)

    in_rows = jnp.stack([slots[1:, r:r + d, :heads] for r in ROWS_ATTN_IN], axis=1)
    edges = jnp.concatenate([in_rows, jnp.stack([it["w_f"] for it in g_attn])[None]])
    rows = [everywhere(edges), by_chip(both("conv_w"), 2),
            by_chip(both("conv_b"), 1), by_chip(both("b_a"), 1), by_chip(both("b_i"), 1), by_chip(both("lam"), 1),
            everywhere(jnp.concatenate(g_ln_g)), everywhere(jnp.concatenate(g_ln_b)),
            everywhere(jnp.stack([it["b_f"] for it in g_attn]))]
    used = sum(r.shape[1] for r in rows)
    small = jnp.concatenate(rows + [jnp.zeros((N_CHIPS, (-used) % 32, LANES), F32)], axis=1)

    spans = [(0, LATE_ROWS // 2), (0, small.shape[1] // 2)]
    theirs = _swap_halves([slots, small], spans)
    pair = [_pair_sum(core1, slots, theirs[0], 0, BF16), _pair_sum(core1, small, theirs[1], 0, F32)]
    dout, *arrived_last = _mm_nt(dz, [(p, j * d) for j, p in enumerate(pieces)], w_cat[0], "attn_dx", pair)
    grad_x = dout.reshape(x.shape)
    summed = []
    for mine_all, got in zip([pair_late] + pair, list(arrived) + arrived_last):
        own = lax.dynamic_index_in_dim(mine_all, me, 0, keepdims=False)
        summed.append(_sum_chips(lax.dynamic_update_index_in_dim(got, own, me, 0)))
    late, early, small_sum = [
        jnp.concatenate([jnp.where(core == 0, mine, other), jnp.where(core == 0, other, mine)])
        for mine, other in zip(summed, _give_sibling(summed))]

    def rows_at(first, n):
        return early[first:first + n] if first < LATE_ROWS else late[first - LATE_ROWS:first - LATE_ROWS + n]

    g_in_group = jnp.stack([rows_at(r, d) for r in ROWS_ATTN_IN])
    g_w_out_a = jnp.stack([rows_at(r, d // N_CHIPS) for r in ROWS_ATTN_OUT])
    g_w_out_r = jnp.stack([rows_at(r, d // N_CHIPS) for r in ROWS_RNN_OUT])
    folded = jnp.stack([rows_at(r, d // 2) for r in ROWS_RNN_IN])
    g_w_in_r = jnp.concatenate([folded[:, :, :d // 2], folded[:, :, d // 2:]], axis=1)
    gate_rows = rnn_w_a.size // LANES
    g_w_a = rows_at(ROWS_GATES, gate_rows).reshape(rnn_w_a.shape)
    g_w_i = rows_at(ROWS_GATES + gate_rows, gate_rows).reshape(rnn_w_i.shape)
    (g_edges, g_conv_w, g_conv_b, g_b_a, g_b_i, g_lam, g_ln_g_sum, g_ln_b_sum,
     g_b_f) = _unpack(small_sum, [
        (N_CHIPS, 2, d, heads), rnn_conv_w.shape, rnn_conv_b.shape, rnn_b_a.shape,
        rnn_b_i.shape, rnn_lambda.shape, ln_g.shape, ln_b.shape, attn_b_f.shape], rows_align=8)
    wide = jnp.concatenate([g_in_group, lax.dynamic_index_in_dim(g_edges, me, 0, keepdims=False)], axis=2)
    g_w_in_a = lax.dynamic_slice(wide, (0, 0, (heads // N_CHIPS) * me), attn_w_in.shape)

    def adam_nd(w, g, m, v, view, rows_per_block):
        outs = _adamw(w.reshape(view), g.reshape(view), m.reshape(view), v.reshape(view), 1, rows_per_block)
        return [t.reshape(w.shape) for t in outs]

    turned = [jnp.transpose(t, (2, 0, 1)) for t in (attn_w_in, g_w_in_a, m_attn_w_in, v_attn_w_in)]
    upd = {
        "attn_w_in": [jnp.transpose(t, (1, 2, 0)) for t in _adamw(*turned, attn_w_in.shape[2] // N_CHIPS, 2)],
        "attn_w_out": adam_nd(attn_w_out, g_w_out_a, m_attn_w_out, v_attn_w_out, attn_w_out.shape, 256),
        "rnn_w_in": adam_nd(rnn_w_in, g_w_in_r, m_rnn_w_in, v_rnn_w_in, rnn_w_in.shape, 512),
        "rnn_w_a": adam_nd(rnn_w_a, g_w_a, m_rnn_w_a, v_rnn_w_a, (1, -1, RNN_BLOCK), 512),
        "rnn_w_i": adam_nd(rnn_w_i, g_w_i, m_rnn_w_i, v_rnn_w_i, (1, -1, RNN_BLOCK), 512),
        "rnn_w_out": adam_nd(rnn_w_out, g_w_out_r, m_rnn_w_out, v_rnn_w_out, rnn_w_out.shape, 256),
    }
    smalls = [rnn_conv_w, rnn_conv_b, rnn_b_a, rnn_b_i, rnn_lambda, ln_g, ln_b, attn_b_f]
    g_small = [g_conv_w, g_conv_b, g_b_a, g_b_i, g_lam, g_ln_g_sum, g_ln_b_sum, g_b_f]
    m_small = [m_rnn_conv_w, m_rnn_conv_b, m_rnn_b_a, m_rnn_b_i, m_rnn_lambda, m_ln_g, m_ln_b, m_attn_b_f]
    v_small = [v_rnn_conv_w, v_rnn_conv_b, v_rnn_b_a, v_rnn_b_i, v_rnn_lambda, v_ln_g, v_ln_b, v_attn_b_f]
    packs = [_pack(t, F32, 16)[None] for t in (smalls, g_small, m_small, v_small)]
    small_out = [_unpack(t[0], [w.shape for w in smalls]) for t in _adamw(*packs, 1, 16)]
    for k, name in enumerate(("rnn_conv_w", "rnn_conv_b", "rnn_b_a", "rnn_b_i", "rnn_lambda", "ln_g", "ln_b",
                              "attn_b_f")):
        upd[name] = [small_out[0][k], small_out[1][k], small_out[2][k]]
    grad = {"attn_w_in": g_w_in_a, "attn_w_out": g_w_out_a, "rnn_w_in": g_w_in_r, "rnn_w_a": g_w_a, "rnn_w_i": g_w_i,
            "rnn_w_out": g_w_out_r, "rnn_conv_w": g_conv_w, "rnn_conv_b": g_conv_b, "rnn_b_a": g_b_a,
            "rnn_b_i": g_b_i, "rnn_lambda": g_lam, "ln_g": g_ln_g_sum, "ln_b": g_ln_b_sum, "attn_b_f": g_b_f}
    names = ("ln_g", "ln_b", "attn_w_in", "attn_b_f", "attn_w_out", "rnn_w_in", "rnn_conv_w", "rnn_conv_b",
             "rnn_w_a", "rnn_b_a", "rnn_w_i", "rnn_b_i", "rnn_lambda", "rnn_w_out")
    return (loss, grad_x, *[grad[n] for n in names], *[upd[n][0] for n in names], *[upd[n][1] for n in names],
            *[upd[n][2] for n in names])
```

```python
import jax
import jax.numpy as jnp
from jax import lax
from jax.experimental import pallas as pl
from jax.experimental.pallas import tpu as pltpu

F32 = jnp.float32
BF16 = jnp.bfloat16
MESH = pl.DeviceIdType.MESH

DEPTH = 4
HEAD_DIM = 64
HEAD_PAIR = 2 * HEAD_DIM
RNN_BLOCK = 256
CONV_WIDTH = 4
LRU_C = 8.0
ALPHA = (2.0 * DEPTH) ** 0.25
LN_EPS = 1e-5
ADAM_LR, ADAM_B1, ADAM_B2, ADAM_EPS, ADAM_WD, ADAM_STEP = 0.001, 0.9, 0.999, 1e-08, 0.01, 10
N_CHIPS = 4
LANES = 1024
NEG = -1e30

NT_DIMS = (((1,), (1,)), ((), ()))
TN_DIMS = (((0,), (0,)), ((), ()))


def _sigmoid(z):
    return 1.0 / (1.0 + jnp.exp(-z))


def _softplus(z):
    return jnp.maximum(z, 0.0) + jnp.log(1.0 + jnp.exp(-jnp.abs(z)))


def _neg_expm1(y):
    poly = y * (1.0 + y * (0.5 + y * (1.0 / 6.0 + y * (1.0 / 24.0))))
    return -jnp.where(y > -0.05, poly, jnp.exp(y) - 1.0)


def _shift_down(cur, prev8, k):
    n = cur.shape[0]
    row = lax.broadcasted_iota(jnp.int32, cur.shape, 0)
    fix = jnp.tile(pltpu.roll(prev8, k, 0), (n // 8, 1))
    return jnp.where(row < k, fix, pltpu.roll(cur, k, 0))


def _shift_up(cur, next8, k):
    n = cur.shape[0]
    row = lax.broadcasted_iota(jnp.int32, cur.shape, 0)
    fix = jnp.tile(pltpu.roll(next8, 8 - k, 0), (n // 8, 1))
    return jnp.where(row >= n - k, fix, pltpu.roll(cur, n - k, 0))


def _proj(x, w, outs, name):
    s_len, d = x.shape
    tm = min(512, s_len)

    def body(x_ref, w_ref, *o_refs):
        xb = x_ref[...].astype(BF16)
        for (c0, wd, dt), o_ref in zip(outs, o_refs):
            step = min(wd, 512)
            for cc in range(0, wd, step):
                o_ref[:, cc:cc + step] = jnp.dot(
                    xb, w_ref[:, c0 + cc:c0 + cc + step], preferred_element_type=F32).astype(dt)

    return pl.pallas_call(
        body, name=name, grid=(s_len // tm,),
        in_specs=[pl.BlockSpec((tm, d), lambda i: (i, 0)), pl.BlockSpec(w.shape, lambda i: (0, 0))],
        out_specs=[pl.BlockSpec((tm, wd), lambda i: (i, 0)) for _, wd, _ in outs],
        out_shape=[jax.ShapeDtypeStruct((s_len, wd), dt) for _, wd, dt in outs],
        compiler_params=pltpu.CompilerParams(dimension_semantics=("parallel",)),
    )(x, w)


def _mm_nt(dz, pieces, w, name, outgoing=()):
    s_len, d = dz.shape
    tm = min(256, s_len)
    steps = s_len // tm
    n = len(pieces)
    ns = len(outgoing)
    cols = [(c0, p.shape[1]) for p, c0 in pieces]

    def body(*refs):
        dz_ref, p_refs, w_ref, o_ref = refs[0], refs[1:1 + n], refs[1 + n], refs[2 + n + ns]
        if ns:
            start, finish = _scatter_steps(refs[2 + n:2 + n + ns], refs[3 + n + ns:3 + n + 2 * ns],
                                           *refs[3 + n + 2 * ns:])
            pl.when(pl.program_id(0) == 0)(start)
        acc = ALPHA * dz_ref[...]
        for (c0, wd), p_ref in zip(cols, p_refs):
            acc = acc + lax.dot_general(p_ref[...], w_ref[:, c0:c0 + wd], NT_DIMS, preferred_element_type=F32)
        o_ref[...] = acc
        if ns:
            pl.when(pl.program_id(0) == steps - 1)(finish)

    out, *arrived = pl.pallas_call(
        body, name=name + "_scatter" if ns else name, grid=(steps,),
        in_specs=[pl.BlockSpec((tm, d), lambda i: (i, 0))]
        + [pl.BlockSpec((tm, wd), lambda i: (i, 0)) for _, wd in cols]
        + [pl.BlockSpec(w.shape, lambda i: (0, 0))] + [ANY] * ns,
        out_specs=[pl.BlockSpec((tm, d), lambda i: (i, 0))] + [ANY] * ns,
        out_shape=[jax.ShapeDtypeStruct((s_len, d), F32)] + [jax.ShapeDtypeStruct(t.shape, t.dtype) for t in outgoing],
        scratch_shapes=[pltpu.SemaphoreType.DMA((3 * ns,))] * 2 if ns else [],
        compiler_params=pltpu.CompilerParams(dimension_semantics=("arbitrary" if ns else "parallel",)),
    )(dz, *[p for p, _ in pieces], w, *outgoing)
    return (out, *arrived)


ROWS_ATTN_IN = (0, 2048)
ROWS_ATTN_OUT = (1024, 1280)
ROWS_RNN_OUT = (1536, 1792)
ROWS_RNN_IN = (3072, 3584)
ROWS_GATES = 4096
LATE_ROWS = 1280
SLOT_ROWS = 4352


DW_ROWS = 1024


def _dw_call(body, name, slots, operands, specs, out_block, out_index, grid, semantics):
    return pl.pallas_call(
        body, name=name, grid=grid,
        in_specs=specs if slots is None else [ANY] + specs,
        out_specs=pl.BlockSpec(out_block, out_index),
        out_shape=jax.ShapeDtypeStruct((N_CHIPS, SLOT_ROWS, LANES), F32),
        input_output_aliases={} if slots is None else {0: 0},
        compiler_params=pltpu.CompilerParams(dimension_semantics=semantics),
    )(*(operands if slots is None else [slots] + operands))


def _dw_attn_in(slots, x, pieces, forget, layer):
    s_len, d = x.shape
    ts = min(s_len, DW_ROWS)
    steps = s_len // ts
    half = d // 2

    def body(g_ref, x_ref, p0, p1, p2, p3, f_ref, o_ref, wf_ref):
        lanes, step = pl.program_id(0), pl.program_id(1)

        @pl.when(step == 0)
        def _():
            o_ref[...] = jnp.zeros_like(o_ref)

        xb = x_ref[...].astype(BF16)
        for j, p_ref in enumerate((p0, p1, p2, p3)):
            o_ref[j] += lax.dot_general(xb, p_ref[...], TN_DIMS, preferred_element_type=F32)

        @pl.when(step == steps - 1)
        def _():
            o_ref[0] = o_ref[0] * (HEAD_DIM ** -0.5)

        @pl.when(lanes == 0)
        def _():
            @pl.when(step == 0)
            def _():
                wf_ref[...] = jnp.zeros_like(wf_ref)

            wf_ref[...] += lax.dot_general(xb, f_ref[...], TN_DIMS, preferred_element_type=F32)

    return pl.pallas_call(
        body, name="dw_attn_in", grid=(2, steps),
        in_specs=[ANY, pl.BlockSpec((ts, d), lambda i, s: (s, 0))] + [pl.BlockSpec((ts, half), lambda i, s: (s, i))] * 4
        + [pl.BlockSpec((ts, 128), lambda i, s: (s, 0))],
        out_specs=[pl.BlockSpec((N_CHIPS, d, half), lambda i, s: (0, ROWS_ATTN_IN[layer] // d, i)),
                   pl.BlockSpec((d, 128), lambda i, s: (0, 0))],
        out_shape=[jax.ShapeDtypeStruct((N_CHIPS, SLOT_ROWS, LANES), F32), jax.ShapeDtypeStruct((d, 128), F32)],
        input_output_aliases={0: 0},
        compiler_params=pltpu.CompilerParams(dimension_semantics=("arbitrary", "arbitrary")),
    )(slots, x, *pieces, forget)


def _dw_out(slots, yg, dz, first_row):
    s_len, d = dz.shape
    ts = min(s_len, DW_ROWS)
    rows = d // N_CHIPS

    def body(*refs):
        a_ref, b_ref, o_ref = refs[-3:]

        @pl.when(pl.program_id(0) == 0)
        def _():
            o_ref[...] = jnp.zeros_like(o_ref)

        prod = lax.dot_general(a_ref[...], b_ref[...].astype(BF16), TN_DIMS, preferred_element_type=F32)
        o_ref[...] += prod.reshape(N_CHIPS, rows, d)

    specs = [pl.BlockSpec((ts, d), lambda s: (s, 0))] * 2
    return _dw_call(body, "dw_out", slots, [yg, dz], specs, (N_CHIPS, rows, d), lambda s: (0, first_row // rows, 0),
                    (s_len // ts,), ("arbitrary",))


def _dw_rnn_in(slots, x, parts, layer):
    s_len, d = x.shape
    ts = min(s_len, DW_ROWS)
    half = d // 2

    def body(*refs):
        x_ref, p_refs, o_ref = refs[-4], refs[-3:-1], refs[-1]

        @pl.when(pl.program_id(0) == 0)
        def _():
            o_ref[...] = jnp.zeros_like(o_ref)

        xb = x_ref[...].astype(BF16)
        for p, p_ref in enumerate(p_refs):
            for jj in (0, 1):
                for r in (0, 1):
                    o_ref[2 * p + jj, :, r * half:(r + 1) * half] += lax.dot_general(
                        xb[:, r * half:(r + 1) * half], p_ref[:, jj * half:(jj + 1) * half], TN_DIMS,
                        preferred_element_type=F32)

    specs = [pl.BlockSpec((ts, d), lambda s: (s, 0))] * 3
    return _dw_call(body, "dw_rnn_in", slots, [x] + list(parts), specs, (N_CHIPS, half, d),
                    lambda s: (0, ROWS_RNN_IN[layer] // half, 0), (s_len // ts,), ("arbitrary",))


def _fcum(fl, bias):
    s_len = fl.shape[0]

    def body(fl_ref, b_ref, cum_ref, sg_ref):
        z = fl_ref[...] + b_ref[...]
        e = jnp.exp(-jnp.abs(z))
        logf = jnp.minimum(z, 0.0) - jnp.log(1.0 + e)
        sneg = jnp.where(z >= 0, e, 1.0) / (1.0 + e)
        run = logf.T[0:16, :]
        sg_ref[...] = sneg.T[0:16, :]
        lane = lax.broadcasted_iota(jnp.int32, (16, s_len), 1)
        sh = 1
        while sh < s_len:
            run = run + jnp.where(lane >= sh, pltpu.roll(run, sh, 1), 0.0)
            sh *= 2
        cum_ref[...] = run

    return pl.pallas_call(
        body, name="fcum",
        out_shape=[jax.ShapeDtypeStruct((16, s_len), F32), jax.ShapeDtypeStruct((16, s_len), F32)],
    )(fl, bias)


def _fbwd(dcum, sneg):
    s_len = dcum.shape[1]

    def body(dc_ref, sg_ref, dl_ref, db_ref):
        run = dc_ref[...]
        lane = lax.broadcasted_iota(jnp.int32, (16, s_len), 1)
        sh = 1
        while sh < s_len:
            run = run + jnp.where(lane < s_len - sh, pltpu.roll(run, s_len - sh, 1), 0.0)
            sh *= 2
        dlog = run * sg_ref[...]
        db_ref[...] = jnp.broadcast_to(jnp.sum(dlog, axis=1, keepdims=True), (16, 128))
        full = jnp.concatenate([dlog, jnp.zeros((112, s_len), F32)], axis=0)
        dl_ref[...] = full.T.astype(BF16)

    return pl.pallas_call(
        body, name="fbwd",
        out_shape=[jax.ShapeDtypeStruct((s_len, 128), BF16), jax.ShapeDtypeStruct((16, 128), F32)],
    )(dcum, sneg)


FWD_TILE = 1024
BWD_TILE = 512


def _flash_fwd(q, k, v, cum, shards=()):
    s_len, width = q.shape
    tile = min(FWD_TILE, s_len // 2)
    nt = s_len // tile
    reps = tile // 128
    cumr = cum.reshape(-1, tile)
    ns = len(shards)
    pairs = width // HEAD_PAIR

    def body(*refs):
        q_ref, k_ref, v_ref, cum_ref = refs[:4]
        o_ref, lse_ref = refs[4 + ns:6 + ns]
        q_t, v_t, cum_col = refs[6 + 2 * ns:9 + 2 * ns]
        pid = pl.program_id(0)
        if ns:
            start, forward, finish = _gather_steps(refs[4:4 + ns], refs[6 + ns:6 + 2 * ns], *refs[9 + 2 * ns:])
            pl.when(pid == 0)(start)
            pl.when(pid == pairs // 2)(forward)
        top = lax.broadcasted_iota(jnp.int32, (HEAD_PAIR, tile), 0) < HEAD_DIM
        causal = (lax.broadcasted_iota(jnp.int32, (tile, tile), 0)
                  <= lax.broadcasted_iota(jnp.int32, (tile, tile), 1))

        def pre(i, carry):
            r0 = pl.multiple_of(i * tile, tile)
            q_t[i] = q_ref[pl.ds(r0, tile), :].astype(F32).T.astype(BF16)
            v_t[i] = v_ref[pl.ds(r0, tile), :].astype(F32).T.astype(BF16)
            for h in (0, 1):
                row = cum_ref[pl.ds((2 * pid + h) * nt + i, 1), :]
                cum_col[h, pl.ds(r0, tile), :] = jnp.broadcast_to(row, (HEAD_PAIR, tile)).T
            return carry

        lax.fori_loop(0, nt, pre, 0)

        def q_loop(qi, carry):
            qt = q_t[qi]
            zt = jnp.zeros_like(qt)
            qms = (jnp.where(top, qt, zt), jnp.where(top, zt, qt))

            def step(kj, state, masked):
                m0, l0, m1, l1, acc = state
                k0 = pl.multiple_of(kj * tile, tile)
                kt = k_ref[pl.ds(k0, tile), :]
                vt = v_t[kj]
                ms, ls, scales, contrib = [m0, m1], [l0, l1], [], None
                for h in (0, 1):
                    s = jnp.dot(kt, qms[h], preferred_element_type=F32)
                    s = s - jnp.tile(cum_col[h, pl.ds(k0, tile), :], (1, reps))
                    if masked:
                        s = jnp.where(causal, s, NEG)
                    m_new = jnp.maximum(ms[h], jnp.max(s, axis=0, keepdims=True))
                    p = jnp.exp(s - m_new)
                    scale = jnp.exp(ms[h] - m_new)
                    ls[h] = scale * ls[h] + jnp.sum(p, axis=0, keepdims=True)
                    ms[h] = m_new
                    scales.append(scale)
                    vm = jnp.where(top, vt, zt) if h == 0 else jnp.where(top, zt, vt)
                    part = jnp.dot(vm, p.astype(BF16), preferred_element_type=F32)
                    contrib = part if contrib is None else contrib + part
                acc = jnp.where(top, scales[0], scales[1]) * acc + contrib
                return ms[0], ls[0], ms[1], ls[1], acc

            low = jnp.full((1, tile), NEG, F32)
            zero = jnp.zeros((1, tile), F32)
            state = step(qi, (low, zero, low, zero, jnp.zeros((HEAD_PAIR, tile), F32)), True)
            m0, l0, m1, l1, acc = lax.fori_loop(0, qi, lambda kj, c: step(kj, c, False), state)
            q0 = pl.multiple_of(qi * tile, tile)
            o_ref[pl.ds(q0, tile), :] = (acc / jnp.where(top, l0, l1)).T
            lse_ref[pl.ds((2 * pid) * nt + qi, 1), :] = m0 + jnp.log(l0)
            lse_ref[pl.ds((2 * pid + 1) * nt + qi, 1), :] = m1 + jnp.log(l1)
            return carry

        lax.fori_loop(0, nt, q_loop, 0)
        if ns:
            pl.when(pid == pairs - 1)(finish)

    blk = pl.BlockSpec((s_len, HEAD_PAIR), lambda p: (0, p))
    full = pl.BlockSpec(cumr.shape, lambda p: (0, 0))
    sems = [pltpu.SemaphoreType.DMA((GATHER_SEMS * ns,))] * 2 if ns else []
    o, lse, *gathered = pl.pallas_call(
        body, name="flash_fwd_gather" if ns else "flash_fwd", grid=(pairs,),
        in_specs=[blk, blk, blk, full] + [ANY] * ns,
        out_specs=[blk, full] + [ANY] * ns,
        out_shape=[jax.ShapeDtypeStruct((s_len, width), F32), jax.ShapeDtypeStruct(cumr.shape, F32)]
        + [jax.ShapeDtypeStruct((N_CHIPS,) + s.shape, s.dtype) for s in shards],
        scratch_shapes=[pltpu.VMEM((nt, HEAD_PAIR, tile), BF16), pltpu.VMEM((nt, HEAD_PAIR, tile), BF16),
                        pltpu.VMEM((2, s_len, HEAD_PAIR), F32)] + sems,
        compiler_params=pltpu.CompilerParams(dimension_semantics=("arbitrary",)),
    )(q, k, v, cumr, *shards)
    return (o, lse.reshape(cum.shape), *gathered)


def _flash_bwd(q, k, v, o, do, lse, cum, outgoing=()):
    s_len, width = q.shape
    tile = min(BWD_TILE, s_len // 2)
    nt = s_len // tile
    reps = tile // 128
    cumr = cum.reshape(-1, tile)
    lse = lse.reshape(-1, tile)
    ns = len(outgoing)
    pairs = width // HEAD_PAIR

    def body(*refs):
        q_ref, k_ref, v_ref, o_ref, do_ref, lse_ref, cum_ref = refs[:7]
        dq_ref, dk_ref, dv_ref, dcum_ref = refs[7 + ns:11 + ns]
        (q_t, do_t, k_t, do_bf, cum_col, dq_t_acc, delta, q_side, dk_acc, dv_acc,
         k_side) = refs[11 + 2 * ns:22 + 2 * ns]
        pid = pl.program_id(0)
        if ns:
            start, finish = _scatter_steps(refs[7:7 + ns], refs[11 + ns:11 + 2 * ns], *refs[22 + 2 * ns:])
            pl.when(pid == 0)(start)
        top = lax.broadcasted_iota(jnp.int32, (HEAD_PAIR, tile), 0) < HEAD_DIM
        left = lax.broadcasted_iota(jnp.int32, (tile, HEAD_PAIR), 1) < HEAD_DIM
        causal = (lax.broadcasted_iota(jnp.int32, (tile, tile), 0)
                  <= lax.broadcasted_iota(jnp.int32, (tile, tile), 1))

        def pre(i, carry):
            r0 = pl.multiple_of(i * tile, tile)
            d_o = do_ref[pl.ds(r0, tile), :]
            prod_t = (d_o * o_ref[pl.ds(r0, tile), :]).T
            delta[pl.ds(i, 1), :] = jnp.sum(prod_t[:HEAD_DIM], axis=0, keepdims=True)
            delta[pl.ds(nt + i, 1), :] = jnp.sum(prod_t[HEAD_DIM:], axis=0, keepdims=True)
            do_bf[pl.ds(r0, tile), :] = d_o.astype(BF16)
            do_t[i] = d_o.T.astype(BF16)
            q_t[i] = q_ref[pl.ds(r0, tile), :].astype(F32).T.astype(BF16)
            k_t[i] = k_ref[pl.ds(r0, tile), :].astype(F32).T.astype(BF16)
            dq_t_acc[i] = jnp.zeros((HEAD_PAIR, tile), F32)
            for h in (0, 1):
                row = cum_ref[pl.ds((2 * pid + h) * nt + i, 1), :]
                cum_col[h, pl.ds(r0, tile), :] = jnp.broadcast_to(row, (HEAD_PAIR, tile)).T
                q_side[pl.ds(h * nt + i, 1), :] = jnp.zeros((1, tile), F32)
            return carry

        lax.fori_loop(0, nt, pre, 0)

        def kv_loop(kj, carry):
            k0 = pl.multiple_of(kj * tile, tile)
            kt = k_ref[pl.ds(k0, tile), :]
            vt = v_ref[pl.ds(k0, tile), :]
            ktt = k_t[kj]
            zt = jnp.zeros_like(ktt)
            zr = jnp.zeros_like(kt)
            kms = (jnp.where(top, ktt, zt), jnp.where(top, zt, ktt))
            cols = [jnp.tile(cum_col[h, pl.ds(k0, tile), :], (1, reps)) for h in (0, 1)]
            dk_acc[...] = jnp.zeros_like(dk_acc)
            dv_acc[...] = jnp.zeros_like(dv_acc)
            k_side[...] = jnp.zeros_like(k_side)

            def step(qi, carry, masked):
                q0 = pl.multiple_of(qi * tile, tile)
                qtt = q_t[qi]
                dtt = do_t[qi]
                q_rows = q_ref[pl.ds(q0, tile), :]
                do_rows = do_bf[pl.ds(q0, tile), :]
                dq_part = None
                for h in (0, 1):
                    q_m = jnp.where(top, qtt, zt) if h == 0 else jnp.where(top, zt, qtt)
                    do_m = jnp.where(top, dtt, zt) if h == 0 else jnp.where(top, zt, dtt)
                    s = jnp.dot(kt, q_m, preferred_element_type=F32) - cols[h]
                    if masked:
                        s = jnp.where(causal, s, NEG)
                    p = jnp.exp(s - lse_ref[pl.ds((2 * pid + h) * nt + qi, 1), :])
                    dp = jnp.dot(vt, do_m, preferred_element_type=F32)
                    ds = p * (dp - delta[pl.ds(h * nt + qi, 1), :])
                    q_side[pl.ds(h * nt + qi, 1), :] += jnp.sum(ds, axis=0, keepdims=True)
                    folded = ds[:, :128]
                    for b in range(1, reps):
                        folded = folded + ds[:, b * 128:(b + 1) * 128]
                    k_side[h] += folded
                    pb = p.astype(BF16)
                    dsb = ds.astype(BF16)
                    do_h = jnp.where(left, do_rows, zr) if h == 0 else jnp.where(left, zr, do_rows)
                    q_h = jnp.where(left, q_rows, zr) if h == 0 else jnp.where(left, zr, q_rows)
                    dv_acc[...] += jnp.dot(pb, do_h, preferred_element_type=F32)
                    dk_acc[...] += jnp.dot(dsb, q_h, preferred_element_type=F32)
                    part = jnp.dot(kms[h], dsb, preferred_element_type=F32)
                    dq_part = part if dq_part is None else dq_part + part
                dq_t_acc[qi] += dq_part
                return carry

            step(kj, 0, True)
            lax.fori_loop(kj + 1, nt, lambda qi, c: step(qi, c, False), 0)
            dk_ref[pl.ds(k0, tile), :] = dk_acc[...].astype(BF16)
            dv_ref[pl.ds(k0, tile), :] = dv_acc[...].astype(BF16)
            for h in (0, 1):
                dcum_ref[pl.ds((2 * pid + h) * nt + kj, 1), :] = -jnp.sum(k_side[h].T, axis=0, keepdims=True)
            return carry

        lax.fori_loop(0, nt, kv_loop, 0)

        def fin(i, carry):
            r0 = pl.multiple_of(i * tile, tile)
            dq_ref[pl.ds(r0, tile), :] = dq_t_acc[i].T.astype(BF16)
            for h in (0, 1):
                dcum_ref[pl.ds((2 * pid + h) * nt + i, 1), :] += q_side[pl.ds(h * nt + i, 1), :]
            return carry

        lax.fori_loop(0, nt, fin, 0)
        if ns:
            pl.when(pid == pairs - 1)(finish)

    blk = pl.BlockSpec((s_len, HEAD_PAIR), lambda p: (0, p))
    full = pl.BlockSpec(cumr.shape, lambda p: (0, 0))
    transposed = pltpu.VMEM((nt, HEAD_PAIR, tile), BF16)
    sems = [pltpu.SemaphoreType.DMA((3 * ns,))] * 2 if ns else []
    dq, dk, dv, dcum, *arrived = pl.pallas_call(
        body, name="flash_bwd_scatter" if ns else "flash_bwd", grid=(pairs,),
        in_specs=[blk, blk, blk, blk, blk, full, full] + [ANY] * ns,
        out_specs=[blk, blk, blk, full] + [ANY] * ns,
        out_shape=[jax.ShapeDtypeStruct((s_len, width), BF16)] * 3 + [jax.ShapeDtypeStruct(cumr.shape, F32)]
        + [jax.ShapeDtypeStruct(t.shape, t.dtype) for t in outgoing],
        scratch_shapes=[transposed, transposed, transposed, pltpu.VMEM((s_len, HEAD_PAIR), BF16),
                        pltpu.VMEM((2, s_len, HEAD_PAIR), F32), pltpu.VMEM((nt, HEAD_PAIR, tile), F32),
                        pltpu.VMEM((2 * nt, tile), F32), pltpu.VMEM((2 * nt, tile), F32),
                        pltpu.VMEM((tile, HEAD_PAIR), F32), pltpu.VMEM((tile, HEAD_PAIR), F32),
                        pltpu.VMEM((2, tile, HEAD_PAIR), F32)] + sems,
        compiler_params=pltpu.CompilerParams(dimension_semantics=("arbitrary",)),
    )(q, k, v, o, do, lse, cumr, *outgoing)
    return (dq, dk, dv, dcum.reshape(cum.shape), *arrived)


def _out_ln(a, gate, x, w, g, b, name):
    s_len, d = x.shape
    tm = min(512, s_len)

    def body(a_ref, gate_ref, x_ref, w_ref, g_ref, b_ref, xn_ref, xh_ref, rs_ref, yg_ref):
        gt = gate_ref[...]
        yg = (a_ref[...] * (gt * _sigmoid(gt))).astype(BF16)
        yg_ref[...] = yg
        z = ALPHA * x_ref[...] + jnp.dot(yg, w_ref[...], preferred_element_type=F32)
        zc = z - jnp.mean(z, axis=1, keepdims=True)
        rstd = lax.rsqrt(jnp.mean(zc * zc, axis=1, keepdims=True) + LN_EPS)
        xh = zc * rstd
        xh_ref[...] = xh
        xn_ref[...] = xh * g_ref[...] + b_ref[...]
        rs_ref[...] = jnp.broadcast_to(rstd, (tm, 128))

    row = pl.BlockSpec((tm, d), lambda i: (i, 0))
    vec = pl.BlockSpec((1, d), lambda i: (0, 0))
    return pl.pallas_call(
        body, name=name, grid=(s_len // tm,),
        in_specs=[row, row, row, pl.BlockSpec(w.shape, lambda i: (0, 0)), vec, vec],
        out_specs=[row, row, pl.BlockSpec((tm, 128), lambda i: (i, 0)), row],
        out_shape=[jax.ShapeDtypeStruct((s_len, d), F32), jax.ShapeDtypeStruct((s_len, d), F32),
                   jax.ShapeDtypeStruct((s_len, 128), F32), jax.ShapeDtypeStruct((s_len, d), BF16)],
        compiler_params=pltpu.CompilerParams(dimension_semantics=("parallel",)),
    )(a, gate, x, w, g, b)


def _ln_bwd(dout, xh, rs, g, w, gate, a, name, swap=None):
    s_len, d = dout.shape
    tm = min(512, s_len)
    steps = s_len // tm
    ns = 0 if swap is None else 1

    def body(*refs):
        do_ref, xh_ref, rs_ref, g_ref, w_ref, gate_ref, a_ref = refs[:7]
        dz_ref, da_ref, dgate_ref, dg_ref, db_ref = refs[7 + ns:12 + ns]
        if ns:
            start, finish = _swap_steps(refs[7:8], refs[12 + ns:13 + ns], [swap[1]], *refs[13 + ns:])
            pl.when(pl.program_id(0) == 0)(start)

        @pl.when(pl.program_id(0) == 0)
        def _():
            dg_ref[...] = jnp.zeros_like(dg_ref)
            db_ref[...] = jnp.zeros_like(db_ref)

        dout_t = do_ref[...]
        xh_t = xh_ref[...]
        dg_ref[...] += jnp.sum(dout_t * xh_t, axis=0, keepdims=True)
        db_ref[...] += jnp.sum(dout_t, axis=0, keepdims=True)
        dxh = dout_t * g_ref[...]
        m1 = jnp.mean(dxh, axis=1, keepdims=True)
        m2 = jnp.mean(dxh * xh_t, axis=1, keepdims=True)
        dz = rs_ref[:, 0:1] * (dxh - m1 - xh_t * m2)
        dz_ref[...] = dz
        dyg = lax.dot_general(dz.astype(BF16), w_ref[...], NT_DIMS, preferred_element_type=F32)
        gt = gate_ref[...]
        sg = _sigmoid(gt)
        da_ref[...] = dyg * (gt * sg)
        dgate_ref[...] = (dyg * a_ref[...] * (sg * (1.0 + gt * (1.0 - sg)))).astype(BF16)
        if ns:
            pl.when(pl.program_id(0) == steps - 1)(finish)

    row = pl.BlockSpec((tm, d), lambda i: (i, 0))
    vec = pl.BlockSpec((1, d), lambda i: (0, 0))
    extra_out = [jax.ShapeDtypeStruct((N_CHIPS, swap[1][1], LANES), swap[0].dtype)] if ns else []
    return pl.pallas_call(
        body, name=name + "_swap" if ns else name, grid=(steps,),
        in_specs=[row, row, pl.BlockSpec((tm, 128), lambda i: (i, 0)), vec, pl.BlockSpec(w.shape, lambda i: (0, 0)),
                  row, row] + [ANY] * ns,
        out_specs=[row, row, row, vec, vec] + [ANY] * ns,
        out_shape=[jax.ShapeDtypeStruct((s_len, d), F32), jax.ShapeDtypeStruct((s_len, d), F32),
                   jax.ShapeDtypeStruct((s_len, d), BF16), jax.ShapeDtypeStruct((1, d), F32),
                   jax.ShapeDtypeStruct((1, d), F32)] + extra_out,
        scratch_shapes=[pltpu.SemaphoreType.DMA((N_CHIPS,))] * 2 if ns else [],
        compiler_params=pltpu.CompilerParams(dimension_semantics=("arbitrary",)),
    )(dout, xh, rs, g, w, gate, a, *([swap[0]] if ns else []))


def _loss_grad(y, target):
    s_len, d = y.shape
    tm = min(512, s_len)

    def body(y_ref, t_ref, dy_ref, acc_ref):
        @pl.when(pl.program_id(0) == 0)
        def _():
            acc_ref[...] = jnp.zeros_like(acc_ref)

        diff = y_ref[...] - t_ref[...]
        dy_ref[...] = diff * (1.0 / d)
        acc_ref[...] += jnp.sum(diff * diff, axis=0, keepdims=True)

    row = pl.BlockSpec((tm, d), lambda i: (i, 0))
    return pl.pallas_call(
        body, name="loss_grad", grid=(s_len // tm,),
        in_specs=[row, row], out_specs=[row, pl.BlockSpec((1, d), lambda i: (0, 0))],
        out_shape=[jax.ShapeDtypeStruct((s_len, d), F32), jax.ShapeDtypeStruct((1, d), F32)],
        compiler_params=pltpu.CompilerParams(dimension_semantics=("arbitrary",)),
    )(y, target)


def _rnn_fwd(u, conv_w, conv_b, wa, ba, wi, bi, lam):
    s_len, width = u.shape
    tm = min(512, s_len // 2)
    nb = width // RNN_BLOCK

    def body(u_ref, up_ref, cw_ref, cb_ref, wa_ref, ba_ref, wi_ref, bi_ref, lam_ref,
             h_ref, uc_ref, r_ref, i_ref, a_ref, b_scr, h_carry):
        step_id = pl.program_id(0)

        @pl.when(step_id == 0)
        def _():
            h_carry[...] = jnp.zeros_like(h_carry)

        for n in range(nb):
            blk = slice(n * RNN_BLOCK, (n + 1) * RNN_BLOCK)
            ut = u_ref[:, blk]
            prev = jnp.where(step_id > 0, up_ref[:, blk], 0.0)
            uc = cb_ref[:, blk] + cw_ref[3:4, blk] * ut
            for k in (1, 2, 3):
                uc = uc + cw_ref[3 - k:4 - k, blk] * _shift_down(ut, prev, k)
            ucb = uc.astype(BF16)
            r = _sigmoid(jnp.dot(ucb, wa_ref[n], preferred_element_type=F32) + ba_ref[:, blk])
            ig = _sigmoid(jnp.dot(ucb, wi_ref[n], preferred_element_type=F32) + bi_ref[:, blk])
            log_a = -LRU_C * r * _softplus(-lam_ref[:, blk])
            uc_ref[:, blk] = uc
            r_ref[:, blk] = r
            i_ref[:, blk] = ig
            a_ref[:, blk] = jnp.exp(log_a)
            b_scr[:, blk] = jnp.sqrt(_neg_expm1(2.0 * log_a)) * (ig * uc)

        def scan(t, h):
            h = a_ref[pl.ds(t, 1), :] * h + b_scr[pl.ds(t, 1), :]
            h_ref[pl.ds(t, 1), :] = h
            return h

        h_carry[...] = lax.fori_loop(0, tm, scan, h_carry[...], unroll=8)

    row = pl.BlockSpec((tm, width), lambda i: (i, 0))
    prev8 = pl.BlockSpec((8, width), lambda i: (jnp.maximum(i * (tm // 8) - 1, 0), 0))
    vec = pl.BlockSpec((1, width), lambda i: (0, 0))
    mat = pl.BlockSpec(wa.shape, lambda i: (0, 0, 0))
    return pl.pallas_call(
        body, name="rnn_fwd", grid=(s_len // tm,),
        in_specs=[row, prev8, pl.BlockSpec(conv_w.shape, lambda i: (0, 0)), vec, mat, vec, mat, vec, vec],
        out_specs=[row] * 5,
        out_shape=[jax.ShapeDtypeStruct((s_len, width), F32)] * 5,
        scratch_shapes=[pltpu.VMEM((tm, width), F32), pltpu.VMEM((1, width), F32)],
        compiler_params=pltpu.CompilerParams(dimension_semantics=("arbitrary",)),
    )(u, u, conv_w, conv_b, wa, ba, wi, bi, lam)


def _rnn_bwd(dh, a, h, r, ig, uc, lam, wa, wi):
    s_len, width = dh.shape
    tm = min(512, s_len // 2)
    nt = s_len // tm
    nb = width // RNN_BLOCK

    def body(dh_ref, a_ref, h_ref, hp_ref, r_ref, i_ref, uc_ref, lam_ref, wa_ref, wi_ref,
             duc_ref, dwa_ref, dwi_ref, dba_ref, dbi_ref, dlam_ref, g_scr, c_scr, dsp_scr):
        step_id = pl.program_id(0)
        tile_id = nt - 1 - step_id

        @pl.when(step_id == 0)
        def _():
            c_scr[...] = jnp.zeros_like(c_scr)
            dsp_scr[...] = jnp.zeros_like(dsp_scr)
            dwa_ref[...] = jnp.zeros_like(dwa_ref)
            dwi_ref[...] = jnp.zeros_like(dwi_ref)
            dba_ref[...] = jnp.zeros_like(dba_ref)
            dbi_ref[...] = jnp.zeros_like(dbi_ref)

        def scan(j, c):
            t = tm - 1 - j
            g = dh_ref[pl.ds(t, 1), :] + c
            g_scr[pl.ds(t, 1), :] = g
            return a_ref[pl.ds(t, 1), :] * g

        c_scr[...] = lax.fori_loop(0, tm, scan, c_scr[...], unroll=8)

        for n in range(nb):
            blk = slice(n * RNN_BLOCK, (n + 1) * RNN_BLOCK)
            g_t = g_scr[:, blk]
            hp8 = jnp.where(tile_id > 0, hp_ref[:, blk], 0.0)
            h_prev = _shift_down(h_ref[:, blk], hp8, 1)
            av, rv, iv, ucv = a_ref[:, blk], r_ref[:, blk], i_ref[:, blk], uc_ref[:, blk]
            sp = _softplus(-lam_ref[:, blk])
            mag = jnp.sqrt(_neg_expm1(-2.0 * LRU_C * rv * sp))
            d_iu = g_t * mag
            dla = g_t * h_prev * av - g_t * (iv * ucv) * (av * av) / mag
            dsp_scr[:, blk] += jnp.sum(dla * (-LRU_C * rv), axis=0, keepdims=True)
            dra = dla * (-LRU_C * sp) * rv * (1.0 - rv)
            dia = d_iu * ucv * iv * (1.0 - iv)
            drab, diab, ucb = dra.astype(BF16), dia.astype(BF16), ucv.astype(BF16)
            duc_ref[:, blk] = (d_iu * iv
                               + lax.dot_general(drab, wa_ref[n], NT_DIMS, preferred_element_type=F32)
                               + lax.dot_general(diab, wi_ref[n], NT_DIMS, preferred_element_type=F32))
            dwa_ref[n] += lax.dot_general(ucb, drab, TN_DIMS, preferred_element_type=F32)
            dwi_ref[n] += lax.dot_general(ucb, diab, TN_DIMS, preferred_element_type=F32)
            dba_ref[:, blk] += jnp.sum(dra, axis=0, keepdims=True)
            dbi_ref[:, blk] += jnp.sum(dia, axis=0, keepdims=True)

        @pl.when(step_id == nt - 1)
        def _():
            dlam_ref[...] = -dsp_scr[...] * _sigmoid(-lam_ref[...])

    row = pl.BlockSpec((tm, width), lambda i: (nt - 1 - i, 0))
    prev8 = pl.BlockSpec((8, width), lambda i: (jnp.maximum((nt - 1 - i) * (tm // 8) - 1, 0), 0))
    vec = pl.BlockSpec((1, width), lambda i: (0, 0))
    mat = pl.BlockSpec(wa.shape, lambda i: (0, 0, 0))
    return pl.pallas_call(
        body, name="rnn_bwd", grid=(nt,),
        in_specs=[row, row, row, prev8, row, row, row, vec, mat, mat],
        out_specs=[row, mat, mat, vec, vec, vec],
        out_shape=[jax.ShapeDtypeStruct((s_len, width), F32), jax.ShapeDtypeStruct(wa.shape, F32),
                   jax.ShapeDtypeStruct(wa.shape, F32)] + [jax.ShapeDtypeStruct((1, width), F32)] * 3,
        scratch_shapes=[pltpu.VMEM((tm, width), F32), pltpu.VMEM((1, width), F32), pltpu.VMEM((1, width), F32)],
        compiler_params=pltpu.CompilerParams(dimension_semantics=("arbitrary",)),
    )(dh, a, h, h, r, ig, uc, lam, wa, wi)


def _conv_bwd(duc, u, conv_w):
    s_len, width = duc.shape
    tm = min(256, s_len)
    nt = s_len // tm

    def body(d_ref, dn_ref, u_ref, up_ref, cw_ref, du_ref, dcw_ref, dcb_ref):
        step_id = pl.program_id(0)

        @pl.when(step_id == 0)
        def _():
            dcw_ref[...] = jnp.zeros_like(dcw_ref)
            dcb_ref[...] = jnp.zeros_like(dcb_ref)

        for n in range(width // RNN_BLOCK):
            blk = slice(n * RNN_BLOCK, (n + 1) * RNN_BLOCK)
            dt = d_ref[:, blk]
            ut = u_ref[:, blk]
            nxt = jnp.where(step_id < nt - 1, dn_ref[:, blk], 0.0)
            prev = jnp.where(step_id > 0, up_ref[:, blk], 0.0)
            du = cw_ref[3:4, blk] * dt
            dcw_ref[3:4, blk] += jnp.sum(dt * ut, axis=0, keepdims=True)
            for k in (1, 2, 3):
                du = du + cw_ref[3 - k:4 - k, blk] * _shift_up(dt, nxt, k)
                dcw_ref[3 - k:4 - k, blk] += jnp.sum(dt * _shift_down(ut, prev, k), axis=0, keepdims=True)
            du_ref[:, blk] = du.astype(BF16)
            dcb_ref[:, blk] += jnp.sum(dt, axis=0, keepdims=True)

    row = pl.BlockSpec((tm, width), lambda i: (i, 0))
    prev8 = pl.BlockSpec((8, width), lambda i: (jnp.maximum(i * (tm // 8) - 1, 0), 0))
    next8 = pl.BlockSpec((8, width), lambda i: (jnp.minimum((i + 1) * (tm // 8), s_len // 8 - 1), 0))
    return pl.pallas_call(
        body, name="conv_bwd", grid=(nt,),
        in_specs=[row, next8, row, prev8, pl.BlockSpec(conv_w.shape, lambda i: (0, 0))],
        out_specs=[row, pl.BlockSpec(conv_w.shape, lambda i: (0, 0)), pl.BlockSpec((1, width), lambda i: (0, 0))],
        out_shape=[jax.ShapeDtypeStruct((s_len, width), BF16), jax.ShapeDtypeStruct(conv_w.shape, F32),
                   jax.ShapeDtypeStruct((1, width), F32)],
        compiler_params=pltpu.CompilerParams(dimension_semantics=("arbitrary",)),
    )(duc, duc, u, u, conv_w)


def _pair_sum(core, grads, theirs, first_row, out_dtype):
    n, half, _ = theirs.shape
    tb = 128 if half % 128 == 0 and first_row % 128 == 0 else half
    assert first_row % tb == 0 and half % tb == 0

    def body(c_ref, a_ref, b_ref, o_ref):
        o_ref[...] = (a_ref[...] + b_ref[...]).astype(out_dtype)

    blk = pl.BlockSpec((n, tb, LANES), lambda i, c: (0, i, 0))
    mine = pl.BlockSpec((n, tb, LANES), lambda i, c: (0, first_row // tb + c[0] * (half // tb) + i, 0))
    return pl.pallas_call(
        body, name="pair_sum",
        grid_spec=pltpu.PrefetchScalarGridSpec(
            num_scalar_prefetch=1, grid=(half // tb,), in_specs=[mine, blk], out_specs=blk),
        out_shape=jax.ShapeDtypeStruct((n, half, LANES), out_dtype),
        compiler_params=pltpu.CompilerParams(dimension_semantics=("parallel",)),
    )(core, grads, theirs)


def _sum_chips(parts):
    rows = parts.shape[1]
    tb = 128 if rows % 128 == 0 else rows

    def body(p_ref, o_ref):
        o_ref[...] = ((p_ref[0].astype(F32) + p_ref[1].astype(F32)) + p_ref[2].astype(F32)) + p_ref[3].astype(F32)

    return pl.pallas_call(
        body, name="chip_sum", grid=(rows // tb,),
        in_specs=[pl.BlockSpec((N_CHIPS, tb, LANES), lambda i: (0, i, 0))],
        out_specs=pl.BlockSpec((tb, LANES), lambda i: (i, 0)),
        out_shape=jax.ShapeDtypeStruct((rows, LANES), F32),
        compiler_params=pltpu.CompilerParams(dimension_semantics=("parallel",)),
    )(parts)


def _adamw(w, g, m, v, lead_per_block, rows_per_block):
    n, rows, cols = w.shape
    blk = pl.BlockSpec((lead_per_block, rows_per_block, cols), lambda i, j: (i, j, 0))

    def body(w_ref, g_ref, m_ref, v_ref, d_ref, nm_ref, nv_ref):
        gt = g_ref[...]
        nm = ADAM_B1 * m_ref[...] + (1.0 - ADAM_B1) * gt
        nv = ADAM_B2 * v_ref[...] + (1.0 - ADAM_B2) * (gt * gt)
        m_hat = nm / (1.0 - ADAM_B1 ** ADAM_STEP)
        v_hat = nv / (1.0 - ADAM_B2 ** ADAM_STEP)
        d_ref[...] = -ADAM_LR * (m_hat / (jnp.sqrt(v_hat) + ADAM_EPS) + ADAM_WD * w_ref[...])
        nm_ref[...] = nm
        nv_ref[...] = nv

    return pl.pallas_call(
        body, name="adamw", grid=(n // lead_per_block, rows // rows_per_block), in_specs=[blk] * 4,
        out_specs=[blk] * 3,
        out_shape=[jax.ShapeDtypeStruct(w.shape, F32)] * 3,
        compiler_params=pltpu.CompilerParams(dimension_semantics=("parallel", "parallel")),
    )(w, g, m, v)


def _place():
    x, y, c = lax.axis_index("x"), lax.axis_index("y"), lax.axis_index("c")
    return x, y, c, [(1 - x, y), (x, 1 - y), (1 - x, 1 - y)]


ANY = pl.BlockSpec(memory_space=pl.ANY)
COPY_PARTS = 4


def _remote_parts(src_of, dst_of, rows, send_sem, recv_sem, to, begin=True):
    parts = COPY_PARTS if rows % (16 * COPY_PARTS) == 0 else 1
    step = rows // parts
    for i in range(parts if begin is not False else 0):
        pltpu.make_async_remote_copy(src_ref=src_of(i * step, step), dst_ref=dst_of(i * step, step),
                                     send_sem=send_sem, recv_sem=recv_sem, device_id=to, device_id_type=MESH).start()
    if begin == "only":
        return None
    return pltpu.make_async_remote_copy(src_ref=src_of(0, rows), dst_ref=dst_of(0, rows), send_sem=send_sem,
                                        recv_sem=recv_sem, device_id=to, device_id_type=MESH)


GATHER_SEMS = 7


def _gather_steps(p_refs, o_refs, send_sems, recv_sems):
    x, y, c, chips = _place()
    me = 2 * x + y

    def half(ref, which):
        rows = ref.shape[0] // 2
        return ref.at[pl.ds(which * rows, rows)]

    def copy(k, src, dst, to):
        return pltpu.make_async_remote_copy(src_ref=src, dst_ref=dst, send_sem=send_sems.at[k],
                                            recv_sem=recv_sems.at[k], device_id=to, device_id_type=MESH)

    def first_copies():
        out = []
        for b, (p_ref, o_ref) in enumerate(zip(p_refs, o_refs)):
            for j, (cx, cy) in enumerate(chips):
                out.append(copy(GATHER_SEMS * b + j, half(p_ref, c), half(o_ref.at[me], c), (cx, cy, c)))
            out.append(copy(GATHER_SEMS * b + 6, p_ref, o_ref.at[me], (x, y, 1 - c)))
        return out

    def passed_copies():
        out = []
        for b, o_ref in enumerate(o_refs):
            for j, (cx, cy) in enumerate(chips):
                landed = half(o_ref.at[2 * cx + cy], c)
                out.append(copy(GATHER_SEMS * b + 3 + j, landed, landed, (x, y, 1 - c)))
        return out

    def start():
        for cp in first_copies():
            cp.start()

    def forward():
        for b, o_ref in enumerate(o_refs):
            for j, (cx, cy) in enumerate(chips):
                landed = half(o_ref.at[2 * cx + cy], c)
                copy(GATHER_SEMS * b + j, landed, landed, (x, y, c)).wait_recv()
        for cp in passed_copies():
            cp.start()

    def finish():
        for b, o_ref in enumerate(o_refs):
            for j, (cx, cy) in enumerate(chips):
                other = half(o_ref.at[2 * cx + cy], 1 - c)
                copy(GATHER_SEMS * b + 3 + j, other, other, (x, y, c)).wait_recv()
            copy(GATHER_SEMS * b + 6, o_ref.at[me], o_ref.at[me], (x, y, c)).wait_recv()
        for cp in first_copies() + passed_copies():
            cp.wait_send()

    return start, forward, finish


def _gather_chips(shards):
    nb = len(shards)

    def body(*refs):
        for step in _gather_steps(refs[:nb], refs[nb:2 * nb], *refs[2 * nb:]):
            step()

    return pl.pallas_call(
        body, name="gather_chips", in_specs=[ANY] * nb, out_specs=[ANY] * nb,
        out_shape=[jax.ShapeDtypeStruct((N_CHIPS,) + s.shape, s.dtype) for s in shards],
        scratch_shapes=[pltpu.SemaphoreType.DMA((GATHER_SEMS * nb,)), pltpu.SemaphoreType.DMA((GATHER_SEMS * nb,))],
    )(*shards)


def _swap_steps(g_refs, t_refs, spans, send_sems, recv_sems):
    x, y, c, _ = _place()

    def gives(begin):
        return [_remote_parts(
            lambda r0, m, g_ref=g_ref, j=j, base=first_row + (1 - c) * half: g_ref.at[j, pl.ds(base + r0, m), :],
            lambda r0, m, t_ref=t_ref, j=j: t_ref.at[j, pl.ds(r0, m), :],
            half, send_sems.at[b * N_CHIPS + j], recv_sems.at[b * N_CHIPS + j], (x, y, 1 - c), begin)
            for b, (g_ref, t_ref, (first_row, half)) in enumerate(zip(g_refs, t_refs, spans)) for j in range(N_CHIPS)]

    def start():
        gives("only")

    def finish():
        for give in gives(False):
            give.wait()

    return start, finish


def _swap_halves(bufs, spans):
    nb = len(bufs)

    def body(*refs):
        for step in _swap_steps(refs[:nb], refs[nb:2 * nb], spans, *refs[2 * nb:]):
            step()

    return pl.pallas_call(
        body, name="swap_halves", in_specs=[ANY] * nb, out_specs=[ANY] * nb,
        out_shape=[jax.ShapeDtypeStruct((b.shape[0], half, b.shape[2]), b.dtype) for b, (_, half) in zip(bufs, spans)],
        scratch_shapes=[pltpu.SemaphoreType.DMA((nb * N_CHIPS,)), pltpu.SemaphoreType.DMA((nb * N_CHIPS,))],
    )(*bufs)


def _scatter_steps(t_refs, o_refs, send_sems, recv_sems):
    x, y, c, chips = _place()
    me = 2 * x + y

    def slot(ref, chip):
        return lambda r0, n: ref.at[chip, pl.ds(r0, n), :]

    def sends(begin):
        return [_remote_parts(slot(t_ref, 2 * cx + cy), slot(o_ref, me), t_ref.shape[1], send_sems.at[3 * b + j],
                              recv_sems.at[3 * b + j], (cx, cy, c), begin)
                for b, (t_ref, o_ref) in enumerate(zip(t_refs, o_refs)) for j, (cx, cy) in enumerate(chips)]

    def start():
        sends("only")

    def finish():
        for b, o_ref in enumerate(o_refs):
            for j, (cx, cy) in enumerate(chips):
                landed = o_ref.at[2 * cx + cy]
                pltpu.make_async_remote_copy(src_ref=landed, dst_ref=landed, send_sem=send_sems.at[3 * b + j],
                                             recv_sem=recv_sems.at[3 * b + j], device_id=(x, y, c),
                                             device_id_type=MESH).wait_recv()
        for cp in sends(False):
            cp.wait_send()

    return start, finish


def _give_sibling(bufs):
    nb = len(bufs)

    def body(*refs):
        h_refs, o_refs, (send_sems, recv_sems) = refs[:nb], refs[nb:2 * nb], refs[2 * nb:]
        x, y, c, _ = _place()
        gives = [_remote_parts(lambda r0, n, h_ref=h_ref: h_ref.at[pl.ds(r0, n), :],
                               lambda r0, n, o_ref=o_ref: o_ref.at[pl.ds(r0, n), :],
                               h_ref.shape[0], send_sems.at[b], recv_sems.at[b], (x, y, 1 - c))
                 for b, (h_ref, o_ref) in enumerate(zip(h_refs, o_refs))]
        for give in gives:
            give.wait()

    return pl.pallas_call(
        body, name="give_sibling", in_specs=[ANY] * nb, out_specs=[ANY] * nb,
        out_shape=[jax.ShapeDtypeStruct(b.shape, b.dtype) for b in bufs],
        scratch_shapes=[pltpu.SemaphoreType.DMA((nb,)), pltpu.SemaphoreType.DMA((nb,))],
    )(*bufs)


def _pack(arrays, dtype, row_align):
    flat = []
    for arr in arrays:
        v = arr.reshape(-1)
        flat.append(jnp.pad(v, (0, (-v.shape[0]) % LANES)))
    total = sum(f.shape[0] for f in flat) // LANES
    pad_rows = (-total) % row_align
    if pad_rows:
        flat.append(jnp.zeros((pad_rows * LANES,), dtype))
    return jnp.concatenate(flat).reshape(-1, LANES)


def _unpack(buf, shapes, rows_align=1):
    lead = buf.shape[:-2]
    flat = buf.reshape(lead + (-1,))
    out, off = [], 0
    for shape in shapes:
        size = 1
        for dim in shape:
            size *= dim
        out.append(flat[..., off:off + size].reshape(lead + tuple(shape)))
        off += size + (-size) % (LANES * rows_align)
    return out


def _from_chips(parts, axis):
    return jnp.concatenate([parts[j] for j in range(N_CHIPS)], axis=axis)


def kernel(x, ln_g, ln_b, attn_w_in, attn_b_f, attn_w_out, rnn_w_in, rnn_conv_w, rnn_conv_b, rnn_w_a, rnn_b_a, rnn_w_i, rnn_b_i, rnn_lambda, rnn_w_out, loss_target, m_ln_g, m_ln_b, m_attn_w_in, m_attn_b_f, m_attn_w_out, m_rnn_w_in, m_rnn_conv_w, m_rnn_conv_b, m_rnn_w_a, m_rnn_b_a, m_rnn_w_i, m_rnn_b_i, m_rnn_lambda, m_rnn_w_out, v_ln_g, v_ln_b, v_attn_w_in, v_attn_b_f, v_attn_w_out, v_rnn_w_in, v_rnn_conv_w, v_rnn_conv_b, v_rnn_w_a, v_rnn_b_a, v_rnn_w_i, v_rnn_b_i, v_rnn_lambda, v_rnn_w_out):
    s_len, d = x.shape[1], x.shape[2]
    xs = x.reshape(s_len, d)
    target = loss_target.reshape(s_len, d)
    heads = attn_b_f.shape[1]
    qkvg = attn_w_in.shape[2] * N_CHIPS - heads

    me = 2 * lax.axis_index("x") + lax.axis_index("y")
    core = lax.axis_index("c").astype(jnp.int32)
    small = [rnn_conv_w, rnn_conv_b, rnn_b_a, rnn_b_i, rnn_lambda]
    a_in, a_out = attn_w_in.astype(BF16), attn_w_out.astype(BF16)
    later = [a_in[1], a_out] + [w.astype(BF16) for w in (rnn_w_in, rnn_w_a, rnn_w_i, rnn_w_out)]
    got_in, got_small = _gather_chips([a_in[0], _pack(small, F32, 16)])
    conv_w, conv_b, b_a, b_i, lam = [
        _from_chips(p, ax) for p, ax in zip(_unpack(got_small, [w.shape for w in small]), (2, 1, 1, 1, 1))]

    def attn_in_weights(w_in):
        shard = w_in.shape[2]

        def columns(first, last):
            return [w_in[j][:, max(first - j * shard, 0):min(last - j * shard, shard)]
                    for j in range(N_CHIPS) if first < (j + 1) * shard and last > j * shard]

        return jnp.concatenate([t * (HEAD_DIM ** -0.5) for t in columns(0, d)] + columns(d, qkvg + heads)
                               + [jnp.zeros((d, 128 - heads), BF16)], axis=1)

    w_cat = [attn_in_weights(got_in), None]
    b_f = jnp.pad(attn_b_f, ((0, 0), (0, 128 - heads)))

    saved = []
    cur = xs
    for layer in range(DEPTH):
        idx = layer // 2
        g_l, b_l = ln_g[layer:layer + 1], ln_b[layer:layer + 1]
        if layer % 2 == 0:
            q, k, v, gate, fl = _proj(cur, w_cat[idx], [(0, d, BF16), (d, d, BF16), (2 * d, d, BF16),
                                                         (3 * d, d, F32), (4 * d, 128, F32)], "attn_proj")
            cum, sneg = _fcum(fl, b_f[idx:idx + 1])
            o, lse, *rest = _flash_fwd(q, k, v, cum, later if layer == 0 else ())
            if layer == 0:
                w_cat[1] = attn_in_weights(rest[0])
                w_out_a = jnp.moveaxis(rest[1], 0, 1).reshape(2, d, d)
                w_in_r = jnp.moveaxis(rest[2], 0, 2).reshape(2, d, 2 * d)
                w_a = jnp.transpose(rest[3], (1, 2, 0, 3, 4)).reshape(2, -1, RNN_BLOCK, RNN_BLOCK)
                w_i = jnp.transpose(rest[4], (1, 2, 0, 3, 4)).reshape(2, -1, RNN_BLOCK, RNN_BLOCK)
                w_out_r = jnp.moveaxis(rest[5], 0, 1).reshape(2, d, d)
            nxt, xh, rs, yg = _out_ln(o, gate, cur, w_out_a[idx], g_l, b_l, "out_ln")
            saved.append(dict(x=cur, q=q, k=k, v=v, gate=gate, cum=cum, sneg=sneg, a=o, lse=lse, xh=xh, rs=rs, yg=yg))
        else:
            u, gate = _proj(cur, w_in_r[idx], [(0, d, F32), (d, d, F32)], "rnn_proj")
            vecs = [t[idx:idx + 1] for t in (conv_b, b_a, b_i, lam)]
            hseq, uc, r, ig, a = _rnn_fwd(u, conv_w[idx], vecs[0], w_a[idx], vecs[1], w_i[idx], vecs[2], vecs[3])
            nxt, xh, rs, yg = _out_ln(hseq, gate, cur, w_out_r[idx], g_l, b_l, "out_ln")
            saved.append(dict(x=cur, u=u, gate=gate, uc=uc, r=r, ig=ig, av=a, a=hseq, xh=xh, rs=rs, yg=yg, lam=vecs[3]))
        cur = nxt

    dout, sq = _loss_grad(cur, target)
    loss_here = 0.5 * jnp.sum(sq) / d

    g_ln_g, g_ln_b = [None] * DEPTH, [None] * DEPTH
    g_attn = [None] * 2
    g_rnn = [None] * 2
    slots = None
    core1 = core.reshape(1)
    late_half = (SLOT_ROWS - LATE_ROWS) // 2

    def by_chip(t, axis):
        shape = t.shape[:axis] + (N_CHIPS, t.shape[axis] // N_CHIPS) + t.shape[axis + 1:]
        flat = jnp.moveaxis(t.reshape(shape), axis, 0).reshape(N_CHIPS, -1)
        return jnp.pad(flat, ((0, 0), (0, (-flat.shape[1]) % (8 * LANES)))).reshape(N_CHIPS, -1, LANES)

    def both(key):
        return jnp.stack([it[key] for it in g_rnn])

    for layer in reversed(range(DEPTH)):
        idx = layer // 2
        sv = saved[layer]
        swap = None
        if layer == 0:
            gates = jnp.concatenate([by_chip(both("w_a"), 2), by_chip(both("w_i"), 2)], axis=1)
            slots = lax.dynamic_update_slice(slots, gates, (0, ROWS_GATES, 0))
            swap = (slots, (LATE_ROWS, late_half))
        w_out = w_out_a[idx] if layer % 2 == 0 else w_out_r[idx]
        dz, da, dgate, dg, db, *theirs_late = _ln_bwd(dout, sv["xh"], sv["rs"], ln_g[layer:layer + 1], w_out,
                                                      sv["gate"], sv["a"], "ln_bwd", swap)
        if layer == 0:
            pair_late = _pair_sum(core1, slots, theirs_late[0], LATE_ROWS, BF16)
        g_ln_g[layer], g_ln_b[layer] = dg, db
        slots = _dw_out(slots, sv["yg"], dz, (ROWS_ATTN_OUT if layer % 2 == 0 else ROWS_RNN_OUT)[idx])
        if layer % 2 == 0:
            dq, dk, dv, dcum, *arrived = _flash_bwd(sv["q"], sv["k"], sv["v"], sv["a"], da, sv["lse"], sv["cum"],
                                                    [pair_late] if layer == 0 else ())
            dlogit, dbf = _fbwd(dcum, sv["sneg"])
            pieces = [dq, dk, dv, dgate, dlogit]
            if layer > 0:
                dout, = _mm_nt(dz, [(p, j * d) for j, p in enumerate(pieces)], w_cat[idx], "attn_dx")
            slots, d_w_f = _dw_attn_in(slots, sv["x"], pieces[:4], dlogit, idx)
            g_attn[idx] = dict(w_f=d_w_f[:, :heads], b_f=dbf[:, 0])
        else:
            duc, d_wa, d_wi, d_ba, d_bi, d_lam = _rnn_bwd(da, sv["av"], sv["a"], sv["r"], sv["ig"], sv["uc"], sv["lam"],
                                                          w_a[idx], w_i[idx])
            du, d_cw, d_cb = _conv_bwd(duc, sv["u"], conv_w[idx])
            dout, = _mm_nt(dz, [(du, 0), (dgate, d)], w_in_r[idx], "rnn_dx")
            slots = _dw_rnn_in(slots, sv["x"], (du, dgate), idx)
            g_rnn[idx] = dict(conv_w=d_cw, conv_b=d_cb[0], w_a=d_wa, b_a=d_ba[0], w_i=d_wi, b_i=d_bi[0], lam=d_lam[0])

    def everywhere(t):
        flat = t.reshape(-1)
        flat = jnp.pad(flat, (0, (-flat.shape[0]) % (8 * LANES))).reshape(-1, LANES)
        return jnp.broadcast_to(flat[None], (N_CHIPS,) + flat.shape)

    in_rows = jnp.stack([slots[1:, r:r + d, :heads] for r in ROWS_ATTN_IN], axis=1)
    edges = jnp.concatenate([in_rows, jnp.stack([it["w_f"] for it in g_attn])[None]])
    rows = [everywhere(edges), by_chip(both("conv_w"), 2),
            by_chip(both("conv_b"), 1), by_chip(both("b_a"), 1), by_chip(both("b_i"), 1), by_chip(both("lam"), 1),
            everywhere(jnp.concatenate(g_ln_g)), everywhere(jnp.concatenate(g_ln_b)),
            everywhere(jnp.stack([it["b_f"] for it in g_attn])), everywhere(loss_here)]
    used = sum(r.shape[1] for r in rows)
    small = jnp.concatenate(rows + [jnp.zeros((N_CHIPS, (-used) % 32, LANES), F32)], axis=1)

    spans = [(0, LATE_ROWS // 2), (0, small.shape[1] // 2)]
    theirs = _swap_halves([slots, small], spans)
    pair = [_pair_sum(core1, slots, theirs[0], 0, BF16), _pair_sum(core1, small, theirs[1], 0, F32)]
    dout, *arrived_last = _mm_nt(dz, [(p, j * d) for j, p in enumerate(pieces)], w_cat[0], "attn_dx", pair)
    grad_x = dout.reshape(x.shape)
    summed = []
    for mine_all, got in zip([pair_late] + pair, list(arrived) + arrived_last):
        own = lax.dynamic_index_in_dim(mine_all, me, 0, keepdims=False)
        summed.append(_sum_chips(lax.dynamic_update_index_in_dim(got, own, me, 0)))
    late, early, small_sum = [
        jnp.concatenate([jnp.where(core == 0, mine, other), jnp.where(core == 0, other, mine)])
        for mine, other in zip(summed, _give_sibling(summed))]

    def rows_at(first, n):
        return early[first:first + n] if first < LATE_ROWS else late[first - LATE_ROWS:first - LATE_ROWS + n]

    g_in_group = jnp.stack([rows_at(r, d) for r in ROWS_ATTN_IN])
    g_w_out_a = jnp.stack([rows_at(r, d // N_CHIPS) for r in ROWS_ATTN_OUT])
    g_w_out_r = jnp.stack([rows_at(r, d // N_CHIPS) for r in ROWS_RNN_OUT])
    folded = jnp.stack([rows_at(r, d // 2) for r in ROWS_RNN_IN])
    g_w_in_r = jnp.concatenate([folded[:, :, :d // 2], folded[:, :, d // 2:]], axis=1)
    gate_rows = rnn_w_a.size // LANES
    g_w_a = rows_at(ROWS_GATES, gate_rows).reshape(rnn_w_a.shape)
    g_w_i = rows_at(ROWS_GATES + gate_rows, gate_rows).reshape(rnn_w_i.shape)
    (g_edges, g_conv_w, g_conv_b, g_b_a, g_b_i, g_lam, g_ln_g_sum, g_ln_b_sum, g_b_f,
     loss) = _unpack(small_sum, [
        (N_CHIPS, 2, d, heads), rnn_conv_w.shape, rnn_conv_b.shape, rnn_b_a.shape,
        rnn_b_i.shape, rnn_lambda.shape, ln_g.shape, ln_b.shape, attn_b_f.shape, ()], rows_align=8)
    wide = jnp.concatenate([g_in_group, lax.dynamic_index_in_dim(g_edges, me, 0, keepdims=False)], axis=2)
    g_w_in_a = lax.dynamic_slice(wide, (0, 0, (heads // N_CHIPS) * me), attn_w_in.shape)

    def adam_nd(w, g, m, v, view, rows_per_block):
        outs = _adamw(w.reshape(view), g.reshape(view), m.reshape(view), v.reshape(view), 1, rows_per_block)
        return [t.reshape(w.shape) for t in outs]

    turned = [jnp.transpose(t, (2, 0, 1)) for t in (attn_w_in, g_w_in_a, m_attn_w_in, v_attn_w_in)]
    upd = {
        "attn_w_in": [jnp.transpose(t, (1, 2, 0)) for t in _adamw(*turned, attn_w_in.shape[2] // N_CHIPS, 2)],
        "attn_w_out": adam_nd(attn_w_out, g_w_out_a, m_attn_w_out, v_attn_w_out, attn_w_out.shape, 256),
        "rnn_w_in": adam_nd(rnn_w_in, g_w_in_r, m_rnn_w_in, v_rnn_w_in, rnn_w_in.shape, 512),
        "rnn_w_a": adam_nd(rnn_w_a, g_w_a, m_rnn_w_a, v_rnn_w_a, (1, -1, RNN_BLOCK), 512),
        "rnn_w_i": adam_nd(rnn_w_i, g_w_i, m_rnn_w_i, v_rnn_w_i, (1, -1, RNN_BLOCK), 512),
        "rnn_w_out": adam_nd(rnn_w_out, g_w_out_r, m_rnn_w_out, v_rnn_w_out, rnn_w_out.shape, 256),
    }
    smalls = [rnn_conv_w, rnn_conv_b, rnn_b_a, rnn_b_i, rnn_lambda, ln_g, ln_b, attn_b_f]
    g_small = [g_conv_w, g_conv_b, g_b_a, g_b_i, g_lam, g_ln_g_sum, g_ln_b_sum, g_b_f]
    m_small = [m_rnn_conv_w, m_rnn_conv_b, m_rnn_b_a, m_rnn_b_i, m_rnn_lambda, m_ln_g, m_ln_b, m_attn_b_f]
    v_small = [v_rnn_conv_w, v_rnn_conv_b, v_rnn_b_a, v_rnn_b_i, v_rnn_lambda, v_ln_g, v_ln_b, v_attn_b_f]
    packs = [_pack(t, F32, 16)[None] for t in (smalls, g_small, m_small, v_small)]
    small_out = [_unpack(t[0], [w.shape for w in smalls]) for t in _adamw(*packs, 1, 16)]
    for k, name in enumerate(("rnn_conv_w", "rnn_conv_b", "rnn_b_a", "rnn_b_i", "rnn_lambda", "ln_g", "ln_b",
                              "attn_b_f")):
        upd[name] = [small_out[0][k], small_out[1][k], small_out[2][k]]
    grad = {"attn_w_in": g_w_in_a, "attn_w_out": g_w_out_a, "rnn_w_in": g_w_in_r, "rnn_w_a": g_w_a, "rnn_w_i": g_w_i,
            "rnn_w_out": g_w_out_r, "rnn_conv_w": g_conv_w, "rnn_conv_b": g_conv_b, "rnn_b_a": g_b_a,
            "rnn_b_i": g_b_i, "rnn_lambda": g_lam, "ln_g": g_ln_g_sum, "ln_b": g_ln_b_sum, "attn_b_f": g_b_f}
    names = ("ln_g", "ln_b", "attn_w_in", "attn_b_f", "attn_w_out", "rnn_w_in", "rnn_conv_w", "rnn_conv_b",
             "rnn_w_a", "rnn_b_a", "rnn_w_i", "rnn_b_i", "rnn_lambda", "rnn_w_out")
    return (loss, grad_x, *[grad[n] for n in names], *[upd[n][0] for n in names], *[upd[n][1] for n in names],
            *[upd[n][2] for n in names])
```

```python
import jax
import jax.numpy as jnp
from jax import lax
from jax.experimental import pallas as pl
from jax.experimental.pallas import tpu as pltpu

F32 = jnp.float32
BF16 = jnp.bfloat16
MESH = pl.DeviceIdType.MESH

DEPTH = 4
HEAD_DIM = 64
HEAD_PAIR = 2 * HEAD_DIM
RNN_BLOCK = 256
CONV_WIDTH = 4
LRU_C = 8.0
ALPHA = (2.0 * DEPTH) ** 0.25
LN_EPS = 1e-5
ADAM_LR, ADAM_B1, ADAM_B2, ADAM_EPS, ADAM_WD, ADAM_STEP = 0.001, 0.9, 0.999, 1e-08, 0.01, 10
N_CHIPS = 4
LANES = 1024
NEG = -1e30

NT_DIMS = (((1,), (1,)), ((), ()))
TN_DIMS = (((0,), (0,)), ((), ()))


def _sigmoid(z):
    return 1.0 / (1.0 + jnp.exp(-z))


def _softplus(z):
    return jnp.maximum(z, 0.0) + jnp.log(1.0 + jnp.exp(-jnp.abs(z)))


def _neg_expm1(y):
    poly = y * (1.0 + y * (0.5 + y * (1.0 / 6.0 + y * (1.0 / 24.0))))
    return -jnp.where(y > -0.05, poly, jnp.exp(y) - 1.0)


def _shift_down(cur, prev8, k):
    n = cur.shape[0]
    row = lax.broadcasted_iota(jnp.int32, cur.shape, 0)
    fix = jnp.tile(pltpu.roll(prev8, k, 0), (n // 8, 1))
    return jnp.where(row < k, fix, pltpu.roll(cur, k, 0))


def _shift_up(cur, next8, k):
    n = cur.shape[0]
    row = lax.broadcasted_iota(jnp.int32, cur.shape, 0)
    fix = jnp.tile(pltpu.roll(next8, 8 - k, 0), (n // 8, 1))
    return jnp.where(row >= n - k, fix, pltpu.roll(cur, n - k, 0))


def _resident(w):
    if isinstance(w, tuple):
        stack, layer = w
        return stack, pl.BlockSpec((None,) + stack.shape[1:], lambda i: (layer,) + (0,) * (stack.ndim - 1))
    return w, pl.BlockSpec(w.shape, lambda i: (0,) * w.ndim)


def _proj(x, w, outs, name):
    s_len, d = x.shape
    tm = min(512, s_len)
    w, w_spec = _resident(w)

    def body(x_ref, w_ref, *o_refs):
        xb = x_ref[...].astype(BF16)
        for (c0, wd, dt), o_ref in zip(outs, o_refs):
            step = min(wd, 512)
            for cc in range(0, wd, step):
                o_ref[:, cc:cc + step] = jnp.dot(
                    xb, w_ref[:, c0 + cc:c0 + cc + step], preferred_element_type=F32).astype(dt)

    return pl.pallas_call(
        body, name=name, grid=(s_len // tm,),
        in_specs=[pl.BlockSpec((tm, d), lambda i: (i, 0)), w_spec],
        out_specs=[pl.BlockSpec((tm, wd), lambda i: (i, 0)) for _, wd, _ in outs],
        out_shape=[jax.ShapeDtypeStruct((s_len, wd), dt) for _, wd, dt in outs],
        compiler_params=pltpu.CompilerParams(dimension_semantics=("parallel",)),
    )(x, w)


def _mm_nt(dz, pieces, w, name, outgoing=()):
    s_len, d = dz.shape
    tm = min(256, s_len)
    steps = s_len // tm
    n = len(pieces)
    ns = len(outgoing)
    w, w_spec = _resident(w)
    cols = [(c0, p.shape[1]) for p, c0 in pieces]

    def body(*refs):
        dz_ref, p_refs, w_ref, o_ref = refs[0], refs[1:1 + n], refs[1 + n], refs[2 + n + ns]
        if ns:
            start, finish = _scatter_steps(refs[2 + n:2 + n + ns], refs[3 + n + ns:3 + n + 2 * ns],
                                           *refs[3 + n + 2 * ns:])
            pl.when(pl.program_id(0) == 0)(start)
        acc = ALPHA * dz_ref[...]
        for (c0, wd), p_ref in zip(cols, p_refs):
            acc = acc + lax.dot_general(p_ref[...], w_ref[:, c0:c0 + wd], NT_DIMS, preferred_element_type=F32)
        o_ref[...] = acc
        if ns:
            pl.when(pl.program_id(0) == steps - 1)(finish)

    out, *arrived = pl.pallas_call(
        body, name=name + "_scatter" if ns else name, grid=(steps,),
        in_specs=[pl.BlockSpec((tm, d), lambda i: (i, 0))]
        + [pl.BlockSpec((tm, wd), lambda i: (i, 0)) for _, wd in cols]
        + [w_spec] + [ANY] * ns,
        out_specs=[pl.BlockSpec((tm, d), lambda i: (i, 0))] + [ANY] * ns,
        out_shape=[jax.ShapeDtypeStruct((s_len, d), F32)] + [jax.ShapeDtypeStruct(t.shape, t.dtype) for t in outgoing],
        scratch_shapes=[pltpu.SemaphoreType.DMA((3 * ns,))] * 2 if ns else [],
        compiler_params=pltpu.CompilerParams(dimension_semantics=("arbitrary" if ns else "parallel",)),
    )(dz, *[p for p, _ in pieces], w, *outgoing)
    return (out, *arrived)


ROWS_ATTN_IN = (0, 2048)
ROWS_ATTN_OUT = (1024, 1280)
ROWS_RNN_OUT = (1536, 1792)
ROWS_RNN_IN = (3072, 3584)
ROWS_GATES = 4096
LATE_ROWS = 1280
SLOT_ROWS = 4352


DW_ROWS = 1024


def _dw_call(body, name, slots, operands, specs, out_block, out_index, grid, semantics):
    return pl.pallas_call(
        body, name=name, grid=grid,
        in_specs=specs if slots is None else [ANY] + specs,
        out_specs=pl.BlockSpec(out_block, out_index),
        out_shape=jax.ShapeDtypeStruct((N_CHIPS, SLOT_ROWS, LANES), F32),
        input_output_aliases={} if slots is None else {0: 0},
        compiler_params=pltpu.CompilerParams(dimension_semantics=semantics),
    )(*(operands if slots is None else [slots] + operands))


def _dw_attn_in(slots, x, pieces, forget, layer):
    s_len, d = x.shape
    ts = min(s_len, DW_ROWS)
    steps = s_len // ts
    half = d // 2

    def body(g_ref, x_ref, p0, p1, p2, p3, f_ref, o_ref, wf_ref):
        lanes, step = pl.program_id(0), pl.program_id(1)

        @pl.when(step == 0)
        def _():
            o_ref[...] = jnp.zeros_like(o_ref)

        xb = x_ref[...].astype(BF16)
        for j, p_ref in enumerate((p0, p1, p2, p3)):
            o_ref[j] += lax.dot_general(xb, p_ref[...], TN_DIMS, preferred_element_type=F32)

        @pl.when(step == steps - 1)
        def _():
            o_ref[0] = o_ref[0] * (HEAD_DIM ** -0.5)

        @pl.when(lanes == 0)
        def _():
            @pl.when(step == 0)
            def _():
                wf_ref[...] = jnp.zeros_like(wf_ref)

            wf_ref[...] += lax.dot_general(xb, f_ref[...], TN_DIMS, preferred_element_type=F32)

    return pl.pallas_call(
        body, name="dw_attn_in", grid=(2, steps),
        in_specs=[ANY, pl.BlockSpec((ts, d), lambda i, s: (s, 0))] + [pl.BlockSpec((ts, half), lambda i, s: (s, i))] * 4
        + [pl.BlockSpec((ts, 128), lambda i, s: (s, 0))],
        out_specs=[pl.BlockSpec((N_CHIPS, d, half), lambda i, s: (0, ROWS_ATTN_IN[layer] // d, i)),
                   pl.BlockSpec((d, 128), lambda i, s: (0, 0))],
        out_shape=[jax.ShapeDtypeStruct((N_CHIPS, SLOT_ROWS, LANES), F32), jax.ShapeDtypeStruct((d, 128), F32)],
        input_output_aliases={0: 0},
        compiler_params=pltpu.CompilerParams(dimension_semantics=("arbitrary", "arbitrary")),
    )(slots, x, *pieces, forget)


def _dw_out(slots, yg, dz, first_row):
    s_len, d = dz.shape
    ts = min(s_len, DW_ROWS)
    rows = d // N_CHIPS

    def body(*refs):
        a_ref, b_ref, o_ref = refs[-3:]

        @pl.when(pl.program_id(0) == 0)
        def _():
            o_ref[...] = jnp.zeros_like(o_ref)

        prod = lax.dot_general(a_ref[...], b_ref[...].astype(BF16), TN_DIMS, preferred_element_type=F32)
        o_ref[...] += prod.reshape(N_CHIPS, rows, d)

    specs = [pl.BlockSpec((ts, d), lambda s: (s, 0))] * 2
    return _dw_call(body, "dw_out", slots, [yg, dz], specs, (N_CHIPS, rows, d), lambda s: (0, first_row // rows, 0),
                    (s_len // ts,), ("arbitrary",))


def _dw_rnn_in(slots, x, parts, layer):
    s_len, d = x.shape
    ts = min(s_len, DW_ROWS)
    half = d // 2

    def body(*refs):
        x_ref, p_refs, o_ref = refs[-4], refs[-3:-1], refs[-1]

        @pl.when(pl.program_id(0) == 0)
        def _():
            o_ref[...] = jnp.zeros_like(o_ref)

        xb = x_ref[...].astype(BF16)
        for p, p_ref in enumerate(p_refs):
            for jj in (0, 1):
                for r in (0, 1):
                    o_ref[2 * p + jj, :, r * half:(r + 1) * half] += lax.dot_general(
                        xb[:, r * half:(r + 1) * half], p_ref[:, jj * half:(jj + 1) * half], TN_DIMS,
                        preferred_element_type=F32)

    specs = [pl.BlockSpec((ts, d), lambda s: (s, 0))] * 3
    return _dw_call(body, "dw_rnn_in", slots, [x] + list(parts), specs, (N_CHIPS, half, d),
                    lambda s: (0, ROWS_RNN_IN[layer] // half, 0), (s_len // ts,), ("arbitrary",))


def _fcum(fl, bias):
    s_len = fl.shape[0]

    def body(fl_ref, b_ref, cum_ref, sg_ref):
        z = fl_ref[...] + b_ref[...]
        e = jnp.exp(-jnp.abs(z))
        logf = jnp.minimum(z, 0.0) - jnp.log(1.0 + e)
        sneg = jnp.where(z >= 0, e, 1.0) / (1.0 + e)
        run = logf.T[0:16, :]
        sg_ref[...] = sneg.T[0:16, :]
        lane = lax.broadcasted_iota(jnp.int32, (16, s_len), 1)
        sh = 1
        while sh < s_len:
            run = run + jnp.where(lane >= sh, pltpu.roll(run, sh, 1), 0.0)
            sh *= 2
        cum_ref[...] = run

    return pl.pallas_call(
        body, name="fcum",
        out_shape=[jax.ShapeDtypeStruct((16, s_len), F32), jax.ShapeDtypeStruct((16, s_len), F32)],
    )(fl, bias)


def _fbwd(dcum, sneg):
    s_len = dcum.shape[1]

    def body(dc_ref, sg_ref, dl_ref, db_ref):
        run = dc_ref[...]
        lane = lax.broadcasted_iota(jnp.int32, (16, s_len), 1)
        sh = 1
        while sh < s_len:
            run = run + jnp.where(lane < s_len - sh, pltpu.roll(run, s_len - sh, 1), 0.0)
            sh *= 2
        dlog = run * sg_ref[...]
        db_ref[...] = jnp.broadcast_to(jnp.sum(dlog, axis=1, keepdims=True), (16, 128))
        full = jnp.concatenate([dlog, jnp.zeros((112, s_len), F32)], axis=0)
        dl_ref[...] = full.T.astype(BF16)

    return pl.pallas_call(
        body, name="fbwd",
        out_shape=[jax.ShapeDtypeStruct((s_len, 128), BF16), jax.ShapeDtypeStruct((16, 128), F32)],
    )(dcum, sneg)


FWD_TILE = 1024
BWD_TILE = 512


def _flash_fwd(q, k, v, cum, shards=()):
    s_len, width = q.shape
    tile = min(FWD_TILE, s_len // 2)
    nt = s_len // tile
    reps = tile // 128
    cumr = cum.reshape(-1, tile)
    ns = len(shards)
    pairs = width // HEAD_PAIR

    def body(*refs):
        q_ref, k_ref, v_ref, cum_ref = refs[:4]
        o_ref, lse_ref = refs[4 + ns:6 + ns]
        q_t, v_t, cum_col = refs[6 + 2 * ns:9 + 2 * ns]
        pid = pl.program_id(0)
        if ns:
            start, forward, finish = _gather_steps(refs[4:4 + ns], refs[6 + ns:6 + 2 * ns], *refs[9 + 2 * ns:])
            pl.when(pid == 0)(start)
            pl.when(pid == pairs // 2)(forward)
        top = lax.broadcasted_iota(jnp.int32, (HEAD_PAIR, tile), 0) < HEAD_DIM
        causal = (lax.broadcasted_iota(jnp.int32, (tile, tile), 0)
                  <= lax.broadcasted_iota(jnp.int32, (tile, tile), 1))

        def pre(i, carry):
            r0 = pl.multiple_of(i * tile, tile)
            q_t[i] = q_ref[pl.ds(r0, tile), :].astype(F32).T.astype(BF16)
            v_t[i] = v_ref[pl.ds(r0, tile), :].astype(F32).T.astype(BF16)
            for h in (0, 1):
                row = cum_ref[pl.ds((2 * pid + h) * nt + i, 1), :]
                cum_col[h, pl.ds(r0, tile), :] = jnp.broadcast_to(row, (HEAD_PAIR, tile)).T
            return carry

        lax.fori_loop(0, nt, pre, 0)

        def q_loop(qi, carry):
            qt = q_t[qi]
            zt = jnp.zeros_like(qt)
            qms = (jnp.where(top, qt, zt), jnp.where(top, zt, qt))

            def step(kj, state, masked):
                m0, l0, m1, l1, acc = state
                k0 = pl.multiple_of(kj * tile, tile)
                kt = k_ref[pl.ds(k0, tile), :]
                vt = v_t[kj]
                ms, ls, scales, contrib = [m0, m1], [l0, l1], [], None
                for h in (0, 1):
                    s = jnp.dot(kt, qms[h], preferred_element_type=F32)
                    s = s - jnp.tile(cum_col[h, pl.ds(k0, tile), :], (1, reps))
                    if masked:
                        s = jnp.where(causal, s, NEG)
                    m_new = jnp.maximum(ms[h], jnp.max(s, axis=0, keepdims=True))
                    p = jnp.exp(s - m_new)
                    scale = jnp.exp(ms[h] - m_new)
                    ls[h] = scale * ls[h] + jnp.sum(p, axis=0, keepdims=True)
                    ms[h] = m_new
                    scales.append(scale)
                    vm = jnp.where(top, vt, zt) if h == 0 else jnp.where(top, zt, vt)
                    part = jnp.dot(vm, p.astype(BF16), preferred_element_type=F32)
                    contrib = part if contrib is None else contrib + part
                acc = jnp.where(top, scales[0], scales[1]) * acc + contrib
                return ms[0], ls[0], ms[1], ls[1], acc

            low = jnp.full((1, tile), NEG, F32)
            zero = jnp.zeros((1, tile), F32)
            state = step(qi, (low, zero, low, zero, jnp.zeros((HEAD_PAIR, tile), F32)), True)
            m0, l0, m1, l1, acc = lax.fori_loop(0, qi, lambda kj, c: step(kj, c, False), state)
            q0 = pl.multiple_of(qi * tile, tile)
            o_ref[pl.ds(q0, tile), :] = (acc / jnp.where(top, l0, l1)).T
            lse_ref[pl.ds((2 * pid) * nt + qi, 1), :] = m0 + jnp.log(l0)
            lse_ref[pl.ds((2 * pid + 1) * nt + qi, 1), :] = m1 + jnp.log(l1)
            return carry

        lax.fori_loop(0, nt, q_loop, 0)
        if ns:
            pl.when(pid == pairs - 1)(finish)

    blk = pl.BlockSpec((s_len, HEAD_PAIR), lambda p: (0, p))
    full = pl.BlockSpec(cumr.shape, lambda p: (0, 0))
    sems = [pltpu.SemaphoreType.DMA((GATHER_SEMS * ns,))] * 2 if ns else []
    o, lse, *gathered = pl.pallas_call(
        body, name="flash_fwd_gather" if ns else "flash_fwd", grid=(pairs,),
        in_specs=[blk, blk, blk, full] + [ANY] * ns,
        out_specs=[blk, full] + [ANY] * ns,
        out_shape=[jax.ShapeDtypeStruct((s_len, width), F32), jax.ShapeDtypeStruct(cumr.shape, F32)]
        + [jax.ShapeDtypeStruct((N_CHIPS,) + s.shape, s.dtype) for s in shards],
        scratch_shapes=[pltpu.VMEM((nt, HEAD_PAIR, tile), BF16), pltpu.VMEM((nt, HEAD_PAIR, tile), BF16),
                        pltpu.VMEM((2, s_len, HEAD_PAIR), F32)] + sems,
        compiler_params=pltpu.CompilerParams(dimension_semantics=("arbitrary",)),
    )(q, k, v, cumr, *shards)
    return (o, lse.reshape(cum.shape), *gathered)


def _flash_bwd(q, k, v, o, do, lse, cum, outgoing=()):
    s_len, width = q.shape
    tile = min(BWD_TILE, s_len // 2)
    nt = s_len // tile
    reps = tile // 128
    cumr = cum.reshape(-1, tile)
    lse = lse.reshape(-1, tile)
    ns = len(outgoing)
    pairs = width // HEAD_PAIR

    def body(*refs):
        q_ref, k_ref, v_ref, o_ref, do_ref, lse_ref, cum_ref = refs[:7]
        dq_ref, dk_ref, dv_ref, dcum_ref = refs[7 + ns:11 + ns]
        (q_t, do_t, k_t, do_bf, cum_col, dq_t_acc, delta, q_side, dk_acc, dv_acc,
         k_side) = refs[11 + 2 * ns:22 + 2 * ns]
        pid = pl.program_id(0)
        if ns:
            start, finish = _scatter_steps(refs[7:7 + ns], refs[11 + ns:11 + 2 * ns], *refs[22 + 2 * ns:])
            pl.when(pid == 0)(start)
        top = lax.broadcasted_iota(jnp.int32, (HEAD_PAIR, tile), 0) < HEAD_DIM
        left = lax.broadcasted_iota(jnp.int32, (tile, HEAD_PAIR), 1) < HEAD_DIM
        causal = (lax.broadcasted_iota(jnp.int32, (tile, tile), 0)
                  <= lax.broadcasted_iota(jnp.int32, (tile, tile), 1))

        def pre(i, carry):
            r0 = pl.multiple_of(i * tile, tile)
            d_o = do_ref[pl.ds(r0, tile), :]
            prod_t = (d_o * o_ref[pl.ds(r0, tile), :]).T
            delta[pl.ds(i, 1), :] = jnp.sum(prod_t[:HEAD_DIM], axis=0, keepdims=True)
            delta[pl.ds(nt + i, 1), :] = jnp.sum(prod_t[HEAD_DIM:], axis=0, keepdims=True)
            do_bf[pl.ds(r0, tile), :] = d_o.astype(BF16)
            do_t[i] = d_o.T.astype(BF16)
            q_t[i] = q_ref[pl.ds(r0, tile), :].astype(F32).T.astype(BF16)
            k_t[i] = k_ref[pl.ds(r0, tile), :].astype(F32).T.astype(BF16)
            dq_t_acc[i] = jnp.zeros((HEAD_PAIR, tile), F32)
            for h in (0, 1):
                row = cum_ref[pl.ds((2 * pid + h) * nt + i, 1), :]
                cum_col[h, pl.ds(r0, tile), :] = jnp.broadcast_to(row, (HEAD_PAIR, tile)).T
                q_side[pl.ds(h * nt + i, 1), :] = jnp.zeros((1, tile), F32)
            return carry

        lax.fori_loop(0, nt, pre, 0)

        def kv_loop(kj, carry):
            k0 = pl.multiple_of(kj * tile, tile)
            kt = k_ref[pl.ds(k0, tile), :]
            vt = v_ref[pl.ds(k0, tile), :]
            ktt = k_t[kj]
            zt = jnp.zeros_like(ktt)
            zr = jnp.zeros_like(kt)
            kms = (jnp.where(top, ktt, zt), jnp.where(top, zt, ktt))
            cols = [jnp.tile(cum_col[h, pl.ds(k0, tile), :], (1, reps)) for h in (0, 1)]
            dk_acc[...] = jnp.zeros_like(dk_acc)
            dv_acc[...] = jnp.zeros_like(dv_acc)
            k_side[...] = jnp.zeros_like(k_side)

            def step(qi, carry, masked):
                q0 = pl.multiple_of(qi * tile, tile)
                qtt = q_t[qi]
                dtt = do_t[qi]
                q_rows = q_ref[pl.ds(q0, tile), :]
                do_rows = do_bf[pl.ds(q0, tile), :]
                dq_part = None
                for h in (0, 1):
                    q_m = jnp.where(top, qtt, zt) if h == 0 else jnp.where(top, zt, qtt)
                    do_m = jnp.where(top, dtt, zt) if h == 0 else jnp.where(top, zt, dtt)
                    s = jnp.dot(kt, q_m, preferred_element_type=F32) - cols[h]
                    if masked:
                        s = jnp.where(causal, s, NEG)
                    p = jnp.exp(s - lse_ref[pl.ds((2 * pid + h) * nt + qi, 1), :])
                    dp = jnp.dot(vt, do_m, preferred_element_type=F32)
                    ds = p * (dp - delta[pl.ds(h * nt + qi, 1), :])
                    q_side[pl.ds(h * nt + qi, 1), :] += jnp.sum(ds, axis=0, keepdims=True)
                    folded = ds[:, :128]
                    for b in range(1, reps):
                        folded = folded + ds[:, b * 128:(b + 1) * 128]
                    k_side[h] += folded
                    pb = p.astype(BF16)
                    dsb = ds.astype(BF16)
                    do_h = jnp.where(left, do_rows, zr) if h == 0 else jnp.where(left, zr, do_rows)
                    q_h = jnp.where(left, q_rows, zr) if h == 0 else jnp.where(left, zr, q_rows)
                    dv_acc[...] += jnp.dot(pb, do_h, preferred_element_type=F32)
                    dk_acc[...] += jnp.dot(dsb, q_h, preferred_element_type=F32)
                    part = jnp.dot(kms[h], dsb, preferred_element_type=F32)
                    dq_part = part if dq_part is None else dq_part + part
                dq_t_acc[qi] += dq_part
                return carry

            step(kj, 0, True)
            lax.fori_loop(kj + 1, nt, lambda qi, c: step(qi, c, False), 0)
            dk_ref[pl.ds(k0, tile), :] = dk_acc[...].astype(BF16)
            dv_ref[pl.ds(k0, tile), :] = dv_acc[...].astype(BF16)
            for h in (0, 1):
                dcum_ref[pl.ds((2 * pid + h) * nt + kj, 1), :] = -jnp.sum(k_side[h].T, axis=0, keepdims=True)
            return carry

        lax.fori_loop(0, nt, kv_loop, 0)

        def fin(i, carry):
            r0 = pl.multiple_of(i * tile, tile)
            dq_ref[pl.ds(r0, tile), :] = dq_t_acc[i].T.astype(BF16)
            for h in (0, 1):
                dcum_ref[pl.ds((2 * pid + h) * nt + i, 1), :] += q_side[pl.ds(h * nt + i, 1), :]
            return carry

        lax.fori_loop(0, nt, fin, 0)
        if ns:
            pl.when(pid == pairs - 1)(finish)

    blk = pl.BlockSpec((s_len, HEAD_PAIR), lambda p: (0, p))
    full = pl.BlockSpec(cumr.shape, lambda p: (0, 0))
    transposed = pltpu.VMEM((nt, HEAD_PAIR, tile), BF16)
    sems = [pltpu.SemaphoreType.DMA((3 * ns,))] * 2 if ns else []
    dq, dk, dv, dcum, *arrived = pl.pallas_call(
        body, name="flash_bwd_scatter" if ns else "flash_bwd", grid=(pairs,),
        in_specs=[blk, blk, blk, blk, blk, full, full] + [ANY] * ns,
        out_specs=[blk, blk, blk, full] + [ANY] * ns,
        out_shape=[jax.ShapeDtypeStruct((s_len, width), BF16)] * 3 + [jax.ShapeDtypeStruct(cumr.shape, F32)]
        + [jax.ShapeDtypeStruct(t.shape, t.dtype) for t in outgoing],
        scratch_shapes=[transposed, transposed, transposed, pltpu.VMEM((s_len, HEAD_PAIR), BF16),
                        pltpu.VMEM((2, s_len, HEAD_PAIR), F32), pltpu.VMEM((nt, HEAD_PAIR, tile), F32),
                        pltpu.VMEM((2 * nt, tile), F32), pltpu.VMEM((2 * nt, tile), F32),
                        pltpu.VMEM((tile, HEAD_PAIR), F32), pltpu.VMEM((tile, HEAD_PAIR), F32),
                        pltpu.VMEM((2, tile, HEAD_PAIR), F32)] + sems,
        compiler_params=pltpu.CompilerParams(dimension_semantics=("arbitrary",)),
    )(q, k, v, o, do, lse, cumr, *outgoing)
    return (dq, dk, dv, dcum.reshape(cum.shape), *arrived)


def _out_ln(a, gate, x, w, g, b, name):
    s_len, d = x.shape
    tm = min(512, s_len)
    w, w_spec = _resident(w)

    def body(a_ref, gate_ref, x_ref, w_ref, g_ref, b_ref, xn_ref, xh_ref, rs_ref, yg_ref):
        gt = gate_ref[...]
        yg = (a_ref[...] * (gt * _sigmoid(gt))).astype(BF16)
        yg_ref[...] = yg
        z = ALPHA * x_ref[...] + jnp.dot(yg, w_ref[...], preferred_element_type=F32)
        zc = z - jnp.mean(z, axis=1, keepdims=True)
        rstd = lax.rsqrt(jnp.mean(zc * zc, axis=1, keepdims=True) + LN_EPS)
        xh = zc * rstd
        xh_ref[...] = xh
        xn_ref[...] = xh * g_ref[...] + b_ref[...]
        rs_ref[...] = jnp.broadcast_to(rstd, (tm, 128))

    row = pl.BlockSpec((tm, d), lambda i: (i, 0))
    vec = pl.BlockSpec((1, d), lambda i: (0, 0))
    return pl.pallas_call(
        body, name=name, grid=(s_len // tm,),
        in_specs=[row, row, row, w_spec, vec, vec],
        out_specs=[row, row, pl.BlockSpec((tm, 128), lambda i: (i, 0)), row],
        out_shape=[jax.ShapeDtypeStruct((s_len, d), F32), jax.ShapeDtypeStruct((s_len, d), F32),
                   jax.ShapeDtypeStruct((s_len, 128), F32), jax.ShapeDtypeStruct((s_len, d), BF16)],
        compiler_params=pltpu.CompilerParams(dimension_semantics=("parallel",)),
    )(a, gate, x, w, g, b)


def _ln_bwd(dout, xh, rs, g, w, gate, a, name, swap=None):
    s_len, d = dout.shape
    tm = min(512, s_len)
    steps = s_len // tm
    ns = 0 if swap is None else 1
    w, w_spec = _resident(w)

    def body(*refs):
        do_ref, xh_ref, rs_ref, g_ref, w_ref, gate_ref, a_ref = refs[:7]
        dz_ref, da_ref, dgate_ref, dg_ref, db_ref = refs[7 + ns:12 + ns]
        if ns:
            start, finish = _swap_steps(refs[7:8], refs[12 + ns:13 + ns], [swap[1]], *refs[13 + ns:])
            pl.when(pl.program_id(0) == 0)(start)

        @pl.when(pl.program_id(0) == 0)
        def _():
            dg_ref[...] = jnp.zeros_like(dg_ref)
            db_ref[...] = jnp.zeros_like(db_ref)

        dout_t = do_ref[...]
        xh_t = xh_ref[...]
        dg_ref[...] += jnp.sum(dout_t * xh_t, axis=0, keepdims=True)
        db_ref[...] += jnp.sum(dout_t, axis=0, keepdims=True)
        dxh = dout_t * g_ref[...]
        m1 = jnp.mean(dxh, axis=1, keepdims=True)
        m2 = jnp.mean(dxh * xh_t, axis=1, keepdims=True)
        dz = rs_ref[:, 0:1] * (dxh - m1 - xh_t * m2)
        dz_ref[...] = dz
        dyg = lax.dot_general(dz.astype(BF16), w_ref[...], NT_DIMS, preferred_element_type=F32)
        gt = gate_ref[...]
        sg = _sigmoid(gt)
        da_ref[...] = dyg * (gt * sg)
        dgate_ref[...] = (dyg * a_ref[...] * (sg * (1.0 + gt * (1.0 - sg)))).astype(BF16)
        if ns:
            pl.when(pl.program_id(0) == steps - 1)(finish)

    row = pl.BlockSpec((tm, d), lambda i: (i, 0))
    vec = pl.BlockSpec((1, d), lambda i: (0, 0))
    extra_out = [jax.ShapeDtypeStruct((N_CHIPS, swap[1][1], LANES), swap[0].dtype)] if ns else []
    return pl.pallas_call(
        body, name=name + "_swap" if ns else name, grid=(steps,),
        in_specs=[row, row, pl.BlockSpec((tm, 128), lambda i: (i, 0)), vec, w_spec, row, row] + [ANY] * ns,
        out_specs=[row, row, row, vec, vec] + [ANY] * ns,
        out_shape=[jax.ShapeDtypeStruct((s_len, d), F32), jax.ShapeDtypeStruct((s_len, d), F32),
                   jax.ShapeDtypeStruct((s_len, d), BF16), jax.ShapeDtypeStruct((1, d), F32),
                   jax.ShapeDtypeStruct((1, d), F32)] + extra_out,
        scratch_shapes=[pltpu.SemaphoreType.DMA((N_CHIPS,))] * 2 if ns else [],
        compiler_params=pltpu.CompilerParams(dimension_semantics=("arbitrary",)),
    )(dout, xh, rs, g, w, gate, a, *([swap[0]] if ns else []))


def _loss_grad(y, target):
    s_len, d = y.shape
    tm = min(512, s_len)

    def body(y_ref, t_ref, dy_ref, acc_ref):
        @pl.when(pl.program_id(0) == 0)
        def _():
            acc_ref[...] = jnp.zeros_like(acc_ref)

        diff = y_ref[...] - t_ref[...]
        dy_ref[...] = diff * (1.0 / d)
        acc_ref[...] += jnp.sum(diff * diff, axis=0, keepdims=True)

    row = pl.BlockSpec((tm, d), lambda i: (i, 0))
    return pl.pallas_call(
        body, name="loss_grad", grid=(s_len // tm,),
        in_specs=[row, row], out_specs=[row, pl.BlockSpec((1, d), lambda i: (0, 0))],
        out_shape=[jax.ShapeDtypeStruct((s_len, d), F32), jax.ShapeDtypeStruct((1, d), F32)],
        compiler_params=pltpu.CompilerParams(dimension_semantics=("arbitrary",)),
    )(y, target)


def _rnn_fwd(u, conv_w, conv_b, wa, ba, wi, bi, lam):
    s_len, width = u.shape
    tm = min(512, s_len // 2)
    nb = width // RNN_BLOCK

    def body(u_ref, up_ref, cw_ref, cb_ref, wa_ref, ba_ref, wi_ref, bi_ref, lam_ref,
             h_ref, uc_ref, r_ref, i_ref, a_ref, b_scr, h_carry):
        step_id = pl.program_id(0)

        @pl.when(step_id == 0)
        def _():
            h_carry[...] = jnp.zeros_like(h_carry)

        for n in range(nb):
            blk = slice(n * RNN_BLOCK, (n + 1) * RNN_BLOCK)
            ut = u_ref[:, blk]
            prev = jnp.where(step_id > 0, up_ref[:, blk], 0.0)
            uc = cb_ref[:, blk] + cw_ref[3:4, blk] * ut
            for k in (1, 2, 3):
                uc = uc + cw_ref[3 - k:4 - k, blk] * _shift_down(ut, prev, k)
            ucb = uc.astype(BF16)
            r = _sigmoid(jnp.dot(ucb, wa_ref[n], preferred_element_type=F32) + ba_ref[:, blk])
            ig = _sigmoid(jnp.dot(ucb, wi_ref[n], preferred_element_type=F32) + bi_ref[:, blk])
            log_a = -LRU_C * r * _softplus(-lam_ref[:, blk])
            uc_ref[:, blk] = uc
            r_ref[:, blk] = r
            i_ref[:, blk] = ig
            a_ref[:, blk] = jnp.exp(log_a)
            b_scr[:, blk] = jnp.sqrt(_neg_expm1(2.0 * log_a)) * (ig * uc)

        def scan(t, h):
            h = a_ref[pl.ds(t, 1), :] * h + b_scr[pl.ds(t, 1), :]
            h_ref[pl.ds(t, 1), :] = h
            return h

        h_carry[...] = lax.fori_loop(0, tm, scan, h_carry[...], unroll=8)

    row = pl.BlockSpec((tm, width), lambda i: (i, 0))
    prev8 = pl.BlockSpec((8, width), lambda i: (jnp.maximum(i * (tm // 8) - 1, 0), 0))
    vec = pl.BlockSpec((1, width), lambda i: (0, 0))
    mat = pl.BlockSpec(wa.shape, lambda i: (0, 0, 0))
    return pl.pallas_call(
        body, name="rnn_fwd", grid=(s_len // tm,),
        in_specs=[row, prev8, pl.BlockSpec(conv_w.shape, lambda i: (0, 0)), vec, mat, vec, mat, vec, vec],
        out_specs=[row] * 5,
        out_shape=[jax.ShapeDtypeStruct((s_len, width), F32)] * 5,
        scratch_shapes=[pltpu.VMEM((tm, width), F32), pltpu.VMEM((1, width), F32)],
        compiler_params=pltpu.CompilerParams(dimension_semantics=("arbitrary",)),
    )(u, u, conv_w, conv_b, wa, ba, wi, bi, lam)


def _rnn_bwd(dh, a, h, r, ig, uc, lam, wa, wi):
    s_len, width = dh.shape
    tm = min(512, s_len // 2)
    nt = s_len // tm
    nb = width // RNN_BLOCK

    def body(dh_ref, a_ref, h_ref, hp_ref, r_ref, i_ref, uc_ref, lam_ref, wa_ref, wi_ref,
             duc_ref, dwa_ref, dwi_ref, dba_ref, dbi_ref, dlam_ref, g_scr, c_scr, dsp_scr):
        step_id = pl.program_id(0)
        tile_id = nt - 1 - step_id

        @pl.when(step_id == 0)
        def _():
            c_scr[...] = jnp.zeros_like(c_scr)
            dsp_scr[...] = jnp.zeros_like(dsp_scr)
            dwa_ref[...] = jnp.zeros_like(dwa_ref)
            dwi_ref[...] = jnp.zeros_like(dwi_ref)
            dba_ref[...] = jnp.zeros_like(dba_ref)
            dbi_ref[...] = jnp.zeros_like(dbi_ref)

        def scan(j, c):
            t = tm - 1 - j
            g = dh_ref[pl.ds(t, 1), :] + c
            g_scr[pl.ds(t, 1), :] = g
            return a_ref[pl.ds(t, 1), :] * g

        c_scr[...] = lax.fori_loop(0, tm, scan, c_scr[...], unroll=8)

        for n in range(nb):
            blk = slice(n * RNN_BLOCK, (n + 1) * RNN_BLOCK)
            g_t = g_scr[:, blk]
            hp8 = jnp.where(tile_id > 0, hp_ref[:, blk], 0.0)
            h_prev = _shift_down(h_ref[:, blk], hp8, 1)
            av, rv, iv, ucv = a_ref[:, blk], r_ref[:, blk], i_ref[:, blk], uc_ref[:, blk]
            sp = _softplus(-lam_ref[:, blk])
            mag = jnp.sqrt(_neg_expm1(-2.0 * LRU_C * rv * sp))
            d_iu = g_t * mag
            dla = g_t * h_prev * av - g_t * (iv * ucv) * (av * av) / mag
            dsp_scr[:, blk] += jnp.sum(dla * (-LRU_C * rv), axis=0, keepdims=True)
            dra = dla * (-LRU_C * sp) * rv * (1.0 - rv)
            dia = d_iu * ucv * iv * (1.0 - iv)
            drab, diab, ucb = dra.astype(BF16), dia.astype(BF16), ucv.astype(BF16)
            duc_ref[:, blk] = (d_iu * iv
                               + lax.dot_general(drab, wa_ref[n], NT_DIMS, preferred_element_type=F32)
                               + lax.dot_general(diab, wi_ref[n], NT_DIMS, preferred_element_type=F32))
            dwa_ref[n] += lax.dot_general(ucb, drab, TN_DIMS, preferred_element_type=F32)
            dwi_ref[n] += lax.dot_general(ucb, diab, TN_DIMS, preferred_element_type=F32)
            dba_ref[:, blk] += jnp.sum(dra, axis=0, keepdims=True)
            dbi_ref[:, blk] += jnp.sum(dia, axis=0, keepdims=True)

        @pl.when(step_id == nt - 1)
        def _():
            dlam_ref[...] = -dsp_scr[...] * _sigmoid(-lam_ref[...])

    row = pl.BlockSpec((tm, width), lambda i: (nt - 1 - i, 0))
    prev8 = pl.BlockSpec((8, width), lambda i: (jnp.maximum((nt - 1 - i) * (tm // 8) - 1, 0), 0))
    vec = pl.BlockSpec((1, width), lambda i: (0, 0))
    mat = pl.BlockSpec(wa.shape, lambda i: (0, 0, 0))
    return pl.pallas_call(
        body, name="rnn_bwd", grid=(nt,),
        in_specs=[row, row, row, prev8, row, row, row, vec, mat, mat],
        out_specs=[row, mat, mat, vec, vec, vec],
        out_shape=[jax.ShapeDtypeStruct((s_len, width), F32), jax.ShapeDtypeStruct(wa.shape, F32),
                   jax.ShapeDtypeStruct(wa.shape, F32)] + [jax.ShapeDtypeStruct((1, width), F32)] * 3,
        scratch_shapes=[pltpu.VMEM((tm, width), F32), pltpu.VMEM((1, width), F32), pltpu.VMEM((1, width), F32)],
        compiler_params=pltpu.CompilerParams(dimension_semantics=("arbitrary",)),
    )(dh, a, h, h, r, ig, uc, lam, wa, wi)


def _conv_bwd(duc, u, conv_w):
    s_len, width = duc.shape
    tm = min(256, s_len)
    nt = s_len // tm

    def body(d_ref, dn_ref, u_ref, up_ref, cw_ref, du_ref, dcw_ref, dcb_ref):
        step_id = pl.program_id(0)

        @pl.when(step_id == 0)
        def _():
            dcw_ref[...] = jnp.zeros_like(dcw_ref)
            dcb_ref[...] = jnp.zeros_like(dcb_ref)

        for n in range(width // RNN_BLOCK):
            blk = slice(n * RNN_BLOCK, (n + 1) * RNN_BLOCK)
            dt = d_ref[:, blk]
            ut = u_ref[:, blk]
            nxt = jnp.where(step_id < nt - 1, dn_ref[:, blk], 0.0)
            prev = jnp.where(step_id > 0, up_ref[:, blk], 0.0)
            du = cw_ref[3:4, blk] * dt
            dcw_ref[3:4, blk] += jnp.sum(dt * ut, axis=0, keepdims=True)
            for k in (1, 2, 3):
                du = du + cw_ref[3 - k:4 - k, blk] * _shift_up(dt, nxt, k)
                dcw_ref[3 - k:4 - k, blk] += jnp.sum(dt * _shift_down(ut, prev, k), axis=0, keepdims=True)
            du_ref[:, blk] = du.astype(BF16)
            dcb_ref[:, blk] += jnp.sum(dt, axis=0, keepdims=True)

    row = pl.BlockSpec((tm, width), lambda i: (i, 0))
    prev8 = pl.BlockSpec((8, width), lambda i: (jnp.maximum(i * (tm // 8) - 1, 0), 0))
    next8 = pl.BlockSpec((8, width), lambda i: (jnp.minimum((i + 1) * (tm // 8), s_len // 8 - 1), 0))
    return pl.pallas_call(
        body, name="conv_bwd", grid=(nt,),
        in_specs=[row, next8, row, prev8, pl.BlockSpec(conv_w.shape, lambda i: (0, 0))],
        out_specs=[row, pl.BlockSpec(conv_w.shape, lambda i: (0, 0)), pl.BlockSpec((1, width), lambda i: (0, 0))],
        out_shape=[jax.ShapeDtypeStruct((s_len, width), BF16), jax.ShapeDtypeStruct(conv_w.shape, F32),
                   jax.ShapeDtypeStruct((1, width), F32)],
        compiler_params=pltpu.CompilerParams(dimension_semantics=("arbitrary",)),
    )(duc, duc, u, u, conv_w)


def _pair_sum(core, grads, theirs, first_row, out_dtype):
    n, half, _ = theirs.shape
    tb = 128 if half % 128 == 0 and first_row % 128 == 0 else half
    assert first_row % tb == 0 and half % tb == 0

    def body(c_ref, a_ref, b_ref, o_ref):
        o_ref[...] = (a_ref[...] + b_ref[...]).astype(out_dtype)

    blk = pl.BlockSpec((n, tb, LANES), lambda i, c: (0, i, 0))
    mine = pl.BlockSpec((n, tb, LANES), lambda i, c: (0, first_row // tb + c[0] * (half // tb) + i, 0))
    return pl.pallas_call(
        body, name="pair_sum",
        grid_spec=pltpu.PrefetchScalarGridSpec(
            num_scalar_prefetch=1, grid=(half // tb,), in_specs=[mine, blk], out_specs=blk),
        out_shape=jax.ShapeDtypeStruct((n, half, LANES), out_dtype),
        compiler_params=pltpu.CompilerParams(dimension_semantics=("parallel",)),
    )(core, grads, theirs)


def _sum_chips(parts):
    rows = parts.shape[1]
    tb = 128 if rows % 128 == 0 else rows

    def body(p_ref, o_ref):
        o_ref[...] = ((p_ref[0].astype(F32) + p_ref[1].astype(F32)) + p_ref[2].astype(F32)) + p_ref[3].astype(F32)

    return pl.pallas_call(
        body, name="chip_sum", grid=(rows // tb,),
        in_specs=[pl.BlockSpec((N_CHIPS, tb, LANES), lambda i: (0, i, 0))],
        out_specs=pl.BlockSpec((tb, LANES), lambda i: (i, 0)),
        out_shape=jax.ShapeDtypeStruct((rows, LANES), F32),
        compiler_params=pltpu.CompilerParams(dimension_semantics=("parallel",)),
    )(parts)


def _adamw(w, g, m, v, lead_per_block, rows_per_block):
    n, rows, cols = w.shape
    blk = pl.BlockSpec((lead_per_block, rows_per_block, cols), lambda i, j: (i, j, 0))

    def body(w_ref, g_ref, m_ref, v_ref, d_ref, nm_ref, nv_ref):
        gt = g_ref[...]
        nm = ADAM_B1 * m_ref[...] + (1.0 - ADAM_B1) * gt
        nv = ADAM_B2 * v_ref[...] + (1.0 - ADAM_B2) * (gt * gt)
        m_hat = nm / (1.0 - ADAM_B1 ** ADAM_STEP)
        v_hat = nv / (1.0 - ADAM_B2 ** ADAM_STEP)
        d_ref[...] = -ADAM_LR * (m_hat / (jnp.sqrt(v_hat) + ADAM_EPS) + ADAM_WD * w_ref[...])
        nm_ref[...] = nm
        nv_ref[...] = nv

    return pl.pallas_call(
        body, name="adamw", grid=(n // lead_per_block, rows // rows_per_block), in_specs=[blk] * 4,
        out_specs=[blk] * 3,
        out_shape=[jax.ShapeDtypeStruct(w.shape, F32)] * 3,
        compiler_params=pltpu.CompilerParams(dimension_semantics=("parallel", "parallel")),
    )(w, g, m, v)


def _place():
    x, y, c = lax.axis_index("x"), lax.axis_index("y"), lax.axis_index("c")
    return x, y, c, [(1 - x, y), (x, 1 - y), (1 - x, 1 - y)]


ANY = pl.BlockSpec(memory_space=pl.ANY)
COPY_PARTS = 4


def _remote_parts(src_of, dst_of, rows, send_sem, recv_sem, to, begin=True):
    parts = COPY_PARTS if rows % (16 * COPY_PARTS) == 0 else 1
    step = rows // parts
    for i in range(parts if begin is not False else 0):
        pltpu.make_async_remote_copy(src_ref=src_of(i * step, step), dst_ref=dst_of(i * step, step),
                                     send_sem=send_sem, recv_sem=recv_sem, device_id=to, device_id_type=MESH).start()
    if begin == "only":
        return None
    return pltpu.make_async_remote_copy(src_ref=src_of(0, rows), dst_ref=dst_of(0, rows), send_sem=send_sem,
                                        recv_sem=recv_sem, device_id=to, device_id_type=MESH)


GATHER_SEMS = 7


def _gather_steps(p_refs, o_refs, send_sems, recv_sems):
    x, y, c, chips = _place()
    me = 2 * x + y

    def half(ref, which):
        rows = ref.shape[0] // 2
        return ref.at[pl.ds(which * rows, rows)]

    def copy(k, src, dst, to):
        return pltpu.make_async_remote_copy(src_ref=src, dst_ref=dst, send_sem=send_sems.at[k],
                                            recv_sem=recv_sems.at[k], device_id=to, device_id_type=MESH)

    def first_copies():
        out = []
        for b, (p_ref, o_ref) in enumerate(zip(p_refs, o_refs)):
            for j, (cx, cy) in enumerate(chips):
                out.append(copy(GATHER_SEMS * b + j, half(p_ref, c), half(o_ref.at[me], c), (cx, cy, c)))
            out.append(copy(GATHER_SEMS * b + 6, p_ref, o_ref.at[me], (x, y, 1 - c)))
        return out

    def passed_copies():
        out = []
        for b, o_ref in enumerate(o_refs):
            for j, (cx, cy) in enumerate(chips):
                landed = half(o_ref.at[2 * cx + cy], c)
                out.append(copy(GATHER_SEMS * b + 3 + j, landed, landed, (x, y, 1 - c)))
        return out

    def start():
        for cp in first_copies():
            cp.start()

    def forward():
        for b, o_ref in enumerate(o_refs):
            for j, (cx, cy) in enumerate(chips):
                landed = half(o_ref.at[2 * cx + cy], c)
                copy(GATHER_SEMS * b + j, landed, landed, (x, y, c)).wait_recv()
        for cp in passed_copies():
            cp.start()

    def finish():
        for b, o_ref in enumerate(o_refs):
            for j, (cx, cy) in enumerate(chips):
                other = half(o_ref.at[2 * cx + cy], 1 - c)
                copy(GATHER_SEMS * b + 3 + j, other, other, (x, y, c)).wait_recv()
            copy(GATHER_SEMS * b + 6, o_ref.at[me], o_ref.at[me], (x, y, c)).wait_recv()
        for cp in first_copies() + passed_copies():
            cp.wait_send()

    return start, forward, finish


def _gather_chips(shards):
    nb = len(shards)

    def body(*refs):
        for step in _gather_steps(refs[:nb], refs[nb:2 * nb], *refs[2 * nb:]):
            step()

    return pl.pallas_call(
        body, name="gather_chips", in_specs=[ANY] * nb, out_specs=[ANY] * nb,
        out_shape=[jax.ShapeDtypeStruct((N_CHIPS,) + s.shape, s.dtype) for s in shards],
        scratch_shapes=[pltpu.SemaphoreType.DMA((GATHER_SEMS * nb,)), pltpu.SemaphoreType.DMA((GATHER_SEMS * nb,))],
    )(*shards)


def _swap_steps(g_refs, t_refs, spans, send_sems, recv_sems):
    x, y, c, _ = _place()

    def gives(begin):
        return [_remote_parts(
            lambda r0, m, g_ref=g_ref, j=j, base=first_row + (1 - c) * half: g_ref.at[j, pl.ds(base + r0, m), :],
            lambda r0, m, t_ref=t_ref, j=j: t_ref.at[j, pl.ds(r0, m), :],
            half, send_sems.at[b * N_CHIPS + j], recv_sems.at[b * N_CHIPS + j], (x, y, 1 - c), begin)
            for b, (g_ref, t_ref, (first_row, half)) in enumerate(zip(g_refs, t_refs, spans)) for j in range(N_CHIPS)]

    def start():
        gives("only")

    def finish():
        for give in gives(False):
            give.wait()

    return start, finish


def _swap_halves(bufs, spans):
    nb = len(bufs)

    def body(*refs):
        for step in _swap_steps(refs[:nb], refs[nb:2 * nb], spans, *refs[2 * nb:]):
            step()

    return pl.pallas_call(
        body, name="swap_halves", in_specs=[ANY] * nb, out_specs=[ANY] * nb,
        out_shape=[jax.ShapeDtypeStruct((b.shape[0], half, b.shape[2]), b.dtype) for b, (_, half) in zip(bufs, spans)],
        scratch_shapes=[pltpu.SemaphoreType.DMA((nb * N_CHIPS,)), pltpu.SemaphoreType.DMA((nb * N_CHIPS,))],
    )(*bufs)


def _scatter_steps(t_refs, o_refs, send_sems, recv_sems):
    x, y, c, chips = _place()
    me = 2 * x + y

    def slot(ref, chip):
        return lambda r0, n: ref.at[chip, pl.ds(r0, n), :]

    def sends(begin):
        return [_remote_parts(slot(t_ref, 2 * cx + cy), slot(o_ref, me), t_ref.shape[1], send_sems.at[3 * b + j],
                              recv_sems.at[3 * b + j], (cx, cy, c), begin)
                for b, (t_ref, o_ref) in enumerate(zip(t_refs, o_refs)) for j, (cx, cy) in enumerate(chips)]

    def start():
        sends("only")

    def finish():
        for b, o_ref in enumerate(o_refs):
            for j, (cx, cy) in enumerate(chips):
                landed = o_ref.at[2 * cx + cy]
                pltpu.make_async_remote_copy(src_ref=landed, dst_ref=landed, send_sem=send_sems.at[3 * b + j],
                                             recv_sem=recv_sems.at[3 * b + j], device_id=(x, y, c),
                                             device_id_type=MESH).wait_recv()
        for cp in sends(False):
            cp.wait_send()

    return start, finish


def _give_sibling(bufs):
    nb = len(bufs)

    def body(*refs):
        h_refs, o_refs, (send_sems, recv_sems) = refs[:nb], refs[nb:2 * nb], refs[2 * nb:]
        x, y, c, _ = _place()
        gives = [_remote_parts(lambda r0, n, h_ref=h_ref: h_ref.at[pl.ds(r0, n), :],
                               lambda r0, n, o_ref=o_ref: o_ref.at[pl.ds(r0, n), :],
                               h_ref.shape[0], send_sems.at[b], recv_sems.at[b], (x, y, 1 - c))
                 for b, (h_ref, o_ref) in enumerate(zip(h_refs, o_refs))]
        for give in gives:
            give.wait()

    return pl.pallas_call(
        body, name="give_sibling", in_specs=[ANY] * nb, out_specs=[ANY] * nb,
        out_shape=[jax.ShapeDtypeStruct(b.shape, b.dtype) for b in bufs],
        scratch_shapes=[pltpu.SemaphoreType.DMA((nb,)), pltpu.SemaphoreType.DMA((nb,))],
    )(*bufs)


def _pack(arrays, dtype, row_align):
    flat = []
    for arr in arrays:
        v = arr.reshape(-1)
        flat.append(jnp.pad(v, (0, (-v.shape[0]) % LANES)))
    total = sum(f.shape[0] for f in flat) // LANES
    pad_rows = (-total) % row_align
    if pad_rows:
        flat.append(jnp.zeros((pad_rows * LANES,), dtype))
    return jnp.concatenate(flat).reshape(-1, LANES)


def _unpack(buf, shapes, rows_align=1):
    lead = buf.shape[:-2]
    flat = buf.reshape(lead + (-1,))
    out, off = [], 0
    for shape in shapes:
        size = 1
        for dim in shape:
            size *= dim
        out.append(flat[..., off:off + size].reshape(lead + tuple(shape)))
        off += size + (-size) % (LANES * rows_align)
    return out


def _from_chips(parts, axis):
    return jnp.concatenate([parts[j] for j in range(N_CHIPS)], axis=axis)


def kernel(x, ln_g, ln_b, attn_w_in, attn_b_f, attn_w_out, rnn_w_in, rnn_conv_w, rnn_conv_b, rnn_w_a, rnn_b_a, rnn_w_i, rnn_b_i, rnn_lambda, rnn_w_out, loss_target, m_ln_g, m_ln_b, m_attn_w_in, m_attn_b_f, m_attn_w_out, m_rnn_w_in, m_rnn_conv_w, m_rnn_conv_b, m_rnn_w_a, m_rnn_b_a, m_rnn_w_i, m_rnn_b_i, m_rnn_lambda, m_rnn_w_out, v_ln_g, v_ln_b, v_attn_w_in, v_attn_b_f, v_attn_w_out, v_rnn_w_in, v_rnn_conv_w, v_rnn_conv_b, v_rnn_w_a, v_rnn_b_a, v_rnn_w_i, v_rnn_b_i, v_rnn_lambda, v_rnn_w_out):
    s_len, d = x.shape[1], x.shape[2]
    xs = x.reshape(s_len, d)
    target = loss_target.reshape(s_len, d)
    heads = attn_b_f.shape[1]
    qkvg = attn_w_in.shape[2] * N_CHIPS - heads

    me = 2 * lax.axis_index("x") + lax.axis_index("y")
    core = lax.axis_index("c").astype(jnp.int32)
    small = [rnn_conv_w, rnn_conv_b, rnn_b_a, rnn_b_i, rnn_lambda]
    a_in, a_out = attn_w_in.astype(BF16), attn_w_out.astype(BF16)
    later = [a_in[1], a_out] + [w.astype(BF16) for w in (rnn_w_in, rnn_w_a, rnn_w_i, rnn_w_out)]
    got_in, got_small = _gather_chips([a_in[0], _pack(small, F32, 16)])
    conv_w, conv_b, b_a, b_i, lam = [
        _from_chips(p, ax) for p, ax in zip(_unpack(got_small, [w.shape for w in small]), (2, 1, 1, 1, 1))]

    def attn_in_weights(w_in):
        shard = w_in.shape[2]

        def columns(first, last):
            return [w_in[j][:, max(first - j * shard, 0):min(last - j * shard, shard)]
                    for j in range(N_CHIPS) if first < (j + 1) * shard and last > j * shard]

        return jnp.concatenate([t * (HEAD_DIM ** -0.5) for t in columns(0, d)] + columns(d, qkvg + heads)
                               + [jnp.zeros((d, 128 - heads), BF16)], axis=1)

    w_cat = [attn_in_weights(got_in), None]
    b_f = jnp.pad(attn_b_f, ((0, 0), (0, 128 - heads)))

    saved = []
    cur = xs
    for layer in range(DEPTH):
        idx = layer // 2
        g_l, b_l = ln_g[layer:layer + 1], ln_b[layer:layer + 1]
        if layer % 2 == 0:
            q, k, v, gate, fl = _proj(cur, w_cat[idx], [(0, d, BF16), (d, d, BF16), (2 * d, d, BF16),
                                                         (3 * d, d, F32), (4 * d, 128, F32)], "attn_proj")
            cum, sneg = _fcum(fl, b_f[idx:idx + 1])
            o, lse, *rest = _flash_fwd(q, k, v, cum, later if layer == 0 else ())
            if layer == 0:
                w_cat[1] = attn_in_weights(rest[0])
                w_out_a = jnp.moveaxis(rest[1], 0, 1).reshape(2, d, d)
                w_in_r = jnp.moveaxis(rest[2], 0, 2).reshape(2, d, 2 * d)
                w_a = jnp.transpose(rest[3], (1, 2, 0, 3, 4)).reshape(2, -1, RNN_BLOCK, RNN_BLOCK)
                w_i = jnp.transpose(rest[4], (1, 2, 0, 3, 4)).reshape(2, -1, RNN_BLOCK, RNN_BLOCK)
                w_out_r = jnp.moveaxis(rest[5], 0, 1).reshape(2, d, d)
            nxt, xh, rs, yg = _out_ln(o, gate, cur, (w_out_a, idx), g_l, b_l, "out_ln")
            saved.append(dict(x=cur, q=q, k=k, v=v, gate=gate, cum=cum, sneg=sneg, a=o, lse=lse, xh=xh, rs=rs, yg=yg))
        else:
            u, gate = _proj(cur, (w_in_r, idx), [(0, d, F32), (d, d, F32)], "rnn_proj")
            vecs = [t[idx:idx + 1] for t in (conv_b, b_a, b_i, lam)]
            hseq, uc, r, ig, a = _rnn_fwd(u, conv_w[idx], vecs[0], w_a[idx], vecs[1], w_i[idx], vecs[2], vecs[3])
            nxt, xh, rs, yg = _out_ln(hseq, gate, cur, (w_out_r, idx), g_l, b_l, "out_ln")
            saved.append(dict(x=cur, u=u, gate=gate, uc=uc, r=r, ig=ig, av=a, a=hseq, xh=xh, rs=rs, yg=yg, lam=vecs[3]))
        cur = nxt

    dout, sq = _loss_grad(cur, target)
    loss_here = 0.5 * jnp.sum(sq) / d

    g_ln_g, g_ln_b = [None] * DEPTH, [None] * DEPTH
    g_attn = [None] * 2
    g_rnn = [None] * 2
    slots = None
    core1 = core.reshape(1)
    late_half = (SLOT_ROWS - LATE_ROWS) // 2

    def by_chip(t, axis):
        shape = t.shape[:axis] + (N_CHIPS, t.shape[axis] // N_CHIPS) + t.shape[axis + 1:]
        flat = jnp.moveaxis(t.reshape(shape), axis, 0).reshape(N_CHIPS, -1)
        return jnp.pad(flat, ((0, 0), (0, (-flat.shape[1]) % (8 * LANES)))).reshape(N_CHIPS, -1, LANES)

    def both(key):
        return jnp.stack([it[key] for it in g_rnn])

    for layer in reversed(range(DEPTH)):
        idx = layer // 2
        sv = saved[layer]
        swap = None
        if layer == 0:
            gates = jnp.concatenate([by_chip(both("w_a"), 2), by_chip(both("w_i"), 2)], axis=1)
            slots = lax.dynamic_update_slice(slots, gates, (0, ROWS_GATES, 0))
            swap = (slots, (LATE_ROWS, late_half))
        w_out = (w_out_a if layer % 2 == 0 else w_out_r, idx)
        dz, da, dgate, dg, db, *theirs_late = _ln_bwd(dout, sv["xh"], sv["rs"], ln_g[layer:layer + 1], w_out,
                                                      sv["gate"], sv["a"], "ln_bwd", swap)
        if layer == 0:
            pair_late = _pair_sum(core1, slots, theirs_late[0], LATE_ROWS, BF16)
        g_ln_g[layer], g_ln_b[layer] = dg, db
        slots = _dw_out(slots, sv["yg"], dz, (ROWS_ATTN_OUT if layer % 2 == 0 else ROWS_RNN_OUT)[idx])
        if layer % 2 == 0:
            dq, dk, dv, dcum, *arrived = _flash_bwd(sv["q"], sv["k"], sv["v"], sv["a"], da, sv["lse"], sv["cum"],
                                                    [pair_late] if layer == 0 else ())
            dlogit, dbf = _fbwd(dcum, sv["sneg"])
            pieces = [dq, dk, dv, dgate, dlogit]
            if layer > 0:
                dout, = _mm_nt(dz, [(p, j * d) for j, p in enumerate(pieces)], w_cat[idx], "attn_dx")
            slots, d_w_f = _dw_attn_in(slots, sv["x"], pieces[:4], dlogit, idx)
            g_attn[idx] = dict(w_f=d_w_f[:, :heads], b_f=dbf[:, 0])
        else:
            duc, d_wa, d_wi, d_ba, d_bi, d_lam = _rnn_bwd(da, sv["av"], sv["a"], sv["r"], sv["ig"], sv["uc"], sv["lam"],
                                                          w_a[idx], w_i[idx])
            du, d_cw, d_cb = _conv_bwd(duc, sv["u"], conv_w[idx])
            dout, = _mm_nt(dz, [(du, 0), (dgate, d)], (w_in_r, idx), "rnn_dx")
            slots = _dw_rnn_in(slots, sv["x"], (du, dgate), idx)
            g_rnn[idx] = dict(conv_w=d_cw, conv_b=d_cb[0], w_a=d_wa, b_a=d_ba[0], w_i=d_wi, b_i=d_bi[0], lam=d_lam[0])

    def everywhere(t):
        flat = t.reshape(-1)
        flat = jnp.pad(flat, (0, (-flat.shape[0]) % (8 * LANES))).reshape(-1, LANES)
        return jnp.broadcast_to(flat[None], (N_CHIPS,) + flat.shape)

    in_rows = jnp.stack([slots[1:, r:r + d, :heads] for r in ROWS_ATTN_IN], axis=1)
    edges = jnp.concatenate([in_rows, jnp.stack([it["w_f"] for it in g_attn])[None]])
    rows = [everywhere(edges), by_chip(both("conv_w"), 2),
            by_chip(both("conv_b"), 1), by_chip(both("b_a"), 1), by_chip(both("b_i"), 1), by_chip(both("lam"), 1),
            everywhere(jnp.concatenate(g_ln_g)), everywhere(jnp.concatenate(g_ln_b)),
            everywhere(jnp.stack([it["b_f"] for it in g_attn])), everywhere(loss_here)]
    used = sum(r.shape[1] for r in rows)
    small = jnp.concatenate(rows + [jnp.zeros((N_CHIPS, (-used) % 32, LANES), F32)], axis=1)

    spans = [(0, LATE_ROWS // 2), (0, small.shape[1] // 2)]
    theirs = _swap_halves([slots, small], spans)
    pair = [_pair_sum(core1, slots, theirs[0], 0, BF16), _pair_sum(core1, small, theirs[1], 0, F32)]
    dout, *arrived_last = _mm_nt(dz, [(p, j * d) for j, p in enumerate(pieces)], w_cat[0], "attn_dx", pair)
    grad_x = dout.reshape(x.shape)
    summed = []
    for mine_all, got in zip([pair_late] + pair, list(arrived) + arrived_last):
        own = lax.dynamic_index_in_dim(mine_all, me, 0, keepdims=False)
        summed.append(_sum_chips(lax.dynamic_update_index_in_dim(got, own, me, 0)))
    late, early, small_sum = [
        jnp.concatenate([jnp.where(core == 0, mine, other), jnp.where(core == 0, other, mine)])
        for mine, other in zip(summed, _give_sibling(summed))]

    def rows_at(first, n):
        return early[first:first + n] if first < LATE_ROWS else late[first - LATE_ROWS:first - LATE_ROWS + n]

    g_in_group = jnp.stack([rows_at(r, d) for r in ROWS_ATTN_IN])
    g_w_out_a = jnp.stack([rows_at(r, d // N_CHIPS) for r in ROWS_ATTN_OUT])
    g_w_out_r = jnp.stack([rows_at(r, d // N_CHIPS) for r in ROWS_RNN_OUT])
    folded = jnp.stack([rows_at(r, d // 2) for r in ROWS_RNN_IN])
    g_w_in_r = jnp.concatenate([folded[:, :, :d // 2], folded[:, :, d // 2:]], axis=1)
    gate_rows = rnn_w_a.size // LANES
    g_w_a = rows_at(ROWS_GATES, gate_rows).reshape(rnn_w_a.shape)
    g_w_i = rows_at(ROWS_GATES + gate_rows, gate_rows).reshape(rnn_w_i.shape)
    (g_edges, g_conv_w, g_conv_b, g_b_a, g_b_i, g_lam, g_ln_g_sum, g_ln_b_sum, g_b_f,
     loss) = _unpack(small_sum, [
        (N_CHIPS, 2, d, heads), rnn_conv_w.shape, rnn_conv_b.shape, rnn_b_a.shape,
        rnn_b_i.shape, rnn_lambda.shape, ln_g.shape, ln_b.shape, attn_b_f.shape, ()], rows_align=8)
    wide = jnp.concatenate([g_in_group, lax.dynamic_index_in_dim(g_edges, me, 0, keepdims=False)], axis=2)
    g_w_in_a = lax.dynamic_slice(wide, (0, 0, (heads // N_CHIPS) * me), attn_w_in.shape)

    def adam_nd(w, g, m, v, view, rows_per_block):
        outs = _adamw(w.reshape(view), g.reshape(view), m.reshape(view), v.reshape(view), 1, rows_per_block)
        return [t.reshape(w.shape) for t in outs]

    turned = [jnp.transpose(t, (2, 0, 1)) for t in (attn_w_in, g_w_in_a, m_attn_w_in, v_attn_w_in)]
    upd = {
        "attn_w_in": [jnp.transpose(t, (1, 2, 0)) for t in _adamw(*turned, attn_w_in.shape[2] // N_CHIPS, 2)],
        "attn_w_out": adam_nd(attn_w_out, g_w_out_a, m_attn_w_out, v_attn_w_out, attn_w_out.shape, 256),
        "rnn_w_in": adam_nd(rnn_w_in, g_w_in_r, m_rnn_w_in, v_rnn_w_in, rnn_w_in.shape, 512),
        "rnn_w_a": adam_nd(rnn_w_a, g_w_a, m_rnn_w_a, v_rnn_w_a, (1, -1, RNN_BLOCK), 512),
        "rnn_w_i": adam_nd(rnn_w_i, g_w_i, m_rnn_w_i, v_rnn_w_i, (1, -1, RNN_BLOCK), 512),
        "rnn_w_out": adam_nd(rnn_w_out, g_w_out_r, m_rnn_w_out, v_rnn_w_out, rnn_w_out.shape, 256),
    }
    smalls = [rnn_conv_w, rnn_conv_b, rnn_b_a, rnn_b_i, rnn_lambda, ln_g, ln_b, attn_b_f]
    g_small = [g_conv_w, g_conv_b, g_b_a, g_b_i, g_lam, g_ln_g_sum, g_ln_b_sum, g_b_f]
    m_small = [m_rnn_conv_w, m_rnn_conv_b, m_rnn_b_a, m_rnn_b_i, m_rnn_lambda, m_ln_g, m_ln_b, m_attn_b_f]
    v_small = [v_rnn_conv_w, v_rnn_conv_b, v_rnn_b_a, v_rnn_b_i, v_rnn_lambda, v_ln_g, v_ln_b, v_attn_b_f]
    packs = [_pack(t, F32, 16)[None] for t in (smalls, g_small, m_small, v_small)]
    small_out = [_unpack(t[0], [w.shape for w in smalls]) for t in _adamw(*packs, 1, 16)]
    for k, name in enumerate(("rnn_conv_w", "rnn_conv_b", "rnn_b_a", "rnn_b_i", "rnn_lambda", "ln_g", "ln_b",
                              "attn_b_f")):
        upd[name] = [small_out[0][k], small_out[1][k], small_out[2][k]]
    grad = {"attn_w_in": g_w_in_a, "attn_w_out": g_w_out_a, "rnn_w_in": g_w_in_r, "rnn_w_a": g_w_a, "rnn_w_i": g_w_i,
            "rnn_w_out": g_w_out_r, "rnn_conv_w": g_conv_w, "rnn_conv_b": g_conv_b, "rnn_b_a": g_b_a,
            "rnn_b_i": g_b_i, "rnn_lambda": g_lam, "ln_g": g_ln_g_sum, "ln_b": g_ln_b_sum, "attn_b_f": g_b_f}
    names = ("ln_g", "ln_b", "attn_w_in", "attn_b_f", "attn_w_out", "rnn_w_in", "rnn_conv_w", "rnn_conv_b",
             "rnn_w_a", "rnn_b_a", "rnn_w_i", "rnn_b_i", "rnn_lambda", "rnn_w_out")
    return (loss, grad_x, *[grad[n] for n in names], *[upd[n][0] for n in names], *[upd[n][1] for n in names],
            *[upd[n][2] for n in names])
```

```python
import jax
import jax.numpy as jnp
from jax import lax
from jax.experimental import pallas as pl
from jax.experimental.pallas import tpu as pltpu

F32 = jnp.float32
BF16 = jnp.bfloat16
MESH = pl.DeviceIdType.MESH

DEPTH = 4
HEAD_DIM = 64
HEAD_PAIR = 2 * HEAD_DIM
RNN_BLOCK = 256
CONV_WIDTH = 4
LRU_C = 8.0
ALPHA = (2.0 * DEPTH) ** 0.25
LN_EPS = 1e-5
ADAM_LR, ADAM_B1, ADAM_B2, ADAM_EPS, ADAM_WD, ADAM_STEP = 0.001, 0.9, 0.999, 1e-08, 0.01, 10
N_CHIPS = 4
LANES = 1024
NEG = -1e30

NT_DIMS = (((1,), (1,)), ((), ()))
TN_DIMS = (((0,), (0,)), ((), ()))


def _sigmoid(z):
    return 1.0 / (1.0 + jnp.exp(-z))


def _softplus(z):
    return jnp.maximum(z, 0.0) + jnp.log(1.0 + jnp.exp(-jnp.abs(z)))


def _neg_expm1(y):
    poly = y * (1.0 + y * (0.5 + y * (1.0 / 6.0 + y * (1.0 / 24.0))))
    return -jnp.where(y > -0.05, poly, jnp.exp(y) - 1.0)


def _shift_down(cur, prev8, k):
    n = cur.shape[0]
    row = lax.broadcasted_iota(jnp.int32, cur.shape, 0)
    fix = jnp.tile(pltpu.roll(prev8, k, 0), (n // 8, 1))
    return jnp.where(row < k, fix, pltpu.roll(cur, k, 0))


def _shift_up(cur, next8, k):
    n = cur.shape[0]
    row = lax.broadcasted_iota(jnp.int32, cur.shape, 0)
    fix = jnp.tile(pltpu.roll(next8, 8 - k, 0), (n // 8, 1))
    return jnp.where(row >= n - k, fix, pltpu.roll(cur, n - k, 0))


def _proj(x, w, outs, name):
    s_len, d = x.shape
    tm = min(512, s_len)

    def body(x_ref, w_ref, *o_refs):
        xb = x_ref[...].astype(BF16)
        for (c0, wd, dt), o_ref in zip(outs, o_refs):
            step = min(wd, 512)
            for cc in range(0, wd, step):
                o_ref[:, cc:cc + step] = jnp.dot(
                    xb, w_ref[:, c0 + cc:c0 + cc + step], preferred_element_type=F32).astype(dt)

    return pl.pallas_call(
        body, name=name, grid=(s_len // tm,),
        in_specs=[pl.BlockSpec((tm, d), lambda i: (i, 0)), pl.BlockSpec(w.shape, lambda i: (0, 0))],
        out_specs=[pl.BlockSpec((tm, wd), lambda i: (i, 0)) for _, wd, _ in outs],
        out_shape=[jax.ShapeDtypeStruct((s_len, wd), dt) for _, wd, dt in outs],
        compiler_params=pltpu.CompilerParams(dimension_semantics=("parallel",)),
    )(x, w)


def _mm_nt(dz, pieces, w, name, outgoing=()):
    s_len, d = dz.shape
    tm = min(256, s_len)
    steps = s_len // tm
    n = len(pieces)
    ns = len(outgoing)
    cols = [(c0, p.shape[1]) for p, c0 in pieces]

    def body(*refs):
        dz_ref, p_refs, w_ref, o_ref = refs[0], refs[1:1 + n], refs[1 + n], refs[2 + n + ns]
        if ns:
            start, finish = _scatter_steps(refs[2 + n:2 + n + ns], refs[3 + n + ns:3 + n + 2 * ns],
                                           *refs[3 + n + 2 * ns:])
            pl.when(pl.program_id(0) == 0)(start)
        acc = ALPHA * dz_ref[...]
        for (c0, wd), p_ref in zip(cols, p_refs):
            acc = acc + lax.dot_general(p_ref[...], w_ref[:, c0:c0 + wd], NT_DIMS, preferred_element_type=F32)
        o_ref[...] = acc
        if ns:
            pl.when(pl.program_id(0) == steps - 1)(finish)

    out, *arrived = pl.pallas_call(
        body, name=name + "_scatter" if ns else name, grid=(steps,),
        in_specs=[pl.BlockSpec((tm, d), lambda i: (i, 0))]
        + [pl.BlockSpec((tm, wd), lambda i: (i, 0)) for _, wd in cols]
        + [pl.BlockSpec(w.shape, lambda i: (0, 0))] + [ANY] * ns,
        out_specs=[pl.BlockSpec((tm, d), lambda i: (i, 0))] + [ANY] * ns,
        out_shape=[jax.ShapeDtypeStruct((s_len, d), F32)] + [jax.ShapeDtypeStruct(t.shape, t.dtype) for t in outgoing],
        scratch_shapes=[pltpu.SemaphoreType.DMA((3 * ns,))] * 2 if ns else [],
        compiler_params=pltpu.CompilerParams(dimension_semantics=("arbitrary" if ns else "parallel",)),
    )(dz, *[p for p, _ in pieces], w, *outgoing)
    return (out, *arrived)


ROWS_ATTN_IN = (0, 2048)
ROWS_ATTN_OUT = (1024, 1280)
ROWS_RNN_OUT = (1536, 1792)
ROWS_RNN_IN = (3072, 3584)
ROWS_GATES = 4096
LATE_ROWS = 1280
SLOT_ROWS = 4352


DW_ROWS = 1024


def _dw_call(body, name, slots, operands, specs, out_block, out_index, grid, semantics):
    return pl.pallas_call(
        body, name=name, grid=grid,
        in_specs=specs if slots is None else [ANY] + specs,
        out_specs=pl.BlockSpec(out_block, out_index),
        out_shape=jax.ShapeDtypeStruct((N_CHIPS, SLOT_ROWS, LANES), F32),
        input_output_aliases={} if slots is None else {0: 0},
        compiler_params=pltpu.CompilerParams(dimension_semantics=semantics),
    )(*(operands if slots is None else [slots] + operands))


def _dw_attn_in(slots, x, pieces, forget, layer):
    s_len, d = x.shape
    ts = min(s_len, DW_ROWS)
    steps = s_len // ts
    half = d // 2

    def body(g_ref, x_ref, p0, p1, p2, p3, f_ref, o_ref, wf_ref):
        lanes, step = pl.program_id(0), pl.program_id(1)

        @pl.when(step == 0)
        def _():
            o_ref[...] = jnp.zeros_like(o_ref)

        xb = x_ref[...].astype(BF16)
        for j, p_ref in enumerate((p0, p1, p2, p3)):
            o_ref[j] += lax.dot_general(xb, p_ref[...], TN_DIMS, preferred_element_type=F32)

        @pl.when(step == steps - 1)
        def _():
            o_ref[0] = o_ref[0] * (HEAD_DIM ** -0.5)

        @pl.when(lanes == 0)
        def _():
            @pl.when(step == 0)
            def _():
                wf_ref[...] = jnp.zeros_like(wf_ref)

            wf_ref[...] += lax.dot_general(xb, f_ref[...], TN_DIMS, preferred_element_type=F32)

    return pl.pallas_call(
        body, name="dw_attn_in", grid=(2, steps),
        in_specs=[ANY, pl.BlockSpec((ts, d), lambda i, s: (s, 0))] + [pl.BlockSpec((ts, half), lambda i, s: (s, i))] * 4
        + [pl.BlockSpec((ts, 128), lambda i, s: (s, 0))],
        out_specs=[pl.BlockSpec((N_CHIPS, d, half), lambda i, s: (0, ROWS_ATTN_IN[layer] // d, i)),
                   pl.BlockSpec((d, 128), lambda i, s: (0, 0))],
        out_shape=[jax.ShapeDtypeStruct((N_CHIPS, SLOT_ROWS, LANES), F32), jax.ShapeDtypeStruct((d, 128), F32)],
        input_output_aliases={0: 0},
        compiler_params=pltpu.CompilerParams(dimension_semantics=("arbitrary", "arbitrary")),
    )(slots, x, *pieces, forget)


def _dw_out(slots, yg, dz, first_row):
    s_len, d = dz.shape
    ts = min(s_len, DW_ROWS)
    rows = d // N_CHIPS

    def body(*refs):
        a_ref, b_ref, o_ref = refs[-3:]

        @pl.when(pl.program_id(0) == 0)
        def _():
            o_ref[...] = jnp.zeros_like(o_ref)

        prod = lax.dot_general(a_ref[...], b_ref[...].astype(BF16), TN_DIMS, preferred_element_type=F32)
        o_ref[...] += prod.reshape(N_CHIPS, rows, d)

    specs = [pl.BlockSpec((ts, d), lambda s: (s, 0))] * 2
    return _dw_call(body, "dw_out", slots, [yg, dz], specs, (N_CHIPS, rows, d), lambda s: (0, first_row // rows, 0),
                    (s_len // ts,), ("arbitrary",))


def _dw_rnn_in(slots, x, parts, layer):
    s_len, d = x.shape
    ts = min(s_len, DW_ROWS)
    half = d // 2

    def body(*refs):
        x_ref, p_refs, o_ref = refs[-4], refs[-3:-1], refs[-1]

        @pl.when(pl.program_id(0) == 0)
        def _():
            o_ref[...] = jnp.zeros_like(o_ref)

        xb = x_ref[...].astype(BF16)
        for p, p_ref in enumerate(p_refs):
            for jj in (0, 1):
                for r in (0, 1):
                    o_ref[2 * p + jj, :, r * half:(r + 1) * half] += lax.dot_general(
                        xb[:, r * half:(r + 1) * half], p_ref[:, jj * half:(jj + 1) * half], TN_DIMS,
                        preferred_element_type=F32)

    specs = [pl.BlockSpec((ts, d), lambda s: (s, 0))] * 3
    return _dw_call(body, "dw_rnn_in", slots, [x] + list(parts), specs, (N_CHIPS, half, d),
                    lambda s: (0, ROWS_RNN_IN[layer] // half, 0), (s_len // ts,), ("arbitrary",))


def _fcum(fl, bias):
    s_len = fl.shape[0]

    def body(fl_ref, b_ref, cum_ref, sg_ref):
        z = fl_ref[...] + b_ref[...]
        e = jnp.exp(-jnp.abs(z))
        logf = jnp.minimum(z, 0.0) - jnp.log(1.0 + e)
        sneg = jnp.where(z >= 0, e, 1.0) / (1.0 + e)
        run = logf.T[0:16, :]
        sg_ref[...] = sneg.T[0:16, :]
        lane = lax.broadcasted_iota(jnp.int32, (16, s_len), 1)
        sh = 1
        while sh < s_len:
            run = run + jnp.where(lane >= sh, pltpu.roll(run, sh, 1), 0.0)
            sh *= 2
        cum_ref[...] = run

    return pl.pallas_call(
        body, name="fcum",
        out_shape=[jax.ShapeDtypeStruct((16, s_len), F32), jax.ShapeDtypeStruct((16, s_len), F32)],
    )(fl, bias)


def _fbwd(dcum, sneg):
    s_len = dcum.shape[1]

    def body(dc_ref, sg_ref, dl_ref, db_ref):
        run = dc_ref[...]
        lane = lax.broadcasted_iota(jnp.int32, (16, s_len), 1)
        sh = 1
        while sh < s_len:
            run = run + jnp.where(lane < s_len - sh, pltpu.roll(run, s_len - sh, 1), 0.0)
            sh *= 2
        dlog = run * sg_ref[...]
        db_ref[...] = jnp.broadcast_to(jnp.sum(dlog, axis=1, keepdims=True), (16, 128))
        full = jnp.concatenate([dlog, jnp.zeros((112, s_len), F32)], axis=0)
        dl_ref[...] = full.T.astype(BF16)

    return pl.pallas_call(
        body, name="fbwd",
        out_shape=[jax.ShapeDtypeStruct((s_len, 128), BF16), jax.ShapeDtypeStruct((16, 128), F32)],
    )(dcum, sneg)


FWD_TILE = 1024
BWD_TILE = 512


def _flash_fwd(q, k, v, cum, shards=()):
    s_len, width = q.shape
    tile = min(FWD_TILE, s_len // 2)
    nt = s_len // tile
    reps = tile // 128
    cumr = cum.reshape(-1, tile)
    ns = len(shards)
    pairs = width // HEAD_PAIR

    def body(*refs):
        q_ref, k_ref, v_ref, cum_ref = refs[:4]
        o_ref, lse_ref = refs[4 + ns:6 + ns]
        q_t, v_t, cum_col = refs[6 + 2 * ns:9 + 2 * ns]
        pid = pl.program_id(0)
        if ns:
            start, forward, finish = _gather_steps(refs[4:4 + ns], refs[6 + ns:6 + 2 * ns], *refs[9 + 2 * ns:])
            pl.when(pid == 0)(start)
            pl.when(pid == pairs - 3)(forward)
        top = lax.broadcasted_iota(jnp.int32, (HEAD_PAIR, tile), 0) < HEAD_DIM
        causal = (lax.broadcasted_iota(jnp.int32, (tile, tile), 0)
                  <= lax.broadcasted_iota(jnp.int32, (tile, tile), 1))

        def pre(i, carry):
            r0 = pl.multiple_of(i * tile, tile)
            q_t[i] = q_ref[pl.ds(r0, tile), :].astype(F32).T.astype(BF16)
            v_t[i] = v_ref[pl.ds(r0, tile), :].astype(F32).T.astype(BF16)
            for h in (0, 1):
                row = cum_ref[pl.ds((2 * pid + h) * nt + i, 1), :]
                cum_col[h, pl.ds(r0, tile), :] = jnp.broadcast_to(row, (HEAD_PAIR, tile)).T
            return carry

        lax.fori_loop(0, nt, pre, 0)

        def q_loop(qi, carry):
            qt = q_t[qi]
            zt = jnp.zeros_like(qt)
            qms = (jnp.where(top, qt, zt), jnp.where(top, zt, qt))

            def step(kj, state, masked):
                m0, l0, m1, l1, acc = state
                k0 = pl.multiple_of(kj * tile, tile)
                kt = k_ref[pl.ds(k0, tile), :]
                vt = v_t[kj]
                ms, ls, scales, contrib = [m0, m1], [l0, l1], [], None
                for h in (0, 1):
                    s = jnp.dot(kt, qms[h], preferred_element_type=F32)
                    s = s - jnp.tile(cum_col[h, pl.ds(k0, tile), :], (1, reps))
                    if masked:
                        s = jnp.where(causal, s, NEG)
                    m_new = jnp.maximum(ms[h], jnp.max(s, axis=0, keepdims=True))
                    p = jnp.exp(s - m_new)
                    scale = jnp.exp(ms[h] - m_new)
                    ls[h] = scale * ls[h] + jnp.sum(p, axis=0, keepdims=True)
                    ms[h] = m_new
                    scales.append(scale)
                    vm = jnp.where(top, vt, zt) if h == 0 else jnp.where(top, zt, vt)
                    part = jnp.dot(vm, p.astype(BF16), preferred_element_type=F32)
                    contrib = part if contrib is None else contrib + part
                acc = jnp.where(top, scales[0], scales[1]) * acc + contrib
                return ms[0], ls[0], ms[1], ls[1], acc

            low = jnp.full((1, tile), NEG, F32)
            zero = jnp.zeros((1, tile), F32)
            state = step(qi, (low, zero, low, zero, jnp.zeros((HEAD_PAIR, tile), F32)), True)
            m0, l0, m1, l1, acc = lax.fori_loop(0, qi, lambda kj, c: step(kj, c, False), state)
            q0 = pl.multiple_of(qi * tile, tile)
            o_ref[pl.ds(q0, tile), :] = (acc / jnp.where(top, l0, l1)).T
            lse_ref[pl.ds((2 * pid) * nt + qi, 1), :] = m0 + jnp.log(l0)
            lse_ref[pl.ds((2 * pid + 1) * nt + qi, 1), :] = m1 + jnp.log(l1)
            return carry

        lax.fori_loop(0, nt, q_loop, 0)
        if ns:
            pl.when(pid == pairs - 1)(finish)

    blk = pl.BlockSpec((s_len, HEAD_PAIR), lambda p: (0, p))
    full = pl.BlockSpec(cumr.shape, lambda p: (0, 0))
    sems = [pltpu.SemaphoreType.DMA((GATHER_SEMS * ns,))] * 2 if ns else []
    o, lse, *gathered = pl.pallas_call(
        body, name="flash_fwd_gather" if ns else "flash_fwd", grid=(pairs,),
        in_specs=[blk, blk, blk, full] + [ANY] * ns,
        out_specs=[blk, full] + [ANY] * ns,
        out_shape=[jax.ShapeDtypeStruct((s_len, width), F32), jax.ShapeDtypeStruct(cumr.shape, F32)]
        + [jax.ShapeDtypeStruct((N_CHIPS,) + s.shape, s.dtype) for s in shards],
        scratch_shapes=[pltpu.VMEM((nt, HEAD_PAIR, tile), BF16), pltpu.VMEM((nt, HEAD_PAIR, tile), BF16),
                        pltpu.VMEM((2, s_len, HEAD_PAIR), F32)] + sems,
        compiler_params=pltpu.CompilerParams(dimension_semantics=("arbitrary",)),
    )(q, k, v, cumr, *shards)
    return (o, lse.reshape(cum.shape), *gathered)


def _flash_bwd(q, k, v, o, do, lse, cum, outgoing=()):
    s_len, width = q.shape
    tile = min(BWD_TILE, s_len // 2)
    nt = s_len // tile
    reps = tile // 128
    cumr = cum.reshape(-1, tile)
    lse = lse.reshape(-1, tile)
    ns = len(outgoing)
    pairs = width // HEAD_PAIR

    def body(*refs):
        q_ref, k_ref, v_ref, o_ref, do_ref, lse_ref, cum_ref = refs[:7]
        dq_ref, dk_ref, dv_ref, dcum_ref = refs[7 + ns:11 + ns]
        (q_t, do_t, k_t, do_bf, cum_col, dq_t_acc, delta, q_side, dk_acc, dv_acc,
         k_side) = refs[11 + 2 * ns:22 + 2 * ns]
        pid = pl.program_id(0)
        if ns:
            start, finish = _scatter_steps(refs[7:7 + ns], refs[11 + ns:11 + 2 * ns], *refs[22 + 2 * ns:])
            pl.when(pid == 0)(start)
        top = lax.broadcasted_iota(jnp.int32, (HEAD_PAIR, tile), 0) < HEAD_DIM
        left = lax.broadcasted_iota(jnp.int32, (tile, HEAD_PAIR), 1) < HEAD_DIM
        causal = (lax.broadcasted_iota(jnp.int32, (tile, tile), 0)
                  <= lax.broadcasted_iota(jnp.int32, (tile, tile), 1))

        def pre(i, carry):
            r0 = pl.multiple_of(i * tile, tile)
            d_o = do_ref[pl.ds(r0, tile), :]
            prod_t = (d_o * o_ref[pl.ds(r0, tile), :]).T
            delta[pl.ds(i, 1), :] = jnp.sum(prod_t[:HEAD_DIM], axis=0, keepdims=True)
            delta[pl.ds(nt + i, 1), :] = jnp.sum(prod_t[HEAD_DIM:], axis=0, keepdims=True)
            do_bf[pl.ds(r0, tile), :] = d_o.astype(BF16)
            do_t[i] = d_o.T.astype(BF16)
            q_t[i] = q_ref[pl.ds(r0, tile), :].astype(F32).T.astype(BF16)
            k_t[i] = k_ref[pl.ds(r0, tile), :].astype(F32).T.astype(BF16)
            dq_t_acc[i] = jnp.zeros((HEAD_PAIR, tile), F32)
            for h in (0, 1):
                row = cum_ref[pl.ds((2 * pid + h) * nt + i, 1), :]
                cum_col[h, pl.ds(r0, tile), :] = jnp.broadcast_to(row, (HEAD_PAIR, tile)).T
                q_side[pl.ds(h * nt + i, 1), :] = jnp.zeros((1, tile), F32)
            return carry

        lax.fori_loop(0, nt, pre, 0)

        def kv_loop(kj, carry):
            k0 = pl.multiple_of(kj * tile, tile)
            kt = k_ref[pl.ds(k0, tile), :]
            vt = v_ref[pl.ds(k0, tile), :]
            ktt = k_t[kj]
            zt = jnp.zeros_like(ktt)
            zr = jnp.zeros_like(kt)
            kms = (jnp.where(top, ktt, zt), jnp.where(top, zt, ktt))
            cols = [jnp.tile(cum_col[h, pl.ds(k0, tile), :], (1, reps)) for h in (0, 1)]
            dk_acc[...] = jnp.zeros_like(dk_acc)
            dv_acc[...] = jnp.zeros_like(dv_acc)
            k_side[...] = jnp.zeros_like(k_side)

            def step(qi, carry, masked):
                q0 = pl.multiple_of(qi * tile, tile)
                qtt = q_t[qi]
                dtt = do_t[qi]
                q_rows = q_ref[pl.ds(q0, tile), :]
                do_rows = do_bf[pl.ds(q0, tile), :]
                dq_part = None
                for h in (0, 1):
                    q_m = jnp.where(top, qtt, zt) if h == 0 else jnp.where(top, zt, qtt)
                    do_m = jnp.where(top, dtt, zt) if h == 0 else jnp.where(top, zt, dtt)
                    s = jnp.dot(kt, q_m, preferred_element_type=F32) - cols[h]
                    if masked:
                        s = jnp.where(causal, s, NEG)
                    p = jnp.exp(s - lse_ref[pl.ds((2 * pid + h) * nt + qi, 1), :])
                    dp = jnp.dot(vt, do_m, preferred_element_type=F32)
                    ds = p * (dp - delta[pl.ds(h * nt + qi, 1), :])
                    q_side[pl.ds(h * nt + qi, 1), :] += jnp.sum(ds, axis=0, keepdims=True)
                    folded = ds[:, :128]
                    for b in range(1, reps):
                        folded = folded + ds[:, b * 128:(b + 1) * 128]
                    k_side[h] += folded
                    pb = p.astype(BF16)
                    dsb = ds.astype(BF16)
                    do_h = jnp.where(left, do_rows, zr) if h == 0 else jnp.where(left, zr, do_rows)
                    q_h = jnp.where(left, q_rows, zr) if h == 0 else jnp.where(left, zr, q_rows)
                    dv_acc[...] += jnp.dot(pb, do_h, preferred_element_type=F32)
                    dk_acc[...] += jnp.dot(dsb, q_h, preferred_element_type=F32)
                    part = jnp.dot(kms[h], dsb, preferred_element_type=F32)
                    dq_part = part if dq_part is None else dq_part + part
                dq_t_acc[qi] += dq_part
                return carry

            step(kj, 0, True)
            lax.fori_loop(kj + 1, nt, lambda qi, c: step(qi, c, False), 0)
            dk_ref[pl.ds(k0, tile), :] = dk_acc[...].astype(BF16)
            dv_ref[pl.ds(k0, tile), :] = dv_acc[...].astype(BF16)
            for h in (0, 1):
                dcum_ref[pl.ds((2 * pid + h) * nt + kj, 1), :] = -jnp.sum(k_side[h].T, axis=0, keepdims=True)
            return carry

        lax.fori_loop(0, nt, kv_loop, 0)

        def fin(i, carry):
            r0 = pl.multiple_of(i * tile, tile)
            dq_ref[pl.ds(r0, tile), :] = dq_t_acc[i].T.astype(BF16)
            for h in (0, 1):
                dcum_ref[pl.ds((2 * pid + h) * nt + i, 1), :] += q_side[pl.ds(h * nt + i, 1), :]
            return carry

        lax.fori_loop(0, nt, fin, 0)
        if ns:
            pl.when(pid == pairs - 1)(finish)

    blk = pl.BlockSpec((s_len, HEAD_PAIR), lambda p: (0, p))
    full = pl.BlockSpec(cumr.shape, lambda p: (0, 0))
    transposed = pltpu.VMEM((nt, HEAD_PAIR, tile), BF16)
    sems = [pltpu.SemaphoreType.DMA((3 * ns,))] * 2 if ns else []
    dq, dk, dv, dcum, *arrived = pl.pallas_call(
        body, name="flash_bwd_scatter" if ns else "flash_bwd", grid=(pairs,),
        in_specs=[blk, blk, blk, blk, blk, full, full] + [ANY] * ns,
        out_specs=[blk, blk, blk, full] + [ANY] * ns,
        out_shape=[jax.ShapeDtypeStruct((s_len, width), BF16)] * 3 + [jax.ShapeDtypeStruct(cumr.shape, F32)]
        + [jax.ShapeDtypeStruct(t.shape, t.dtype) for t in outgoing],
        scratch_shapes=[transposed, transposed, transposed, pltpu.VMEM((s_len, HEAD_PAIR), BF16),
                        pltpu.VMEM((2, s_len, HEAD_PAIR), F32), pltpu.VMEM((nt, HEAD_PAIR, tile), F32),
                        pltpu.VMEM((2 * nt, tile), F32), pltpu.VMEM((2 * nt, tile), F32),
                        pltpu.VMEM((tile, HEAD_PAIR), F32), pltpu.VMEM((tile, HEAD_PAIR), F32),
                        pltpu.VMEM((2, tile, HEAD_PAIR), F32)] + sems,
        compiler_params=pltpu.CompilerParams(dimension_semantics=("arbitrary",)),
    )(q, k, v, o, do, lse, cumr, *outgoing)
    return (dq, dk, dv, dcum.reshape(cum.shape), *arrived)


def _out_ln(a, gate, x, w, g, b, name, target=None):
    s_len, d = x.shape
    tm = min(512, s_len)
    nt = 0 if target is None else 1

    def body(*refs):
        a_ref, gate_ref, x_ref, w_ref, g_ref, b_ref = refs[:6]
        first_ref, xh_ref, rs_ref, yg_ref = refs[6 + nt:10 + nt]
        gt = gate_ref[...]
        yg = (a_ref[...] * (gt * _sigmoid(gt))).astype(BF16)
        yg_ref[...] = yg
        z = ALPHA * x_ref[...] + jnp.dot(yg, w_ref[...], preferred_element_type=F32)
        zc = z - jnp.mean(z, axis=1, keepdims=True)
        rstd = lax.rsqrt(jnp.mean(zc * zc, axis=1, keepdims=True) + LN_EPS)
        xh = zc * rstd
        xh_ref[...] = xh
        y = xh * g_ref[...] + b_ref[...]
        rs_ref[...] = jnp.broadcast_to(rstd, (tm, 128))
        if nt:
            sq_ref = refs[10 + nt]

            @pl.when(pl.program_id(0) == 0)
            def _():
                sq_ref[...] = jnp.zeros_like(sq_ref)

            diff = y - refs[6][...]
            first_ref[...] = diff * (1.0 / d)
            sq_ref[...] += jnp.sum(diff * diff, axis=0, keepdims=True)
        else:
            first_ref[...] = y

    row = pl.BlockSpec((tm, d), lambda i: (i, 0))
    vec = pl.BlockSpec((1, d), lambda i: (0, 0))
    return pl.pallas_call(
        body, name=name + "_loss" if nt else name, grid=(s_len // tm,),
        in_specs=[row, row, row, pl.BlockSpec(w.shape, lambda i: (0, 0)), vec, vec] + [row] * nt,
        out_specs=[row, row, pl.BlockSpec((tm, 128), lambda i: (i, 0)), row] + [vec] * nt,
        out_shape=[jax.ShapeDtypeStruct((s_len, d), F32), jax.ShapeDtypeStruct((s_len, d), F32),
                   jax.ShapeDtypeStruct((s_len, 128), F32), jax.ShapeDtypeStruct((s_len, d), BF16)]
        + [jax.ShapeDtypeStruct((1, d), F32)] * nt,
        compiler_params=pltpu.CompilerParams(dimension_semantics=("arbitrary" if nt else "parallel",)),
    )(a, gate, x, w, g, b, *([target] if nt else []))


def _ln_bwd(dout, xh, rs, g, w, gate, a, name, swap=None):
    s_len, d = dout.shape
    tm = min(512, s_len)
    steps = s_len // tm
    ns = 0 if swap is None else 1

    def body(*refs):
        do_ref, xh_ref, rs_ref, g_ref, w_ref, gate_ref, a_ref = refs[:7]
        dz_ref, da_ref, dgate_ref, dg_ref, db_ref = refs[7 + ns:12 + ns]
        if ns:
            start, finish = _swap_steps(refs[7:8], refs[12 + ns:13 + ns], [swap[1]], *refs[13 + ns:])
            pl.when(pl.program_id(0) == 0)(start)

        @pl.when(pl.program_id(0) == 0)
        def _():
            dg_ref[...] = jnp.zeros_like(dg_ref)
            db_ref[...] = jnp.zeros_like(db_ref)

        dout_t = do_ref[...]
        xh_t = xh_ref[...]
        dg_ref[...] += jnp.sum(dout_t * xh_t, axis=0, keepdims=True)
        db_ref[...] += jnp.sum(dout_t, axis=0, keepdims=True)
        dxh = dout_t * g_ref[...]
        m1 = jnp.mean(dxh, axis=1, keepdims=True)
        m2 = jnp.mean(dxh * xh_t, axis=1, keepdims=True)
        dz = rs_ref[:, 0:1] * (dxh - m1 - xh_t * m2)
        dz_ref[...] = dz
        dyg = lax.dot_general(dz.astype(BF16), w_ref[...], NT_DIMS, preferred_element_type=F32)
        gt = gate_ref[...]
        sg = _sigmoid(gt)
        da_ref[...] = dyg * (gt * sg)
        dgate_ref[...] = (dyg * a_ref[...] * (sg * (1.0 + gt * (1.0 - sg)))).astype(BF16)
        if ns:
            pl.when(pl.program_id(0) == steps - 1)(finish)

    row = pl.BlockSpec((tm, d), lambda i: (i, 0))
    vec = pl.BlockSpec((1, d), lambda i: (0, 0))
    extra_out = [jax.ShapeDtypeStruct((N_CHIPS, swap[1][1], LANES), swap[0].dtype)] if ns else []
    return pl.pallas_call(
        body, name=name + "_swap" if ns else name, grid=(steps,),
        in_specs=[row, row, pl.BlockSpec((tm, 128), lambda i: (i, 0)), vec, pl.BlockSpec(w.shape, lambda i: (0, 0)),
                  row, row] + [ANY] * ns,
        out_specs=[row, row, row, vec, vec] + [ANY] * ns,
        out_shape=[jax.ShapeDtypeStruct((s_len, d), F32), jax.ShapeDtypeStruct((s_len, d), F32),
                   jax.ShapeDtypeStruct((s_len, d), BF16), jax.ShapeDtypeStruct((1, d), F32),
                   jax.ShapeDtypeStruct((1, d), F32)] + extra_out,
        scratch_shapes=[pltpu.SemaphoreType.DMA((N_CHIPS,))] * 2 if ns else [],
        compiler_params=pltpu.CompilerParams(dimension_semantics=("arbitrary",)),
    )(dout, xh, rs, g, w, gate, a, *([swap[0]] if ns else []))


def _rnn_fwd(u, conv_w, conv_b, wa, ba, wi, bi, lam):
    s_len, width = u.shape
    tm = min(512, s_len // 2)
    nb = width // RNN_BLOCK

    def body(u_ref, up_ref, cw_ref, cb_ref, wa_ref, ba_ref, wi_ref, bi_ref, lam_ref,
             h_ref, uc_ref, r_ref, i_ref, a_ref, b_scr, h_carry):
        step_id = pl.program_id(0)

        @pl.when(step_id == 0)
        def _():
            h_carry[...] = jnp.zeros_like(h_carry)

        for n in range(nb):
            blk = slice(n * RNN_BLOCK, (n + 1) * RNN_BLOCK)
            ut = u_ref[:, blk]
            prev = jnp.where(step_id > 0, up_ref[:, blk], 0.0)
            uc = cb_ref[:, blk] + cw_ref[3:4, blk] * ut
            for k in (1, 2, 3):
                uc = uc + cw_ref[3 - k:4 - k, blk] * _shift_down(ut, prev, k)
            ucb = uc.astype(BF16)
            r = _sigmoid(jnp.dot(ucb, wa_ref[n], preferred_element_type=F32) + ba_ref[:, blk])
            ig = _sigmoid(jnp.dot(ucb, wi_ref[n], preferred_element_type=F32) + bi_ref[:, blk])
            log_a = -LRU_C * r * _softplus(-lam_ref[:, blk])
            uc_ref[:, blk] = uc
            r_ref[:, blk] = r
            i_ref[:, blk] = ig
            a_ref[:, blk] = jnp.exp(log_a)
            b_scr[:, blk] = jnp.sqrt(_neg_expm1(2.0 * log_a)) * (ig * uc)

        def scan(t, h):
            h = a_ref[pl.ds(t, 1), :] * h + b_scr[pl.ds(t, 1), :]
            h_ref[pl.ds(t, 1), :] = h
            return h

        h_carry[...] = lax.fori_loop(0, tm, scan, h_carry[...], unroll=8)

    row = pl.BlockSpec((tm, width), lambda i: (i, 0))
    prev8 = pl.BlockSpec((8, width), lambda i: (jnp.maximum(i * (tm // 8) - 1, 0), 0))
    vec = pl.BlockSpec((1, width), lambda i: (0, 0))
    mat = pl.BlockSpec(wa.shape, lambda i: (0, 0, 0))
    return pl.pallas_call(
        body, name="rnn_fwd", grid=(s_len // tm,),
        in_specs=[row, prev8, pl.BlockSpec(conv_w.shape, lambda i: (0, 0)), vec, mat, vec, mat, vec, vec],
        out_specs=[row] * 5,
        out_shape=[jax.ShapeDtypeStruct((s_len, width), F32)] * 5,
        scratch_shapes=[pltpu.VMEM((tm, width), F32), pltpu.VMEM((1, width), F32)],
        compiler_params=pltpu.CompilerParams(dimension_semantics=("arbitrary",)),
    )(u, u, conv_w, conv_b, wa, ba, wi, bi, lam)


def _rnn_bwd(dh, a, h, r, ig, uc, lam, wa, wi):
    s_len, width = dh.shape
    tm = min(512, s_len // 2)
    nt = s_len // tm
    nb = width // RNN_BLOCK

    def body(dh_ref, a_ref, h_ref, hp_ref, r_ref, i_ref, uc_ref, lam_ref, wa_ref, wi_ref,
             duc_ref, dwa_ref, dwi_ref, dba_ref, dbi_ref, dlam_ref, g_scr, c_scr, dsp_scr):
        step_id = pl.program_id(0)
        tile_id = nt - 1 - step_id

        @pl.when(step_id == 0)
        def _():
            c_scr[...] = jnp.zeros_like(c_scr)
            dsp_scr[...] = jnp.zeros_like(dsp_scr)
            dwa_ref[...] = jnp.zeros_like(dwa_ref)
            dwi_ref[...] = jnp.zeros_like(dwi_ref)
            dba_ref[...] = jnp.zeros_like(dba_ref)
            dbi_ref[...] = jnp.zeros_like(dbi_ref)

        def scan(j, c):
            t = tm - 1 - j
            g = dh_ref[pl.ds(t, 1), :] + c
            g_scr[pl.ds(t, 1), :] = g
            return a_ref[pl.ds(t, 1), :] * g

        c_scr[...] = lax.fori_loop(0, tm, scan, c_scr[...], unroll=8)

        for n in range(nb):
            blk = slice(n * RNN_BLOCK, (n + 1) * RNN_BLOCK)
            g_t = g_scr[:, blk]
            hp8 = jnp.where(tile_id > 0, hp_ref[:, blk], 0.0)
            h_prev = _shift_down(h_ref[:, blk], hp8, 1)
            av, rv, iv, ucv = a_ref[:, blk], r_ref[:, blk], i_ref[:, blk], uc_ref[:, blk]
            sp = _softplus(-lam_ref[:, blk])
            mag = jnp.sqrt(_neg_expm1(-2.0 * LRU_C * rv * sp))
            d_iu = g_t * mag
            dla = g_t * h_prev * av - g_t * (iv * ucv) * (av * av) / mag
            dsp_scr[:, blk] += jnp.sum(dla * (-LRU_C * rv), axis=0, keepdims=True)
            dra = dla * (-LRU_C * sp) * rv * (1.0 - rv)
            dia = d_iu * ucv * iv * (1.0 - iv)
            drab, diab, ucb = dra.astype(BF16), dia.astype(BF16), ucv.astype(BF16)
            duc_ref[:, blk] = (d_iu * iv
                               + lax.dot_general(drab, wa_ref[n], NT_DIMS, preferred_element_type=F32)
                               + lax.dot_general(diab, wi_ref[n], NT_DIMS, preferred_element_type=F32))
            dwa_ref[n] += lax.dot_general(ucb, drab, TN_DIMS, preferred_element_type=F32)
            dwi_ref[n] += lax.dot_general(ucb, diab, TN_DIMS, preferred_element_type=F32)
            dba_ref[:, blk] += jnp.sum(dra, axis=0, keepdims=True)
            dbi_ref[:, blk] += jnp.sum(dia, axis=0, keepdims=True)

        @pl.when(step_id == nt - 1)
        def _():
            dlam_ref[...] = -dsp_scr[...] * _sigmoid(-lam_ref[...])

    row = pl.BlockSpec((tm, width), lambda i: (nt - 1 - i, 0))
    prev8 = pl.BlockSpec((8, width), lambda i: (jnp.maximum((nt - 1 - i) * (tm // 8) - 1, 0), 0))
    vec = pl.BlockSpec((1, width), lambda i: (0, 0))
    mat = pl.BlockSpec(wa.shape, lambda i: (0, 0, 0))
    return pl.pallas_call(
        body, name="rnn_bwd", grid=(nt,),
        in_specs=[row, row, row, prev8, row, row, row, vec, mat, mat],
        out_specs=[row, mat, mat, vec, vec, vec],
        out_shape=[jax.ShapeDtypeStruct((s_len, width), F32), jax.ShapeDtypeStruct(wa.shape, F32),
                   jax.ShapeDtypeStruct(wa.shape, F32)] + [jax.ShapeDtypeStruct((1, width), F32)] * 3,
        scratch_shapes=[pltpu.VMEM((tm, width), F32), pltpu.VMEM((1, width), F32), pltpu.VMEM((1, width), F32)],
        compiler_params=pltpu.CompilerParams(dimension_semantics=("arbitrary",)),
    )(dh, a, h, h, r, ig, uc, lam, wa, wi)


def _conv_bwd(duc, u, conv_w):
    s_len, width = duc.shape
    tm = min(256, s_len)
    nt = s_len // tm

    def body(d_ref, dn_ref, u_ref, up_ref, cw_ref, du_ref, dcw_ref, dcb_ref):
        step_id = pl.program_id(0)

        @pl.when(step_id == 0)
        def _():
            dcw_ref[...] = jnp.zeros_like(dcw_ref)
            dcb_ref[...] = jnp.zeros_like(dcb_ref)

        for n in range(width // RNN_BLOCK):
            blk = slice(n * RNN_BLOCK, (n + 1) * RNN_BLOCK)
            dt = d_ref[:, blk]
            ut = u_ref[:, blk]
            nxt = jnp.where(step_id < nt - 1, dn_ref[:, blk], 0.0)
            prev = jnp.where(step_id > 0, up_ref[:, blk], 0.0)
            du = cw_ref[3:4, blk] * dt
            dcw_ref[3:4, blk] += jnp.sum(dt * ut, axis=0, keepdims=True)
            for k in (1, 2, 3):
                du = du + cw_ref[3 - k:4 - k, blk] * _shift_up(dt, nxt, k)
                dcw_ref[3 - k:4 - k, blk] += jnp.sum(dt * _shift_down(ut, prev, k), axis=0, keepdims=True)
            du_ref[:, blk] = du.astype(BF16)
            dcb_ref[:, blk] += jnp.sum(dt, axis=0, keepdims=True)

    row = pl.BlockSpec((tm, width), lambda i: (i, 0))
    prev8 = pl.BlockSpec((8, width), lambda i: (jnp.maximum(i * (tm // 8) - 1, 0), 0))
    next8 = pl.BlockSpec((8, width), lambda i: (jnp.minimum((i + 1) * (tm // 8), s_len // 8 - 1), 0))
    return pl.pallas_call(
        body, name="conv_bwd", grid=(nt,),
        in_specs=[row, next8, row, prev8, pl.BlockSpec(conv_w.shape, lambda i: (0, 0))],
        out_specs=[row, pl.BlockSpec(conv_w.shape, lambda i: (0, 0)), pl.BlockSpec((1, width), lambda i: (0, 0))],
        out_shape=[jax.ShapeDtypeStruct((s_len, width), BF16), jax.ShapeDtypeStruct(conv_w.shape, F32),
                   jax.ShapeDtypeStruct((1, width), F32)],
        compiler_params=pltpu.CompilerParams(dimension_semantics=("arbitrary",)),
    )(duc, duc, u, u, conv_w)


def _pair_sum(core, grads, theirs, first_row, out_dtype):
    n, half, _ = theirs.shape
    tb = 128 if half % 128 == 0 and first_row % 128 == 0 else half
    assert first_row % tb == 0 and half % tb == 0

    def body(c_ref, a_ref, b_ref, o_ref):
        o_ref[...] = (a_ref[...] + b_ref[...]).astype(out_dtype)

    blk = pl.BlockSpec((n, tb, LANES), lambda i, c: (0, i, 0))
    mine = pl.BlockSpec((n, tb, LANES), lambda i, c: (0, first_row // tb + c[0] * (half // tb) + i, 0))
    return pl.pallas_call(
        body, name="pair_sum",
        grid_spec=pltpu.PrefetchScalarGridSpec(
            num_scalar_prefetch=1, grid=(half // tb,), in_specs=[mine, blk], out_specs=blk),
        out_shape=jax.ShapeDtypeStruct((n, half, LANES), out_dtype),
        compiler_params=pltpu.CompilerParams(dimension_semantics=("parallel",)),
    )(core, grads, theirs)


def _sum_chips(parts):
    rows = parts.shape[1]
    tb = 128 if rows % 128 == 0 else rows

    def body(p_ref, o_ref):
        o_ref[...] = ((p_ref[0].astype(F32) + p_ref[1].astype(F32)) + p_ref[2].astype(F32)) + p_ref[3].astype(F32)

    return pl.pallas_call(
        body, name="chip_sum", grid=(rows // tb,),
        in_specs=[pl.BlockSpec((N_CHIPS, tb, LANES), lambda i: (0, i, 0))],
        out_specs=pl.BlockSpec((tb, LANES), lambda i: (i, 0)),
        out_shape=jax.ShapeDtypeStruct((rows, LANES), F32),
        compiler_params=pltpu.CompilerParams(dimension_semantics=("parallel",)),
    )(parts)


def _adamw(w, g, m, v, lead_per_block, rows_per_block):
    n, rows, cols = w.shape
    blk = pl.BlockSpec((lead_per_block, rows_per_block, cols), lambda i, j: (i, j, 0))

    def body(w_ref, g_ref, m_ref, v_ref, d_ref, nm_ref, nv_ref):
        gt = g_ref[...]
        nm = ADAM_B1 * m_ref[...] + (1.0 - ADAM_B1) * gt
        nv = ADAM_B2 * v_ref[...] + (1.0 - ADAM_B2) * (gt * gt)
        m_hat = nm / (1.0 - ADAM_B1 ** ADAM_STEP)
        v_hat = nv / (1.0 - ADAM_B2 ** ADAM_STEP)
        d_ref[...] = -ADAM_LR * (m_hat / (jnp.sqrt(v_hat) + ADAM_EPS) + ADAM_WD * w_ref[...])
        nm_ref[...] = nm
        nv_ref[...] = nv

    return pl.pallas_call(
        body, name="adamw", grid=(n // lead_per_block, rows // rows_per_block), in_specs=[blk] * 4,
        out_specs=[blk] * 3,
        out_shape=[jax.ShapeDtypeStruct(w.shape, F32)] * 3,
        compiler_params=pltpu.CompilerParams(dimension_semantics=("parallel", "parallel")),
    )(w, g, m, v)


def _place():
    x, y, c = lax.axis_index("x"), lax.axis_index("y"), lax.axis_index("c")
    return x, y, c, [(1 - x, y), (x, 1 - y), (1 - x, 1 - y)]


ANY = pl.BlockSpec(memory_space=pl.ANY)
COPY_PARTS = 4


def _remote_parts(src_of, dst_of, rows, send_sem, recv_sem, to, begin=True):
    parts = COPY_PARTS if rows % (16 * COPY_PARTS) == 0 else 1
    step = rows // parts
    for i in range(parts if begin is not False else 0):
        pltpu.make_async_remote_copy(src_ref=src_of(i * step, step), dst_ref=dst_of(i * step, step),
                                     send_sem=send_sem, recv_sem=recv_sem, device_id=to, device_id_type=MESH).start()
    if begin == "only":
        return None
    return pltpu.make_async_remote_copy(src_ref=src_of(0, rows), dst_ref=dst_of(0, rows), send_sem=send_sem,
                                        recv_sem=recv_sem, device_id=to, device_id_type=MESH)


GATHER_SEMS = 7


def _gather_steps(p_refs, o_refs, send_sems, recv_sems):
    x, y, c, chips = _place()
    me = 2 * x + y

    def half(ref, which):
        rows = ref.shape[0] // 2
        return ref.at[pl.ds(which * rows, rows)]

    def copy(k, src, dst, to):
        return pltpu.make_async_remote_copy(src_ref=src, dst_ref=dst, send_sem=send_sems.at[k],
                                            recv_sem=recv_sems.at[k], device_id=to, device_id_type=MESH)

    def first_copies():
        out = []
        for b, (p_ref, o_ref) in enumerate(zip(p_refs, o_refs)):
            for j, (cx, cy) in enumerate(chips):
                out.append(copy(GATHER_SEMS * b + j, half(p_ref, c), half(o_ref.at[me], c), (cx, cy, c)))
            out.append(copy(GATHER_SEMS * b + 6, p_ref, o_ref.at[me], (x, y, 1 - c)))
        return out

    def passed_copies():
        out = []
        for b, o_ref in enumerate(o_refs):
            for j, (cx, cy) in enumerate(chips):
                landed = half(o_ref.at[2 * cx + cy], c)
                out.append(copy(GATHER_SEMS * b + 3 + j, landed, landed, (x, y, 1 - c)))
        return out

    def start():
        for cp in first_copies():
            cp.start()

    def forward():
        for b, o_ref in enumerate(o_refs):
            for j, (cx, cy) in enumerate(chips):
                landed = half(o_ref.at[2 * cx + cy], c)
                copy(GATHER_SEMS * b + j, landed, landed, (x, y, c)).wait_recv()
        for cp in passed_copies():
            cp.start()

    def finish():
        for b, o_ref in enumerate(o_refs):
            for j, (cx, cy) in enumerate(chips):
                other = half(o_ref.at[2 * cx + cy], 1 - c)
                copy(GATHER_SEMS * b + 3 + j, other, other, (x, y, c)).wait_recv()
            copy(GATHER_SEMS * b + 6, o_ref.at[me], o_ref.at[me], (x, y, c)).wait_recv()
        for cp in first_copies() + passed_copies():
            cp.wait_send()

    return start, forward, finish


def _gather_chips(shards):
    nb = len(shards)

    def body(*refs):
        for step in _gather_steps(refs[:nb], refs[nb:2 * nb], *refs[2 * nb:]):
            step()

    return pl.pallas_call(
        body, name="gather_chips", in_specs=[ANY] * nb, out_specs=[ANY] * nb,
        out_shape=[jax.ShapeDtypeStruct((N_CHIPS,) + s.shape, s.dtype) for s in shards],
        scratch_shapes=[pltpu.SemaphoreType.DMA((GATHER_SEMS * nb,)), pltpu.SemaphoreType.DMA((GATHER_SEMS * nb,))],
    )(*shards)


def _swap_steps(g_refs, t_refs, spans, send_sems, recv_sems):
    x, y, c, _ = _place()

    def gives(begin):
        return [_remote_parts(
            lambda r0, m, g_ref=g_ref, j=j, base=first_row + (1 - c) * half: g_ref.at[j, pl.ds(base + r0, m), :],
            lambda r0, m, t_ref=t_ref, j=j: t_ref.at[j, pl.ds(r0, m), :],
            half, send_sems.at[b * N_CHIPS + j], recv_sems.at[b * N_CHIPS + j], (x, y, 1 - c), begin)
            for b, (g_ref, t_ref, (first_row, half)) in enumerate(zip(g_refs, t_refs, spans)) for j in range(N_CHIPS)]

    def start():
        gives("only")

    def finish():
        for give in gives(False):
            give.wait()

    return start, finish


def _swap_halves(bufs, spans):
    nb = len(bufs)

    def body(*refs):
        for step in _swap_steps(refs[:nb], refs[nb:2 * nb], spans, *refs[2 * nb:]):
            step()

    return pl.pallas_call(
        body, name="swap_halves", in_specs=[ANY] * nb, out_specs=[ANY] * nb,
        out_shape=[jax.ShapeDtypeStruct((b.shape[0], half, b.shape[2]), b.dtype) for b, (_, half) in zip(bufs, spans)],
        scratch_shapes=[pltpu.SemaphoreType.DMA((nb * N_CHIPS,)), pltpu.SemaphoreType.DMA((nb * N_CHIPS,))],
    )(*bufs)


def _scatter_steps(t_refs, o_refs, send_sems, recv_sems):
    x, y, c, chips = _place()
    me = 2 * x + y

    def slot(ref, chip):
        return lambda r0, n: ref.at[chip, pl.ds(r0, n), :]

    def sends(begin):
        return [_remote_parts(slot(t_ref, 2 * cx + cy), slot(o_ref, me), t_ref.shape[1], send_sems.at[3 * b + j],
                              recv_sems.at[3 * b + j], (cx, cy, c), begin)
                for b, (t_ref, o_ref) in enumerate(zip(t_refs, o_refs)) for j, (cx, cy) in enumerate(chips)]

    def start():
        sends("only")

    def finish():
        for b, o_ref in enumerate(o_refs):
            for j, (cx, cy) in enumerate(chips):
                landed = o_ref.at[2 * cx + cy]
                pltpu.make_async_remote_copy(src_ref=landed, dst_ref=landed, send_sem=send_sems.at[3 * b + j],
                                             recv_sem=recv_sems.at[3 * b + j], device_id=(x, y, c),
                                             device_id_type=MESH).wait_recv()
        for cp in sends(False):
            cp.wait_send()

    return start, finish


def _give_sibling(bufs):
    nb = len(bufs)

    def body(*refs):
        h_refs, o_refs, (send_sems, recv_sems) = refs[:nb], refs[nb:2 * nb], refs[2 * nb:]
        x, y, c, _ = _place()
        gives = [_remote_parts(lambda r0, n, h_ref=h_ref: h_ref.at[pl.ds(r0, n), :],
                               lambda r0, n, o_ref=o_ref: o_ref.at[pl.ds(r0, n), :],
                               h_ref.shape[0], send_sems.at[b], recv_sems.at[b], (x, y, 1 - c))
                 for b, (h_ref, o_ref) in enumerate(zip(h_refs, o_refs))]
        for give in gives:
            give.wait()

    return pl.pallas_call(
        body, name="give_sibling", in_specs=[ANY] * nb, out_specs=[ANY] * nb,
        out_shape=[jax.ShapeDtypeStruct(b.shape, b.dtype) for b in bufs],
        scratch_shapes=[pltpu.SemaphoreType.DMA((nb,)), pltpu.SemaphoreType.DMA((nb,))],
    )(*bufs)


def _pack(arrays, dtype, row_align):
    flat = []
    for arr in arrays:
        v = arr.reshape(-1)
        flat.append(jnp.pad(v, (0, (-v.shape[0]) % LANES)))
    total = sum(f.shape[0] for f in flat) // LANES
    pad_rows = (-total) % row_align
    if pad_rows:
        flat.append(jnp.zeros((pad_rows * LANES,), dtype))
    return jnp.concatenate(flat).reshape(-1, LANES)


def _unpack(buf, shapes, rows_align=1):
    lead = buf.shape[:-2]
    flat = buf.reshape(lead + (-1,))
    out, off = [], 0
    for shape in shapes:
        size = 1
        for dim in shape:
            size *= dim
        out.append(flat[..., off:off + size].reshape(lead + tuple(shape)))
        off += size + (-size) % (LANES * rows_align)
    return out


def _from_chips(parts, axis):
    return jnp.concatenate([parts[j] for j in range(N_CHIPS)], axis=axis)


def kernel(x, ln_g, ln_b, attn_w_in, attn_b_f, attn_w_out, rnn_w_in, rnn_conv_w, rnn_conv_b, rnn_w_a, rnn_b_a, rnn_w_i, rnn_b_i, rnn_lambda, rnn_w_out, loss_target, m_ln_g, m_ln_b, m_attn_w_in, m_attn_b_f, m_attn_w_out, m_rnn_w_in, m_rnn_conv_w, m_rnn_conv_b, m_rnn_w_a, m_rnn_b_a, m_rnn_w_i, m_rnn_b_i, m_rnn_lambda, m_rnn_w_out, v_ln_g, v_ln_b, v_attn_w_in, v_attn_b_f, v_attn_w_out, v_rnn_w_in, v_rnn_conv_w, v_rnn_conv_b, v_rnn_w_a, v_rnn_b_a, v_rnn_w_i, v_rnn_b_i, v_rnn_lambda, v_rnn_w_out):
    s_len, d = x.shape[1], x.shape[2]
    xs = x.reshape(s_len, d)
    target = loss_target.reshape(s_len, d)
    heads = attn_b_f.shape[1]
    qkvg = attn_w_in.shape[2] * N_CHIPS - heads

    me = 2 * lax.axis_index("x") + lax.axis_index("y")
    core = lax.axis_index("c").astype(jnp.int32)
    small = [rnn_conv_w, rnn_conv_b, rnn_b_a, rnn_b_i, rnn_lambda]
    a_in, a_out = attn_w_in.astype(BF16), attn_w_out.astype(BF16)
    later = [a_in[1], a_out] + [w.astype(BF16) for w in (rnn_w_in, rnn_w_a, rnn_w_i, rnn_w_out)]
    got_in, got_small = _gather_chips([a_in[0], _pack(small, F32, 16)])
    conv_w, conv_b, b_a, b_i, lam = [
        _from_chips(p, ax) for p, ax in zip(_unpack(got_small, [w.shape for w in small]), (2, 1, 1, 1, 1))]

    def attn_in_weights(w_in):
        shard = w_in.shape[2]

        def columns(first, last):
            return [w_in[j][:, max(first - j * shard, 0):min(last - j * shard, shard)]
                    for j in range(N_CHIPS) if first < (j + 1) * shard and last > j * shard]

        return jnp.concatenate([t * (HEAD_DIM ** -0.5) for t in columns(0, d)] + columns(d, qkvg + heads)
                               + [jnp.zeros((d, 128 - heads), BF16)], axis=1)

    w_cat = [attn_in_weights(got_in), None]
    b_f = jnp.pad(attn_b_f, ((0, 0), (0, 128 - heads)))

    saved = []
    cur = xs
    for layer in range(DEPTH):
        idx = layer // 2
        g_l, b_l = ln_g[layer:layer + 1], ln_b[layer:layer + 1]
        goal = target if layer == DEPTH - 1 else None
        if layer % 2 == 0:
            q, k, v, gate, fl = _proj(cur, w_cat[idx], [(0, d, BF16), (d, d, BF16), (2 * d, d, BF16),
                                                         (3 * d, d, F32), (4 * d, 128, F32)], "attn_proj")
            cum, sneg = _fcum(fl, b_f[idx:idx + 1])
            o, lse, *rest = _flash_fwd(q, k, v, cum, later if layer == 0 else ())
            if layer == 0:
                w_cat[1] = attn_in_weights(rest[0])
                w_out_a = jnp.moveaxis(rest[1], 0, 1).reshape(2, d, d)
                w_in_r = jnp.moveaxis(rest[2], 0, 2).reshape(2, d, 2 * d)
                w_a = jnp.transpose(rest[3], (1, 2, 0, 3, 4)).reshape(2, -1, RNN_BLOCK, RNN_BLOCK)
                w_i = jnp.transpose(rest[4], (1, 2, 0, 3, 4)).reshape(2, -1, RNN_BLOCK, RNN_BLOCK)
                w_out_r = jnp.moveaxis(rest[5], 0, 1).reshape(2, d, d)
            nxt, xh, rs, yg, *sq = _out_ln(o, gate, cur, w_out_a[idx], g_l, b_l, "out_ln", goal)
            saved.append(dict(x=cur, q=q, k=k, v=v, gate=gate, cum=cum, sneg=sneg, a=o, lse=lse, xh=xh, rs=rs, yg=yg))
        else:
            u, gate = _proj(cur, w_in_r[idx], [(0, d, F32), (d, d, F32)], "rnn_proj")
            vecs = [t[idx:idx + 1] for t in (conv_b, b_a, b_i, lam)]
            hseq, uc, r, ig, a = _rnn_fwd(u, conv_w[idx], vecs[0], w_a[idx], vecs[1], w_i[idx], vecs[2], vecs[3])
            nxt, xh, rs, yg, *sq = _out_ln(hseq, gate, cur, w_out_r[idx], g_l, b_l, "out_ln", goal)
            saved.append(dict(x=cur, u=u, gate=gate, uc=uc, r=r, ig=ig, av=a, a=hseq, xh=xh, rs=rs, yg=yg, lam=vecs[3]))
        cur = nxt

    dout = cur
    loss_here = 0.5 * jnp.sum(sq[0]) / d

    g_ln_g, g_ln_b = [None] * DEPTH, [None] * DEPTH
    g_attn = [None] * 2
    g_rnn = [None] * 2
    slots = None
    core1 = core.reshape(1)
    late_half = (SLOT_ROWS - LATE_ROWS) // 2

    def by_chip(t, axis):
        shape = t.shape[:axis] + (N_CHIPS, t.shape[axis] // N_CHIPS) + t.shape[axis + 1:]
        flat = jnp.moveaxis(t.reshape(shape), axis, 0).reshape(N_CHIPS, -1)
        return jnp.pad(flat, ((0, 0), (0, (-flat.shape[1]) % (8 * LANES)))).reshape(N_CHIPS, -1, LANES)

    def both(key):
        return jnp.stack([it[key] for it in g_rnn])

    for layer in reversed(range(DEPTH)):
        idx = layer // 2
        sv = saved[layer]
        swap = None
        if layer == 0:
            gates = jnp.concatenate([by_chip(both("w_a"), 2), by_chip(both("w_i"), 2)], axis=1)
            slots = lax.dynamic_update_slice(slots, gates, (0, ROWS_GATES, 0))
            swap = (slots, (LATE_ROWS, late_half))
        w_out = w_out_a[idx] if layer % 2 == 0 else w_out_r[idx]
        dz, da, dgate, dg, db, *theirs_late = _ln_bwd(dout, sv["xh"], sv["rs"], ln_g[layer:layer + 1], w_out,
                                                      sv["gate"], sv["a"], "ln_bwd", swap)
        if layer == 0:
            pair_late = _pair_sum(core1, slots, theirs_late[0], LATE_ROWS, BF16)
        g_ln_g[layer], g_ln_b[layer] = dg, db
        slots = _dw_out(slots, sv["yg"], dz, (ROWS_ATTN_OUT if layer % 2 == 0 else ROWS_RNN_OUT)[idx])
        if layer % 2 == 0:
            dq, dk, dv, dcum, *arrived = _flash_bwd(sv["q"], sv["k"], sv["v"], sv["a"], da, sv["lse"], sv["cum"],
                                                    [pair_late] if layer == 0 else ())
            dlogit, dbf = _fbwd(dcum, sv["sneg"])
            pieces = [dq, dk, dv, dgate, dlogit]
            if layer > 0:
                dout, = _mm_nt(dz, [(p, j * d) for j, p in enumerate(pieces)], w_cat[idx], "attn_dx")
            slots, d_w_f = _dw_attn_in(slots, sv["x"], pieces[:4], dlogit, idx)
            g_attn[idx] = dict(w_f=d_w_f[:, :heads], b_f=dbf[:, 0])
        else:
            duc, d_wa, d_wi, d_ba, d_bi, d_lam = _rnn_bwd(da, sv["av"], sv["a"], sv["r"], sv["ig"], sv["uc"], sv["lam"],
                                                          w_a[idx], w_i[idx])
            du, d_cw, d_cb = _conv_bwd(duc, sv["u"], conv_w[idx])
            dout, = _mm_nt(dz, [(du, 0), (dgate, d)], w_in_r[idx], "rnn_dx")
            slots = _dw_rnn_in(slots, sv["x"], (du, dgate), idx)
            g_rnn[idx] = dict(conv_w=d_cw, conv_b=d_cb[0], w_a=d_wa, b_a=d_ba[0], w_i=d_wi, b_i=d_bi[0], lam=d_lam[0])

    def everywhere(t):
        flat = t.reshape(-1)
        flat = jnp.pad(flat, (0, (-flat.shape[0]) % (8 * LANES))).reshape(-1, LANES)
        return jnp.broadcast_to(flat[None], (N_CHIPS,) + flat.shape)

    in_rows = jnp.stack([slots[1:, r:r + d, :heads] for r in ROWS_ATTN_IN], axis=1)
    edges = jnp.concatenate([in_rows, jnp.stack([it["w_f"] for it in g_attn])[None]])
    rows = [everywhere(edges), by_chip(both("conv_w"), 2),
            by_chip(both("conv_b"), 1), by_chip(both("b_a"), 1), by_chip(both("b_i"), 1), by_chip(both("lam"), 1),
            everywhere(jnp.concatenate(g_ln_g)), everywhere(jnp.concatenate(g_ln_b)),
            everywhere(jnp.stack([it["b_f"] for it in g_attn])), everywhere(loss_here)]
    used = sum(r.shape[1] for r in rows)
    small = jnp.concatenate(rows + [jnp.zeros((N_CHIPS, (-used) % 32, LANES), F32)], axis=1)

    spans = [(0, LATE_ROWS // 2), (0, small.shape[1] // 2)]
    theirs = _swap_halves([slots, small], spans)
    pair = [_pair_sum(core1, slots, theirs[0], 0, BF16), _pair_sum(core1, small, theirs[1], 0, F32)]
    dout, *arrived_last = _mm_nt(dz, [(p, j * d) for j, p in enumerate(pieces)], w_cat[0], "attn_dx", pair)
    grad_x = dout.reshape(x.shape)
    summed = []
    for mine_all, got in zip([pair_late] + pair, list(arrived) + arrived_last):
        own = lax.dynamic_index_in_dim(mine_all, me, 0, keepdims=False)
        summed.append(_sum_chips(lax.dynamic_update_index_in_dim(got, own, me, 0)))
    late, early, small_sum = [
        jnp.concatenate([jnp.where(core == 0, mine, other), jnp.where(core == 0, other, mine)])
        for mine, other in zip(summed, _give_sibling(summed))]

    def rows_at(first, n):
        return early[first:first + n] if first < LATE_ROWS else late[first - LATE_ROWS:first - LATE_ROWS + n]

    g_in_group = jnp.stack([rows_at(r, d) for r in ROWS_ATTN_IN])
    g_w_out_a = jnp.stack([rows_at(r, d // N_CHIPS) for r in ROWS_ATTN_OUT])
    g_w_out_r = jnp.stack([rows_at(r, d // N_CHIPS) for r in ROWS_RNN_OUT])
    folded = jnp.stack([rows_at(r, d // 2) for r in ROWS_RNN_IN])
    g_w_in_r = jnp.concatenate([folded[:, :, :d // 2], folded[:, :, d // 2:]], axis=1)
    gate_rows = rnn_w_a.size // LANES
    g_w_a = rows_at(ROWS_GATES, gate_rows).reshape(rnn_w_a.shape)
    g_w_i = rows_at(ROWS_GATES + gate_rows, gate_rows).reshape(rnn_w_i.shape)
    (g_edges, g_conv_w, g_conv_b, g_b_a, g_b_i, g_lam, g_ln_g_sum, g_ln_b_sum, g_b_f,
     loss) = _unpack(small_sum, [
        (N_CHIPS, 2, d, heads), rnn_conv_w.shape, rnn_conv_b.shape, rnn_b_a.shape,
        rnn_b_i.shape, rnn_lambda.shape, ln_g.shape, ln_b.shape, attn_b_f.shape, ()], rows_align=8)
    wide = jnp.concatenate([g_in_group, lax.dynamic_index_in_dim(g_edges, me, 0, keepdims=False)], axis=2)
    g_w_in_a = lax.dynamic_slice(wide, (0, 0, (heads // N_CHIPS) * me), attn_w_in.shape)

    def adam_nd(w, g, m, v, view, rows_per_block):
        outs = _adamw(w.reshape(view), g.reshape(view), m.reshape(view), v.reshape(view), 1, rows_per_block)
        return [t.reshape(w.shape) for t in outs]

    turned = [jnp.transpose(t, (2, 0, 1)) for t in (attn_w_in, g_w_in_a, m_attn_w_in, v_attn_w_in)]
    upd = {
        "attn_w_in": [jnp.transpose(t, (1, 2, 0)) for t in _adamw(*turned, attn_w_in.shape[2] // N_CHIPS, 2)],
        "attn_w_out": adam_nd(attn_w_out, g_w_out_a, m_attn_w_out, v_attn_w_out, attn_w_out.shape, 256),
        "rnn_w_in": adam_nd(rnn_w_in, g_w_in_r, m_rnn_w_in, v_rnn_w_in, rnn_w_in.shape, 512),
        "rnn_w_a": adam_nd(rnn_w_a, g_w_a, m_rnn_w_a, v_rnn_w_a, (1, -1, RNN_BLOCK), 512),
        "rnn_w_i": adam_nd(rnn_w_i, g_w_i, m_rnn_w_i, v_rnn_w_i, (1, -1, RNN_BLOCK), 512),
        "rnn_w_out": adam_nd(rnn_w_out, g_w_out_r, m_rnn_w_out, v_rnn_w_out, rnn_w_out.shape, 256),
    }
    smalls = [rnn_conv_w, rnn_conv_b, rnn_b_a, rnn_b_i, rnn_lambda, ln_g, ln_b, attn_b_f]
    g_small = [g_conv_w, g_conv_b, g_b_a, g_b_i, g_lam, g_ln_g_sum, g_ln_b_sum, g_b_f]
    m_small = [m_rnn_conv_w, m_rnn_conv_b, m_rnn_b_a, m_rnn_b_i, m_rnn_lambda, m_ln_g, m_ln_b, m_attn_b_f]
    v_small = [v_rnn_conv_w, v_rnn_conv_b, v_rnn_b_a, v_rnn_b_i, v_rnn_lambda, v_ln_g, v_ln_b, v_attn_b_f]
    packs = [_pack(t, F32, 16)[None] for t in (smalls, g_small, m_small, v_small)]
    small_out = [_unpack(t[0], [w.shape for w in smalls]) for t in _adamw(*packs, 1, 16)]
    for k, name in enumerate(("rnn_conv_w", "rnn_conv_b", "rnn_b_a", "rnn_b_i", "rnn_lambda", "ln_g", "ln_b",
                              "attn_b_f")):
        upd[name] = [small_out[0][k], small_out[1][k], small_out[2][k]]
    grad = {"attn_w_in": g_w_in_a, "attn_w_out": g_w_out_a, "rnn_w_in": g_w_in_r, "rnn_w_a": g_w_a, "rnn_w_i": g_w_i,
            "rnn_w_out": g_w_out_r, "rnn_conv_w": g_conv_w, "rnn_conv_b": g_conv_b, "rnn_b_a": g_b_a,
            "rnn_b_i": g_b_i, "rnn_lambda": g_lam, "ln_g": g_ln_g_sum, "ln_b": g_ln_b_sum, "attn_b_f": g_b_f}
    names = ("ln_g", "ln_b", "attn_w_in", "attn_b_f", "attn_w_out", "rnn_w_in", "rnn_conv_w", "rnn_conv_b",
             "rnn_w_a", "rnn_b_a", "rnn_w_i", "rnn_b_i", "rnn_lambda", "rnn_w_out")
    return (loss, grad_x, *[grad[n] for n in names], *[upd[n][0] for n in names], *[upd[n][1] for n in names],
            *[upd[n][2] for n in names])
```

```python
import jax
import jax.numpy as jnp
from jax import lax
from jax.experimental import pallas as pl
from jax.experimental.pallas import tpu as pltpu

F32 = jnp.float32
BF16 = jnp.bfloat16
MESH = pl.DeviceIdType.MESH

DEPTH = 4
HEAD_DIM = 64
HEAD_PAIR = 2 * HEAD_DIM
RNN_BLOCK = 256
CONV_WIDTH = 4
LRU_C = 8.0
ALPHA = (2.0 * DEPTH) ** 0.25
LN_EPS = 1e-5
ADAM_LR, ADAM_B1, ADAM_B2, ADAM_EPS, ADAM_WD, ADAM_STEP = 0.001, 0.9, 0.999, 1e-08, 0.01, 10
N_CHIPS = 4
LANES = 1024
NEG = -1e30

NT_DIMS = (((1,), (1,)), ((), ()))
TN_DIMS = (((0,), (0,)), ((), ()))


def _sigmoid(z):
    return 1.0 / (1.0 + jnp.exp(-z))


def _softplus(z):
    return jnp.maximum(z, 0.0) + jnp.log(1.0 + jnp.exp(-jnp.abs(z)))


def _neg_expm1(y):
    poly = y * (1.0 + y * (0.5 + y * (1.0 / 6.0 + y * (1.0 / 24.0))))
    return -jnp.where(y > -0.05, poly, jnp.exp(y) - 1.0)


def _shift_down(cur, prev8, k):
    n = cur.shape[0]
    row = lax.broadcasted_iota(jnp.int32, cur.shape, 0)
    fix = jnp.tile(pltpu.roll(prev8, k, 0), (n // 8, 1))
    return jnp.where(row < k, fix, pltpu.roll(cur, k, 0))


def _shift_up(cur, next8, k):
    n = cur.shape[0]
    row = lax.broadcasted_iota(jnp.int32, cur.shape, 0)
    fix = jnp.tile(pltpu.roll(next8, 8 - k, 0), (n // 8, 1))
    return jnp.where(row >= n - k, fix, pltpu.roll(cur, n - k, 0))


def _proj(x, w, outs, name):
    s_len, d = x.shape
    tm = min(512, s_len)

    def body(x_ref, w_ref, *o_refs):
        xb = x_ref[...].astype(BF16)
        for (c0, wd, dt), o_ref in zip(outs, o_refs):
            step = min(wd, 512)
            for cc in range(0, wd, step):
                o_ref[:, cc:cc + step] = jnp.dot(
                    xb, w_ref[:, c0 + cc:c0 + cc + step], preferred_element_type=F32).astype(dt)

    return pl.pallas_call(
        body, name=name, grid=(s_len // tm,),
        in_specs=[pl.BlockSpec((tm, d), lambda i: (i, 0)), pl.BlockSpec(w.shape, lambda i: (0, 0))],
        out_specs=[pl.BlockSpec((tm, wd), lambda i: (i, 0)) for _, wd, _ in outs],
        out_shape=[jax.ShapeDtypeStruct((s_len, wd), dt) for _, wd, dt in outs],
        compiler_params=pltpu.CompilerParams(dimension_semantics=("parallel",)),
    )(x, w)


def _mm_nt(dz, pieces, w, name, outgoing=()):
    s_len, d = dz.shape
    tm = min(256, s_len)
    steps = s_len // tm
    n = len(pieces)
    ns = len(outgoing)
    cols = [(c0, p.shape[1]) for p, c0 in pieces]

    def body(*refs):
        dz_ref, p_refs, w_ref, o_ref = refs[0], refs[1:1 + n], refs[1 + n], refs[2 + n + ns]
        if ns:
            start, finish = _scatter_steps(refs[2 + n:2 + n + ns], refs[3 + n + ns:3 + n + 2 * ns],
                                           *refs[3 + n + 2 * ns:])
            pl.when(pl.program_id(0) == 0)(start)
        acc = ALPHA * dz_ref[...]
        for (c0, wd), p_ref in zip(cols, p_refs):
            acc = acc + lax.dot_general(p_ref[...], w_ref[:, c0:c0 + wd], NT_DIMS, preferred_element_type=F32)
        o_ref[...] = acc
        if ns:
            pl.when(pl.program_id(0) == steps - 1)(finish)

    out, *arrived = pl.pallas_call(
        body, name=name + "_scatter" if ns else name, grid=(steps,),
        in_specs=[pl.BlockSpec((tm, d), lambda i: (i, 0))]
        + [pl.BlockSpec((tm, wd), lambda i: (i, 0)) for _, wd in cols]
        + [pl.BlockSpec(w.shape, lambda i: (0, 0))] + [ANY] * ns,
        out_specs=[pl.BlockSpec((tm, d), lambda i: (i, 0))] + [ANY] * ns,
        out_shape=[jax.ShapeDtypeStruct((s_len, d), F32)] + [jax.ShapeDtypeStruct(t.shape, t.dtype) for t in outgoing],
        scratch_shapes=[pltpu.SemaphoreType.DMA((3 * ns,))] * 2 if ns else [],
        compiler_params=pltpu.CompilerParams(dimension_semantics=("arbitrary" if ns else "parallel",)),
    )(dz, *[p for p, _ in pieces], w, *outgoing)
    return (out, *arrived)


ROWS_ATTN_IN = (0, 2048)
ROWS_ATTN_OUT = (1024, 1280)
ROWS_RNN_OUT = (1536, 1792)
ROWS_RNN_IN = (3072, 3584)
ROWS_GATES = 4096
LATE_ROWS = 1280
SLOT_ROWS = 4352


DW_ROWS = 1024


def _dw_call(body, name, slots, operands, specs, out_block, out_index, grid, semantics):
    return pl.pallas_call(
        body, name=name, grid=grid,
        in_specs=specs if slots is None else [ANY] + specs,
        out_specs=pl.BlockSpec(out_block, out_index),
        out_shape=jax.ShapeDtypeStruct((N_CHIPS, SLOT_ROWS, LANES), F32),
        input_output_aliases={} if slots is None else {0: 0},
        compiler_params=pltpu.CompilerParams(dimension_semantics=semantics),
    )(*(operands if slots is None else [slots] + operands))


def _dw_attn_in(slots, x, pieces, forget, layer):
    s_len, d = x.shape
    ts = min(s_len, DW_ROWS)
    steps = s_len // ts
    half = d // 2

    def body(g_ref, x_ref, p0, p1, p2, p3, f_ref, o_ref, wf_ref):
        lanes, step = pl.program_id(0), pl.program_id(1)

        @pl.when(step == 0)
        def _():
            o_ref[...] = jnp.zeros_like(o_ref)

        xb = x_ref[...].astype(BF16)
        for j, p_ref in enumerate((p0, p1, p2, p3)):
            o_ref[j] += lax.dot_general(xb, p_ref[...], TN_DIMS, preferred_element_type=F32)

        @pl.when(step == steps - 1)
        def _():
            o_ref[0] = o_ref[0] * (HEAD_DIM ** -0.5)

        @pl.when(lanes == 0)
        def _():
            @pl.when(step == 0)
            def _():
                wf_ref[...] = jnp.zeros_like(wf_ref)

            wf_ref[...] += lax.dot_general(xb, f_ref[...], TN_DIMS, preferred_element_type=F32)

    return pl.pallas_call(
        body, name="dw_attn_in", grid=(2, steps),
        in_specs=[ANY, pl.BlockSpec((ts, d), lambda i, s: (s, 0))] + [pl.BlockSpec((ts, half), lambda i, s: (s, i))] * 4
        + [pl.BlockSpec((ts, 128), lambda i, s: (s, 0))],
        out_specs=[pl.BlockSpec((N_CHIPS, d, half), lambda i, s: (0, ROWS_ATTN_IN[layer] // d, i)),
                   pl.BlockSpec((d, 128), lambda i, s: (0, 0))],
        out_shape=[jax.ShapeDtypeStruct((N_CHIPS, SLOT_ROWS, LANES), F32), jax.ShapeDtypeStruct((d, 128), F32)],
        input_output_aliases={0: 0},
        compiler_params=pltpu.CompilerParams(dimension_semantics=("arbitrary", "arbitrary")),
    )(slots, x, *pieces, forget)


def _dw_out(slots, yg, dz, first_row):
    s_len, d = dz.shape
    ts = min(s_len, DW_ROWS)
    rows = d // N_CHIPS

    def body(*refs):
        a_ref, b_ref, o_ref = refs[-3:]

        @pl.when(pl.program_id(0) == 0)
        def _():
            o_ref[...] = jnp.zeros_like(o_ref)

        prod = lax.dot_general(a_ref[...], b_ref[...].astype(BF16), TN_DIMS, preferred_element_type=F32)
        o_ref[...] += prod.reshape(N_CHIPS, rows, d)

    specs = [pl.BlockSpec((ts, d), lambda s: (s, 0))] * 2
    return _dw_call(body, "dw_out", slots, [yg, dz], specs, (N_CHIPS, rows, d), lambda s: (0, first_row // rows, 0),
                    (s_len // ts,), ("arbitrary",))


def _dw_rnn_in(slots, x, parts, layer):
    s_len, d = x.shape
    ts = min(s_len, DW_ROWS)
    half = d // 2

    def body(*refs):
        x_ref, p_refs, o_ref = refs[-4], refs[-3:-1], refs[-1]

        @pl.when(pl.program_id(0) == 0)
        def _():
            o_ref[...] = jnp.zeros_like(o_ref)

        xb = x_ref[...].astype(BF16)
        for p, p_ref in enumerate(p_refs):
            for jj in (0, 1):
                for r in (0, 1):
                    o_ref[2 * p + jj, :, r * half:(r + 1) * half] += lax.dot_general(
                        xb[:, r * half:(r + 1) * half], p_ref[:, jj * half:(jj + 1) * half], TN_DIMS,
                        preferred_element_type=F32)

    specs = [pl.BlockSpec((ts, d), lambda s: (s, 0))] * 3
    return _dw_call(body, "dw_rnn_in", slots, [x] + list(parts), specs, (N_CHIPS, half, d),
                    lambda s: (0, ROWS_RNN_IN[layer] // half, 0), (s_len // ts,), ("arbitrary",))


def _fcum(fl, bias):
    s_len = fl.shape[0]

    def body(fl_ref, b_ref, cum_ref, sg_ref):
        z = fl_ref[...] + b_ref[...]
        e = jnp.exp(-jnp.abs(z))
        logf = jnp.minimum(z, 0.0) - jnp.log(1.0 + e)
        sneg = jnp.where(z >= 0, e, 1.0) / (1.0 + e)
        run = logf.T[0:16, :]
        sg_ref[...] = sneg.T[0:16, :]
        lane = lax.broadcasted_iota(jnp.int32, (16, s_len), 1)
        sh = 1
        while sh < s_len:
            run = run + jnp.where(lane >= sh, pltpu.roll(run, sh, 1), 0.0)
            sh *= 2
        cum_ref[...] = run

    return pl.pallas_call(
        body, name="fcum",
        out_shape=[jax.ShapeDtypeStruct((16, s_len), F32), jax.ShapeDtypeStruct((16, s_len), F32)],
    )(fl, bias)


def _fbwd(dcum, sneg):
    s_len = dcum.shape[1]

    def body(dc_ref, sg_ref, dl_ref, db_ref):
        run = dc_ref[...]
        lane = lax.broadcasted_iota(jnp.int32, (16, s_len), 1)
        sh = 1
        while sh < s_len:
            run = run + jnp.where(lane < s_len - sh, pltpu.roll(run, s_len - sh, 1), 0.0)
            sh *= 2
        dlog = run * sg_ref[...]
        db_ref[...] = jnp.broadcast_to(jnp.sum(dlog, axis=1, keepdims=True), (16, 128))
        full = jnp.concatenate([dlog, jnp.zeros((112, s_len), F32)], axis=0)
        dl_ref[...] = full.T.astype(BF16)

    return pl.pallas_call(
        body, name="fbwd",
        out_shape=[jax.ShapeDtypeStruct((s_len, 128), BF16), jax.ShapeDtypeStruct((16, 128), F32)],
    )(dcum, sneg)


FWD_TILE = 1024
BWD_TILE = 1024


def _flash_fwd(q, k, v, cum, shards=()):
    s_len, width = q.shape
    tile = min(FWD_TILE, s_len // 2)
    nt = s_len // tile
    reps = tile // 128
    cumr = cum.reshape(-1, tile)
    ns = len(shards)
    pairs = width // HEAD_PAIR

    def body(*refs):
        q_ref, k_ref, v_ref, cum_ref = refs[:4]
        o_ref, lse_ref = refs[4 + ns:6 + ns]
        q_t, v_t, cum_col = refs[6 + 2 * ns:9 + 2 * ns]
        pid = pl.program_id(0)
        if ns:
            start, forward, finish = _gather_steps(refs[4:4 + ns], refs[6 + ns:6 + 2 * ns], *refs[9 + 2 * ns:])
            pl.when(pid == 0)(start)
            pl.when(pid == pairs - 3)(forward)
        top = lax.broadcasted_iota(jnp.int32, (HEAD_PAIR, tile), 0) < HEAD_DIM
        causal = (lax.broadcasted_iota(jnp.int32, (tile, tile), 0)
                  <= lax.broadcasted_iota(jnp.int32, (tile, tile), 1))

        def pre(i, carry):
            r0 = pl.multiple_of(i * tile, tile)
            q_t[i] = q_ref[pl.ds(r0, tile), :].astype(F32).T.astype(BF16)
            v_t[i] = v_ref[pl.ds(r0, tile), :].astype(F32).T.astype(BF16)
            for h in (0, 1):
                row = cum_ref[pl.ds((2 * pid + h) * nt + i, 1), :]
                cum_col[h, pl.ds(r0, tile), :] = jnp.broadcast_to(row, (HEAD_PAIR, tile)).T
            return carry

        lax.fori_loop(0, nt, pre, 0)

        def q_loop(qi, carry):
            qt = q_t[qi]
            zt = jnp.zeros_like(qt)
            qms = (jnp.where(top, qt, zt), jnp.where(top, zt, qt))

            def step(kj, state, masked):
                m0, l0, m1, l1, acc = state
                k0 = pl.multiple_of(kj * tile, tile)
                kt = k_ref[pl.ds(k0, tile), :]
                vt = v_t[kj]
                ms, ls, scales, contrib = [m0, m1], [l0, l1], [], None
                for h in (0, 1):
                    s = jnp.dot(kt, qms[h], preferred_element_type=F32)
                    s = s - jnp.tile(cum_col[h, pl.ds(k0, tile), :], (1, reps))
                    if masked:
                        s = jnp.where(causal, s, NEG)
                    m_new = jnp.maximum(ms[h], jnp.max(s, axis=0, keepdims=True))
                    p = jnp.exp(s - m_new)
                    scale = jnp.exp(ms[h] - m_new)
                    ls[h] = scale * ls[h] + jnp.sum(p, axis=0, keepdims=True)
                    ms[h] = m_new
                    scales.append(scale)
                    vm = jnp.where(top, vt, zt) if h == 0 else jnp.where(top, zt, vt)
                    part = jnp.dot(vm, p.astype(BF16), preferred_element_type=F32)
                    contrib = part if contrib is None else contrib + part
                acc = jnp.where(top, scales[0], scales[1]) * acc + contrib
                return ms[0], ls[0], ms[1], ls[1], acc

            low = jnp.full((1, tile), NEG, F32)
            zero = jnp.zeros((1, tile), F32)
            state = step(qi, (low, zero, low, zero, jnp.zeros((HEAD_PAIR, tile), F32)), True)
            m0, l0, m1, l1, acc = lax.fori_loop(0, qi, lambda kj, c: step(kj, c, False), state)
            q0 = pl.multiple_of(qi * tile, tile)
            o_ref[pl.ds(q0, tile), :] = (acc / jnp.where(top, l0, l1)).T
            lse_ref[pl.ds((2 * pid) * nt + qi, 1), :] = m0 + jnp.log(l0)
            lse_ref[pl.ds((2 * pid + 1) * nt + qi, 1), :] = m1 + jnp.log(l1)
            return carry

        lax.fori_loop(0, nt, q_loop, 0)
        if ns:
            pl.when(pid == pairs - 1)(finish)

    blk = pl.BlockSpec((s_len, HEAD_PAIR), lambda p: (0, p))
    full = pl.BlockSpec(cumr.shape, lambda p: (0, 0))
    sems = [pltpu.SemaphoreType.DMA((GATHER_SEMS * ns,))] * 2 if ns else []
    o, lse, *gathered = pl.pallas_call(
        body, name="flash_fwd_gather" if ns else "flash_fwd", grid=(pairs,),
        in_specs=[blk, blk, blk, full] + [ANY] * ns,
        out_specs=[blk, full] + [ANY] * ns,
        out_shape=[jax.ShapeDtypeStruct((s_len, width), F32), jax.ShapeDtypeStruct(cumr.shape, F32)]
        + [jax.ShapeDtypeStruct((N_CHIPS,) + s.shape, s.dtype) for s in shards],
        scratch_shapes=[pltpu.VMEM((nt, HEAD_PAIR, tile), BF16), pltpu.VMEM((nt, HEAD_PAIR, tile), BF16),
                        pltpu.VMEM((2, s_len, HEAD_PAIR), F32)] + sems,
        compiler_params=pltpu.CompilerParams(dimension_semantics=("arbitrary",)),
    )(q, k, v, cumr, *shards)
    return (o, lse.reshape(cum.shape), *gathered)


def _flash_bwd(q, k, v, o, do, lse, cum, outgoing=()):
    s_len, width = q.shape
    tile = min(BWD_TILE, s_len // 2)
    nt = s_len // tile
    reps = tile // 128
    cumr = cum.reshape(-1, tile)
    lse = lse.reshape(-1, tile)
    ns = len(outgoing)
    pairs = width // HEAD_PAIR

    def body(*refs):
        q_ref, k_ref, v_ref, o_ref, do_ref, lse_ref, cum_ref = refs[:7]
        dq_ref, dk_ref, dv_ref, dcum_ref = refs[7 + ns:11 + ns]
        (q_t, do_t, k_t, do_bf, cum_col, dq_t_acc, delta, q_side, dk_acc, dv_acc,
         k_side) = refs[11 + 2 * ns:22 + 2 * ns]
        pid = pl.program_id(0)
        if ns:
            start, finish = _scatter_steps(refs[7:7 + ns], refs[11 + ns:11 + 2 * ns], *refs[22 + 2 * ns:])
            pl.when(pid == 0)(start)
        top = lax.broadcasted_iota(jnp.int32, (HEAD_PAIR, tile), 0) < HEAD_DIM
        left = lax.broadcasted_iota(jnp.int32, (tile, HEAD_PAIR), 1) < HEAD_DIM
        causal = (lax.broadcasted_iota(jnp.int32, (tile, tile), 0)
                  <= lax.broadcasted_iota(jnp.int32, (tile, tile), 1))

        def pre(i, carry):
            r0 = pl.multiple_of(i * tile, tile)
            d_o = do_ref[pl.ds(r0, tile), :]
            prod_t = (d_o * o_ref[pl.ds(r0, tile), :]).T
            delta[pl.ds(i, 1), :] = jnp.sum(prod_t[:HEAD_DIM], axis=0, keepdims=True)
            delta[pl.ds(nt + i, 1), :] = jnp.sum(prod_t[HEAD_DIM:], axis=0, keepdims=True)
            do_bf[pl.ds(r0, tile), :] = d_o.astype(BF16)
            do_t[i] = d_o.T.astype(BF16)
            q_t[i] = q_ref[pl.ds(r0, tile), :].astype(F32).T.astype(BF16)
            k_t[i] = k_ref[pl.ds(r0, tile), :].astype(F32).T.astype(BF16)
            dq_t_acc[i] = jnp.zeros((HEAD_PAIR, tile), F32)
            for h in (0, 1):
                row = cum_ref[pl.ds((2 * pid + h) * nt + i, 1), :]
                cum_col[h, pl.ds(r0, tile), :] = jnp.broadcast_to(row, (HEAD_PAIR, tile)).T
                q_side[pl.ds(h * nt + i, 1), :] = jnp.zeros((1, tile), F32)
            return carry

        lax.fori_loop(0, nt, pre, 0)

        def kv_loop(kj, carry):
            k0 = pl.multiple_of(kj * tile, tile)
            kt = k_ref[pl.ds(k0, tile), :]
            vt = v_ref[pl.ds(k0, tile), :]
            ktt = k_t[kj]
            zt = jnp.zeros_like(ktt)
            zr = jnp.zeros_like(kt)
            kms = (jnp.where(top, ktt, zt), jnp.where(top, zt, ktt))
            cols = [jnp.tile(cum_col[h, pl.ds(k0, tile), :], (1, reps)) for h in (0, 1)]
            dk_acc[...] = jnp.zeros_like(dk_acc)
            dv_acc[...] = jnp.zeros_like(dv_acc)
            k_side[...] = jnp.zeros_like(k_side)

            def step(qi, carry, masked):
                q0 = pl.multiple_of(qi * tile, tile)
                qtt = q_t[qi]
                dtt = do_t[qi]
                q_rows = q_ref[pl.ds(q0, tile), :]
                do_rows = do_bf[pl.ds(q0, tile), :]
                dq_part = None
                for h in (0, 1):
                    q_m = jnp.where(top, qtt, zt) if h == 0 else jnp.where(top, zt, qtt)
                    do_m = jnp.where(top, dtt, zt) if h == 0 else jnp.where(top, zt, dtt)
                    s = jnp.dot(kt, q_m, preferred_element_type=F32) - cols[h]
                    if masked:
                        s = jnp.where(causal, s, NEG)
                    p = jnp.exp(s - lse_ref[pl.ds((2 * pid + h) * nt + qi, 1), :])
                    dp = jnp.dot(vt, do_m, preferred_element_type=F32)
                    ds = p * (dp - delta[pl.ds(h * nt + qi, 1), :])
                    q_side[pl.ds(h * nt + qi, 1), :] += jnp.sum(ds, axis=0, keepdims=True)
                    folded = ds[:, :128]
                    for b in range(1, reps):
                        folded = folded + ds[:, b * 128:(b + 1) * 128]
                    k_side[h] += folded
                    pb = p.astype(BF16)
                    dsb = ds.astype(BF16)
                    do_h = jnp.where(left, do_rows, zr) if h == 0 else jnp.where(left, zr, do_rows)
                    q_h = jnp.where(left, q_rows, zr) if h == 0 else jnp.where(left, zr, q_rows)
                    dv_acc[...] += jnp.dot(pb, do_h, preferred_element_type=F32)
                    dk_acc[...] += jnp.dot(dsb, q_h, preferred_element_type=F32)
                    part = jnp.dot(kms[h], dsb, preferred_element_type=F32)
                    dq_part = part if dq_part is None else dq_part + part
                dq_t_acc[qi] += dq_part
                return carry

            step(kj, 0, True)
            lax.fori_loop(kj + 1, nt, lambda qi, c: step(qi, c, False), 0)
            dk_ref[pl.ds(k0, tile), :] = dk_acc[...].astype(BF16)
            dv_ref[pl.ds(k0, tile), :] = dv_acc[...].astype(BF16)
            for h in (0, 1):
                dcum_ref[pl.ds((2 * pid + h) * nt + kj, 1), :] = -jnp.sum(k_side[h].T, axis=0, keepdims=True)
            return carry

        lax.fori_loop(0, nt, kv_loop, 0)

        def fin(i, carry):
            r0 = pl.multiple_of(i * tile, tile)
            dq_ref[pl.ds(r0, tile), :] = dq_t_acc[i].T.astype(BF16)
            for h in (0, 1):
                dcum_ref[pl.ds((2 * pid + h) * nt + i, 1), :] += q_side[pl.ds(h * nt + i, 1), :]
            return carry

        lax.fori_loop(0, nt, fin, 0)
        if ns:
            pl.when(pid == pairs - 1)(finish)

    blk = pl.BlockSpec((s_len, HEAD_PAIR), lambda p: (0, p))
    full = pl.BlockSpec(cumr.shape, lambda p: (0, 0))
    transposed = pltpu.VMEM((nt, HEAD_PAIR, tile), BF16)
    sems = [pltpu.SemaphoreType.DMA((3 * ns,))] * 2 if ns else []
    dq, dk, dv, dcum, *arrived = pl.pallas_call(
        body, name="flash_bwd_scatter" if ns else "flash_bwd", grid=(pairs,),
        in_specs=[blk, blk, blk, blk, blk, full, full] + [ANY] * ns,
        out_specs=[blk, blk, blk, full] + [ANY] * ns,
        out_shape=[jax.ShapeDtypeStruct((s_len, width), BF16)] * 3 + [jax.ShapeDtypeStruct(cumr.shape, F32)]
        + [jax.ShapeDtypeStruct(t.shape, t.dtype) for t in outgoing],
        scratch_shapes=[transposed, transposed, transposed, pltpu.VMEM((s_len, HEAD_PAIR), BF16),
                        pltpu.VMEM((2, s_len, HEAD_PAIR), F32), pltpu.VMEM((nt, HEAD_PAIR, tile), F32),
                        pltpu.VMEM((2 * nt, tile), F32), pltpu.VMEM((2 * nt, tile), F32),
                        pltpu.VMEM((tile, HEAD_PAIR), F32), pltpu.VMEM((tile, HEAD_PAIR), F32),
                        pltpu.VMEM((2, tile, HEAD_PAIR), F32)] + sems,
        compiler_params=pltpu.CompilerParams(dimension_semantics=("arbitrary",)),
    )(q, k, v, o, do, lse, cumr, *outgoing)
    return (dq, dk, dv, dcum.reshape(cum.shape), *arrived)


def _out_ln(a, gate, x, w, g, b, name, target=None):
    s_len, d = x.shape
    tm = min(512, s_len)
    nt = 0 if target is None else 1

    def body(*refs):
        a_ref, gate_ref, x_ref, w_ref, g_ref, b_ref = refs[:6]
        first_ref, xh_ref, rs_ref, yg_ref = refs[6 + nt:10 + nt]
        gt = gate_ref[...]
        yg = (a_ref[...] * (gt * _sigmoid(gt))).astype(BF16)
        yg_ref[...] = yg
        z = ALPHA * x_ref[...] + jnp.dot(yg, w_ref[...], preferred_element_type=F32)
        zc = z - jnp.mean(z, axis=1, keepdims=True)
        rstd = lax.rsqrt(jnp.mean(zc * zc, axis=1, keepdims=True) + LN_EPS)
        xh = zc * rstd
        xh_ref[...] = xh
        y = xh * g_ref[...] + b_ref[...]
        rs_ref[...] = jnp.broadcast_to(rstd, (tm, 128))
        if nt:
            sq_ref = refs[10 + nt]

            @pl.when(pl.program_id(0) == 0)
            def _():
                sq_ref[...] = jnp.zeros_like(sq_ref)

            diff = y - refs[6][...]
            first_ref[...] = diff * (1.0 / d)
            sq_ref[...] += jnp.sum(diff * diff, axis=0, keepdims=True)
        else:
            first_ref[...] = y

    row = pl.BlockSpec((tm, d), lambda i: (i, 0))
    vec = pl.BlockSpec((1, d), lambda i: (0, 0))
    return pl.pallas_call(
        body, name=name + "_loss" if nt else name, grid=(s_len // tm,),
        in_specs=[row, row, row, pl.BlockSpec(w.shape, lambda i: (0, 0)), vec, vec] + [row] * nt,
        out_specs=[row, row, pl.BlockSpec((tm, 128), lambda i: (i, 0)), row] + [vec] * nt,
        out_shape=[jax.ShapeDtypeStruct((s_len, d), F32), jax.ShapeDtypeStruct((s_len, d), F32),
                   jax.ShapeDtypeStruct((s_len, 128), F32), jax.ShapeDtypeStruct((s_len, d), BF16)]
        + [jax.ShapeDtypeStruct((1, d), F32)] * nt,
        compiler_params=pltpu.CompilerParams(dimension_semantics=("arbitrary" if nt else "parallel",)),
    )(a, gate, x, w, g, b, *([target] if nt else []))


def _ln_bwd(dout, xh, rs, g, w, gate, a, name, swap=None):
    s_len, d = dout.shape
    tm = min(512, s_len)
    steps = s_len // tm
    ns = 0 if swap is None else 1

    def body(*refs):
        do_ref, xh_ref, rs_ref, g_ref, w_ref, gate_ref, a_ref = refs[:7]
        dz_ref, da_ref, dgate_ref, dg_ref, db_ref = refs[7 + ns:12 + ns]
        if ns:
            start, finish = _swap_steps(refs[7:8], refs[12 + ns:13 + ns], [swap[1]], *refs[13 + ns:])
            pl.when(pl.program_id(0) == 0)(start)

        @pl.when(pl.program_id(0) == 0)
        def _():
            dg_ref[...] = jnp.zeros_like(dg_ref)
            db_ref[...] = jnp.zeros_like(db_ref)

        dout_t = do_ref[...]
        xh_t = xh_ref[...]
        dg_ref[...] += jnp.sum(dout_t * xh_t, axis=0, keepdims=True)
        db_ref[...] += jnp.sum(dout_t, axis=0, keepdims=True)
        dxh = dout_t * g_ref[...]
        m1 = jnp.mean(dxh, axis=1, keepdims=True)
        m2 = jnp.mean(dxh * xh_t, axis=1, keepdims=True)
        dz = rs_ref[:, 0:1] * (dxh - m1 - xh_t * m2)
        dz_ref[...] = dz
        dyg = lax.dot_general(dz.astype(BF16), w_ref[...], NT_DIMS, preferred_element_type=F32)
        gt = gate_ref[...]
        sg = _sigmoid(gt)
        da_ref[...] = dyg * (gt * sg)
        dgate_ref[...] = (dyg * a_ref[...] * (sg * (1.0 + gt * (1.0 - sg)))).astype(BF16)
        if ns:
            pl.when(pl.program_id(0) == steps - 1)(finish)

    row = pl.BlockSpec((tm, d), lambda i: (i, 0))
    vec = pl.BlockSpec((1, d), lambda i: (0, 0))
    extra_out = [jax.ShapeDtypeStruct((N_CHIPS, swap[1][1], LANES), swap[0].dtype)] if ns else []
    return pl.pallas_call(
        body, name=name + "_swap" if ns else name, grid=(steps,),
        in_specs=[row, row, pl.BlockSpec((tm, 128), lambda i: (i, 0)), vec, pl.BlockSpec(w.shape, lambda i: (0, 0)),
                  row, row] + [ANY] * ns,
        out_specs=[row, row, row, vec, vec] + [ANY] * ns,
        out_shape=[jax.ShapeDtypeStruct((s_len, d), F32), jax.ShapeDtypeStruct((s_len, d), F32),
                   jax.ShapeDtypeStruct((s_len, d), BF16), jax.ShapeDtypeStruct((1, d), F32),
                   jax.ShapeDtypeStruct((1, d), F32)] + extra_out,
        scratch_shapes=[pltpu.SemaphoreType.DMA((N_CHIPS,))] * 2 if ns else [],
        compiler_params=pltpu.CompilerParams(dimension_semantics=("arbitrary",)),
    )(dout, xh, rs, g, w, gate, a, *([swap[0]] if ns else []))


def _rnn_fwd(u, conv_w, conv_b, wa, ba, wi, bi, lam):
    s_len, width = u.shape
    tm = min(512, s_len // 2)
    nb = width // RNN_BLOCK

    def body(u_ref, up_ref, cw_ref, cb_ref, wa_ref, ba_ref, wi_ref, bi_ref, lam_ref,
             h_ref, uc_ref, r_ref, i_ref, a_ref, b_scr, h_carry):
        step_id = pl.program_id(0)

        @pl.when(step_id == 0)
        def _():
            h_carry[...] = jnp.zeros_like(h_carry)

        for n in range(nb):
            blk = slice(n * RNN_BLOCK, (n + 1) * RNN_BLOCK)
            ut = u_ref[:, blk]
            prev = jnp.where(step_id > 0, up_ref[:, blk], 0.0)
            uc = cb_ref[:, blk] + cw_ref[3:4, blk] * ut
            for k in (1, 2, 3):
                uc = uc + cw_ref[3 - k:4 - k, blk] * _shift_down(ut, prev, k)
            ucb = uc.astype(BF16)
            r = _sigmoid(jnp.dot(ucb, wa_ref[n], preferred_element_type=F32) + ba_ref[:, blk])
            ig = _sigmoid(jnp.dot(ucb, wi_ref[n], preferred_element_type=F32) + bi_ref[:, blk])
            log_a = -LRU_C * r * _softplus(-lam_ref[:, blk])
            uc_ref[:, blk] = uc
            r_ref[:, blk] = r
            i_ref[:, blk] = ig
            a_ref[:, blk] = jnp.exp(log_a)
            b_scr[:, blk] = jnp.sqrt(_neg_expm1(2.0 * log_a)) * (ig * uc)

        def scan(t, h):
            h = a_ref[pl.ds(t, 1), :] * h + b_scr[pl.ds(t, 1), :]
            h_ref[pl.ds(t, 1), :] = h
            return h

        h_carry[...] = lax.fori_loop(0, tm, scan, h_carry[...], unroll=8)

    row = pl.BlockSpec((tm, width), lambda i: (i, 0))
    prev8 = pl.BlockSpec((8, width), lambda i: (jnp.maximum(i * (tm // 8) - 1, 0), 0))
    vec = pl.BlockSpec((1, width), lambda i: (0, 0))
    mat = pl.BlockSpec(wa.shape, lambda i: (0, 0, 0))
    return pl.pallas_call(
        body, name="rnn_fwd", grid=(s_len // tm,),
        in_specs=[row, prev8, pl.BlockSpec(conv_w.shape, lambda i: (0, 0)), vec, mat, vec, mat, vec, vec],
        out_specs=[row] * 5,
        out_shape=[jax.ShapeDtypeStruct((s_len, width), F32)] * 5,
        scratch_shapes=[pltpu.VMEM((tm, width), F32), pltpu.VMEM((1, width), F32)],
        compiler_params=pltpu.CompilerParams(dimension_semantics=("arbitrary",)),
    )(u, u, conv_w, conv_b, wa, ba, wi, bi, lam)


def _rnn_bwd(dh, a, h, r, ig, uc, lam, wa, wi):
    s_len, width = dh.shape
    tm = min(512, s_len // 2)
    nt = s_len // tm
    nb = width // RNN_BLOCK

    def body(dh_ref, a_ref, h_ref, hp_ref, r_ref, i_ref, uc_ref, lam_ref, wa_ref, wi_ref,
             duc_ref, dwa_ref, dwi_ref, dba_ref, dbi_ref, dlam_ref, g_scr, c_scr, dsp_scr):
        step_id = pl.program_id(0)
        tile_id = nt - 1 - step_id

        @pl.when(step_id == 0)
        def _():
            c_scr[...] = jnp.zeros_like(c_scr)
            dsp_scr[...] = jnp.zeros_like(dsp_scr)
            dwa_ref[...] = jnp.zeros_like(dwa_ref)
            dwi_ref[...] = jnp.zeros_like(dwi_ref)
            dba_ref[...] = jnp.zeros_like(dba_ref)
            dbi_ref[...] = jnp.zeros_like(dbi_ref)

        def scan(j, c):
            t = tm - 1 - j
            g = dh_ref[pl.ds(t, 1), :] + c
            g_scr[pl.ds(t, 1), :] = g
            return a_ref[pl.ds(t, 1), :] * g

        c_scr[...] = lax.fori_loop(0, tm, scan, c_scr[...], unroll=8)

        for n in range(nb):
            blk = slice(n * RNN_BLOCK, (n + 1) * RNN_BLOCK)
            g_t = g_scr[:, blk]
            hp8 = jnp.where(tile_id > 0, hp_ref[:, blk], 0.0)
            h_prev = _shift_down(h_ref[:, blk], hp8, 1)
            av, rv, iv, ucv = a_ref[:, blk], r_ref[:, blk], i_ref[:, blk], uc_ref[:, blk]
            sp = _softplus(-lam_ref[:, blk])
            mag = jnp.sqrt(_neg_expm1(-2.0 * LRU_C * rv * sp))
            d_iu = g_t * mag
            dla = g_t * h_prev * av - g_t * (iv * ucv) * (av * av) / mag
            dsp_scr[:, blk] += jnp.sum(dla * (-LRU_C * rv), axis=0, keepdims=True)
            dra = dla * (-LRU_C * sp) * rv * (1.0 - rv)
            dia = d_iu * ucv * iv * (1.0 - iv)
            drab, diab, ucb = dra.astype(BF16), dia.astype(BF16), ucv.astype(BF16)
            duc_ref[:, blk] = (d_iu * iv
                               + lax.dot_general(drab, wa_ref[n], NT_DIMS, preferred_element_type=F32)
                               + lax.dot_general(diab, wi_ref[n], NT_DIMS, preferred_element_type=F32))
            dwa_ref[n] += lax.dot_general(ucb, drab, TN_DIMS, preferred_element_type=F32)
            dwi_ref[n] += lax.dot_general(ucb, diab, TN_DIMS, preferred_element_type=F32)
            dba_ref[:, blk] += jnp.sum(dra, axis=0, keepdims=True)
            dbi_ref[:, blk] += jnp.sum(dia, axis=0, keepdims=True)

        @pl.when(step_id == nt - 1)
        def _():
            dlam_ref[...] = -dsp_scr[...] * _sigmoid(-lam_ref[...])

    row = pl.BlockSpec((tm, width), lambda i: (nt - 1 - i, 0))
    prev8 = pl.BlockSpec((8, width), lambda i: (jnp.maximum((nt - 1 - i) * (tm // 8) - 1, 0), 0))
    vec = pl.BlockSpec((1, width), lambda i: (0, 0))
    mat = pl.BlockSpec(wa.shape, lambda i: (0, 0, 0))
    return pl.pallas_call(
        body, name="rnn_bwd", grid=(nt,),
        in_specs=[row, row, row, prev8, row, row, row, vec, mat, mat],
        out_specs=[row, mat, mat, vec, vec, vec],
        out_shape=[jax.ShapeDtypeStruct((s_len, width), F32), jax.ShapeDtypeStruct(wa.shape, F32),
                   jax.ShapeDtypeStruct(wa.shape, F32)] + [jax.ShapeDtypeStruct((1, width), F32)] * 3,
        scratch_shapes=[pltpu.VMEM((tm, width), F32), pltpu.VMEM((1, width), F32), pltpu.VMEM((1, width), F32)],
        compiler_params=pltpu.CompilerParams(dimension_semantics=("arbitrary",)),
    )(dh, a, h, h, r, ig, uc, lam, wa, wi)


def _conv_bwd(duc, u, conv_w):
    s_len, width = duc.shape
    tm = min(256, s_len)
    nt = s_len // tm

    def body(d_ref, dn_ref, u_ref, up_ref, cw_ref, du_ref, dcw_ref, dcb_ref):
        step_id = pl.program_id(0)

        @pl.when(step_id == 0)
        def _():
            dcw_ref[...] = jnp.zeros_like(dcw_ref)
            dcb_ref[...] = jnp.zeros_like(dcb_ref)

        for n in range(width // RNN_BLOCK):
            blk = slice(n * RNN_BLOCK, (n + 1) * RNN_BLOCK)
            dt = d_ref[:, blk]
            ut = u_ref[:, blk]
            nxt = jnp.where(step_id < nt - 1, dn_ref[:, blk], 0.0)
            prev = jnp.where(step_id > 0, up_ref[:, blk], 0.0)
            du = cw_ref[3:4, blk] * dt
            dcw_ref[3:4, blk] += jnp.sum(dt * ut, axis=0, keepdims=True)
            for k in (1, 2, 3):
                du = du + cw_ref[3 - k:4 - k, blk] * _shift_up(dt, nxt, k)
                dcw_ref[3 - k:4 - k, blk] += jnp.sum(dt * _shift_down(ut, prev, k), axis=0, keepdims=True)
            du_ref[:, blk] = du.astype(BF16)
            dcb_ref[:, blk] += jnp.sum(dt, axis=0, keepdims=True)

    row = pl.BlockSpec((tm, width), lambda i: (i, 0))
    prev8 = pl.BlockSpec((8, width), lambda i: (jnp.maximum(i * (tm // 8) - 1, 0), 0))
    next8 = pl.BlockSpec((8, width), lambda i: (jnp.minimum((i + 1) * (tm // 8), s_len // 8 - 1), 0))
    return pl.pallas_call(
        body, name="conv_bwd", grid=(nt,),
        in_specs=[row, next8, row, prev8, pl.BlockSpec(conv_w.shape, lambda i: (0, 0))],
        out_specs=[row, pl.BlockSpec(conv_w.shape, lambda i: (0, 0)), pl.BlockSpec((1, width), lambda i: (0, 0))],
        out_shape=[jax.ShapeDtypeStruct((s_len, width), BF16), jax.ShapeDtypeStruct(conv_w.shape, F32),
                   jax.ShapeDtypeStruct((1, width), F32)],
        compiler_params=pltpu.CompilerParams(dimension_semantics=("arbitrary",)),
    )(duc, duc, u, u, conv_w)


def _pair_sum(core, grads, theirs, first_row, out_dtype):
    n, half, _ = theirs.shape
    tb = 128 if half % 128 == 0 and first_row % 128 == 0 else half
    assert first_row % tb == 0 and half % tb == 0

    def body(c_ref, a_ref, b_ref, o_ref):
        o_ref[...] = (a_ref[...] + b_ref[...]).astype(out_dtype)

    blk = pl.BlockSpec((n, tb, LANES), lambda i, c: (0, i, 0))
    mine = pl.BlockSpec((n, tb, LANES), lambda i, c: (0, first_row // tb + c[0] * (half // tb) + i, 0))
    return pl.pallas_call(
        body, name="pair_sum",
        grid_spec=pltpu.PrefetchScalarGridSpec(
            num_scalar_prefetch=1, grid=(half // tb,), in_specs=[mine, blk], out_specs=blk),
        out_shape=jax.ShapeDtypeStruct((n, half, LANES), out_dtype),
        compiler_params=pltpu.CompilerParams(dimension_semantics=("parallel",)),
    )(core, grads, theirs)


def _sum_chips(parts):
    rows = parts.shape[1]
    tb = 128 if rows % 128 == 0 else rows

    def body(p_ref, o_ref):
        o_ref[...] = ((p_ref[0].astype(F32) + p_ref[1].astype(F32)) + p_ref[2].astype(F32)) + p_ref[3].astype(F32)

    return pl.pallas_call(
        body, name="chip_sum", grid=(rows // tb,),
        in_specs=[pl.BlockSpec((N_CHIPS, tb, LANES), lambda i: (0, i, 0))],
        out_specs=pl.BlockSpec((tb, LANES), lambda i: (i, 0)),
        out_shape=jax.ShapeDtypeStruct((rows, LANES), F32),
        compiler_params=pltpu.CompilerParams(dimension_semantics=("parallel",)),
    )(parts)


def _adamw(w, g, m, v, lead_per_block, rows_per_block):
    n, rows, cols = w.shape
    blk = pl.BlockSpec((lead_per_block, rows_per_block, cols), lambda i, j: (i, j, 0))

    def body(w_ref, g_ref, m_ref, v_ref, d_ref, nm_ref, nv_ref):
        gt = g_ref[...]
        nm = ADAM_B1 * m_ref[...] + (1.0 - ADAM_B1) * gt
        nv = ADAM_B2 * v_ref[...] + (1.0 - ADAM_B2) * (gt * gt)
        m_hat = nm / (1.0 - ADAM_B1 ** ADAM_STEP)
        v_hat = nv / (1.0 - ADAM_B2 ** ADAM_STEP)
        d_ref[...] = -ADAM_LR * (m_hat / (jnp.sqrt(v_hat) + ADAM_EPS) + ADAM_WD * w_ref[...])
        nm_ref[...] = nm
        nv_ref[...] = nv

    return pl.pallas_call(
        body, name="adamw", grid=(n // lead_per_block, rows // rows_per_block), in_specs=[blk] * 4,
        out_specs=[blk] * 3,
        out_shape=[jax.ShapeDtypeStruct(w.shape, F32)] * 3,
        compiler_params=pltpu.CompilerParams(dimension_semantics=("parallel", "parallel")),
    )(w, g, m, v)


def _place():
    x, y, c = lax.axis_index("x"), lax.axis_index("y"), lax.axis_index("c")
    return x, y, c, [(1 - x, y), (x, 1 - y), (1 - x, 1 - y)]


ANY = pl.BlockSpec(memory_space=pl.ANY)
COPY_PARTS = 4


def _remote_parts(src_of, dst_of, rows, send_sem, recv_sem, to, begin=True):
    parts = COPY_PARTS if rows % (16 * COPY_PARTS) == 0 else 1
    step = rows // parts
    for i in range(parts if begin is not False else 0):
        pltpu.make_async_remote_copy(src_ref=src_of(i * step, step), dst_ref=dst_of(i * step, step),
                                     send_sem=send_sem, recv_sem=recv_sem, device_id=to, device_id_type=MESH).start()
    if begin == "only":
        return None
    return pltpu.make_async_remote_copy(src_ref=src_of(0, rows), dst_ref=dst_of(0, rows), send_sem=send_sem,
                                        recv_sem=recv_sem, device_id=to, device_id_type=MESH)


GATHER_SEMS = 7


def _gather_steps(p_refs, o_refs, send_sems, recv_sems):
    x, y, c, chips = _place()
    me = 2 * x + y

    def half(ref, which):
        rows = ref.shape[0] // 2
        return ref.at[pl.ds(which * rows, rows)]

    def copy(k, src, dst, to):
        return pltpu.make_async_remote_copy(src_ref=src, dst_ref=dst, send_sem=send_sems.at[k],
                                            recv_sem=recv_sems.at[k], device_id=to, device_id_type=MESH)

    def first_copies():
        out = []
        for b, (p_ref, o_ref) in enumerate(zip(p_refs, o_refs)):
            for j, (cx, cy) in enumerate(chips):
                out.append(copy(GATHER_SEMS * b + j, half(p_ref, c), half(o_ref.at[me], c), (cx, cy, c)))
            out.append(copy(GATHER_SEMS * b + 6, p_ref, o_ref.at[me], (x, y, 1 - c)))
        return out

    def passed_copies():
        out = []
        for b, o_ref in enumerate(o_refs):
            for j, (cx, cy) in enumerate(chips):
                landed = half(o_ref.at[2 * cx + cy], c)
                out.append(copy(GATHER_SEMS * b + 3 + j, landed, landed, (x, y, 1 - c)))
        return out

    def start():
        for cp in first_copies():
            cp.start()

    def forward():
        for b, o_ref in enumerate(o_refs):
            for j, (cx, cy) in enumerate(chips):
                landed = half(o_ref.at[2 * cx + cy], c)
                copy(GATHER_SEMS * b + j, landed, landed, (x, y, c)).wait_recv()
        for cp in passed_copies():
            cp.start()

    def finish():
        for b, o_ref in enumerate(o_refs):
            for j, (cx, cy) in enumerate(chips):
                other = half(o_ref.at[2 * cx + cy], 1 - c)
                copy(GATHER_SEMS * b + 3 + j, other, other, (x, y, c)).wait_recv()
            copy(GATHER_SEMS * b + 6, o_ref.at[me], o_ref.at[me], (x, y, c)).wait_recv()
        for cp in first_copies() + passed_copies():
            cp.wait_send()

    return start, forward, finish


def _gather_chips(shards):
    nb = len(shards)

    def body(*refs):
        for step in _gather_steps(refs[:nb], refs[nb:2 * nb], *refs[2 * nb:]):
            step()

    return pl.pallas_call(
        body, name="gather_chips", in_specs=[ANY] * nb, out_specs=[ANY] * nb,
        out_shape=[jax.ShapeDtypeStruct((N_CHIPS,) + s.shape, s.dtype) for s in shards],
        scratch_shapes=[pltpu.SemaphoreType.DMA((GATHER_SEMS * nb,)), pltpu.SemaphoreType.DMA((GATHER_SEMS * nb,))],
    )(*shards)


def _swap_steps(g_refs, t_refs, spans, send_sems, recv_sems):
    x, y, c, _ = _place()

    def gives(begin):
        return [_remote_parts(
            lambda r0, m, g_ref=g_ref, j=j, base=first_row + (1 - c) * half: g_ref.at[j, pl.ds(base + r0, m), :],
            lambda r0, m, t_ref=t_ref, j=j: t_ref.at[j, pl.ds(r0, m), :],
            half, send_sems.at[b * N_CHIPS + j], recv_sems.at[b * N_CHIPS + j], (x, y, 1 - c), begin)
            for b, (g_ref, t_ref, (first_row, half)) in enumerate(zip(g_refs, t_refs, spans)) for j in range(N_CHIPS)]

    def start():
        gives("only")

    def finish():
        for give in gives(False):
            give.wait()

    return start, finish


def _swap_halves(bufs, spans):
    nb = len(bufs)

    def body(*refs):
        for step in _swap_steps(refs[:nb], refs[nb:2 * nb], spans, *refs[2 * nb:]):
            step()

    return pl.pallas_call(
        body, name="swap_halves", in_specs=[ANY] * nb, out_specs=[ANY] * nb,
        out_shape=[jax.ShapeDtypeStruct((b.shape[0], half, b.shape[2]), b.dtype) for b, (_, half) in zip(bufs, spans)],
        scratch_shapes=[pltpu.SemaphoreType.DMA((nb * N_CHIPS,)), pltpu.SemaphoreType.DMA((nb * N_CHIPS,))],
    )(*bufs)


def _scatter_steps(t_refs, o_refs, send_sems, recv_sems):
    x, y, c, chips = _place()
    me = 2 * x + y

    def slot(ref, chip):
        return lambda r0, n: ref.at[chip, pl.ds(r0, n), :]

    def sends(begin):
        return [_remote_parts(slot(t_ref, 2 * cx + cy), slot(o_ref, me), t_ref.shape[1], send_sems.at[3 * b + j],
                              recv_sems.at[3 * b + j], (cx, cy, c), begin)
                for b, (t_ref, o_ref) in enumerate(zip(t_refs, o_refs)) for j, (cx, cy) in enumerate(chips)]

    def start():
        sends("only")

    def finish():
        for b, o_ref in enumerate(o_refs):
            for j, (cx, cy) in enumerate(chips):
                landed = o_ref.at[2 * cx + cy]
                pltpu.make_async_remote_copy(src_ref=landed, dst_ref=landed, send_sem=send_sems.at[3 * b + j],
                                             recv_sem=recv_sems.at[3 * b + j], device_id=(x, y, c),
                                             device_id_type=MESH).wait_recv()
        for cp in sends(False):
            cp.wait_send()

    return start, finish


def _give_sibling(bufs):
    nb = len(bufs)

    def body(*refs):
        h_refs, o_refs, (send_sems, recv_sems) = refs[:nb], refs[nb:2 * nb], refs[2 * nb:]
        x, y, c, _ = _place()
        gives = [_remote_parts(lambda r0, n, h_ref=h_ref: h_ref.at[pl.ds(r0, n), :],
                               lambda r0, n, o_ref=o_ref: o_ref.at[pl.ds(r0, n), :],
                               h_ref.shape[0], send_sems.at[b], recv_sems.at[b], (x, y, 1 - c))
                 for b, (h_ref, o_ref) in enumerate(zip(h_refs, o_refs))]
        for give in gives:
            give.wait()

    return pl.pallas_call(
        body, name="give_sibling", in_specs=[ANY] * nb, out_specs=[ANY] * nb,
        out_shape=[jax.ShapeDtypeStruct(b.shape, b.dtype) for b in bufs],
        scratch_shapes=[pltpu.SemaphoreType.DMA((nb,)), pltpu.SemaphoreType.DMA((nb,))],
    )(*bufs)


def _pack(arrays, dtype, row_align):
    flat = []
    for arr in arrays:
        v = arr.reshape(-1)
        flat.append(jnp.pad(v, (0, (-v.shape[0]) % LANES)))
    total = sum(f.shape[0] for f in flat) // LANES
    pad_rows = (-total) % row_align
    if pad_rows:
        flat.append(jnp.zeros((pad_rows * LANES,), dtype))
    return jnp.concatenate(flat).reshape(-1, LANES)


def _unpack(buf, shapes, rows_align=1):
    lead = buf.shape[:-2]
    flat = buf.reshape(lead + (-1,))
    out, off = [], 0
    for shape in shapes:
        size = 1
        for dim in shape:
            size *= dim
        out.append(flat[..., off:off + size].reshape(lead + tuple(shape)))
        off += size + (-size) % (LANES * rows_align)
    return out


def _from_chips(parts, axis):
    return jnp.concatenate([parts[j] for j in range(N_CHIPS)], axis=axis)


def kernel(x, ln_g, ln_b, attn_w_in, attn_b_f, attn_w_out, rnn_w_in, rnn_conv_w, rnn_conv_b, rnn_w_a, rnn_b_a, rnn_w_i, rnn_b_i, rnn_lambda, rnn_w_out, loss_target, m_ln_g, m_ln_b, m_attn_w_in, m_attn_b_f, m_attn_w_out, m_rnn_w_in, m_rnn_conv_w, m_rnn_conv_b, m_rnn_w_a, m_rnn_b_a, m_rnn_w_i, m_rnn_b_i, m_rnn_lambda, m_rnn_w_out, v_ln_g, v_ln_b, v_attn_w_in, v_attn_b_f, v_attn_w_out, v_rnn_w_in, v_rnn_conv_w, v_rnn_conv_b, v_rnn_w_a, v_rnn_b_a, v_rnn_w_i, v_rnn_b_i, v_rnn_lambda, v_rnn_w_out):
    s_len, d = x.shape[1], x.shape[2]
    xs = x.reshape(s_len, d)
    target = loss_target.reshape(s_len, d)
    heads = attn_b_f.shape[1]
    qkvg = attn_w_in.shape[2] * N_CHIPS - heads

    me = 2 * lax.axis_index("x") + lax.axis_index("y")
    core = lax.axis_index("c").astype(jnp.int32)
    small = [rnn_conv_w, rnn_conv_b, rnn_b_a, rnn_b_i, rnn_lambda]
    a_in, a_out = attn_w_in.astype(BF16), attn_w_out.astype(BF16)
    later = [a_in[1], a_out] + [w.astype(BF16) for w in (rnn_w_in, rnn_w_a, rnn_w_i, rnn_w_out)]
    got_in, got_small = _gather_chips([a_in[0], _pack(small, F32, 16)])
    conv_w, conv_b, b_a, b_i, lam = [
        _from_chips(p, ax) for p, ax in zip(_unpack(got_small, [w.shape for w in small]), (2, 1, 1, 1, 1))]

    def attn_in_weights(w_in):
        shard = w_in.shape[2]

        def columns(first, last):
            return [w_in[j][:, max(first - j * shard, 0):min(last - j * shard, shard)]
                    for j in range(N_CHIPS) if first < (j + 1) * shard and last > j * shard]

        return jnp.concatenate([t * (HEAD_DIM ** -0.5) for t in columns(0, d)] + columns(d, qkvg + heads)
                               + [jnp.zeros((d, 128 - heads), BF16)], axis=1)

    w_cat = [attn_in_weights(got_in), None]
    b_f = jnp.pad(attn_b_f, ((0, 0), (0, 128 - heads)))

    saved = []
    cur = xs
    for layer in range(DEPTH):
        idx = layer // 2
        g_l, b_l = ln_g[layer:layer + 1], ln_b[layer:layer + 1]
        goal = target if layer == DEPTH - 1 else None
        if layer % 2 == 0:
            q, k, v, gate, fl = _proj(cur, w_cat[idx], [(0, d, BF16), (d, d, BF16), (2 * d, d, BF16),
                                                         (3 * d, d, F32), (4 * d, 128, F32)], "attn_proj")
            cum, sneg = _fcum(fl, b_f[idx:idx + 1])
            o, lse, *rest = _flash_fwd(q, k, v, cum, later if layer == 0 else ())
            if layer == 0:
                w_cat[1] = attn_in_weights(rest[0])
                w_out_a = jnp.moveaxis(rest[1], 0, 1).reshape(2, d, d)
                w_in_r = jnp.moveaxis(rest[2], 0, 2).reshape(2, d, 2 * d)
                w_a = jnp.transpose(rest[3], (1, 2, 0, 3, 4)).reshape(2, -1, RNN_BLOCK, RNN_BLOCK)
                w_i = jnp.transpose(rest[4], (1, 2, 0, 3, 4)).reshape(2, -1, RNN_BLOCK, RNN_BLOCK)
                w_out_r = jnp.moveaxis(rest[5], 0, 1).reshape(2, d, d)
            nxt, xh, rs, yg, *sq = _out_ln(o, gate, cur, w_out_a[idx], g_l, b_l, "out_ln", goal)
            saved.append(dict(x=cur, q=q, k=k, v=v, gate=gate, cum=cum, sneg=sneg, a=o, lse=lse, xh=xh, rs=rs, yg=yg))
        else:
            u, gate = _proj(cur, w_in_r[idx], [(0, d, F32), (d, d, F32)], "rnn_proj")
            vecs = [t[idx:idx + 1] for t in (conv_b, b_a, b_i, lam)]
            hseq, uc, r, ig, a = _rnn_fwd(u, conv_w[idx], vecs[0], w_a[idx], vecs[1], w_i[idx], vecs[2], vecs[3])
            nxt, xh, rs, yg, *sq = _out_ln(hseq, gate, cur, w_out_r[idx], g_l, b_l, "out_ln", goal)
            saved.append(dict(x=cur, u=u, gate=gate, uc=uc, r=r, ig=ig, av=a, a=hseq, xh=xh, rs=rs, yg=yg, lam=vecs[3]))
        cur = nxt

    dout = cur
    loss_here = 0.5 * jnp.sum(sq[0]) / d

    g_ln_g, g_ln_b = [None] * DEPTH, [None] * DEPTH
    g_attn = [None] * 2
    g_rnn = [None] * 2
    slots = None
    core1 = core.reshape(1)
    late_half = (SLOT_ROWS - LATE_ROWS) // 2

    def by_chip(t, axis):
        shape = t.shape[:axis] + (N_CHIPS, t.shape[axis] // N_CHIPS) + t.shape[axis + 1:]
        flat = jnp.moveaxis(t.reshape(shape), axis, 0).reshape(N_CHIPS, -1)
        return jnp.pad(flat, ((0, 0), (0, (-flat.shape[1]) % (8 * LANES)))).reshape(N_CHIPS, -1, LANES)

    def both(key):
        return jnp.stack([it[key] for it in g_rnn])

    for layer in reversed(range(DEPTH)):
        idx = layer // 2
        sv = saved[layer]
        swap = None
        if layer == 0:
            gates = jnp.concatenate([by_chip(both("w_a"), 2), by_chip(both("w_i"), 2)], axis=1)
            slots = lax.dynamic_update_slice(slots, gates, (0, ROWS_GATES, 0))
            swap = (slots, (LATE_ROWS, late_half))
        w_out = w_out_a[idx] if layer % 2 == 0 else w_out_r[idx]
        dz, da, dgate, dg, db, *theirs_late = _ln_bwd(dout, sv["xh"], sv["rs"], ln_g[layer:layer + 1], w_out,
                                                      sv["gate"], sv["a"], "ln_bwd", swap)
        if layer == 0:
            pair_late = _pair_sum(core1, slots, theirs_late[0], LATE_ROWS, BF16)
        g_ln_g[layer], g_ln_b[layer] = dg, db
        slots = _dw_out(slots, sv["yg"], dz, (ROWS_ATTN_OUT if layer % 2 == 0 else ROWS_RNN_OUT)[idx])
        if layer % 2 == 0:
            dq, dk, dv, dcum, *arrived = _flash_bwd(sv["q"], sv["k"], sv["v"], sv["a"], da, sv["lse"], sv["cum"],
                                                    [pair_late] if layer == 0 else ())
            dlogit, dbf = _fbwd(dcum, sv["sneg"])
            pieces = [dq, dk, dv, dgate, dlogit]
            if layer > 0:
                dout, = _mm_nt(dz, [(p, j * d) for j, p in enumerate(pieces)], w_cat[idx], "attn_dx")
            slots, d_w_f = _dw_attn_in(slots, sv["x"], pieces[:4], dlogit, idx)
            g_attn[idx] = dict(w_f=d_w_f[:, :heads], b_f=dbf[:, 0])
        else:
            duc, d_wa, d_wi, d_ba, d_bi, d_lam = _rnn_bwd(da, sv["av"], sv["a"], sv["r"], sv["ig"], sv["uc"], sv["lam"],
                                                          w_a[idx], w_i[idx])
            du, d_cw, d_cb = _conv_bwd(duc, sv["u"], conv_w[idx])
            dout, = _mm_nt(dz, [(du, 0), (dgate, d)], w_in_r[idx], "rnn_dx")
            slots = _dw_rnn_in(slots, sv["x"], (du, dgate), idx)
            g_rnn[idx] = dict(conv_w=d_cw, conv_b=d_cb[0], w_a=d_wa, b_a=d_ba[0], w_i=d_wi, b_i=d_bi[0], lam=d_lam[0])

    def everywhere(t):
        flat = t.reshape(-1)
        flat = jnp.pad(flat, (0, (-flat.shape[0]) % (8 * LANES))).reshape(-1, LANES)
        return jnp.broadcast_to(flat[None], (N_CHIPS,) + flat.shape)

    in_rows = jnp.stack([slots[1:, r:r + d, :heads] for r in ROWS_ATTN_IN], axis=1)
    edges = jnp.concatenate([in_rows, jnp.stack([it["w_f"] for it in g_attn])[None]])
    rows = [everywhere(edges), by_chip(both("conv_w"), 2),
            by_chip(both("conv_b"), 1), by_chip(both("b_a"), 1), by_chip(both("b_i"), 1), by_chip(both("lam"), 1),
            everywhere(jnp.concatenate(g_ln_g)), everywhere(jnp.concatenate(g_ln_b)),
            everywhere(jnp.stack([it["b_f"] for it in g_attn])), everywhere(loss_here)]
    used = sum(r.shape[1] for r in rows)
    small = jnp.concatenate(rows + [jnp.zeros((N_CHIPS, (-used) % 32, LANES), F32)], axis=1)

    spans = [(0, LATE_ROWS // 2), (0, small.shape[1] // 2)]
    theirs = _swap_halves([slots, small], spans)
    pair = [_pair_sum(core1, slots, theirs[0], 0, BF16), _pair_sum(core1, small, theirs[1], 0, F32)]
    dout, *arrived_last = _mm_nt(dz, [(p, j * d) for j, p in enumerate(pieces)], w_cat[0], "attn_dx", pair)
    grad_x = dout.reshape(x.shape)
    summed = []
    for mine_all, got in zip([pair_late] + pair, list(arrived) + arrived_last):
        own = lax.dynamic_index_in_dim(mine_all, me, 0, keepdims=False)
        summed.append(_sum_chips(lax.dynamic_update_index_in_dim(got, own, me, 0)))
    late, early, small_sum = [
        jnp.concatenate([jnp.where(core == 0, mine, other), jnp.where(core == 0, other, mine)])
        for mine, other in zip(summed, _give_sibling(summed))]

    def rows_at(first, n):
        return early[first:first + n] if first < LATE_ROWS else late[first - LATE_ROWS:first - LATE_ROWS + n]

    g_in_group = jnp.stack([rows_at(r, d) for r in ROWS_ATTN_IN])
    g_w_out_a = jnp.stack([rows_at(r, d // N_CHIPS) for r in ROWS_ATTN_OUT])
    g_w_out_r = jnp.stack([rows_at(r, d // N_CHIPS) for r in ROWS_RNN_OUT])
    folded = jnp.stack([rows_at(r, d // 2) for r in ROWS_RNN_IN])
    g_w_in_r = jnp.concatenate([folded[:, :, :d // 2], folded[:, :, d // 2:]], axis=1)
    gate_rows = rnn_w_a.size // LANES
    g_w_a = rows_at(ROWS_GATES, gate_rows).reshape(rnn_w_a.shape)
    g_w_i = rows_at(ROWS_GATES + gate_rows, gate_rows).reshape(rnn_w_i.shape)
    (g_edges, g_conv_w, g_conv_b, g_b_a, g_b_i, g_lam, g_ln_g_sum, g_ln_b_sum, g_b_f,
     loss) = _unpack(small_sum, [
        (N_CHIPS, 2, d, heads), rnn_conv_w.shape, rnn_conv_b.shape, rnn_b_a.shape,
        rnn_b_i.shape, rnn_lambda.shape, ln_g.shape, ln_b.shape, attn_b_f.shape, ()], rows_align=8)
    wide = jnp.concatenate([g_in_group, lax.dynamic_index_in_dim(g_edges, me, 0, keepdims=False)], axis=2)
    g_w_in_a = lax.dynamic_slice(wide, (0, 0, (heads // N_CHIPS) * me), attn_w_in.shape)

    def adam_nd(w, g, m, v, view, rows_per_block):
        outs = _adamw(w.reshape(view), g.reshape(view), m.reshape(view), v.reshape(view), 1, rows_per_block)
        return [t.reshape(w.shape) for t in outs]

    turned = [jnp.transpose(t, (2, 0, 1)) for t in (attn_w_in, g_w_in_a, m_attn_w_in, v_attn_w_in)]
    upd = {
        "attn_w_in": [jnp.transpose(t, (1, 2, 0)) for t in _adamw(*turned, attn_w_in.shape[2] // N_CHIPS, 2)],
        "attn_w_out": adam_nd(attn_w_out, g_w_out_a, m_attn_w_out, v_attn_w_out, attn_w_out.shape, 256),
        "rnn_w_in": adam_nd(rnn_w_in, g_w_in_r, m_rnn_w_in, v_rnn_w_in, rnn_w_in.shape, 512),
        "rnn_w_a": adam_nd(rnn_w_a, g_w_a, m_rnn_w_a, v_rnn_w_a, (1, -1, RNN_BLOCK), 512),
        "rnn_w_i": adam_nd(rnn_w_i, g_w_i, m_rnn_w_i, v_rnn_w_i, (1, -1, RNN_BLOCK), 512),
        "rnn_w_out": adam_nd(rnn_w_out, g_w_out_r, m_rnn_w_out, v_rnn_w_out, rnn_w_out.shape, 256),
    }
    smalls = [rnn_conv_w, rnn_conv_b, rnn_b_a, rnn_b_i, rnn_lambda, ln_g, ln_b, attn_b_f]
    g_small = [g_conv_w, g_conv_b, g_b_a, g_b_i, g_lam, g_ln_g_sum, g_ln_b_sum, g_b_f]
    m_small = [m_rnn_conv_w, m_rnn_conv_b, m_rnn_b_a, m_rnn_b_i, m_rnn_lambda, m_ln_g, m_ln_b, m_attn_b_f]
    v_small = [v_rnn_conv_w, v_rnn_conv_b, v_rnn_b_a, v_rnn_b_i, v_rnn_lambda, v_ln_g, v_ln_b, v_attn_b_f]
    packs = [_pack(t, F32, 16)[None] for t in (smalls, g_small, m_small, v_small)]
    small_out = [_unpack(t[0], [w.shape for w in smalls]) for t in _adamw(*packs, 1, 16)]
    for k, name in enumerate(("rnn_conv_w", "rnn_conv_b", "rnn_b_a", "rnn_b_i", "rnn_lambda", "ln_g", "ln_b",
                              "attn_b_f")):
        upd[name] = [small_out[0][k], small_out[1][k], small_out[2][k]]
    grad = {"attn_w_in": g_w_in_a, "attn_w_out": g_w_out_a, "rnn_w_in": g_w_in_r, "rnn_w_a": g_w_a, "rnn_w_i": g_w_i,
            "rnn_w_out": g_w_out_r, "rnn_conv_w": g_conv_w, "rnn_conv_b": g_conv_b, "rnn_b_a": g_b_a,
            "rnn_b_i": g_b_i, "rnn_lambda": g_lam, "ln_g": g_ln_g_sum, "ln_b": g_ln_b_sum, "attn_b_f": g_b_f}
    names = ("ln_g", "ln_b", "attn_w_in", "attn_b_f", "attn_w_out", "rnn_w_in", "rnn_conv_w", "rnn_conv_b",
             "rnn_w_a", "rnn_b_a", "rnn_w_i", "rnn_b_i", "rnn_lambda", "rnn_w_out")
    return (loss, grad_x, *[grad[n] for n in names], *[upd[n][0] for n in names], *[upd[n][1] for n in names],
            *[upd[n][2] for n in names])
```

```python
import jax
import jax.numpy as jnp
from jax import lax
from jax.experimental import pallas as pl
from jax.experimental.pallas import tpu as pltpu

F32 = jnp.float32
BF16 = jnp.bfloat16
MESH = pl.DeviceIdType.MESH

DEPTH = 4
HEAD_DIM = 64
HEAD_PAIR = 2 * HEAD_DIM
RNN_BLOCK = 256
CONV_WIDTH = 4
LRU_C = 8.0
ALPHA = (2.0 * DEPTH) ** 0.25
LN_EPS = 1e-5
ADAM_LR, ADAM_B1, ADAM_B2, ADAM_EPS, ADAM_WD, ADAM_STEP = 0.001, 0.9, 0.999, 1e-08, 0.01, 10
N_CHIPS = 4
LANES = 1024
NEG = -1e30

NT_DIMS = (((1,), (1,)), ((), ()))
TN_DIMS = (((0,), (0,)), ((), ()))


def _sigmoid(z):
    return 1.0 / (1.0 + jnp.exp(-z))


def _softplus(z):
    return jnp.maximum(z, 0.0) + jnp.log(1.0 + jnp.exp(-jnp.abs(z)))


def _neg_expm1(y):
    poly = y * (1.0 + y * (0.5 + y * (1.0 / 6.0 + y * (1.0 / 24.0))))
    return -jnp.where(y > -0.05, poly, jnp.exp(y) - 1.0)


def _shift_down(cur, prev8, k):
    n = cur.shape[0]
    row = lax.broadcasted_iota(jnp.int32, cur.shape, 0)
    fix = jnp.tile(pltpu.roll(prev8, k, 0), (n // 8, 1))
    return jnp.where(row < k, fix, pltpu.roll(cur, k, 0))


def _shift_up(cur, next8, k):
    n = cur.shape[0]
    row = lax.broadcasted_iota(jnp.int32, cur.shape, 0)
    fix = jnp.tile(pltpu.roll(next8, 8 - k, 0), (n // 8, 1))
    return jnp.where(row >= n - k, fix, pltpu.roll(cur, n - k, 0))


def _proj(x, w, outs, name):
    s_len, d = x.shape
    tm = min(512, s_len)

    def body(x_ref, w_ref, *o_refs):
        xb = x_ref[...].astype(BF16)
        for (c0, wd, dt), o_ref in zip(outs, o_refs):
            step = min(wd, 512)
            for cc in range(0, wd, step):
                o_ref[:, cc:cc + step] = jnp.dot(
                    xb, w_ref[:, c0 + cc:c0 + cc + step], preferred_element_type=F32).astype(dt)

    return pl.pallas_call(
        body, name=name, grid=(s_len // tm,),
        in_specs=[pl.BlockSpec((tm, d), lambda i: (i, 0)), pl.BlockSpec(w.shape, lambda i: (0, 0))],
        out_specs=[pl.BlockSpec((tm, wd), lambda i: (i, 0)) for _, wd, _ in outs],
        out_shape=[jax.ShapeDtypeStruct((s_len, wd), dt) for _, wd, dt in outs],
        compiler_params=pltpu.CompilerParams(dimension_semantics=("parallel",)),
    )(x, w)


def _mm_nt(dz, pieces, w, name, outgoing=()):
    s_len, d = dz.shape
    tm = min(256, s_len)
    steps = s_len // tm
    n = len(pieces)
    ns = len(outgoing)
    cols = [(c0, p.shape[1]) for p, c0 in pieces]

    def body(*refs):
        dz_ref, p_refs, w_ref, o_ref = refs[0], refs[1:1 + n], refs[1 + n], refs[2 + n + ns]
        if ns:
            start, finish = _scatter_steps(refs[2 + n:2 + n + ns], refs[3 + n + ns:3 + n + 2 * ns],
                                           *refs[3 + n + 2 * ns:])
            pl.when(pl.program_id(0) == 0)(start)
        acc = ALPHA * dz_ref[...]
        for (c0, wd), p_ref in zip(cols, p_refs):
            acc = acc + lax.dot_general(p_ref[...], w_ref[:, c0:c0 + wd], NT_DIMS, preferred_element_type=F32)
        o_ref[...] = acc
        if ns:
            pl.when(pl.program_id(0) == steps - 1)(finish)

    out, *arrived = pl.pallas_call(
        body, name=name + "_scatter" if ns else name, grid=(steps,),
        in_specs=[pl.BlockSpec((tm, d), lambda i: (i, 0))]
        + [pl.BlockSpec((tm, wd), lambda i: (i, 0)) for _, wd in cols]
        + [pl.BlockSpec(w.shape, lambda i: (0, 0))] + [ANY] * ns,
        out_specs=[pl.BlockSpec((tm, d), lambda i: (i, 0))] + [ANY] * ns,
        out_shape=[jax.ShapeDtypeStruct((s_len, d), F32)] + [jax.ShapeDtypeStruct(t.shape, t.dtype) for t in outgoing],
        scratch_shapes=[pltpu.SemaphoreType.DMA((3 * ns,))] * 2 if ns else [],
        compiler_params=pltpu.CompilerParams(dimension_semantics=("arbitrary" if ns else "parallel",)),
    )(dz, *[p for p, _ in pieces], w, *outgoing)
    return (out, *arrived)


ROWS_ATTN_IN = (0, 2048)
ROWS_ATTN_OUT = (1024, 1280)
ROWS_RNN_OUT = (1536, 1792)
ROWS_RNN_IN = (3072, 3584)
ROWS_GATES = 4096
LATE_ROWS = 1280
SLOT_ROWS = 4352


DW_ROWS = 2048


def _dw_call(body, name, slots, operands, specs, out_block, out_index, grid, semantics):
    return pl.pallas_call(
        body, name=name, grid=grid,
        in_specs=specs if slots is None else [ANY] + specs,
        out_specs=pl.BlockSpec(out_block, out_index),
        out_shape=jax.ShapeDtypeStruct((N_CHIPS, SLOT_ROWS, LANES), F32),
        input_output_aliases={} if slots is None else {0: 0},
        compiler_params=pltpu.CompilerParams(dimension_semantics=semantics),
    )(*(operands if slots is None else [slots] + operands))


def _dw_attn_in(slots, x, pieces, forget, layer):
    s_len, d = x.shape
    ts = min(s_len, DW_ROWS)
    steps = s_len // ts
    half = d // 2

    def body(g_ref, x_ref, p0, p1, p2, p3, f_ref, o_ref, wf_ref):
        lanes, step = pl.program_id(0), pl.program_id(1)

        @pl.when(step == 0)
        def _():
            o_ref[...] = jnp.zeros_like(o_ref)

        xb = x_ref[...].astype(BF16)
        for j, p_ref in enumerate((p0, p1, p2, p3)):
            o_ref[j] += lax.dot_general(xb, p_ref[...], TN_DIMS, preferred_element_type=F32)

        @pl.when(step == steps - 1)
        def _():
            o_ref[0] = o_ref[0] * (HEAD_DIM ** -0.5)

        @pl.when(lanes == 0)
        def _():
            @pl.when(step == 0)
            def _():
                wf_ref[...] = jnp.zeros_like(wf_ref)

            wf_ref[...] += lax.dot_general(xb, f_ref[...], TN_DIMS, preferred_element_type=F32)

    return pl.pallas_call(
        body, name="dw_attn_in", grid=(2, steps),
        in_specs=[ANY, pl.BlockSpec((ts, d), lambda i, s: (s, 0))] + [pl.BlockSpec((ts, half), lambda i, s: (s, i))] * 4
        + [pl.BlockSpec((ts, 128), lambda i, s: (s, 0))],
        out_specs=[pl.BlockSpec((N_CHIPS, d, half), lambda i, s: (0, ROWS_ATTN_IN[layer] // d, i)),
                   pl.BlockSpec((d, 128), lambda i, s: (0, 0))],
        out_shape=[jax.ShapeDtypeStruct((N_CHIPS, SLOT_ROWS, LANES), F32), jax.ShapeDtypeStruct((d, 128), F32)],
        input_output_aliases={0: 0},
        compiler_params=pltpu.CompilerParams(dimension_semantics=("arbitrary", "arbitrary")),
    )(slots, x, *pieces, forget)


def _dw_out(slots, yg, dz, first_row):
    s_len, d = dz.shape
    ts = min(s_len, DW_ROWS)
    rows = d // N_CHIPS

    def body(*refs):
        a_ref, b_ref, o_ref = refs[-3:]

        @pl.when(pl.program_id(0) == 0)
        def _():
            o_ref[...] = jnp.zeros_like(o_ref)

        prod = lax.dot_general(a_ref[...], b_ref[...].astype(BF16), TN_DIMS, preferred_element_type=F32)
        o_ref[...] += prod.reshape(N_CHIPS, rows, d)

    specs = [pl.BlockSpec((ts, d), lambda s: (s, 0))] * 2
    return _dw_call(body, "dw_out", slots, [yg, dz], specs, (N_CHIPS, rows, d), lambda s: (0, first_row // rows, 0),
                    (s_len // ts,), ("arbitrary",))


def _dw_rnn_in(slots, x, parts, layer):
    s_len, d = x.shape
    ts = min(s_len, DW_ROWS)
    half = d // 2

    def body(*refs):
        x_ref, p_refs, o_ref = refs[-4], refs[-3:-1], refs[-1]

        @pl.when(pl.program_id(0) == 0)
        def _():
            o_ref[...] = jnp.zeros_like(o_ref)

        xb = x_ref[...].astype(BF16)
        for p, p_ref in enumerate(p_refs):
            for jj in (0, 1):
                for r in (0, 1):
                    o_ref[2 * p + jj, :, r * half:(r + 1) * half] += lax.dot_general(
                        xb[:, r * half:(r + 1) * half], p_ref[:, jj * half:(jj + 1) * half], TN_DIMS,
                        preferred_element_type=F32)

    specs = [pl.BlockSpec((ts, d), lambda s: (s, 0))] * 3
    return _dw_call(body, "dw_rnn_in", slots, [x] + list(parts), specs, (N_CHIPS, half, d),
                    lambda s: (0, ROWS_RNN_IN[layer] // half, 0), (s_len // ts,), ("arbitrary",))


def _fcum(fl, bias):
    s_len = fl.shape[0]

    def body(fl_ref, b_ref, cum_ref, sg_ref):
        z = fl_ref[...] + b_ref[...]
        e = jnp.exp(-jnp.abs(z))
        logf = jnp.minimum(z, 0.0) - jnp.log(1.0 + e)
        sneg = jnp.where(z >= 0, e, 1.0) / (1.0 + e)
        run = logf.T[0:16, :]
        sg_ref[...] = sneg.T[0:16, :]
        lane = lax.broadcasted_iota(jnp.int32, (16, s_len), 1)
        sh = 1
        while sh < s_len:
            run = run + jnp.where(lane >= sh, pltpu.roll(run, sh, 1), 0.0)
            sh *= 2
        cum_ref[...] = run

    return pl.pallas_call(
        body, name="fcum",
        out_shape=[jax.ShapeDtypeStruct((16, s_len), F32), jax.ShapeDtypeStruct((16, s_len), F32)],
    )(fl, bias)


def _fbwd(dcum, sneg):
    s_len = dcum.shape[1]

    def body(dc_ref, sg_ref, dl_ref, db_ref):
        run = dc_ref[...]
        lane = lax.broadcasted_iota(jnp.int32, (16, s_len), 1)
        sh = 1
        while sh < s_len:
            run = run + jnp.where(lane < s_len - sh, pltpu.roll(run, s_len - sh, 1), 0.0)
            sh *= 2
        dlog = run * sg_ref[...]
        db_ref[...] = jnp.broadcast_to(jnp.sum(dlog, axis=1, keepdims=True), (16, 128))
        full = jnp.concatenate([dlog, jnp.zeros((112, s_len), F32)], axis=0)
        dl_ref[...] = full.T.astype(BF16)

    return pl.pallas_call(
        body, name="fbwd",
        out_shape=[jax.ShapeDtypeStruct((s_len, 128), BF16), jax.ShapeDtypeStruct((16, 128), F32)],
    )(dcum, sneg)


FWD_TILE = 1024
BWD_TILE = 1024


def _flash_fwd(q, k, v, cum, shards=()):
    s_len, width = q.shape
    tile = min(FWD_TILE, s_len // 2)
    nt = s_len // tile
    reps = tile // 128
    cumr = cum.reshape(-1, tile)
    ns = len(shards)
    pairs = width // HEAD_PAIR

    def body(*refs):
        q_ref, k_ref, v_ref, cum_ref = refs[:4]
        o_ref, lse_ref = refs[4 + ns:6 + ns]
        q_t, v_t, cum_col = refs[6 + 2 * ns:9 + 2 * ns]
        pid = pl.program_id(0)
        if ns:
            start, forward, finish = _gather_steps(refs[4:4 + ns], refs[6 + ns:6 + 2 * ns], *refs[9 + 2 * ns:])
            pl.when(pid == 0)(start)
            pl.when(pid == pairs - 3)(forward)
        top = lax.broadcasted_iota(jnp.int32, (HEAD_PAIR, tile), 0) < HEAD_DIM
        causal = (lax.broadcasted_iota(jnp.int32, (tile, tile), 0)
                  <= lax.broadcasted_iota(jnp.int32, (tile, tile), 1))

        def pre(i, carry):
            r0 = pl.multiple_of(i * tile, tile)
            q_t[i] = q_ref[pl.ds(r0, tile), :].astype(F32).T.astype(BF16)
            v_t[i] = v_ref[pl.ds(r0, tile), :].astype(F32).T.astype(BF16)
            for h in (0, 1):
                row = cum_ref[pl.ds((2 * pid + h) * nt + i, 1), :]
                cum_col[h, pl.ds(r0, tile), :] = jnp.broadcast_to(row, (HEAD_PAIR, tile)).T
            return carry

        lax.fori_loop(0, nt, pre, 0)

        def q_loop(qi, carry):
            qt = q_t[qi]
            zt = jnp.zeros_like(qt)
            qms = (jnp.where(top, qt, zt), jnp.where(top, zt, qt))

            def step(kj, state, masked):
                m0, l0, m1, l1, acc = state
                k0 = pl.multiple_of(kj * tile, tile)
                kt = k_ref[pl.ds(k0, tile), :]
                vt = v_t[kj]
                ms, ls, scales, contrib = [m0, m1], [l0, l1], [], None
                for h in (0, 1):
                    s = jnp.dot(kt, qms[h], preferred_element_type=F32)
                    s = s - jnp.tile(cum_col[h, pl.ds(k0, tile), :], (1, reps))
                    if masked:
                        s = jnp.where(causal, s, NEG)
                    m_new = jnp.maximum(ms[h], jnp.max(s, axis=0, keepdims=True))
                    p = jnp.exp(s - m_new)
                    scale = jnp.exp(ms[h] - m_new)
                    ls[h] = scale * ls[h] + jnp.sum(p, axis=0, keepdims=True)
                    ms[h] = m_new
                    scales.append(scale)
                    vm = jnp.where(top, vt, zt) if h == 0 else jnp.where(top, zt, vt)
                    part = jnp.dot(vm, p.astype(BF16), preferred_element_type=F32)
                    contrib = part if contrib is None else contrib + part
                acc = jnp.where(top, scales[0], scales[1]) * acc + contrib
                return ms[0], ls[0], ms[1], ls[1], acc

            low = jnp.full((1, tile), NEG, F32)
            zero = jnp.zeros((1, tile), F32)
            state = step(qi, (low, zero, low, zero, jnp.zeros((HEAD_PAIR, tile), F32)), True)
            m0, l0, m1, l1, acc = lax.fori_loop(0, qi, lambda kj, c: step(kj, c, False), state)
            q0 = pl.multiple_of(qi * tile, tile)
            o_ref[pl.ds(q0, tile), :] = (acc / jnp.where(top, l0, l1)).T
            lse_ref[pl.ds((2 * pid) * nt + qi, 1), :] = m0 + jnp.log(l0)
            lse_ref[pl.ds((2 * pid + 1) * nt + qi, 1), :] = m1 + jnp.log(l1)
            return carry

        lax.fori_loop(0, nt, q_loop, 0)
        if ns:
            pl.when(pid == pairs - 1)(finish)

    blk = pl.BlockSpec((s_len, HEAD_PAIR), lambda p: (0, p))
    full = pl.BlockSpec(cumr.shape, lambda p: (0, 0))
    sems = [pltpu.SemaphoreType.DMA((GATHER_SEMS * ns,))] * 2 if ns else []
    o, lse, *gathered = pl.pallas_call(
        body, name="flash_fwd_gather" if ns else "flash_fwd", grid=(pairs,),
        in_specs=[blk, blk, blk, full] + [ANY] * ns,
        out_specs=[blk, full] + [ANY] * ns,
        out_shape=[jax.ShapeDtypeStruct((s_len, width), F32), jax.ShapeDtypeStruct(cumr.shape, F32)]
        + [jax.ShapeDtypeStruct((N_CHIPS,) + s.shape, s.dtype) for s in shards],
        scratch_shapes=[pltpu.VMEM((nt, HEAD_PAIR, tile), BF16), pltpu.VMEM((nt, HEAD_PAIR, tile), BF16),
                        pltpu.VMEM((2, s_len, HEAD_PAIR), F32)] + sems,
        compiler_params=pltpu.CompilerParams(dimension_semantics=("arbitrary",)),
    )(q, k, v, cumr, *shards)
    return (o, lse.reshape(cum.shape), *gathered)


def _flash_bwd(q, k, v, o, do, lse, cum, outgoing=()):
    s_len, width = q.shape
    tile = min(BWD_TILE, s_len // 2)
    nt = s_len // tile
    reps = tile // 128
    cumr = cum.reshape(-1, tile)
    lse = lse.reshape(-1, tile)
    ns = len(outgoing)
    pairs = width // HEAD_PAIR

    def body(*refs):
        q_ref, k_ref, v_ref, o_ref, do_ref, lse_ref, cum_ref = refs[:7]
        dq_ref, dk_ref, dv_ref, dcum_ref = refs[7 + ns:11 + ns]
        (q_t, do_t, k_t, do_bf, cum_col, dq_t_acc, delta, q_side, dk_acc, dv_acc,
         k_side) = refs[11 + 2 * ns:22 + 2 * ns]
        pid = pl.program_id(0)
        if ns:
            start, finish = _scatter_steps(refs[7:7 + ns], refs[11 + ns:11 + 2 * ns], *refs[22 + 2 * ns:])
            pl.when(pid == 0)(start)
        top = lax.broadcasted_iota(jnp.int32, (HEAD_PAIR, tile), 0) < HEAD_DIM
        left = lax.broadcasted_iota(jnp.int32, (tile, HEAD_PAIR), 1) < HEAD_DIM
        causal = (lax.broadcasted_iota(jnp.int32, (tile, tile), 0)
                  <= lax.broadcasted_iota(jnp.int32, (tile, tile), 1))

        def pre(i, carry):
            r0 = pl.multiple_of(i * tile, tile)
            d_o = do_ref[pl.ds(r0, tile), :]
            prod_t = (d_o * o_ref[pl.ds(r0, tile), :]).T
            delta[pl.ds(i, 1), :] = jnp.sum(prod_t[:HEAD_DIM], axis=0, keepdims=True)
            delta[pl.ds(nt + i, 1), :] = jnp.sum(prod_t[HEAD_DIM:], axis=0, keepdims=True)
            do_bf[pl.ds(r0, tile), :] = d_o.astype(BF16)
            do_t[i] = d_o.T.astype(BF16)
            q_t[i] = q_ref[pl.ds(r0, tile), :].astype(F32).T.astype(BF16)
            k_t[i] = k_ref[pl.ds(r0, tile), :].astype(F32).T.astype(BF16)
            dq_t_acc[i] = jnp.zeros((HEAD_PAIR, tile), F32)
            for h in (0, 1):
                row = cum_ref[pl.ds((2 * pid + h) * nt + i, 1), :]
                cum_col[h, pl.ds(r0, tile), :] = jnp.broadcast_to(row, (HEAD_PAIR, tile)).T
                q_side[pl.ds(h * nt + i, 1), :] = jnp.zeros((1, tile), F32)
            return carry

        lax.fori_loop(0, nt, pre, 0)

        def kv_loop(kj, carry):
            k0 = pl.multiple_of(kj * tile, tile)
            kt = k_ref[pl.ds(k0, tile), :]
            vt = v_ref[pl.ds(k0, tile), :]
            ktt = k_t[kj]
            zt = jnp.zeros_like(ktt)
            zr = jnp.zeros_like(kt)
            kms = (jnp.where(top, ktt, zt), jnp.where(top, zt, ktt))
            cols = [jnp.tile(cum_col[h, pl.ds(k0, tile), :], (1, reps)) for h in (0, 1)]
            dk_acc[...] = jnp.zeros_like(dk_acc)
            dv_acc[...] = jnp.zeros_like(dv_acc)
            k_side[...] = jnp.zeros_like(k_side)

            def step(qi, carry, masked):
                q0 = pl.multiple_of(qi * tile, tile)
                qtt = q_t[qi]
                dtt = do_t[qi]
                q_rows = q_ref[pl.ds(q0, tile), :]
                do_rows = do_bf[pl.ds(q0, tile), :]
                dq_part = None
                for h in (0, 1):
                    q_m = jnp.where(top, qtt, zt) if h == 0 else jnp.where(top, zt, qtt)
                    do_m = jnp.where(top, dtt, zt) if h == 0 else jnp.where(top, zt, dtt)
                    s = jnp.dot(kt, q_m, preferred_element_type=F32) - cols[h]
                    if masked:
                        s = jnp.where(causal, s, NEG)
                    p = jnp.exp(s - lse_ref[pl.ds((2 * pid + h) * nt + qi, 1), :])
                    dp = jnp.dot(vt, do_m, preferred_element_type=F32)
                    ds = p * (dp - delta[pl.ds(h * nt + qi, 1), :])
                    q_side[pl.ds(h * nt + qi, 1), :] += jnp.sum(ds, axis=0, keepdims=True)
                    folded = ds[:, :128]
                    for b in range(1, reps):
                        folded = folded + ds[:, b * 128:(b + 1) * 128]
                    k_side[h] += folded
                    pb = p.astype(BF16)
                    dsb = ds.astype(BF16)
                    do_h = jnp.where(left, do_rows, zr) if h == 0 else jnp.where(left, zr, do_rows)
                    q_h = jnp.where(left, q_rows, zr) if h == 0 else jnp.where(left, zr, q_rows)
                    dv_acc[...] += jnp.dot(pb, do_h, preferred_element_type=F32)
                    dk_acc[...] += jnp.dot(dsb, q_h, preferred_element_type=F32)
                    part = jnp.dot(kms[h], dsb, preferred_element_type=F32)
                    dq_part = part if dq_part is None else dq_part + part
                dq_t_acc[qi] += dq_part
                return carry

            step(kj, 0, True)
            lax.fori_loop(kj + 1, nt, lambda qi, c: step(qi, c, False), 0)
            dk_ref[pl.ds(k0, tile), :] = dk_acc[...].astype(BF16)
            dv_ref[pl.ds(k0, tile), :] = dv_acc[...].astype(BF16)
            for h in (0, 1):
                dcum_ref[pl.ds((2 * pid + h) * nt + kj, 1), :] = -jnp.sum(k_side[h].T, axis=0, keepdims=True)
            return carry

        lax.fori_loop(0, nt, kv_loop, 0)

        def fin(i, carry):
            r0 = pl.multiple_of(i * tile, tile)
            dq_ref[pl.ds(r0, tile), :] = dq_t_acc[i].T.astype(BF16)
            for h in (0, 1):
                dcum_ref[pl.ds((2 * pid + h) * nt + i, 1), :] += q_side[pl.ds(h * nt + i, 1), :]
            return carry

        lax.fori_loop(0, nt, fin, 0)
        if ns:
            pl.when(pid == pairs - 1)(finish)

    blk = pl.BlockSpec((s_len, HEAD_PAIR), lambda p: (0, p))
    full = pl.BlockSpec(cumr.shape, lambda p: (0, 0))
    transposed = pltpu.VMEM((nt, HEAD_PAIR, tile), BF16)
    sems = [pltpu.SemaphoreType.DMA((3 * ns,))] * 2 if ns else []
    dq, dk, dv, dcum, *arrived = pl.pallas_call(
        body, name="flash_bwd_scatter" if ns else "flash_bwd", grid=(pairs,),
        in_specs=[blk, blk, blk, blk, blk, full, full] + [ANY] * ns,
        out_specs=[blk, blk, blk, full] + [ANY] * ns,
        out_shape=[jax.ShapeDtypeStruct((s_len, width), BF16)] * 3 + [jax.ShapeDtypeStruct(cumr.shape, F32)]
        + [jax.ShapeDtypeStruct(t.shape, t.dtype) for t in outgoing],
        scratch_shapes=[transposed, transposed, transposed, pltpu.VMEM((s_len, HEAD_PAIR), BF16),
                        pltpu.VMEM((2, s_len, HEAD_PAIR), F32), pltpu.VMEM((nt, HEAD_PAIR, tile), F32),
                        pltpu.VMEM((2 * nt, tile), F32), pltpu.VMEM((2 * nt, tile), F32),
                        pltpu.VMEM((tile, HEAD_PAIR), F32), pltpu.VMEM((tile, HEAD_PAIR), F32),
                        pltpu.VMEM((2, tile, HEAD_PAIR), F32)] + sems,
        compiler_params=pltpu.CompilerParams(dimension_semantics=("arbitrary",)),
    )(q, k, v, o, do, lse, cumr, *outgoing)
    return (dq, dk, dv, dcum.reshape(cum.shape), *arrived)


def _out_ln(a, gate, x, w, g, b, name, target=None):
    s_len, d = x.shape
    tm = min(512, s_len)
    nt = 0 if target is None else 1

    def body(*refs):
        a_ref, gate_ref, x_ref, w_ref, g_ref, b_ref = refs[:6]
        first_ref, xh_ref, rs_ref, yg_ref = refs[6 + nt:10 + nt]
        gt = gate_ref[...]
        yg = (a_ref[...] * (gt * _sigmoid(gt))).astype(BF16)
        yg_ref[...] = yg
        z = ALPHA * x_ref[...] + jnp.dot(yg, w_ref[...], preferred_element_type=F32)
        zc = z - jnp.mean(z, axis=1, keepdims=True)
        rstd = lax.rsqrt(jnp.mean(zc * zc, axis=1, keepdims=True) + LN_EPS)
        xh = zc * rstd
        xh_ref[...] = xh
        y = xh * g_ref[...] + b_ref[...]
        rs_ref[...] = jnp.broadcast_to(rstd, (tm, 128))
        if nt:
            sq_ref = refs[10 + nt]

            @pl.when(pl.program_id(0) == 0)
            def _():
                sq_ref[...] = jnp.zeros_like(sq_ref)

            diff = y - refs[6][...]
            first_ref[...] = diff * (1.0 / d)
            sq_ref[...] += jnp.sum(diff * diff, axis=0, keepdims=True)
        else:
            first_ref[...] = y

    row = pl.BlockSpec((tm, d), lambda i: (i, 0))
    vec = pl.BlockSpec((1, d), lambda i: (0, 0))
    return pl.pallas_call(
        body, name=name + "_loss" if nt else name, grid=(s_len // tm,),
        in_specs=[row, row, row, pl.BlockSpec(w.shape, lambda i: (0, 0)), vec, vec] + [row] * nt,
        out_specs=[row, row, pl.BlockSpec((tm, 128), lambda i: (i, 0)), row] + [vec] * nt,
        out_shape=[jax.ShapeDtypeStruct((s_len, d), F32), jax.ShapeDtypeStruct((s_len, d), F32),
                   jax.ShapeDtypeStruct((s_len, 128), F32), jax.ShapeDtypeStruct((s_len, d), BF16)]
        + [jax.ShapeDtypeStruct((1, d), F32)] * nt,
        compiler_params=pltpu.CompilerParams(dimension_semantics=("arbitrary" if nt else "parallel",)),
    )(a, gate, x, w, g, b, *([target] if nt else []))


def _ln_bwd(dout, xh, rs, g, w, gate, a, name, swap=None):
    s_len, d = dout.shape
    tm = min(512, s_len)
    steps = s_len // tm
    ns = 0 if swap is None else 1

    def body(*refs):
        do_ref, xh_ref, rs_ref, g_ref, w_ref, gate_ref, a_ref = refs[:7]
        dz_ref, da_ref, dgate_ref, dg_ref, db_ref = refs[7 + ns:12 + ns]
        if ns:
            start, finish = _swap_steps(refs[7:8], refs[12 + ns:13 + ns], [swap[1]], *refs[13 + ns:])
            pl.when(pl.program_id(0) == 0)(start)

        @pl.when(pl.program_id(0) == 0)
        def _():
            dg_ref[...] = jnp.zeros_like(dg_ref)
            db_ref[...] = jnp.zeros_like(db_ref)

        dout_t = do_ref[...]
        xh_t = xh_ref[...]
        dg_ref[...] += jnp.sum(dout_t * xh_t, axis=0, keepdims=True)
        db_ref[...] += jnp.sum(dout_t, axis=0, keepdims=True)
        dxh = dout_t * g_ref[...]
        m1 = jnp.mean(dxh, axis=1, keepdims=True)
        m2 = jnp.mean(dxh * xh_t, axis=1, keepdims=True)
        dz = rs_ref[:, 0:1] * (dxh - m1 - xh_t * m2)
        dz_ref[...] = dz
        dyg = lax.dot_general(dz.astype(BF16), w_ref[...], NT_DIMS, preferred_element_type=F32)
        gt = gate_ref[...]
        sg = _sigmoid(gt)
        da_ref[...] = dyg * (gt * sg)
        dgate_ref[...] = (dyg * a_ref[...] * (sg * (1.0 + gt * (1.0 - sg)))).astype(BF16)
        if ns:
            pl.when(pl.program_id(0) == steps - 1)(finish)

    row = pl.BlockSpec((tm, d), lambda i: (i, 0))
    vec = pl.BlockSpec((1, d), lambda i: (0, 0))
    extra_out = [jax.ShapeDtypeStruct((N_CHIPS, swap[1][1], LANES), swap[0].dtype)] if ns else []
    return pl.pallas_call(
        body, name=name + "_swap" if ns else name, grid=(steps,),
        in_specs=[row, row, pl.BlockSpec((tm, 128), lambda i: (i, 0)), vec, pl.BlockSpec(w.shape, lambda i: (0, 0)),
                  row, row] + [ANY] * ns,
        out_specs=[row, row, row, vec, vec] + [ANY] * ns,
        out_shape=[jax.ShapeDtypeStruct((s_len, d), F32), jax.ShapeDtypeStruct((s_len, d), F32),
                   jax.ShapeDtypeStruct((s_len, d), BF16), jax.ShapeDtypeStruct((1, d), F32),
                   jax.ShapeDtypeStruct((1, d), F32)] + extra_out,
        scratch_shapes=[pltpu.SemaphoreType.DMA((N_CHIPS,))] * 2 if ns else [],
        compiler_params=pltpu.CompilerParams(dimension_semantics=("arbitrary",)),
    )(dout, xh, rs, g, w, gate, a, *([swap[0]] if ns else []))


def _rnn_fwd(u, conv_w, conv_b, wa, ba, wi, bi, lam):
    s_len, width = u.shape
    tm = min(512, s_len // 2)
    nb = width // RNN_BLOCK

    def body(u_ref, up_ref, cw_ref, cb_ref, wa_ref, ba_ref, wi_ref, bi_ref, lam_ref,
             h_ref, uc_ref, r_ref, i_ref, a_ref, b_scr, h_carry):
        step_id = pl.program_id(0)

        @pl.when(step_id == 0)
        def _():
            h_carry[...] = jnp.zeros_like(h_carry)

        for n in range(nb):
            blk = slice(n * RNN_BLOCK, (n + 1) * RNN_BLOCK)
            ut = u_ref[:, blk]
            prev = jnp.where(step_id > 0, up_ref[:, blk], 0.0)
            uc = cb_ref[:, blk] + cw_ref[3:4, blk] * ut
            for k in (1, 2, 3):
                uc = uc + cw_ref[3 - k:4 - k, blk] * _shift_down(ut, prev, k)
            ucb = uc.astype(BF16)
            r = _sigmoid(jnp.dot(ucb, wa_ref[n], preferred_element_type=F32) + ba_ref[:, blk])
            ig = _sigmoid(jnp.dot(ucb, wi_ref[n], preferred_element_type=F32) + bi_ref[:, blk])
            log_a = -LRU_C * r * _softplus(-lam_ref[:, blk])
            uc_ref[:, blk] = uc
            r_ref[:, blk] = r
            i_ref[:, blk] = ig
            a_ref[:, blk] = jnp.exp(log_a)
            b_scr[:, blk] = jnp.sqrt(_neg_expm1(2.0 * log_a)) * (ig * uc)

        def scan(t, h):
            h = a_ref[pl.ds(t, 1), :] * h + b_scr[pl.ds(t, 1), :]
            h_ref[pl.ds(t, 1), :] = h
            return h

        h_carry[...] = lax.fori_loop(0, tm, scan, h_carry[...], unroll=8)

    row = pl.BlockSpec((tm, width), lambda i: (i, 0))
    prev8 = pl.BlockSpec((8, width), lambda i: (jnp.maximum(i * (tm // 8) - 1, 0), 0))
    vec = pl.BlockSpec((1, width), lambda i: (0, 0))
    mat = pl.BlockSpec(wa.shape, lambda i: (0, 0, 0))
    return pl.pallas_call(
        body, name="rnn_fwd", grid=(s_len // tm,),
        in_specs=[row, prev8, pl.BlockSpec(conv_w.shape, lambda i: (0, 0)), vec, mat, vec, mat, vec, vec],
        out_specs=[row] * 5,
        out_shape=[jax.ShapeDtypeStruct((s_len, width), F32)] * 5,
        scratch_shapes=[pltpu.VMEM((tm, width), F32), pltpu.VMEM((1, width), F32)],
        compiler_params=pltpu.CompilerParams(dimension_semantics=("arbitrary",)),
    )(u, u, conv_w, conv_b, wa, ba, wi, bi, lam)


def _rnn_bwd(dh, a, h, r, ig, uc, lam, wa, wi):
    s_len, width = dh.shape
    tm = min(512, s_len // 2)
    nt = s_len // tm
    nb = width // RNN_BLOCK

    def body(dh_ref, a_ref, h_ref, hp_ref, r_ref, i_ref, uc_ref, lam_ref, wa_ref, wi_ref,
             duc_ref, dwa_ref, dwi_ref, dba_ref, dbi_ref, dlam_ref, g_scr, c_scr, dsp_scr):
        step_id = pl.program_id(0)
        tile_id = nt - 1 - step_id

        @pl.when(step_id == 0)
        def _():
            c_scr[...] = jnp.zeros_like(c_scr)
            dsp_scr[...] = jnp.zeros_like(dsp_scr)
            dwa_ref[...] = jnp.zeros_like(dwa_ref)
            dwi_ref[...] = jnp.zeros_like(dwi_ref)
            dba_ref[...] = jnp.zeros_like(dba_ref)
            dbi_ref[...] = jnp.zeros_like(dbi_ref)

        def scan(j, c):
            t = tm - 1 - j
            g = dh_ref[pl.ds(t, 1), :] + c
            g_scr[pl.ds(t, 1), :] = g
            return a_ref[pl.ds(t, 1), :] * g

        c_scr[...] = lax.fori_loop(0, tm, scan, c_scr[...], unroll=8)

        for n in range(nb):
            blk = slice(n * RNN_BLOCK, (n + 1) * RNN_BLOCK)
            g_t = g_scr[:, blk]
            hp8 = jnp.where(tile_id > 0, hp_ref[:, blk], 0.0)
            h_prev = _shift_down(h_ref[:, blk], hp8, 1)
            av, rv, iv, ucv = a_ref[:, blk], r_ref[:, blk], i_ref[:, blk], uc_ref[:, blk]
            sp = _softplus(-lam_ref[:, blk])
            mag = jnp.sqrt(_neg_expm1(-2.0 * LRU_C * rv * sp))
            d_iu = g_t * mag
            dla = g_t * h_prev * av - g_t * (iv * ucv) * (av * av) / mag
            dsp_scr[:, blk] += jnp.sum(dla * (-LRU_C * rv), axis=0, keepdims=True)
            dra = dla * (-LRU_C * sp) * rv * (1.0 - rv)
            dia = d_iu * ucv * iv * (1.0 - iv)
            drab, diab, ucb = dra.astype(BF16), dia.astype(BF16), ucv.astype(BF16)
            duc_ref[:, blk] = (d_iu * iv
                               + lax.dot_general(drab, wa_ref[n], NT_DIMS, preferred_element_type=F32)
                               + lax.dot_general(diab, wi_ref[n], NT_DIMS, preferred_element_type=F32))
            dwa_ref[n] += lax.dot_general(ucb, drab, TN_DIMS, preferred_element_type=F32)
            dwi_ref[n] += lax.dot_general(ucb, diab, TN_DIMS, preferred_element_type=F32)
            dba_ref[:, blk] += jnp.sum(dra, axis=0, keepdims=True)
            dbi_ref[:, blk] += jnp.sum(dia, axis=0, keepdims=True)

        @pl.when(step_id == nt - 1)
        def _():
            dlam_ref[...] = -dsp_scr[...] * _sigmoid(-lam_ref[...])

    row = pl.BlockSpec((tm, width), lambda i: (nt - 1 - i, 0))
    prev8 = pl.BlockSpec((8, width), lambda i: (jnp.maximum((nt - 1 - i) * (tm // 8) - 1, 0), 0))
    vec = pl.BlockSpec((1, width), lambda i: (0, 0))
    mat = pl.BlockSpec(wa.shape, lambda i: (0, 0, 0))
    return pl.pallas_call(
        body, name="rnn_bwd", grid=(nt,),
        in_specs=[row, row, row, prev8, row, row, row, vec, mat, mat],
        out_specs=[row, mat, mat, vec, vec, vec],
        out_shape=[jax.ShapeDtypeStruct((s_len, width), F32), jax.ShapeDtypeStruct(wa.shape, F32),
                   jax.ShapeDtypeStruct(wa.shape, F32)] + [jax.ShapeDtypeStruct((1, width), F32)] * 3,
        scratch_shapes=[pltpu.VMEM((tm, width), F32), pltpu.VMEM((1, width), F32), pltpu.VMEM((1, width), F32)],
        compiler_params=pltpu.CompilerParams(dimension_semantics=("arbitrary",)),
    )(dh, a, h, h, r, ig, uc, lam, wa, wi)


def _conv_bwd(duc, u, conv_w):
    s_len, width = duc.shape
    tm = min(256, s_len)
    nt = s_len // tm

    def body(d_ref, dn_ref, u_ref, up_ref, cw_ref, du_ref, dcw_ref, dcb_ref):
        step_id = pl.program_id(0)

        @pl.when(step_id == 0)
        def _():
            dcw_ref[...] = jnp.zeros_like(dcw_ref)
            dcb_ref[...] = jnp.zeros_like(dcb_ref)

        for n in range(width // RNN_BLOCK):
            blk = slice(n * RNN_BLOCK, (n + 1) * RNN_BLOCK)
            dt = d_ref[:, blk]
            ut = u_ref[:, blk]
            nxt = jnp.where(step_id < nt - 1, dn_ref[:, blk], 0.0)
            prev = jnp.where(step_id > 0, up_ref[:, blk], 0.0)
            du = cw_ref[3:4, blk] * dt
            dcw_ref[3:4, blk] += jnp.sum(dt * ut, axis=0, keepdims=True)
            for k in (1, 2, 3):
                du = du + cw_ref[3 - k:4 - k, blk] * _shift_up(dt, nxt, k)
                dcw_ref[3 - k:4 - k, blk] += jnp.sum(dt * _shift_down(ut, prev, k), axis=0, keepdims=True)
            du_ref[:, blk] = du.astype(BF16)
            dcb_ref[:, blk] += jnp.sum(dt, axis=0, keepdims=True)

    row = pl.BlockSpec((tm, width), lambda i: (i, 0))
    prev8 = pl.BlockSpec((8, width), lambda i: (jnp.maximum(i * (tm // 8) - 1, 0), 0))
    next8 = pl.BlockSpec((8, width), lambda i: (jnp.minimum((i + 1) * (tm // 8), s_len // 8 - 1), 0))
    return pl.pallas_call(
        body, name="conv_bwd", grid=(nt,),
        in_specs=[row, next8, row, prev8, pl.BlockSpec(conv_w.shape, lambda i: (0, 0))],
        out_specs=[row, pl.BlockSpec(conv_w.shape, lambda i: (0, 0)), pl.BlockSpec((1, width), lambda i: (0, 0))],
        out_shape=[jax.ShapeDtypeStruct((s_len, width), BF16), jax.ShapeDtypeStruct(conv_w.shape, F32),
                   jax.ShapeDtypeStruct((1, width), F32)],
        compiler_params=pltpu.CompilerParams(dimension_semantics=("arbitrary",)),
    )(duc, duc, u, u, conv_w)


def _pair_sum(core, grads, theirs, first_row, out_dtype):
    n, half, _ = theirs.shape
    tb = 128 if half % 128 == 0 and first_row % 128 == 0 else half
    assert first_row % tb == 0 and half % tb == 0

    def body(c_ref, a_ref, b_ref, o_ref):
        o_ref[...] = (a_ref[...] + b_ref[...]).astype(out_dtype)

    blk = pl.BlockSpec((n, tb, LANES), lambda i, c: (0, i, 0))
    mine = pl.BlockSpec((n, tb, LANES), lambda i, c: (0, first_row // tb + c[0] * (half // tb) + i, 0))
    return pl.pallas_call(
        body, name="pair_sum",
        grid_spec=pltpu.PrefetchScalarGridSpec(
            num_scalar_prefetch=1, grid=(half // tb,), in_specs=[mine, blk], out_specs=blk),
        out_shape=jax.ShapeDtypeStruct((n, half, LANES), out_dtype),
        compiler_params=pltpu.CompilerParams(dimension_semantics=("parallel",)),
    )(core, grads, theirs)


def _sum_chips(parts):
    rows = parts.shape[1]
    tb = 128 if rows % 128 == 0 else rows

    def body(p_ref, o_ref):
        o_ref[...] = ((p_ref[0].astype(F32) + p_ref[1].astype(F32)) + p_ref[2].astype(F32)) + p_ref[3].astype(F32)

    return pl.pallas_call(
        body, name="chip_sum", grid=(rows // tb,),
        in_specs=[pl.BlockSpec((N_CHIPS, tb, LANES), lambda i: (0, i, 0))],
        out_specs=pl.BlockSpec((tb, LANES), lambda i: (i, 0)),
        out_shape=jax.ShapeDtypeStruct((rows, LANES), F32),
        compiler_params=pltpu.CompilerParams(dimension_semantics=("parallel",)),
    )(parts)


def _adamw(w, g, m, v, lead_per_block, rows_per_block):
    n, rows, cols = w.shape
    blk = pl.BlockSpec((lead_per_block, rows_per_block, cols), lambda i, j: (i, j, 0))

    def body(w_ref, g_ref, m_ref, v_ref, d_ref, nm_ref, nv_ref):
        gt = g_ref[...]
        nm = ADAM_B1 * m_ref[...] + (1.0 - ADAM_B1) * gt
        nv = ADAM_B2 * v_ref[...] + (1.0 - ADAM_B2) * (gt * gt)
        m_hat = nm / (1.0 - ADAM_B1 ** ADAM_STEP)
        v_hat = nv / (1.0 - ADAM_B2 ** ADAM_STEP)
        d_ref[...] = -ADAM_LR * (m_hat / (jnp.sqrt(v_hat) + ADAM_EPS) + ADAM_WD * w_ref[...])
        nm_ref[...] = nm
        nv_ref[...] = nv

    return pl.pallas_call(
        body, name="adamw", grid=(n // lead_per_block, rows // rows_per_block), in_specs=[blk] * 4,
        out_specs=[blk] * 3,
        out_shape=[jax.ShapeDtypeStruct(w.shape, F32)] * 3,
        compiler_params=pltpu.CompilerParams(dimension_semantics=("parallel", "parallel")),
    )(w, g, m, v)


def _place():
    x, y, c = lax.axis_index("x"), lax.axis_index("y"), lax.axis_index("c")
    return x, y, c, [(1 - x, y), (x, 1 - y), (1 - x, 1 - y)]


ANY = pl.BlockSpec(memory_space=pl.ANY)
COPY_PARTS = 4


def _remote_parts(src_of, dst_of, rows, send_sem, recv_sem, to, begin=True):
    parts = COPY_PARTS if rows % (16 * COPY_PARTS) == 0 else 1
    step = rows // parts
    for i in range(parts if begin is not False else 0):
        pltpu.make_async_remote_copy(src_ref=src_of(i * step, step), dst_ref=dst_of(i * step, step),
                                     send_sem=send_sem, recv_sem=recv_sem, device_id=to, device_id_type=MESH).start()
    if begin == "only":
        return None
    return pltpu.make_async_remote_copy(src_ref=src_of(0, rows), dst_ref=dst_of(0, rows), send_sem=send_sem,
                                        recv_sem=recv_sem, device_id=to, device_id_type=MESH)


GATHER_SEMS = 7


def _gather_steps(p_refs, o_refs, send_sems, recv_sems):
    x, y, c, chips = _place()
    me = 2 * x + y

    def half(ref, which):
        rows = ref.shape[0] // 2
        return ref.at[pl.ds(which * rows, rows)]

    def copy(k, src, dst, to):
        return pltpu.make_async_remote_copy(src_ref=src, dst_ref=dst, send_sem=send_sems.at[k],
                                            recv_sem=recv_sems.at[k], device_id=to, device_id_type=MESH)

    def first_copies():
        out = []
        for b, (p_ref, o_ref) in enumerate(zip(p_refs, o_refs)):
            for j, (cx, cy) in enumerate(chips):
                out.append(copy(GATHER_SEMS * b + j, half(p_ref, c), half(o_ref.at[me], c), (cx, cy, c)))
            out.append(copy(GATHER_SEMS * b + 6, p_ref, o_ref.at[me], (x, y, 1 - c)))
        return out

    def passed_copies():
        out = []
        for b, o_ref in enumerate(o_refs):
            for j, (cx, cy) in enumerate(chips):
                landed = half(o_ref.at[2 * cx + cy], c)
                out.append(copy(GATHER_SEMS * b + 3 + j, landed, landed, (x, y, 1 - c)))
        return out

    def start():
        for cp in first_copies():
            cp.start()

    def forward():
        for b, o_ref in enumerate(o_refs):
            for j, (cx, cy) in enumerate(chips):
                landed = half(o_ref.at[2 * cx + cy], c)
                copy(GATHER_SEMS * b + j, landed, landed, (x, y, c)).wait_recv()
        for cp in passed_copies():
            cp.start()

    def finish():
        for b, o_ref in enumerate(o_refs):
            for j, (cx, cy) in enumerate(chips):
                other = half(o_ref.at[2 * cx + cy], 1 - c)
                copy(GATHER_SEMS * b + 3 + j, other, other, (x, y, c)).wait_recv()
            copy(GATHER_SEMS * b + 6, o_ref.at[me], o_ref.at[me], (x, y, c)).wait_recv()
        for cp in first_copies() + passed_copies():
            cp.wait_send()

    return start, forward, finish


def _gather_chips(shards):
    nb = len(shards)

    def body(*refs):
        for step in _gather_steps(refs[:nb], refs[nb:2 * nb], *refs[2 * nb:]):
            step()

    return pl.pallas_call(
        body, name="gather_chips", in_specs=[ANY] * nb, out_specs=[ANY] * nb,
        out_shape=[jax.ShapeDtypeStruct((N_CHIPS,) + s.shape, s.dtype) for s in shards],
        scratch_shapes=[pltpu.SemaphoreType.DMA((GATHER_SEMS * nb,)), pltpu.SemaphoreType.DMA((GATHER_SEMS * nb,))],
    )(*shards)


def _swap_steps(g_refs, t_refs, spans, send_sems, recv_sems):
    x, y, c, _ = _place()

    def gives(begin):
        return [_remote_parts(
            lambda r0, m, g_ref=g_ref, j=j, base=first_row + (1 - c) * half: g_ref.at[j, pl.ds(base + r0, m), :],
            lambda r0, m, t_ref=t_ref, j=j: t_ref.at[j, pl.ds(r0, m), :],
            half, send_sems.at[b * N_CHIPS + j], recv_sems.at[b * N_CHIPS + j], (x, y, 1 - c), begin)
            for b, (g_ref, t_ref, (first_row, half)) in enumerate(zip(g_refs, t_refs, spans)) for j in range(N_CHIPS)]

    def start():
        gives("only")

    def finish():
        for give in gives(False):
            give.wait()

    return start, finish


def _swap_halves(bufs, spans):
    nb = len(bufs)

    def body(*refs):
        for step in _swap_steps(refs[:nb], refs[nb:2 * nb], spans, *refs[2 * nb:]):
            step()

    return pl.pallas_call(
        body, name="swap_halves", in_specs=[ANY] * nb, out_specs=[ANY] * nb,
        out_shape=[jax.ShapeDtypeStruct((b.shape[0], half, b.shape[2]), b.dtype) for b, (_, half) in zip(bufs, spans)],
        scratch_shapes=[pltpu.SemaphoreType.DMA((nb * N_CHIPS,)), pltpu.SemaphoreType.DMA((nb * N_CHIPS,))],
    )(*bufs)


def _scatter_steps(t_refs, o_refs, send_sems, recv_sems):
    x, y, c, chips = _place()
    me = 2 * x + y

    def slot(ref, chip):
        return lambda r0, n: ref.at[chip, pl.ds(r0, n), :]

    def sends(begin):
        return [_remote_parts(slot(t_ref, 2 * cx + cy), slot(o_ref, me), t_ref.shape[1], send_sems.at[3 * b + j],
                              recv_sems.at[3 * b + j], (cx, cy, c), begin)
                for b, (t_ref, o_ref) in enumerate(zip(t_refs, o_refs)) for j, (cx, cy) in enumerate(chips)]

    def start():
        sends("only")

    def finish():
        for b, o_ref in enumerate(o_refs):
            for j, (cx, cy) in enumerate(chips):
                landed = o_ref.at[2 * cx + cy]
                pltpu.make_async_remote_copy(src_ref=landed, dst_ref=landed, send_sem=send_sems.at[3 * b + j],
                                             recv_sem=recv_sems.at[3 * b + j], device_id=(x, y, c),
                                             device_id_type=MESH).wait_recv()
        for cp in sends(False):
            cp.wait_send()

    return start, finish


def _give_sibling(bufs):
    nb = len(bufs)

    def body(*refs):
        h_refs, o_refs, (send_sems, recv_sems) = refs[:nb], refs[nb:2 * nb], refs[2 * nb:]
        x, y, c, _ = _place()
        gives = [_remote_parts(lambda r0, n, h_ref=h_ref: h_ref.at[pl.ds(r0, n), :],
                               lambda r0, n, o_ref=o_ref: o_ref.at[pl.ds(r0, n), :],
                               h_ref.shape[0], send_sems.at[b], recv_sems.at[b], (x, y, 1 - c))
                 for b, (h_ref, o_ref) in enumerate(zip(h_refs, o_refs))]
        for give in gives:
            give.wait()

    return pl.pallas_call(
        body, name="give_sibling", in_specs=[ANY] * nb, out_specs=[ANY] * nb,
        out_shape=[jax.ShapeDtypeStruct(b.shape, b.dtype) for b in bufs],
        scratch_shapes=[pltpu.SemaphoreType.DMA((nb,)), pltpu.SemaphoreType.DMA((nb,))],
    )(*bufs)


def _pack(arrays, dtype, row_align):
    flat = []
    for arr in arrays:
        v = arr.reshape(-1)
        flat.append(jnp.pad(v, (0, (-v.shape[0]) % LANES)))
    total = sum(f.shape[0] for f in flat) // LANES
    pad_rows = (-total) % row_align
    if pad_rows:
        flat.append(jnp.zeros((pad_rows * LANES,), dtype))
    return jnp.concatenate(flat).reshape(-1, LANES)


def _unpack(buf, shapes, rows_align=1):
    lead = buf.shape[:-2]
    flat = buf.reshape(lead + (-1,))
    out, off = [], 0
    for shape in shapes:
        size = 1
        for dim in shape:
            size *= dim
        out.append(flat[..., off:off + size].reshape(lead + tuple(shape)))
        off += size + (-size) % (LANES * rows_align)
    return out


def _from_chips(parts, axis):
    return jnp.concatenate([parts[j] for j in range(N_CHIPS)], axis=axis)


def kernel(x, ln_g, ln_b, attn_w_in, attn_b_f, attn_w_out, rnn_w_in, rnn_conv_w, rnn_conv_b, rnn_w_a, rnn_b_a, rnn_w_i, rnn_b_i, rnn_lambda, rnn_w_out, loss_target, m_ln_g, m_ln_b, m_attn_w_in, m_attn_b_f, m_attn_w_out, m_rnn_w_in, m_rnn_conv_w, m_rnn_conv_b, m_rnn_w_a, m_rnn_b_a, m_rnn_w_i, m_rnn_b_i, m_rnn_lambda, m_rnn_w_out, v_ln_g, v_ln_b, v_attn_w_in, v_attn_b_f, v_attn_w_out, v_rnn_w_in, v_rnn_conv_w, v_rnn_conv_b, v_rnn_w_a, v_rnn_b_a, v_rnn_w_i, v_rnn_b_i, v_rnn_lambda, v_rnn_w_out):
    s_len, d = x.shape[1], x.shape[2]
    xs = x.reshape(s_len, d)
    target = loss_target.reshape(s_len, d)
    heads = attn_b_f.shape[1]
    qkvg = attn_w_in.shape[2] * N_CHIPS - heads

    me = 2 * lax.axis_index("x") + lax.axis_index("y")
    core = lax.axis_index("c").astype(jnp.int32)
    small = [rnn_conv_w, rnn_conv_b, rnn_b_a, rnn_b_i, rnn_lambda]
    a_in, a_out = attn_w_in.astype(BF16), attn_w_out.astype(BF16)
    later = [a_in[1], a_out] + [w.astype(BF16) for w in (rnn_w_in, rnn_w_a, rnn_w_i, rnn_w_out)]
    got_in, got_small = _gather_chips([a_in[0], _pack(small, F32, 16)])
    conv_w, conv_b, b_a, b_i, lam = [
        _from_chips(p, ax) for p, ax in zip(_unpack(got_small, [w.shape for w in small]), (2, 1, 1, 1, 1))]

    def attn_in_weights(w_in):
        shard = w_in.shape[2]

        def columns(first, last):
            return [w_in[j][:, max(first - j * shard, 0):min(last - j * shard, shard)]
                    for j in range(N_CHIPS) if first < (j + 1) * shard and last > j * shard]

        return jnp.concatenate([t * (HEAD_DIM ** -0.5) for t in columns(0, d)] + columns(d, qkvg + heads)
                               + [jnp.zeros((d, 128 - heads), BF16)], axis=1)

    w_cat = [attn_in_weights(got_in), None]
    b_f = jnp.pad(attn_b_f, ((0, 0), (0, 128 - heads)))

    saved = []
    cur = xs
    for layer in range(DEPTH):
        idx = layer // 2
        g_l, b_l = ln_g[layer:layer + 1], ln_b[layer:layer + 1]
        goal = target if layer == DEPTH - 1 else None
        if layer % 2 == 0:
            q, k, v, gate, fl = _proj(cur, w_cat[idx], [(0, d, BF16), (d, d, BF16), (2 * d, d, BF16),
                                                         (3 * d, d, F32), (4 * d, 128, F32)], "attn_proj")
            cum, sneg = _fcum(fl, b_f[idx:idx + 1])
            o, lse, *rest = _flash_fwd(q, k, v, cum, later if layer == 0 else ())
            if layer == 0:
                w_cat[1] = attn_in_weights(rest[0])
                w_out_a = jnp.moveaxis(rest[1], 0, 1).reshape(2, d, d)
                w_in_r = jnp.moveaxis(rest[2], 0, 2).reshape(2, d, 2 * d)
                w_a = jnp.transpose(rest[3], (1, 2, 0, 3, 4)).reshape(2, -1, RNN_BLOCK, RNN_BLOCK)
                w_i = jnp.transpose(rest[4], (1, 2, 0, 3, 4)).reshape(2, -1, RNN_BLOCK, RNN_BLOCK)
                w_out_r = jnp.moveaxis(rest[5], 0, 1).reshape(2, d, d)
            nxt, xh, rs, yg, *sq = _out_ln(o, gate, cur, w_out_a[idx], g_l, b_l, "out_ln", goal)
            saved.append(dict(x=cur, q=q, k=k, v=v, gate=gate, cum=cum, sneg=sneg, a=o, lse=lse, xh=xh, rs=rs, yg=yg))
        else:
            u, gate = _proj(cur, w_in_r[idx], [(0, d, F32), (d, d, F32)], "rnn_proj")
            vecs = [t[idx:idx + 1] for t in (conv_b, b_a, b_i, lam)]
            hseq, uc, r, ig, a = _rnn_fwd(u, conv_w[idx], vecs[0], w_a[idx], vecs[1], w_i[idx], vecs[2], vecs[3])
            nxt, xh, rs, yg, *sq = _out_ln(hseq, gate, cur, w_out_r[idx], g_l, b_l, "out_ln", goal)
            saved.append(dict(x=cur, u=u, gate=gate, uc=uc, r=r, ig=ig, av=a, a=hseq, xh=xh, rs=rs, yg=yg, lam=vecs[3]))
        cur = nxt

    dout = cur
    loss_here = 0.5 * jnp.sum(sq[0]) / d

    g_ln_g, g_ln_b = [None] * DEPTH, [None] * DEPTH
    g_attn = [None] * 2
    g_rnn = [None] * 2
    slots = None
    core1 = core.reshape(1)
    late_half = (SLOT_ROWS - LATE_ROWS) // 2

    def by_chip(t, axis):
        shape = t.shape[:axis] + (N_CHIPS, t.shape[axis] // N_CHIPS) + t.shape[axis + 1:]
        flat = jnp.moveaxis(t.reshape(shape), axis, 0).reshape(N_CHIPS, -1)
        return jnp.pad(flat, ((0, 0), (0, (-flat.shape[1]) % (8 * LANES)))).reshape(N_CHIPS, -1, LANES)

    def both(key):
        return jnp.stack([it[key] for it in g_rnn])

    for layer in reversed(range(DEPTH)):
        idx = layer // 2
        sv = saved[layer]
        swap = None
        if layer == 0:
            gates = jnp.concatenate([by_chip(both("w_a"), 2), by_chip(both("w_i"), 2)], axis=1)
            slots = lax.dynamic_update_slice(slots, gates, (0, ROWS_GATES, 0))
            swap = (slots, (LATE_ROWS, late_half))
        w_out = w_out_a[idx] if layer % 2 == 0 else w_out_r[idx]
        dz, da, dgate, dg, db, *theirs_late = _ln_bwd(dout, sv["xh"], sv["rs"], ln_g[layer:layer + 1], w_out,
                                                      sv["gate"], sv["a"], "ln_bwd", swap)
        if layer == 0:
            pair_late = _pair_sum(core1, slots, theirs_late[0], LATE_ROWS, BF16)
        g_ln_g[layer], g_ln_b[layer] = dg, db
        slots = _dw_out(slots, sv["yg"], dz, (ROWS_ATTN_OUT if layer % 2 == 0 else ROWS_RNN_OUT)[idx])
        if layer % 2 == 0:
            dq, dk, dv, dcum, *arrived = _flash_bwd(sv["q"], sv["k"], sv["v"], sv["a"], da, sv["lse"], sv["cum"],
                                                    [pair_late] if layer == 0 else ())
            dlogit, dbf = _fbwd(dcum, sv["sneg"])
            pieces = [dq, dk, dv, dgate, dlogit]
            if layer > 0:
                dout, = _mm_nt(dz, [(p, j * d) for j, p in enumerate(pieces)], w_cat[idx], "attn_dx")
            slots, d_w_f = _dw_attn_in(slots, sv["x"], pieces[:4], dlogit, idx)
            g_attn[idx] = dict(w_f=d_w_f[:, :heads], b_f=dbf[:, 0])
        else:
            duc, d_wa, d_wi, d_ba, d_bi, d_lam = _rnn_bwd(da, sv["av"], sv["a"], sv["r"], sv["ig"], sv["uc"], sv["lam"],
                                                          w_a[idx], w_i[idx])
            du, d_cw, d_cb = _conv_bwd(duc, sv["u"], conv_w[idx])
            dout, = _mm_nt(dz, [(du, 0), (dgate, d)], w_in_r[idx], "rnn_dx")
            slots = _dw_rnn_in(slots, sv["x"], (du, dgate), idx)
            g_rnn[idx] = dict(conv_w=d_cw, conv_b=d_cb[0], w_a=d_wa, b_a=d_ba[0], w_i=d_wi, b_i=d_bi[0], lam=d_lam[0])

    def everywhere(t):
        flat = t.reshape(-1)
        flat = jnp.pad(flat, (0, (-flat.shape[0]) % (8 * LANES))).reshape(-1, LANES)
        return jnp.broadcast_to(flat[None], (N_CHIPS,) + flat.shape)

    in_rows = jnp.stack([slots[1:, r:r + d, :heads] for r in ROWS_ATTN_IN], axis=1)
    edges = jnp.concatenate([in_rows, jnp.stack([it["w_f"] for it in g_attn])[None]])
    rows = [everywhere(edges), by_chip(both("conv_w"), 2),
            by_chip(both("conv_b"), 1), by_chip(both("b_a"), 1), by_chip(both("b_i"), 1), by_chip(both("lam"), 1),
            everywhere(jnp.concatenate(g_ln_g)), everywhere(jnp.concatenate(g_ln_b)),
            everywhere(jnp.stack([it["b_f"] for it in g_attn])), everywhere(loss_here)]
    used = sum(r.shape[1] for r in rows)
    small = jnp.concatenate(rows + [jnp.zeros((N_CHIPS, (-used) % 32, LANES), F32)], axis=1)

    spans = [(0, LATE_ROWS // 2), (0, small.shape[1] // 2)]
    theirs = _swap_halves([slots, small], spans)
    pair = [_pair_sum(core1, slots, theirs[0], 0, BF16), _pair_sum(core1, small, theirs[1], 0, F32)]
    dout, *arrived_last = _mm_nt(dz, [(p, j * d) for j, p in enumerate(pieces)], w_cat[0], "attn_dx", pair)
    grad_x = dout.reshape(x.shape)
    summed = []
    for mine_all, got in zip([pair_late] + pair, list(arrived) + arrived_last):
        own = lax.dynamic_index_in_dim(mine_all, me, 0, keepdims=False)
        summed.append(_sum_chips(lax.dynamic_update_index_in_dim(got, own, me, 0)))
    late, early, small_sum = [
        jnp.concatenate([jnp.where(core == 0, mine, other), jnp.where(core == 0, other, mine)])
        for mine, other in zip(summed, _give_sibling(summed))]

    def rows_at(first, n):
        return early[first:first + n] if first < LATE_ROWS else late[first - LATE_ROWS:first - LATE_ROWS + n]

    g_in_group = jnp.stack([rows_at(r, d) for r in ROWS_ATTN_IN])
    g_w_out_a = jnp.stack([rows_at(r, d // N_CHIPS) for r in ROWS_ATTN_OUT])
    g_w_out_r = jnp.stack([rows_at(r, d // N_CHIPS) for r in ROWS_RNN_OUT])
    folded = jnp.stack([rows_at(r, d // 2) for r in ROWS_RNN_IN])
    g_w_in_r = jnp.concatenate([folded[:, :, :d // 2], folded[:, :, d // 2:]], axis=1)
    gate_rows = rnn_w_a.size // LANES
    g_w_a = rows_at(ROWS_GATES, gate_rows).reshape(rnn_w_a.shape)
    g_w_i = rows_at(ROWS_GATES + gate_rows, gate_rows).reshape(rnn_w_i.shape)
    (g_edges, g_conv_w, g_conv_b, g_b_a, g_b_i, g_lam, g_ln_g_sum, g_ln_b_sum, g_b_f,
     loss) = _unpack(small_sum, [
        (N_CHIPS, 2, d, heads), rnn_conv_w.shape, rnn_conv_b.shape, rnn_b_a.shape,
        rnn_b_i.shape, rnn_lambda.shape, ln_g.shape, ln_b.shape, attn_b_f.shape, ()], rows_align=8)
    wide = jnp.concatenate([g_in_group, lax.dynamic_index_in_dim(g_edges, me, 0, keepdims=False)], axis=2)
    g_w_in_a = lax.dynamic_slice(wide, (0, 0, (heads // N_CHIPS) * me), attn_w_in.shape)

    def adam_nd(w, g, m, v, view, rows_per_block):
        outs = _adamw(w.reshape(view), g.reshape(view), m.reshape(view), v.reshape(view), 1, rows_per_block)
        return [t.reshape(w.shape) for t in outs]

    turned = [jnp.transpose(t, (2, 0, 1)) for t in (attn_w_in, g_w_in_a, m_attn_w_in, v_attn_w_in)]
    upd = {
        "attn_w_in": [jnp.transpose(t, (1, 2, 0)) for t in _adamw(*turned, attn_w_in.shape[2] // N_CHIPS, 2)],
        "attn_w_out": adam_nd(attn_w_out, g_w_out_a, m_attn_w_out, v_attn_w_out, attn_w_out.shape, 256),
        "rnn_w_in": adam_nd(rnn_w_in, g_w_in_r, m_rnn_w_in, v_rnn_w_in, rnn_w_in.shape, 512),
        "rnn_w_a": adam_nd(rnn_w_a, g_w_a, m_rnn_w_a, v_rnn_w_a, (1, -1, RNN_BLOCK), 512),
        "rnn_w_i": adam_nd(rnn_w_i, g_w_i, m_rnn_w_i, v_rnn_w_i, (1, -1, RNN_BLOCK), 512),
        "rnn_w_out": adam_nd(rnn_w_out, g_w_out_r, m_rnn_w_out, v_rnn_w_out, rnn_w_out.shape, 256),
    }
    smalls = [rnn_conv_w, rnn_conv_b, rnn_b_a, rnn_b_i, rnn_lambda, ln_g, ln_b, attn_b_f]
    g_small = [g_conv_w, g_conv_b, g_b_a, g_b_i, g_lam, g_ln_g_sum, g_ln_b_sum, g_b_f]
    m_small = [m_rnn_conv_w, m_rnn_conv_b, m_rnn_b_a, m_rnn_b_i, m_rnn_lambda, m_ln_g, m_ln_b, m_attn_b_f]
    v_small = [v_rnn_conv_w, v_rnn_conv_b, v_rnn_b_a, v_rnn_b_i, v_rnn_lambda, v_ln_g, v_ln_b, v_attn_b_f]
    packs = [_pack(t, F32, 16)[None] for t in (smalls, g_small, m_small, v_small)]
    small_out = [_unpack(t[0], [w.shape for w in smalls]) for t in _adamw(*packs, 1, 16)]
    for k, name in enumerate(("rnn_conv_w", "rnn_conv_b", "rnn_b_a", "rnn_b_i", "rnn_lambda", "ln_g", "ln_b",
                              "attn_b_f")):
        upd[name] = [small_out[0][k], small_out[1][k], small_out[2][k]]
    grad = {"attn_w_in": g_w_in_a, "attn_w_out": g_w_out_a, "rnn_w_in": g_w_in_r, "rnn_w_a": g_w_a, "rnn_w_i": g_w_i,
            "rnn_w_out": g_w_out_r, "rnn_conv_w": g_conv_w, "rnn_conv_b": g_conv_b, "rnn_b_a": g_b_a,
            "rnn_b_i": g_b_i, "rnn_lambda": g_lam, "ln_g": g_ln_g_sum, "ln_b": g_ln_b_sum, "attn_b_f": g_b_f}
    names = ("ln_g", "ln_b", "attn_w_in", "attn_b_f", "attn_w_out", "rnn_w_in", "rnn_conv_w", "rnn_conv_b",
             "rnn_w_a", "rnn_b_a", "rnn_w_i", "rnn_b_i", "rnn_lambda", "rnn_w_out")
    return (loss, grad_x, *[grad[n] for n in names], *[upd[n][0] for n in names], *[upd[n][1] for n in names],
            *[upd[n][2] for n in names])
```

```python
import jax
import jax.numpy as jnp
from jax import lax
from jax.experimental import pallas as pl
from jax.experimental.pallas import tpu as pltpu

F32 = jnp.float32
BF16 = jnp.bfloat16
MESH = pl.DeviceIdType.MESH

DEPTH = 4
HEAD_DIM = 64
HEAD_PAIR = 2 * HEAD_DIM
RNN_BLOCK = 256
CONV_WIDTH = 4
LRU_C = 8.0
ALPHA = (2.0 * DEPTH) ** 0.25
LN_EPS = 1e-5
ADAM_LR, ADAM_B1, ADAM_B2, ADAM_EPS, ADAM_WD, ADAM_STEP = 0.001, 0.9, 0.999, 1e-08, 0.01, 10
N_CHIPS = 4
LANES = 1024
NEG = -1e30

NT_DIMS = (((1,), (1,)), ((), ()))
TN_DIMS = (((0,), (0,)), ((), ()))


def _sigmoid(z):
    return 1.0 / (1.0 + jnp.exp(-z))


def _softplus(z):
    return jnp.maximum(z, 0.0) + jnp.log(1.0 + jnp.exp(-jnp.abs(z)))


def _neg_expm1(y):
    poly = y * (1.0 + y * (0.5 + y * (1.0 / 6.0 + y * (1.0 / 24.0))))
    return -jnp.where(y > -0.05, poly, jnp.exp(y) - 1.0)


def _shift_down(cur, prev8, k):
    n = cur.shape[0]
    row = lax.broadcasted_iota(jnp.int32, cur.shape, 0)
    fix = jnp.tile(pltpu.roll(prev8, k, 0), (n // 8, 1))
    return jnp.where(row < k, fix, pltpu.roll(cur, k, 0))


def _shift_up(cur, next8, k):
    n = cur.shape[0]
    row = lax.broadcasted_iota(jnp.int32, cur.shape, 0)
    fix = jnp.tile(pltpu.roll(next8, 8 - k, 0), (n // 8, 1))
    return jnp.where(row >= n - k, fix, pltpu.roll(cur, n - k, 0))


def _proj(x, w, outs, name):
    s_len, d = x.shape
    tm = min(512, s_len)

    def body(x_ref, w_ref, *o_refs):
        xb = x_ref[...].astype(BF16)
        for (c0, wd, dt), o_ref in zip(outs, o_refs):
            step = min(wd, 512)
            for cc in range(0, wd, step):
                o_ref[:, cc:cc + step] = jnp.dot(
                    xb, w_ref[:, c0 + cc:c0 + cc + step], preferred_element_type=F32).astype(dt)

    return pl.pallas_call(
        body, name=name, grid=(s_len // tm,),
        in_specs=[pl.BlockSpec((tm, d), lambda i: (i, 0)), pl.BlockSpec(w.shape, lambda i: (0, 0))],
        out_specs=[pl.BlockSpec((tm, wd), lambda i: (i, 0)) for _, wd, _ in outs],
        out_shape=[jax.ShapeDtypeStruct((s_len, wd), dt) for _, wd, dt in outs],
        compiler_params=pltpu.CompilerParams(dimension_semantics=("parallel",)),
    )(x, w)


def _mm_nt(dz, pieces, w, name, outgoing=()):
    s_len, d = dz.shape
    tm = min(512, s_len)
    steps = s_len // tm
    n = len(pieces)
    ns = len(outgoing)
    cols = [(c0, p.shape[1]) for p, c0 in pieces]

    def body(*refs):
        dz_ref, p_refs, w_ref, o_ref = refs[0], refs[1:1 + n], refs[1 + n], refs[2 + n + ns]
        if ns:
            start, finish = _scatter_steps(refs[2 + n:2 + n + ns], refs[3 + n + ns:3 + n + 2 * ns],
                                           *refs[3 + n + 2 * ns:])
            pl.when(pl.program_id(0) == 0)(start)
        acc = ALPHA * dz_ref[...]
        for (c0, wd), p_ref in zip(cols, p_refs):
            acc = acc + lax.dot_general(p_ref[...], w_ref[:, c0:c0 + wd], NT_DIMS, preferred_element_type=F32)
        o_ref[...] = acc
        if ns:
            pl.when(pl.program_id(0) == steps - 1)(finish)

    out, *arrived = pl.pallas_call(
        body, name=name + "_scatter" if ns else name, grid=(steps,),
        in_specs=[pl.BlockSpec((tm, d), lambda i: (i, 0))]
        + [pl.BlockSpec((tm, wd), lambda i: (i, 0)) for _, wd in cols]
        + [pl.BlockSpec(w.shape, lambda i: (0, 0))] + [ANY] * ns,
        out_specs=[pl.BlockSpec((tm, d), lambda i: (i, 0))] + [ANY] * ns,
        out_shape=[jax.ShapeDtypeStruct((s_len, d), F32)] + [jax.ShapeDtypeStruct(t.shape, t.dtype) for t in outgoing],
        scratch_shapes=[pltpu.SemaphoreType.DMA((3 * ns,))] * 2 if ns else [],
        compiler_params=pltpu.CompilerParams(dimension_semantics=("arbitrary" if ns else "parallel",)),
    )(dz, *[p for p, _ in pieces], w, *outgoing)
    return (out, *arrived)


ROWS_ATTN_IN = (0, 2048)
ROWS_ATTN_OUT = (1024, 1280)
ROWS_RNN_OUT = (1536, 1792)
ROWS_RNN_IN = (3072, 3584)
ROWS_GATES = 4096
LATE_ROWS = 1280
SLOT_ROWS = 4352


DW_ROWS = 1024


def _dw_call(body, name, slots, operands, specs, out_block, out_index, grid, semantics):
    return pl.pallas_call(
        body, name=name, grid=grid,
        in_specs=specs if slots is None else [ANY] + specs,
        out_specs=pl.BlockSpec(out_block, out_index),
        out_shape=jax.ShapeDtypeStruct((N_CHIPS, SLOT_ROWS, LANES), F32),
        input_output_aliases={} if slots is None else {0: 0},
        compiler_params=pltpu.CompilerParams(dimension_semantics=semantics),
    )(*(operands if slots is None else [slots] + operands))


def _dw_attn_in(slots, x, pieces, forget, layer):
    s_len, d = x.shape
    ts = min(s_len, DW_ROWS)
    steps = s_len // ts
    half = d // 2

    def body(g_ref, x_ref, p0, p1, p2, p3, f_ref, o_ref, wf_ref):
        lanes, step = pl.program_id(0), pl.program_id(1)

        @pl.when(step == 0)
        def _():
            o_ref[...] = jnp.zeros_like(o_ref)

        xb = x_ref[...].astype(BF16)
        for j, p_ref in enumerate((p0, p1, p2, p3)):
            o_ref[j] += lax.dot_general(xb, p_ref[...], TN_DIMS, preferred_element_type=F32)

        @pl.when(step == steps - 1)
        def _():
            o_ref[0] = o_ref[0] * (HEAD_DIM ** -0.5)

        @pl.when(lanes == 0)
        def _():
            @pl.when(step == 0)
            def _():
                wf_ref[...] = jnp.zeros_like(wf_ref)

            wf_ref[...] += lax.dot_general(xb, f_ref[...], TN_DIMS, preferred_element_type=F32)

    return pl.pallas_call(
        body, name="dw_attn_in", grid=(2, steps),
        in_specs=[ANY, pl.BlockSpec((ts, d), lambda i, s: (s, 0))] + [pl.BlockSpec((ts, half), lambda i, s: (s, i))] * 4
        + [pl.BlockSpec((ts, 128), lambda i, s: (s, 0))],
        out_specs=[pl.BlockSpec((N_CHIPS, d, half), lambda i, s: (0, ROWS_ATTN_IN[layer] // d, i)),
                   pl.BlockSpec((d, 128), lambda i, s: (0, 0))],
        out_shape=[jax.ShapeDtypeStruct((N_CHIPS, SLOT_ROWS, LANES), F32), jax.ShapeDtypeStruct((d, 128), F32)],
        input_output_aliases={0: 0},
        compiler_params=pltpu.CompilerParams(dimension_semantics=("arbitrary", "arbitrary")),
    )(slots, x, *pieces, forget)


def _dw_out(slots, yg, dz, first_row):
    s_len, d = dz.shape
    ts = min(s_len, DW_ROWS)
    rows = d // N_CHIPS

    def body(*refs):
        a_ref, b_ref, o_ref = refs[-3:]

        @pl.when(pl.program_id(0) == 0)
        def _():
            o_ref[...] = jnp.zeros_like(o_ref)

        prod = lax.dot_general(a_ref[...], b_ref[...].astype(BF16), TN_DIMS, preferred_element_type=F32)
        o_ref[...] += prod.reshape(N_CHIPS, rows, d)

    specs = [pl.BlockSpec((ts, d), lambda s: (s, 0))] * 2
    return _dw_call(body, "dw_out", slots, [yg, dz], specs, (N_CHIPS, rows, d), lambda s: (0, first_row // rows, 0),
                    (s_len // ts,), ("arbitrary",))


def _dw_rnn_in(slots, x, parts, layer):
    s_len, d = x.shape
    ts = min(s_len, DW_ROWS)
    half = d // 2

    def body(*refs):
        x_ref, p_refs, o_ref = refs[-4], refs[-3:-1], refs[-1]

        @pl.when(pl.program_id(0) == 0)
        def _():
            o_ref[...] = jnp.zeros_like(o_ref)

        xb = x_ref[...].astype(BF16)
        for p, p_ref in enumerate(p_refs):
            for jj in (0, 1):
                for r in (0, 1):
                    o_ref[2 * p + jj, :, r * half:(r + 1) * half] += lax.dot_general(
                        xb[:, r * half:(r + 1) * half], p_ref[:, jj * half:(jj + 1) * half], TN_DIMS,
                        preferred_element_type=F32)

    specs = [pl.BlockSpec((ts, d), lambda s: (s, 0))] * 3
    return _dw_call(body, "dw_rnn_in", slots, [x] + list(parts), specs, (N_CHIPS, half, d),
                    lambda s: (0, ROWS_RNN_IN[layer] // half, 0), (s_len // ts,), ("arbitrary",))


def _fcum(fl, bias):
    s_len = fl.shape[0]

    def body(fl_ref, b_ref, cum_ref, sg_ref):
        z = fl_ref[...] + b_ref[...]
        e = jnp.exp(-jnp.abs(z))
        logf = jnp.minimum(z, 0.0) - jnp.log(1.0 + e)
        sneg = jnp.where(z >= 0, e, 1.0) / (1.0 + e)
        run = logf.T[0:16, :]
        sg_ref[...] = sneg.T[0:16, :]
        lane = lax.broadcasted_iota(jnp.int32, (16, s_len), 1)
        sh = 1
        while sh < s_len:
            run = run + jnp.where(lane >= sh, pltpu.roll(run, sh, 1), 0.0)
            sh *= 2
        cum_ref[...] = run

    return pl.pallas_call(
        body, name="fcum",
        out_shape=[jax.ShapeDtypeStruct((16, s_len), F32), jax.ShapeDtypeStruct((16, s_len), F32)],
    )(fl, bias)


def _fbwd(dcum, sneg):
    s_len = dcum.shape[1]

    def body(dc_ref, sg_ref, dl_ref, db_ref):
        run = dc_ref[...]
        lane = lax.broadcasted_iota(jnp.int32, (16, s_len), 1)
        sh = 1
        while sh < s_len:
            run = run + jnp.where(lane < s_len - sh, pltpu.roll(run, s_len - sh, 1), 0.0)
            sh *= 2
        dlog = run * sg_ref[...]
        db_ref[...] = jnp.broadcast_to(jnp.sum(dlog, axis=1, keepdims=True), (16, 128))
        full = jnp.concatenate([dlog, jnp.zeros((112, s_len), F32)], axis=0)
        dl_ref[...] = full.T.astype(BF16)

    return pl.pallas_call(
        body, name="fbwd",
        out_shape=[jax.ShapeDtypeStruct((s_len, 128), BF16), jax.ShapeDtypeStruct((16, 128), F32)],
    )(dcum, sneg)


FWD_TILE = 1024
BWD_TILE = 1024


def _flash_fwd(q, k, v, cum, shards=()):
    s_len, width = q.shape
    tile = min(FWD_TILE, s_len // 2)
    nt = s_len // tile
    reps = tile // 128
    cumr = cum.reshape(-1, tile)
    ns = len(shards)
    pairs = width // HEAD_PAIR

    def body(*refs):
        q_ref, k_ref, v_ref, cum_ref = refs[:4]
        o_ref, lse_ref = refs[4 + ns:6 + ns]
        q_t, v_t, cum_col = refs[6 + 2 * ns:9 + 2 * ns]
        pid = pl.program_id(0)
        if ns:
            start, forward, finish = _gather_steps(refs[4:4 + ns], refs[6 + ns:6 + 2 * ns], *refs[9 + 2 * ns:])
            pl.when(pid == 0)(start)
            pl.when(pid == pairs - 3)(forward)
        top = lax.broadcasted_iota(jnp.int32, (HEAD_PAIR, tile), 0) < HEAD_DIM
        causal = (lax.broadcasted_iota(jnp.int32, (tile, tile), 0)
                  <= lax.broadcasted_iota(jnp.int32, (tile, tile), 1))

        def pre(i, carry):
            r0 = pl.multiple_of(i * tile, tile)
            q_t[i] = q_ref[pl.ds(r0, tile), :].astype(F32).T.astype(BF16)
            v_t[i] = v_ref[pl.ds(r0, tile), :].astype(F32).T.astype(BF16)
            for h in (0, 1):
                row = cum_ref[pl.ds((2 * pid + h) * nt + i, 1), :]
                cum_col[h, pl.ds(r0, tile), :] = jnp.broadcast_to(row, (HEAD_PAIR, tile)).T
            return carry

        lax.fori_loop(0, nt, pre, 0)

        def q_loop(qi, carry):
            qt = q_t[qi]
            zt = jnp.zeros_like(qt)
            qms = (jnp.where(top, qt, zt), jnp.where(top, zt, qt))

            def step(kj, state, masked):
                m0, l0, m1, l1, acc = state
                k0 = pl.multiple_of(kj * tile, tile)
                kt = k_ref[pl.ds(k0, tile), :]
                vt = v_t[kj]
                ms, ls, scales, contrib = [m0, m1], [l0, l1], [], None
                for h in (0, 1):
                    s = jnp.dot(kt, qms[h], preferred_element_type=F32)
                    s = s - jnp.tile(cum_col[h, pl.ds(k0, tile), :], (1, reps))
                    if masked:
                        s = jnp.where(causal, s, NEG)
                    m_new = jnp.maximum(ms[h], jnp.max(s, axis=0, keepdims=True))
                    p = jnp.exp(s - m_new)
                    scale = jnp.exp(ms[h] - m_new)
                    ls[h] = scale * ls[h] + jnp.sum(p, axis=0, keepdims=True)
                    ms[h] = m_new
                    scales.append(scale)
                    vm = jnp.where(top, vt, zt) if h == 0 else jnp.where(top, zt, vt)
                    part = jnp.dot(vm, p.astype(BF16), preferred_element_type=F32)
                    contrib = part if contrib is None else contrib + part
                acc = jnp.where(top, scales[0], scales[1]) * acc + contrib
                return ms[0], ls[0], ms[1], ls[1], acc

            low = jnp.full((1, tile), NEG, F32)
            zero = jnp.zeros((1, tile), F32)
            state = step(qi, (low, zero, low, zero, jnp.zeros((HEAD_PAIR, tile), F32)), True)
            m0, l0, m1, l1, acc = lax.fori_loop(0, qi, lambda kj, c: step(kj, c, False), state)
            q0 = pl.multiple_of(qi * tile, tile)
            o_ref[pl.ds(q0, tile), :] = (acc / jnp.where(top, l0, l1)).T
            lse_ref[pl.ds((2 * pid) * nt + qi, 1), :] = m0 + jnp.log(l0)
            lse_ref[pl.ds((2 * pid + 1) * nt + qi, 1), :] = m1 + jnp.log(l1)
            return carry

        lax.fori_loop(0, nt, q_loop, 0)
        if ns:
            pl.when(pid == pairs - 1)(finish)

    blk = pl.BlockSpec((s_len, HEAD_PAIR), lambda p: (0, p))
    full = pl.BlockSpec(cumr.shape, lambda p: (0, 0))
    sems = [pltpu.SemaphoreType.DMA((GATHER_SEMS * ns,))] * 2 if ns else []
    o, lse, *gathered = pl.pallas_call(
        body, name="flash_fwd_gather" if ns else "flash_fwd", grid=(pairs,),
        in_specs=[blk, blk, blk, full] + [ANY] * ns,
        out_specs=[blk, full] + [ANY] * ns,
        out_shape=[jax.ShapeDtypeStruct((s_len, width), F32), jax.ShapeDtypeStruct(cumr.shape, F32)]
        + [jax.ShapeDtypeStruct((N_CHIPS,) + s.shape, s.dtype) for s in shards],
        scratch_shapes=[pltpu.VMEM((nt, HEAD_PAIR, tile), BF16), pltpu.VMEM((nt, HEAD_PAIR, tile), BF16),
                        pltpu.VMEM((2, s_len, HEAD_PAIR), F32)] + sems,
        compiler_params=pltpu.CompilerParams(dimension_semantics=("arbitrary",)),
    )(q, k, v, cumr, *shards)
    return (o, lse.reshape(cum.shape), *gathered)


def _flash_bwd(q, k, v, o, do, lse, cum, outgoing=()):
    s_len, width = q.shape
    tile = min(BWD_TILE, s_len // 2)
    nt = s_len // tile
    reps = tile // 128
    cumr = cum.reshape(-1, tile)
    lse = lse.reshape(-1, tile)
    ns = len(outgoing)
    pairs = width // HEAD_PAIR

    def body(*refs):
        q_ref, k_ref, v_ref, o_ref, do_ref, lse_ref, cum_ref = refs[:7]
        dq_ref, dk_ref, dv_ref, dcum_ref = refs[7 + ns:11 + ns]
        (q_t, do_t, k_t, do_bf, cum_col, dq_t_acc, delta, q_side, dk_acc, dv_acc,
         k_side) = refs[11 + 2 * ns:22 + 2 * ns]
        pid = pl.program_id(0)
        if ns:
            start, finish = _scatter_steps(refs[7:7 + ns], refs[11 + ns:11 + 2 * ns], *refs[22 + 2 * ns:])
            pl.when(pid == 0)(start)
        top = lax.broadcasted_iota(jnp.int32, (HEAD_PAIR, tile), 0) < HEAD_DIM
        left = lax.broadcasted_iota(jnp.int32, (tile, HEAD_PAIR), 1) < HEAD_DIM
        causal = (lax.broadcasted_iota(jnp.int32, (tile, tile), 0)
                  <= lax.broadcasted_iota(jnp.int32, (tile, tile), 1))

        def pre(i, carry):
            r0 = pl.multiple_of(i * tile, tile)
            d_o = do_ref[pl.ds(r0, tile), :]
            prod_t = (d_o * o_ref[pl.ds(r0, tile), :]).T
            delta[pl.ds(i, 1), :] = jnp.sum(prod_t[:HEAD_DIM], axis=0, keepdims=True)
            delta[pl.ds(nt + i, 1), :] = jnp.sum(prod_t[HEAD_DIM:], axis=0, keepdims=True)
            do_bf[pl.ds(r0, tile), :] = d_o.astype(BF16)
            do_t[i] = d_o.T.astype(BF16)
            q_t[i] = q_ref[pl.ds(r0, tile), :].astype(F32).T.astype(BF16)
            k_t[i] = k_ref[pl.ds(r0, tile), :].astype(F32).T.astype(BF16)
            dq_t_acc[i] = jnp.zeros((HEAD_PAIR, tile), F32)
            for h in (0, 1):
                row = cum_ref[pl.ds((2 * pid + h) * nt + i, 1), :]
                cum_col[h, pl.ds(r0, tile), :] = jnp.broadcast_to(row, (HEAD_PAIR, tile)).T
                q_side[pl.ds(h * nt + i, 1), :] = jnp.zeros((1, tile), F32)
            return carry

        lax.fori_loop(0, nt, pre, 0)

        def kv_loop(kj, carry):
            k0 = pl.multiple_of(kj * tile, tile)
            kt = k_ref[pl.ds(k0, tile), :]
            vt = v_ref[pl.ds(k0, tile), :]
            ktt = k_t[kj]
            zt = jnp.zeros_like(ktt)
            zr = jnp.zeros_like(kt)
            kms = (jnp.where(top, ktt, zt), jnp.where(top, zt, ktt))
            cols = [jnp.tile(cum_col[h, pl.ds(k0, tile), :], (1, reps)) for h in (0, 1)]
            dk_acc[...] = jnp.zeros_like(dk_acc)
            dv_acc[...] = jnp.zeros_like(dv_acc)
            k_side[...] = jnp.zeros_like(k_side)

            def step(qi, carry, masked):
                q0 = pl.multiple_of(qi * tile, tile)
                qtt = q_t[qi]
                dtt = do_t[qi]
                q_rows = q_ref[pl.ds(q0, tile), :]
                do_rows = do_bf[pl.ds(q0, tile), :]
                dq_part = None
                for h in (0, 1):
                    q_m = jnp.where(top, qtt, zt) if h == 0 else jnp.where(top, zt, qtt)
                    do_m = jnp.where(top, dtt, zt) if h == 0 else jnp.where(top, zt, dtt)
                    s = jnp.dot(kt, q_m, preferred_element_type=F32) - cols[h]
                    if masked:
                        s = jnp.where(causal, s, NEG)
                    p = jnp.exp(s - lse_ref[pl.ds((2 * pid + h) * nt + qi, 1), :])
                    dp = jnp.dot(vt, do_m, preferred_element_type=F32)
                    ds = p * (dp - delta[pl.ds(h * nt + qi, 1), :])
                    q_side[pl.ds(h * nt + qi, 1), :] += jnp.sum(ds, axis=0, keepdims=True)
                    folded = ds[:, :128]
                    for b in range(1, reps):
                        folded = folded + ds[:, b * 128:(b + 1) * 128]
                    k_side[h] += folded
                    pb = p.astype(BF16)
                    dsb = ds.astype(BF16)
                    do_h = jnp.where(left, do_rows, zr) if h == 0 else jnp.where(left, zr, do_rows)
                    q_h = jnp.where(left, q_rows, zr) if h == 0 else jnp.where(left, zr, q_rows)
                    dv_acc[...] += jnp.dot(pb, do_h, preferred_element_type=F32)
                    dk_acc[...] += jnp.dot(dsb, q_h, preferred_element_type=F32)
                    part = jnp.dot(kms[h], dsb, preferred_element_type=F32)
                    dq_part = part if dq_part is None else dq_part + part
                dq_t_acc[qi] += dq_part
                return carry

            step(kj, 0, True)
            lax.fori_loop(kj + 1, nt, lambda qi, c: step(qi, c, False), 0)
            dk_ref[pl.ds(k0, tile), :] = dk_acc[...].astype(BF16)
            dv_ref[pl.ds(k0, tile), :] = dv_acc[...].astype(BF16)
            for h in (0, 1):
                dcum_ref[pl.ds((2 * pid + h) * nt + kj, 1), :] = -jnp.sum(k_side[h].T, axis=0, keepdims=True)
            return carry

        lax.fori_loop(0, nt, kv_loop, 0)

        def fin(i, carry):
            r0 = pl.multiple_of(i * tile, tile)
            dq_ref[pl.ds(r0, tile), :] = dq_t_acc[i].T.astype(BF16)
            for h in (0, 1):
                dcum_ref[pl.ds((2 * pid + h) * nt + i, 1), :] += q_side[pl.ds(h * nt + i, 1), :]
            return carry

        lax.fori_loop(0, nt, fin, 0)
        if ns:
            pl.when(pid == pairs - 1)(finish)

    blk = pl.BlockSpec((s_len, HEAD_PAIR), lambda p: (0, p))
    full = pl.BlockSpec(cumr.shape, lambda p: (0, 0))
    transposed = pltpu.VMEM((nt, HEAD_PAIR, tile), BF16)
    sems = [pltpu.SemaphoreType.DMA((3 * ns,))] * 2 if ns else []
    dq, dk, dv, dcum, *arrived = pl.pallas_call(
        body, name="flash_bwd_scatter" if ns else "flash_bwd", grid=(pairs,),
        in_specs=[blk, blk, blk, blk, blk, full, full] + [ANY] * ns,
        out_specs=[blk, blk, blk, full] + [ANY] * ns,
        out_shape=[jax.ShapeDtypeStruct((s_len, width), BF16)] * 3 + [jax.ShapeDtypeStruct(cumr.shape, F32)]
        + [jax.ShapeDtypeStruct(t.shape, t.dtype) for t in outgoing],
        scratch_shapes=[transposed, transposed, transposed, pltpu.VMEM((s_len, HEAD_PAIR), BF16),
                        pltpu.VMEM((2, s_len, HEAD_PAIR), F32), pltpu.VMEM((nt, HEAD_PAIR, tile), F32),
                        pltpu.VMEM((2 * nt, tile), F32), pltpu.VMEM((2 * nt, tile), F32),
                        pltpu.VMEM((tile, HEAD_PAIR), F32), pltpu.VMEM((tile, HEAD_PAIR), F32),
                        pltpu.VMEM((2, tile, HEAD_PAIR), F32)] + sems,
        compiler_params=pltpu.CompilerParams(dimension_semantics=("arbitrary",)),
    )(q, k, v, o, do, lse, cumr, *outgoing)
    return (dq, dk, dv, dcum.reshape(cum.shape), *arrived)


def _out_ln(a, gate, x, w, g, b, name, target=None):
    s_len, d = x.shape
    tm = min(512, s_len)
    nt = 0 if target is None else 1

    def body(*refs):
        a_ref, gate_ref, x_ref, w_ref, g_ref, b_ref = refs[:6]
        first_ref, xh_ref, rs_ref, yg_ref = refs[6 + nt:10 + nt]
        gt = gate_ref[...]
        yg = (a_ref[...] * (gt * _sigmoid(gt))).astype(BF16)
        yg_ref[...] = yg
        z = ALPHA * x_ref[...] + jnp.dot(yg, w_ref[...], preferred_element_type=F32)
        zc = z - jnp.mean(z, axis=1, keepdims=True)
        rstd = lax.rsqrt(jnp.mean(zc * zc, axis=1, keepdims=True) + LN_EPS)
        xh = zc * rstd
        xh_ref[...] = xh
        y = xh * g_ref[...] + b_ref[...]
        rs_ref[...] = jnp.broadcast_to(rstd, (tm, 128))
        if nt:
            sq_ref = refs[10 + nt]

            @pl.when(pl.program_id(0) == 0)
            def _():
                sq_ref[...] = jnp.zeros_like(sq_ref)

            diff = y - refs[6][...]
            first_ref[...] = diff * (1.0 / d)
            sq_ref[...] += jnp.sum(diff * diff, axis=0, keepdims=True)
        else:
            first_ref[...] = y

    row = pl.BlockSpec((tm, d), lambda i: (i, 0))
    vec = pl.BlockSpec((1, d), lambda i: (0, 0))
    return pl.pallas_call(
        body, name=name + "_loss" if nt else name, grid=(s_len // tm,),
        in_specs=[row, row, row, pl.BlockSpec(w.shape, lambda i: (0, 0)), vec, vec] + [row] * nt,
        out_specs=[row, row, pl.BlockSpec((tm, 128), lambda i: (i, 0)), row] + [vec] * nt,
        out_shape=[jax.ShapeDtypeStruct((s_len, d), F32), jax.ShapeDtypeStruct((s_len, d), F32),
                   jax.ShapeDtypeStruct((s_len, 128), F32), jax.ShapeDtypeStruct((s_len, d), BF16)]
        + [jax.ShapeDtypeStruct((1, d), F32)] * nt,
        compiler_params=pltpu.CompilerParams(dimension_semantics=("arbitrary" if nt else "parallel",)),
    )(a, gate, x, w, g, b, *([target] if nt else []))


def _ln_bwd(dout, xh, rs, g, w, gate, a, name, swap=None):
    s_len, d = dout.shape
    tm = min(512, s_len)
    steps = s_len // tm
    ns = 0 if swap is None else 1

    def body(*refs):
        do_ref, xh_ref, rs_ref, g_ref, w_ref, gate_ref, a_ref = refs[:7]
        dz_ref, da_ref, dgate_ref, dg_ref, db_ref = refs[7 + ns:12 + ns]
        if ns:
            start, finish = _swap_steps(refs[7:8], refs[12 + ns:13 + ns], [swap[1]], *refs[13 + ns:])
            pl.when(pl.program_id(0) == 0)(start)

        @pl.when(pl.program_id(0) == 0)
        def _():
            dg_ref[...] = jnp.zeros_like(dg_ref)
            db_ref[...] = jnp.zeros_like(db_ref)

        dout_t = do_ref[...]
        xh_t = xh_ref[...]
        dg_ref[...] += jnp.sum(dout_t * xh_t, axis=0, keepdims=True)
        db_ref[...] += jnp.sum(dout_t, axis=0, keepdims=True)
        dxh = dout_t * g_ref[...]
        m1 = jnp.mean(dxh, axis=1, keepdims=True)
        m2 = jnp.mean(dxh * xh_t, axis=1, keepdims=True)
        dz = rs_ref[:, 0:1] * (dxh - m1 - xh_t * m2)
        dz_ref[...] = dz
        dyg = lax.dot_general(dz.astype(BF16), w_ref[...], NT_DIMS, preferred_element_type=F32)
        gt = gate_ref[...]
        sg = _sigmoid(gt)
        da_ref[...] = dyg * (gt * sg)
        dgate_ref[...] = (dyg * a_ref[...] * (sg * (1.0 + gt * (1.0 - sg)))).astype(BF16)
        if ns:
            pl.when(pl.program_id(0) == steps - 1)(finish)

    row = pl.BlockSpec((tm, d), lambda i: (i, 0))
    vec = pl.BlockSpec((1, d), lambda i: (0, 0))
    extra_out = [jax.ShapeDtypeStruct((N_CHIPS, swap[1][1], LANES), swap[0].dtype)] if ns else []
    return pl.pallas_call(
        body, name=name + "_swap" if ns else name, grid=(steps,),
        in_specs=[row, row, pl.BlockSpec((tm, 128), lambda i: (i, 0)), vec, pl.BlockSpec(w.shape, lambda i: (0, 0)),
                  row, row] + [ANY] * ns,
        out_specs=[row, row, row, vec, vec] + [ANY] * ns,
        out_shape=[jax.ShapeDtypeStruct((s_len, d), F32), jax.ShapeDtypeStruct((s_len, d), F32),
                   jax.ShapeDtypeStruct((s_len, d), BF16), jax.ShapeDtypeStruct((1, d), F32),
                   jax.ShapeDtypeStruct((1, d), F32)] + extra_out,
        scratch_shapes=[pltpu.SemaphoreType.DMA((N_CHIPS,))] * 2 if ns else [],
        compiler_params=pltpu.CompilerParams(dimension_semantics=("arbitrary",)),
    )(dout, xh, rs, g, w, gate, a, *([swap[0]] if ns else []))


def _rnn_fwd(u, conv_w, conv_b, wa, ba, wi, bi, lam):
    s_len, width = u.shape
    tm = min(512, s_len // 2)
    nb = width // RNN_BLOCK

    def body(u_ref, up_ref, cw_ref, cb_ref, wa_ref, ba_ref, wi_ref, bi_ref, lam_ref,
             h_ref, uc_ref, r_ref, i_ref, a_ref, b_scr, h_carry):
        step_id = pl.program_id(0)

        @pl.when(step_id == 0)
        def _():
            h_carry[...] = jnp.zeros_like(h_carry)

        for n in range(nb):
            blk = slice(n * RNN_BLOCK, (n + 1) * RNN_BLOCK)
            ut = u_ref[:, blk]
            prev = jnp.where(step_id > 0, up_ref[:, blk], 0.0)
            uc = cb_ref[:, blk] + cw_ref[3:4, blk] * ut
            for k in (1, 2, 3):
                uc = uc + cw_ref[3 - k:4 - k, blk] * _shift_down(ut, prev, k)
            ucb = uc.astype(BF16)
            r = _sigmoid(jnp.dot(ucb, wa_ref[n], preferred_element_type=F32) + ba_ref[:, blk])
            ig = _sigmoid(jnp.dot(ucb, wi_ref[n], preferred_element_type=F32) + bi_ref[:, blk])
            log_a = -LRU_C * r * _softplus(-lam_ref[:, blk])
            uc_ref[:, blk] = uc
            r_ref[:, blk] = r
            i_ref[:, blk] = ig
            a_ref[:, blk] = jnp.exp(log_a)
            b_scr[:, blk] = jnp.sqrt(_neg_expm1(2.0 * log_a)) * (ig * uc)

        def scan(t, h):
            h = a_ref[pl.ds(t, 1), :] * h + b_scr[pl.ds(t, 1), :]
            h_ref[pl.ds(t, 1), :] = h
            return h

        h_carry[...] = lax.fori_loop(0, tm, scan, h_carry[...], unroll=8)

    row = pl.BlockSpec((tm, width), lambda i: (i, 0))
    prev8 = pl.BlockSpec((8, width), lambda i: (jnp.maximum(i * (tm // 8) - 1, 0), 0))
    vec = pl.BlockSpec((1, width), lambda i: (0, 0))
    mat = pl.BlockSpec(wa.shape, lambda i: (0, 0, 0))
    return pl.pallas_call(
        body, name="rnn_fwd", grid=(s_len // tm,),
        in_specs=[row, prev8, pl.BlockSpec(conv_w.shape, lambda i: (0, 0)), vec, mat, vec, mat, vec, vec],
        out_specs=[row] * 5,
        out_shape=[jax.ShapeDtypeStruct((s_len, width), F32)] * 5,
        scratch_shapes=[pltpu.VMEM((tm, width), F32), pltpu.VMEM((1, width), F32)],
        compiler_params=pltpu.CompilerParams(dimension_semantics=("arbitrary",)),
    )(u, u, conv_w, conv_b, wa, ba, wi, bi, lam)


def _rnn_bwd(dh, a, h, r, ig, uc, lam, wa, wi):
    s_len, width = dh.shape
    tm = min(512, s_len // 2)
    nt = s_len // tm
    nb = width // RNN_BLOCK

    def body(dh_ref, a_ref, h_ref, hp_ref, r_ref, i_ref, uc_ref, lam_ref, wa_ref, wi_ref,
             duc_ref, dwa_ref, dwi_ref, dba_ref, dbi_ref, dlam_ref, g_scr, c_scr, dsp_scr):
        step_id = pl.program_id(0)
        tile_id = nt - 1 - step_id

        @pl.when(step_id == 0)
        def _():
            c_scr[...] = jnp.zeros_like(c_scr)
            dsp_scr[...] = jnp.zeros_like(dsp_scr)
            dwa_ref[...] = jnp.zeros_like(dwa_ref)
            dwi_ref[...] = jnp.zeros_like(dwi_ref)
            dba_ref[...] = jnp.zeros_like(dba_ref)
            dbi_ref[...] = jnp.zeros_like(dbi_ref)

        def scan(j, c):
            t = tm - 1 - j
            g = dh_ref[pl.ds(t, 1), :] + c
            g_scr[pl.ds(t, 1), :] = g
            return a_ref[pl.ds(t, 1), :] * g

        c_scr[...] = lax.fori_loop(0, tm, scan, c_scr[...], unroll=8)

        for n in range(nb):
            blk = slice(n * RNN_BLOCK, (n + 1) * RNN_BLOCK)
            g_t = g_scr[:, blk]
            hp8 = jnp.where(tile_id > 0, hp_ref[:, blk], 0.0)
            h_prev = _shift_down(h_ref[:, blk], hp8, 1)
            av, rv, iv, ucv = a_ref[:, blk], r_ref[:, blk], i_ref[:, blk], uc_ref[:, blk]
            sp = _softplus(-lam_ref[:, blk])
            mag = jnp.sqrt(_neg_expm1(-2.0 * LRU_C * rv * sp))
            d_iu = g_t * mag
            dla = g_t * h_prev * av - g_t * (iv * ucv) * (av * av) / mag
            dsp_scr[:, blk] += jnp.sum(dla * (-LRU_C * rv), axis=0, keepdims=True)
            dra = dla * (-LRU_C * sp) * rv * (1.0 - rv)
            dia = d_iu * ucv * iv * (1.0 - iv)
            drab, diab, ucb = dra.astype(BF16), dia.astype(BF16), ucv.astype(BF16)
            duc_ref[:, blk] = (d_iu * iv
                               + lax.dot_general(drab, wa_ref[n], NT_DIMS, preferred_element_type=F32)
                               + lax.dot_general(diab, wi_ref[n], NT_DIMS, preferred_element_type=F32))
            dwa_ref[n] += lax.dot_general(ucb, drab, TN_DIMS, preferred_element_type=F32)
            dwi_ref[n] += lax.dot_general(ucb, diab, TN_DIMS, preferred_element_type=F32)
            dba_ref[:, blk] += jnp.sum(dra, axis=0, keepdims=True)
            dbi_ref[:, blk] += jnp.sum(dia, axis=0, keepdims=True)

        @pl.when(step_id == nt - 1)
        def _():
            dlam_ref[...] = -dsp_scr[...] * _sigmoid(-lam_ref[...])

    row = pl.BlockSpec((tm, width), lambda i: (nt - 1 - i, 0))
    prev8 = pl.BlockSpec((8, width), lambda i: (jnp.maximum((nt - 1 - i) * (tm // 8) - 1, 0), 0))
    vec = pl.BlockSpec((1, width), lambda i: (0, 0))
    mat = pl.BlockSpec(wa.shape, lambda i: (0, 0, 0))
    return pl.pallas_call(
        body, name="rnn_bwd", grid=(nt,),
        in_specs=[row, row, row, prev8, row, row, row, vec, mat, mat],
        out_specs=[row, mat, mat, vec, vec, vec],
        out_shape=[jax.ShapeDtypeStruct((s_len, width), F32), jax.ShapeDtypeStruct(wa.shape, F32),
                   jax.ShapeDtypeStruct(wa.shape, F32)] + [jax.ShapeDtypeStruct((1, width), F32)] * 3,
        scratch_shapes=[pltpu.VMEM((tm, width), F32), pltpu.VMEM((1, width), F32), pltpu.VMEM((1, width), F32)],
        compiler_params=pltpu.CompilerParams(dimension_semantics=("arbitrary",)),
    )(dh, a, h, h, r, ig, uc, lam, wa, wi)


def _conv_bwd(duc, u, conv_w):
    s_len, width = duc.shape
    tm = min(256, s_len)
    nt = s_len // tm

    def body(d_ref, dn_ref, u_ref, up_ref, cw_ref, du_ref, dcw_ref, dcb_ref):
        step_id = pl.program_id(0)

        @pl.when(step_id == 0)
        def _():
            dcw_ref[...] = jnp.zeros_like(dcw_ref)
            dcb_ref[...] = jnp.zeros_like(dcb_ref)

        for n in range(width // RNN_BLOCK):
            blk = slice(n * RNN_BLOCK, (n + 1) * RNN_BLOCK)
            dt = d_ref[:, blk]
            ut = u_ref[:, blk]
            nxt = jnp.where(step_id < nt - 1, dn_ref[:, blk], 0.0)
            prev = jnp.where(step_id > 0, up_ref[:, blk], 0.0)
            du = cw_ref[3:4, blk] * dt
            dcw_ref[3:4, blk] += jnp.sum(dt * ut, axis=0, keepdims=True)
            for k in (1, 2, 3):
                du = du + cw_ref[3 - k:4 - k, blk] * _shift_up(dt, nxt, k)
                dcw_ref[3 - k:4 - k, blk] += jnp.sum(dt * _shift_down(ut, prev, k), axis=0, keepdims=True)
            du_ref[:, blk] = du.astype(BF16)
            dcb_ref[:, blk] += jnp.sum(dt, axis=0, keepdims=True)

    row = pl.BlockSpec((tm, width), lambda i: (i, 0))
    prev8 = pl.BlockSpec((8, width), lambda i: (jnp.maximum(i * (tm // 8) - 1, 0), 0))
    next8 = pl.BlockSpec((8, width), lambda i: (jnp.minimum((i + 1) * (tm // 8), s_len // 8 - 1), 0))
    return pl.pallas_call(
        body, name="conv_bwd", grid=(nt,),
        in_specs=[row, next8, row, prev8, pl.BlockSpec(conv_w.shape, lambda i: (0, 0))],
        out_specs=[row, pl.BlockSpec(conv_w.shape, lambda i: (0, 0)), pl.BlockSpec((1, width), lambda i: (0, 0))],
        out_shape=[jax.ShapeDtypeStruct((s_len, width), BF16), jax.ShapeDtypeStruct(conv_w.shape, F32),
                   jax.ShapeDtypeStruct((1, width), F32)],
        compiler_params=pltpu.CompilerParams(dimension_semantics=("arbitrary",)),
    )(duc, duc, u, u, conv_w)


def _pair_sum(core, grads, theirs, first_row, out_dtype):
    n, half, _ = theirs.shape
    tb = 128 if half % 128 == 0 and first_row % 128 == 0 else half
    assert first_row % tb == 0 and half % tb == 0

    def body(c_ref, a_ref, b_ref, o_ref):
        o_ref[...] = (a_ref[...] + b_ref[...]).astype(out_dtype)

    blk = pl.BlockSpec((n, tb, LANES), lambda i, c: (0, i, 0))
    mine = pl.BlockSpec((n, tb, LANES), lambda i, c: (0, first_row // tb + c[0] * (half // tb) + i, 0))
    return pl.pallas_call(
        body, name="pair_sum",
        grid_spec=pltpu.PrefetchScalarGridSpec(
            num_scalar_prefetch=1, grid=(half // tb,), in_specs=[mine, blk], out_specs=blk),
        out_shape=jax.ShapeDtypeStruct((n, half, LANES), out_dtype),
        compiler_params=pltpu.CompilerParams(dimension_semantics=("parallel",)),
    )(core, grads, theirs)


def _sum_chips(parts):
    rows = parts.shape[1]
    tb = 128 if rows % 128 == 0 else rows

    def body(p_ref, o_ref):
        o_ref[...] = ((p_ref[0].astype(F32) + p_ref[1].astype(F32)) + p_ref[2].astype(F32)) + p_ref[3].astype(F32)

    return pl.pallas_call(
        body, name="chip_sum", grid=(rows // tb,),
        in_specs=[pl.BlockSpec((N_CHIPS, tb, LANES), lambda i: (0, i, 0))],
        out_specs=pl.BlockSpec((tb, LANES), lambda i: (i, 0)),
        out_shape=jax.ShapeDtypeStruct((rows, LANES), F32),
        compiler_params=pltpu.CompilerParams(dimension_semantics=("parallel",)),
    )(parts)


def _adamw(w, g, m, v, lead_per_block, rows_per_block):
    n, rows, cols = w.shape
    blk = pl.BlockSpec((lead_per_block, rows_per_block, cols), lambda i, j: (i, j, 0))

    def body(w_ref, g_ref, m_ref, v_ref, d_ref, nm_ref, nv_ref):
        gt = g_ref[...]
        nm = ADAM_B1 * m_ref[...] + (1.0 - ADAM_B1) * gt
        nv = ADAM_B2 * v_ref[...] + (1.0 - ADAM_B2) * (gt * gt)
        m_hat = nm / (1.0 - ADAM_B1 ** ADAM_STEP)
        v_hat = nv / (1.0 - ADAM_B2 ** ADAM_STEP)
        d_ref[...] = -ADAM_LR * (m_hat / (jnp.sqrt(v_hat) + ADAM_EPS) + ADAM_WD * w_ref[...])
        nm_ref[...] = nm
        nv_ref[...] = nv

    return pl.pallas_call(
        body, name="adamw", grid=(n // lead_per_block, rows // rows_per_block), in_specs=[blk] * 4,
        out_specs=[blk] * 3,
        out_shape=[jax.ShapeDtypeStruct(w.shape, F32)] * 3,
        compiler_params=pltpu.CompilerParams(dimension_semantics=("parallel", "parallel")),
    )(w, g, m, v)


def _place():
    x, y, c = lax.axis_index("x"), lax.axis_index("y"), lax.axis_index("c")
    return x, y, c, [(1 - x, y), (x, 1 - y), (1 - x, 1 - y)]


ANY = pl.BlockSpec(memory_space=pl.ANY)
COPY_PARTS = 4


def _remote_parts(src_of, dst_of, rows, send_sem, recv_sem, to, begin=True):
    parts = COPY_PARTS if rows % (16 * COPY_PARTS) == 0 else 1
    step = rows // parts
    for i in range(parts if begin is not False else 0):
        pltpu.make_async_remote_copy(src_ref=src_of(i * step, step), dst_ref=dst_of(i * step, step),
                                     send_sem=send_sem, recv_sem=recv_sem, device_id=to, device_id_type=MESH).start()
    if begin == "only":
        return None
    return pltpu.make_async_remote_copy(src_ref=src_of(0, rows), dst_ref=dst_of(0, rows), send_sem=send_sem,
                                        recv_sem=recv_sem, device_id=to, device_id_type=MESH)


GATHER_SEMS = 7


def _gather_steps(p_refs, o_refs, send_sems, recv_sems):
    x, y, c, chips = _place()
    me = 2 * x + y

    def half(ref, which):
        rows = ref.shape[0] // 2
        return ref.at[pl.ds(which * rows, rows)]

    def copy(k, src, dst, to):
        return pltpu.make_async_remote_copy(src_ref=src, dst_ref=dst, send_sem=send_sems.at[k],
                                            recv_sem=recv_sems.at[k], device_id=to, device_id_type=MESH)

    def first_copies():
        out = []
        for b, (p_ref, o_ref) in enumerate(zip(p_refs, o_refs)):
            for j, (cx, cy) in enumerate(chips):
                out.append(copy(GATHER_SEMS * b + j, half(p_ref, c), half(o_ref.at[me], c), (cx, cy, c)))
            out.append(copy(GATHER_SEMS * b + 6, p_ref, o_ref.at[me], (x, y, 1 - c)))
        return out

    def passed_copies():
        out = []
        for b, o_ref in enumerate(o_refs):
            for j, (cx, cy) in enumerate(chips):
                landed = half(o_ref.at[2 * cx + cy], c)
                out.append(copy(GATHER_SEMS * b + 3 + j, landed, landed, (x, y, 1 - c)))
        return out

    def start():
        for cp in first_copies():
            cp.start()

    def forward():
        for b, o_ref in enumerate(o_refs):
            for j, (cx, cy) in enumerate(chips):
                landed = half(o_ref.at[2 * cx + cy], c)
                copy(GATHER_SEMS * b + j, landed, landed, (x, y, c)).wait_recv()
        for cp in passed_copies():
            cp.start()

    def finish():
        for b, o_ref in enumerate(o_refs):
            for j, (cx, cy) in enumerate(chips):
                other = half(o_ref.at[2 * cx + cy], 1 - c)
                copy(GATHER_SEMS * b + 3 + j, other, other, (x, y, c)).wait_recv()
            copy(GATHER_SEMS * b + 6, o_ref.at[me], o_ref.at[me], (x, y, c)).wait_recv()
        for cp in first_copies() + passed_copies():
            cp.wait_send()

    return start, forward, finish


def _gather_chips(shards):
    nb = len(shards)

    def body(*refs):
        for step in _gather_steps(refs[:nb], refs[nb:2 * nb], *refs[2 * nb:]):
            step()

    return pl.pallas_call(
        body, name="gather_chips", in_specs=[ANY] * nb, out_specs=[ANY] * nb,
        out_shape=[jax.ShapeDtypeStruct((N_CHIPS,) + s.shape, s.dtype) for s in shards],
        scratch_shapes=[pltpu.SemaphoreType.DMA((GATHER_SEMS * nb,)), pltpu.SemaphoreType.DMA((GATHER_SEMS * nb,))],
    )(*shards)


def _swap_steps(g_refs, t_refs, spans, send_sems, recv_sems):
    x, y, c, _ = _place()

    def gives(begin):
        return [_remote_parts(
            lambda r0, m, g_ref=g_ref, j=j, base=first_row + (1 - c) * half: g_ref.at[j, pl.ds(base + r0, m), :],
            lambda r0, m, t_ref=t_ref, j=j: t_ref.at[j, pl.ds(r0, m), :],
            half, send_sems.at[b * N_CHIPS + j], recv_sems.at[b * N_CHIPS + j], (x, y, 1 - c), begin)
            for b, (g_ref, t_ref, (first_row, half)) in enumerate(zip(g_refs, t_refs, spans)) for j in range(N_CHIPS)]

    def start():
        gives("only")

    def finish():
        for give in gives(False):
            give.wait()

    return start, finish


def _swap_halves(bufs, spans):
    nb = len(bufs)

    def body(*refs):
        for step in _swap_steps(refs[:nb], refs[nb:2 * nb], spans, *refs[2 * nb:]):
            step()

    return pl.pallas_call(
        body, name="swap_halves", in_specs=[ANY] * nb, out_specs=[ANY] * nb,
        out_shape=[jax.ShapeDtypeStruct((b.shape[0], half, b.shape[2]), b.dtype) for b, (_, half) in zip(bufs, spans)],
        scratch_shapes=[pltpu.SemaphoreType.DMA((nb * N_CHIPS,)), pltpu.SemaphoreType.DMA((nb * N_CHIPS,))],
    )(*bufs)


def _scatter_steps(t_refs, o_refs, send_sems, recv_sems):
    x, y, c, chips = _place()
    me = 2 * x + y

    def slot(ref, chip):
        return lambda r0, n: ref.at[chip, pl.ds(r0, n), :]

    def sends(begin):
        return [_remote_parts(slot(t_ref, 2 * cx + cy), slot(o_ref, me), t_ref.shape[1], send_sems.at[3 * b + j],
                              recv_sems.at[3 * b + j], (cx, cy, c), begin)
                for b, (t_ref, o_ref) in enumerate(zip(t_refs, o_refs)) for j, (cx, cy) in enumerate(chips)]

    def start():
        sends("only")

    def finish():
        for b, o_ref in enumerate(o_refs):
            for j, (cx, cy) in enumerate(chips):
                landed = o_ref.at[2 * cx + cy]
                pltpu.make_async_remote_copy(src_ref=landed, dst_ref=landed, send_sem=send_sems.at[3 * b + j],
                                             recv_sem=recv_sems.at[3 * b + j], device_id=(x, y, c),
                                             device_id_type=MESH).wait_recv()
        for cp in sends(False):
            cp.wait_send()

    return start, finish


def _give_sibling(bufs):
    nb = len(bufs)

    def body(*refs):
        h_refs, o_refs, (send_sems, recv_sems) = refs[:nb], refs[nb:2 * nb], refs[2 * nb:]
        x, y, c, _ = _place()
        gives = [_remote_parts(lambda r0, n, h_ref=h_ref: h_ref.at[pl.ds(r0, n), :],
                               lambda r0, n, o_ref=o_ref: o_ref.at[pl.ds(r0, n), :],
                               h_ref.shape[0], send_sems.at[b], recv_sems.at[b], (x, y, 1 - c))
                 for b, (h_ref, o_ref) in enumerate(zip(h_refs, o_refs))]
        for give in gives:
            give.wait()

    return pl.pallas_call(
        body, name="give_sibling", in_specs=[ANY] * nb, out_specs=[ANY] * nb,
        out_shape=[jax.ShapeDtypeStruct(b.shape, b.dtype) for b in bufs],
        scratch_shapes=[pltpu.SemaphoreType.DMA((nb,)), pltpu.SemaphoreType.DMA((nb,))],
    )(*bufs)


def _pack(arrays, dtype, row_align):
    flat = []
    for arr in arrays:
        v = arr.reshape(-1)
        flat.append(jnp.pad(v, (0, (-v.shape[0]) % LANES)))
    total = sum(f.shape[0] for f in flat) // LANES
    pad_rows = (-total) % row_align
    if pad_rows:
        flat.append(jnp.zeros((pad_rows * LANES,), dtype))
    return jnp.concatenate(flat).reshape(-1, LANES)


def _unpack(buf, shapes, rows_align=1):
    lead = buf.shape[:-2]
    flat = buf.reshape(lead + (-1,))
    out, off = [], 0
    for shape in shapes:
        size = 1
        for dim in shape:
            size *= dim
        out.append(flat[..., off:off + size].reshape(lead + tuple(shape)))
        off += size + (-size) % (LANES * rows_align)
    return out


def _from_chips(parts, axis):
    return jnp.concatenate([parts[j] for j in range(N_CHIPS)], axis=axis)


def kernel(x, ln_g, ln_b, attn_w_in, attn_b_f, attn_w_out, rnn_w_in, rnn_conv_w, rnn_conv_b, rnn_w_a, rnn_b_a, rnn_w_i, rnn_b_i, rnn_lambda, rnn_w_out, loss_target, m_ln_g, m_ln_b, m_attn_w_in, m_attn_b_f, m_attn_w_out, m_rnn_w_in, m_rnn_conv_w, m_rnn_conv_b, m_rnn_w_a, m_rnn_b_a, m_rnn_w_i, m_rnn_b_i, m_rnn_lambda, m_rnn_w_out, v_ln_g, v_ln_b, v_attn_w_in, v_attn_b_f, v_attn_w_out, v_rnn_w_in, v_rnn_conv_w, v_rnn_conv_b, v_rnn_w_a, v_rnn_b_a, v_rnn_w_i, v_rnn_b_i, v_rnn_lambda, v_rnn_w_out):
    s_len, d = x.shape[1], x.shape[2]
    xs = x.reshape(s_len, d)
    target = loss_target.reshape(s_len, d)
    heads = attn_b_f.shape[1]
    qkvg = attn_w_in.shape[2] * N_CHIPS - heads

    me = 2 * lax.axis_index("x") + lax.axis_index("y")
    core = lax.axis_index("c").astype(jnp.int32)
    small = [rnn_conv_w, rnn_conv_b, rnn_b_a, rnn_b_i, rnn_lambda]
    a_in, a_out = attn_w_in.astype(BF16), attn_w_out.astype(BF16)
    later = [a_in[1], a_out] + [w.astype(BF16) for w in (rnn_w_in, rnn_w_a, rnn_w_i, rnn_w_out)]
    got_in, got_small = _gather_chips([a_in[0], _pack(small, F32, 16)])
    conv_w, conv_b, b_a, b_i, lam = [
        _from_chips(p, ax) for p, ax in zip(_unpack(got_small, [w.shape for w in small]), (2, 1, 1, 1, 1))]

    def attn_in_weights(w_in):
        shard = w_in.shape[2]

        def columns(first, last):
            return [w_in[j][:, max(first - j * shard, 0):min(last - j * shard, shard)]
                    for j in range(N_CHIPS) if first < (j + 1) * shard and last > j * shard]

        return jnp.concatenate([t * (HEAD_DIM ** -0.5) for t in columns(0, d)] + columns(d, qkvg + heads)
                               + [jnp.zeros((d, 128 - heads), BF16)], axis=1)

    w_cat = [attn_in_weights(got_in), None]
    b_f = jnp.pad(attn_b_f, ((0, 0), (0, 128 - heads)))

    saved = []
    cur = xs
    for layer in range(DEPTH):
        idx = layer // 2
        g_l, b_l = ln_g[layer:layer + 1], ln_b[layer:layer + 1]
        goal = target if layer == DEPTH - 1 else None
        if layer % 2 == 0:
            q, k, v, gate, fl = _proj(cur, w_cat[idx], [(0, d, BF16), (d, d, BF16), (2 * d, d, BF16),
                                                         (3 * d, d, F32), (4 * d, 128, F32)], "attn_proj")
            cum, sneg = _fcum(fl, b_f[idx:idx + 1])
            o, lse, *rest = _flash_fwd(q, k, v, cum, later if layer == 0 else ())
            if layer == 0:
                w_cat[1] = attn_in_weights(rest[0])
                w_out_a = jnp.moveaxis(rest[1], 0, 1).reshape(2, d, d)
                w_in_r = jnp.moveaxis(rest[2], 0, 2).reshape(2, d, 2 * d)
                w_a = jnp.transpose(rest[3], (1, 2, 0, 3, 4)).reshape(2, -1, RNN_BLOCK, RNN_BLOCK)
                w_i = jnp.transpose(rest[4], (1, 2, 0, 3, 4)).reshape(2, -1, RNN_BLOCK, RNN_BLOCK)
                w_out_r = jnp.moveaxis(rest[5], 0, 1).reshape(2, d, d)
            nxt, xh, rs, yg, *sq = _out_ln(o, gate, cur, w_out_a[idx], g_l, b_l, "out_ln", goal)
            saved.append(dict(x=cur, q=q, k=k, v=v, gate=gate, cum=cum, sneg=sneg, a=o, lse=lse, xh=xh, rs=rs, yg=yg))
        else:
            u, gate = _proj(cur, w_in_r[idx], [(0, d, F32), (d, d, F32)], "rnn_proj")
            vecs = [t[idx:idx + 1] for t in (conv_b, b_a, b_i, lam)]
            hseq, uc, r, ig, a = _rnn_fwd(u, conv_w[idx], vecs[0], w_a[idx], vecs[1], w_i[idx], vecs[2], vecs[3])
            nxt, xh, rs, yg, *sq = _out_ln(hseq, gate, cur, w_out_r[idx], g_l, b_l, "out_ln", goal)
            saved.append(dict(x=cur, u=u, gate=gate, uc=uc, r=r, ig=ig, av=a, a=hseq, xh=xh, rs=rs, yg=yg, lam=vecs[3]))
        cur = nxt

    dout = cur
    loss_here = 0.5 * jnp.sum(sq[0]) / d

    g_ln_g, g_ln_b = [None] * DEPTH, [None] * DEPTH
    g_attn = [None] * 2
    g_rnn = [None] * 2
    slots = None
    core1 = core.reshape(1)
    late_half = (SLOT_ROWS - LATE_ROWS) // 2

    def by_chip(t, axis):
        shape = t.shape[:axis] + (N_CHIPS, t.shape[axis] // N_CHIPS) + t.shape[axis + 1:]
        flat = jnp.moveaxis(t.reshape(shape), axis, 0).reshape(N_CHIPS, -1)
        return jnp.pad(flat, ((0, 0), (0, (-flat.shape[1]) % (8 * LANES)))).reshape(N_CHIPS, -1, LANES)

    def both(key):
        return jnp.stack([it[key] for it in g_rnn])

    for layer in reversed(range(DEPTH)):
        idx = layer // 2
        sv = saved[layer]
        swap = None
        if layer == 0:
            gates = jnp.concatenate([by_chip(both("w_a"), 2), by_chip(both("w_i"), 2)], axis=1)
            slots = lax.dynamic_update_slice(slots, gates, (0, ROWS_GATES, 0))
            swap = (slots, (LATE_ROWS, late_half))
        w_out = w_out_a[idx] if layer % 2 == 0 else w_out_r[idx]
        dz, da, dgate, dg, db, *theirs_late = _ln_bwd(dout, sv["xh"], sv["rs"], ln_g[layer:layer + 1], w_out,
                                                      sv["gate"], sv["a"], "ln_bwd", swap)
        if layer == 0:
            pair_late = _pair_sum(core1, slots, theirs_late[0], LATE_ROWS, BF16)
        g_ln_g[layer], g_ln_b[layer] = dg, db
        slots = _dw_out(slots, sv["yg"], dz, (ROWS_ATTN_OUT if layer % 2 == 0 else ROWS_RNN_OUT)[idx])
        if layer % 2 == 0:
            dq, dk, dv, dcum, *arrived = _flash_bwd(sv["q"], sv["k"], sv["v"], sv["a"], da, sv["lse"], sv["cum"],
                                                    [pair_late] if layer == 0 else ())
            dlogit, dbf = _fbwd(dcum, sv["sneg"])
            pieces = [dq, dk, dv, dgate, dlogit]
            if layer > 0:
                dout, = _mm_nt(dz, [(p, j * d) for j, p in enumerate(pieces)], w_cat[idx], "attn_dx")
            slots, d_w_f = _dw_attn_in(slots, sv["x"], pieces[:4], dlogit, idx)
            g_attn[idx] = dict(w_f=d_w_f[:, :heads], b_f=dbf[:, 0])
        else:
            duc, d_wa, d_wi, d_ba, d_bi, d_lam = _rnn_bwd(da, sv["av"], sv["a"], sv["r"], sv["ig"], sv["uc"], sv["lam"],
                                                          w_a[idx], w_i[idx])
            du, d_cw, d_cb = _conv_bwd(duc, sv["u"], conv_w[idx])
            dout, = _mm_nt(dz, [(du, 0), (dgate, d)], w_in_r[idx], "rnn_dx")
            slots = _dw_rnn_in(slots, sv["x"], (du, dgate), idx)
            g_rnn[idx] = dict(conv_w=d_cw, conv_b=d_cb[0], w_a=d_wa, b_a=d_ba[0], w_i=d_wi, b_i=d_bi[0], lam=d_lam[0])

    def everywhere(t):
        flat = t.reshape(-1)
        flat = jnp.pad(flat, (0, (-flat.shape[0]) % (8 * LANES))).reshape(-1, LANES)
        return jnp.broadcast_to(flat[None], (N_CHIPS,) + flat.shape)

    in_rows = jnp.stack([slots[1:, r:r + d, :heads] for r in ROWS_ATTN_IN], axis=1)
    edges = jnp.concatenate([in_rows, jnp.stack([it["w_f"] for it in g_attn])[None]])
    rows = [everywhere(edges), by_chip(both("conv_w"), 2),
            by_chip(both("conv_b"), 1), by_chip(both("b_a"), 1), by_chip(both("b_i"), 1), by_chip(both("lam"), 1),
            everywhere(jnp.concatenate(g_ln_g)), everywhere(jnp.concatenate(g_ln_b)),
            everywhere(jnp.stack([it["b_f"] for it in g_attn])), everywhere(loss_here)]
    used = sum(r.shape[1] for r in rows)
    small = jnp.concatenate(rows + [jnp.zeros((N_CHIPS, (-used) % 32, LANES), F32)], axis=1)

    spans = [(0, LATE_ROWS // 2), (0, small.shape[1] // 2)]
    theirs = _swap_halves([slots, small], spans)
    pair = [_pair_sum(core1, slots, theirs[0], 0, BF16), _pair_sum(core1, small, theirs[1], 0, F32)]
    dout, *arrived_last = _mm_nt(dz, [(p, j * d) for j, p in enumerate(pieces)], w_cat[0], "attn_dx", pair)
    grad_x = dout.reshape(x.shape)
    summed = []
    for mine_all, got in zip([pair_late] + pair, list(arrived) + arrived_last):
        own = lax.dynamic_index_in_dim(mine_all, me, 0, keepdims=False)
        summed.append(_sum_chips(lax.dynamic_update_index_in_dim(got, own, me, 0)))
    late, early, small_sum = [
        jnp.concatenate([jnp.where(core == 0, mine, other), jnp.where(core == 0, other, mine)])
        for mine, other in zip(summed, _give_sibling(summed))]

    def rows_at(first, n):
        return early[first:first + n] if first < LATE_ROWS else late[first - LATE_ROWS:first - LATE_ROWS + n]

    g_in_group = jnp.stack([rows_at(r, d) for r in ROWS_ATTN_IN])
    g_w_out_a = jnp.stack([rows_at(r, d // N_CHIPS) for r in ROWS_ATTN_OUT])
    g_w_out_r = jnp.stack([rows_at(r, d // N_CHIPS) for r in ROWS_RNN_OUT])
    folded = jnp.stack([rows_at(r, d // 2) for r in ROWS_RNN_IN])
    g_w_in_r = jnp.concatenate([folded[:, :, :d // 2], folded[:, :, d // 2:]], axis=1)
    gate_rows = rnn_w_a.size // LANES
    g_w_a = rows_at(ROWS_GATES, gate_rows).reshape(rnn_w_a.shape)
    g_w_i = rows_at(ROWS_GATES + gate_rows, gate_rows).reshape(rnn_w_i.shape)
    (g_edges, g_conv_w, g_conv_b, g_b_a, g_b_i, g_lam, g_ln_g_sum, g_ln_b_sum, g_b_f,
     loss) = _unpack(small_sum, [
        (N_CHIPS, 2, d, heads), rnn_conv_w.shape, rnn_conv_b.shape, rnn_b_a.shape,
        rnn_b_i.shape, rnn_lambda.shape, ln_g.shape, ln_b.shape, attn_b_f.shape, ()], rows_align=8)
    wide = jnp.concatenate([g_in_group, lax.dynamic_index_in_dim(g_edges, me, 0, keepdims=False)], axis=2)
    g_w_in_a = lax.dynamic_slice(wide, (0, 0, (heads // N_CHIPS) * me), attn_w_in.shape)

    def adam_nd(w, g, m, v, view, rows_per_block):
        outs = _adamw(w.reshape(view), g.reshape(view), m.reshape(view), v.reshape(view), 1, rows_per_block)
        return [t.reshape(w.shape) for t in outs]

    turned = [jnp.transpose(t, (2, 0, 1)) for t in (attn_w_in, g_w_in_a, m_attn_w_in, v_attn_w_in)]
    upd = {
        "attn_w_in": [jnp.transpose(t, (1, 2, 0)) for t in _adamw(*turned, attn_w_in.shape[2] // N_CHIPS, 2)],
        "attn_w_out": adam_nd(attn_w_out, g_w_out_a, m_attn_w_out, v_attn_w_out, attn_w_out.shape, 256),
        "rnn_w_in": adam_nd(rnn_w_in, g_w_in_r, m_rnn_w_in, v_rnn_w_in, rnn_w_in.shape, 512),
        "rnn_w_a": adam_nd(rnn_w_a, g_w_a, m_rnn_w_a, v_rnn_w_a, (1, -1, RNN_BLOCK), 512),
        "rnn_w_i": adam_nd(rnn_w_i, g_w_i, m_rnn_w_i, v_rnn_w_i, (1, -1, RNN_BLOCK), 512),
        "rnn_w_out": adam_nd(rnn_w_out, g_w_out_r, m_rnn_w_out, v_rnn_w_out, rnn_w_out.shape, 256),
    }
    smalls = [rnn_conv_w, rnn_conv_b, rnn_b_a, rnn_b_i, rnn_lambda, ln_g, ln_b, attn_b_f]
    g_small = [g_conv_w, g_conv_b, g_b_a, g_b_i, g_lam, g_ln_g_sum, g_ln_b_sum, g_b_f]
    m_small = [m_rnn_conv_w, m_rnn_conv_b, m_rnn_b_a, m_rnn_b_i, m_rnn_lambda, m_ln_g, m_ln_b, m_attn_b_f]
    v_small = [v_rnn_conv_w, v_rnn_conv_b, v_rnn_b_a, v_rnn_b_i, v_rnn_lambda, v_ln_g, v_ln_b, v_attn_b_f]
    packs = [_pack(t, F32, 16)[None] for t in (smalls, g_small, m_small, v_small)]
    small_out = [_unpack(t[0], [w.shape for w in smalls]) for t in _adamw(*packs, 1, 16)]
    for k, name in enumerate(("rnn_conv_w", "rnn_conv_b", "rnn_b_a", "rnn_b_i", "rnn_lambda", "ln_g", "ln_b",
                              "attn_b_f")):
        upd[name] = [small_out[0][k], small_out[1][k], small_out[2][k]]
    grad = {"attn_w_in": g_w_in_a, "attn_w_out": g_w_out_a, "rnn_w_in": g_w_in_r, "rnn_w_a": g_w_a, "rnn_w_i": g_w_i,
            "rnn_w_out": g_w_out_r, "rnn_conv_w": g_conv_w, "rnn_conv_b": g_conv_b, "rnn_b_a": g_b_a,
            "rnn_b_i": g_b_i, "rnn_lambda": g_lam, "ln_g": g_ln_g_sum, "ln_b": g_ln_b_sum, "attn_b_f": g_b_f}
    names = ("ln_g", "ln_b", "attn_w_in", "attn_b_f", "attn_w_out", "rnn_w_in", "rnn_conv_w", "rnn_conv_b",
             "rnn_w_a", "rnn_b_a", "rnn_w_i", "rnn_b_i", "rnn_lambda", "rnn_w_out")
    return (loss, grad_x, *[grad[n] for n in names], *[upd[n][0] for n in names], *[upd[n][1] for n in names],
            *[upd[n][2] for n in names])
```

```python
import jax
import jax.numpy as jnp
from jax import lax
from jax.experimental import pallas as pl
from jax.experimental.pallas import tpu as pltpu

F32 = jnp.float32
BF16 = jnp.bfloat16
MESH = pl.DeviceIdType.MESH

DEPTH = 4
HEAD_DIM = 64
HEAD_PAIR = 2 * HEAD_DIM
RNN_BLOCK = 256
CONV_WIDTH = 4
LRU_C = 8.0
ALPHA = (2.0 * DEPTH) ** 0.25
LN_EPS = 1e-5
ADAM_LR, ADAM_B1, ADAM_B2, ADAM_EPS, ADAM_WD, ADAM_STEP = 0.001, 0.9, 0.999, 1e-08, 0.01, 10
N_CHIPS = 4
LANES = 1024
NEG = -1e30

NT_DIMS = (((1,), (1,)), ((), ()))
TN_DIMS = (((0,), (0,)), ((), ()))


def _sigmoid(z):
    return 1.0 / (1.0 + jnp.exp(-z))


def _softplus(z):
    return jnp.maximum(z, 0.0) + jnp.log(1.0 + jnp.exp(-jnp.abs(z)))


def _neg_expm1(y):
    poly = y * (1.0 + y * (0.5 + y * (1.0 / 6.0 + y * (1.0 / 24.0))))
    return -jnp.where(y > -0.05, poly, jnp.exp(y) - 1.0)


def _shift_down(cur, prev8, k):
    n = cur.shape[0]
    row = lax.broadcasted_iota(jnp.int32, cur.shape, 0)
    fix = jnp.tile(pltpu.roll(prev8, k, 0), (n // 8, 1))
    return jnp.where(row < k, fix, pltpu.roll(cur, k, 0))


def _shift_up(cur, next8, k):
    n = cur.shape[0]
    row = lax.broadcasted_iota(jnp.int32, cur.shape, 0)
    fix = jnp.tile(pltpu.roll(next8, 8 - k, 0), (n // 8, 1))
    return jnp.where(row >= n - k, fix, pltpu.roll(cur, n - k, 0))


def _proj(x, w, outs, name):
    s_len, d = x.shape
    tm = min(1024, s_len)

    def body(x_ref, w_ref, *o_refs):
        xb = x_ref[...].astype(BF16)
        for (c0, wd, dt), o_ref in zip(outs, o_refs):
            step = min(wd, 512)
            for cc in range(0, wd, step):
                o_ref[:, cc:cc + step] = jnp.dot(
                    xb, w_ref[:, c0 + cc:c0 + cc + step], preferred_element_type=F32).astype(dt)

    return pl.pallas_call(
        body, name=name, grid=(s_len // tm,),
        in_specs=[pl.BlockSpec((tm, d), lambda i: (i, 0)), pl.BlockSpec(w.shape, lambda i: (0, 0))],
        out_specs=[pl.BlockSpec((tm, wd), lambda i: (i, 0)) for _, wd, _ in outs],
        out_shape=[jax.ShapeDtypeStruct((s_len, wd), dt) for _, wd, dt in outs],
        compiler_params=pltpu.CompilerParams(dimension_semantics=("parallel",)),
    )(x, w)


def _mm_nt(dz, pieces, w, name, outgoing=()):
    s_len, d = dz.shape
    tm = min(512, s_len)
    steps = s_len // tm
    n = len(pieces)
    ns = len(outgoing)
    cols = [(c0, p.shape[1]) for p, c0 in pieces]

    def body(*refs):
        dz_ref, p_refs, w_ref, o_ref = refs[0], refs[1:1 + n], refs[1 + n], refs[2 + n + ns]
        if ns:
            start, finish = _scatter_steps(refs[2 + n:2 + n + ns], refs[3 + n + ns:3 + n + 2 * ns],
                                           *refs[3 + n + 2 * ns:])
            pl.when(pl.program_id(0) == 0)(start)
        acc = ALPHA * dz_ref[...]
        for (c0, wd), p_ref in zip(cols, p_refs):
            acc = acc + lax.dot_general(p_ref[...], w_ref[:, c0:c0 + wd], NT_DIMS, preferred_element_type=F32)
        o_ref[...] = acc
        if ns:
            pl.when(pl.program_id(0) == steps - 1)(finish)

    out, *arrived = pl.pallas_call(
        body, name=name + "_scatter" if ns else name, grid=(steps,),
        in_specs=[pl.BlockSpec((tm, d), lambda i: (i, 0))]
        + [pl.BlockSpec((tm, wd), lambda i: (i, 0)) for _, wd in cols]
        + [pl.BlockSpec(w.shape, lambda i: (0, 0))] + [ANY] * ns,
        out_specs=[pl.BlockSpec((tm, d), lambda i: (i, 0))] + [ANY] * ns,
        out_shape=[jax.ShapeDtypeStruct((s_len, d), F32)] + [jax.ShapeDtypeStruct(t.shape, t.dtype) for t in outgoing],
        scratch_shapes=[pltpu.SemaphoreType.DMA((3 * ns,))] * 2 if ns else [],
        compiler_params=pltpu.CompilerParams(dimension_semantics=("arbitrary" if ns else "parallel",)),
    )(dz, *[p for p, _ in pieces], w, *outgoing)
    return (out, *arrived)


ROWS_ATTN_IN = (0, 2048)
ROWS_ATTN_OUT = (1024, 1280)
ROWS_RNN_OUT = (1536, 1792)
ROWS_RNN_IN = (3072, 3584)
ROWS_GATES = 4096
LATE_ROWS = 1280
SLOT_ROWS = 4352


DW_ROWS = 1024


def _dw_call(body, name, slots, operands, specs, out_block, out_index, grid, semantics):
    return pl.pallas_call(
        body, name=name, grid=grid,
        in_specs=specs if slots is None else [ANY] + specs,
        out_specs=pl.BlockSpec(out_block, out_index),
        out_shape=jax.ShapeDtypeStruct((N_CHIPS, SLOT_ROWS, LANES), F32),
        input_output_aliases={} if slots is None else {0: 0},
        compiler_params=pltpu.CompilerParams(dimension_semantics=semantics),
    )(*(operands if slots is None else [slots] + operands))


def _dw_attn_in(slots, x, pieces, forget, layer):
    s_len, d = x.shape
    ts = min(s_len, DW_ROWS)
    steps = s_len // ts
    half = d // 2

    def body(g_ref, x_ref, p0, p1, p2, p3, f_ref, o_ref, wf_ref):
        lanes, step = pl.program_id(0), pl.program_id(1)

        @pl.when(step == 0)
        def _():
            o_ref[...] = jnp.zeros_like(o_ref)

        xb = x_ref[...].astype(BF16)
        for j, p_ref in enumerate((p0, p1, p2, p3)):
            o_ref[j] += lax.dot_general(xb, p_ref[...], TN_DIMS, preferred_element_type=F32)

        @pl.when(step == steps - 1)
        def _():
            o_ref[0] = o_ref[0] * (HEAD_DIM ** -0.5)

        @pl.when(lanes == 0)
        def _():
            @pl.when(step == 0)
            def _():
                wf_ref[...] = jnp.zeros_like(wf_ref)

            wf_ref[...] += lax.dot_general(xb, f_ref[...], TN_DIMS, preferred_element_type=F32)

    return pl.pallas_call(
        body, name="dw_attn_in", grid=(2, steps),
        in_specs=[ANY, pl.BlockSpec((ts, d), lambda i, s: (s, 0))] + [pl.BlockSpec((ts, half), lambda i, s: (s, i))] * 4
        + [pl.BlockSpec((ts, 128), lambda i, s: (s, 0))],
        out_specs=[pl.BlockSpec((N_CHIPS, d, half), lambda i, s: (0, ROWS_ATTN_IN[layer] // d, i)),
                   pl.BlockSpec((d, 128), lambda i, s: (0, 0))],
        out_shape=[jax.ShapeDtypeStruct((N_CHIPS, SLOT_ROWS, LANES), F32), jax.ShapeDtypeStruct((d, 128), F32)],
        input_output_aliases={0: 0},
        compiler_params=pltpu.CompilerParams(dimension_semantics=("arbitrary", "arbitrary")),
    )(slots, x, *pieces, forget)


def _dw_out(slots, yg, dz, first_row):
    s_len, d = dz.shape
    ts = min(s_len, DW_ROWS)
    rows = d // N_CHIPS

    def body(*refs):
        a_ref, b_ref, o_ref = refs[-3:]

        @pl.when(pl.program_id(0) == 0)
        def _():
            o_ref[...] = jnp.zeros_like(o_ref)

        prod = lax.dot_general(a_ref[...], b_ref[...].astype(BF16), TN_DIMS, preferred_element_type=F32)
        o_ref[...] += prod.reshape(N_CHIPS, rows, d)

    specs = [pl.BlockSpec((ts, d), lambda s: (s, 0))] * 2
    return _dw_call(body, "dw_out", slots, [yg, dz], specs, (N_CHIPS, rows, d), lambda s: (0, first_row // rows, 0),
                    (s_len // ts,), ("arbitrary",))


def _dw_rnn_in(slots, x, parts, layer):
    s_len, d = x.shape
    ts = min(s_len, DW_ROWS)
    half = d // 2

    def body(*refs):
        x_ref, p_refs, o_ref = refs[-4], refs[-3:-1], refs[-1]

        @pl.when(pl.program_id(0) == 0)
        def _():
            o_ref[...] = jnp.zeros_like(o_ref)

        xb = x_ref[...].astype(BF16)
        for p, p_ref in enumerate(p_refs):
            for jj in (0, 1):
                for r in (0, 1):
                    o_ref[2 * p + jj, :, r * half:(r + 1) * half] += lax.dot_general(
                        xb[:, r * half:(r + 1) * half], p_ref[:, jj * half:(jj + 1) * half], TN_DIMS,
                        preferred_element_type=F32)

    specs = [pl.BlockSpec((ts, d), lambda s: (s, 0))] * 3
    return _dw_call(body, "dw_rnn_in", slots, [x] + list(parts), specs, (N_CHIPS, half, d),
                    lambda s: (0, ROWS_RNN_IN[layer] // half, 0), (s_len // ts,), ("arbitrary",))


def _fcum(fl, bias):
    s_len = fl.shape[0]

    def body(fl_ref, b_ref, cum_ref, sg_ref):
        z = fl_ref[...] + b_ref[...]
        e = jnp.exp(-jnp.abs(z))
        logf = jnp.minimum(z, 0.0) - jnp.log(1.0 + e)
        sneg = jnp.where(z >= 0, e, 1.0) / (1.0 + e)
        run = logf.T[0:16, :]
        sg_ref[...] = sneg.T[0:16, :]
        lane = lax.broadcasted_iota(jnp.int32, (16, s_len), 1)
        sh = 1
        while sh < s_len:
            run = run + jnp.where(lane >= sh, pltpu.roll(run, sh, 1), 0.0)
            sh *= 2
        cum_ref[...] = run

    return pl.pallas_call(
        body, name="fcum",
        out_shape=[jax.ShapeDtypeStruct((16, s_len), F32), jax.ShapeDtypeStruct((16, s_len), F32)],
    )(fl, bias)


def _fbwd(dcum, sneg):
    s_len = dcum.shape[1]

    def body(dc_ref, sg_ref, dl_ref, db_ref):
        run = dc_ref[...]
        lane = lax.broadcasted_iota(jnp.int32, (16, s_len), 1)
        sh = 1
        while sh < s_len:
            run = run + jnp.where(lane < s_len - sh, pltpu.roll(run, s_len - sh, 1), 0.0)
            sh *= 2
        dlog = run * sg_ref[...]
        db_ref[...] = jnp.broadcast_to(jnp.sum(dlog, axis=1, keepdims=True), (16, 128))
        full = jnp.concatenate([dlog, jnp.zeros((112, s_len), F32)], axis=0)
        dl_ref[...] = full.T.astype(BF16)

    return pl.pallas_call(
        body, name="fbwd",
        out_shape=[jax.ShapeDtypeStruct((s_len, 128), BF16), jax.ShapeDtypeStruct((16, 128), F32)],
    )(dcum, sneg)


FWD_TILE = 1024
BWD_TILE = 1024


def _flash_fwd(q, k, v, cum, shards=()):
    s_len, width = q.shape
    tile = min(FWD_TILE, s_len // 2)
    nt = s_len // tile
    reps = tile // 128
    cumr = cum.reshape(-1, tile)
    ns = len(shards)
    pairs = width // HEAD_PAIR

    def body(*refs):
        q_ref, k_ref, v_ref, cum_ref = refs[:4]
        o_ref, lse_ref = refs[4 + ns:6 + ns]
        q_t, v_t, cum_col = refs[6 + 2 * ns:9 + 2 * ns]
        pid = pl.program_id(0)
        if ns:
            start, forward, finish = _gather_steps(refs[4:4 + ns], refs[6 + ns:6 + 2 * ns], *refs[9 + 2 * ns:])
            pl.when(pid == 0)(start)
            pl.when(pid == pairs - 3)(forward)
        top = lax.broadcasted_iota(jnp.int32, (HEAD_PAIR, tile), 0) < HEAD_DIM
        causal = (lax.broadcasted_iota(jnp.int32, (tile, tile), 0)
                  <= lax.broadcasted_iota(jnp.int32, (tile, tile), 1))

        def pre(i, carry):
            r0 = pl.multiple_of(i * tile, tile)
            q_t[i] = q_ref[pl.ds(r0, tile), :].astype(F32).T.astype(BF16)
            v_t[i] = v_ref[pl.ds(r0, tile), :].astype(F32).T.astype(BF16)
            for h in (0, 1):
                row = cum_ref[pl.ds((2 * pid + h) * nt + i, 1), :]
                cum_col[h, pl.ds(r0, tile), :] = jnp.broadcast_to(row, (HEAD_PAIR, tile)).T
            return carry

        lax.fori_loop(0, nt, pre, 0)

        def q_loop(qi, carry):
            qt = q_t[qi]
            zt = jnp.zeros_like(qt)
            qms = (jnp.where(top, qt, zt), jnp.where(top, zt, qt))

            def step(kj, state, masked):
                m0, l0, m1, l1, acc = state
                k0 = pl.multiple_of(kj * tile, tile)
                kt = k_ref[pl.ds(k0, tile), :]
                vt = v_t[kj]
                ms, ls, scales, contrib = [m0, m1], [l0, l1], [], None
                for h in (0, 1):
                    s = jnp.dot(kt, qms[h], preferred_element_type=F32)
                    s = s - jnp.tile(cum_col[h, pl.ds(k0, tile), :], (1, reps))
                    if masked:
                        s = jnp.where(causal, s, NEG)
                    m_new = jnp.maximum(ms[h], jnp.max(s, axis=0, keepdims=True))
                    p = jnp.exp(s - m_new)
                    scale = jnp.exp(ms[h] - m_new)
                    ls[h] = scale * ls[h] + jnp.sum(p, axis=0, keepdims=True)
                    ms[h] = m_new
                    scales.append(scale)
                    vm = jnp.where(top, vt, zt) if h == 0 else jnp.where(top, zt, vt)
                    part = jnp.dot(vm, p.astype(BF16), preferred_element_type=F32)
                    contrib = part if contrib is None else contrib + part
                acc = jnp.where(top, scales[0], scales[1]) * acc + contrib
                return ms[0], ls[0], ms[1], ls[1], acc

            low = jnp.full((1, tile), NEG, F32)
            zero = jnp.zeros((1, tile), F32)
            state = step(qi, (low, zero, low, zero, jnp.zeros((HEAD_PAIR, tile), F32)), True)
            m0, l0, m1, l1, acc = lax.fori_loop(0, qi, lambda kj, c: step(kj, c, False), state)
            q0 = pl.multiple_of(qi * tile, tile)
            o_ref[pl.ds(q0, tile), :] = (acc / jnp.where(top, l0, l1)).T
            lse_ref[pl.ds((2 * pid) * nt + qi, 1), :] = m0 + jnp.log(l0)
            lse_ref[pl.ds((2 * pid + 1) * nt + qi, 1), :] = m1 + jnp.log(l1)
            return carry

        lax.fori_loop(0, nt, q_loop, 0)
        if ns:
            pl.when(pid == pairs - 1)(finish)

    blk = pl.BlockSpec((s_len, HEAD_PAIR), lambda p: (0, p))
    full = pl.BlockSpec(cumr.shape, lambda p: (0, 0))
    sems = [pltpu.SemaphoreType.DMA((GATHER_SEMS * ns,))] * 2 if ns else []
    o, lse, *gathered = pl.pallas_call(
        body, name="flash_fwd_gather" if ns else "flash_fwd", grid=(pairs,),
        in_specs=[blk, blk, blk, full] + [ANY] * ns,
        out_specs=[blk, full] + [ANY] * ns,
        out_shape=[jax.ShapeDtypeStruct((s_len, width), F32), jax.ShapeDtypeStruct(cumr.shape, F32)]
        + [jax.ShapeDtypeStruct((N_CHIPS,) + s.shape, s.dtype) for s in shards],
        scratch_shapes=[pltpu.VMEM((nt, HEAD_PAIR, tile), BF16), pltpu.VMEM((nt, HEAD_PAIR, tile), BF16),
                        pltpu.VMEM((2, s_len, HEAD_PAIR), F32)] + sems,
        compiler_params=pltpu.CompilerParams(dimension_semantics=("arbitrary",)),
    )(q, k, v, cumr, *shards)
    return (o, lse.reshape(cum.shape), *gathered)


def _flash_bwd(q, k, v, o, do, lse, cum, outgoing=()):
    s_len, width = q.shape
    tile = min(BWD_TILE, s_len // 2)
    nt = s_len // tile
    reps = tile // 128
    cumr = cum.reshape(-1, tile)
    lse = lse.reshape(-1, tile)
    ns = len(outgoing)
    pairs = width // HEAD_PAIR

    def body(*refs):
        q_ref, k_ref, v_ref, o_ref, do_ref, lse_ref, cum_ref = refs[:7]
        dq_ref, dk_ref, dv_ref, dcum_ref = refs[7 + ns:11 + ns]
        (q_t, do_t, k_t, do_bf, cum_col, dq_t_acc, delta, q_side, dk_acc, dv_acc,
         k_side) = refs[11 + 2 * ns:22 + 2 * ns]
        pid = pl.program_id(0)
        if ns:
            start, finish = _scatter_steps(refs[7:7 + ns], refs[11 + ns:11 + 2 * ns], *refs[22 + 2 * ns:])
            pl.when(pid == 0)(start)
        top = lax.broadcasted_iota(jnp.int32, (HEAD_PAIR, tile), 0) < HEAD_DIM
        left = lax.broadcasted_iota(jnp.int32, (tile, HEAD_PAIR), 1) < HEAD_DIM
        causal = (lax.broadcasted_iota(jnp.int32, (tile, tile), 0)
                  <= lax.broadcasted_iota(jnp.int32, (tile, tile), 1))

        def pre(i, carry):
            r0 = pl.multiple_of(i * tile, tile)
            d_o = do_ref[pl.ds(r0, tile), :]
            prod_t = (d_o * o_ref[pl.ds(r0, tile), :]).T
            delta[pl.ds(i, 1), :] = jnp.sum(prod_t[:HEAD_DIM], axis=0, keepdims=True)
            delta[pl.ds(nt + i, 1), :] = jnp.sum(prod_t[HEAD_DIM:], axis=0, keepdims=True)
            do_bf[pl.ds(r0, tile), :] = d_o.astype(BF16)
            do_t[i] = d_o.T.astype(BF16)
            q_t[i] = q_ref[pl.ds(r0, tile), :].astype(F32).T.astype(BF16)
            k_t[i] = k_ref[pl.ds(r0, tile), :].astype(F32).T.astype(BF16)
            dq_t_acc[i] = jnp.zeros((HEAD_PAIR, tile), F32)
            for h in (0, 1):
                row = cum_ref[pl.ds((2 * pid + h) * nt + i, 1), :]
                cum_col[h, pl.ds(r0, tile), :] = jnp.broadcast_to(row, (HEAD_PAIR, tile)).T
                q_side[pl.ds(h * nt + i, 1), :] = jnp.zeros((1, tile), F32)
            return carry

        lax.fori_loop(0, nt, pre, 0)

        def kv_loop(kj, carry):
            k0 = pl.multiple_of(kj * tile, tile)
            kt = k_ref[pl.ds(k0, tile), :]
            vt = v_ref[pl.ds(k0, tile), :]
            ktt = k_t[kj]
            zt = jnp.zeros_like(ktt)
            zr = jnp.zeros_like(kt)
            kms = (jnp.where(top, ktt, zt), jnp.where(top, zt, ktt))
            cols = [jnp.tile(cum_col[h, pl.ds(k0, tile), :], (1, reps)) for h in (0, 1)]
            dk_acc[...] = jnp.zeros_like(dk_acc)
            dv_acc[...] = jnp.zeros_like(dv_acc)
            k_side[...] = jnp.zeros_like(k_side)

            def step(qi, carry, masked):
                q0 = pl.multiple_of(qi * tile, tile)
                qtt = q_t[qi]
                dtt = do_t[qi]
                q_rows = q_ref[pl.ds(q0, tile), :]
                do_rows = do_bf[pl.ds(q0, tile), :]
                dq_part = None
                for h in (0, 1):
                    q_m = jnp.where(top, qtt, zt) if h == 0 else jnp.where(top, zt, qtt)
                    do_m = jnp.where(top, dtt, zt) if h == 0 else jnp.where(top, zt, dtt)
                    s = jnp.dot(kt, q_m, preferred_element_type=F32) - cols[h]
                    if masked:
                        s = jnp.where(causal, s, NEG)
                    p = jnp.exp(s - lse_ref[pl.ds((2 * pid + h) * nt + qi, 1), :])
                    dp = jnp.dot(vt, do_m, preferred_element_type=F32)
                    ds = p * (dp - delta[pl.ds(h * nt + qi, 1), :])
                    q_side[pl.ds(h * nt + qi, 1), :] += jnp.sum(ds, axis=0, keepdims=True)
                    folded = ds[:, :128]
                    for b in range(1, reps):
                        folded = folded + ds[:, b * 128:(b + 1) * 128]
                    k_side[h] += folded
                    pb = p.astype(BF16)
                    dsb = ds.astype(BF16)
                    do_h = jnp.where(left, do_rows, zr) if h == 0 else jnp.where(left, zr, do_rows)
                    q_h = jnp.where(left, q_rows, zr) if h == 0 else jnp.where(left, zr, q_rows)
                    dv_acc[...] += jnp.dot(pb, do_h, preferred_element_type=F32)
                    dk_acc[...] += jnp.dot(dsb, q_h, preferred_element_type=F32)
                    part = jnp.dot(kms[h], dsb, preferred_element_type=F32)
                    dq_part = part if dq_part is None else dq_part + part
                dq_t_acc[qi] += dq_part
                return carry

            step(kj, 0, True)
            lax.fori_loop(kj + 1, nt, lambda qi, c: step(qi, c, False), 0)
            dk_ref[pl.ds(k0, tile), :] = dk_acc[...].astype(BF16)
            dv_ref[pl.ds(k0, tile), :] = dv_acc[...].astype(BF16)
            for h in (0, 1):
                dcum_ref[pl.ds((2 * pid + h) * nt + kj, 1), :] = -jnp.sum(k_side[h].T, axis=0, keepdims=True)
            return carry

        lax.fori_loop(0, nt, kv_loop, 0)

        def fin(i, carry):
            r0 = pl.multiple_of(i * tile, tile)
            dq_ref[pl.ds(r0, tile), :] = dq_t_acc[i].T.astype(BF16)
            for h in (0, 1):
                dcum_ref[pl.ds((2 * pid + h) * nt + i, 1), :] += q_side[pl.ds(h * nt + i, 1), :]
            return carry

        lax.fori_loop(0, nt, fin, 0)
        if ns:
            pl.when(pid == pairs - 1)(finish)

    blk = pl.BlockSpec((s_len, HEAD_PAIR), lambda p: (0, p))
    full = pl.BlockSpec(cumr.shape, lambda p: (0, 0))
    transposed = pltpu.VMEM((nt, HEAD_PAIR, tile), BF16)
    sems = [pltpu.SemaphoreType.DMA((3 * ns,))] * 2 if ns else []
    dq, dk, dv, dcum, *arrived = pl.pallas_call(
        body, name="flash_bwd_scatter" if ns else "flash_bwd", grid=(pairs,),
        in_specs=[blk, blk, blk, blk, blk, full, full] + [ANY] * ns,
        out_specs=[blk, blk, blk, full] + [ANY] * ns,
        out_shape=[jax.ShapeDtypeStruct((s_len, width), BF16)] * 3 + [jax.ShapeDtypeStruct(cumr.shape, F32)]
        + [jax.ShapeDtypeStruct(t.shape, t.dtype) for t in outgoing],
        scratch_shapes=[transposed, transposed, transposed, pltpu.VMEM((s_len, HEAD_PAIR), BF16),
                        pltpu.VMEM((2, s_len, HEAD_PAIR), F32), pltpu.VMEM((nt, HEAD_PAIR, tile), F32),
                        pltpu.VMEM((2 * nt, tile), F32), pltpu.VMEM((2 * nt, tile), F32),
                        pltpu.VMEM((tile, HEAD_PAIR), F32), pltpu.VMEM((tile, HEAD_PAIR), F32),
                        pltpu.VMEM((2, tile, HEAD_PAIR), F32)] + sems,
        compiler_params=pltpu.CompilerParams(dimension_semantics=("arbitrary",)),
    )(q, k, v, o, do, lse, cumr, *outgoing)
    return (dq, dk, dv, dcum.reshape(cum.shape), *arrived)


def _out_ln(a, gate, x, w, g, b, name, target=None):
    s_len, d = x.shape
    tm = min(512, s_len)
    nt = 0 if target is None else 1

    def body(*refs):
        a_ref, gate_ref, x_ref, w_ref, g_ref, b_ref = refs[:6]
        first_ref, xh_ref, rs_ref, yg_ref = refs[6 + nt:10 + nt]
        gt = gate_ref[...]
        yg = (a_ref[...] * (gt * _sigmoid(gt))).astype(BF16)
        yg_ref[...] = yg
        z = ALPHA * x_ref[...] + jnp.dot(yg, w_ref[...], preferred_element_type=F32)
        zc = z - jnp.mean(z, axis=1, keepdims=True)
        rstd = lax.rsqrt(jnp.mean(zc * zc, axis=1, keepdims=True) + LN_EPS)
        xh = zc * rstd
        xh_ref[...] = xh
        y = xh * g_ref[...] + b_ref[...]
        rs_ref[...] = jnp.broadcast_to(rstd, (tm, 128))
        if nt:
            sq_ref = refs[10 + nt]

            @pl.when(pl.program_id(0) == 0)
            def _():
                sq_ref[...] = jnp.zeros_like(sq_ref)

            diff = y - refs[6][...]
            first_ref[...] = diff * (1.0 / d)
            sq_ref[...] += jnp.sum(diff * diff, axis=0, keepdims=True)
        else:
            first_ref[...] = y

    row = pl.BlockSpec((tm, d), lambda i: (i, 0))
    vec = pl.BlockSpec((1, d), lambda i: (0, 0))
    return pl.pallas_call(
        body, name=name + "_loss" if nt else name, grid=(s_len // tm,),
        in_specs=[row, row, row, pl.BlockSpec(w.shape, lambda i: (0, 0)), vec, vec] + [row] * nt,
        out_specs=[row, row, pl.BlockSpec((tm, 128), lambda i: (i, 0)), row] + [vec] * nt,
        out_shape=[jax.ShapeDtypeStruct((s_len, d), F32), jax.ShapeDtypeStruct((s_len, d), F32),
                   jax.ShapeDtypeStruct((s_len, 128), F32), jax.ShapeDtypeStruct((s_len, d), BF16)]
        + [jax.ShapeDtypeStruct((1, d), F32)] * nt,
        compiler_params=pltpu.CompilerParams(dimension_semantics=("arbitrary" if nt else "parallel",)),
    )(a, gate, x, w, g, b, *([target] if nt else []))


def _ln_bwd(dout, xh, rs, g, w, gate, a, name, swap=None):
    s_len, d = dout.shape
    tm = min(512, s_len)
    steps = s_len // tm
    ns = 0 if swap is None else 1

    def body(*refs):
        do_ref, xh_ref, rs_ref, g_ref, w_ref, gate_ref, a_ref = refs[:7]
        dz_ref, da_ref, dgate_ref, dg_ref, db_ref = refs[7 + ns:12 + ns]
        if ns:
            start, finish = _swap_steps(refs[7:8], refs[12 + ns:13 + ns], [swap[1]], *refs[13 + ns:])
            pl.when(pl.program_id(0) == 0)(start)

        @pl.when(pl.program_id(0) == 0)
        def _():
            dg_ref[...] = jnp.zeros_like(dg_ref)
            db_ref[...] = jnp.zeros_like(db_ref)

        dout_t = do_ref[...]
        xh_t = xh_ref[...]
        dg_ref[...] += jnp.sum(dout_t * xh_t, axis=0, keepdims=True)
        db_ref[...] += jnp.sum(dout_t, axis=0, keepdims=True)
        dxh = dout_t * g_ref[...]
        m1 = jnp.mean(dxh, axis=1, keepdims=True)
        m2 = jnp.mean(dxh * xh_t, axis=1, keepdims=True)
        dz = rs_ref[:, 0:1] * (dxh - m1 - xh_t * m2)
        dz_ref[...] = dz
        dyg = lax.dot_general(dz.astype(BF16), w_ref[...], NT_DIMS, preferred_element_type=F32)
        gt = gate_ref[...]
        sg = _sigmoid(gt)
        da_ref[...] = dyg * (gt * sg)
        dgate_ref[...] = (dyg * a_ref[...] * (sg * (1.0 + gt * (1.0 - sg)))).astype(BF16)
        if ns:
            pl.when(pl.program_id(0) == steps - 1)(finish)

    row = pl.BlockSpec((tm, d), lambda i: (i, 0))
    vec = pl.BlockSpec((1, d), lambda i: (0, 0))
    extra_out = [jax.ShapeDtypeStruct((N_CHIPS, swap[1][1], LANES), swap[0].dtype)] if ns else []
    return pl.pallas_call(
        body, name=name + "_swap" if ns else name, grid=(steps,),
        in_specs=[row, row, pl.BlockSpec((tm, 128), lambda i: (i, 0)), vec, pl.BlockSpec(w.shape, lambda i: (0, 0)),
                  row, row] + [ANY] * ns,
        out_specs=[row, row, row, vec, vec] + [ANY] * ns,
        out_shape=[jax.ShapeDtypeStruct((s_len, d), F32), jax.ShapeDtypeStruct((s_len, d), F32),
                   jax.ShapeDtypeStruct((s_len, d), BF16), jax.ShapeDtypeStruct((1, d), F32),
                   jax.ShapeDtypeStruct((1, d), F32)] + extra_out,
        scratch_shapes=[pltpu.SemaphoreType.DMA((N_CHIPS,))] * 2 if ns else [],
        compiler_params=pltpu.CompilerParams(dimension_semantics=("arbitrary",)),
    )(dout, xh, rs, g, w, gate, a, *([swap[0]] if ns else []))


def _rnn_fwd(u, conv_w, conv_b, wa, ba, wi, bi, lam):
    s_len, width = u.shape
    tm = min(512, s_len // 2)
    nb = width // RNN_BLOCK

    def body(u_ref, up_ref, cw_ref, cb_ref, wa_ref, ba_ref, wi_ref, bi_ref, lam_ref,
             h_ref, uc_ref, r_ref, i_ref, a_ref, b_scr, h_carry):
        step_id = pl.program_id(0)

        @pl.when(step_id == 0)
        def _():
            h_carry[...] = jnp.zeros_like(h_carry)

        for n in range(nb):
            blk = slice(n * RNN_BLOCK, (n + 1) * RNN_BLOCK)
            ut = u_ref[:, blk]
            prev = jnp.where(step_id > 0, up_ref[:, blk], 0.0)
            uc = cb_ref[:, blk] + cw_ref[3:4, blk] * ut
            for k in (1, 2, 3):
                uc = uc + cw_ref[3 - k:4 - k, blk] * _shift_down(ut, prev, k)
            ucb = uc.astype(BF16)
            r = _sigmoid(jnp.dot(ucb, wa_ref[n], preferred_element_type=F32) + ba_ref[:, blk])
            ig = _sigmoid(jnp.dot(ucb, wi_ref[n], preferred_element_type=F32) + bi_ref[:, blk])
            log_a = -LRU_C * r * _softplus(-lam_ref[:, blk])
            uc_ref[:, blk] = uc
            r_ref[:, blk] = r
            i_ref[:, blk] = ig
            a_ref[:, blk] = jnp.exp(log_a)
            b_scr[:, blk] = jnp.sqrt(_neg_expm1(2.0 * log_a)) * (ig * uc)

        def scan(t, h):
            h = a_ref[pl.ds(t, 1), :] * h + b_scr[pl.ds(t, 1), :]
            h_ref[pl.ds(t, 1), :] = h
            return h

        h_carry[...] = lax.fori_loop(0, tm, scan, h_carry[...], unroll=8)

    row = pl.BlockSpec((tm, width), lambda i: (i, 0))
    prev8 = pl.BlockSpec((8, width), lambda i: (jnp.maximum(i * (tm // 8) - 1, 0), 0))
    vec = pl.BlockSpec((1, width), lambda i: (0, 0))
    mat = pl.BlockSpec(wa.shape, lambda i: (0, 0, 0))
    return pl.pallas_call(
        body, name="rnn_fwd", grid=(s_len // tm,),
        in_specs=[row, prev8, pl.BlockSpec(conv_w.shape, lambda i: (0, 0)), vec, mat, vec, mat, vec, vec],
        out_specs=[row] * 5,
        out_shape=[jax.ShapeDtypeStruct((s_len, width), F32)] * 5,
        scratch_shapes=[pltpu.VMEM((tm, width), F32), pltpu.VMEM((1, width), F32)],
        compiler_params=pltpu.CompilerParams(dimension_semantics=("arbitrary",)),
    )(u, u, conv_w, conv_b, wa, ba, wi, bi, lam)


def _rnn_bwd(dh, a, h, r, ig, uc, lam, wa, wi):
    s_len, width = dh.shape
    tm = min(512, s_len // 2)
    nt = s_len // tm
    nb = width // RNN_BLOCK

    def body(dh_ref, a_ref, h_ref, hp_ref, r_ref, i_ref, uc_ref, lam_ref, wa_ref, wi_ref,
             duc_ref, dwa_ref, dwi_ref, dba_ref, dbi_ref, dlam_ref, g_scr, c_scr, dsp_scr):
        step_id = pl.program_id(0)
        tile_id = nt - 1 - step_id

        @pl.when(step_id == 0)
        def _():
            c_scr[...] = jnp.zeros_like(c_scr)
            dsp_scr[...] = jnp.zeros_like(dsp_scr)
            dwa_ref[...] = jnp.zeros_like(dwa_ref)
            dwi_ref[...] = jnp.zeros_like(dwi_ref)
            dba_ref[...] = jnp.zeros_like(dba_ref)
            dbi_ref[...] = jnp.zeros_like(dbi_ref)

        def scan(j, c):
            t = tm - 1 - j
            g = dh_ref[pl.ds(t, 1), :] + c
            g_scr[pl.ds(t, 1), :] = g
            return a_ref[pl.ds(t, 1), :] * g

        c_scr[...] = lax.fori_loop(0, tm, scan, c_scr[...], unroll=8)

        for n in range(nb):
            blk = slice(n * RNN_BLOCK, (n + 1) * RNN_BLOCK)
            g_t = g_scr[:, blk]
            hp8 = jnp.where(tile_id > 0, hp_ref[:, blk], 0.0)
            h_prev = _shift_down(h_ref[:, blk], hp8, 1)
            av, rv, iv, ucv = a_ref[:, blk], r_ref[:, blk], i_ref[:, blk], uc_ref[:, blk]
            sp = _softplus(-lam_ref[:, blk])
            mag = jnp.sqrt(_neg_expm1(-2.0 * LRU_C * rv * sp))
            d_iu = g_t * mag
            dla = g_t * h_prev * av - g_t * (iv * ucv) * (av * av) / mag
            dsp_scr[:, blk] += jnp.sum(dla * (-LRU_C * rv), axis=0, keepdims=True)
            dra = dla * (-LRU_C * sp) * rv * (1.0 - rv)
            dia = d_iu * ucv * iv * (1.0 - iv)
            drab, diab, ucb = dra.astype(BF16), dia.astype(BF16), ucv.astype(BF16)
            duc_ref[:, blk] = (d_iu * iv
                               + lax.dot_general(drab, wa_ref[n], NT_DIMS, preferred_element_type=F32)
                               + lax.dot_general(diab, wi_ref[n], NT_DIMS, preferred_element_type=F32))
            dwa_ref[n] += lax.dot_general(ucb, drab, TN_DIMS, preferred_element_type=F32)
            dwi_ref[n] += lax.dot_general(ucb, diab, TN_DIMS, preferred_element_type=F32)
            dba_ref[:, blk] += jnp.sum(dra, axis=0, keepdims=True)
            dbi_ref[:, blk] += jnp.sum(dia, axis=0, keepdims=True)

        @pl.when(step_id == nt - 1)
        def _():
            dlam_ref[...] = -dsp_scr[...] * _sigmoid(-lam_ref[...])

    row = pl.BlockSpec((tm, width), lambda i: (nt - 1 - i, 0))
    prev8 = pl.BlockSpec((8, width), lambda i: (jnp.maximum((nt - 1 - i) * (tm // 8) - 1, 0), 0))
    vec = pl.BlockSpec((1, width), lambda i: (0, 0))
    mat = pl.BlockSpec(wa.shape, lambda i: (0, 0, 0))
    return pl.pallas_call(
        body, name="rnn_bwd", grid=(nt,),
        in_specs=[row, row, row, prev8, row, row, row, vec, mat, mat],
        out_specs=[row, mat, mat, vec, vec, vec],
        out_shape=[jax.ShapeDtypeStruct((s_len, width), F32), jax.ShapeDtypeStruct(wa.shape, F32),
                   jax.ShapeDtypeStruct(wa.shape, F32)] + [jax.ShapeDtypeStruct((1, width), F32)] * 3,
        scratch_shapes=[pltpu.VMEM((tm, width), F32), pltpu.VMEM((1, width), F32), pltpu.VMEM((1, width), F32)],
        compiler_params=pltpu.CompilerParams(dimension_semantics=("arbitrary",)),
    )(dh, a, h, h, r, ig, uc, lam, wa, wi)


def _conv_bwd(duc, u, conv_w):
    s_len, width = duc.shape
    tm = min(256, s_len)
    nt = s_len // tm

    def body(d_ref, dn_ref, u_ref, up_ref, cw_ref, du_ref, dcw_ref, dcb_ref):
        step_id = pl.program_id(0)

        @pl.when(step_id == 0)
        def _():
            dcw_ref[...] = jnp.zeros_like(dcw_ref)
            dcb_ref[...] = jnp.zeros_like(dcb_ref)

        for n in range(width // RNN_BLOCK):
            blk = slice(n * RNN_BLOCK, (n + 1) * RNN_BLOCK)
            dt = d_ref[:, blk]
            ut = u_ref[:, blk]
            nxt = jnp.where(step_id < nt - 1, dn_ref[:, blk], 0.0)
            prev = jnp.where(step_id > 0, up_ref[:, blk], 0.0)
            du = cw_ref[3:4, blk] * dt
            dcw_ref[3:4, blk] += jnp.sum(dt * ut, axis=0, keepdims=True)
            for k in (1, 2, 3):
                du = du + cw_ref[3 - k:4 - k, blk] * _shift_up(dt, nxt, k)
                dcw_ref[3 - k:4 - k, blk] += jnp.sum(dt * _shift_down(ut, prev, k), axis=0, keepdims=True)
            du_ref[:, blk] = du.astype(BF16)
            dcb_ref[:, blk] += jnp.sum(dt, axis=0, keepdims=True)

    row = pl.BlockSpec((tm, width), lambda i: (i, 0))
    prev8 = pl.BlockSpec((8, width), lambda i: (jnp.maximum(i * (tm // 8) - 1, 0), 0))
    next8 = pl.BlockSpec((8, width), lambda i: (jnp.minimum((i + 1) * (tm // 8), s_len // 8 - 1), 0))
    return pl.pallas_call(
        body, name="conv_bwd", grid=(nt,),
        in_specs=[row, next8, row, prev8, pl.BlockSpec(conv_w.shape, lambda i: (0, 0))],
        out_specs=[row, pl.BlockSpec(conv_w.shape, lambda i: (0, 0)), pl.BlockSpec((1, width), lambda i: (0, 0))],
        out_shape=[jax.ShapeDtypeStruct((s_len, width), BF16), jax.ShapeDtypeStruct(conv_w.shape, F32),
                   jax.ShapeDtypeStruct((1, width), F32)],
        compiler_params=pltpu.CompilerParams(dimension_semantics=("arbitrary",)),
    )(duc, duc, u, u, conv_w)


def _pair_sum(core, grads, theirs, first_row, out_dtype):
    n, half, _ = theirs.shape
    tb = 128 if half % 128 == 0 and first_row % 128 == 0 else half
    assert first_row % tb == 0 and half % tb == 0

    def body(c_ref, a_ref, b_ref, o_ref):
        o_ref[...] = (a_ref[...] + b_ref[...]).astype(out_dtype)

    blk = pl.BlockSpec((n, tb, LANES), lambda i, c: (0, i, 0))
    mine = pl.BlockSpec((n, tb, LANES), lambda i, c: (0, first_row // tb + c[0] * (half // tb) + i, 0))
    return pl.pallas_call(
        body, name="pair_sum",
        grid_spec=pltpu.PrefetchScalarGridSpec(
            num_scalar_prefetch=1, grid=(half // tb,), in_specs=[mine, blk], out_specs=blk),
        out_shape=jax.ShapeDtypeStruct((n, half, LANES), out_dtype),
        compiler_params=pltpu.CompilerParams(dimension_semantics=("parallel",)),
    )(core, grads, theirs)


def _sum_chips(parts):
    rows = parts.shape[1]
    tb = 128 if rows % 128 == 0 else rows

    def body(p_ref, o_ref):
        o_ref[...] = ((p_ref[0].astype(F32) + p_ref[1].astype(F32)) + p_ref[2].astype(F32)) + p_ref[3].astype(F32)

    return pl.pallas_call(
        body, name="chip_sum", grid=(rows // tb,),
        in_specs=[pl.BlockSpec((N_CHIPS, tb, LANES), lambda i: (0, i, 0))],
        out_specs=pl.BlockSpec((tb, LANES), lambda i: (i, 0)),
        out_shape=jax.ShapeDtypeStruct((rows, LANES), F32),
        compiler_params=pltpu.CompilerParams(dimension_semantics=("parallel",)),
    )(parts)


def _adamw(w, g, m, v, lead_per_block, rows_per_block):
    n, rows, cols = w.shape
    blk = pl.BlockSpec((lead_per_block, rows_per_block, cols), lambda i, j: (i, j, 0))

    def body(w_ref, g_ref, m_ref, v_ref, d_ref, nm_ref, nv_ref):
        gt = g_ref[...]
        nm = ADAM_B1 * m_ref[...] + (1.0 - ADAM_B1) * gt
        nv = ADAM_B2 * v_ref[...] + (1.0 - ADAM_B2) * (gt * gt)
        m_hat = nm / (1.0 - ADAM_B1 ** ADAM_STEP)
        v_hat = nv / (1.0 - ADAM_B2 ** ADAM_STEP)
        d_ref[...] = -ADAM_LR * (m_hat / (jnp.sqrt(v_hat) + ADAM_EPS) + ADAM_WD * w_ref[...])
        nm_ref[...] = nm
        nv_ref[...] = nv

    return pl.pallas_call(
        body, name="adamw", grid=(n // lead_per_block, rows // rows_per_block), in_specs=[blk] * 4,
        out_specs=[blk] * 3,
        out_shape=[jax.ShapeDtypeStruct(w.shape, F32)] * 3,
        compiler_params=pltpu.CompilerParams(dimension_semantics=("parallel", "parallel")),
    )(w, g, m, v)


def _place():
    x, y, c = lax.axis_index("x"), lax.axis_index("y"), lax.axis_index("c")
    return x, y, c, [(1 - x, y), (x, 1 - y), (1 - x, 1 - y)]


ANY = pl.BlockSpec(memory_space=pl.ANY)
COPY_PARTS = 4


def _remote_parts(src_of, dst_of, rows, send_sem, recv_sem, to, begin=True):
    parts = COPY_PARTS if rows % (16 * COPY_PARTS) == 0 else 1
    step = rows // parts
    for i in range(parts if begin is not False else 0):
        pltpu.make_async_remote_copy(src_ref=src_of(i * step, step), dst_ref=dst_of(i * step, step),
                                     send_sem=send_sem, recv_sem=recv_sem, device_id=to, device_id_type=MESH).start()
    if begin == "only":
        return None
    return pltpu.make_async_remote_copy(src_ref=src_of(0, rows), dst_ref=dst_of(0, rows), send_sem=send_sem,
                                        recv_sem=recv_sem, device_id=to, device_id_type=MESH)


GATHER_SEMS = 7


def _gather_steps(p_refs, o_refs, send_sems, recv_sems):
    x, y, c, chips = _place()
    me = 2 * x + y

    def half(ref, which):
        rows = ref.shape[0] // 2
        return ref.at[pl.ds(which * rows, rows)]

    def copy(k, src, dst, to):
        return pltpu.make_async_remote_copy(src_ref=src, dst_ref=dst, send_sem=send_sems.at[k],
                                            recv_sem=recv_sems.at[k], device_id=to, device_id_type=MESH)

    def first_copies():
        out = []
        for b, (p_ref, o_ref) in enumerate(zip(p_refs, o_refs)):
            for j, (cx, cy) in enumerate(chips):
                out.append(copy(GATHER_SEMS * b + j, half(p_ref, c), half(o_ref.at[me], c), (cx, cy, c)))
            out.append(copy(GATHER_SEMS * b + 6, p_ref, o_ref.at[me], (x, y, 1 - c)))
        return out

    def passed_copies():
        out = []
        for b, o_ref in enumerate(o_refs):
            for j, (cx, cy) in enumerate(chips):
                landed = half(o_ref.at[2 * cx + cy], c)
                out.append(copy(GATHER_SEMS * b + 3 + j, landed, landed, (x, y, 1 - c)))
        return out

    def start():
        for cp in first_copies():
            cp.start()

    def forward():
        for b, o_ref in enumerate(o_refs):
            for j, (cx, cy) in enumerate(chips):
                landed = half(o_ref.at[2 * cx + cy], c)
                copy(GATHER_SEMS * b + j, landed, landed, (x, y, c)).wait_recv()
        for cp in passed_copies():
            cp.start()

    def finish():
        for b, o_ref in enumerate(o_refs):
            for j, (cx, cy) in enumerate(chips):
                other = half(o_ref.at[2 * cx + cy], 1 - c)
                copy(GATHER_SEMS * b + 3 + j, other, other, (x, y, c)).wait_recv()
            copy(GATHER_SEMS * b + 6, o_ref.at[me], o_ref.at[me], (x, y, c)).wait_recv()
        for cp in first_copies() + passed_copies():
            cp.wait_send()

    return start, forward, finish


def _gather_chips(shards):
    nb = len(shards)

    def body(*refs):
        for step in _gather_steps(refs[:nb], refs[nb:2 * nb], *refs[2 * nb:]):
            step()

    return pl.pallas_call(
        body, name="gather_chips", in_specs=[ANY] * nb, out_specs=[ANY] * nb,
        out_shape=[jax.ShapeDtypeStruct((N_CHIPS,) + s.shape, s.dtype) for s in shards],
        scratch_shapes=[pltpu.SemaphoreType.DMA((GATHER_SEMS * nb,)), pltpu.SemaphoreType.DMA((GATHER_SEMS * nb,))],
    )(*shards)


def _swap_steps(g_refs, t_refs, spans, send_sems, recv_sems):
    x, y, c, _ = _place()

    def gives(begin):
        return [_remote_parts(
            lambda r0, m, g_ref=g_ref, j=j, base=first_row + (1 - c) * half: g_ref.at[j, pl.ds(base + r0, m), :],
            lambda r0, m, t_ref=t_ref, j=j: t_ref.at[j, pl.ds(r0, m), :],
            half, send_sems.at[b * N_CHIPS + j], recv_sems.at[b * N_CHIPS + j], (x, y, 1 - c), begin)
            for b, (g_ref, t_ref, (first_row, half)) in enumerate(zip(g_refs, t_refs, spans)) for j in range(N_CHIPS)]

    def start():
        gives("only")

    def finish():
        for give in gives(False):
            give.wait()

    return start, finish


def _swap_halves(bufs, spans):
    nb = len(bufs)

    def body(*refs):
        for step in _swap_steps(refs[:nb], refs[nb:2 * nb], spans, *refs[2 * nb:]):
            step()

    return pl.pallas_call(
        body, name="swap_halves", in_specs=[ANY] * nb, out_specs=[ANY] * nb,
        out_shape=[jax.ShapeDtypeStruct((b.shape[0], half, b.shape[2]), b.dtype) for b, (_, half) in zip(bufs, spans)],
        scratch_shapes=[pltpu.SemaphoreType.DMA((nb * N_CHIPS,)), pltpu.SemaphoreType.DMA((nb * N_CHIPS,))],
    )(*bufs)


def _scatter_steps(t_refs, o_refs, send_sems, recv_sems):
    x, y, c, chips = _place()
    me = 2 * x + y

    def slot(ref, chip):
        return lambda r0, n: ref.at[chip, pl.ds(r0, n), :]

    def sends(begin):
        return [_remote_parts(slot(t_ref, 2 * cx + cy), slot(o_ref, me), t_ref.shape[1], send_sems.at[3 * b + j],
                              recv_sems.at[3 * b + j], (cx, cy, c), begin)
                for b, (t_ref, o_ref) in enumerate(zip(t_refs, o_refs)) for j, (cx, cy) in enumerate(chips)]

    def start():
        sends("only")

    def finish():
        for b, o_ref in enumerate(o_refs):
            for j, (cx, cy) in enumerate(chips):
                landed = o_ref.at[2 * cx + cy]
                pltpu.make_async_remote_copy(src_ref=landed, dst_ref=landed, send_sem=send_sems.at[3 * b + j],
                                             recv_sem=recv_sems.at[3 * b + j], device_id=(x, y, c),
                                             device_id_type=MESH).wait_recv()
        for cp in sends(False):
            cp.wait_send()

    return start, finish


def _give_sibling(bufs):
    nb = len(bufs)

    def body(*refs):
        h_refs, o_refs, (send_sems, recv_sems) = refs[:nb], refs[nb:2 * nb], refs[2 * nb:]
        x, y, c, _ = _place()
        gives = [_remote_parts(lambda r0, n, h_ref=h_ref: h_ref.at[pl.ds(r0, n), :],
                               lambda r0, n, o_ref=o_ref: o_ref.at[pl.ds(r0, n), :],
                               h_ref.shape[0], send_sems.at[b], recv_sems.at[b], (x, y, 1 - c))
                 for b, (h_ref, o_ref) in enumerate(zip(h_refs, o_refs))]
        for give in gives:
            give.wait()

    return pl.pallas_call(
        body, name="give_sibling", in_specs=[ANY] * nb, out_specs=[ANY] * nb,
        out_shape=[jax.ShapeDtypeStruct(b.shape, b.dtype) for b in bufs],
        scratch_shapes=[pltpu.SemaphoreType.DMA((nb,)), pltpu.SemaphoreType.DMA((nb,))],
    )(*bufs)


def _pack(arrays, dtype, row_align):
    flat = []
    for arr in arrays:
        v = arr.reshape(-1)
        flat.append(jnp.pad(v, (0, (-v.shape[0]) % LANES)))
    total = sum(f.shape[0] for f in flat) // LANES
    pad_rows = (-total) % row_align
    if pad_rows:
        flat.append(jnp.zeros((pad_rows * LANES,), dtype))
    return jnp.concatenate(flat).reshape(-1, LANES)


def _unpack(buf, shapes, rows_align=1):
    lead = buf.shape[:-2]
    flat = buf.reshape(lead + (-1,))
    out, off = [], 0
    for shape in shapes:
        size = 1
        for dim in shape:
            size *= dim
        out.append(flat[..., off:off + size].reshape(lead + tuple(shape)))
        off += size + (-size) % (LANES * rows_align)
    return out


def _from_chips(parts, axis):
    return jnp.concatenate([parts[j] for j in range(N_CHIPS)], axis=axis)


def kernel(x, ln_g, ln_b, attn_w_in, attn_b_f, attn_w_out, rnn_w_in, rnn_conv_w, rnn_conv_b, rnn_w_a, rnn_b_a, rnn_w_i, rnn_b_i, rnn_lambda, rnn_w_out, loss_target, m_ln_g, m_ln_b, m_attn_w_in, m_attn_b_f, m_attn_w_out, m_rnn_w_in, m_rnn_conv_w, m_rnn_conv_b, m_rnn_w_a, m_rnn_b_a, m_rnn_w_i, m_rnn_b_i, m_rnn_lambda, m_rnn_w_out, v_ln_g, v_ln_b, v_attn_w_in, v_attn_b_f, v_attn_w_out, v_rnn_w_in, v_rnn_conv_w, v_rnn_conv_b, v_rnn_w_a, v_rnn_b_a, v_rnn_w_i, v_rnn_b_i, v_rnn_lambda, v_rnn_w_out):
    s_len, d = x.shape[1], x.shape[2]
    xs = x.reshape(s_len, d)
    target = loss_target.reshape(s_len, d)
    heads = attn_b_f.shape[1]
    qkvg = attn_w_in.shape[2] * N_CHIPS - heads

    me = 2 * lax.axis_index("x") + lax.axis_index("y")
    core = lax.axis_index("c").astype(jnp.int32)
    small = [rnn_conv_w, rnn_conv_b, rnn_b_a, rnn_b_i, rnn_lambda]
    a_in, a_out = attn_w_in.astype(BF16), attn_w_out.astype(BF16)
    later = [a_in[1], a_out] + [w.astype(BF16) for w in (rnn_w_in, rnn_w_a, rnn_w_i, rnn_w_out)]
    got_in, got_small = _gather_chips([a_in[0], _pack(small, F32, 16)])
    conv_w, conv_b, b_a, b_i, lam = [
        _from_chips(p, ax) for p, ax in zip(_unpack(got_small, [w.shape for w in small]), (2, 1, 1, 1, 1))]

    def attn_in_weights(w_in):
        shard = w_in.shape[2]

        def columns(first, last):
            return [w_in[j][:, max(first - j * shard, 0):min(last - j * shard, shard)]
                    for j in range(N_CHIPS) if first < (j + 1) * shard and last > j * shard]

        return jnp.concatenate([t * (HEAD_DIM ** -0.5) for t in columns(0, d)] + columns(d, qkvg + heads)
                               + [jnp.zeros((d, 128 - heads), BF16)], axis=1)

    w_cat = [attn_in_weights(got_in), None]
    b_f = jnp.pad(attn_b_f, ((0, 0), (0, 128 - heads)))

    saved = []
    cur = xs
    for layer in range(DEPTH):
        idx = layer // 2
        g_l, b_l = ln_g[layer:layer + 1], ln_b[layer:layer + 1]
        goal = target if layer == DEPTH - 1 else None
        if layer % 2 == 0:
            q, k, v, gate, fl = _proj(cur, w_cat[idx], [(0, d, BF16), (d, d, BF16), (2 * d, d, BF16),
                                                         (3 * d, d, F32), (4 * d, 128, F32)], "attn_proj")
            cum, sneg = _fcum(fl, b_f[idx:idx + 1])
            o, lse, *rest = _flash_fwd(q, k, v, cum, later if layer == 0 else ())
            if layer == 0:
                w_cat[1] = attn_in_weights(rest[0])
                w_out_a = jnp.moveaxis(rest[1], 0, 1).reshape(2, d, d)
                w_in_r = jnp.moveaxis(rest[2], 0, 2).reshape(2, d, 2 * d)
                w_a = jnp.transpose(rest[3], (1, 2, 0, 3, 4)).reshape(2, -1, RNN_BLOCK, RNN_BLOCK)
                w_i = jnp.transpose(rest[4], (1, 2, 0, 3, 4)).reshape(2, -1, RNN_BLOCK, RNN_BLOCK)
                w_out_r = jnp.moveaxis(rest[5], 0, 1).reshape(2, d, d)
            nxt, xh, rs, yg, *sq = _out_ln(o, gate, cur, w_out_a[idx], g_l, b_l, "out_ln", goal)
            saved.append(dict(x=cur, q=q, k=k, v=v, gate=gate, cum=cum, sneg=sneg, a=o, lse=lse, xh=xh, rs=rs, yg=yg))
        else:
            u, gate = _proj(cur, w_in_r[idx], [(0, d, F32), (d, d, F32)], "rnn_proj")
            vecs = [t[idx:idx + 1] for t in (conv_b, b_a, b_i, lam)]
            hseq, uc, r, ig, a = _rnn_fwd(u, conv_w[idx], vecs[0], w_a[idx], vecs[1], w_i[idx], vecs[2], vecs[3])
            nxt, xh, rs, yg, *sq = _out_ln(hseq, gate, cur, w_out_r[idx], g_l, b_l, "out_ln", goal)
            saved.append(dict(x=cur, u=u, gate=gate, uc=uc, r=r, ig=ig, av=a, a=hseq, xh=xh, rs=rs, yg=yg, lam=vecs[3]))
        cur = nxt

    dout = cur
    loss_here = 0.5 * jnp.sum(sq[0]) / d

    g_ln_g, g_ln_b = [None] * DEPTH, [None] * DEPTH
    g_attn = [None] * 2
    g_rnn = [None] * 2
    slots = None
    core1 = core.reshape(1)
    late_half = (SLOT_ROWS - LATE_ROWS) // 2

    def by_chip(t, axis):
        shape = t.shape[:axis] + (N_CHIPS, t.shape[axis] // N_CHIPS) + t.shape[axis + 1:]
        flat = jnp.moveaxis(t.reshape(shape), axis, 0).reshape(N_CHIPS, -1)
        return jnp.pad(flat, ((0, 0), (0, (-flat.shape[1]) % (8 * LANES)))).reshape(N_CHIPS, -1, LANES)

    def both(key):
        return jnp.stack([it[key] for it in g_rnn])

    for layer in reversed(range(DEPTH)):
        idx = layer // 2
        sv = saved[layer]
        swap = None
        if layer == 0:
            gates = jnp.concatenate([by_chip(both("w_a"), 2), by_chip(both("w_i"), 2)], axis=1)
            slots = lax.dynamic_update_slice(slots, gates, (0, ROWS_GATES, 0))
            swap = (slots, (LATE_ROWS, late_half))
        w_out = w_out_a[idx] if layer % 2 == 0 else w_out_r[idx]
        dz, da, dgate, dg, db, *theirs_late = _ln_bwd(dout, sv["xh"], sv["rs"], ln_g[layer:layer + 1], w_out,
                                                      sv["gate"], sv["a"], "ln_bwd", swap)
        if layer == 0:
            pair_late = _pair_sum(core1, slots, theirs_late[0], LATE_ROWS, BF16)
        g_ln_g[layer], g_ln_b[layer] = dg, db
        slots = _dw_out(slots, sv["yg"], dz, (ROWS_ATTN_OUT if layer % 2 == 0 else ROWS_RNN_OUT)[idx])
        if layer % 2 == 0:
            dq, dk, dv, dcum, *arrived = _flash_bwd(sv["q"], sv["k"], sv["v"], sv["a"], da, sv["lse"], sv["cum"],
                                                    [pair_late] if layer == 0 else ())
            dlogit, dbf = _fbwd(dcum, sv["sneg"])
            pieces = [dq, dk, dv, dgate, dlogit]
            if layer > 0:
                dout, = _mm_nt(dz, [(p, j * d) for j, p in enumerate(pieces)], w_cat[idx], "attn_dx")
            slots, d_w_f = _dw_attn_in(slots, sv["x"], pieces[:4], dlogit, idx)
            g_attn[idx] = dict(w_f=d_w_f[:, :heads], b_f=dbf[:, 0])
        else:
            duc, d_wa, d_wi, d_ba, d_bi, d_lam = _rnn_bwd(da, sv["av"], sv["a"], sv["r"], sv["ig"], sv["uc"], sv["lam"],
                                                          w_a[idx], w_i[idx])
            du, d_cw, d_cb = _conv_bwd(duc, sv["u"], conv_w[idx])
            dout, = _mm_nt(dz, [(du, 0), (dgate, d)], w_in_r[idx], "rnn_dx")
            slots = _dw_rnn_in(slots, sv["x"], (du, dgate), idx)
            g_rnn[idx] = dict(conv_w=d_cw, conv_b=d_cb[0], w_a=d_wa, b_a=d_ba[0], w_i=d_wi, b_i=d_bi[0], lam=d_lam[0])

    def everywhere(t):
        flat = t.reshape(-1)
        flat = jnp.pad(flat, (0, (-flat.shape[0]) % (8 * LANES))).reshape(-1, LANES)
        return jnp.broadcast_to(flat[None], (N_CHIPS,) + flat.shape)

    in_rows = jnp.stack([slots[1:, r:r + d, :heads] for r in ROWS_ATTN_IN], axis=1)
    edges = jnp.concatenate([in_rows, jnp.stack([it["w_f"] for it in g_attn])[None]])
    rows = [everywhere(edges), by_chip(both("conv_w"), 2),
            by_chip(both("conv_b"), 1), by_chip(both("b_a"), 1), by_chip(both("b_i"), 1), by_chip(both("lam"), 1),
            everywhere(jnp.concatenate(g_ln_g)), everywhere(jnp.concatenate(g_ln_b)),
            everywhere(jnp.stack([it["b_f"] for it in g_attn])), everywhere(loss_here)]
    used = sum(r.shape[1] for r in rows)
    small = jnp.concatenate(rows + [jnp.zeros((N_CHIPS, (-used) % 32, LANES), F32)], axis=1)

    spans = [(0, LATE_ROWS // 2), (0, small.shape[1] // 2)]
    theirs = _swap_halves([slots, small], spans)
    pair = [_pair_sum(core1, slots, theirs[0], 0, BF16), _pair_sum(core1, small, theirs[1], 0, F32)]
    dout, *arrived_last = _mm_nt(dz, [(p, j * d) for j, p in enumerate(pieces)], w_cat[0], "attn_dx", pair)
    grad_x = dout.reshape(x.shape)
    summed = []
    for mine_all, got in zip([pair_late] + pair, list(arrived) + arrived_last):
        own = lax.dynamic_index_in_dim(mine_all, me, 0, keepdims=False)
        summed.append(_sum_chips(lax.dynamic_update_index_in_dim(got, own, me, 0)))
    late, early, small_sum = [
        jnp.concatenate([jnp.where(core == 0, mine, other), jnp.where(core == 0, other, mine)])
        for mine, other in zip(summed, _give_sibling(summed))]

    def rows_at(first, n):
        return early[first:first + n] if first < LATE_ROWS else late[first - LATE_ROWS:first - LATE_ROWS + n]

    g_in_group = jnp.stack([rows_at(r, d) for r in ROWS_ATTN_IN])
    g_w_out_a = jnp.stack([rows_at(r, d // N_CHIPS) for r in ROWS_ATTN_OUT])
    g_w_out_r = jnp.stack([rows_at(r, d // N_CHIPS) for r in ROWS_RNN_OUT])
    folded = jnp.stack([rows_at(r, d // 2) for r in ROWS_RNN_IN])
    g_w_in_r = jnp.concatenate([folded[:, :, :d // 2], folded[:, :, d // 2:]], axis=1)
    gate_rows = rnn_w_a.size // LANES
    g_w_a = rows_at(ROWS_GATES, gate_rows).reshape(rnn_w_a.shape)
    g_w_i = rows_at(ROWS_GATES + gate_rows, gate_rows).reshape(rnn_w_i.shape)
    (g_edges, g_conv_w, g_conv_b, g_b_a, g_b_i, g_lam, g_ln_g_sum, g_ln_b_sum, g_b_f,
     loss) = _unpack(small_sum, [
        (N_CHIPS, 2, d, heads), rnn_conv_w.shape, rnn_conv_b.shape, rnn_b_a.shape,
        rnn_b_i.shape, rnn_lambda.shape, ln_g.shape, ln_b.shape, attn_b_f.shape, ()], rows_align=8)
    wide = jnp.concatenate([g_in_group, lax.dynamic_index_in_dim(g_edges, me, 0, keepdims=False)], axis=2)
    g_w_in_a = lax.dynamic_slice(wide, (0, 0, (heads // N_CHIPS) * me), attn_w_in.shape)

    def adam_nd(w, g, m, v, view, rows_per_block):
        outs = _adamw(w.reshape(view), g.reshape(view), m.reshape(view), v.reshape(view), 1, rows_per_block)
        return [t.reshape(w.shape) for t in outs]

    turned = [jnp.transpose(t, (2, 0, 1)) for t in (attn_w_in, g_w_in_a, m_attn_w_in, v_attn_w_in)]
    upd = {
        "attn_w_in": [jnp.transpose(t, (1, 2, 0)) for t in _adamw(*turned, attn_w_in.shape[2] // N_CHIPS, 2)],
        "attn_w_out": adam_nd(attn_w_out, g_w_out_a, m_attn_w_out, v_attn_w_out, attn_w_out.shape, 256),
        "rnn_w_in": adam_nd(rnn_w_in, g_w_in_r, m_rnn_w_in, v_rnn_w_in, rnn_w_in.shape, 512),
        "rnn_w_a": adam_nd(rnn_w_a, g_w_a, m_rnn_w_a, v_rnn_w_a, (1, -1, RNN_BLOCK), 512),
        "rnn_w_i": adam_nd(rnn_w_i, g_w_i, m_rnn_w_i, v_rnn_w_i, (1, -1, RNN_BLOCK), 512),
        "rnn_w_out": adam_nd(rnn_w_out, g_w_out_r, m_rnn_w_out, v_rnn_w_out, rnn_w_out.shape, 256),
    }
    smalls = [rnn_conv_w, rnn_conv_b, rnn_b_a, rnn_b_i, rnn_lambda, ln_g, ln_b, attn_b_f]
    g_small = [g_conv_w, g_conv_b, g_b_a, g_b_i, g_lam, g_ln_g_sum, g_ln_b_sum, g_b_f]
    m_small = [m_rnn_conv_w, m_rnn_conv_b, m_rnn_b_a, m_rnn_b_i, m_rnn_lambda, m_ln_g, m_ln_b, m_attn_b_f]
    v_small = [v_rnn_conv_w, v_rnn_conv_b, v_rnn_b_a, v_rnn_b_i, v_rnn_lambda, v_ln_g, v_ln_b, v_attn_b_f]
    packs = [_pack(t, F32, 16)[None] for t in (smalls, g_small, m_small, v_small)]
    small_out = [_unpack(t[0], [w.shape for w in smalls]) for t in _adamw(*packs, 1, 16)]
    for k, name in enumerate(("rnn_conv_w", "rnn_conv_b", "rnn_b_a", "rnn_b_i", "rnn_lambda", "ln_g", "ln_b",
                              "attn_b_f")):
        upd[name] = [small_out[0][k], small_out[1][k], small_out[2][k]]
    grad = {"attn_w_in": g_w_in_a, "attn_w_out": g_w_out_a, "rnn_w_in": g_w_in_r, "rnn_w_a": g_w_a, "rnn_w_i": g_w_i,
            "rnn_w_out": g_w_out_r, "rnn_conv_w": g_conv_w, "rnn_conv_b": g_conv_b, "rnn_b_a": g_b_a,
            "rnn_b_i": g_b_i, "rnn_lambda": g_lam, "ln_g": g_ln_g_sum, "ln_b": g_ln_b_sum, "attn_b_f": g_b_f}
    names = ("ln_g", "ln_b", "attn_w_in", "attn_b_f", "attn_w_out", "rnn_w_in", "rnn_conv_w", "rnn_conv_b",
             "rnn_w_a", "rnn_b_a", "rnn_w_i", "rnn_b_i", "rnn_lambda", "rnn_w_out")
    return (loss, grad_x, *[grad[n] for n in names], *[upd[n][0] for n in names], *[upd[n][1] for n in names],
            *[upd[n][2] for n in names])
```

```python
import jax
import jax.numpy as jnp
from jax import lax
from jax.experimental import pallas as pl
from jax.experimental.pallas import tpu as pltpu

F32 = jnp.float32
BF16 = jnp.bfloat16
MESH = pl.DeviceIdType.MESH

DEPTH = 4
HEAD_DIM = 64
HEAD_PAIR = 2 * HEAD_DIM
RNN_BLOCK = 256
CONV_WIDTH = 4
LRU_C = 8.0
ALPHA = (2.0 * DEPTH) ** 0.25
LN_EPS = 1e-5
ADAM_LR, ADAM_B1, ADAM_B2, ADAM_EPS, ADAM_WD, ADAM_STEP = 0.001, 0.9, 0.999, 1e-08, 0.01, 10
N_CHIPS = 4
LANES = 1024
NEG = -1e30

NT_DIMS = (((1,), (1,)), ((), ()))
TN_DIMS = (((0,), (0,)), ((), ()))


def _sigmoid(z):
    return 1.0 / (1.0 + jnp.exp(-z))


def _softplus(z):
    return jnp.maximum(z, 0.0) + jnp.log(1.0 + jnp.exp(-jnp.abs(z)))


def _neg_expm1(y):
    poly = y * (1.0 + y * (0.5 + y * (1.0 / 6.0 + y * (1.0 / 24.0))))
    return -jnp.where(y > -0.05, poly, jnp.exp(y) - 1.0)


def _shift_down(cur, prev8, k):
    rolled = pltpu.roll(cur, k, 0)
    row = lax.broadcasted_iota(jnp.int32, prev8.shape, 0)
    head = jnp.where(row < k, pltpu.roll(prev8, k, 0), rolled[:8])
    return jnp.concatenate([head, rolled[8:]], axis=0)


def _shift_up(cur, next8, k):
    n = cur.shape[0]
    rolled = pltpu.roll(cur, n - k, 0)
    row = lax.broadcasted_iota(jnp.int32, next8.shape, 0)
    tail = jnp.where(row >= 8 - k, pltpu.roll(next8, 8 - k, 0), rolled[n - 8:])
    return jnp.concatenate([rolled[:n - 8], tail], axis=0)


def _proj(x, w, outs, name):
    s_len, d = x.shape
    tm = min(512, s_len)

    def body(x_ref, w_ref, *o_refs):
        xb = x_ref[...].astype(BF16)
        for (c0, wd, dt), o_ref in zip(outs, o_refs):
            step = min(wd, 512)
            for cc in range(0, wd, step):
                o_ref[:, cc:cc + step] = jnp.dot(
                    xb, w_ref[:, c0 + cc:c0 + cc + step], preferred_element_type=F32).astype(dt)

    return pl.pallas_call(
        body, name=name, grid=(s_len // tm,),
        in_specs=[pl.BlockSpec((tm, d), lambda i: (i, 0)), pl.BlockSpec(w.shape, lambda i: (0, 0))],
        out_specs=[pl.BlockSpec((tm, wd), lambda i: (i, 0)) for _, wd, _ in outs],
        out_shape=[jax.ShapeDtypeStruct((s_len, wd), dt) for _, wd, dt in outs],
        compiler_params=pltpu.CompilerParams(dimension_semantics=("parallel",)),
    )(x, w)


def _mm_nt(dz, pieces, w, name, outgoing=()):
    s_len, d = dz.shape
    tm = min(512, s_len)
    steps = s_len // tm
    n = len(pieces)
    ns = len(outgoing)
    cols = [(c0, p.shape[1]) for p, c0 in pieces]

    def body(*refs):
        dz_ref, p_refs, w_ref, o_ref = refs[0], refs[1:1 + n], refs[1 + n], refs[2 + n + ns]
        if ns:
            start, finish = _scatter_steps(refs[2 + n:2 + n + ns], refs[3 + n + ns:3 + n + 2 * ns],
                                           *refs[3 + n + 2 * ns:])
            pl.when(pl.program_id(0) == 0)(start)
        acc = ALPHA * dz_ref[...]
        for (c0, wd), p_ref in zip(cols, p_refs):
            acc = acc + lax.dot_general(p_ref[...], w_ref[:, c0:c0 + wd], NT_DIMS, preferred_element_type=F32)
        o_ref[...] = acc
        if ns:
            pl.when(pl.program_id(0) == steps - 1)(finish)

    out, *arrived = pl.pallas_call(
        body, name=name + "_scatter" if ns else name, grid=(steps,),
        in_specs=[pl.BlockSpec((tm, d), lambda i: (i, 0))]
        + [pl.BlockSpec((tm, wd), lambda i: (i, 0)) for _, wd in cols]
        + [pl.BlockSpec(w.shape, lambda i: (0, 0))] + [ANY] * ns,
        out_specs=[pl.BlockSpec((tm, d), lambda i: (i, 0))] + [ANY] * ns,
        out_shape=[jax.ShapeDtypeStruct((s_len, d), F32)] + [jax.ShapeDtypeStruct(t.shape, t.dtype) for t in outgoing],
        scratch_shapes=[pltpu.SemaphoreType.DMA((3 * ns,))] * 2 if ns else [],
        compiler_params=pltpu.CompilerParams(dimension_semantics=("arbitrary" if ns else "parallel",)),
    )(dz, *[p for p, _ in pieces], w, *outgoing)
    return (out, *arrived)


ROWS_ATTN_IN = (0, 2048)
ROWS_ATTN_OUT = (1024, 1280)
ROWS_RNN_OUT = (1536, 1792)
ROWS_RNN_IN = (3072, 3584)
ROWS_GATES = 4096
LATE_ROWS = 1280
SLOT_ROWS = 4352


DW_ROWS = 1024


def _dw_call(body, name, slots, operands, specs, out_block, out_index, grid, semantics):
    return pl.pallas_call(
        body, name=name, grid=grid,
        in_specs=specs if slots is None else [ANY] + specs,
        out_specs=pl.BlockSpec(out_block, out_index),
        out_shape=jax.ShapeDtypeStruct((N_CHIPS, SLOT_ROWS, LANES), F32),
        input_output_aliases={} if slots is None else {0: 0},
        compiler_params=pltpu.CompilerParams(dimension_semantics=semantics),
    )(*(operands if slots is None else [slots] + operands))


def _dw_attn_in(slots, x, pieces, forget, layer):
    s_len, d = x.shape
    ts = min(s_len, DW_ROWS)
    steps = s_len // ts
    half = d // 2

    def body(g_ref, x_ref, p0, p1, p2, p3, f_ref, o_ref, wf_ref):
        lanes, step = pl.program_id(0), pl.program_id(1)

        @pl.when(step == 0)
        def _():
            o_ref[...] = jnp.zeros_like(o_ref)

        xb = x_ref[...].astype(BF16)
        for j, p_ref in enumerate((p0, p1, p2, p3)):
            o_ref[j] += lax.dot_general(xb, p_ref[...], TN_DIMS, preferred_element_type=F32)

        @pl.when(step == steps - 1)
        def _():
            o_ref[0] = o_ref[0] * (HEAD_DIM ** -0.5)

        @pl.when(lanes == 0)
        def _():
            @pl.when(step == 0)
            def _():
                wf_ref[...] = jnp.zeros_like(wf_ref)

            wf_ref[...] += lax.dot_general(xb, f_ref[...], TN_DIMS, preferred_element_type=F32)

    return pl.pallas_call(
        body, name="dw_attn_in", grid=(2, steps),
        in_specs=[ANY, pl.BlockSpec((ts, d), lambda i, s: (s, 0))] + [pl.BlockSpec((ts, half), lambda i, s: (s, i))] * 4
        + [pl.BlockSpec((ts, 128), lambda i, s: (s, 0))],
        out_specs=[pl.BlockSpec((N_CHIPS, d, half), lambda i, s: (0, ROWS_ATTN_IN[layer] // d, i)),
                   pl.BlockSpec((d, 128), lambda i, s: (0, 0))],
        out_shape=[jax.ShapeDtypeStruct((N_CHIPS, SLOT_ROWS, LANES), F32), jax.ShapeDtypeStruct((d, 128), F32)],
        input_output_aliases={0: 0},
        compiler_params=pltpu.CompilerParams(dimension_semantics=("arbitrary", "arbitrary")),
    )(slots, x, *pieces, forget)


def _dw_out(slots, yg, dz, first_row):
    s_len, d = dz.shape
    ts = min(s_len, DW_ROWS)
    rows = d // N_CHIPS

    def body(*refs):
        a_ref, b_ref, o_ref = refs[-3:]

        @pl.when(pl.program_id(0) == 0)
        def _():
            o_ref[...] = jnp.zeros_like(o_ref)

        prod = lax.dot_general(a_ref[...], b_ref[...].astype(BF16), TN_DIMS, preferred_element_type=F32)
        o_ref[...] += prod.reshape(N_CHIPS, rows, d)

    specs = [pl.BlockSpec((ts, d), lambda s: (s, 0))] * 2
    return _dw_call(body, "dw_out", slots, [yg, dz], specs, (N_CHIPS, rows, d), lambda s: (0, first_row // rows, 0),
                    (s_len // ts,), ("arbitrary",))


def _dw_rnn_in(slots, x, parts, layer):
    s_len, d = x.shape
    ts = min(s_len, DW_ROWS)
    half = d // 2

    def body(*refs):
        x_ref, p_refs, o_ref = refs[-4], refs[-3:-1], refs[-1]

        @pl.when(pl.program_id(0) == 0)
        def _():
            o_ref[...] = jnp.zeros_like(o_ref)

        xb = x_ref[...].astype(BF16)
        for p, p_ref in enumerate(p_refs):
            for jj in (0, 1):
                for r in (0, 1):
                    o_ref[2 * p + jj, :, r * half:(r + 1) * half] += lax.dot_general(
                        xb[:, r * half:(r + 1) * half], p_ref[:, jj * half:(jj + 1) * half], TN_DIMS,
                        preferred_element_type=F32)

    specs = [pl.BlockSpec((ts, d), lambda s: (s, 0))] * 3
    return _dw_call(body, "dw_rnn_in", slots, [x] + list(parts), specs, (N_CHIPS, half, d),
                    lambda s: (0, ROWS_RNN_IN[layer] // half, 0), (s_len // ts,), ("arbitrary",))


def _fcum(fl, bias):
    s_len = fl.shape[0]

    def body(fl_ref, b_ref, cum_ref, sg_ref):
        z = fl_ref[...] + b_ref[...]
        e = jnp.exp(-jnp.abs(z))
        logf = jnp.minimum(z, 0.0) - jnp.log(1.0 + e)
        sneg = jnp.where(z >= 0, e, 1.0) / (1.0 + e)
        run = logf.T[0:16, :]
        sg_ref[...] = sneg.T[0:16, :]
        lane = lax.broadcasted_iota(jnp.int32, (16, s_len), 1)
        sh = 1
        while sh < s_len:
            run = run + jnp.where(lane >= sh, pltpu.roll(run, sh, 1), 0.0)
            sh *= 2
        cum_ref[...] = run

    return pl.pallas_call(
        body, name="fcum",
        out_shape=[jax.ShapeDtypeStruct((16, s_len), F32), jax.ShapeDtypeStruct((16, s_len), F32)],
    )(fl, bias)


def _fbwd(dcum, sneg):
    s_len = dcum.shape[1]

    def body(dc_ref, sg_ref, dl_ref, db_ref):
        run = dc_ref[...]
        lane = lax.broadcasted_iota(jnp.int32, (16, s_len), 1)
        sh = 1
        while sh < s_len:
            run = run + jnp.where(lane < s_len - sh, pltpu.roll(run, s_len - sh, 1), 0.0)
            sh *= 2
        dlog = run * sg_ref[...]
        db_ref[...] = jnp.broadcast_to(jnp.sum(dlog, axis=1, keepdims=True), (16, 128))
        full = jnp.concatenate([dlog, jnp.zeros((112, s_len), F32)], axis=0)
        dl_ref[...] = full.T.astype(BF16)

    return pl.pallas_call(
        body, name="fbwd",
        out_shape=[jax.ShapeDtypeStruct((s_len, 128), BF16), jax.ShapeDtypeStruct((16, 128), F32)],
    )(dcum, sneg)


FWD_TILE = 1024
BWD_TILE = 1024


def _flash_fwd(q, k, v, cum, shards=()):
    s_len, width = q.shape
    tile = min(FWD_TILE, s_len // 2)
    nt = s_len // tile
    reps = tile // 128
    cumr = cum.reshape(-1, tile)
    ns = len(shards)
    pairs = width // HEAD_PAIR

    def body(*refs):
        q_ref, k_ref, v_ref, cum_ref = refs[:4]
        o_ref, lse_ref = refs[4 + ns:6 + ns]
        q_t, v_t, cum_col = refs[6 + 2 * ns:9 + 2 * ns]
        pid = pl.program_id(0)
        if ns:
            start, forward, finish = _gather_steps(refs[4:4 + ns], refs[6 + ns:6 + 2 * ns], *refs[9 + 2 * ns:])
            pl.when(pid == 0)(start)
            pl.when(pid == pairs - 3)(forward)
        top = lax.broadcasted_iota(jnp.int32, (HEAD_PAIR, tile), 0) < HEAD_DIM
        causal = (lax.broadcasted_iota(jnp.int32, (tile, tile), 0)
                  <= lax.broadcasted_iota(jnp.int32, (tile, tile), 1))

        def pre(i, carry):
            r0 = pl.multiple_of(i * tile, tile)
            q_t[i] = q_ref[pl.ds(r0, tile), :].astype(F32).T.astype(BF16)
            v_t[i] = v_ref[pl.ds(r0, tile), :].astype(F32).T.astype(BF16)
            for h in (0, 1):
                row = cum_ref[pl.ds((2 * pid + h) * nt + i, 1), :]
                cum_col[h, pl.ds(r0, tile), :] = jnp.broadcast_to(row, (HEAD_PAIR, tile)).T
            return carry

        lax.fori_loop(0, nt, pre, 0)

        def q_loop(qi, carry):
            qt = q_t[qi]
            zt = jnp.zeros_like(qt)
            qms = (jnp.where(top, qt, zt), jnp.where(top, zt, qt))

            def step(kj, state, masked):
                m0, l0, m1, l1, acc = state
                k0 = pl.multiple_of(kj * tile, tile)
                kt = k_ref[pl.ds(k0, tile), :]
                vt = v_t[kj]
                ms, ls, scales, contrib = [m0, m1], [l0, l1], [], None
                for h in (0, 1):
                    s = jnp.dot(kt, qms[h], preferred_element_type=F32)
                    s = s - jnp.tile(cum_col[h, pl.ds(k0, tile), :], (1, reps))
                    if masked:
                        s = jnp.where(causal, s, NEG)
                    m_new = jnp.maximum(ms[h], jnp.max(s, axis=0, keepdims=True))
                    p = jnp.exp(s - m_new)
                    scale = jnp.exp(ms[h] - m_new)
                    ls[h] = scale * ls[h] + jnp.sum(p, axis=0, keepdims=True)
                    ms[h] = m_new
                    scales.append(scale)
                    vm = jnp.where(top, vt, zt) if h == 0 else jnp.where(top, zt, vt)
                    part = jnp.dot(vm, p.astype(BF16), preferred_element_type=F32)
                    contrib = part if contrib is None else contrib + part
                acc = jnp.where(top, scales[0], scales[1]) * acc + contrib
                return ms[0], ls[0], ms[1], ls[1], acc

            low = jnp.full((1, tile), NEG, F32)
            zero = jnp.zeros((1, tile), F32)
            state = step(qi, (low, zero, low, zero, jnp.zeros((HEAD_PAIR, tile), F32)), True)
            m0, l0, m1, l1, acc = lax.fori_loop(0, qi, lambda kj, c: step(kj, c, False), state)
            q0 = pl.multiple_of(qi * tile, tile)
            o_ref[pl.ds(q0, tile), :] = (acc / jnp.where(top, l0, l1)).T
            lse_ref[pl.ds((2 * pid) * nt + qi, 1), :] = m0 + jnp.log(l0)
            lse_ref[pl.ds((2 * pid + 1) * nt + qi, 1), :] = m1 + jnp.log(l1)
            return carry

        lax.fori_loop(0, nt, q_loop, 0)
        if ns:
            pl.when(pid == pairs - 1)(finish)

    blk = pl.BlockSpec((s_len, HEAD_PAIR), lambda p: (0, p))
    full = pl.BlockSpec(cumr.shape, lambda p: (0, 0))
    sems = [pltpu.SemaphoreType.DMA((GATHER_SEMS * ns,))] * 2 if ns else []
    o, lse, *gathered = pl.pallas_call(
        body, name="flash_fwd_gather" if ns else "flash_fwd", grid=(pairs,),
        in_specs=[blk, blk, blk, full] + [ANY] * ns,
        out_specs=[blk, full] + [ANY] * ns,
        out_shape=[jax.ShapeDtypeStruct((s_len, width), F32), jax.ShapeDtypeStruct(cumr.shape, F32)]
        + [jax.ShapeDtypeStruct((N_CHIPS,) + s.shape, s.dtype) for s in shards],
        scratch_shapes=[pltpu.VMEM((nt, HEAD_PAIR, tile), BF16), pltpu.VMEM((nt, HEAD_PAIR, tile), BF16),
                        pltpu.VMEM((2, s_len, HEAD_PAIR), F32)] + sems,
        compiler_params=pltpu.CompilerParams(dimension_semantics=("arbitrary",)),
    )(q, k, v, cumr, *shards)
    return (o, lse.reshape(cum.shape), *gathered)


def _flash_bwd(q, k, v, o, do, lse, cum, outgoing=()):
    s_len, width = q.shape
    tile = min(BWD_TILE, s_len // 2)
    nt = s_len // tile
    reps = tile // 128
    cumr = cum.reshape(-1, tile)
    lse = lse.reshape(-1, tile)
    ns = len(outgoing)
    pairs = width // HEAD_PAIR

    def body(*refs):
        q_ref, k_ref, v_ref, o_ref, do_ref, lse_ref, cum_ref = refs[:7]
        dq_ref, dk_ref, dv_ref, dcum_ref = refs[7 + ns:11 + ns]
        (q_t, do_t, k_t, do_bf, cum_col, dq_t_acc, delta, q_side, dk_acc, dv_acc,
         k_side) = refs[11 + 2 * ns:22 + 2 * ns]
        pid = pl.program_id(0)
        if ns:
            start, finish = _scatter_steps(refs[7:7 + ns], refs[11 + ns:11 + 2 * ns], *refs[22 + 2 * ns:])
            pl.when(pid == 0)(start)
        top = lax.broadcasted_iota(jnp.int32, (HEAD_PAIR, tile), 0) < HEAD_DIM
        left = lax.broadcasted_iota(jnp.int32, (tile, HEAD_PAIR), 1) < HEAD_DIM
        causal = (lax.broadcasted_iota(jnp.int32, (tile, tile), 0)
                  <= lax.broadcasted_iota(jnp.int32, (tile, tile), 1))

        def pre(i, carry):
            r0 = pl.multiple_of(i * tile, tile)
            d_o = do_ref[pl.ds(r0, tile), :]
            prod_t = (d_o * o_ref[pl.ds(r0, tile), :]).T
            delta[pl.ds(i, 1), :] = jnp.sum(prod_t[:HEAD_DIM], axis=0, keepdims=True)
            delta[pl.ds(nt + i, 1), :] = jnp.sum(prod_t[HEAD_DIM:], axis=0, keepdims=True)
            do_bf[pl.ds(r0, tile), :] = d_o.astype(BF16)
            do_t[i] = d_o.T.astype(BF16)
            q_t[i] = q_ref[pl.ds(r0, tile), :].astype(F32).T.astype(BF16)
            k_t[i] = k_ref[pl.ds(r0, tile), :].astype(F32).T.astype(BF16)
            dq_t_acc[i] = jnp.zeros((HEAD_PAIR, tile), F32)
            for h in (0, 1):
                row = cum_ref[pl.ds((2 * pid + h) * nt + i, 1), :]
                cum_col[h, pl.ds(r0, tile), :] = jnp.broadcast_to(row, (HEAD_PAIR, tile)).T
                q_side[pl.ds(h * nt + i, 1), :] = jnp.zeros((1, tile), F32)
            return carry

        lax.fori_loop(0, nt, pre, 0)

        def kv_loop(kj, carry):
            k0 = pl.multiple_of(kj * tile, tile)
            kt = k_ref[pl.ds(k0, tile), :]
            vt = v_ref[pl.ds(k0, tile), :]
            ktt = k_t[kj]
            zt = jnp.zeros_like(ktt)
            zr = jnp.zeros_like(kt)
            kms = (jnp.where(top, ktt, zt), jnp.where(top, zt, ktt))
            cols = [jnp.tile(cum_col[h, pl.ds(k0, tile), :], (1, reps)) for h in (0, 1)]
            dk_acc[...] = jnp.zeros_like(dk_acc)
            dv_acc[...] = jnp.zeros_like(dv_acc)
            k_side[...] = jnp.zeros_like(k_side)

            def step(qi, carry, masked):
                q0 = pl.multiple_of(qi * tile, tile)
                qtt = q_t[qi]
                dtt = do_t[qi]
                q_rows = q_ref[pl.ds(q0, tile), :]
                do_rows = do_bf[pl.ds(q0, tile), :]
                dq_part = None
                for h in (0, 1):
                    q_m = jnp.where(top, qtt, zt) if h == 0 else jnp.where(top, zt, qtt)
                    do_m = jnp.where(top, dtt, zt) if h == 0 else jnp.where(top, zt, dtt)
                    s = jnp.dot(kt, q_m, preferred_element_type=F32) - cols[h]
                    if masked:
                        s = jnp.where(causal, s, NEG)
                    p = jnp.exp(s - lse_ref[pl.ds((2 * pid + h) * nt + qi, 1), :])
                    dp = jnp.dot(vt, do_m, preferred_element_type=F32)
                    ds = p * (dp - delta[pl.ds(h * nt + qi, 1), :])
                    q_side[pl.ds(h * nt + qi, 1), :] += jnp.sum(ds, axis=0, keepdims=True)
                    folded = ds[:, :128]
                    for b in range(1, reps):
                        folded = folded + ds[:, b * 128:(b + 1) * 128]
                    k_side[h] += folded
                    pb = p.astype(BF16)
                    dsb = ds.astype(BF16)
                    do_h = jnp.where(left, do_rows, zr) if h == 0 else jnp.where(left, zr, do_rows)
                    q_h = jnp.where(left, q_rows, zr) if h == 0 else jnp.where(left, zr, q_rows)
                    dv_acc[...] += jnp.dot(pb, do_h, preferred_element_type=F32)
                    dk_acc[...] += jnp.dot(dsb, q_h, preferred_element_type=F32)
                    part = jnp.dot(kms[h], dsb, preferred_element_type=F32)
                    dq_part = part if dq_part is None else dq_part + part
                dq_t_acc[qi] += dq_part
                return carry

            step(kj, 0, True)
            lax.fori_loop(kj + 1, nt, lambda qi, c: step(qi, c, False), 0)
            dk_ref[pl.ds(k0, tile), :] = dk_acc[...].astype(BF16)
            dv_ref[pl.ds(k0, tile), :] = dv_acc[...].astype(BF16)
            for h in (0, 1):
                dcum_ref[pl.ds((2 * pid + h) * nt + kj, 1), :] = -jnp.sum(k_side[h].T, axis=0, keepdims=True)
            return carry

        lax.fori_loop(0, nt, kv_loop, 0)

        def fin(i, carry):
            r0 = pl.multiple_of(i * tile, tile)
            dq_ref[pl.ds(r0, tile), :] = dq_t_acc[i].T.astype(BF16)
            for h in (0, 1):
                dcum_ref[pl.ds((2 * pid + h) * nt + i, 1), :] += q_side[pl.ds(h * nt + i, 1), :]
            return carry

        lax.fori_loop(0, nt, fin, 0)
        if ns:
            pl.when(pid == pairs - 1)(finish)

    blk = pl.BlockSpec((s_len, HEAD_PAIR), lambda p: (0, p))
    full = pl.BlockSpec(cumr.shape, lambda p: (0, 0))
    transposed = pltpu.VMEM((nt, HEAD_PAIR, tile), BF16)
    sems = [pltpu.SemaphoreType.DMA((3 * ns,))] * 2 if ns else []
    dq, dk, dv, dcum, *arrived = pl.pallas_call(
        body, name="flash_bwd_scatter" if ns else "flash_bwd", grid=(pairs,),
        in_specs=[blk, blk, blk, blk, blk, full, full] + [ANY] * ns,
        out_specs=[blk, blk, blk, full] + [ANY] * ns,
        out_shape=[jax.ShapeDtypeStruct((s_len, width), BF16)] * 3 + [jax.ShapeDtypeStruct(cumr.shape, F32)]
        + [jax.ShapeDtypeStruct(t.shape, t.dtype) for t in outgoing],
        scratch_shapes=[transposed, transposed, transposed, pltpu.VMEM((s_len, HEAD_PAIR), BF16),
                        pltpu.VMEM((2, s_len, HEAD_PAIR), F32), pltpu.VMEM((nt, HEAD_PAIR, tile), F32),
                        pltpu.VMEM((2 * nt, tile), F32), pltpu.VMEM((2 * nt, tile), F32),
                        pltpu.VMEM((tile, HEAD_PAIR), F32), pltpu.VMEM((tile, HEAD_PAIR), F32),
                        pltpu.VMEM((2, tile, HEAD_PAIR), F32)] + sems,
        compiler_params=pltpu.CompilerParams(dimension_semantics=("arbitrary",)),
    )(q, k, v, o, do, lse, cumr, *outgoing)
    return (dq, dk, dv, dcum.reshape(cum.shape), *arrived)


def _out_ln(a, gate, x, w, g, b, name, target=None):
    s_len, d = x.shape
    tm = min(512, s_len)
    nt = 0 if target is None else 1

    def body(*refs):
        a_ref, gate_ref, x_ref, w_ref, g_ref, b_ref = refs[:6]
        first_ref, xh_ref, rs_ref, yg_ref = refs[6 + nt:10 + nt]
        gt = gate_ref[...]
        yg = (a_ref[...] * (gt * _sigmoid(gt))).astype(BF16)
        yg_ref[...] = yg
        z = ALPHA * x_ref[...] + jnp.dot(yg, w_ref[...], preferred_element_type=F32)
        zc = z - jnp.mean(z, axis=1, keepdims=True)
        rstd = lax.rsqrt(jnp.mean(zc * zc, axis=1, keepdims=True) + LN_EPS)
        xh = zc * rstd
        xh_ref[...] = xh
        y = xh * g_ref[...] + b_ref[...]
        rs_ref[...] = jnp.broadcast_to(rstd, (tm, 128))
        if nt:
            sq_ref = refs[10 + nt]

            @pl.when(pl.program_id(0) == 0)
            def _():
                sq_ref[...] = jnp.zeros_like(sq_ref)

            diff = y - refs[6][...]
            first_ref[...] = diff * (1.0 / d)
            sq_ref[...] += jnp.sum(diff * diff, axis=0, keepdims=True)
        else:
            first_ref[...] = y

    row = pl.BlockSpec((tm, d), lambda i: (i, 0))
    vec = pl.BlockSpec((1, d), lambda i: (0, 0))
    return pl.pallas_call(
        body, name=name + "_loss" if nt else name, grid=(s_len // tm,),
        in_specs=[row, row, row, pl.BlockSpec(w.shape, lambda i: (0, 0)), vec, vec] + [row] * nt,
        out_specs=[row, row, pl.BlockSpec((tm, 128), lambda i: (i, 0)), row] + [vec] * nt,
        out_shape=[jax.ShapeDtypeStruct((s_len, d), F32), jax.ShapeDtypeStruct((s_len, d), F32),
                   jax.ShapeDtypeStruct((s_len, 128), F32), jax.ShapeDtypeStruct((s_len, d), BF16)]
        + [jax.ShapeDtypeStruct((1, d), F32)] * nt,
        compiler_params=pltpu.CompilerParams(dimension_semantics=("arbitrary" if nt else "parallel",)),
    )(a, gate, x, w, g, b, *([target] if nt else []))


def _ln_bwd(dout, xh, rs, g, w, gate, a, name, swap=None):
    s_len, d = dout.shape
    tm = min(512, s_len)
    steps = s_len // tm
    ns = 0 if swap is None else 1

    def body(*refs):
        do_ref, xh_ref, rs_ref, g_ref, w_ref, gate_ref, a_ref = refs[:7]
        dz_ref, da_ref, dgate_ref, dg_ref, db_ref = refs[7 + ns:12 + ns]
        if ns:
            start, finish = _swap_steps(refs[7:8], refs[12 + ns:13 + ns], [swap[1]], *refs[13 + ns:])
            pl.when(pl.program_id(0) == 0)(start)

        @pl.when(pl.program_id(0) == 0)
        def _():
            dg_ref[...] = jnp.zeros_like(dg_ref)
            db_ref[...] = jnp.zeros_like(db_ref)

        dout_t = do_ref[...]
        xh_t = xh_ref[...]
        dg_ref[...] += jnp.sum(dout_t * xh_t, axis=0, keepdims=True)
        db_ref[...] += jnp.sum(dout_t, axis=0, keepdims=True)
        dxh = dout_t * g_ref[...]
        m1 = jnp.mean(dxh, axis=1, keepdims=True)
        m2 = jnp.mean(dxh * xh_t, axis=1, keepdims=True)
        dz = rs_ref[:, 0:1] * (dxh - m1 - xh_t * m2)
        dz_ref[...] = dz
        dyg = lax.dot_general(dz.astype(BF16), w_ref[...], NT_DIMS, preferred_element_type=F32)
        gt = gate_ref[...]
        sg = _sigmoid(gt)
        da_ref[...] = dyg * (gt * sg)
        dgate_ref[...] = (dyg * a_ref[...] * (sg * (1.0 + gt * (1.0 - sg)))).astype(BF16)
        if ns:
            pl.when(pl.program_id(0) == steps - 1)(finish)

    row = pl.BlockSpec((tm, d), lambda i: (i, 0))
    vec = pl.BlockSpec((1, d), lambda i: (0, 0))
    extra_out = [jax.ShapeDtypeStruct((N_CHIPS, swap[1][1], LANES), swap[0].dtype)] if ns else []
    return pl.pallas_call(
        body, name=name + "_swap" if ns else name, grid=(steps,),
        in_specs=[row, row, pl.BlockSpec((tm, 128), lambda i: (i, 0)), vec, pl.BlockSpec(w.shape, lambda i: (0, 0)),
                  row, row] + [ANY] * ns,
        out_specs=[row, row, row, vec, vec] + [ANY] * ns,
        out_shape=[jax.ShapeDtypeStruct((s_len, d), F32), jax.ShapeDtypeStruct((s_len, d), F32),
                   jax.ShapeDtypeStruct((s_len, d), BF16), jax.ShapeDtypeStruct((1, d), F32),
                   jax.ShapeDtypeStruct((1, d), F32)] + extra_out,
        scratch_shapes=[pltpu.SemaphoreType.DMA((N_CHIPS,))] * 2 if ns else [],
        compiler_params=pltpu.CompilerParams(dimension_semantics=("arbitrary",)),
    )(dout, xh, rs, g, w, gate, a, *([swap[0]] if ns else []))


def _rnn_fwd(u, conv_w, conv_b, wa, ba, wi, bi, lam):
    s_len, width = u.shape
    tm = min(512, s_len // 2)
    nb = width // RNN_BLOCK

    def body(u_ref, up_ref, cw_ref, cb_ref, wa_ref, ba_ref, wi_ref, bi_ref, lam_ref,
             h_ref, uc_ref, r_ref, i_ref, a_ref, b_scr, h_carry):
        step_id = pl.program_id(0)

        @pl.when(step_id == 0)
        def _():
            h_carry[...] = jnp.zeros_like(h_carry)

        for n in range(nb):
            blk = slice(n * RNN_BLOCK, (n + 1) * RNN_BLOCK)
            ut = u_ref[:, blk]
            prev = jnp.where(step_id > 0, up_ref[:, blk], 0.0)
            uc = cb_ref[:, blk] + cw_ref[3:4, blk] * ut
            for k in (1, 2, 3):
                uc = uc + cw_ref[3 - k:4 - k, blk] * _shift_down(ut, prev, k)
            ucb = uc.astype(BF16)
            r = _sigmoid(jnp.dot(ucb, wa_ref[n], preferred_element_type=F32) + ba_ref[:, blk])
            ig = _sigmoid(jnp.dot(ucb, wi_ref[n], preferred_element_type=F32) + bi_ref[:, blk])
            log_a = -LRU_C * r * _softplus(-lam_ref[:, blk])
            uc_ref[:, blk] = uc
            r_ref[:, blk] = r
            i_ref[:, blk] = ig
            a_ref[:, blk] = jnp.exp(log_a)
            b_scr[:, blk] = jnp.sqrt(_neg_expm1(2.0 * log_a)) * (ig * uc)

        def scan(t, h):
            h = a_ref[pl.ds(t, 1), :] * h + b_scr[pl.ds(t, 1), :]
            h_ref[pl.ds(t, 1), :] = h
            return h

        h_carry[...] = lax.fori_loop(0, tm, scan, h_carry[...], unroll=8)

    row = pl.BlockSpec((tm, width), lambda i: (i, 0))
    prev8 = pl.BlockSpec((8, width), lambda i: (jnp.maximum(i * (tm // 8) - 1, 0), 0))
    vec = pl.BlockSpec((1, width), lambda i: (0, 0))
    mat = pl.BlockSpec(wa.shape, lambda i: (0, 0, 0))
    return pl.pallas_call(
        body, name="rnn_fwd", grid=(s_len // tm,),
        in_specs=[row, prev8, pl.BlockSpec(conv_w.shape, lambda i: (0, 0)), vec, mat, vec, mat, vec, vec],
        out_specs=[row] * 5,
        out_shape=[jax.ShapeDtypeStruct((s_len, width), F32)] * 5,
        scratch_shapes=[pltpu.VMEM((tm, width), F32), pltpu.VMEM((1, width), F32)],
        compiler_params=pltpu.CompilerParams(dimension_semantics=("arbitrary",)),
    )(u, u, conv_w, conv_b, wa, ba, wi, bi, lam)


def _rnn_bwd(dh, a, h, r, ig, uc, lam, wa, wi):
    s_len, width = dh.shape
    tm = min(512, s_len // 2)
    nt = s_len // tm
    nb = width // RNN_BLOCK

    def body(dh_ref, a_ref, h_ref, hp_ref, r_ref, i_ref, uc_ref, lam_ref, wa_ref, wi_ref,
             duc_ref, dwa_ref, dwi_ref, dba_ref, dbi_ref, dlam_ref, g_scr, c_scr, dsp_scr):
        step_id = pl.program_id(0)
        tile_id = nt - 1 - step_id

        @pl.when(step_id == 0)
        def _():
            c_scr[...] = jnp.zeros_like(c_scr)
            dsp_scr[...] = jnp.zeros_like(dsp_scr)
            dwa_ref[...] = jnp.zeros_like(dwa_ref)
            dwi_ref[...] = jnp.zeros_like(dwi_ref)
            dba_ref[...] = jnp.zeros_like(dba_ref)
            dbi_ref[...] = jnp.zeros_like(dbi_ref)

        def scan(j, c):
            t = tm - 1 - j
            g = dh_ref[pl.ds(t, 1), :] + c
            g_scr[pl.ds(t, 1), :] = g
            return a_ref[pl.ds(t, 1), :] * g

        c_scr[...] = lax.fori_loop(0, tm, scan, c_scr[...], unroll=8)

        for n in range(nb):
            blk = slice(n * RNN_BLOCK, (n + 1) * RNN_BLOCK)
            g_t = g_scr[:, blk]
            hp8 = jnp.where(tile_id > 0, hp_ref[:, blk], 0.0)
            h_prev = _shift_down(h_ref[:, blk], hp8, 1)
            av, rv, iv, ucv = a_ref[:, blk], r_ref[:, blk], i_ref[:, blk], uc_ref[:, blk]
            sp = _softplus(-lam_ref[:, blk])
            mag = jnp.sqrt(_neg_expm1(-2.0 * LRU_C * rv * sp))
            d_iu = g_t * mag
            dla = g_t * h_prev * av - g_t * (iv * ucv) * (av * av) / mag
            dsp_scr[:, blk] += jnp.sum(dla * (-LRU_C * rv), axis=0, keepdims=True)
            dra = dla * (-LRU_C * sp) * rv * (1.0 - rv)
            dia = d_iu * ucv * iv * (1.0 - iv)
            drab, diab, ucb = dra.astype(BF16), dia.astype(BF16), ucv.astype(BF16)
            duc_ref[:, blk] = (d_iu * iv
                               + lax.dot_general(drab, wa_ref[n], NT_DIMS, preferred_element_type=F32)
                               + lax.dot_general(diab, wi_ref[n], NT_DIMS, preferred_element_type=F32))
            dwa_ref[n] += lax.dot_general(ucb, drab, TN_DIMS, preferred_element_type=F32)
            dwi_ref[n] += lax.dot_general(ucb, diab, TN_DIMS, preferred_element_type=F32)
            dba_ref[:, blk] += jnp.sum(dra, axis=0, keepdims=True)
            dbi_ref[:, blk] += jnp.sum(dia, axis=0, keepdims=True)

        @pl.when(step_id == nt - 1)
        def _():
            dlam_ref[...] = -dsp_scr[...] * _sigmoid(-lam_ref[...])

    row = pl.BlockSpec((tm, width), lambda i: (nt - 1 - i, 0))
    prev8 = pl.BlockSpec((8, width), lambda i: (jnp.maximum((nt - 1 - i) * (tm // 8) - 1, 0), 0))
    vec = pl.BlockSpec((1, width), lambda i: (0, 0))
    mat = pl.BlockSpec(wa.shape, lambda i: (0, 0, 0))
    return pl.pallas_call(
        body, name="rnn_bwd", grid=(nt,),
        in_specs=[row, row, row, prev8, row, row, row, vec, mat, mat],
        out_specs=[row, mat, mat, vec, vec, vec],
        out_shape=[jax.ShapeDtypeStruct((s_len, width), F32), jax.ShapeDtypeStruct(wa.shape, F32),
                   jax.ShapeDtypeStruct(wa.shape, F32)] + [jax.ShapeDtypeStruct((1, width), F32)] * 3,
        scratch_shapes=[pltpu.VMEM((tm, width), F32), pltpu.VMEM((1, width), F32), pltpu.VMEM((1, width), F32)],
        compiler_params=pltpu.CompilerParams(dimension_semantics=("arbitrary",)),
    )(dh, a, h, h, r, ig, uc, lam, wa, wi)


def _conv_bwd(duc, u, conv_w):
    s_len, width = duc.shape
    tm = min(256, s_len)
    nt = s_len // tm

    def body(d_ref, dn_ref, u_ref, up_ref, cw_ref, du_ref, dcw_ref, dcb_ref):
        step_id = pl.program_id(0)

        @pl.when(step_id == 0)
        def _():
            dcw_ref[...] = jnp.zeros_like(dcw_ref)
            dcb_ref[...] = jnp.zeros_like(dcb_ref)

        slab = 128
        for n in range(width // slab):
            blk = slice(n * slab, (n + 1) * slab)
            dt = d_ref[:, blk]
            ut = u_ref[:, blk]
            nxt = jnp.where(step_id < nt - 1, dn_ref[:, blk], 0.0)
            prev = jnp.where(step_id > 0, up_ref[:, blk], 0.0)
            du = cw_ref[3:4, blk] * dt
            dcw_ref[3:4, blk] += jnp.sum(dt * ut, axis=0, keepdims=True)
            for k in (1, 2, 3):
                du = du + cw_ref[3 - k:4 - k, blk] * _shift_up(dt, nxt, k)
                dcw_ref[3 - k:4 - k, blk] += jnp.sum(dt * _shift_down(ut, prev, k), axis=0, keepdims=True)
            du_ref[:, blk] = du.astype(BF16)
            dcb_ref[:, blk] += jnp.sum(dt, axis=0, keepdims=True)

    row = pl.BlockSpec((tm, width), lambda i: (i, 0))
    prev8 = pl.BlockSpec((8, width), lambda i: (jnp.maximum(i * (tm // 8) - 1, 0), 0))
    next8 = pl.BlockSpec((8, width), lambda i: (jnp.minimum((i + 1) * (tm // 8), s_len // 8 - 1), 0))
    return pl.pallas_call(
        body, name="conv_bwd", grid=(nt,),
        in_specs=[row, next8, row, prev8, pl.BlockSpec(conv_w.shape, lambda i: (0, 0))],
        out_specs=[row, pl.BlockSpec(conv_w.shape, lambda i: (0, 0)), pl.BlockSpec((1, width), lambda i: (0, 0))],
        out_shape=[jax.ShapeDtypeStruct((s_len, width), BF16), jax.ShapeDtypeStruct(conv_w.shape, F32),
                   jax.ShapeDtypeStruct((1, width), F32)],
        compiler_params=pltpu.CompilerParams(dimension_semantics=("arbitrary",)),
    )(duc, duc, u, u, conv_w)


def _pair_sum(core, grads, theirs, first_row, out_dtype):
    n, half, _ = theirs.shape
    tb = 128 if half % 128 == 0 and first_row % 128 == 0 else half
    assert first_row % tb == 0 and half % tb == 0

    def body(c_ref, a_ref, b_ref, o_ref):
        o_ref[...] = (a_ref[...] + b_ref[...]).astype(out_dtype)

    blk = pl.BlockSpec((n, tb, LANES), lambda i, c: (0, i, 0))
    mine = pl.BlockSpec((n, tb, LANES), lambda i, c: (0, first_row // tb + c[0] * (half // tb) + i, 0))
    return pl.pallas_call(
        body, name="pair_sum",
        grid_spec=pltpu.PrefetchScalarGridSpec(
            num_scalar_prefetch=1, grid=(half // tb,), in_specs=[mine, blk], out_specs=blk),
        out_shape=jax.ShapeDtypeStruct((n, half, LANES), out_dtype),
        compiler_params=pltpu.CompilerParams(dimension_semantics=("parallel",)),
    )(core, grads, theirs)


def _sum_chips(parts):
    rows = parts.shape[1]
    tb = 128 if rows % 128 == 0 else rows

    def body(p_ref, o_ref):
        o_ref[...] = ((p_ref[0].astype(F32) + p_ref[1].astype(F32)) + p_ref[2].astype(F32)) + p_ref[3].astype(F32)

    return pl.pallas_call(
        body, name="chip_sum", grid=(rows // tb,),
        in_specs=[pl.BlockSpec((N_CHIPS, tb, LANES), lambda i: (0, i, 0))],
        out_specs=pl.BlockSpec((tb, LANES), lambda i: (i, 0)),
        out_shape=jax.ShapeDtypeStruct((rows, LANES), F32),
        compiler_params=pltpu.CompilerParams(dimension_semantics=("parallel",)),
    )(parts)


def _adamw(w, g, m, v, lead_per_block, rows_per_block):
    n, rows, cols = w.shape
    blk = pl.BlockSpec((lead_per_block, rows_per_block, cols), lambda i, j: (i, j, 0))

    def body(w_ref, g_ref, m_ref, v_ref, d_ref, nm_ref, nv_ref):
        gt = g_ref[...]
        nm = ADAM_B1 * m_ref[...] + (1.0 - ADAM_B1) * gt
        nv = ADAM_B2 * v_ref[...] + (1.0 - ADAM_B2) * (gt * gt)
        m_hat = nm / (1.0 - ADAM_B1 ** ADAM_STEP)
        v_hat = nv / (1.0 - ADAM_B2 ** ADAM_STEP)
        d_ref[...] = -ADAM_LR * (m_hat / (jnp.sqrt(v_hat) + ADAM_EPS) + ADAM_WD * w_ref[...])
        nm_ref[...] = nm
        nv_ref[...] = nv

    return pl.pallas_call(
        body, name="adamw", grid=(n // lead_per_block, rows // rows_per_block), in_specs=[blk] * 4,
        out_specs=[blk] * 3,
        out_shape=[jax.ShapeDtypeStruct(w.shape, F32)] * 3,
        compiler_params=pltpu.CompilerParams(dimension_semantics=("parallel", "parallel")),
    )(w, g, m, v)


def _place():
    x, y, c = lax.axis_index("x"), lax.axis_index("y"), lax.axis_index("c")
    return x, y, c, [(1 - x, y), (x, 1 - y), (1 - x, 1 - y)]


ANY = pl.BlockSpec(memory_space=pl.ANY)
COPY_PARTS = 4


def _remote_parts(src_of, dst_of, rows, send_sem, recv_sem, to, begin=True):
    parts = COPY_PARTS if rows % (16 * COPY_PARTS) == 0 else 1
    step = rows // parts
    for i in range(parts if begin is not False else 0):
        pltpu.make_async_remote_copy(src_ref=src_of(i * step, step), dst_ref=dst_of(i * step, step),
                                     send_sem=send_sem, recv_sem=recv_sem, device_id=to, device_id_type=MESH).start()
    if begin == "only":
        return None
    return pltpu.make_async_remote_copy(src_ref=src_of(0, rows), dst_ref=dst_of(0, rows), send_sem=send_sem,
                                        recv_sem=recv_sem, device_id=to, device_id_type=MESH)


GATHER_SEMS = 7


def _gather_steps(p_refs, o_refs, send_sems, recv_sems):
    x, y, c, chips = _place()
    me = 2 * x + y

    def half(ref, which):
        rows = ref.shape[0] // 2
        return ref.at[pl.ds(which * rows, rows)]

    def copy(k, src, dst, to):
        return pltpu.make_async_remote_copy(src_ref=src, dst_ref=dst, send_sem=send_sems.at[k],
                                            recv_sem=recv_sems.at[k], device_id=to, device_id_type=MESH)

    def first_copies():
        out = []
        for b, (p_ref, o_ref) in enumerate(zip(p_refs, o_refs)):
            for j, (cx, cy) in enumerate(chips):
                out.append(copy(GATHER_SEMS * b + j, half(p_ref, c), half(o_ref.at[me], c), (cx, cy, c)))
            out.append(copy(GATHER_SEMS * b + 6, p_ref, o_ref.at[me], (x, y, 1 - c)))
        return out

    def passed_copies():
        out = []
        for b, o_ref in enumerate(o_refs):
            for j, (cx, cy) in enumerate(chips):
                landed = half(o_ref.at[2 * cx + cy], c)
                out.append(copy(GATHER_SEMS * b + 3 + j, landed, landed, (x, y, 1 - c)))
        return out

    def start():
        for cp in first_copies():
            cp.start()

    def forward():
        for b, o_ref in enumerate(o_refs):
            for j, (cx, cy) in enumerate(chips):
                landed = half(o_ref.at[2 * cx + cy], c)
                copy(GATHER_SEMS * b + j, landed, landed, (x, y, c)).wait_recv()
        for cp in passed_copies():
            cp.start()

    def finish():
        for b, o_ref in enumerate(o_refs):
            for j, (cx, cy) in enumerate(chips):
                other = half(o_ref.at[2 * cx + cy], 1 - c)
                copy(GATHER_SEMS * b + 3 + j, other, other, (x, y, c)).wait_recv()
            copy(GATHER_SEMS * b + 6, o_ref.at[me], o_ref.at[me], (x, y, c)).wait_recv()
        for cp in first_copies() + passed_copies():
            cp.wait_send()

    return start, forward, finish


def _gather_chips(shards):
    nb = len(shards)

    def body(*refs):
        for step in _gather_steps(refs[:nb], refs[nb:2 * nb], *refs[2 * nb:]):
            step()

    return pl.pallas_call(
        body, name="gather_chips", in_specs=[ANY] * nb, out_specs=[ANY] * nb,
        out_shape=[jax.ShapeDtypeStruct((N_CHIPS,) + s.shape, s.dtype) for s in shards],
        scratch_shapes=[pltpu.SemaphoreType.DMA((GATHER_SEMS * nb,)), pltpu.SemaphoreType.DMA((GATHER_SEMS * nb,))],
    )(*shards)


def _swap_steps(g_refs, t_refs, spans, send_sems, recv_sems):
    x, y, c, _ = _place()

    def gives(begin):
        return [_remote_parts(
            lambda r0, m, g_ref=g_ref, j=j, base=first_row + (1 - c) * half: g_ref.at[j, pl.ds(base + r0, m), :],
            lambda r0, m, t_ref=t_ref, j=j: t_ref.at[j, pl.ds(r0, m), :],
            half, send_sems.at[b * N_CHIPS + j], recv_sems.at[b * N_CHIPS + j], (x, y, 1 - c), begin)
            for b, (g_ref, t_ref, (first_row, half)) in enumerate(zip(g_refs, t_refs, spans)) for j in range(N_CHIPS)]

    def start():
        gives("only")

    def finish():
        for give in gives(False):
            give.wait()

    return start, finish


def _swap_halves(bufs, spans):
    nb = len(bufs)

    def body(*refs):
        for step in _swap_steps(refs[:nb], refs[nb:2 * nb], spans, *refs[2 * nb:]):
            step()

    return pl.pallas_call(
        body, name="swap_halves", in_specs=[ANY] * nb, out_specs=[ANY] * nb,
        out_shape=[jax.ShapeDtypeStruct((b.shape[0], half, b.shape[2]), b.dtype) for b, (_, half) in zip(bufs, spans)],
        scratch_shapes=[pltpu.SemaphoreType.DMA((nb * N_CHIPS,)), pltpu.SemaphoreType.DMA((nb * N_CHIPS,))],
    )(*bufs)


def _scatter_steps(t_refs, o_refs, send_sems, recv_sems):
    x, y, c, chips = _place()
    me = 2 * x + y

    def slot(ref, chip):
        return lambda r0, n: ref.at[chip, pl.ds(r0, n), :]

    def sends(begin):
        return [_remote_parts(slot(t_ref, 2 * cx + cy), slot(o_ref, me), t_ref.shape[1], send_sems.at[3 * b + j],
                              recv_sems.at[3 * b + j], (cx, cy, c), begin)
                for b, (t_ref, o_ref) in enumerate(zip(t_refs, o_refs)) for j, (cx, cy) in enumerate(chips)]

    def start():
        sends("only")

    def finish():
        for b, o_ref in enumerate(o_refs):
            for j, (cx, cy) in enumerate(chips):
                landed = o_ref.at[2 * cx + cy]
                pltpu.make_async_remote_copy(src_ref=landed, dst_ref=landed, send_sem=send_sems.at[3 * b + j],
                                             recv_sem=recv_sems.at[3 * b + j], device_id=(x, y, c),
                                             device_id_type=MESH).wait_recv()
        for cp in sends(False):
            cp.wait_send()

    return start, finish


def _give_sibling(bufs):
    nb = len(bufs)

    def body(*refs):
        h_refs, o_refs, (send_sems, recv_sems) = refs[:nb], refs[nb:2 * nb], refs[2 * nb:]
        x, y, c, _ = _place()
        gives = [_remote_parts(lambda r0, n, h_ref=h_ref: h_ref.at[pl.ds(r0, n), :],
                               lambda r0, n, o_ref=o_ref: o_ref.at[pl.ds(r0, n), :],
                               h_ref.shape[0], send_sems.at[b], recv_sems.at[b], (x, y, 1 - c))
                 for b, (h_ref, o_ref) in enumerate(zip(h_refs, o_refs))]
        for give in gives:
            give.wait()

    return pl.pallas_call(
        body, name="give_sibling", in_specs=[ANY] * nb, out_specs=[ANY] * nb,
        out_shape=[jax.ShapeDtypeStruct(b.shape, b.dtype) for b in bufs],
        scratch_shapes=[pltpu.SemaphoreType.DMA((nb,)), pltpu.SemaphoreType.DMA((nb,))],
    )(*bufs)


def _pack(arrays, dtype, row_align):
    flat = []
    for arr in arrays:
        v = arr.reshape(-1)
        flat.append(jnp.pad(v, (0, (-v.shape[0]) % LANES)))
    total = sum(f.shape[0] for f in flat) // LANES
    pad_rows = (-total) % row_align
    if pad_rows:
        flat.append(jnp.zeros((pad_rows * LANES,), dtype))
    return jnp.concatenate(flat).reshape(-1, LANES)


def _unpack(buf, shapes, rows_align=1):
    lead = buf.shape[:-2]
    flat = buf.reshape(lead + (-1,))
    out, off = [], 0
    for shape in shapes:
        size = 1
        for dim in shape:
            size *= dim
        out.append(flat[..., off:off + size].reshape(lead + tuple(shape)))
        off += size + (-size) % (LANES * rows_align)
    return out


def _from_chips(parts, axis):
    return jnp.concatenate([parts[j] for j in range(N_CHIPS)], axis=axis)


def kernel(x, ln_g, ln_b, attn_w_in, attn_b_f, attn_w_out, rnn_w_in, rnn_conv_w, rnn_conv_b, rnn_w_a, rnn_b_a, rnn_w_i, rnn_b_i, rnn_lambda, rnn_w_out, loss_target, m_ln_g, m_ln_b, m_attn_w_in, m_attn_b_f, m_attn_w_out, m_rnn_w_in, m_rnn_conv_w, m_rnn_conv_b, m_rnn_w_a, m_rnn_b_a, m_rnn_w_i, m_rnn_b_i, m_rnn_lambda, m_rnn_w_out, v_ln_g, v_ln_b, v_attn_w_in, v_attn_b_f, v_attn_w_out, v_rnn_w_in, v_rnn_conv_w, v_rnn_conv_b, v_rnn_w_a, v_rnn_b_a, v_rnn_w_i, v_rnn_b_i, v_rnn_lambda, v_rnn_w_out):
    s_len, d = x.shape[1], x.shape[2]
    xs = x.reshape(s_len, d)
    target = loss_target.reshape(s_len, d)
    heads = attn_b_f.shape[1]
    qkvg = attn_w_in.shape[2] * N_CHIPS - heads

    me = 2 * lax.axis_index("x") + lax.axis_index("y")
    core = lax.axis_index("c").astype(jnp.int32)
    small = [rnn_conv_w, rnn_conv_b, rnn_b_a, rnn_b_i, rnn_lambda]
    a_in, a_out = attn_w_in.astype(BF16), attn_w_out.astype(BF16)
    later = [a_in[1], a_out] + [w.astype(BF16) for w in (rnn_w_in, rnn_w_a, rnn_w_i, rnn_w_out)]
    got_in, got_small = _gather_chips([a_in[0], _pack(small, F32, 16)])
    conv_w, conv_b, b_a, b_i, lam = [
        _from_chips(p, ax) for p, ax in zip(_unpack(got_small, [w.shape for w in small]), (2, 1, 1, 1, 1))]

    def attn_in_weights(w_in):
        shard = w_in.shape[2]

        def columns(first, last):
            return [w_in[j][:, max(first - j * shard, 0):min(last - j * shard, shard)]
                    for j in range(N_CHIPS) if first < (j + 1) * shard and last > j * shard]

        return jnp.concatenate([t * (HEAD_DIM ** -0.5) for t in columns(0, d)] + columns(d, qkvg + heads)
                               + [jnp.zeros((d, 128 - heads), BF16)], axis=1)

    w_cat = [attn_in_weights(got_in), None]
    b_f = jnp.pad(attn_b_f, ((0, 0), (0, 128 - heads)))

    saved = []
    cur = xs
    for layer in range(DEPTH):
        idx = layer // 2
        g_l, b_l = ln_g[layer:layer + 1], ln_b[layer:layer + 1]
        goal = target if layer == DEPTH - 1 else None
        if layer % 2 == 0:
            q, k, v, gate, fl = _proj(cur, w_cat[idx], [(0, d, BF16), (d, d, BF16), (2 * d, d, BF16),
                                                         (3 * d, d, F32), (4 * d, 128, F32)], "attn_proj")
            cum, sneg = _fcum(fl, b_f[idx:idx + 1])
            o, lse, *rest = _flash_fwd(q, k, v, cum, later if layer == 0 else ())
            if layer == 0:
                w_cat[1] = attn_in_weights(rest[0])
                w_out_a = jnp.moveaxis(rest[1], 0, 1).reshape(2, d, d)
                w_in_r = jnp.moveaxis(rest[2], 0, 2).reshape(2, d, 2 * d)
                w_a = jnp.transpose(rest[3], (1, 2, 0, 3, 4)).reshape(2, -1, RNN_BLOCK, RNN_BLOCK)
                w_i = jnp.transpose(rest[4], (1, 2, 0, 3, 4)).reshape(2, -1, RNN_BLOCK, RNN_BLOCK)
                w_out_r = jnp.moveaxis(rest[5], 0, 1).reshape(2, d, d)
            nxt, xh, rs, yg, *sq = _out_ln(o, gate, cur, w_out_a[idx], g_l, b_l, "out_ln", goal)
            saved.append(dict(x=cur, q=q, k=k, v=v, gate=gate, cum=cum, sneg=sneg, a=o, lse=lse, xh=xh, rs=rs, yg=yg))
        else:
            u, gate = _proj(cur, w_in_r[idx], [(0, d, F32), (d, d, F32)], "rnn_proj")
            vecs = [t[idx:idx + 1] for t in (conv_b, b_a, b_i, lam)]
            hseq, uc, r, ig, a = _rnn_fwd(u, conv_w[idx], vecs[0], w_a[idx], vecs[1], w_i[idx], vecs[2], vecs[3])
            nxt, xh, rs, yg, *sq = _out_ln(hseq, gate, cur, w_out_r[idx], g_l, b_l, "out_ln", goal)
            saved.append(dict(x=cur, u=u, gate=gate, uc=uc, r=r, ig=ig, av=a, a=hseq, xh=xh, rs=rs, yg=yg, lam=vecs[3]))
        cur = nxt

    dout = cur
    loss_here = 0.5 * jnp.sum(sq[0]) / d

    g_ln_g, g_ln_b = [None] * DEPTH, [None] * DEPTH
    g_attn = [None] * 2
    g_rnn = [None] * 2
    slots = None
    core1 = core.reshape(1)
    late_half = (SLOT_ROWS - LATE_ROWS) // 2

    def by_chip(t, axis):
        shape = t.shape[:axis] + (N_CHIPS, t.shape[axis] // N_CHIPS) + t.shape[axis + 1:]
        flat = jnp.moveaxis(t.reshape(shape), axis, 0).reshape(N_CHIPS, -1)
        return jnp.pad(flat, ((0, 0), (0, (-flat.shape[1]) % (8 * LANES)))).reshape(N_CHIPS, -1, LANES)

    def both(key):
        return jnp.stack([it[key] for it in g_rnn])

    for layer in reversed(range(DEPTH)):
        idx = layer // 2
        sv = saved[layer]
        swap = None
        if layer == 0:
            gates = jnp.concatenate([by_chip(both("w_a"), 2), by_chip(both("w_i"), 2)], axis=1)
            slots = lax.dynamic_update_slice(slots, gates, (0, ROWS_GATES, 0))
            swap = (slots, (LATE_ROWS, late_half))
        w_out = w_out_a[idx] if layer % 2 == 0 else w_out_r[idx]
        dz, da, dgate, dg, db, *theirs_late = _ln_bwd(dout, sv["xh"], sv["rs"], ln_g[layer:layer + 1], w_out,
                                                      sv["gate"], sv["a"], "ln_bwd", swap)
        if layer == 0:
            pair_late = _pair_sum(core1, slots, theirs_late[0], LATE_ROWS, BF16)
        g_ln_g[layer], g_ln_b[layer] = dg, db
        slots = _dw_out(slots, sv["yg"], dz, (ROWS_ATTN_OUT if layer % 2 == 0 else ROWS_RNN_OUT)[idx])
        if layer % 2 == 0:
            dq, dk, dv, dcum, *arrived = _flash_bwd(sv["q"], sv["k"], sv["v"], sv["a"], da, sv["lse"], sv["cum"],
                                                    [pair_late] if layer == 0 else ())
            dlogit, dbf = _fbwd(dcum, sv["sneg"])
            pieces = [dq, dk, dv, dgate, dlogit]
            if layer > 0:
                dout, = _mm_nt(dz, [(p, j * d) for j, p in enumerate(pieces)], w_cat[idx], "attn_dx")
            slots, d_w_f = _dw_attn_in(slots, sv["x"], pieces[:4], dlogit, idx)
            g_attn[idx] = dict(w_f=d_w_f[:, :heads], b_f=dbf[:, 0])
        else:
            duc, d_wa, d_wi, d_ba, d_bi, d_lam = _rnn_bwd(da, sv["av"], sv["a"], sv["r"], sv["ig"], sv["uc"], sv["lam"],
                                                          w_a[idx], w_i[idx])
            du, d_cw, d_cb = _conv_bwd(duc, sv["u"], conv_w[idx])
            dout, = _mm_nt(dz, [(du, 0), (dgate, d)], w_in_r[idx], "rnn_dx")
            slots = _dw_rnn_in(slots, sv["x"], (du, dgate), idx)
            g_rnn[idx] = dict(conv_w=d_cw, conv_b=d_cb[0], w_a=d_wa, b_a=d_ba[0], w_i=d_wi, b_i=d_bi[0], lam=d_lam[0])

    def everywhere(t):
        flat = t.reshape(-1)
        flat = jnp.pad(flat, (0, (-flat.shape[0]) % (8 * LANES))).reshape(-1, LANES)
        return jnp.broadcast_to(flat[None], (N_CHIPS,) + flat.shape)

    in_rows = jnp.stack([slots[1:, r:r + d, :heads] for r in ROWS_ATTN_IN], axis=1)
    edges = jnp.concatenate([in_rows, jnp.stack([it["w_f"] for it in g_attn])[None]])
    rows = [everywhere(edges), by_chip(both("conv_w"), 2),
            by_chip(both("conv_b"), 1), by_chip(both("b_a"), 1), by_chip(both("b_i"), 1), by_chip(both("lam"), 1),
            everywhere(jnp.concatenate(g_ln_g)), everywhere(jnp.concatenate(g_ln_b)),
            everywhere(jnp.stack([it["b_f"] for it in g_attn])), everywhere(loss_here)]
    used = sum(r.shape[1] for r in rows)
    small = jnp.concatenate(rows + [jnp.zeros((N_CHIPS, (-used) % 32, LANES), F32)], axis=1)

    spans = [(0, LATE_ROWS // 2), (0, small.shape[1] // 2)]
    theirs = _swap_halves([slots, small], spans)
    pair = [_pair_sum(core1, slots, theirs[0], 0, BF16), _pair_sum(core1, small, theirs[1], 0, F32)]
    dout, *arrived_last = _mm_nt(dz, [(p, j * d) for j, p in enumerate(pieces)], w_cat[0], "attn_dx", pair)
    grad_x = dout.reshape(x.shape)
    summed = []
    for mine_all, got in zip([pair_late] + pair, list(arrived) + arrived_last):
        own = lax.dynamic_index_in_dim(mine_all, me, 0, keepdims=False)
        summed.append(_sum_chips(lax.dynamic_update_index_in_dim(got, own, me, 0)))
    late, early, small_sum = [
        jnp.concatenate([jnp.where(core == 0, mine, other), jnp.where(core == 0, other, mine)])
        for mine, other in zip(summed, _give_sibling(summed))]

    def rows_at(first, n):
        return early[first:first + n] if first < LATE_ROWS else late[first - LATE_ROWS:first - LATE_ROWS + n]

    g_in_group = jnp.stack([rows_at(r, d) for r in ROWS_ATTN_IN])
    g_w_out_a = jnp.stack([rows_at(r, d // N_CHIPS) for r in ROWS_ATTN_OUT])
    g_w_out_r = jnp.stack([rows_at(r, d // N_CHIPS) for r in ROWS_RNN_OUT])
    folded = jnp.stack([rows_at(r, d // 2) for r in ROWS_RNN_IN])
    g_w_in_r = jnp.concatenate([folded[:, :, :d // 2], folded[:, :, d // 2:]], axis=1)
    gate_rows = rnn_w_a.size // LANES
    g_w_a = rows_at(ROWS_GATES, gate_rows).reshape(rnn_w_a.shape)
    g_w_i = rows_at(ROWS_GATES + gate_rows, gate_rows).reshape(rnn_w_i.shape)
    (g_edges, g_conv_w, g_conv_b, g_b_a, g_b_i, g_lam, g_ln_g_sum, g_ln_b_sum, g_b_f,
     loss) = _unpack(small_sum, [
        (N_CHIPS, 2, d, heads), rnn_conv_w.shape, rnn_conv_b.shape, rnn_b_a.shape,
        rnn_b_i.shape, rnn_lambda.shape, ln_g.shape, ln_b.shape, attn_b_f.shape, ()], rows_align=8)
    wide = jnp.concatenate([g_in_group, lax.dynamic_index_in_dim(g_edges, me, 0, keepdims=False)], axis=2)
    g_w_in_a = lax.dynamic_slice(wide, (0, 0, (heads // N_CHIPS) * me), attn_w_in.shape)

    def adam_nd(w, g, m, v, view, rows_per_block):
        outs = _adamw(w.reshape(view), g.reshape(view), m.reshape(view), v.reshape(view), 1, rows_per_block)
        return [t.reshape(w.shape) for t in outs]

    turned = [jnp.transpose(t, (2, 0, 1)) for t in (attn_w_in, g_w_in_a, m_attn_w_in, v_attn_w_in)]
    upd = {
        "attn_w_in": [jnp.transpose(t, (1, 2, 0)) for t in _adamw(*turned, attn_w_in.shape[2] // N_CHIPS, 2)],
        "attn_w_out": adam_nd(attn_w_out, g_w_out_a, m_attn_w_out, v_attn_w_out, attn_w_out.shape, 256),
        "rnn_w_in": adam_nd(rnn_w_in, g_w_in_r, m_rnn_w_in, v_rnn_w_in, rnn_w_in.shape, 512),
        "rnn_w_a": adam_nd(rnn_w_a, g_w_a, m_rnn_w_a, v_rnn_w_a, (1, -1, RNN_BLOCK), 512),
        "rnn_w_i": adam_nd(rnn_w_i, g_w_i, m_rnn_w_i, v_rnn_w_i, (1, -1, RNN_BLOCK), 512),
        "rnn_w_out": adam_nd(rnn_w_out, g_w_out_r, m_rnn_w_out, v_rnn_w_out, rnn_w_out.shape, 256),
    }
    smalls = [rnn_conv_w, rnn_conv_b, rnn_b_a, rnn_b_i, rnn_lambda, ln_g, ln_b, attn_b_f]
    g_small = [g_conv_w, g_conv_b, g_b_a, g_b_i, g_lam, g_ln_g_sum, g_ln_b_sum, g_b_f]
    m_small = [m_rnn_conv_w, m_rnn_conv_b, m_rnn_b_a, m_rnn_b_i, m_rnn_lambda, m_ln_g, m_ln_b, m_attn_b_f]
    v_small = [v_rnn_conv_w, v_rnn_conv_b, v_rnn_b_a, v_rnn_b_i, v_rnn_lambda, v_ln_g, v_ln_b, v_attn_b_f]
    packs = [_pack(t, F32, 16)[None] for t in (smalls, g_small, m_small, v_small)]
    small_out = [_unpack(t[0], [w.shape for w in smalls]) for t in _adamw(*packs, 1, 16)]
    for k, name in enumerate(("rnn_conv_w", "rnn_conv_b", "rnn_b_a", "rnn_b_i", "rnn_lambda", "ln_g", "ln_b",
                              "attn_b_f")):
        upd[name] = [small_out[0][k], small_out[1][k], small_out[2][k]]
    grad = {"attn_w_in": g_w_in_a, "attn_w_out": g_w_out_a, "rnn_w_in": g_w_in_r, "rnn_w_a": g_w_a, "rnn_w_i": g_w_i,
            "rnn_w_out": g_w_out_r, "rnn_conv_w": g_conv_w, "rnn_conv_b": g_conv_b, "rnn_b_a": g_b_a,
            "rnn_b_i": g_b_i, "rnn_lambda": g_lam, "ln_g": g_ln_g_sum, "ln_b": g_ln_b_sum, "attn_b_f": g_b_f}
    names = ("ln_g", "ln_b", "attn_w_in", "attn_b_f", "attn_w_out", "rnn_w_in", "rnn_conv_w", "rnn_conv_b",
             "rnn_w_a", "rnn_b_a", "rnn_w_i", "rnn_b_i", "rnn_lambda", "rnn_w_out")
    return (loss, grad_x, *[grad[n] for n in names], *[upd[n][0] for n in names], *[upd[n][1] for n in names],
            *[upd[n][2] for n in names])
```
